```python
import jax, jax.numpy as jnp
from jax import lax
import numpy as np

D_MODEL = 1024
BATCH = 32
SEQ = 2048
DEPTH = 1

N_META = 16
D_CONV = D_MODEL
CONV_WIDTH = 3
GLA_HEADS = 4
DK = D_MODEL // 2
DV = D_MODEL
HEAD_K = DK // GLA_HEADS
HEAD_V = DV // GLA_HEADS
GATE_RANK = 16
GATE_NORMALIZER = 16.0
CHUNK = 64
EPS = 1e-6
IN_SPLITS = (D_CONV, D_CONV, D_CONV, D_CONV, DK, DK, DV, DV, GATE_RANK, GATE_RANK, D_MODEL, D_MODEL)
N_IN = sum(IN_SPLITS)

kernel_name = "hybrid_gated_shortconv_bigla_block"


def rms_norm(x, g):
    xf = x.astype(jnp.float32)
    y = xf * lax.rsqrt(jnp.mean(xf * xf, axis=-1, keepdims=True) + EPS)
    return (y * g.astype(jnp.float32)).astype(x.dtype)


def short_conv_centred(u, w):
    half = CONV_WIDTH // 2
    L = u.shape[1]
    up = jnp.pad(u, ((0, 0), (half, half), (0, 0)))
    return sum(up[:, i:i + L] * w[i] for i in range(CONV_WIDTH))


def to_chunks(t, pad_front, pad_back, n_heads, head_dim):
    t = jnp.pad(t, ((0, 0), (pad_front, pad_back), (0, 0)))
    bn, lp, _ = t.shape
    t = t.reshape(bn, lp // CHUNK, CHUNK, n_heads, head_dim)
    return t.transpose(0, 3, 1, 2, 4)


def gla_chunked(q, k, v, g, strict):
    bn, nh, _, c, dk = q.shape
    dv = v.shape[-1]
    b = jnp.cumsum(g.astype(jnp.float32), axis=3)
    q_in = q * jnp.exp(b)
    k_in = k * jnp.exp(-b)
    scores = jnp.einsum('bhncd,bhnjd->bhncj', q_in, k_in)
    mask = jnp.tril(jnp.ones((c, c), dtype=bool), k=-1 if strict else 0)
    scores = jnp.where(mask, scores, 0.0)
    o_intra = jnp.einsum('bhncj,bhnje->bhnce', scores, v)
    b_last = b[..., -1:, :]
    k_dec = k * jnp.exp(b_last - b)
    decay = jnp.exp(b_last[..., 0, :])

    def step(state, xs):
        q_n, k_n, v_n, d_n = xs
        o_n = jnp.einsum('bhcd,bhde->bhce', q_n, state)
        state = state * d_n[..., None] + jnp.einsum('bhcd,bhce->bhde', k_n, v_n)
        return state, o_n

    xs = tuple(jnp.moveaxis(t, 2, 0) for t in (q_in, k_dec, v, decay))
    s0 = jnp.zeros((bn, nh, dk, dv), jnp.float32)
    _, o_inter = lax.scan(step, s0, xs)
    return o_intra + jnp.moveaxis(o_inter, 0, 2)


def hybrid_layer(h, g_pre, w_in, conv_w, w_gate_f, b_gate_f, w_gate_b, b_gate_b,
                 gla_g, w_out_c, w_out_g, w_out, g_post):
    bn, L, _ = h.shape
    pad_front = (-N_META) % CHUNK
    pad_back = (-(L - N_META)) % CHUNK
    u = rms_norm(h, g_pre)
    proj = jnp.einsum('bld,dn->bln', u, w_in)
    split_idx = np.cumsum(IN_SPLITS)[:-1].tolist()
    (c_b, c_c, c_x, c_z, q, k, v, r, lr_f, lr_b, m_a, m_b) = jnp.split(proj, split_idx, axis=-1)

    y_conv = c_b * short_conv_centred(c_c * c_x, conv_w) * jax.nn.silu(c_z)
    p_conv = jnp.einsum('blc,cd->bld', y_conv, w_out_c)

    g_f = jax.nn.log_sigmoid((lr_f @ w_gate_f + b_gate_f).astype(jnp.float32)) / GATE_NORMALIZER
    g_b = jax.nn.log_sigmoid((lr_b @ w_gate_b + b_gate_b).astype(jnp.float32)) / GATE_NORMALIZER
    qc = to_chunks(q * (HEAD_K ** -0.5), pad_front, pad_back, GLA_HEADS, HEAD_K)
    kc = to_chunks(k, pad_front, pad_back, GLA_HEADS, HEAD_K)
    vc = to_chunks(v, pad_front, pad_back, GLA_HEADS, HEAD_V)
    gfc = to_chunks(g_f, pad_front, pad_back, GLA_HEADS, HEAD_K)
    gbc = to_chunks(g_b, pad_front, pad_back, GLA_HEADS, HEAD_K)
    rev = lambda t: jnp.flip(t, axis=(2, 3))
    o_f = gla_chunked(qc, kc, vc, gfc, strict=False)
    o_b = rev(gla_chunked(rev(qc), rev(kc), rev(vc), rev(gbc), strict=True))
    o = (o_f + o_b).transpose(0, 2, 3, 1, 4).reshape(bn, -1, GLA_HEADS, HEAD_V)
    o = o[:, pad_front:pad_front + L]
    o = rms_norm(o, gla_g).reshape(bn, L, DV).astype(h.dtype)
    y_gla = o * jax.nn.silu(r)
    p_gla = jnp.einsum('blc,cd->bld', y_gla, w_out_g)

    merged = jax.nn.sigmoid(m_a) * p_conv + jax.nn.sigmoid(m_b) * p_gla
    out = jnp.einsum('bld,de->ble', merged, w_out)
    return h + rms_norm(out, g_post)


def _fwd_setup_inputs(seed: int = 0) -> dict:
    key = jax.random.key(seed)
    ks = jax.random.split(key, 16)
    nrm = lambda k, shape, scale: jax.random.normal(k, shape, jnp.float32) * scale
    return {
        "x": nrm(ks[0], (BATCH, SEQ, D_MODEL), 1.0),
        "meta_tokens": nrm(ks[1], (N_META, D_MODEL), 1.0),
        "norm_pre": 1.0 + nrm(ks[2], (DEPTH, D_MODEL), 0.05),
        "w_in": nrm(ks[3], (DEPTH, D_MODEL, N_IN), D_MODEL ** -0.5),
        "conv_w": nrm(ks[4], (DEPTH, CONV_WIDTH, D_CONV), CONV_WIDTH ** -0.5),
        "w_gate_fwd": nrm(ks[5], (DEPTH, GATE_RANK, DK), GATE_RANK ** -0.5),
        "b_gate_fwd": nrm(ks[6], (DEPTH, DK), 0.1),
        "w_gate_bwd": nrm(ks[7], (DEPTH, GATE_RANK, DK), GATE_RANK ** -0.5),
        "b_gate_bwd": nrm(ks[8], (DEPTH, DK), 0.1),
        "gla_norm": 1.0 + nrm(ks[9], (DEPTH, HEAD_V), 0.05),
        "w_out_conv": nrm(ks[10], (DEPTH, D_CONV, D_MODEL), D_CONV ** -0.5),
        "w_out_gla": nrm(ks[11], (DEPTH, DV, D_MODEL), DV ** -0.5),
        "w_merge_out": nrm(ks[12], (DEPTH, D_MODEL, D_MODEL), D_MODEL ** -0.5),
        "norm_post": 1.0 + nrm(ks[13], (DEPTH, D_MODEL), 0.05),
    }


def _fwd_reference(x, meta_tokens, norm_pre, w_in, conv_w, w_gate_fwd, b_gate_fwd, w_gate_bwd,
              b_gate_bwd, gla_norm, w_out_conv, w_out_gla, w_merge_out, norm_post):
    bn = x.shape[0]
    meta = jnp.broadcast_to(meta_tokens[None].astype(x.dtype), (bn, N_META, D_MODEL))
    h = jnp.concatenate([meta, x], axis=1)
    for l in range(DEPTH):
        h = hybrid_layer(h, norm_pre[l], w_in[l], conv_w[l], w_gate_fwd[l], b_gate_fwd[l],
                         w_gate_bwd[l], b_gate_bwd[l], gla_norm[l], w_out_conv[l],
                         w_out_gla[l], w_merge_out[l], norm_post[l])
    return h[:, N_META:]


import jax as _jax
import jax.numpy as _jnp

TWIN_FORMAT = 'train_step'
FWD_PARAMS = ['x', 'meta_tokens', 'norm_pre', 'w_in', 'conv_w', 'w_gate_fwd', 'b_gate_fwd', 'w_gate_bwd', 'b_gate_bwd', 'gla_norm', 'w_out_conv', 'w_out_gla', 'w_merge_out', 'norm_post']
TWIN_WEIGHTS = ['meta_tokens', 'norm_pre', 'w_in', 'conv_w', 'w_gate_fwd', 'b_gate_fwd', 'w_gate_bwd', 'b_gate_bwd', 'gla_norm', 'w_out_conv', 'w_out_gla', 'w_merge_out', 'norm_post']
TWIN_DIFF_INPUT = 'x'
TWIN_INPUTS = ['x', 'meta_tokens', 'norm_pre', 'w_in', 'conv_w', 'w_gate_fwd', 'b_gate_fwd', 'w_gate_bwd', 'b_gate_bwd', 'gla_norm', 'w_out_conv', 'w_out_gla', 'w_merge_out', 'norm_post', 'loss_target', 'm_meta_tokens', 'm_norm_pre', 'm_w_in', 'm_conv_w', 'm_w_gate_fwd', 'm_b_gate_fwd', 'm_w_gate_bwd', 'm_b_gate_bwd', 'm_gla_norm', 'm_w_out_conv', 'm_w_out_gla', 'm_w_merge_out', 'm_norm_post', 'v_meta_tokens', 'v_norm_pre', 'v_w_in', 'v_conv_w', 'v_w_gate_fwd', 'v_b_gate_fwd', 'v_w_gate_bwd', 'v_b_gate_bwd', 'v_gla_norm', 'v_w_out_conv', 'v_w_out_gla', 'v_w_merge_out', 'v_norm_post']
TWIN_OUTPUTS = ['loss', 'grad_x', 'grad_meta_tokens', 'grad_norm_pre', 'grad_w_in', 'grad_conv_w', 'grad_w_gate_fwd', 'grad_b_gate_fwd', 'grad_w_gate_bwd', 'grad_b_gate_bwd', 'grad_gla_norm', 'grad_w_out_conv', 'grad_w_out_gla', 'grad_w_merge_out', 'grad_norm_post', 'delta_meta_tokens', 'delta_norm_pre', 'delta_w_in', 'delta_conv_w', 'delta_w_gate_fwd', 'delta_b_gate_fwd', 'delta_w_gate_bwd', 'delta_b_gate_bwd', 'delta_gla_norm', 'delta_w_out_conv', 'delta_w_out_gla', 'delta_w_merge_out', 'delta_norm_post', 'new_m_meta_tokens', 'new_m_norm_pre', 'new_m_w_in', 'new_m_conv_w', 'new_m_w_gate_fwd', 'new_m_b_gate_fwd', 'new_m_w_gate_bwd', 'new_m_b_gate_bwd', 'new_m_gla_norm', 'new_m_w_out_conv', 'new_m_w_out_gla', 'new_m_w_merge_out', 'new_m_norm_post', 'new_v_meta_tokens', 'new_v_norm_pre', 'new_v_w_in', 'new_v_conv_w', 'new_v_w_gate_fwd', 'new_v_b_gate_fwd', 'new_v_w_gate_bwd', 'new_v_b_gate_bwd', 'new_v_gla_norm', 'new_v_w_out_conv', 'new_v_w_out_gla', 'new_v_w_merge_out', 'new_v_norm_post']
TWIN_LEAF_KINDS = {'loss': 'loss', 'grad_x': 'grad_x', 'grad_meta_tokens': 'grad_w', 'grad_norm_pre': 'grad_w', 'grad_w_in': 'grad_w', 'grad_conv_w': 'grad_w', 'grad_w_gate_fwd': 'grad_w', 'grad_b_gate_fwd': 'grad_w', 'grad_w_gate_bwd': 'grad_w', 'grad_b_gate_bwd': 'grad_w', 'grad_gla_norm': 'grad_w', 'grad_w_out_conv': 'grad_w', 'grad_w_out_gla': 'grad_w', 'grad_w_merge_out': 'grad_w', 'grad_norm_post': 'grad_w', 'delta_meta_tokens': 'delta_w', 'delta_norm_pre': 'delta_w', 'delta_w_in': 'delta_w', 'delta_conv_w': 'delta_w', 'delta_w_gate_fwd': 'delta_w', 'delta_b_gate_fwd': 'delta_w', 'delta_w_gate_bwd': 'delta_w', 'delta_b_gate_bwd': 'delta_w', 'delta_gla_norm': 'delta_w', 'delta_w_out_conv': 'delta_w', 'delta_w_out_gla': 'delta_w', 'delta_w_merge_out': 'delta_w', 'delta_norm_post': 'delta_w', 'new_m_meta_tokens': 'new_m', 'new_m_norm_pre': 'new_m', 'new_m_w_in': 'new_m', 'new_m_conv_w': 'new_m', 'new_m_w_gate_fwd': 'new_m', 'new_m_b_gate_fwd': 'new_m', 'new_m_w_gate_bwd': 'new_m', 'new_m_b_gate_bwd': 'new_m', 'new_m_gla_norm': 'new_m', 'new_m_w_out_conv': 'new_m', 'new_m_w_out_gla': 'new_m', 'new_m_w_merge_out': 'new_m', 'new_m_norm_post': 'new_m', 'new_v_meta_tokens': 'new_v', 'new_v_norm_pre': 'new_v', 'new_v_w_in': 'new_v', 'new_v_conv_w': 'new_v', 'new_v_w_gate_fwd': 'new_v', 'new_v_b_gate_fwd': 'new_v', 'new_v_w_gate_bwd': 'new_v', 'new_v_b_gate_bwd': 'new_v', 'new_v_gla_norm': 'new_v', 'new_v_w_out_conv': 'new_v', 'new_v_w_out_gla': 'new_v', 'new_v_w_merge_out': 'new_v', 'new_v_norm_post': 'new_v'}


def _forward(args):
    return _fwd_reference(*[args[k] for k in FWD_PARAMS])


def _output_shape():
    out = _jax.eval_shape(lambda: _forward(_fwd_setup_inputs(0)))
    return out.shape, out.dtype

N_MICROBATCH = 1
ADAM_LR = 0.001
ADAM_B1 = 0.9
ADAM_B2 = 0.999
ADAM_EPS = 1e-08
ADAM_WD = 0.01
ADAM_STEP = 10
PER_EXAMPLE_BATCH_AXIS = {'x': 0, 'loss_target': 0}
SHARED_INPUTS = []
_WEIGHT_DTYPES = {'meta_tokens': _jnp.float32, 'norm_pre': _jnp.float32, 'w_in': _jnp.float32, 'conv_w': _jnp.float32, 'w_gate_fwd': _jnp.float32, 'b_gate_fwd': _jnp.float32, 'w_gate_bwd': _jnp.float32, 'b_gate_bwd': _jnp.float32, 'gla_norm': _jnp.float32, 'w_out_conv': _jnp.float32, 'w_out_gla': _jnp.float32, 'w_merge_out': _jnp.float32, 'norm_post': _jnp.float32}
MOMENT_SCALE = {'meta_tokens': 1.213864e-02, 'norm_pre': 7.650143e-01, 'w_in': 2.472889e-01, 'conv_w': 2.582877e-01, 'w_gate_fwd': 2.956991e-02, 'b_gate_fwd': 1.410115e-01, 'w_gate_bwd': 3.178755e-02, 'b_gate_bwd': 1.299412e-01, 'gla_norm': 5.962552e-01, 'w_out_conv': 2.495692e-01, 'w_out_gla': 2.619280e-01, 'w_merge_out': 3.737949e-01, 'norm_post': 6.387640e+01}


def _to_microbatches(a, axis):
    t = _jnp.moveaxis(a, axis, 0)
    t = t.reshape((N_MICROBATCH, t.shape[0] // N_MICROBATCH) + t.shape[1:])
    return _jnp.moveaxis(t, 1, axis + 1)


def setup_inputs(seed: int = 0) -> dict:
    inp = _fwd_setup_inputs(seed)
    key = _jax.random.fold_in(_jax.random.key(seed), 7919)
    shape, _ = _output_shape()
    out = dict(inp)
    out["loss_target"] = _jax.random.normal(_jax.random.fold_in(key, 0), shape, _jnp.float32)
    for i, name in enumerate(TWIN_WEIGHTS):
        w = inp[name].astype(_jnp.float32)
        if MOMENT_SCALE is None:
            s = _jnp.sqrt(_jnp.mean(_jnp.square(w)) + 1e-30)
        else:
            s = MOMENT_SCALE[name]
        km, kv = _jax.random.split(_jax.random.fold_in(key, i + 1))
        out[name] = w
        out["m_" + name] = s * _jax.random.normal(km, w.shape, _jnp.float32)
        out["v_" + name] = (s * s) * _jax.random.uniform(kv, w.shape, _jnp.float32, 0.5, 1.5)
    if N_MICROBATCH > 1:
        for name, axis in PER_EXAMPLE_BATCH_AXIS.items():
            out[name] = _to_microbatches(out[name], axis)
    return {'x': out['x'], 'meta_tokens': out['meta_tokens'], 'norm_pre': out['norm_pre'], 'w_in': out['w_in'], 'conv_w': out['conv_w'], 'w_gate_fwd': out['w_gate_fwd'], 'b_gate_fwd': out['b_gate_fwd'], 'w_gate_bwd': out['w_gate_bwd'], 'b_gate_bwd': out['b_gate_bwd'], 'gla_norm': out['gla_norm'], 'w_out_conv': out['w_out_conv'], 'w_out_gla': out['w_out_gla'], 'w_merge_out': out['w_merge_out'], 'norm_post': out['norm_post'], 'loss_target': out['loss_target'], 'm_meta_tokens': out['m_meta_tokens'], 'm_norm_pre': out['m_norm_pre'], 'm_w_in': out['m_w_in'], 'm_conv_w': out['m_conv_w'], 'm_w_gate_fwd': out['m_w_gate_fwd'], 'm_b_gate_fwd': out['m_b_gate_fwd'], 'm_w_gate_bwd': out['m_w_gate_bwd'], 'm_b_gate_bwd': out['m_b_gate_bwd'], 'm_gla_norm': out['m_gla_norm'], 'm_w_out_conv': out['m_w_out_conv'], 'm_w_out_gla': out['m_w_out_gla'], 'm_w_merge_out': out['m_w_merge_out'], 'm_norm_post': out['m_norm_post'], 'v_meta_tokens': out['v_meta_tokens'], 'v_norm_pre': out['v_norm_pre'], 'v_w_in': out['v_w_in'], 'v_conv_w': out['v_conv_w'], 'v_w_gate_fwd': out['v_w_gate_fwd'], 'v_b_gate_fwd': out['v_b_gate_fwd'], 'v_w_gate_bwd': out['v_w_gate_bwd'], 'v_b_gate_bwd': out['v_b_gate_bwd'], 'v_gla_norm': out['v_gla_norm'], 'v_w_out_conv': out['v_w_out_conv'], 'v_w_out_gla': out['v_w_out_gla'], 'v_w_merge_out': out['v_w_merge_out'], 'v_norm_post': out['v_norm_post']}


def _loss(weights, diff, rest, loss_target):
    with _jax.named_scope("forward"):
        args = {**rest, TWIN_DIFF_INPUT: diff, **{k: w.astype(_WEIGHT_DTYPES[k]) for k, w in weights.items()}}
        y = _forward(args)
    with _jax.named_scope("loss_head"):
        err = _jnp.square(y.astype(_jnp.float32) - loss_target)
        return 0.5 * _jnp.sum(_jnp.mean(err, axis=-1)) if err.ndim else 0.5 * err


def _adamw(w, g, m, v):
    m = ADAM_B1 * m + (1.0 - ADAM_B1) * g
    v = ADAM_B2 * v + (1.0 - ADAM_B2) * _jnp.square(g)
    m_hat = m / (1.0 - ADAM_B1 ** ADAM_STEP)
    v_hat = v / (1.0 - ADAM_B2 ** ADAM_STEP)
    delta = -ADAM_LR * (m_hat / (_jnp.sqrt(v_hat) + ADAM_EPS) + ADAM_WD * w)
    return delta, m, v


def reference(x, meta_tokens, norm_pre, w_in, conv_w, w_gate_fwd, b_gate_fwd, w_gate_bwd, b_gate_bwd, gla_norm, w_out_conv, w_out_gla, w_merge_out, norm_post, loss_target, m_meta_tokens, m_norm_pre, m_w_in, m_conv_w, m_w_gate_fwd, m_b_gate_fwd, m_w_gate_bwd, m_b_gate_bwd, m_gla_norm, m_w_out_conv, m_w_out_gla, m_w_merge_out, m_norm_post, v_meta_tokens, v_norm_pre, v_w_in, v_conv_w, v_w_gate_fwd, v_b_gate_fwd, v_w_gate_bwd, v_b_gate_bwd, v_gla_norm, v_w_out_conv, v_w_out_gla, v_w_merge_out, v_norm_post):
    given = dict(x=x, meta_tokens=meta_tokens, norm_pre=norm_pre, w_in=w_in, conv_w=conv_w, w_gate_fwd=w_gate_fwd, b_gate_fwd=b_gate_fwd, w_gate_bwd=w_gate_bwd, b_gate_bwd=b_gate_bwd, gla_norm=gla_norm, w_out_conv=w_out_conv, w_out_gla=w_out_gla, w_merge_out=w_merge_out, norm_post=norm_post, loss_target=loss_target, m_meta_tokens=m_meta_tokens, m_norm_pre=m_norm_pre, m_w_in=m_w_in, m_conv_w=m_conv_w, m_w_gate_fwd=m_w_gate_fwd, m_b_gate_fwd=m_b_gate_fwd, m_w_gate_bwd=m_w_gate_bwd, m_b_gate_bwd=m_b_gate_bwd, m_gla_norm=m_gla_norm, m_w_out_conv=m_w_out_conv, m_w_out_gla=m_w_out_gla, m_w_merge_out=m_w_merge_out, m_norm_post=m_norm_post, v_meta_tokens=v_meta_tokens, v_norm_pre=v_norm_pre, v_w_in=v_w_in, v_conv_w=v_conv_w, v_w_gate_fwd=v_w_gate_fwd, v_b_gate_fwd=v_b_gate_fwd, v_w_gate_bwd=v_w_gate_bwd, v_b_gate_bwd=v_b_gate_bwd, v_gla_norm=v_gla_norm, v_w_out_conv=v_w_out_conv, v_w_out_gla=v_w_out_gla, v_w_merge_out=v_w_merge_out, v_norm_post=v_norm_post)
    weights = {n: given[n] for n in TWIN_WEIGHTS}
    shared = {n: given[n] for n in SHARED_INPUTS}
    per_example = {n: given[n] for n in ['x']}
    grad_fn = _jax.value_and_grad(_loss, argnums=(0, 1))

    def one_microbatch(ex, loss_target):
        ex = dict(ex)
        diff = ex.pop(TWIN_DIFF_INPUT)
        return grad_fn(weights, diff, {**shared, **ex}, loss_target)

    if N_MICROBATCH == 1:
        loss, (grad_w, grad_x) = one_microbatch(per_example, given["loss_target"])
    else:
        def body(carry, xs):
            loss_sum, grad_sum = carry
            l_k, (gw_k, gx_k) = one_microbatch(xs[0], xs[1])
            with _jax.named_scope("update"):
                return (loss_sum + l_k, _jax.tree.map(_jnp.add, grad_sum, gw_k)), gx_k

        init = (_jnp.zeros((), _jnp.float32), _jax.tree.map(_jnp.zeros_like, weights))
        (loss, grad_w), grad_x = _jax.lax.scan(body, init, (per_example, given["loss_target"]))
    with _jax.named_scope("update"):
        delta_w, new_m, new_v = {}, {}, {}
        for n in TWIN_WEIGHTS:
            delta_w[n], new_m[n], new_v[n] = _adamw(weights[n], grad_w[n], given["m_" + n], given["v_" + n])
    return (loss, grad_x, *[grad_w[n] for n in TWIN_WEIGHTS], *[delta_w[n] for n in TWIN_WEIGHTS],
            *[new_m[n] for n in TWIN_WEIGHTS], *[new_v[n] for n in TWIN_WEIGHTS])
```

```python
import functools

import jax
import jax.numpy as jnp
from jax import lax
from jax.experimental import pallas as pl
from jax.experimental.pallas import tpu as pltpu

F32 = jnp.float32
BF16 = jnp.bfloat16
MESH = pl.DeviceIdType.MESH

D = 1024
N_META = 16
CHUNK = 64
PAD_FRONT = CHUNK - N_META
N_HEADS = 4
HEAD_K = 128
HEAD_V = 256
RANK = 16
EPS = 1e-6
GATE_NORM = 16.0
N_IN = 9248
SHARD_IN = N_IN // 4
LANES = 128
N_CONV_TILES = 8
W_CONV = 4096
W_GLA = 2048
W_TAIL = 3072
N_MAIN = W_CONV + W_GLA + W_TAIL
MIB = 1024 * 1024

ADAM_LR = 0.001
ADAM_B1 = 0.9
ADAM_B2 = 0.999
ADAM_EPS = 1e-08
ADAM_WD = 0.01
ADAM_STEP = 10


def _params(sem=None, vmem_mib=None):
    return pltpu.CompilerParams(
        dimension_semantics=sem,
        vmem_limit_bytes=None if vmem_mib is None else vmem_mib * MIB)


def _pick_tile(n, target, mult):
    best = None
    for t in range(mult, min(n, target) + 1, mult):
        if n % t == 0:
            best = t
    return n if best is None else best


def _sigmoid(v):
    return 1.0 / (1.0 + jnp.exp(-v))


def _log_sigmoid(v):
    return jnp.minimum(v, 0.0) - jnp.log(1.0 + jnp.exp(-jnp.abs(v)))


def _dot(a, b):
    return jnp.dot(a, b, preferred_element_type=F32)


def _dot_nt(a, b):
    return lax.dot_general(a, b, (((1,), (1,)), ((), ())), preferred_element_type=F32)


def _dot_tn(a, b):
    return lax.dot_general(a, b, (((0,), (0,)), ((), ())), preferred_element_type=F32)


def _tri_dot(tri, v):
    hi = v.astype(BF16)
    r1 = v - hi.astype(F32)
    mid = r1.astype(BF16)
    lo = (r1 - mid.astype(F32)).astype(BF16)
    return _dot(tri, hi) + _dot(tri, mid) + _dot(tri, lo)


PLANE_FLIPS = ((1, 0, 0), (0, 1, 0), (1, 1, 0))
ALL_FLIPS = tuple((m >> 2 & 1, m >> 1 & 1, m & 1) for m in range(1, 8))
SIBLING_FLIPS = ((0, 0, 1),)


def _exchange(name, arrs, flips, slot_weights, mode):
    n = len(arrs)
    n_slots = 1
    for w in slot_weights:
        n_slots += w
    if mode == "gather":
        out_shape = [jax.ShapeDtypeStruct((n_slots,) + a.shape, a.dtype) for a in arrs]
    else:
        out_shape = [jax.ShapeDtypeStruct(a.shape, a.dtype) for a in arrs]

    def body(*refs):
        ins, outs = refs[:n], refs[n:2 * n]
        send_sems, recv_sems, local_sems = refs[2 * n:]
        pos = (lax.axis_index("x"), lax.axis_index("y"), lax.axis_index("c"))

        def slot_of(p):
            return p[0] * slot_weights[0] + p[1] * slot_weights[1] + p[2] * slot_weights[2]

        peers = [tuple(1 - pos[a] if f[a] else pos[a] for a in range(3)) for f in flips]
        me = slot_of(pos)
        local = []
        sends = []
        for i in range(n):
            if mode != "swap":
                src = ins[i] if mode == "gather" else ins[i].at[me]
                cp = pltpu.make_async_copy(src, outs[i].at[me], local_sems.at[i])
                cp.start()
                local.append(cp)
            for k, peer in enumerate(peers):
                if mode == "gather":
                    src, dst = ins[i], outs[i].at[me]
                elif mode == "scatter":
                    src, dst = ins[i].at[slot_of(peer)], outs[i].at[me]
                else:
                    src, dst = ins[i], outs[i]
                cp = pltpu.make_async_remote_copy(
                    src_ref=src, dst_ref=dst, send_sem=send_sems.at[i, k], recv_sem=recv_sems.at[i, k],
                    device_id=peer, device_id_type=MESH)
                cp.start()
                sends.append(cp)
        for i in range(n):
            for k, peer in enumerate(peers):
                if mode == "gather":
                    src, dst = ins[i], outs[i].at[slot_of(peer)]
                elif mode == "scatter":
                    src, dst = ins[i].at[me], outs[i].at[slot_of(peer)]
                else:
                    src, dst = ins[i], outs[i]
                arrival = pltpu.make_async_remote_copy(
                    src_ref=src, dst_ref=dst, send_sem=send_sems.at[i, k], recv_sem=recv_sems.at[i, k],
                    device_id=peer, device_id_type=MESH)
                arrival.wait_recv()
        for cp in sends:
            cp.wait_send()
        for cp in local:
            cp.wait()

    hbm = pl.BlockSpec(memory_space=pl.ANY)
    outs = pl.pallas_call(
        body, name=name, out_shape=out_shape,
        in_specs=[hbm] * n, out_specs=[hbm] * n,
        scratch_shapes=[pltpu.SemaphoreType.DMA((n, len(flips))),
                        pltpu.SemaphoreType.DMA((n, len(flips))),
                        pltpu.SemaphoreType.DMA((n,))],
        compiler_params=pltpu.CompilerParams(has_side_effects=True),
    )(*arrs)
    return list(outs)


def _cast_bf16(a, name):
    rows, cols = a.shape
    rt = _pick_tile(rows, 256, 16)

    def body(a_ref, o_ref):
        o_ref[...] = a_ref[...].astype(BF16)

    return pl.pallas_call(
        body, name=name, grid=(rows // rt,),
        in_specs=[pl.BlockSpec((rt, cols), lambda i: (i, 0))],
        out_specs=pl.BlockSpec((rt, cols), lambda i: (i, 0)),
        out_shape=jax.ShapeDtypeStruct(a.shape, BF16),
        compiler_params=_params(("parallel",)),
    )(a)


def _sum_slots(buf, name):
    n_slots, rows, cols = buf.shape
    rt = _pick_tile(rows, 128, 16)

    def body(b_ref, o_ref):
        acc = b_ref[0].astype(F32)
        for s in range(1, n_slots):
            acc = acc + b_ref[s].astype(F32)
        o_ref[...] = acc

    return pl.pallas_call(
        body, name=name, grid=(rows // rt,),
        in_specs=[pl.BlockSpec((n_slots, rt, cols), lambda i: (0, i, 0))],
        out_specs=pl.BlockSpec((rt, cols), lambda i: (i, 0)),
        out_shape=jax.ShapeDtypeStruct((rows, cols), F32),
        compiler_params=_params(("parallel",), 40),
    )(buf)


def _adamw(w, grads, m, v, name):
    rows, cols = w.shape
    rt = _pick_tile(rows, 128, 8)
    n_g = len(grads)
    c1 = 1.0 - ADAM_B1 ** ADAM_STEP
    c2 = 1.0 - ADAM_B2 ** ADAM_STEP

    def body(*refs):
        w_ref = refs[0]
        g_refs = refs[1:1 + n_g]
        m_ref, v_ref, g_out, d_out, m_out, v_out = refs[1 + n_g:]
        g = g_refs[0][...]
        for r in g_refs[1:]:
            g = g + r[...]
        m_new = ADAM_B1 * m_ref[...] + (1.0 - ADAM_B1) * g
        v_new = ADAM_B2 * v_ref[...] + (1.0 - ADAM_B2) * (g * g)
        m_hat = m_new / c1
        v_hat = v_new / c2
        g_out[...] = g
        d_out[...] = -ADAM_LR * (m_hat / (jnp.sqrt(v_hat) + ADAM_EPS) + ADAM_WD * w_ref[...])
        m_out[...] = m_new
        v_out[...] = v_new

    spec = pl.BlockSpec((rt, cols), lambda i: (i, 0))
    shape = jax.ShapeDtypeStruct((rows, cols), F32)
    return pl.pallas_call(
        body, name=name, grid=(rows // rt,),
        in_specs=[spec] * (3 + n_g), out_specs=[spec] * 4, out_shape=[shape] * 4,
        compiler_params=_params(("parallel",), 48),
    )(w, *grads, m, v)


def _in_proj(h, g_pre, w_main, w_lr):
    t_rows = h.shape[0]
    tm = _pick_tile(t_rows, 528, 16)
    n_col = w_main.shape[1] // D

    def body(h_ref, g_ref, w_ref, wlr_ref, proj_ref, u_ref, lr_ref):
        @pl.when(pl.program_id(1) == 0)
        def _():
            hh = h_ref[...]
            rstd = lax.rsqrt(jnp.mean(hh * hh, axis=-1, keepdims=True) + EPS)
            u = (hh * rstd * g_ref[...]).astype(BF16)
            u_ref[...] = u
            lr_ref[...] = _dot(u, wlr_ref[...])

        proj_ref[...] = _dot(u_ref[...], w_ref[...]).astype(BF16)

    return pl.pallas_call(
        body, name="in_proj", grid=(t_rows // tm, n_col),
        in_specs=[pl.BlockSpec((tm, D), lambda i, j: (i, 0)),
                  pl.BlockSpec((1, D), lambda i, j: (0, 0)),
                  pl.BlockSpec((D, D), lambda i, j: (0, j)),
                  pl.BlockSpec((D, LANES), lambda i, j: (0, 0))],
        out_specs=[pl.BlockSpec((tm, D), lambda i, j: (i, j)),
                   pl.BlockSpec((tm, D), lambda i, j: (i, 0)),
                   pl.BlockSpec((tm, LANES), lambda i, j: (i, 0))],
        out_shape=[jax.ShapeDtypeStruct((t_rows, w_main.shape[1]), BF16),
                   jax.ShapeDtypeStruct((t_rows, D), BF16),
                   jax.ShapeDtypeStruct((t_rows, LANES), F32)],
        compiler_params=_params(("parallel", "arbitrary"), 48),
    )(h, g_pre, w_main, w_lr)


def _conv_parts(p_ref, w_ref):
    cb = p_ref[:, 0:128].astype(F32)
    cc = p_ref[:, 128:256].astype(F32)
    cx = p_ref[:, 256:384].astype(F32)
    cz = p_ref[:, 384:512].astype(F32)
    rows = cb.shape[0]
    w = w_ref[...]
    p = cc * cx
    conv = pltpu.roll(p, 1, 0) * w[0:1] + p * w[1:2] + pltpu.roll(p, rows - 1, 0) * w[2:3]
    sz = _sigmoid(cz)
    return cb, cc, cx, cz, p, conv, sz, w


def _conv_fwd(proj, conv_w, n_seq, lf):
    def body(p_ref, w_ref, y_ref):
        cb, _, _, cz, _, conv, sz, _ = _conv_parts(p_ref, w_ref)
        y_ref[...] = (cb * conv * (cz * sz)).astype(BF16)

    return pl.pallas_call(
        body, name="conv_fwd", grid=(n_seq, N_CONV_TILES),
        in_specs=[pl.BlockSpec((lf, 512), lambda b, j: (b, j)),
                  pl.BlockSpec((3, 128), lambda b, j: (0, j))],
        out_specs=pl.BlockSpec((lf, 128), lambda b, j: (b, j)),
        out_shape=jax.ShapeDtypeStruct((n_seq * lf, D), BF16),
        compiler_params=_params(("parallel", "parallel"), 48),
    )(proj, conv_w)


def _conv_bwd(proj, conv_w, dyc, n_seq, lf):
    def body(p_ref, w_ref, dy_ref, dp_ref, dw_ref):
        cb, cc, cx, cz, p, conv, sz, w = _conv_parts(p_ref, w_ref)
        rows = cb.shape[0]
        dy = dy_ref[...].astype(F32)
        silu = cz * sz
        dcb = dy * conv * silu
        dconv = dy * cb * silu
        dcz = dy * cb * conv * (sz * (1.0 + cz * (1.0 - sz)))
        d_next = pltpu.roll(dconv, rows - 1, 0)
        d_prev = pltpu.roll(dconv, 1, 0)
        dp = d_next * w[0:1] + dconv * w[1:2] + d_prev * w[2:3]
        dp_ref[:, 0:128] = dcb.astype(BF16)
        dp_ref[:, 128:256] = (dp * cx).astype(BF16)
        dp_ref[:, 256:384] = (dp * cc).astype(BF16)
        dp_ref[:, 384:512] = dcz.astype(BF16)
        dw_ref[0:1, :] = jnp.sum(dconv * pltpu.roll(p, 1, 0), axis=0, keepdims=True)
        dw_ref[1:2, :] = jnp.sum(dconv * p, axis=0, keepdims=True)
        dw_ref[2:3, :] = jnp.sum(dconv * pltpu.roll(p, rows - 1, 0), axis=0, keepdims=True)

    return pl.pallas_call(
        body, name="conv_bwd", grid=(n_seq, N_CONV_TILES),
        in_specs=[pl.BlockSpec((lf, 512), lambda b, j: (b, j)),
                  pl.BlockSpec((3, 128), lambda b, j: (0, j)),
                  pl.BlockSpec((lf, 128), lambda b, j: (b, j))],
        out_specs=[pl.BlockSpec((lf, 512), lambda b, j: (b, j)),
                   pl.BlockSpec((None, 3, 128), lambda b, j: (b, 0, j))],
        out_shape=[jax.ShapeDtypeStruct((n_seq * lf, W_CONV), BF16),
                   jax.ShapeDtypeStruct((n_seq, 3, D), F32)],
        compiler_params=_params(("parallel", "parallel"), 48),
    )(proj, conv_w, dyc)


def _chunk_masks():
    row = lax.broadcasted_iota(jnp.int32, (CHUNK, CHUNK), 0)
    col = lax.broadcasted_iota(jnp.int32, (CHUNK, CHUNK), 1)
    lower = col <= row
    upper = col >= row
    strict_upper = col > row
    return lower, upper, strict_upper


def _gla_gates(lr_bf, wg_ref, bg_ref, lf):
    z = _dot(lr_bf, wg_ref[...]) + bg_ref[...]
    valid = lax.broadcasted_iota(jnp.int32, (lf, HEAD_K), 0) >= PAD_FRONT
    return z, valid


def _gla_states(direction, n_chunks, qkv_ref, g_s, b_s, st_s, tri):
    def local(c, carry):
        rows = pl.ds(pl.multiple_of(c * CHUNK, CHUNK), CHUNK)
        b = _tri_dot(tri, g_s[rows, :])
        b_s[rows, :] = b
        b_end = b[CHUNK - 1:CHUNK, :] if direction == 0 else b[0:1, :]
        k = qkv_ref[rows, 128:256].astype(F32)
        v = qkv_ref[rows, 256:512]
        k_dec = (k * jnp.exp(b_end - b)).astype(BF16)
        st_s[c] = _dot_tn(v, k_dec)
        return carry

    lax.fori_loop(0, n_chunks, local, 0)

    def scan(i, state):
        c = i if direction == 0 else n_chunks - 1 - i
        decay = _chunk_decay(direction, c, b_s)
        update = st_s[c]
        st_s[c] = state
        return state * decay + update

    lax.fori_loop(0, n_chunks, scan, jnp.zeros((HEAD_V, HEAD_K), F32))


def _chunk_decay(direction, c, b_s):
    if direction == 0:
        grp = b_s[pl.ds(pl.multiple_of(c * CHUNK + CHUNK - 8, 8), 8), :]
        return jnp.exp(grp[7:8, :])
    grp = b_s[pl.ds(pl.multiple_of(c * CHUNK, 8), 8), :]
    return jnp.exp(grp[0:1, :])


def _gla_fwd(proj, lr, wgf, wgb, bgf, bgb, n_seq, lf):
    n_chunks = lf // CHUNK
    scale = HEAD_K ** -0.5

    def body(qkv_ref, lr_ref, wgf_ref, wgb_ref, bgf_ref, bgb_ref, o_ref, g_s, b_s, st_s):
        lower, upper, strict_upper = _chunk_masks()
        lr_bf = lr_ref[...].astype(BF16)
        for direction in (0, 1):
            wg_ref, bg_ref = ((wgf_ref, bgf_ref), (wgb_ref, bgb_ref))[direction]
            z, valid = _gla_gates(lr_bf, wg_ref, bg_ref, lf)
            g_s[...] = jnp.where(valid, _log_sigmoid(z) / GATE_NORM, 0.0)
            tri = (lower if direction == 0 else upper).astype(BF16)
            smask = lower if direction == 0 else strict_upper
            _gla_states(direction, n_chunks, qkv_ref, g_s, b_s, st_s, tri)

            def out(c, carry):
                rows = pl.ds(pl.multiple_of(c * CHUNK, CHUNK), CHUNK)
                b = b_s[rows, :]
                q = qkv_ref[rows, 0:128].astype(F32) * scale
                k = qkv_ref[rows, 128:256].astype(F32)
                v = qkv_ref[rows, 256:512]
                q_in = (q * jnp.exp(b)).astype(BF16)
                k_in = (k * jnp.exp(-b)).astype(BF16)
                s = jnp.where(smask, _dot_nt(q_in, k_in), 0.0).astype(BF16)
                o = _dot(s, v) + _dot_nt(q_in, st_s[c].astype(BF16))
                if direction == 0:
                    o_ref[rows, :] = o
                else:
                    o_ref[rows, :] = o_ref[rows, :] + o
                return carry

            lax.fori_loop(0, n_chunks, out, 0)

    return pl.pallas_call(
        body, name="gla_fwd", grid=(n_seq, N_HEADS),
        in_specs=[pl.BlockSpec((lf, 512), lambda b, h: (b, N_CONV_TILES + h)),
                  pl.BlockSpec((lf, LANES), lambda b, h: (b, 0)),
                  pl.BlockSpec((None, LANES, HEAD_K), lambda b, h: (h, 0, 0)),
                  pl.BlockSpec((None, LANES, HEAD_K), lambda b, h: (h, 0, 0)),
                  pl.BlockSpec((None, 1, HEAD_K), lambda b, h: (h, 0, 0)),
                  pl.BlockSpec((None, 1, HEAD_K), lambda b, h: (h, 0, 0))],
        out_specs=pl.BlockSpec((lf, HEAD_V), lambda b, h: (b, h)),
        out_shape=jax.ShapeDtypeStruct((n_seq * lf, D), F32),
        scratch_shapes=[pltpu.VMEM((lf, HEAD_K), F32), pltpu.VMEM((lf, HEAD_K), F32),
                        pltpu.VMEM((n_chunks, HEAD_V, HEAD_K), F32)],
        compiler_params=_params(("parallel", "parallel"), 48),
    )(proj, lr, wgf, wgb, bgf, bgb)


def _gla_bwd(proj, lr, d_o, wgf, wgb, bgf, bgb, n_seq, lf):
    n_chunks = lf // CHUNK
    scale = HEAD_K ** -0.5

    def body(qkv_ref, lr_ref, do_ref, wgf_ref, wgb_ref, bgf_ref, bgb_ref,
             dqkv_ref, dlr_ref, dwgf_ref, dwgb_ref, dbg_ref,
             g_s, b_s, fac_s, dg_s, st_s, dst_s, acc_s):
        lower, upper, strict_upper = _chunk_masks()
        lr_bf = lr_ref[...].astype(BF16)
        acc_s[...] = jnp.zeros_like(acc_s)
        dlr = jnp.zeros((lf, LANES), F32)
        for direction in (0, 1):
            wg_ref, bg_ref = ((wgf_ref, bgf_ref), (wgb_ref, bgb_ref))[direction]
            z, valid = _gla_gates(lr_bf, wg_ref, bg_ref, lf)
            g_s[...] = jnp.where(valid, _log_sigmoid(z) / GATE_NORM, 0.0)
            fac_s[...] = jnp.where(valid, _sigmoid(-z) / GATE_NORM, 0.0)
            tri = (lower if direction == 0 else upper).astype(BF16)
            tri_t = (upper if direction == 0 else lower).astype(BF16)
            smask = lower if direction == 0 else strict_upper
            end_row = CHUNK - 1 if direction == 0 else 0
            _gla_states(direction, n_chunks, qkv_ref, g_s, b_s, st_s, tri)

            def state_grad_local(c, carry):
                rows = pl.ds(pl.multiple_of(c * CHUNK, CHUNK), CHUNK)
                q = qkv_ref[rows, 0:128].astype(F32) * scale
                q_in = (q * jnp.exp(b_s[rows, :])).astype(BF16)
                dst_s[c] = _dot_tn(do_ref[rows, :], q_in)
                return carry

            lax.fori_loop(0, n_chunks, state_grad_local, 0)

            def state_grad_scan(i, grad):
                c = n_chunks - 1 - i if direction == 0 else i
                decay = _chunk_decay(direction, c, b_s)
                local = dst_s[c]
                dst_s[c] = grad
                return local + grad * decay

            lax.fori_loop(0, n_chunks, state_grad_scan, jnp.zeros((HEAD_V, HEAD_K), F32))

            def chunk_grads(c, carry):
                rows = pl.ds(pl.multiple_of(c * CHUNK, CHUNK), CHUNK)
                b = b_s[rows, :]
                b_end = b[end_row:end_row + 1, :]
                q = qkv_ref[rows, 0:128].astype(F32) * scale
                k = qkv_ref[rows, 128:256].astype(F32)
                v = qkv_ref[rows, 256:512]
                d_out = do_ref[rows, :]
                e_pos = jnp.exp(b)
                e_neg = jnp.exp(-b)
                e_end = jnp.exp(b_end - b)
                q_in = q * e_pos
                k_in = k * e_neg
                k_dec = k * e_end
                q_in_bf = q_in.astype(BF16)
                k_in_bf = k_in.astype(BF16)
                state = st_s[c]
                d_state = dst_s[c]
                state_bf = state.astype(BF16)
                d_state_bf = d_state.astype(BF16)
                s = jnp.where(smask, _dot_nt(q_in_bf, k_in_bf), 0.0).astype(BF16)
                ds = jnp.where(smask, _dot_nt(d_out, v), 0.0).astype(BF16)
                dv = _dot_tn(s, d_out) + _dot_nt(k_dec.astype(BF16), d_state_bf)
                dq_in = _dot(ds, k_in_bf) + _dot(d_out, state_bf)
                dk_in = _dot_tn(ds, q_in_bf)
                dk_dec = _dot(v, d_state_bf)
                acc_s[rows, 0:128] = acc_s[rows, 0:128] + dq_in * e_pos * scale
                acc_s[rows, 128:256] = acc_s[rows, 128:256] + dk_in * e_neg + dk_dec * e_end
                acc_s[rows, 256:512] = acc_s[rows, 256:512] + dv
                dkk = dk_dec * k_dec
                db = dq_in * q_in - dk_in * k_in - dkk
                d_decay = jnp.sum(d_state * state, axis=0, keepdims=True)
                db_end = jnp.sum(dkk, axis=0, keepdims=True) + d_decay * jnp.exp(b_end)
                at_end = lax.broadcasted_iota(jnp.int32, (CHUNK, HEAD_K), 0) == end_row
                db = db + jnp.where(at_end, db_end, 0.0)
                dg_s[rows, :] = _tri_dot(tri_t, db)
                return carry

            lax.fori_loop(0, n_chunks, chunk_grads, 0)

            dz = dg_s[...] * fac_s[...]
            dz_bf = dz.astype(BF16)
            dbg_ref[direction:direction + 1, :] = jnp.sum(dz, axis=0, keepdims=True)
            (dwgf_ref, dwgb_ref)[direction][...] = _dot_tn(lr_bf, dz_bf)
            dlr = dlr + _dot_nt(dz_bf, wg_ref[...])

        dqkv_ref[...] = acc_s[...].astype(BF16)

        @pl.when(pl.program_id(1) == 0)
        def _():
            dlr_ref[...] = dlr

        @pl.when(pl.program_id(1) != 0)
        def _():
            dlr_ref[...] = dlr_ref[...] + dlr

    gate_w = pl.BlockSpec((None, LANES, HEAD_K), lambda b, h: (h, 0, 0))
    gate_b = pl.BlockSpec((None, 1, HEAD_K), lambda b, h: (h, 0, 0))
    return pl.pallas_call(
        body, name="gla_bwd", grid=(n_seq, N_HEADS),
        in_specs=[pl.BlockSpec((lf, 512), lambda b, h: (b, N_CONV_TILES + h)),
                  pl.BlockSpec((lf, LANES), lambda b, h: (b, 0)),
                  pl.BlockSpec((lf, HEAD_V), lambda b, h: (b, h)),
                  gate_w, gate_w, gate_b, gate_b],
        out_specs=[pl.BlockSpec((lf, 512), lambda b, h: (b, h)),
                   pl.BlockSpec((lf, LANES), lambda b, h: (b, 0)),
                   pl.BlockSpec((None, None, LANES, HEAD_K), lambda b, h: (b, h, 0, 0)),
                   pl.BlockSpec((None, None, LANES, HEAD_K), lambda b, h: (b, h, 0, 0)),
                   pl.BlockSpec((None, None, 2, HEAD_K), lambda b, h: (b, h, 0, 0))],
        out_shape=[jax.ShapeDtypeStruct((n_seq * lf, W_GLA), BF16),
                   jax.ShapeDtypeStruct((n_seq * lf, LANES), F32),
                   jax.ShapeDtypeStruct((n_seq, N_HEADS, LANES, HEAD_K), F32),
                   jax.ShapeDtypeStruct((n_seq, N_HEADS, LANES, HEAD_K), F32),
                   jax.ShapeDtypeStruct((n_seq, N_HEADS, 2, HEAD_K), F32)],
        scratch_shapes=[pltpu.VMEM((lf, HEAD_K), F32), pltpu.VMEM((lf, HEAD_K), F32),
                        pltpu.VMEM((lf, HEAD_K), F32), pltpu.VMEM((lf, HEAD_K), F32),
                        pltpu.VMEM((n_chunks, HEAD_V, HEAD_K), F32),
                        pltpu.VMEM((n_chunks, HEAD_V, HEAD_K), F32),
                        pltpu.VMEM((lf, 512), F32)],
        compiler_params=_params(("parallel", "arbitrary"), 56),
    )(proj, lr, d_o, wgf, wgb, bgf, bgb)


def _tail(h, tgt, yc, o, proj, w3, gamma, g_post, lf):
    t_rows = h.shape[0]
    tm = _pick_tile(t_rows, 256, CHUNK)
    n_chunks = lf // CHUNK
    per_tile = tm // CHUNK

    def body(h_ref, tgt_ref, yc_ref, o_ref, r_ref, ma_ref, mb_ref, w_hbm, gamma_ref, gpost_ref,
             dres_ref, yg_ref, merged_ref, dout_ref, dpc_ref, dpg_ref, dyc_ref, do_ref, dtail_ref,
             loss_ref, dgpost_ref, dgamma_ref, w_s, w_sem):
        i = pl.program_id(0)

        @pl.when(i == 0)
        def _():
            cp = pltpu.make_async_copy(w_hbm, w_s, w_sem)
            cp.start()
            cp.wait()
            loss_ref[...] = jnp.zeros_like(loss_ref)
            dgpost_ref[...] = jnp.zeros_like(dgpost_ref)
            dgamma_ref[...] = jnp.zeros_like(dgamma_ref)

        gamma = gamma_ref[...]
        o = o_ref[...]
        r = r_ref[...].astype(F32)
        sr = _sigmoid(r)
        silu_r = r * sr
        n_parts, rstd_parts = [], []
        for hd in range(N_HEADS):
            oh = o[:, hd * HEAD_V:(hd + 1) * HEAD_V]
            rstd = lax.rsqrt(jnp.mean(oh * oh, axis=-1, keepdims=True) + EPS)
            n_parts.append(oh * rstd)
            rstd_parts.append(rstd)
        n = jnp.concatenate(n_parts, axis=-1)
        gamma_t = jnp.concatenate([gamma] * N_HEADS, axis=-1)
        yg = n * gamma_t * silu_r
        yg_bf = yg.astype(BF16)
        yg_ref[...] = yg_bf
        yc = yc_ref[...]
        pc = _dot(yc, w_s[0])
        pg = _dot(yg_bf, w_s[1])
        sa = _sigmoid(ma_ref[...].astype(F32))
        sb = _sigmoid(mb_ref[...].astype(F32))
        merged = (sa * pc + sb * pg).astype(BF16)
        merged_ref[...] = merged
        out = _dot(merged, w_s[2])
        rstd2 = lax.rsqrt(jnp.mean(out * out, axis=-1, keepdims=True) + EPS)
        nn = out * rstd2
        gpost = gpost_ref[...]
        y = h_ref[...] + nn * gpost

        rowi = lax.broadcasted_iota(jnp.int32, (tm, 1), 0)
        keep = jnp.zeros((tm, 1), F32)
        for kk in range(per_tile):
            is_tok = ((i * per_tile + kk) % n_chunks) != 0
            f = jnp.where(is_tok, 1.0, 0.0)
            keep = jnp.where((rowi >= kk * CHUNK) & (rowi < (kk + 1) * CHUNK), f, keep)
        diff = (y - tgt_ref[...]) * keep
        loss_ref[...] += jnp.sum(diff * diff) * (0.5 / D)
        dy = diff * (1.0 / D)
        dres_ref[...] = dy
        dgpost_ref[...] += jnp.sum(dy * nn, axis=0, keepdims=True)
        dn = dy * gpost
        dout = (rstd2 * (dn - nn * jnp.mean(dn * nn, axis=-1, keepdims=True))).astype(BF16)
        dout_ref[...] = dout
        dmerged = _dot_nt(dout, w_s[2])
        dpc = (dmerged * sa).astype(BF16)
        dpg = (dmerged * sb).astype(BF16)
        dpc_ref[...] = dpc
        dpg_ref[...] = dpg
        dtail_ref[:, D:2 * D] = (dmerged * pc * (sa * (1.0 - sa))).astype(BF16)
        dtail_ref[:, 2 * D:3 * D] = (dmerged * pg * (sb * (1.0 - sb))).astype(BF16)
        dyc_ref[...] = _dot_nt(dpc, w_s[0]).astype(BF16)
        dyg = _dot_nt(dpg, w_s[1])
        dtail_ref[:, 0:D] = (dyg * n * gamma_t * (sr * (1.0 + r * (1.0 - sr)))).astype(BF16)
        dgam_full = jnp.sum(dyg * n * silu_r, axis=0, keepdims=True)
        dgam = dgam_full[:, 0:HEAD_V]
        for hd in range(1, N_HEADS):
            dgam = dgam + dgam_full[:, hd * HEAD_V:(hd + 1) * HEAD_V]
        dgamma_ref[...] += dgam
        dng = dyg * gamma_t * silu_r
        do_parts = []
        for hd in range(N_HEADS):
            sl = slice(hd * HEAD_V, (hd + 1) * HEAD_V)
            dnh = dng[:, sl]
            nh = n_parts[hd]
            do_parts.append(rstd_parts[hd] * (dnh - nh * jnp.mean(dnh * nh, axis=-1, keepdims=True)))
        do_ref[...] = jnp.concatenate(do_parts, axis=-1).astype(BF16)

    row = lambda c: pl.BlockSpec((tm, D), lambda i: (i, c))
    const = lambda shape: pl.BlockSpec(shape, lambda i: (0, 0))
    act = jax.ShapeDtypeStruct((t_rows, D), BF16)
    return pl.pallas_call(
        body, name="tail", grid=(t_rows // tm,),
        in_specs=[row(0), row(0), row(0), row(0), row(6), row(7), row(8),
                  pl.BlockSpec(memory_space=pl.ANY), const((1, HEAD_V)), const((1, D))],
        out_specs=[row(0)] * 8 + [pl.BlockSpec((tm, W_TAIL), lambda i: (i, 0)),
                                  const((8, LANES)), const((1, D)), const((1, HEAD_V))],
        out_shape=[jax.ShapeDtypeStruct((t_rows, D), F32)] + [act] * 7
                  + [jax.ShapeDtypeStruct((t_rows, W_TAIL), BF16),
                     jax.ShapeDtypeStruct((8, LANES), F32),
                     jax.ShapeDtypeStruct((1, D), F32),
                     jax.ShapeDtypeStruct((1, HEAD_V), F32)],
        scratch_shapes=[pltpu.VMEM((3, D, D), BF16), pltpu.SemaphoreType.DMA],
        compiler_params=_params(("arbitrary",), 56),
    )(h, tgt, yc, o, proj, proj, proj, w3, gamma, g_post)


def _wgrad(a, b, name, out_dtype=BF16):
    t_rows, m = a.shape
    n = b.shape[1]
    tn = D if n % D == 0 else n
    tk = _pick_tile(t_rows, 528, 16)
    n_k = t_rows // tk

    def body(a_ref, b_ref, o_ref, acc):
        k = pl.program_id(1)

        @pl.when(k == 0)
        def _():
            acc[...] = jnp.zeros_like(acc)

        acc[...] += _dot_tn(a_ref[...].astype(BF16), b_ref[...].astype(BF16))

        @pl.when(k == n_k - 1)
        def _():
            o_ref[...] = acc[...].astype(out_dtype)

    return pl.pallas_call(
        body, name=name, grid=(n // tn, n_k),
        in_specs=[pl.BlockSpec((tk, m), lambda j, k: (k, 0)),
                  pl.BlockSpec((tk, tn), lambda j, k: (k, j))],
        out_specs=pl.BlockSpec((m, tn), lambda j, k: (0, j)),
        out_shape=jax.ShapeDtypeStruct((m, n), out_dtype),
        scratch_shapes=[pltpu.VMEM((m, tn), F32)],
        compiler_params=_params(("parallel", "arbitrary"), 48),
    )(a, b)


def _dgrad_in(dpc, dpg, dpt, dlr, w_main, w_lr, h, g_pre, dres):
    t_rows = h.shape[0]
    tm = _pick_tile(t_rows, 528, 16)
    n_c, n_g, n_t = W_CONV // D, W_GLA // D, W_TAIL // D
    n_col = n_c + n_g + n_t

    def body(dpc_ref, dpg_ref, dpt_ref, dlr_ref, w_ref, wlr_ref, h_ref, g_ref, dres_ref,
             dh_ref, dg_ref, acc):
        i, j = pl.program_id(0), pl.program_id(1)

        @pl.when(j == 0)
        def _():
            acc[...] = _dot_nt(dlr_ref[...].astype(BF16), wlr_ref[...])

        @pl.when(j < n_c)
        def _():
            acc[...] += _dot_nt(dpc_ref[...], w_ref[...])

        @pl.when((j >= n_c) & (j < n_c + n_g))
        def _():
            acc[...] += _dot_nt(dpg_ref[...], w_ref[...])

        @pl.when(j >= n_c + n_g)
        def _():
            acc[...] += _dot_nt(dpt_ref[...], w_ref[...])

        @pl.when((i == 0) & (j == 0))
        def _():
            dg_ref[...] = jnp.zeros_like(dg_ref)

        @pl.when(j == n_col - 1)
        def _():
            hh = h_ref[...]
            rstd = lax.rsqrt(jnp.mean(hh * hh, axis=-1, keepdims=True) + EPS)
            xhat = hh * rstd
            du = acc[...]
            dg_ref[...] += jnp.sum(du * xhat, axis=0, keepdims=True)
            dx = du * g_ref[...]
            dh_ref[...] = rstd * (dx - xhat * jnp.mean(dx * xhat, axis=-1, keepdims=True)) + dres_ref[...]

    clamp = lambda v, lo, hi: jnp.minimum(jnp.maximum(v, lo), hi)
    return pl.pallas_call(
        body, name="dgrad_in", grid=(t_rows // tm, n_col),
        in_specs=[pl.BlockSpec((tm, D), lambda i, j: (i, clamp(j, 0, n_c - 1))),
                  pl.BlockSpec((tm, D), lambda i, j: (i, clamp(j - n_c, 0, n_g - 1))),
                  pl.BlockSpec((tm, D), lambda i, j: (i, clamp(j - n_c - n_g, 0, n_t - 1))),
                  pl.BlockSpec((tm, LANES), lambda i, j: (i, 0)),
                  pl.BlockSpec((D, D), lambda i, j: (0, j)),
                  pl.BlockSpec((D, LANES), lambda i, j: (0, 0)),
                  pl.BlockSpec((tm, D), lambda i, j: (i, 0)),
                  pl.BlockSpec((1, D), lambda i, j: (0, 0)),
                  pl.BlockSpec((tm, D), lambda i, j: (i, 0))],
        out_specs=[pl.BlockSpec((tm, D), lambda i, j: (i, 0)),
                   pl.BlockSpec((1, D), lambda i, j: (0, 0))],
        out_shape=[jax.ShapeDtypeStruct((t_rows, D), F32), jax.ShapeDtypeStruct((1, D), F32)],
        scratch_shapes=[pltpu.VMEM((tm, D), F32)],
        compiler_params=_params(("arbitrary", "arbitrary"), 48),
    )(dpc, dpg, dpt, dlr, w_main, w_lr, h, g_pre, dres)


OFF_CB, OFF_CC, OFF_CX, OFF_CZ = 0, 1024, 2048, 3072
OFF_Q, OFF_K, OFF_V, OFF_R = 4096, 4608, 5120, 6144
OFF_LR, OFF_MA, OFF_MB = 7168, 7200, 8224


def _main_columns(w):
    parts = []
    for j in range(N_CONV_TILES):
        for off in (OFF_CB, OFF_CC, OFF_CX, OFF_CZ):
            parts.append(w[:, off + 128 * j: off + 128 * (j + 1)])
    for hd in range(N_HEADS):
        parts.append(w[:, OFF_Q + 128 * hd: OFF_Q + 128 * (hd + 1)])
        parts.append(w[:, OFF_K + 128 * hd: OFF_K + 128 * (hd + 1)])
        parts.append(w[:, OFF_V + 256 * hd: OFF_V + 256 * (hd + 1)])
    parts.append(w[:, OFF_R: OFF_R + D])
    parts.append(w[:, OFF_MA: OFF_MA + D])
    parts.append(w[:, OFF_MB: OFF_MB + D])
    return jnp.concatenate(parts, axis=1)


def _reference_columns(g_conv, g_gla, g_tail, g_lr):
    parts = []
    for g in range(4):
        for j in range(N_CONV_TILES):
            parts.append(g_conv[:, 512 * j + 128 * g: 512 * j + 128 * (g + 1)])
    for hd in range(N_HEADS):
        parts.append(g_gla[:, 512 * hd: 512 * hd + 128])
    for hd in range(N_HEADS):
        parts.append(g_gla[:, 512 * hd + 128: 512 * hd + 256])
    for hd in range(N_HEADS):
        parts.append(g_gla[:, 512 * hd + 256: 512 * hd + 512])
    parts.append(g_tail[:, 0:D])
    parts.append(g_lr[:, 0:2 * RANK])
    parts.append(g_tail[:, D:3 * D])
    return jnp.concatenate(parts, axis=1)


def _pack(arrs, rows):
    flat = jnp.concatenate([a.reshape(-1) for a in arrs])
    return jnp.pad(flat, (0, rows * LANES - flat.shape[0])).reshape(rows, LANES)


def _unpack(packed, shapes):
    flat = packed.reshape(-1)
    out, pos = [], 0
    for s in shapes:
        size = 1
        for d in s:
            size *= d
        out.append(flat[pos:pos + size].reshape(s))
        pos += size
    return out


def _rows_for(shapes):
    total = 0
    for s in shapes:
        size = 1
        for d in s:
            size *= d
        total += size
    return -(-total // (8 * LANES)) * 8


def kernel(x, meta_tokens, norm_pre, w_in, conv_w, w_gate_fwd, b_gate_fwd, w_gate_bwd, b_gate_bwd, gla_norm, w_out_conv, w_out_gla, w_merge_out, norm_post, loss_target, m_meta_tokens, m_norm_pre, m_w_in, m_conv_w, m_w_gate_fwd, m_b_gate_fwd, m_w_gate_bwd, m_b_gate_bwd, m_gla_norm, m_w_out_conv, m_w_out_gla, m_w_merge_out, m_norm_post, v_meta_tokens, v_norm_pre, v_w_in, v_conv_w, v_w_gate_fwd, v_b_gate_fwd, v_w_gate_bwd, v_b_gate_bwd, v_gla_norm, v_w_out_conv, v_w_out_gla, v_w_merge_out, v_norm_post):
    n_seq, seq, _ = x.shape
    lf = CHUNK + seq
    t_rows = n_seq * lf
    shard = 2 * lax.axis_index("x") + lax.axis_index("y")

    w_in_bf = _cast_bf16(w_in[0], "cast_w_in")
    w_out_bf = _cast_bf16(jnp.concatenate([w_out_conv[0], w_out_gla[0], w_merge_out[0]], axis=0), "cast_w_out")
    small_shapes = [(N_META, D // 4), (3, D // 4), (RANK, HEAD_K), (RANK, HEAD_K)]
    small = _pack([meta_tokens, conv_w[0], w_gate_fwd[0], w_gate_bwd[0]], _rows_for(small_shapes))
    w_in_all, w_out_all, small_all = _exchange(
        "gather_weights", [w_in_bf, w_out_bf, small], PLANE_FLIPS, (2, 1, 0), "gather")

    w_full = jnp.transpose(w_in_all, (1, 0, 2)).reshape(D, N_IN)
    w_main = _main_columns(w_full)
    w_lr = jnp.pad(w_full[:, OFF_LR:OFF_LR + 2 * RANK], ((0, 0), (0, LANES - 2 * RANK)))
    w3 = jnp.transpose(w_out_all.reshape(4, 3, D // 4, D), (1, 0, 2, 3)).reshape(3, D, D)
    smalls = [_unpack(small_all[s], small_shapes) for s in range(4)]
    meta_full = jnp.concatenate([smalls[s][0] for s in range(4)], axis=1)
    conv_full = jnp.concatenate([smalls[s][1] for s in range(4)], axis=1)
    wgf = jnp.stack([jnp.pad(smalls[s][2], ((0, LANES - RANK), (0, 0))) for s in range(4)]).astype(BF16)
    wgb = jnp.stack([jnp.pad(smalls[s][3], ((RANK, LANES - 2 * RANK), (0, 0))) for s in range(4)]).astype(BF16)
    bgf = b_gate_fwd.reshape(N_HEADS, 1, HEAD_K)
    bgb = b_gate_bwd.reshape(N_HEADS, 1, HEAD_K)

    head = jnp.concatenate([jnp.zeros((PAD_FRONT, D), F32), meta_full], axis=0)
    h = jnp.concatenate([jnp.broadcast_to(head[None], (n_seq, CHUNK, D)), x], axis=1).reshape(t_rows, D)
    tgt = jnp.pad(loss_target, ((0, 0), (CHUNK, 0), (0, 0))).reshape(t_rows, D)

    proj, u, lr = _in_proj(h, norm_pre, w_main, w_lr)
    yc = _conv_fwd(proj, conv_full, n_seq, lf)
    o = _gla_fwd(proj, lr, wgf, wgb, bgf, bgb, n_seq, lf)
    (dres, yg, merged, dout, dpc_out, dpg_out, dyc, d_o, dtail, loss_acc, d_gpost, d_gamma) = _tail(
        h, tgt, yc, o, proj, w3, gla_norm, norm_post, lf)
    g_w_oc = _wgrad(yc, dpc_out, "wgrad_out_conv")
    g_w_og = _wgrad(yg, dpg_out, "wgrad_out_gla")
    g_w_mo = _wgrad(merged, dout, "wgrad_merge_out")
    dgla, dlr, dwgf_p, dwgb_p, dbg_p = _gla_bwd(proj, lr, d_o, wgf, wgb, bgf, bgb, n_seq, lf)
    dconv, dconvw_p = _conv_bwd(proj, conv_full, dyc, n_seq, lf)
    dh, d_gpre = _dgrad_in(dconv, dgla, dtail, dlr, w_main, w_lr, h, norm_pre, dres)
    g_conv = _wgrad(u, dconv, "wgrad_in_conv")
    g_gla = _wgrad(u, dgla, "wgrad_in_gla")
    g_tail = _wgrad(u, dtail, "wgrad_in_tail")
    g_lr = _wgrad(u, dlr, "wgrad_in_lr")

    dh3 = dh.reshape(n_seq, lf, D)
    grad_x = dh3[:, CHUNK:, :]

    d_meta = jnp.sum(dh3[:, PAD_FRONT:CHUNK, :], axis=0)
    d_convw = jnp.sum(dconvw_p, axis=0)
    d_wgf = jnp.transpose(jnp.sum(dwgf_p, axis=0)[:, 0:RANK, :], (1, 0, 2)).reshape(RANK, N_HEADS * HEAD_K)
    d_wgb = jnp.transpose(jnp.sum(dwgb_p, axis=0)[:, RANK:2 * RANK, :], (1, 0, 2)).reshape(RANK, N_HEADS * HEAD_K)
    d_bg = jnp.sum(dbg_p, axis=0)
    d_bgf = d_bg[:, 0, :].reshape(1, N_HEADS * HEAD_K)
    d_bgb = d_bg[:, 1, :].reshape(1, N_HEADS * HEAD_K)
    part_shapes = [(N_META, D), (3, D), (RANK, 512), (RANK, 512), (1, D), (1, 512), (1, 512), (1, HEAD_V),
                   (1, D), (1, LANES)]
    parts = _pack([d_meta, d_convw, d_wgf, d_wgb, d_gpre, d_bgf, d_bgb, d_gamma, d_gpost, loss_acc[0:1, :]],
                  _rows_for(part_shapes))
    (parts_all,) = _exchange("gather_small_grads", [parts], ALL_FLIPS, (4, 2, 1), "gather")
    (g_meta, g_convw, g_wgf, g_wgb, g_npre, g_bgf, g_bgb, g_gnorm, g_npost, loss_row) = _unpack(
        _sum_slots(parts_all, "sum_small_grads"), part_shapes)
    loss = loss_row[0, 0]

    def col_shard(a, width):
        return lax.dynamic_slice_in_dim(a, shard * width, width, axis=a.ndim - 1)

    upd_shapes = [(N_META, D // 4), (3, D // 4), (RANK, HEAD_K), (RANK, HEAD_K), (1, D), (1, 512), (1, 512),
                  (1, HEAD_V), (1, D)]
    upd_rows = _rows_for(upd_shapes)
    small_w = _pack([meta_tokens, conv_w[0], w_gate_fwd[0], w_gate_bwd[0], norm_pre, b_gate_fwd, b_gate_bwd,
                     gla_norm, norm_post], upd_rows)
    small_g = _pack([col_shard(g_meta, D // 4), col_shard(g_convw, D // 4), col_shard(g_wgf, HEAD_K),
                     col_shard(g_wgb, HEAD_K), g_npre, g_bgf, g_bgb, g_gnorm, g_npost], upd_rows)
    small_m = _pack([m_meta_tokens, m_conv_w[0], m_w_gate_fwd[0], m_w_gate_bwd[0], m_norm_pre, m_b_gate_fwd,
                     m_b_gate_bwd, m_gla_norm, m_norm_post], upd_rows)
    small_v = _pack([v_meta_tokens, v_conv_w[0], v_w_gate_fwd[0], v_w_gate_bwd[0], v_norm_pre, v_b_gate_fwd,
                     v_b_gate_bwd, v_gla_norm, v_norm_post], upd_rows)
    small_out = [_unpack(a, upd_shapes) for a in _adamw(small_w, [small_g], small_m, small_v, "adamw_small")]

    g_in_full = _reference_columns(g_conv, g_gla, g_tail, g_lr)
    g_in_slots = jnp.stack([g_in_full[:, SHARD_IN * s: SHARD_IN * (s + 1)] for s in range(4)])
    g_out_slots = jnp.concatenate([g.reshape(4, D // 4, D) for g in (g_w_oc, g_w_og, g_w_mo)], axis=1)
    got_in, got_out = _exchange("scatter_weight_grads", [g_in_slots, g_out_slots], PLANE_FLIPS, (2, 1, 0), "scatter")
    plane_in = _sum_slots(got_in, "sum_w_in_grads")
    plane_out = _sum_slots(got_out, "sum_w_out_grads")
    other_in, other_out = _exchange("swap_plane_sums", [plane_in, plane_out], SIBLING_FLIPS, (0, 0, 0), "swap")
    big_in = _adamw(w_in[0], [plane_in, other_in], m_w_in[0], v_w_in[0], "adamw_w_in")
    w_out_rows = jnp.concatenate([w_out_conv[0], w_out_gla[0], w_merge_out[0]], axis=0)
    m_out_rows = jnp.concatenate([m_w_out_conv[0], m_w_out_gla[0], m_w_merge_out[0]], axis=0)
    v_out_rows = jnp.concatenate([v_w_out_conv[0], v_w_out_gla[0], v_w_merge_out[0]], axis=0)
    big_out = _adamw(w_out_rows, [plane_out, other_out], m_out_rows, v_out_rows, "adamw_w_out")

    results = []
    for kind in range(4):
        sm = small_out[kind]
        w_in_part = big_in[kind][None]
        outs3 = big_out[kind].reshape(3, 1, D // 4, D)
        results.extend([
            sm[0], sm[4], w_in_part, sm[1][None], sm[2][None], sm[5], sm[3][None], sm[6], sm[7],
            outs3[0], outs3[1], outs3[2], sm[8]])
    return (loss, grad_x, *results)
```

```python
import functools

import jax
import jax.numpy as jnp
from jax import lax
from jax.experimental import pallas as pl
from jax.experimental.pallas import tpu as pltpu

F32 = jnp.float32
BF16 = jnp.bfloat16
MESH = pl.DeviceIdType.MESH

D = 1024
N_META = 16
CHUNK = 64
PAD_FRONT = CHUNK - N_META
N_HEADS = 4
HEAD_K = 128
HEAD_V = 256
RANK = 16
EPS = 1e-6
GATE_NORM = 16.0
N_IN = 9248
SHARD_IN = N_IN // 4
LANES = 128
N_CONV_TILES = 8
W_CONV = 4096
W_GLA = 2048
W_TAIL = 3072
N_MAIN = W_CONV + W_GLA + W_TAIL
MIB = 1024 * 1024

ADAM_LR = 0.001
ADAM_B1 = 0.9
ADAM_B2 = 0.999
ADAM_EPS = 1e-08
ADAM_WD = 0.01
ADAM_STEP = 10


def _params(sem=None, vmem_mib=None):
    return pltpu.CompilerParams(
        dimension_semantics=sem,
        vmem_limit_bytes=None if vmem_mib is None else vmem_mib * MIB)


def _pick_tile(n, target, mult):
    best = None
    for t in range(mult, min(n, target) + 1, mult):
        if n % t == 0:
            best = t
    return n if best is None else best


def _sigmoid(v):
    return 1.0 / (1.0 + jnp.exp(-v))


def _log_sigmoid(v):
    return jnp.minimum(v, 0.0) - jnp.log(1.0 + jnp.exp(-jnp.abs(v)))


def _dot(a, b):
    return jnp.dot(a, b, preferred_element_type=F32)


def _dot_nt(a, b):
    return lax.dot_general(a, b, (((1,), (1,)), ((), ())), preferred_element_type=F32)


def _dot_tn(a, b):
    return lax.dot_general(a, b, (((0,), (0,)), ((), ())), preferred_element_type=F32)


def _tri_dot(tri, v):
    hi = v.astype(BF16)
    lo = (v - hi.astype(F32)).astype(BF16)
    return _dot(tri, hi) + _dot(tri, lo)


PLANE_FLIPS = ((1, 0, 0), (0, 1, 0), (1, 1, 0))
ALL_FLIPS = tuple((m >> 2 & 1, m >> 1 & 1, m & 1) for m in range(1, 8))
SIBLING_FLIPS = ((0, 0, 1),)


def _exchange(name, arrs, flips, slot_weights, mode):
    n = len(arrs)
    n_slots = 1
    for w in slot_weights:
        n_slots += w
    if mode == "gather":
        out_shape = [jax.ShapeDtypeStruct((n_slots,) + a.shape, a.dtype) for a in arrs]
    else:
        out_shape = [jax.ShapeDtypeStruct(a.shape, a.dtype) for a in arrs]

    def body(*refs):
        ins, outs = refs[:n], refs[n:2 * n]
        send_sems, recv_sems, local_sems = refs[2 * n:]
        pos = (lax.axis_index("x"), lax.axis_index("y"), lax.axis_index("c"))

        def slot_of(p):
            return p[0] * slot_weights[0] + p[1] * slot_weights[1] + p[2] * slot_weights[2]

        peers = [tuple(1 - pos[a] if f[a] else pos[a] for a in range(3)) for f in flips]
        me = slot_of(pos)
        local = []
        sends = []
        for i in range(n):
            if mode != "swap":
                src = ins[i] if mode == "gather" else ins[i].at[me]
                cp = pltpu.make_async_copy(src, outs[i].at[me], local_sems.at[i])
                cp.start()
                local.append(cp)
            for k, peer in enumerate(peers):
                if mode == "gather":
                    src, dst = ins[i], outs[i].at[me]
                elif mode == "scatter":
                    src, dst = ins[i].at[slot_of(peer)], outs[i].at[me]
                else:
                    src, dst = ins[i], outs[i]
                cp = pltpu.make_async_remote_copy(
                    src_ref=src, dst_ref=dst, send_sem=send_sems.at[i, k], recv_sem=recv_sems.at[i, k],
                    device_id=peer, device_id_type=MESH)
                cp.start()
                sends.append(cp)
        for i in range(n):
            for k, peer in enumerate(peers):
                if mode == "gather":
                    src, dst = ins[i], outs[i].at[slot_of(peer)]
                elif mode == "scatter":
                    src, dst = ins[i].at[me], outs[i].at[slot_of(peer)]
                else:
                    src, dst = ins[i], outs[i]
                arrival = pltpu.make_async_remote_copy(
                    src_ref=src, dst_ref=dst, send_sem=send_sems.at[i, k], recv_sem=recv_sems.at[i, k],
                    device_id=peer, device_id_type=MESH)
                arrival.wait_recv()
        for cp in sends:
            cp.wait_send()
        for cp in local:
            cp.wait()

    hbm = pl.BlockSpec(memory_space=pl.ANY)
    outs = pl.pallas_call(
        body, name=name, out_shape=out_shape,
        in_specs=[hbm] * n, out_specs=[hbm] * n,
        scratch_shapes=[pltpu.SemaphoreType.DMA((n, len(flips))),
                        pltpu.SemaphoreType.DMA((n, len(flips))),
                        pltpu.SemaphoreType.DMA((n,))],
        compiler_params=pltpu.CompilerParams(has_side_effects=True),
    )(*arrs)
    return list(outs)


def _cast_bf16(a, name):
    rows, cols = a.shape
    rt = _pick_tile(rows, 256, 16)

    def body(a_ref, o_ref):
        o_ref[...] = a_ref[...].astype(BF16)

    return pl.pallas_call(
        body, name=name, grid=(rows // rt,),
        in_specs=[pl.BlockSpec((rt, cols), lambda i: (i, 0))],
        out_specs=pl.BlockSpec((rt, cols), lambda i: (i, 0)),
        out_shape=jax.ShapeDtypeStruct(a.shape, BF16),
        compiler_params=_params(("parallel",)),
    )(a)


def _sum_slots(buf, name):
    n_slots, rows, cols = buf.shape
    rt = _pick_tile(rows, 128, 16)

    def body(b_ref, o_ref):
        acc = b_ref[0].astype(F32)
        for s in range(1, n_slots):
            acc = acc + b_ref[s].astype(F32)
        o_ref[...] = acc

    return pl.pallas_call(
        body, name=name, grid=(rows // rt,),
        in_specs=[pl.BlockSpec((n_slots, rt, cols), lambda i: (0, i, 0))],
        out_specs=pl.BlockSpec((rt, cols), lambda i: (i, 0)),
        out_shape=jax.ShapeDtypeStruct((rows, cols), F32),
        compiler_params=_params(("parallel",), 40),
    )(buf)


def _adamw(w, grads, m, v, name):
    rows, cols = w.shape
    rt = _pick_tile(rows, 128, 8)
    n_g = len(grads)
    c1 = 1.0 - ADAM_B1 ** ADAM_STEP
    c2 = 1.0 - ADAM_B2 ** ADAM_STEP

    def body(*refs):
        w_ref = refs[0]
        g_refs = refs[1:1 + n_g]
        m_ref, v_ref, g_out, d_out, m_out, v_out = refs[1 + n_g:]
        g = g_refs[0][...]
        for r in g_refs[1:]:
            g = g + r[...]
        m_new = ADAM_B1 * m_ref[...] + (1.0 - ADAM_B1) * g
        v_new = ADAM_B2 * v_ref[...] + (1.0 - ADAM_B2) * (g * g)
        m_hat = m_new / c1
        v_hat = v_new / c2
        g_out[...] = g
        d_out[...] = -ADAM_LR * (m_hat / (jnp.sqrt(v_hat) + ADAM_EPS) + ADAM_WD * w_ref[...])
        m_out[...] = m_new
        v_out[...] = v_new

    spec = pl.BlockSpec((rt, cols), lambda i: (i, 0))
    shape = jax.ShapeDtypeStruct((rows, cols), F32)
    return pl.pallas_call(
        body, name=name, grid=(rows // rt,),
        in_specs=[spec] * (3 + n_g), out_specs=[spec] * 4, out_shape=[shape] * 4,
        compiler_params=_params(("parallel",), 48),
    )(w, *grads, m, v)


def _in_proj(h, g_pre, w_main, w_lr):
    t_rows = h.shape[0]
    tm = _pick_tile(t_rows, 528, 16)
    n_col = w_main.shape[1] // D

    def body(h_ref, g_ref, w_ref, wlr_ref, proj_ref, u_ref, lr_ref):
        @pl.when(pl.program_id(1) == 0)
        def _():
            hh = h_ref[...]
            rstd = lax.rsqrt(jnp.mean(hh * hh, axis=-1, keepdims=True) + EPS)
            u = (hh * rstd * g_ref[...]).astype(BF16)
            u_ref[...] = u
            lr_ref[...] = _dot(u, wlr_ref[...])

        proj_ref[...] = _dot(u_ref[...], w_ref[...]).astype(BF16)

    return pl.pallas_call(
        body, name="in_proj", grid=(t_rows // tm, n_col),
        in_specs=[pl.BlockSpec((tm, D), lambda i, j: (i, 0)),
                  pl.BlockSpec((1, D), lambda i, j: (0, 0)),
                  pl.BlockSpec((D, D), lambda i, j: (0, j)),
                  pl.BlockSpec((D, LANES), lambda i, j: (0, 0))],
        out_specs=[pl.BlockSpec((tm, D), lambda i, j: (i, j)),
                   pl.BlockSpec((tm, D), lambda i, j: (i, 0)),
                   pl.BlockSpec((tm, LANES), lambda i, j: (i, 0))],
        out_shape=[jax.ShapeDtypeStruct((t_rows, w_main.shape[1]), BF16),
                   jax.ShapeDtypeStruct((t_rows, D), BF16),
                   jax.ShapeDtypeStruct((t_rows, LANES), F32)],
        compiler_params=_params(("parallel", "arbitrary"), 48),
    )(h, g_pre, w_main, w_lr)


def _conv_parts(p_ref, w_ref):
    cb = p_ref[:, 0:128].astype(F32)
    cc = p_ref[:, 128:256].astype(F32)
    cx = p_ref[:, 256:384].astype(F32)
    cz = p_ref[:, 384:512].astype(F32)
    rows = cb.shape[0]
    w = w_ref[...]
    p = cc * cx
    conv = pltpu.roll(p, 1, 0) * w[0:1] + p * w[1:2] + pltpu.roll(p, rows - 1, 0) * w[2:3]
    sz = _sigmoid(cz)
    return cb, cc, cx, cz, p, conv, sz, w


def _conv_fwd(proj, conv_w, n_seq, lf):
    def body(p_ref, w_ref, y_ref):
        cb, _, _, cz, _, conv, sz, _ = _conv_parts(p_ref, w_ref)
        y_ref[...] = (cb * conv * (cz * sz)).astype(BF16)

    return pl.pallas_call(
        body, name="conv_fwd", grid=(n_seq, N_CONV_TILES),
        in_specs=[pl.BlockSpec((lf, 512), lambda b, j: (b, j)),
                  pl.BlockSpec((3, 128), lambda b, j: (0, j))],
        out_specs=pl.BlockSpec((lf, 128), lambda b, j: (b, j)),
        out_shape=jax.ShapeDtypeStruct((n_seq * lf, D), BF16),
        compiler_params=_params(("parallel", "parallel"), 48),
    )(proj, conv_w)


def _conv_bwd(proj, conv_w, dyc, n_seq, lf):
    def body(p_ref, w_ref, dy_ref, dp_ref, dw_ref):
        cb, cc, cx, cz, p, conv, sz, w = _conv_parts(p_ref, w_ref)
        rows = cb.shape[0]
        dy = dy_ref[...].astype(F32)
        silu = cz * sz
        dcb = dy * conv * silu
        dconv = dy * cb * silu
        dcz = dy * cb * conv * (sz * (1.0 + cz * (1.0 - sz)))
        d_next = pltpu.roll(dconv, rows - 1, 0)
        d_prev = pltpu.roll(dconv, 1, 0)
        dp = d_next * w[0:1] + dconv * w[1:2] + d_prev * w[2:3]
        dp_ref[:, 0:128] = dcb.astype(BF16)
        dp_ref[:, 128:256] = (dp * cx).astype(BF16)
        dp_ref[:, 256:384] = (dp * cc).astype(BF16)
        dp_ref[:, 384:512] = dcz.astype(BF16)
        dw_ref[0:1, :] = jnp.sum(dconv * pltpu.roll(p, 1, 0), axis=0, keepdims=True)
        dw_ref[1:2, :] = jnp.sum(dconv * p, axis=0, keepdims=True)
        dw_ref[2:3, :] = jnp.sum(dconv * pltpu.roll(p, rows - 1, 0), axis=0, keepdims=True)

    return pl.pallas_call(
        body, name="conv_bwd", grid=(n_seq, N_CONV_TILES),
        in_specs=[pl.BlockSpec((lf, 512), lambda b, j: (b, j)),
                  pl.BlockSpec((3, 128), lambda b, j: (0, j)),
                  pl.BlockSpec((lf, 128), lambda b, j: (b, j))],
        out_specs=[pl.BlockSpec((lf, 512), lambda b, j: (b, j)),
                   pl.BlockSpec((None, 3, 128), lambda b, j: (b, 0, j))],
        out_shape=[jax.ShapeDtypeStruct((n_seq * lf, W_CONV), BF16),
                   jax.ShapeDtypeStruct((n_seq, 3, D), F32)],
        compiler_params=_params(("parallel", "parallel"), 48),
    )(proj, conv_w, dyc)


def _chunk_unroll(n_chunks):
    for u in (11, 3):
        if n_chunks % u == 0:
            return u
    return 1


def _chunk_masks():
    row = lax.broadcasted_iota(jnp.int32, (CHUNK, CHUNK), 0)
    col = lax.broadcasted_iota(jnp.int32, (CHUNK, CHUNK), 1)
    lower = col <= row
    upper = col >= row
    strict_upper = col > row
    return lower, upper, strict_upper


def _gla_gates(lr_bf, wg_ref, bg_ref, lf):
    z = _dot(lr_bf, wg_ref[...]) + bg_ref[...]
    valid = lax.broadcasted_iota(jnp.int32, (lf, HEAD_K), 0) >= PAD_FRONT
    return z, valid


def _gla_states(direction, n_chunks, qkv_ref, g_s, b_s, st_s, tri):
    def local(c, carry):
        rows = pl.ds(pl.multiple_of(c * CHUNK, CHUNK), CHUNK)
        b = _tri_dot(tri, g_s[rows, :])
        b_s[rows, :] = b
        b_end = b[CHUNK - 1:CHUNK, :] if direction == 0 else b[0:1, :]
        k = qkv_ref[rows, 128:256].astype(F32)
        v = qkv_ref[rows, 256:512]
        k_dec = (k * jnp.exp(b_end - b)).astype(BF16)
        st_s[c] = _dot_tn(v, k_dec)
        return carry

    lax.fori_loop(0, n_chunks, local, 0, unroll=_chunk_unroll(n_chunks))

    def scan(i, state):
        c = i if direction == 0 else n_chunks - 1 - i
        decay = _chunk_decay(direction, c, b_s)
        update = st_s[c]
        st_s[c] = state
        return state * decay + update

    lax.fori_loop(0, n_chunks, scan, jnp.zeros((HEAD_V, HEAD_K), F32))


def _chunk_decay(direction, c, b_s):
    if direction == 0:
        grp = b_s[pl.ds(pl.multiple_of(c * CHUNK + CHUNK - 8, 8), 8), :]
        return jnp.exp(grp[7:8, :])
    grp = b_s[pl.ds(pl.multiple_of(c * CHUNK, 8), 8), :]
    return jnp.exp(grp[0:1, :])


def _gla_fwd(proj, lr, wgf, wgb, bgf, bgb, n_seq, lf):
    n_chunks = lf // CHUNK
    scale = HEAD_K ** -0.5

    def body(qkv_ref, lr_ref, wgf_ref, wgb_ref, bgf_ref, bgb_ref, o_ref, g_s, b_s, st_s):
        lower, upper, strict_upper = _chunk_masks()
        lr_bf = lr_ref[...].astype(BF16)
        for direction in (0, 1):
            wg_ref, bg_ref = ((wgf_ref, bgf_ref), (wgb_ref, bgb_ref))[direction]
            z, valid = _gla_gates(lr_bf, wg_ref, bg_ref, lf)
            g_s[...] = jnp.where(valid, _log_sigmoid(z) / GATE_NORM, 0.0)
            tri = (lower if direction == 0 else upper).astype(BF16)
            smask = lower if direction == 0 else strict_upper
            _gla_states(direction, n_chunks, qkv_ref, g_s, b_s, st_s, tri)

            def out(c, carry):
                rows = pl.ds(pl.multiple_of(c * CHUNK, CHUNK), CHUNK)
                b = b_s[rows, :]
                q = qkv_ref[rows, 0:128].astype(F32) * scale
                k = qkv_ref[rows, 128:256].astype(F32)
                v = qkv_ref[rows, 256:512]
                q_in = (q * jnp.exp(b)).astype(BF16)
                k_in = (k * jnp.exp(-b)).astype(BF16)
                s = jnp.where(smask, _dot_nt(q_in, k_in), 0.0).astype(BF16)
                o = _dot(s, v) + _dot_nt(q_in, st_s[c].astype(BF16))
                if direction == 0:
                    o_ref[rows, :] = o
                else:
                    o_ref[rows, :] = o_ref[rows, :] + o
                return carry

            lax.fori_loop(0, n_chunks, out, 0, unroll=_chunk_unroll(n_chunks))

    return pl.pallas_call(
        body, name="gla_fwd", grid=(n_seq, N_HEADS),
        in_specs=[pl.BlockSpec((lf, 512), lambda b, h: (b, N_CONV_TILES + h)),
                  pl.BlockSpec((lf, LANES), lambda b, h: (b, 0)),
                  pl.BlockSpec((None, LANES, HEAD_K), lambda b, h: (h, 0, 0)),
                  pl.BlockSpec((None, LANES, HEAD_K), lambda b, h: (h, 0, 0)),
                  pl.BlockSpec((None, 1, HEAD_K), lambda b, h: (h, 0, 0)),
                  pl.BlockSpec((None, 1, HEAD_K), lambda b, h: (h, 0, 0))],
        out_specs=pl.BlockSpec((lf, HEAD_V), lambda b, h: (b, h)),
        out_shape=jax.ShapeDtypeStruct((n_seq * lf, D), F32),
        scratch_shapes=[pltpu.VMEM((lf, HEAD_K), F32), pltpu.VMEM((lf, HEAD_K), F32),
                        pltpu.VMEM((n_chunks, HEAD_V, HEAD_K), F32)],
        compiler_params=_params(("parallel", "parallel"), 48),
    )(proj, lr, wgf, wgb, bgf, bgb)


def _gla_bwd(proj, lr, d_o, wgf, wgb, bgf, bgb, n_seq, lf):
    n_chunks = lf // CHUNK
    scale = HEAD_K ** -0.5

    def body(qkv_ref, lr_ref, do_ref, wgf_ref, wgb_ref, bgf_ref, bgb_ref,
             dqkv_ref, dlr_ref, dwgf_ref, dwgb_ref, dbg_ref,
             g_s, b_s, fac_s, dg_s, st_s, dst_s, acc_s):
        lower, upper, strict_upper = _chunk_masks()
        lr_bf = lr_ref[...].astype(BF16)
        acc_s[...] = jnp.zeros_like(acc_s)
        dlr = jnp.zeros((lf, LANES), F32)
        for direction in (0, 1):
            wg_ref, bg_ref = ((wgf_ref, bgf_ref), (wgb_ref, bgb_ref))[direction]
            z, valid = _gla_gates(lr_bf, wg_ref, bg_ref, lf)
            g_s[...] = jnp.where(valid, _log_sigmoid(z) / GATE_NORM, 0.0)
            fac_s[...] = jnp.where(valid, _sigmoid(-z) / GATE_NORM, 0.0)
            tri = (lower if direction == 0 else upper).astype(BF16)
            tri_t = (upper if direction == 0 else lower).astype(BF16)
            smask = lower if direction == 0 else strict_upper
            end_row = CHUNK - 1 if direction == 0 else 0
            _gla_states(direction, n_chunks, qkv_ref, g_s, b_s, st_s, tri)

            def state_grad_local(c, carry):
                rows = pl.ds(pl.multiple_of(c * CHUNK, CHUNK), CHUNK)
                q = qkv_ref[rows, 0:128].astype(F32) * scale
                q_in = (q * jnp.exp(b_s[rows, :])).astype(BF16)
                dst_s[c] = _dot_tn(do_ref[rows, :], q_in)
                return carry

            lax.fori_loop(0, n_chunks, state_grad_local, 0, unroll=_chunk_unroll(n_chunks))

            def state_grad_scan(i, grad):
                c = n_chunks - 1 - i if direction == 0 else i
                decay = _chunk_decay(direction, c, b_s)
                local = dst_s[c]
                dst_s[c] = grad
                return local + grad * decay

            lax.fori_loop(0, n_chunks, state_grad_scan, jnp.zeros((HEAD_V, HEAD_K), F32))

            def chunk_grads(c, carry):
                rows = pl.ds(pl.multiple_of(c * CHUNK, CHUNK), CHUNK)
                b = b_s[rows, :]
                b_end = b[end_row:end_row + 1, :]
                q = qkv_ref[rows, 0:128].astype(F32) * scale
                k = qkv_ref[rows, 128:256].astype(F32)
                v = qkv_ref[rows, 256:512]
                d_out = do_ref[rows, :]
                e_pos = jnp.exp(b)
                e_neg = jnp.exp(-b)
                e_end = jnp.exp(b_end - b)
                q_in = q * e_pos
                k_in = k * e_neg
                k_dec = k * e_end
                q_in_bf = q_in.astype(BF16)
                k_in_bf = k_in.astype(BF16)
                state = st_s[c]
                d_state = dst_s[c]
                state_bf = state.astype(BF16)
                d_state_bf = d_state.astype(BF16)
                s = jnp.where(smask, _dot_nt(q_in_bf, k_in_bf), 0.0).astype(BF16)
                ds = jnp.where(smask, _dot_nt(d_out, v), 0.0).astype(BF16)
                dv = _dot_tn(s, d_out) + _dot_nt(k_dec.astype(BF16), d_state_bf)
                dq_in = _dot(ds, k_in_bf) + _dot(d_out, state_bf)
                dk_in = _dot_tn(ds, q_in_bf)
                dk_dec = _dot(v, d_state_bf)
                acc_s[rows, 0:128] = acc_s[rows, 0:128] + dq_in * e_pos * scale
                acc_s[rows, 128:256] = acc_s[rows, 128:256] + dk_in * e_neg + dk_dec * e_end
                acc_s[rows, 256:512] = acc_s[rows, 256:512] + dv
                dkk = dk_dec * k_dec
                db = dq_in * q_in - dk_in * k_in - dkk
                d_decay = jnp.sum(d_state * state, axis=0, keepdims=True)
                db_end = jnp.sum(dkk, axis=0, keepdims=True) + d_decay * jnp.exp(b_end)
                at_end = lax.broadcasted_iota(jnp.int32, (CHUNK, HEAD_K), 0) == end_row
                db = db + jnp.where(at_end, db_end, 0.0)
                dg_s[rows, :] = _tri_dot(tri_t, db)
                return carry

            lax.fori_loop(0, n_chunks, chunk_grads, 0, unroll=_chunk_unroll(n_chunks))

            dz = dg_s[...] * fac_s[...]
            dz_bf = dz.astype(BF16)
            dbg_ref[direction:direction + 1, :] = jnp.sum(dz, axis=0, keepdims=True)
            (dwgf_ref, dwgb_ref)[direction][...] = _dot_tn(lr_bf, dz_bf)
            dlr = dlr + _dot_nt(dz_bf, wg_ref[...])

        dqkv_ref[...] = acc_s[...].astype(BF16)

        @pl.when(pl.program_id(1) == 0)
        def _():
            dlr_ref[...] = dlr

        @pl.when(pl.program_id(1) != 0)
        def _():
            dlr_ref[...] = dlr_ref[...] + dlr

    gate_w = pl.BlockSpec((None, LANES, HEAD_K), lambda b, h: (h, 0, 0))
    gate_b = pl.BlockSpec((None, 1, HEAD_K), lambda b, h: (h, 0, 0))
    return pl.pallas_call(
        body, name="gla_bwd", grid=(n_seq, N_HEADS),
        in_specs=[pl.BlockSpec((lf, 512), lambda b, h: (b, N_CONV_TILES + h)),
                  pl.BlockSpec((lf, LANES), lambda b, h: (b, 0)),
                  pl.BlockSpec((lf, HEAD_V), lambda b, h: (b, h)),
                  gate_w, gate_w, gate_b, gate_b],
        out_specs=[pl.BlockSpec((lf, 512), lambda b, h: (b, h)),
                   pl.BlockSpec((lf, LANES), lambda b, h: (b, 0)),
                   pl.BlockSpec((None, None, LANES, HEAD_K), lambda b, h: (b, h, 0, 0)),
                   pl.BlockSpec((None, None, LANES, HEAD_K), lambda b, h: (b, h, 0, 0)),
                   pl.BlockSpec((None, None, 2, HEAD_K), lambda b, h: (b, h, 0, 0))],
        out_shape=[jax.ShapeDtypeStruct((n_seq * lf, W_GLA), BF16),
                   jax.ShapeDtypeStruct((n_seq * lf, LANES), F32),
                   jax.ShapeDtypeStruct((n_seq, N_HEADS, LANES, HEAD_K), F32),
                   jax.ShapeDtypeStruct((n_seq, N_HEADS, LANES, HEAD_K), F32),
                   jax.ShapeDtypeStruct((n_seq, N_HEADS, 2, HEAD_K), F32)],
        scratch_shapes=[pltpu.VMEM((lf, HEAD_K), F32), pltpu.VMEM((lf, HEAD_K), F32),
                        pltpu.VMEM((lf, HEAD_K), F32), pltpu.VMEM((lf, HEAD_K), F32),
                        pltpu.VMEM((n_chunks, HEAD_V, HEAD_K), F32),
                        pltpu.VMEM((n_chunks, HEAD_V, HEAD_K), F32),
                        pltpu.VMEM((lf, 512), F32)],
        compiler_params=_params(("parallel", "arbitrary"), 56),
    )(proj, lr, d_o, wgf, wgb, bgf, bgb)


def _tail(h, tgt, yc, o, proj, w3, gamma, g_post, lf):
    t_rows = h.shape[0]
    tm = _pick_tile(t_rows, 256, CHUNK)
    n_chunks = lf // CHUNK
    per_tile = tm // CHUNK

    def body(h_ref, tgt_ref, yc_ref, o_ref, r_ref, ma_ref, mb_ref, w_hbm, gamma_ref, gpost_ref,
             dres_ref, yg_ref, merged_ref, dout_ref, dpc_ref, dpg_ref, dyc_ref, do_ref, dtail_ref,
             loss_ref, dgpost_ref, dgamma_ref, w_s, w_sem):
        i = pl.program_id(0)

        @pl.when(i == 0)
        def _():
            cp = pltpu.make_async_copy(w_hbm, w_s, w_sem)
            cp.start()
            cp.wait()
            loss_ref[...] = jnp.zeros_like(loss_ref)
            dgpost_ref[...] = jnp.zeros_like(dgpost_ref)
            dgamma_ref[...] = jnp.zeros_like(dgamma_ref)

        gamma = gamma_ref[...]
        o = o_ref[...]
        r = r_ref[...].astype(F32)
        sr = _sigmoid(r)
        silu_r = r * sr
        n_parts, rstd_parts = [], []
        for hd in range(N_HEADS):
            oh = o[:, hd * HEAD_V:(hd + 1) * HEAD_V]
            rstd = lax.rsqrt(jnp.mean(oh * oh, axis=-1, keepdims=True) + EPS)
            n_parts.append(oh * rstd)
            rstd_parts.append(rstd)
        n = jnp.concatenate(n_parts, axis=-1)
        gamma_t = jnp.concatenate([gamma] * N_HEADS, axis=-1)
        yg = n * gamma_t * silu_r
        yg_bf = yg.astype(BF16)
        yg_ref[...] = yg_bf
        yc = yc_ref[...]
        pc = _dot(yc, w_s[0])
        pg = _dot(yg_bf, w_s[1])
        sa = _sigmoid(ma_ref[...].astype(F32))
        sb = _sigmoid(mb_ref[...].astype(F32))
        merged = (sa * pc + sb * pg).astype(BF16)
        merged_ref[...] = merged
        out = _dot(merged, w_s[2])
        rstd2 = lax.rsqrt(jnp.mean(out * out, axis=-1, keepdims=True) + EPS)
        nn = out * rstd2
        gpost = gpost_ref[...]
        y = h_ref[...] + nn * gpost

        rowi = lax.broadcasted_iota(jnp.int32, (tm, 1), 0)
        keep = jnp.zeros((tm, 1), F32)
        for kk in range(per_tile):
            is_tok = ((i * per_tile + kk) % n_chunks) != 0
            f = jnp.where(is_tok, 1.0, 0.0)
            keep = jnp.where((rowi >= kk * CHUNK) & (rowi < (kk + 1) * CHUNK), f, keep)
        diff = (y - tgt_ref[...]) * keep
        loss_ref[...] += jnp.sum(diff * diff) * (0.5 / D)
        dy = diff * (1.0 / D)
        dres_ref[...] = dy
        dgpost_ref[...] += jnp.sum(dy * nn, axis=0, keepdims=True)
        dn = dy * gpost
        dout = (rstd2 * (dn - nn * jnp.mean(dn * nn, axis=-1, keepdims=True))).astype(BF16)
        dout_ref[...] = dout
        dmerged = _dot_nt(dout, w_s[2])
        dpc = (dmerged * sa).astype(BF16)
        dpg = (dmerged * sb).astype(BF16)
        dpc_ref[...] = dpc
        dpg_ref[...] = dpg
        dtail_ref[:, D:2 * D] = (dmerged * pc * (sa * (1.0 - sa))).astype(BF16)
        dtail_ref[:, 2 * D:3 * D] = (dmerged * pg * (sb * (1.0 - sb))).astype(BF16)
        dyc_ref[...] = _dot_nt(dpc, w_s[0]).astype(BF16)
        dyg = _dot_nt(dpg, w_s[1])
        dtail_ref[:, 0:D] = (dyg * n * gamma_t * (sr * (1.0 + r * (1.0 - sr)))).astype(BF16)
        dgam_full = jnp.sum(dyg * n * silu_r, axis=0, keepdims=True)
        dgam = dgam_full[:, 0:HEAD_V]
        for hd in range(1, N_HEADS):
            dgam = dgam + dgam_full[:, hd * HEAD_V:(hd + 1) * HEAD_V]
        dgamma_ref[...] += dgam
        dng = dyg * gamma_t * silu_r
        do_parts = []
        for hd in range(N_HEADS):
            sl = slice(hd * HEAD_V, (hd + 1) * HEAD_V)
            dnh = dng[:, sl]
            nh = n_parts[hd]
            do_parts.append(rstd_parts[hd] * (dnh - nh * jnp.mean(dnh * nh, axis=-1, keepdims=True)))
        do_ref[...] = jnp.concatenate(do_parts, axis=-1).astype(BF16)

    row = lambda c: pl.BlockSpec((tm, D), lambda i: (i, c))
    const = lambda shape: pl.BlockSpec(shape, lambda i: (0, 0))
    act = jax.ShapeDtypeStruct((t_rows, D), BF16)
    return pl.pallas_call(
        body, name="tail", grid=(t_rows // tm,),
        in_specs=[row(0), row(0), row(0), row(0), row(6), row(7), row(8),
                  pl.BlockSpec(memory_space=pl.ANY), const((1, HEAD_V)), const((1, D))],
        out_specs=[row(0)] * 8 + [pl.BlockSpec((tm, W_TAIL), lambda i: (i, 0)),
                                  const((8, LANES)), const((1, D)), const((1, HEAD_V))],
        out_shape=[jax.ShapeDtypeStruct((t_rows, D), F32)] + [act] * 7
                  + [jax.ShapeDtypeStruct((t_rows, W_TAIL), BF16),
                     jax.ShapeDtypeStruct((8, LANES), F32),
                     jax.ShapeDtypeStruct((1, D), F32),
                     jax.ShapeDtypeStruct((1, HEAD_V), F32)],
        scratch_shapes=[pltpu.VMEM((3, D, D), BF16), pltpu.SemaphoreType.DMA],
        compiler_params=_params(("arbitrary",), 56),
    )(h, tgt, yc, o, proj, proj, proj, w3, gamma, g_post)


def _wgrad(a, b, name, out_dtype=BF16):
    t_rows, m = a.shape
    n = b.shape[1]
    tn = D if n % D == 0 else n
    tk = _pick_tile(t_rows, 528, 16)
    n_k = t_rows // tk

    def body(a_ref, b_ref, o_ref, acc):
        k = pl.program_id(1)

        @pl.when(k == 0)
        def _():
            acc[...] = jnp.zeros_like(acc)

        acc[...] += _dot_tn(a_ref[...].astype(BF16), b_ref[...].astype(BF16))

        @pl.when(k == n_k - 1)
        def _():
            o_ref[...] = acc[...].astype(out_dtype)

    return pl.pallas_call(
        body, name=name, grid=(n // tn, n_k),
        in_specs=[pl.BlockSpec((tk, m), lambda j, k: (k, 0)),
                  pl.BlockSpec((tk, tn), lambda j, k: (k, j))],
        out_specs=pl.BlockSpec((m, tn), lambda j, k: (0, j)),
        out_shape=jax.ShapeDtypeStruct((m, n), out_dtype),
        scratch_shapes=[pltpu.VMEM((m, tn), F32)],
        compiler_params=_params(("parallel", "arbitrary"), 48),
    )(a, b)


def _dgrad_in(dpc, dpg, dpt, dlr, w_main, w_lr, h, g_pre, dres):
    t_rows = h.shape[0]
    tm = _pick_tile(t_rows, 528, 16)
    n_c, n_g, n_t = W_CONV // D, W_GLA // D, W_TAIL // D
    n_col = n_c + n_g + n_t

    def body(dpc_ref, dpg_ref, dpt_ref, dlr_ref, w_ref, wlr_ref, h_ref, g_ref, dres_ref,
             dh_ref, dg_ref, acc):
        i, j = pl.program_id(0), pl.program_id(1)

        @pl.when(j == 0)
        def _():
            acc[...] = _dot_nt(dlr_ref[...].astype(BF16), wlr_ref[...])

        @pl.when(j < n_c)
        def _():
            acc[...] += _dot_nt(dpc_ref[...], w_ref[...])

        @pl.when((j >= n_c) & (j < n_c + n_g))
        def _():
            acc[...] += _dot_nt(dpg_ref[...], w_ref[...])

        @pl.when(j >= n_c + n_g)
        def _():
            acc[...] += _dot_nt(dpt_ref[...], w_ref[...])

        @pl.when((i == 0) & (j == 0))
        def _():
            dg_ref[...] = jnp.zeros_like(dg_ref)

        @pl.when(j == n_col - 1)
        def _():
            hh = h_ref[...]
            rstd = lax.rsqrt(jnp.mean(hh * hh, axis=-1, keepdims=True) + EPS)
            xhat = hh * rstd
            du = acc[...]
            dg_ref[...] += jnp.sum(du * xhat, axis=0, keepdims=True)
            dx = du * g_ref[...]
            dh_ref[...] = rstd * (dx - xhat * jnp.mean(dx * xhat, axis=-1, keepdims=True)) + dres_ref[...]

    clamp = lambda v, lo, hi: jnp.minimum(jnp.maximum(v, lo), hi)
    return pl.pallas_call(
        body, name="dgrad_in", grid=(t_rows // tm, n_col),
        in_specs=[pl.BlockSpec((tm, D), lambda i, j: (i, clamp(j, 0, n_c - 1))),
                  pl.BlockSpec((tm, D), lambda i, j: (i, clamp(j - n_c, 0, n_g - 1))),
                  pl.BlockSpec((tm, D), lambda i, j: (i, clamp(j - n_c - n_g, 0, n_t - 1))),
                  pl.BlockSpec((tm, LANES), lambda i, j: (i, 0)),
                  pl.BlockSpec((D, D), lambda i, j: (0, j)),
                  pl.BlockSpec((D, LANES), lambda i, j: (0, 0)),
                  pl.BlockSpec((tm, D), lambda i, j: (i, 0)),
                  pl.BlockSpec((1, D), lambda i, j: (0, 0)),
                  pl.BlockSpec((tm, D), lambda i, j: (i, 0))],
        out_specs=[pl.BlockSpec((tm, D), lambda i, j: (i, 0)),
                   pl.BlockSpec((1, D), lambda i, j: (0, 0))],
        out_shape=[jax.ShapeDtypeStruct((t_rows, D), F32), jax.ShapeDtypeStruct((1, D), F32)],
        scratch_shapes=[pltpu.VMEM((tm, D), F32)],
        compiler_params=_params(("arbitrary", "arbitrary"), 48),
    )(dpc, dpg, dpt, dlr, w_main, w_lr, h, g_pre, dres)


OFF_CB, OFF_CC, OFF_CX, OFF_CZ = 0, 1024, 2048, 3072
OFF_Q, OFF_K, OFF_V, OFF_R = 4096, 4608, 5120, 6144
OFF_LR, OFF_MA, OFF_MB = 7168, 7200, 8224


def _main_columns(w):
    rows = w.shape[0]
    conv = w[:, 0:W_CONV].reshape(rows, 4, N_CONV_TILES, 128).transpose(0, 2, 1, 3).reshape(rows, W_CONV)
    q = w[:, OFF_Q:OFF_K].reshape(rows, N_HEADS, HEAD_K)
    k = w[:, OFF_K:OFF_V].reshape(rows, N_HEADS, HEAD_K)
    v = w[:, OFF_V:OFF_R].reshape(rows, N_HEADS, HEAD_V)
    gla = jnp.concatenate([q, k, v], axis=2).reshape(rows, W_GLA)
    return jnp.concatenate([conv, gla, w[:, OFF_R:OFF_R + D], w[:, OFF_MA:OFF_MA + 2 * D]], axis=1)


def _reference_columns(g_conv, g_gla, g_tail, g_lr):
    rows = g_conv.shape[0]
    conv = g_conv.reshape(rows, N_CONV_TILES, 4, 128).transpose(0, 2, 1, 3).reshape(rows, W_CONV)
    gla = g_gla.reshape(rows, N_HEADS, 512)
    q = gla[:, :, 0:128].reshape(rows, N_HEADS * HEAD_K)
    k = gla[:, :, 128:256].reshape(rows, N_HEADS * HEAD_K)
    v = gla[:, :, 256:512].reshape(rows, N_HEADS * HEAD_V)
    return jnp.concatenate([conv, q, k, v, g_tail[:, 0:D], g_lr[:, 0:2 * RANK], g_tail[:, D:3 * D]], axis=1)


def _pack(arrs, rows):
    flat = jnp.concatenate([a.reshape(-1) for a in arrs])
    return jnp.pad(flat, (0, rows * LANES - flat.shape[0])).reshape(rows, LANES)


def _unpack(packed, shapes):
    flat = packed.reshape(-1)
    out, pos = [], 0
    for s in shapes:
        size = 1
        for d in s:
            size *= d
        out.append(flat[pos:pos + size].reshape(s))
        pos += size
    return out


def _rows_for(shapes):
    total = 0
    for s in shapes:
        size = 1
        for d in s:
            size *= d
        total += size
    return -(-total // (8 * LANES)) * 8


def kernel(x, meta_tokens, norm_pre, w_in, conv_w, w_gate_fwd, b_gate_fwd, w_gate_bwd, b_gate_bwd, gla_norm, w_out_conv, w_out_gla, w_merge_out, norm_post, loss_target, m_meta_tokens, m_norm_pre, m_w_in, m_conv_w, m_w_gate_fwd, m_b_gate_fwd, m_w_gate_bwd, m_b_gate_bwd, m_gla_norm, m_w_out_conv, m_w_out_gla, m_w_merge_out, m_norm_post, v_meta_tokens, v_norm_pre, v_w_in, v_conv_w, v_w_gate_fwd, v_b_gate_fwd, v_w_gate_bwd, v_b_gate_bwd, v_gla_norm, v_w_out_conv, v_w_out_gla, v_w_merge_out, v_norm_post):
    n_seq, seq, _ = x.shape
    lf = CHUNK + seq
    t_rows = n_seq * lf
    shard = 2 * lax.axis_index("x") + lax.axis_index("y")

    w_in_bf = _cast_bf16(w_in[0], "cast_w_in")
    w_out_bf = _cast_bf16(jnp.concatenate([w_out_conv[0], w_out_gla[0], w_merge_out[0]], axis=0), "cast_w_out")
    small_shapes = [(N_META, D // 4), (3, D // 4), (RANK, HEAD_K), (RANK, HEAD_K)]
    small = _pack([meta_tokens, conv_w[0], w_gate_fwd[0], w_gate_bwd[0]], _rows_for(small_shapes))
    w_in_all, w_out_all, small_all = _exchange(
        "gather_weights", [w_in_bf, w_out_bf, small], PLANE_FLIPS, (2, 1, 0), "gather")

    w_full = jnp.transpose(w_in_all, (1, 0, 2)).reshape(D, N_IN)
    w_main = _main_columns(w_full)
    w_lr = jnp.pad(w_full[:, OFF_LR:OFF_LR + 2 * RANK], ((0, 0), (0, LANES - 2 * RANK)))
    w3 = jnp.transpose(w_out_all.reshape(4, 3, D // 4, D), (1, 0, 2, 3)).reshape(3, D, D)
    smalls = [_unpack(small_all[s], small_shapes) for s in range(4)]
    meta_full = jnp.concatenate([smalls[s][0] for s in range(4)], axis=1)
    conv_full = jnp.concatenate([smalls[s][1] for s in range(4)], axis=1)
    wgf = jnp.stack([jnp.pad(smalls[s][2], ((0, LANES - RANK), (0, 0))) for s in range(4)]).astype(BF16)
    wgb = jnp.stack([jnp.pad(smalls[s][3], ((RANK, LANES - 2 * RANK), (0, 0))) for s in range(4)]).astype(BF16)
    bgf = b_gate_fwd.reshape(N_HEADS, 1, HEAD_K)
    bgb = b_gate_bwd.reshape(N_HEADS, 1, HEAD_K)

    head = jnp.concatenate([jnp.zeros((PAD_FRONT, D), F32), meta_full], axis=0)
    h = jnp.concatenate([jnp.broadcast_to(head[None], (n_seq, CHUNK, D)), x], axis=1).reshape(t_rows, D)
    tgt = jnp.pad(loss_target, ((0, 0), (CHUNK, 0), (0, 0))).reshape(t_rows, D)

    proj, u, lr = _in_proj(h, norm_pre, w_main, w_lr)
    yc = _conv_fwd(proj, conv_full, n_seq, lf)
    o = _gla_fwd(proj, lr, wgf, wgb, bgf, bgb, n_seq, lf)
    (dres, yg, merged, dout, dpc_out, dpg_out, dyc, d_o, dtail, loss_acc, d_gpost, d_gamma) = _tail(
        h, tgt, yc, o, proj, w3, gla_norm, norm_post, lf)
    g_w_oc = _wgrad(yc, dpc_out, "wgrad_out_conv")
    g_w_og = _wgrad(yg, dpg_out, "wgrad_out_gla")
    g_w_mo = _wgrad(merged, dout, "wgrad_merge_out")
    dgla, dlr, dwgf_p, dwgb_p, dbg_p = _gla_bwd(proj, lr, d_o, wgf, wgb, bgf, bgb, n_seq, lf)
    dconv, dconvw_p = _conv_bwd(proj, conv_full, dyc, n_seq, lf)
    dh, d_gpre = _dgrad_in(dconv, dgla, dtail, dlr, w_main, w_lr, h, norm_pre, dres)
    g_conv = _wgrad(u, dconv, "wgrad_in_conv")
    g_gla = _wgrad(u, dgla, "wgrad_in_gla")
    g_tail = _wgrad(u, dtail, "wgrad_in_tail")
    g_lr = _wgrad(u, dlr, "wgrad_in_lr")

    dh3 = dh.reshape(n_seq, lf, D)
    grad_x = dh3[:, CHUNK:, :]

    d_meta = jnp.sum(dh3[:, PAD_FRONT:CHUNK, :], axis=0)
    d_convw = jnp.sum(dconvw_p, axis=0)
    d_wgf = jnp.transpose(jnp.sum(dwgf_p, axis=0)[:, 0:RANK, :], (1, 0, 2)).reshape(RANK, N_HEADS * HEAD_K)
    d_wgb = jnp.transpose(jnp.sum(dwgb_p, axis=0)[:, RANK:2 * RANK, :], (1, 0, 2)).reshape(RANK, N_HEADS * HEAD_K)
    d_bg = jnp.sum(dbg_p, axis=0)
    d_bgf = d_bg[:, 0, :].reshape(1, N_HEADS * HEAD_K)
    d_bgb = d_bg[:, 1, :].reshape(1, N_HEADS * HEAD_K)
    part_shapes = [(N_META, D), (3, D), (RANK, 512), (RANK, 512), (1, D), (1, 512), (1, 512), (1, HEAD_V),
                   (1, D), (1, LANES)]
    parts = _pack([d_meta, d_convw, d_wgf, d_wgb, d_gpre, d_bgf, d_bgb, d_gamma, d_gpost, loss_acc[0:1, :]],
                  _rows_for(part_shapes))
    (parts_all,) = _exchange("gather_small_grads", [parts], ALL_FLIPS, (4, 2, 1), "gather")
    (g_meta, g_convw, g_wgf, g_wgb, g_npre, g_bgf, g_bgb, g_gnorm, g_npost, loss_row) = _unpack(
        _sum_slots(parts_all, "sum_small_grads"), part_shapes)
    loss = loss_row[0, 0]

    def col_shard(a, width):
        return lax.dynamic_slice_in_dim(a, shard * width, width, axis=a.ndim - 1)

    upd_shapes = [(N_META, D // 4), (3, D // 4), (RANK, HEAD_K), (RANK, HEAD_K), (1, D), (1, 512), (1, 512),
                  (1, HEAD_V), (1, D)]
    upd_rows = _rows_for(upd_shapes)
    small_w = _pack([meta_tokens, conv_w[0], w_gate_fwd[0], w_gate_bwd[0], norm_pre, b_gate_fwd, b_gate_bwd,
                     gla_norm, norm_post], upd_rows)
    small_g = _pack([col_shard(g_meta, D // 4), col_shard(g_convw, D // 4), col_shard(g_wgf, HEAD_K),
                     col_shard(g_wgb, HEAD_K), g_npre, g_bgf, g_bgb, g_gnorm, g_npost], upd_rows)
    small_m = _pack([m_meta_tokens, m_conv_w[0], m_w_gate_fwd[0], m_w_gate_bwd[0], m_norm_pre, m_b_gate_fwd,
                     m_b_gate_bwd, m_gla_norm, m_norm_post], upd_rows)
    small_v = _pack([v_meta_tokens, v_conv_w[0], v_w_gate_fwd[0], v_w_gate_bwd[0], v_norm_pre, v_b_gate_fwd,
                     v_b_gate_bwd, v_gla_norm, v_norm_post], upd_rows)
    small_out = [_unpack(a, upd_shapes) for a in _adamw(small_w, [small_g], small_m, small_v, "adamw_small")]

    g_in_full = _reference_columns(g_conv, g_gla, g_tail, g_lr)
    g_in_slots = jnp.transpose(g_in_full.reshape(D, 4, SHARD_IN), (1, 0, 2))
    g_out_slots = jnp.concatenate([g.reshape(4, D // 4, D) for g in (g_w_oc, g_w_og, g_w_mo)], axis=1)
    got_in, got_out = _exchange("scatter_weight_grads", [g_in_slots, g_out_slots], PLANE_FLIPS, (2, 1, 0), "scatter")
    plane_in = _sum_slots(got_in, "sum_w_in_grads")
    plane_out = _sum_slots(got_out, "sum_w_out_grads")
    other_in, other_out = _exchange("swap_plane_sums", [plane_in, plane_out], SIBLING_FLIPS, (0, 0, 0), "swap")
    big_in = _adamw(w_in[0], [plane_in, other_in], m_w_in[0], v_w_in[0], "adamw_w_in")
    w_out_rows = jnp.concatenate([w_out_conv[0], w_out_gla[0], w_merge_out[0]], axis=0)
    m_out_rows = jnp.concatenate([m_w_out_conv[0], m_w_out_gla[0], m_w_merge_out[0]], axis=0)
    v_out_rows = jnp.concatenate([v_w_out_conv[0], v_w_out_gla[0], v_w_merge_out[0]], axis=0)
    big_out = _adamw(w_out_rows, [plane_out, other_out], m_out_rows, v_out_rows, "adamw_w_out")

    results = []
    for kind in range(4):
        sm = small_out[kind]
        w_in_part = big_in[kind][None]
        outs3 = big_out[kind].reshape(3, 1, D // 4, D)
        results.extend([
            sm[0], sm[4], w_in_part, sm[1][None], sm[2][None], sm[5], sm[3][None], sm[6], sm[7],
            outs3[0], outs3[1], outs3[2], sm[8]])
    return (loss, grad_x, *results)
```

```python
import functools

import jax
import jax.numpy as jnp
from jax import lax
from jax.experimental import pallas as pl
from jax.experimental.pallas import tpu as pltpu

F32 = jnp.float32
BF16 = jnp.bfloat16
MESH = pl.DeviceIdType.MESH

D = 1024
N_META = 16
CHUNK = 64
PAD_FRONT = CHUNK - N_META
N_HEADS = 4
HEAD_K = 128
HEAD_V = 256
RANK = 16
EPS = 1e-6
GATE_NORM = 16.0
N_IN = 9248
SHARD_IN = N_IN // 4
LANES = 128
N_CONV_TILES = 8
W_CONV = 4096
W_GLA = 2048
W_TAIL = 3072
N_MAIN = W_CONV + W_GLA + W_TAIL
MIB = 1024 * 1024

ADAM_LR = 0.001
ADAM_B1 = 0.9
ADAM_B2 = 0.999
ADAM_EPS = 1e-08
ADAM_WD = 0.01
ADAM_STEP = 10


def _params(sem=None, vmem_mib=None):
    return pltpu.CompilerParams(
        dimension_semantics=sem,
        vmem_limit_bytes=None if vmem_mib is None else vmem_mib * MIB)


def _pick_tile(n, target, mult):
    best = None
    for t in range(mult, min(n, target) + 1, mult):
        if n % t == 0:
            best = t
    return n if best is None else best


def _sigmoid(v):
    return 1.0 / (1.0 + jnp.exp(-v))


def _log_sigmoid(v):
    return jnp.minimum(v, 0.0) - jnp.log(1.0 + jnp.exp(-jnp.abs(v)))


def _dot(a, b):
    return jnp.dot(a, b, preferred_element_type=F32)


def _dot_nt(a, b):
    return lax.dot_general(a, b, (((1,), (1,)), ((), ())), preferred_element_type=F32)


def _dot_tn(a, b):
    return lax.dot_general(a, b, (((0,), (0,)), ((), ())), preferred_element_type=F32)


def _tri_dot(tri, v):
    hi = v.astype(BF16)
    lo = (v - hi.astype(F32)).astype(BF16)
    return _dot(tri, hi) + _dot(tri, lo)


PLANE_FLIPS = ((1, 0, 0), (0, 1, 0), (1, 1, 0))
ALL_FLIPS = tuple((m >> 2 & 1, m >> 1 & 1, m & 1) for m in range(1, 8))
SIBLING_FLIPS = ((0, 0, 1),)


def _exchange(name, arrs, flips, slot_weights, mode):
    n = len(arrs)
    n_slots = 1
    for w in slot_weights:
        n_slots += w
    if mode == "gather":
        out_shape = [jax.ShapeDtypeStruct((n_slots,) + a.shape, a.dtype) for a in arrs]
    else:
        out_shape = [jax.ShapeDtypeStruct(a.shape, a.dtype) for a in arrs]

    def body(*refs):
        ins, outs = refs[:n], refs[n:2 * n]
        send_sems, recv_sems, local_sems = refs[2 * n:]
        pos = (lax.axis_index("x"), lax.axis_index("y"), lax.axis_index("c"))

        def slot_of(p):
            return p[0] * slot_weights[0] + p[1] * slot_weights[1] + p[2] * slot_weights[2]

        peers = [tuple(1 - pos[a] if f[a] else pos[a] for a in range(3)) for f in flips]
        me = slot_of(pos)
        local = []
        sends = []
        for i in range(n):
            if mode != "swap":
                src = ins[i] if mode == "gather" else ins[i].at[me]
                cp = pltpu.make_async_copy(src, outs[i].at[me], local_sems.at[i])
                cp.start()
                local.append(cp)
            for k, peer in enumerate(peers):
                if mode == "gather":
                    src, dst = ins[i], outs[i].at[me]
                elif mode == "scatter":
                    src, dst = ins[i].at[slot_of(peer)], outs[i].at[me]
                else:
                    src, dst = ins[i], outs[i]
                cp = pltpu.make_async_remote_copy(
                    src_ref=src, dst_ref=dst, send_sem=send_sems.at[i, k], recv_sem=recv_sems.at[i, k],
                    device_id=peer, device_id_type=MESH)
                cp.start()
                sends.append(cp)
        for i in range(n):
            for k, peer in enumerate(peers):
                if mode == "gather":
                    src, dst = ins[i], outs[i].at[slot_of(peer)]
                elif mode == "scatter":
                    src, dst = ins[i].at[me], outs[i].at[slot_of(peer)]
                else:
                    src, dst = ins[i], outs[i]
                arrival = pltpu.make_async_remote_copy(
                    src_ref=src, dst_ref=dst, send_sem=send_sems.at[i, k], recv_sem=recv_sems.at[i, k],
                    device_id=peer, device_id_type=MESH)
                arrival.wait_recv()
        for cp in sends:
            cp.wait_send()
        for cp in local:
            cp.wait()

    hbm = pl.BlockSpec(memory_space=pl.ANY)
    outs = pl.pallas_call(
        body, name=name, out_shape=out_shape,
        in_specs=[hbm] * n, out_specs=[hbm] * n,
        scratch_shapes=[pltpu.SemaphoreType.DMA((n, len(flips))),
                        pltpu.SemaphoreType.DMA((n, len(flips))),
                        pltpu.SemaphoreType.DMA((n,))],
        compiler_params=pltpu.CompilerParams(has_side_effects=True),
    )(*arrs)
    return list(outs)


def _gather_via_sibling(name, arrs):
    n = len(arrs)
    out_shape = [jax.ShapeDtypeStruct((4,) + a.shape, a.dtype) for a in arrs]

    def body(*refs):
        ins, outs = refs[:n], refs[n:2 * n]
        send_sems, recv_sems, local_sems = refs[2 * n:]
        x, y, c = lax.axis_index("x"), lax.axis_index("y"), lax.axis_index("c")
        me = 2 * x + y
        chips = [(1 - x, y), (x, 1 - y), (1 - x, 1 - y)]
        sibling = (x, y, 1 - c)

        def half(ref, which):
            rows = ref.shape[0] // 2
            return ref.at[pl.ds(which * rows, rows)]

        def copy(src, dst, i, k, to):
            return pltpu.make_async_remote_copy(
                src_ref=src, dst_ref=dst, send_sem=send_sems.at[i, k], recv_sem=recv_sems.at[i, k],
                device_id=to, device_id_type=MESH)

        local, sends = [], []
        for i in range(n):
            cp = pltpu.make_async_copy(ins[i], outs[i].at[me], local_sems.at[i])
            cp.start()
            local.append(cp)
            for k, (px, py) in enumerate(chips):
                cp = copy(half(ins[i], c), half(outs[i].at[me], c), i, k, (px, py, c))
                cp.start()
                sends.append(cp)
        for k, (px, py) in enumerate(chips):
            slot = 2 * px + py
            for i in range(n):
                landed = half(outs[i].at[slot], c)
                copy(half(ins[i], c), landed, i, k, (px, py, c)).wait_recv()
                cp = copy(landed, landed, i, 3 + k, sibling)
                cp.start()
                sends.append(cp)
        for k, (px, py) in enumerate(chips):
            slot = 2 * px + py
            for i in range(n):
                passed = half(outs[i].at[slot], 1 - c)
                copy(passed, passed, i, 3 + k, sibling).wait_recv()
        for cp in sends:
            cp.wait_send()
        for cp in local:
            cp.wait()

    hbm = pl.BlockSpec(memory_space=pl.ANY)
    outs = pl.pallas_call(
        body, name=name, out_shape=out_shape,
        in_specs=[hbm] * n, out_specs=[hbm] * n,
        scratch_shapes=[pltpu.SemaphoreType.DMA((n, 6)), pltpu.SemaphoreType.DMA((n, 6)),
                        pltpu.SemaphoreType.DMA((n,))],
        compiler_params=pltpu.CompilerParams(has_side_effects=True),
    )(*arrs)
    return list(outs)


HBM_SPEC = pl.BlockSpec(memory_space=pltpu.HBM)
SEM_SPEC = pl.BlockSpec(memory_space=pltpu.SEMAPHORE)
DATAFLOW = pltpu.SideEffectType.DATAFLOW_SIDE_EFFECTING


def _plane_peers():
    x, y, c = lax.axis_index("x"), lax.axis_index("y"), lax.axis_index("c")
    return 2 * x + y, [((1 - x, y, c), 2 * (1 - x) + y), ((x, 1 - y, c), 2 * x + 1 - y),
                       ((1 - x, 1 - y, c), 2 * (1 - x) + 1 - y)]


def _plane_start(name, arrs, mode, after):
    n = len(arrs)
    lands = [lax.empty(((4,) + a.shape) if mode == "gather" else a.shape, a.dtype) for a in arrs]

    def body(*refs):
        srcs, landing = refs[:n], refs[n:2 * n]
        send_sems, recv_sems = refs[2 * n + 1], refs[2 * n + 2]
        token = refs[-1]
        me, peers = _plane_peers()
        for i in range(n):
            for k, (peer, peer_slot) in enumerate(peers):
                src = srcs[i] if mode == "gather" else srcs[i].at[peer_slot]
                pltpu.make_async_remote_copy(
                    src_ref=src, dst_ref=landing[i].at[me], send_sem=send_sems.at[3 * i + k],
                    recv_sem=recv_sems.at[3 * i + k], device_id=peer, device_id_type=MESH).start()
        token[...] = jnp.zeros_like(token)

    hbm_in = [pltpu.with_memory_space_constraint(a, pltpu.HBM) for a in list(arrs) + lands]
    out = pl.pallas_call(
        body, name=name,
        out_shape=[pltpu.SemaphoreType.DMA((3 * n,)), pltpu.SemaphoreType.DMA((3 * n,))]
                  + [pltpu.HBM(a.shape, a.dtype) for a in hbm_in]
                  + [jax.ShapeDtypeStruct((8, LANES), F32)],
        in_specs=[HBM_SPEC] * (2 * n) + [pl.BlockSpec(memory_space=pl.ANY)],
        out_specs=[SEM_SPEC, SEM_SPEC] + [HBM_SPEC] * (2 * n) + [pl.BlockSpec(memory_space=pltpu.VMEM)],
        input_output_aliases={i: 2 + i for i in range(2 * n)},
        compiler_params=pltpu.CompilerParams(has_side_effects=DATAFLOW),
    )(*hbm_in, after)
    return out[:-1], out[-1]


def _plane_wait(name, state, mode, after):
    send_sems, recv_sems = state[0], state[1]
    bufs = list(state[2:])
    n = len(bufs) // 2

    def body(*refs):
        srcs, landing = refs[:n], refs[n:2 * n]
        send_sems, recv_sems = refs[2 * n], refs[2 * n + 1]
        me, peers = _plane_peers()
        for i in range(n):
            for k, (peer, peer_slot) in enumerate(peers):
                src = srcs[i] if mode == "gather" else srcs[i].at[peer_slot]
                cp = pltpu.make_async_remote_copy(
                    src_ref=src, dst_ref=landing[i].at[peer_slot], send_sem=send_sems.at[3 * i + k],
                    recv_sem=recv_sems.at[3 * i + k], device_id=peer, device_id_type=MESH)
                cp.wait_send()
                cp.wait_recv()

    out = pl.pallas_call(
        body, name=name,
        out_shape=[pltpu.HBM(a.shape, a.dtype) for a in bufs],
        in_specs=[HBM_SPEC] * (2 * n) + [SEM_SPEC, SEM_SPEC, pl.BlockSpec(memory_space=pl.ANY)],
        out_specs=[HBM_SPEC] * (2 * n),
        input_output_aliases={i: i for i in range(2 * n)},
        compiler_params=pltpu.CompilerParams(has_side_effects=DATAFLOW),
    )(*bufs, send_sems, recv_sems, after)
    return list(out[n:])


def _cast_bf16(a, name):
    rows, cols = a.shape
    rt = _pick_tile(rows, 256, 16)

    def body(a_ref, o_ref):
        o_ref[...] = a_ref[...].astype(BF16)

    return pl.pallas_call(
        body, name=name, grid=(rows // rt,),
        in_specs=[pl.BlockSpec((rt, cols), lambda i: (i, 0))],
        out_specs=pl.BlockSpec((rt, cols), lambda i: (i, 0)),
        out_shape=jax.ShapeDtypeStruct(a.shape, BF16),
        compiler_params=_params(("parallel",)),
    )(a)


def _sum_slots(buf, name, own=None):
    n_slots, rows, cols = buf.shape
    rt = _pick_tile(rows, 128, 16)
    n_in = 1 if own is None else 2

    def body(*refs):
        b_ref, o_ref = refs[0], refs[-1]
        me = None if own is None else 2 * lax.axis_index("x") + lax.axis_index("y")
        acc = None
        for s in range(n_slots):
            term = b_ref[s] if own is None else jnp.where(me == s, refs[1][s], b_ref[s])
            acc = term.astype(F32) if acc is None else acc + term.astype(F32)
        o_ref[...] = acc

    return pl.pallas_call(
        body, name=name, grid=(rows // rt,),
        in_specs=[pl.BlockSpec((n_slots, rt, cols), lambda i: (0, i, 0))] * n_in,
        out_specs=pl.BlockSpec((rt, cols), lambda i: (i, 0)),
        out_shape=jax.ShapeDtypeStruct((rows, cols), F32),
        compiler_params=_params(("parallel",), 48),
    )(*([buf] if own is None else [buf, own]))


def _adamw(w, grads, m, v, name):
    rows, cols = w.shape
    rt = _pick_tile(rows, 128, 8)
    n_g = len(grads)
    c1 = 1.0 - ADAM_B1 ** ADAM_STEP
    c2 = 1.0 - ADAM_B2 ** ADAM_STEP

    def body(*refs):
        w_ref = refs[0]
        g_refs = refs[1:1 + n_g]
        m_ref, v_ref, g_out, d_out, m_out, v_out = refs[1 + n_g:]
        g = g_refs[0][...]
        for r in g_refs[1:]:
            g = g + r[...]
        m_new = ADAM_B1 * m_ref[...] + (1.0 - ADAM_B1) * g
        v_new = ADAM_B2 * v_ref[...] + (1.0 - ADAM_B2) * (g * g)
        m_hat = m_new / c1
        v_hat = v_new / c2
        g_out[...] = g
        d_out[...] = -ADAM_LR * (m_hat / (jnp.sqrt(v_hat) + ADAM_EPS) + ADAM_WD * w_ref[...])
        m_out[...] = m_new
        v_out[...] = v_new

    spec = pl.BlockSpec((rt, cols), lambda i: (i, 0))
    shape = jax.ShapeDtypeStruct((rows, cols), F32)
    return pl.pallas_call(
        body, name=name, grid=(rows // rt,),
        in_specs=[spec] * (3 + n_g), out_specs=[spec] * 4, out_shape=[shape] * 4,
        compiler_params=_params(("parallel",), 48),
    )(w, *grads, m, v)


def _in_proj(h, g_pre, w_main, w_lr):
    t_rows = h.shape[0]
    tm = _pick_tile(t_rows, 528, 16)
    n_col = w_main.shape[1] // D

    def body(h_ref, g_ref, w_ref, wlr_ref, proj_ref, u_ref, lr_ref):
        @pl.when(pl.program_id(1) == 0)
        def _():
            hh = h_ref[...]
            rstd = lax.rsqrt(jnp.mean(hh * hh, axis=-1, keepdims=True) + EPS)
            u = (hh * rstd * g_ref[...]).astype(BF16)
            u_ref[...] = u
            lr_ref[...] = _dot(u, wlr_ref[...])

        proj_ref[...] = _dot(u_ref[...], w_ref[...]).astype(BF16)

    return pl.pallas_call(
        body, name="in_proj", grid=(t_rows // tm, n_col),
        in_specs=[pl.BlockSpec((tm, D), lambda i, j: (i, 0)),
                  pl.BlockSpec((1, D), lambda i, j: (0, 0)),
                  pl.BlockSpec((D, D), lambda i, j: (0, j)),
                  pl.BlockSpec((D, LANES), lambda i, j: (0, 0))],
        out_specs=[pl.BlockSpec((tm, D), lambda i, j: (i, j)),
                   pl.BlockSpec((tm, D), lambda i, j: (i, 0)),
                   pl.BlockSpec((tm, LANES), lambda i, j: (i, 0))],
        out_shape=[jax.ShapeDtypeStruct((t_rows, w_main.shape[1]), BF16),
                   jax.ShapeDtypeStruct((t_rows, D), BF16),
                   jax.ShapeDtypeStruct((t_rows, LANES), F32)],
        compiler_params=_params(("parallel", "arbitrary"), 48),
    )(h, g_pre, w_main, w_lr)


def _conv_parts(p_ref, w_ref):
    cb = p_ref[:, 0:128].astype(F32)
    cc = p_ref[:, 128:256].astype(F32)
    cx = p_ref[:, 256:384].astype(F32)
    cz = p_ref[:, 384:512].astype(F32)
    rows = cb.shape[0]
    w = w_ref[...]
    p = cc * cx
    conv = pltpu.roll(p, 1, 0) * w[0:1] + p * w[1:2] + pltpu.roll(p, rows - 1, 0) * w[2:3]
    sz = _sigmoid(cz)
    return cb, cc, cx, cz, p, conv, sz, w


def _conv_fwd(proj, conv_w, n_seq, lf):
    def body(p_ref, w_ref, y_ref):
        cb, _, _, cz, _, conv, sz, _ = _conv_parts(p_ref, w_ref)
        y_ref[...] = (cb * conv * (cz * sz)).astype(BF16)

    return pl.pallas_call(
        body, name="conv_fwd", grid=(n_seq, N_CONV_TILES),
        in_specs=[pl.BlockSpec((lf, 512), lambda b, j: (b, j)),
                  pl.BlockSpec((3, 128), lambda b, j: (0, j))],
        out_specs=pl.BlockSpec((lf, 128), lambda b, j: (b, j)),
        out_shape=jax.ShapeDtypeStruct((n_seq * lf, D), BF16),
        compiler_params=_params(("parallel", "parallel"), 48),
    )(proj, conv_w)


def _conv_bwd(proj, conv_w, dyc, n_seq, lf):
    def body(p_ref, w_ref, dy_ref, dp_ref, dw_ref):
        cb, cc, cx, cz, p, conv, sz, w = _conv_parts(p_ref, w_ref)
        rows = cb.shape[0]
        dy = dy_ref[...].astype(F32)
        silu = cz * sz
        dcb = dy * conv * silu
        dconv = dy * cb * silu
        dcz = dy * cb * conv * (sz * (1.0 + cz * (1.0 - sz)))
        d_next = pltpu.roll(dconv, rows - 1, 0)
        d_prev = pltpu.roll(dconv, 1, 0)
        dp = d_next * w[0:1] + dconv * w[1:2] + d_prev * w[2:3]
        dp_ref[:, 0:128] = dcb.astype(BF16)
        dp_ref[:, 128:256] = (dp * cx).astype(BF16)
        dp_ref[:, 256:384] = (dp * cc).astype(BF16)
        dp_ref[:, 384:512] = dcz.astype(BF16)
        dw_ref[0:1, :] = jnp.sum(dconv * pltpu.roll(p, 1, 0), axis=0, keepdims=True)
        dw_ref[1:2, :] = jnp.sum(dconv * p, axis=0, keepdims=True)
        dw_ref[2:3, :] = jnp.sum(dconv * pltpu.roll(p, rows - 1, 0), axis=0, keepdims=True)

    return pl.pallas_call(
        body, name="conv_bwd", grid=(n_seq, N_CONV_TILES),
        in_specs=[pl.BlockSpec((lf, 512), lambda b, j: (b, j)),
                  pl.BlockSpec((3, 128), lambda b, j: (0, j)),
                  pl.BlockSpec((lf, 128), lambda b, j: (b, j))],
        out_specs=[pl.BlockSpec((lf, 512), lambda b, j: (b, j)),
                   pl.BlockSpec((None, 3, 128), lambda b, j: (b, 0, j))],
        out_shape=[jax.ShapeDtypeStruct((n_seq * lf, W_CONV), BF16),
                   jax.ShapeDtypeStruct((n_seq, 3, D), F32)],
        compiler_params=_params(("parallel", "parallel"), 48),
    )(proj, conv_w, dyc)


def _chunk_unroll(n_chunks):
    for u in (11, 3):
        if n_chunks % u == 0:
            return u
    return 1


def _chunk_masks():
    row = lax.broadcasted_iota(jnp.int32, (CHUNK, CHUNK), 0)
    col = lax.broadcasted_iota(jnp.int32, (CHUNK, CHUNK), 1)
    lower = col <= row
    upper = col >= row
    strict_upper = col > row
    return lower, upper, strict_upper


def _gla_gates(lr_bf, wg_ref, bg_ref, lf):
    z = _dot(lr_bf, wg_ref[...]) + bg_ref[...]
    valid = lax.broadcasted_iota(jnp.int32, (lf, HEAD_K), 0) >= PAD_FRONT
    return z, valid


def _gla_states(direction, n_chunks, qkv_ref, g_s, b_s, st_s, tri):
    def local(c, carry):
        rows = pl.ds(pl.multiple_of(c * CHUNK, CHUNK), CHUNK)
        b = _tri_dot(tri, g_s[rows, :])
        b_s[rows, :] = b
        b_end = b[CHUNK - 1:CHUNK, :] if direction == 0 else b[0:1, :]
        k = qkv_ref[rows, 128:256].astype(F32)
        v = qkv_ref[rows, 256:512]
        k_dec = (k * jnp.exp(b_end - b)).astype(BF16)
        st_s[c] = _dot_tn(v, k_dec)
        return carry

    lax.fori_loop(0, n_chunks, local, 0, unroll=_chunk_unroll(n_chunks))

    def scan(i, state):
        c = i if direction == 0 else n_chunks - 1 - i
        decay = _chunk_decay(direction, c, b_s)
        update = st_s[c]
        st_s[c] = state
        return state * decay + update

    lax.fori_loop(0, n_chunks, scan, jnp.zeros((HEAD_V, HEAD_K), F32))


def _chunk_decay(direction, c, b_s):
    if direction == 0:
        grp = b_s[pl.ds(pl.multiple_of(c * CHUNK + CHUNK - 8, 8), 8), :]
        return jnp.exp(grp[7:8, :])
    grp = b_s[pl.ds(pl.multiple_of(c * CHUNK, 8), 8), :]
    return jnp.exp(grp[0:1, :])


def _gla_fwd(proj, lr, wgf, wgb, bgf, bgb, n_seq, lf):
    n_chunks = lf // CHUNK
    scale = HEAD_K ** -0.5

    def body(qkv_ref, lr_ref, wgf_ref, wgb_ref, bgf_ref, bgb_ref, o_ref, g_s, b_s, st_s):
        lower, upper, strict_upper = _chunk_masks()
        lr_bf = lr_ref[...].astype(BF16)
        for direction in (0, 1):
            wg_ref, bg_ref = ((wgf_ref, bgf_ref), (wgb_ref, bgb_ref))[direction]
            z, valid = _gla_gates(lr_bf, wg_ref, bg_ref, lf)
            g_s[...] = jnp.where(valid, _log_sigmoid(z) / GATE_NORM, 0.0)
            tri = (lower if direction == 0 else upper).astype(BF16)
            smask = lower if direction == 0 else strict_upper
            _gla_states(direction, n_chunks, qkv_ref, g_s, b_s, st_s, tri)

            def out(c, carry):
                rows = pl.ds(pl.multiple_of(c * CHUNK, CHUNK), CHUNK)
                b = b_s[rows, :]
                q = qkv_ref[rows, 0:128].astype(F32) * scale
                k = qkv_ref[rows, 128:256].astype(F32)
                v = qkv_ref[rows, 256:512]
                q_in = (q * jnp.exp(b)).astype(BF16)
                k_in = (k * jnp.exp(-b)).astype(BF16)
                s = jnp.where(smask, _dot_nt(q_in, k_in), 0.0).astype(BF16)
                o = _dot(s, v) + _dot_nt(q_in, st_s[c].astype(BF16))
                if direction == 0:
                    o_ref[rows, :] = o
                else:
                    o_ref[rows, :] = o_ref[rows, :] + o
                return carry

            lax.fori_loop(0, n_chunks, out, 0, unroll=_chunk_unroll(n_chunks))

    return pl.pallas_call(
        body, name="gla_fwd", grid=(n_seq, N_HEADS),
        in_specs=[pl.BlockSpec((lf, 512), lambda b, h: (b, N_CONV_TILES + h)),
                  pl.BlockSpec((lf, LANES), lambda b, h: (b, 0)),
                  pl.BlockSpec((None, LANES, HEAD_K), lambda b, h: (h, 0, 0)),
                  pl.BlockSpec((None, LANES, HEAD_K), lambda b, h: (h, 0, 0)),
                  pl.BlockSpec((None, 1, HEAD_K), lambda b, h: (h, 0, 0)),
                  pl.BlockSpec((None, 1, HEAD_K), lambda b, h: (h, 0, 0))],
        out_specs=pl.BlockSpec((lf, HEAD_V), lambda b, h: (b, h)),
        out_shape=jax.ShapeDtypeStruct((n_seq * lf, D), F32),
        scratch_shapes=[pltpu.VMEM((lf, HEAD_K), F32), pltpu.VMEM((lf, HEAD_K), F32),
                        pltpu.VMEM((n_chunks, HEAD_V, HEAD_K), F32)],
        compiler_params=_params(("parallel", "parallel"), 48),
    )(proj, lr, wgf, wgb, bgf, bgb)


def _gla_bwd(proj, lr, d_o, wgf, wgb, bgf, bgb, n_seq, lf):
    n_chunks = lf // CHUNK
    scale = HEAD_K ** -0.5

    def body(qkv_ref, lr_ref, do_ref, wgf_ref, wgb_ref, bgf_ref, bgb_ref,
             dqkv_ref, dlr_ref, dwgf_ref, dwgb_ref, dbg_ref,
             g_s, b_s, fac_s, dg_s, st_s, dst_s, acc_s):
        lower, upper, strict_upper = _chunk_masks()
        lr_bf = lr_ref[...].astype(BF16)
        acc_s[...] = jnp.zeros_like(acc_s)
        dlr = jnp.zeros((lf, LANES), F32)
        for direction in (0, 1):
            wg_ref, bg_ref = ((wgf_ref, bgf_ref), (wgb_ref, bgb_ref))[direction]
            z, valid = _gla_gates(lr_bf, wg_ref, bg_ref, lf)
            g_s[...] = jnp.where(valid, _log_sigmoid(z) / GATE_NORM, 0.0)
            fac_s[...] = jnp.where(valid, _sigmoid(-z) / GATE_NORM, 0.0)
            tri = (lower if direction == 0 else upper).astype(BF16)
            tri_t = (upper if direction == 0 else lower).astype(BF16)
            smask = lower if direction == 0 else strict_upper
            end_row = CHUNK - 1 if direction == 0 else 0
            _gla_states(direction, n_chunks, qkv_ref, g_s, b_s, st_s, tri)

            def state_grad_local(c, carry):
                rows = pl.ds(pl.multiple_of(c * CHUNK, CHUNK), CHUNK)
                q = qkv_ref[rows, 0:128].astype(F32) * scale
                q_in = (q * jnp.exp(b_s[rows, :])).astype(BF16)
                dst_s[c] = _dot_tn(do_ref[rows, :], q_in)
                return carry

            lax.fori_loop(0, n_chunks, state_grad_local, 0, unroll=_chunk_unroll(n_chunks))

            def state_grad_scan(i, grad):
                c = n_chunks - 1 - i if direction == 0 else i
                decay = _chunk_decay(direction, c, b_s)
                local = dst_s[c]
                dst_s[c] = grad
                return local + grad * decay

            lax.fori_loop(0, n_chunks, state_grad_scan, jnp.zeros((HEAD_V, HEAD_K), F32))

            def chunk_grads(c, carry):
                rows = pl.ds(pl.multiple_of(c * CHUNK, CHUNK), CHUNK)
                b = b_s[rows, :]
                b_end = b[end_row:end_row + 1, :]
                q = qkv_ref[rows, 0:128].astype(F32) * scale
                k = qkv_ref[rows, 128:256].astype(F32)
                v = qkv_ref[rows, 256:512]
                d_out = do_ref[rows, :]
                e_pos = jnp.exp(b)
                e_neg = jnp.exp(-b)
                e_end = jnp.exp(b_end - b)
                q_in = q * e_pos
                k_in = k * e_neg
                k_dec = k * e_end
                q_in_bf = q_in.astype(BF16)
                k_in_bf = k_in.astype(BF16)
                state = st_s[c]
                d_state = dst_s[c]
                state_bf = state.astype(BF16)
                d_state_bf = d_state.astype(BF16)
                s = jnp.where(smask, _dot_nt(q_in_bf, k_in_bf), 0.0).astype(BF16)
                ds = jnp.where(smask, _dot_nt(d_out, v), 0.0).astype(BF16)
                dv = _dot_tn(s, d_out) + _dot_nt(k_dec.astype(BF16), d_state_bf)
                dq_in = _dot(ds, k_in_bf) + _dot(d_out, state_bf)
                dk_in = _dot_tn(ds, q_in_bf)
                dk_dec = _dot(v, d_state_bf)
                acc_s[rows, 0:128] = acc_s[rows, 0:128] + dq_in * e_pos * scale
                acc_s[rows, 128:256] = acc_s[rows, 128:256] + dk_in * e_neg + dk_dec * e_end
                acc_s[rows, 256:512] = acc_s[rows, 256:512] + dv
                dkk = dk_dec * k_dec
                db = dq_in * q_in - dk_in * k_in - dkk
                d_decay = jnp.sum(d_state * state, axis=0, keepdims=True)
                db_end = jnp.sum(dkk, axis=0, keepdims=True) + d_decay * jnp.exp(b_end)
                at_end = lax.broadcasted_iota(jnp.int32, (CHUNK, HEAD_K), 0) == end_row
                db = db + jnp.where(at_end, db_end, 0.0)
                dg_s[rows, :] = _tri_dot(tri_t, db)
                return carry

            lax.fori_loop(0, n_chunks, chunk_grads, 0, unroll=_chunk_unroll(n_chunks))

            dz = dg_s[...] * fac_s[...]
            dz_bf = dz.astype(BF16)
            dbg_ref[direction:direction + 1, :] = jnp.sum(dz, axis=0, keepdims=True)
            (dwgf_ref, dwgb_ref)[direction][...] = _dot_tn(lr_bf, dz_bf)
            dlr = dlr + _dot_nt(dz_bf, wg_ref[...])

        dqkv_ref[...] = acc_s[...].astype(BF16)

        @pl.when(pl.program_id(1) == 0)
        def _():
            dlr_ref[...] = dlr

        @pl.when(pl.program_id(1) != 0)
        def _():
            dlr_ref[...] = dlr_ref[...] + dlr

    gate_w = pl.BlockSpec((None, LANES, HEAD_K), lambda b, h: (h, 0, 0))
    gate_b = pl.BlockSpec((None, 1, HEAD_K), lambda b, h: (h, 0, 0))
    return pl.pallas_call(
        body, name="gla_bwd", grid=(n_seq, N_HEADS),
        in_specs=[pl.BlockSpec((lf, 512), lambda b, h: (b, N_CONV_TILES + h)),
                  pl.BlockSpec((lf, LANES), lambda b, h: (b, 0)),
                  pl.BlockSpec((lf, HEAD_V), lambda b, h: (b, h)),
                  gate_w, gate_w, gate_b, gate_b],
        out_specs=[pl.BlockSpec((lf, 512), lambda b, h: (b, h)),
                   pl.BlockSpec((lf, LANES), lambda b, h: (b, 0)),
                   pl.BlockSpec((None, None, LANES, HEAD_K), lambda b, h: (b, h, 0, 0)),
                   pl.BlockSpec((None, None, LANES, HEAD_K), lambda b, h: (b, h, 0, 0)),
                   pl.BlockSpec((None, None, 2, HEAD_K), lambda b, h: (b, h, 0, 0))],
        out_shape=[jax.ShapeDtypeStruct((n_seq * lf, W_GLA), BF16),
                   jax.ShapeDtypeStruct((n_seq * lf, LANES), F32),
                   jax.ShapeDtypeStruct((n_seq, N_HEADS, LANES, HEAD_K), F32),
                   jax.ShapeDtypeStruct((n_seq, N_HEADS, LANES, HEAD_K), F32),
                   jax.ShapeDtypeStruct((n_seq, N_HEADS, 2, HEAD_K), F32)],
        scratch_shapes=[pltpu.VMEM((lf, HEAD_K), F32), pltpu.VMEM((lf, HEAD_K), F32),
                        pltpu.VMEM((lf, HEAD_K), F32), pltpu.VMEM((lf, HEAD_K), F32),
                        pltpu.VMEM((n_chunks, HEAD_V, HEAD_K), F32),
                        pltpu.VMEM((n_chunks, HEAD_V, HEAD_K), F32),
                        pltpu.VMEM((lf, 512), F32)],
        compiler_params=_params(("parallel", "arbitrary"), 56),
    )(proj, lr, d_o, wgf, wgb, bgf, bgb)


def _tail(h, tgt, yc, o, proj, w3, gamma, g_post, lf):
    t_rows = h.shape[0]
    tm = _pick_tile(t_rows, 256, CHUNK)
    n_chunks = lf // CHUNK
    per_tile = tm // CHUNK

    def body(h_ref, tgt_ref, yc_ref, o_ref, r_ref, ma_ref, mb_ref, w_hbm, gamma_ref, gpost_ref,
             dres_ref, yg_ref, merged_ref, dout_ref, dpc_ref, dpg_ref, dyc_ref, do_ref, dtail_ref,
             loss_ref, dgpost_ref, dgamma_ref, w_s, w_sem):
        i = pl.program_id(0)

        @pl.when(i == 0)
        def _():
            cp = pltpu.make_async_copy(w_hbm, w_s, w_sem)
            cp.start()
            cp.wait()
            loss_ref[...] = jnp.zeros_like(loss_ref)
            dgpost_ref[...] = jnp.zeros_like(dgpost_ref)
            dgamma_ref[...] = jnp.zeros_like(dgamma_ref)

        gamma = gamma_ref[...]
        o = o_ref[...]
        r = r_ref[...].astype(F32)
        sr = _sigmoid(r)
        silu_r = r * sr
        n_parts, rstd_parts = [], []
        for hd in range(N_HEADS):
            oh = o[:, hd * HEAD_V:(hd + 1) * HEAD_V]
            rstd = lax.rsqrt(jnp.mean(oh * oh, axis=-1, keepdims=True) + EPS)
            n_parts.append(oh * rstd)
            rstd_parts.append(rstd)
        n = jnp.concatenate(n_parts, axis=-1)
        gamma_t = jnp.concatenate([gamma] * N_HEADS, axis=-1)
        yg = n * gamma_t * silu_r
        yg_bf = yg.astype(BF16)
        yg_ref[...] = yg_bf
        yc = yc_ref[...]
        pc = _dot(yc, w_s[0])
        pg = _dot(yg_bf, w_s[1])
        sa = _sigmoid(ma_ref[...].astype(F32))
        sb = _sigmoid(mb_ref[...].astype(F32))
        merged = (sa * pc + sb * pg).astype(BF16)
        merged_ref[...] = merged
        out = _dot(merged, w_s[2])
        rstd2 = lax.rsqrt(jnp.mean(out * out, axis=-1, keepdims=True) + EPS)
        nn = out * rstd2
        gpost = gpost_ref[...]
        y = h_ref[...] + nn * gpost

        rowi = lax.broadcasted_iota(jnp.int32, (tm, 1), 0)
        keep = jnp.zeros((tm, 1), F32)
        for kk in range(per_tile):
            is_tok = ((i * per_tile + kk) % n_chunks) != 0
            f = jnp.where(is_tok, 1.0, 0.0)
            keep = jnp.where((rowi >= kk * CHUNK) & (rowi < (kk + 1) * CHUNK), f, keep)
        diff = (y - tgt_ref[...]) * keep
        loss_ref[...] += jnp.sum(diff * diff) * (0.5 / D)
        dy = diff * (1.0 / D)
        dres_ref[...] = dy
        dgpost_ref[...] += jnp.sum(dy * nn, axis=0, keepdims=True)
        dn = dy * gpost
        dout = (rstd2 * (dn - nn * jnp.mean(dn * nn, axis=-1, keepdims=True))).astype(BF16)
        dout_ref[...] = dout
        dmerged = _dot_nt(dout, w_s[2])
        dpc = (dmerged * sa).astype(BF16)
        dpg = (dmerged * sb).astype(BF16)
        dpc_ref[...] = dpc
        dpg_ref[...] = dpg
        dtail_ref[:, D:2 * D] = (dmerged * pc * (sa * (1.0 - sa))).astype(BF16)
        dtail_ref[:, 2 * D:3 * D] = (dmerged * pg * (sb * (1.0 - sb))).astype(BF16)
        dyc_ref[...] = _dot_nt(dpc, w_s[0]).astype(BF16)
        dyg = _dot_nt(dpg, w_s[1])
        dtail_ref[:, 0:D] = (dyg * n * gamma_t * (sr * (1.0 + r * (1.0 - sr)))).astype(BF16)
        dgam_full = jnp.sum(dyg * n * silu_r, axis=0, keepdims=True)
        dgam = dgam_full[:, 0:HEAD_V]
        for hd in range(1, N_HEADS):
            dgam = dgam + dgam_full[:, hd * HEAD_V:(hd + 1) * HEAD_V]
        dgamma_ref[...] += dgam
        dng = dyg * gamma_t * silu_r
        do_parts = []
        for hd in range(N_HEADS):
            sl = slice(hd * HEAD_V, (hd + 1) * HEAD_V)
            dnh = dng[:, sl]
            nh = n_parts[hd]
            do_parts.append(rstd_parts[hd] * (dnh - nh * jnp.mean(dnh * nh, axis=-1, keepdims=True)))
        do_ref[...] = jnp.concatenate(do_parts, axis=-1).astype(BF16)

    row = lambda c: pl.BlockSpec((tm, D), lambda i: (i, c))
    const = lambda shape: pl.BlockSpec(shape, lambda i: (0, 0))
    act = jax.ShapeDtypeStruct((t_rows, D), BF16)
    return pl.pallas_call(
        body, name="tail", grid=(t_rows // tm,),
        in_specs=[row(0), row(0), row(0), row(0), row(6), row(7), row(8),
                  pl.BlockSpec(memory_space=pl.ANY), const((1, HEAD_V)), const((1, D))],
        out_specs=[row(0)] * 8 + [pl.BlockSpec((tm, W_TAIL), lambda i: (i, 0)),
                                  const((8, LANES)), const((1, D)), const((1, HEAD_V))],
        out_shape=[jax.ShapeDtypeStruct((t_rows, D), F32)] + [act] * 7
                  + [jax.ShapeDtypeStruct((t_rows, W_TAIL), BF16),
                     jax.ShapeDtypeStruct((8, LANES), F32),
                     jax.ShapeDtypeStruct((1, D), F32),
                     jax.ShapeDtypeStruct((1, HEAD_V), F32)],
        scratch_shapes=[pltpu.VMEM((3, D, D), BF16), pltpu.SemaphoreType.DMA],
        compiler_params=_params(("arbitrary",), 56),
    )(h, tgt, yc, o, proj, proj, proj, w3, gamma, g_post)


def _wgrad(a, b, name, out_dtype=BF16):
    t_rows, m = a.shape
    n = b.shape[1]
    tn = D if n % D == 0 else n
    tk = _pick_tile(t_rows, 528, 16)
    n_k = t_rows // tk

    def body(a_ref, b_ref, o_ref, acc):
        k = pl.program_id(1)

        @pl.when(k == 0)
        def _():
            acc[...] = jnp.zeros_like(acc)

        acc[...] += _dot_tn(a_ref[...].astype(BF16), b_ref[...].astype(BF16))

        @pl.when(k == n_k - 1)
        def _():
            o_ref[...] = acc[...].astype(out_dtype)

    return pl.pallas_call(
        body, name=name, grid=(n // tn, n_k),
        in_specs=[pl.BlockSpec((tk, m), lambda j, k: (k, 0)),
                  pl.BlockSpec((tk, tn), lambda j, k: (k, j))],
        out_specs=pl.BlockSpec((m, tn), lambda j, k: (0, j)),
        out_shape=jax.ShapeDtypeStruct((m, n), out_dtype),
        scratch_shapes=[pltpu.VMEM((m, tn), F32)],
        compiler_params=_params(("parallel", "arbitrary"), 48),
    )(a, b)


def _dgrad_in(dpc, dpg, dpt, dlr, w_main, w_lr, h, g_pre, dres, token):
    t_rows = h.shape[0]
    tm = _pick_tile(t_rows, 528, 16)
    n_c, n_g, n_t = W_CONV // D, W_GLA // D, W_TAIL // D
    n_col = n_c + n_g + n_t

    def body(dpc_ref, dpg_ref, dpt_ref, dlr_ref, w_ref, wlr_ref, h_ref, g_ref, dres_ref, token_ref,
             dh_ref, dg_ref, acc):
        i, j = pl.program_id(0), pl.program_id(1)

        @pl.when(j == 0)
        def _():
            acc[...] = _dot_nt(dlr_ref[...].astype(BF16), wlr_ref[...])

        @pl.when(j < n_c)
        def _():
            acc[...] += _dot_nt(dpc_ref[...], w_ref[...])

        @pl.when((j >= n_c) & (j < n_c + n_g))
        def _():
            acc[...] += _dot_nt(dpg_ref[...], w_ref[...])

        @pl.when(j >= n_c + n_g)
        def _():
            acc[...] += _dot_nt(dpt_ref[...], w_ref[...])

        @pl.when((i == 0) & (j == 0))
        def _():
            dg_ref[...] = jnp.zeros_like(dg_ref)

        @pl.when(j == n_col - 1)
        def _():
            hh = h_ref[...]
            rstd = lax.rsqrt(jnp.mean(hh * hh, axis=-1, keepdims=True) + EPS)
            xhat = hh * rstd
            du = acc[...]
            dg_ref[...] += jnp.sum(du * xhat, axis=0, keepdims=True)
            dx = du * g_ref[...]
            dh_ref[...] = rstd * (dx - xhat * jnp.mean(dx * xhat, axis=-1, keepdims=True)) + dres_ref[...]

    clamp = lambda v, lo, hi: jnp.minimum(jnp.maximum(v, lo), hi)
    return pl.pallas_call(
        body, name="dgrad_in", grid=(t_rows // tm, n_col),
        in_specs=[pl.BlockSpec((tm, D), lambda i, j: (i, clamp(j, 0, n_c - 1))),
                  pl.BlockSpec((tm, D), lambda i, j: (i, clamp(j - n_c, 0, n_g - 1))),
                  pl.BlockSpec((tm, D), lambda i, j: (i, clamp(j - n_c - n_g, 0, n_t - 1))),
                  pl.BlockSpec((tm, LANES), lambda i, j: (i, 0)),
                  pl.BlockSpec((D, D), lambda i, j: (0, j)),
                  pl.BlockSpec((D, LANES), lambda i, j: (0, 0)),
                  pl.BlockSpec((tm, D), lambda i, j: (i, 0)),
                  pl.BlockSpec((1, D), lambda i, j: (0, 0)),
                  pl.BlockSpec((tm, D), lambda i, j: (i, 0)),
                  pl.BlockSpec((8, LANES), lambda i, j: (0, 0))],
        out_specs=[pl.BlockSpec((tm, D), lambda i, j: (i, 0)),
                   pl.BlockSpec((1, D), lambda i, j: (0, 0))],
        out_shape=[jax.ShapeDtypeStruct((t_rows, D), F32), jax.ShapeDtypeStruct((1, D), F32)],
        scratch_shapes=[pltpu.VMEM((tm, D), F32)],
        compiler_params=_params(("arbitrary", "arbitrary"), 48),
    )(dpc, dpg, dpt, dlr, w_main, w_lr, h, g_pre, dres, token)


OFF_CB, OFF_CC, OFF_CX, OFF_CZ = 0, 1024, 2048, 3072
OFF_Q, OFF_K, OFF_V, OFF_R = 4096, 4608, 5120, 6144
OFF_LR, OFF_MA, OFF_MB = 7168, 7200, 8224


def _main_columns(w):
    rows = w.shape[0]
    conv = w[:, 0:W_CONV].reshape(rows, 4, N_CONV_TILES, 128).transpose(0, 2, 1, 3).reshape(rows, W_CONV)
    q = w[:, OFF_Q:OFF_K].reshape(rows, N_HEADS, HEAD_K)
    k = w[:, OFF_K:OFF_V].reshape(rows, N_HEADS, HEAD_K)
    v = w[:, OFF_V:OFF_R].reshape(rows, N_HEADS, HEAD_V)
    gla = jnp.concatenate([q, k, v], axis=2).reshape(rows, W_GLA)
    return jnp.concatenate([conv, gla, w[:, OFF_R:OFF_R + D], w[:, OFF_MA:OFF_MA + 2 * D]], axis=1)


def _reference_columns(g_conv, g_gla, g_tail, g_lr):
    rows = g_conv.shape[0]
    conv = g_conv.reshape(rows, N_CONV_TILES, 4, 128).transpose(0, 2, 1, 3).reshape(rows, W_CONV)
    gla = g_gla.reshape(rows, N_HEADS, 512)
    q = gla[:, :, 0:128].reshape(rows, N_HEADS * HEAD_K)
    k = gla[:, :, 128:256].reshape(rows, N_HEADS * HEAD_K)
    v = gla[:, :, 256:512].reshape(rows, N_HEADS * HEAD_V)
    return jnp.concatenate([conv, q, k, v, g_tail[:, 0:D], g_lr[:, 0:2 * RANK], g_tail[:, D:3 * D]], axis=1)


def _pack(arrs, rows):
    flat = jnp.concatenate([a.reshape(-1) for a in arrs])
    return jnp.pad(flat, (0, rows * LANES - flat.shape[0])).reshape(rows, LANES)


def _unpack(packed, shapes):
    flat = packed.reshape(-1)
    out, pos = [], 0
    for s in shapes:
        size = 1
        for d in s:
            size *= d
        out.append(flat[pos:pos + size].reshape(s))
        pos += size
    return out


def _rows_for(shapes, mult=8):
    total = 0
    for s in shapes:
        size = 1
        for d in s:
            size *= d
        total += size
    return -(-total // (mult * LANES)) * mult


def kernel(x, meta_tokens, norm_pre, w_in, conv_w, w_gate_fwd, b_gate_fwd, w_gate_bwd, b_gate_bwd, gla_norm, w_out_conv, w_out_gla, w_merge_out, norm_post, loss_target, m_meta_tokens, m_norm_pre, m_w_in, m_conv_w, m_w_gate_fwd, m_b_gate_fwd, m_w_gate_bwd, m_b_gate_bwd, m_gla_norm, m_w_out_conv, m_w_out_gla, m_w_merge_out, m_norm_post, v_meta_tokens, v_norm_pre, v_w_in, v_conv_w, v_w_gate_fwd, v_b_gate_fwd, v_w_gate_bwd, v_b_gate_bwd, v_gla_norm, v_w_out_conv, v_w_out_gla, v_w_merge_out, v_norm_post):
    n_seq, seq, _ = x.shape
    lf = CHUNK + seq
    t_rows = n_seq * lf
    shard = 2 * lax.axis_index("x") + lax.axis_index("y")

    w_in_bf = _cast_bf16(w_in[0], "cast_w_in")
    w_out_bf = _cast_bf16(jnp.concatenate([w_out_conv[0], w_out_gla[0], w_merge_out[0]], axis=0), "cast_w_out")
    small_shapes = [(N_META, D // 4), (3, D // 4), (RANK, HEAD_K), (RANK, HEAD_K)]
    small = _pack([meta_tokens, conv_w[0], w_gate_fwd[0], w_gate_bwd[0]], _rows_for(small_shapes, 16))
    w_in_all, small_all = _gather_via_sibling("gather_w_in", [w_in_bf, small])
    w_out_state, _ = _plane_start("gather_w_out_start", [w_out_bf], "gather", small_all)

    w_full = jnp.transpose(w_in_all, (1, 0, 2)).reshape(D, N_IN)
    w_main = _main_columns(w_full)
    w_lr = jnp.pad(w_full[:, OFF_LR:OFF_LR + 2 * RANK], ((0, 0), (0, LANES - 2 * RANK)))
    smalls = [_unpack(small_all[s], small_shapes) for s in range(4)]
    meta_full = jnp.concatenate([smalls[s][0] for s in range(4)], axis=1)
    conv_full = jnp.concatenate([smalls[s][1] for s in range(4)], axis=1)
    wgf = jnp.stack([jnp.pad(smalls[s][2], ((0, LANES - RANK), (0, 0))) for s in range(4)]).astype(BF16)
    wgb = jnp.stack([jnp.pad(smalls[s][3], ((RANK, LANES - 2 * RANK), (0, 0))) for s in range(4)]).astype(BF16)
    bgf = b_gate_fwd.reshape(N_HEADS, 1, HEAD_K)
    bgb = b_gate_bwd.reshape(N_HEADS, 1, HEAD_K)

    head = jnp.concatenate([jnp.zeros((PAD_FRONT, D), F32), meta_full], axis=0)
    h = jnp.concatenate([jnp.broadcast_to(head[None], (n_seq, CHUNK, D)), x], axis=1).reshape(t_rows, D)
    tgt = jnp.pad(loss_target, ((0, 0), (CHUNK, 0), (0, 0))).reshape(t_rows, D)

    proj, u, lr = _in_proj(h, norm_pre, w_main, w_lr)
    yc = _conv_fwd(proj, conv_full, n_seq, lf)
    o = _gla_fwd(proj, lr, wgf, wgb, bgf, bgb, n_seq, lf)
    (w_out_landed,) = _plane_wait("gather_w_out_wait", w_out_state, "gather", o)
    slot_ids = lax.broadcasted_iota(jnp.int32, (4, 1, 1), 0)
    w_out_all = jnp.where(slot_ids == shard, w_out_bf[None], w_out_landed)
    w3 = jnp.transpose(w_out_all.reshape(4, 3, D // 4, D), (1, 0, 2, 3)).reshape(3, D, D)
    (dres, yg, merged, dout, dpc_out, dpg_out, dyc, d_o, dtail, loss_acc, d_gpost, d_gamma) = _tail(
        h, tgt, yc, o, proj, w3, gla_norm, norm_post, lf)
    g_w_oc = _wgrad(yc, dpc_out, "wgrad_out_conv")
    g_w_og = _wgrad(yg, dpg_out, "wgrad_out_gla")
    g_w_mo = _wgrad(merged, dout, "wgrad_merge_out")
    dgla, dlr, dwgf_p, dwgb_p, dbg_p = _gla_bwd(proj, lr, d_o, wgf, wgb, bgf, bgb, n_seq, lf)
    dconv, dconvw_p = _conv_bwd(proj, conv_full, dyc, n_seq, lf)
    g_conv = _wgrad(u, dconv, "wgrad_in_conv")
    g_gla = _wgrad(u, dgla, "wgrad_in_gla")
    g_tail = _wgrad(u, dtail, "wgrad_in_tail")
    g_lr = _wgrad(u, dlr, "wgrad_in_lr")

    g_in_full = _reference_columns(g_conv, g_gla, g_tail, g_lr)
    g_in_slots = jnp.transpose(g_in_full.reshape(D, 4, SHARD_IN), (1, 0, 2))
    g_out_slots = jnp.concatenate([g.reshape(4, D // 4, D) for g in (g_w_oc, g_w_og, g_w_mo)], axis=1)
    grads_state, grads_token = _plane_start("scatter_grads_start", [g_in_slots, g_out_slots], "scatter", g_lr)
    dh, d_gpre = _dgrad_in(dconv, dgla, dtail, dlr, w_main, w_lr, h, norm_pre, dres, grads_token)
    got_in, got_out = _plane_wait("scatter_grads_wait", grads_state, "scatter", d_gpre)

    dh3 = dh.reshape(n_seq, lf, D)
    grad_x = dh3[:, CHUNK:, :]

    d_meta = jnp.sum(dh3[:, PAD_FRONT:CHUNK, :], axis=0)
    d_convw = jnp.sum(dconvw_p, axis=0)
    d_wgf = jnp.transpose(jnp.sum(dwgf_p, axis=0)[:, 0:RANK, :], (1, 0, 2)).reshape(RANK, N_HEADS * HEAD_K)
    d_wgb = jnp.transpose(jnp.sum(dwgb_p, axis=0)[:, RANK:2 * RANK, :], (1, 0, 2)).reshape(RANK, N_HEADS * HEAD_K)
    d_bg = jnp.sum(dbg_p, axis=0)
    d_bgf = d_bg[:, 0, :].reshape(1, N_HEADS * HEAD_K)
    d_bgb = d_bg[:, 1, :].reshape(1, N_HEADS * HEAD_K)
    part_shapes = [(N_META, D), (3, D), (RANK, 512), (RANK, 512), (1, D), (1, 512), (1, 512), (1, HEAD_V),
                   (1, D), (1, LANES)]
    parts = _pack([d_meta, d_convw, d_wgf, d_wgb, d_gpre, d_bgf, d_bgb, d_gamma, d_gpost, loss_acc[0:1, :]],
                  _rows_for(part_shapes))
    (parts_all,) = _exchange("gather_small_grads", [parts], ALL_FLIPS, (4, 2, 1), "gather")
    (g_meta, g_convw, g_wgf, g_wgb, g_npre, g_bgf, g_bgb, g_gnorm, g_npost, loss_row) = _unpack(
        _sum_slots(parts_all, "sum_small_grads"), part_shapes)
    loss = loss_row[0, 0]

    def col_shard(a, width):
        return lax.dynamic_slice_in_dim(a, shard * width, width, axis=a.ndim - 1)

    upd_shapes = [(N_META, D // 4), (3, D // 4), (RANK, HEAD_K), (RANK, HEAD_K), (1, D), (1, 512), (1, 512),
                  (1, HEAD_V), (1, D)]
    upd_rows = _rows_for(upd_shapes)
    small_w = _pack([meta_tokens, conv_w[0], w_gate_fwd[0], w_gate_bwd[0], norm_pre, b_gate_fwd, b_gate_bwd,
                     gla_norm, norm_post], upd_rows)
    small_g = _pack([col_shard(g_meta, D // 4), col_shard(g_convw, D // 4), col_shard(g_wgf, HEAD_K),
                     col_shard(g_wgb, HEAD_K), g_npre, g_bgf, g_bgb, g_gnorm, g_npost], upd_rows)
    small_m = _pack([m_meta_tokens, m_conv_w[0], m_w_gate_fwd[0], m_w_gate_bwd[0], m_norm_pre, m_b_gate_fwd,
                     m_b_gate_bwd, m_gla_norm, m_norm_post], upd_rows)
    small_v = _pack([v_meta_tokens, v_conv_w[0], v_w_gate_fwd[0], v_w_gate_bwd[0], v_norm_pre, v_b_gate_fwd,
                     v_b_gate_bwd, v_gla_norm, v_norm_post], upd_rows)
    small_out = [_unpack(a, upd_shapes) for a in _adamw(small_w, [small_g], small_m, small_v, "adamw_small")]

    plane_in = _sum_slots(got_in, "sum_w_in_grads", own=g_in_slots)
    plane_out = _sum_slots(got_out, "sum_w_out_grads", own=g_out_slots)
    other_in, other_out = _exchange("swap_plane_sums", [plane_in, plane_out], SIBLING_FLIPS, (0, 0, 0), "swap")
    big_in = _adamw(w_in[0], [plane_in, other_in], m_w_in[0], v_w_in[0], "adamw_w_in")
    w_out_rows = jnp.concatenate([w_out_conv[0], w_out_gla[0], w_merge_out[0]], axis=0)
    m_out_rows = jnp.concatenate([m_w_out_conv[0], m_w_out_gla[0], m_w_merge_out[0]], axis=0)
    v_out_rows = jnp.concatenate([v_w_out_conv[0], v_w_out_gla[0], v_w_merge_out[0]], axis=0)
    big_out = _adamw(w_out_rows, [plane_out, other_out], m_out_rows, v_out_rows, "adamw_w_out")

    results = []
    for kind in range(4):
        sm = small_out[kind]
        w_in_part = big_in[kind][None]
        outs3 = big_out[kind].reshape(3, 1, D // 4, D)
        results.extend([
            sm[0], sm[4], w_in_part, sm[1][None], sm[2][None], sm[5], sm[3][None], sm[6], sm[7],
            outs3[0], outs3[1], outs3[2], sm[8]])
    return (loss, grad_x, *results)
```

```python
import functools

import jax
import jax.numpy as jnp
from jax import lax
from jax.experimental import pallas as pl
from jax.experimental.pallas import tpu as pltpu

F32 = jnp.float32
BF16 = jnp.bfloat16
MESH = pl.DeviceIdType.MESH

D = 1024
N_META = 16
CHUNK = 64
PAD_FRONT = CHUNK - N_META
N_HEADS = 4
HEAD_K = 128
HEAD_V = 256
RANK = 16
EPS = 1e-6
GATE_NORM = 16.0
N_IN = 9248
SHARD_IN = N_IN // 4
LANES = 128
N_CONV_TILES = 8
W_CONV = 4096
W_GLA = 2048
W_TAIL = 3072
N_MAIN = W_CONV + W_GLA + W_TAIL
MIB = 1024 * 1024

ADAM_LR = 0.001
ADAM_B1 = 0.9
ADAM_B2 = 0.999
ADAM_EPS = 1e-08
ADAM_WD = 0.01
ADAM_STEP = 10


def _params(sem=None, vmem_mib=None):
    return pltpu.CompilerParams(
        dimension_semantics=sem,
        vmem_limit_bytes=None if vmem_mib is None else vmem_mib * MIB)


def _pick_tile(n, target, mult):
    best = None
    for t in range(mult, min(n, target) + 1, mult):
        if n % t == 0:
            best = t
    return n if best is None else best


def _sigmoid(v):
    return 1.0 / (1.0 + jnp.exp(-v))


def _log_sigmoid(v):
    return jnp.minimum(v, 0.0) - jnp.log(1.0 + jnp.exp(-jnp.abs(v)))


def _dot(a, b):
    return jnp.dot(a, b, preferred_element_type=F32)


def _dot_nt(a, b):
    return lax.dot_general(a, b, (((1,), (1,)), ((), ())), preferred_element_type=F32)


def _dot_tn(a, b):
    return lax.dot_general(a, b, (((0,), (0,)), ((), ())), preferred_element_type=F32)


def _tri_dot(tri, v):
    hi = v.astype(BF16)
    lo = (v - hi.astype(F32)).astype(BF16)
    return _dot(tri, hi) + _dot(tri, lo)


PLANE_FLIPS = ((1, 0, 0), (0, 1, 0), (1, 1, 0))
ALL_FLIPS = tuple((m >> 2 & 1, m >> 1 & 1, m & 1) for m in range(1, 8))
SIBLING_FLIPS = ((0, 0, 1),)


def _exchange(name, arrs, flips, slot_weights, mode):
    n = len(arrs)
    n_slots = 1
    for w in slot_weights:
        n_slots += w
    if mode == "gather":
        out_shape = [jax.ShapeDtypeStruct((n_slots,) + a.shape, a.dtype) for a in arrs]
    else:
        out_shape = [jax.ShapeDtypeStruct(a.shape, a.dtype) for a in arrs]

    def body(*refs):
        ins, outs = refs[:n], refs[n:2 * n]
        send_sems, recv_sems, local_sems = refs[2 * n:]
        pos = (lax.axis_index("x"), lax.axis_index("y"), lax.axis_index("c"))

        def slot_of(p):
            return p[0] * slot_weights[0] + p[1] * slot_weights[1] + p[2] * slot_weights[2]

        peers = [tuple(1 - pos[a] if f[a] else pos[a] for a in range(3)) for f in flips]
        me = slot_of(pos)
        local = []
        sends = []
        for i in range(n):
            if mode != "swap":
                src = ins[i] if mode == "gather" else ins[i].at[me]
                cp = pltpu.make_async_copy(src, outs[i].at[me], local_sems.at[i])
                cp.start()
                local.append(cp)
            for k, peer in enumerate(peers):
                if mode == "gather":
                    src, dst = ins[i], outs[i].at[me]
                elif mode == "scatter":
                    src, dst = ins[i].at[slot_of(peer)], outs[i].at[me]
                else:
                    src, dst = ins[i], outs[i]
                cp = pltpu.make_async_remote_copy(
                    src_ref=src, dst_ref=dst, send_sem=send_sems.at[i, k], recv_sem=recv_sems.at[i, k],
                    device_id=peer, device_id_type=MESH)
                cp.start()
                sends.append(cp)
        for i in range(n):
            for k, peer in enumerate(peers):
                if mode == "gather":
                    src, dst = ins[i], outs[i].at[slot_of(peer)]
                elif mode == "scatter":
                    src, dst = ins[i].at[me], outs[i].at[slot_of(peer)]
                else:
                    src, dst = ins[i], outs[i]
                arrival = pltpu.make_async_remote_copy(
                    src_ref=src, dst_ref=dst, send_sem=send_sems.at[i, k], recv_sem=recv_sems.at[i, k],
                    device_id=peer, device_id_type=MESH)
                arrival.wait_recv()
        for cp in sends:
            cp.wait_send()
        for cp in local:
            cp.wait()

    hbm = pl.BlockSpec(memory_space=pl.ANY)
    outs = pl.pallas_call(
        body, name=name, out_shape=out_shape,
        in_specs=[hbm] * n, out_specs=[hbm] * n,
        scratch_shapes=[pltpu.SemaphoreType.DMA((n, len(flips))),
                        pltpu.SemaphoreType.DMA((n, len(flips))),
                        pltpu.SemaphoreType.DMA((n,))],
        compiler_params=pltpu.CompilerParams(has_side_effects=True),
    )(*arrs)
    return list(outs)


def _gather_via_sibling(name, arrs):
    n = len(arrs)
    out_shape = [jax.ShapeDtypeStruct((4,) + a.shape, a.dtype) for a in arrs]

    def body(*refs):
        ins, outs = refs[:n], refs[n:2 * n]
        send_sems, recv_sems, local_sems = refs[2 * n:]
        x, y, c = lax.axis_index("x"), lax.axis_index("y"), lax.axis_index("c")
        me = 2 * x + y
        chips = [(1 - x, y), (x, 1 - y), (1 - x, 1 - y)]
        sibling = (x, y, 1 - c)

        def half(ref, which):
            rows = ref.shape[0] // 2
            return ref.at[pl.ds(which * rows, rows)]

        def copy(src, dst, i, k, to):
            return pltpu.make_async_remote_copy(
                src_ref=src, dst_ref=dst, send_sem=send_sems.at[i, k], recv_sem=recv_sems.at[i, k],
                device_id=to, device_id_type=MESH)

        local, sends = [], []
        for i in range(n):
            cp = pltpu.make_async_copy(ins[i], outs[i].at[me], local_sems.at[i])
            cp.start()
            local.append(cp)
            for k, (px, py) in enumerate(chips):
                cp = copy(half(ins[i], c), half(outs[i].at[me], c), i, k, (px, py, c))
                cp.start()
                sends.append(cp)
        for k, (px, py) in enumerate(chips):
            slot = 2 * px + py
            for i in range(n):
                landed = half(outs[i].at[slot], c)
                copy(half(ins[i], c), landed, i, k, (px, py, c)).wait_recv()
                cp = copy(landed, landed, i, 3 + k, sibling)
                cp.start()
                sends.append(cp)
        for k, (px, py) in enumerate(chips):
            slot = 2 * px + py
            for i in range(n):
                passed = half(outs[i].at[slot], 1 - c)
                copy(passed, passed, i, 3 + k, sibling).wait_recv()
        for cp in sends:
            cp.wait_send()
        for cp in local:
            cp.wait()

    hbm = pl.BlockSpec(memory_space=pl.ANY)
    outs = pl.pallas_call(
        body, name=name, out_shape=out_shape,
        in_specs=[hbm] * n, out_specs=[hbm] * n,
        scratch_shapes=[pltpu.SemaphoreType.DMA((n, 6)), pltpu.SemaphoreType.DMA((n, 6)),
                        pltpu.SemaphoreType.DMA((n,))],
        compiler_params=pltpu.CompilerParams(has_side_effects=True),
    )(*arrs)
    return list(outs)


HBM_SPEC = pl.BlockSpec(memory_space=pltpu.HBM)
SEM_SPEC = pl.BlockSpec(memory_space=pltpu.SEMAPHORE)
DATAFLOW = pltpu.SideEffectType.DATAFLOW_SIDE_EFFECTING


def _plane_peers():
    x, y, c = lax.axis_index("x"), lax.axis_index("y"), lax.axis_index("c")
    return 2 * x + y, [((1 - x, y, c), 2 * (1 - x) + y), ((x, 1 - y, c), 2 * x + 1 - y),
                       ((1 - x, 1 - y, c), 2 * (1 - x) + 1 - y)]


def _plane_start(name, arrs, mode, after):
    n = len(arrs)
    lands = [lax.empty(((4,) + a.shape) if mode == "gather" else a.shape, a.dtype) for a in arrs]

    def body(*refs):
        srcs, landing = refs[:n], refs[n:2 * n]
        send_sems, recv_sems = refs[2 * n + 1], refs[2 * n + 2]
        token = refs[-1]
        me, peers = _plane_peers()
        for i in range(n):
            for k, (peer, peer_slot) in enumerate(peers):
                src = srcs[i] if mode == "gather" else srcs[i].at[peer_slot]
                pltpu.make_async_remote_copy(
                    src_ref=src, dst_ref=landing[i].at[me], send_sem=send_sems.at[3 * i + k],
                    recv_sem=recv_sems.at[3 * i + k], device_id=peer, device_id_type=MESH).start()
        token[...] = jnp.zeros_like(token)

    hbm_in = [pltpu.with_memory_space_constraint(a, pltpu.HBM) for a in list(arrs) + lands]
    out = pl.pallas_call(
        body, name=name,
        out_shape=[pltpu.SemaphoreType.DMA((3 * n,)), pltpu.SemaphoreType.DMA((3 * n,))]
                  + [pltpu.HBM(a.shape, a.dtype) for a in hbm_in]
                  + [jax.ShapeDtypeStruct((8, LANES), F32)],
        in_specs=[HBM_SPEC] * (2 * n) + [pl.BlockSpec(memory_space=pl.ANY)],
        out_specs=[SEM_SPEC, SEM_SPEC] + [HBM_SPEC] * (2 * n) + [pl.BlockSpec(memory_space=pltpu.VMEM)],
        input_output_aliases={i: 2 + i for i in range(2 * n)},
        compiler_params=pltpu.CompilerParams(has_side_effects=DATAFLOW),
    )(*hbm_in, after)
    return out[:-1], out[-1]


def _plane_wait(name, state, mode, after):
    send_sems, recv_sems = state[0], state[1]
    bufs = list(state[2:])
    n = len(bufs) // 2

    def body(*refs):
        srcs, landing = refs[:n], refs[n:2 * n]
        send_sems, recv_sems = refs[2 * n], refs[2 * n + 1]
        me, peers = _plane_peers()
        for i in range(n):
            for k, (peer, peer_slot) in enumerate(peers):
                src = srcs[i] if mode == "gather" else srcs[i].at[peer_slot]
                cp = pltpu.make_async_remote_copy(
                    src_ref=src, dst_ref=landing[i].at[peer_slot], send_sem=send_sems.at[3 * i + k],
                    recv_sem=recv_sems.at[3 * i + k], device_id=peer, device_id_type=MESH)
                cp.wait_send()
                cp.wait_recv()

    out = pl.pallas_call(
        body, name=name,
        out_shape=[pltpu.HBM(a.shape, a.dtype) for a in bufs],
        in_specs=[HBM_SPEC] * (2 * n) + [SEM_SPEC, SEM_SPEC, pl.BlockSpec(memory_space=pl.ANY)],
        out_specs=[HBM_SPEC] * (2 * n),
        input_output_aliases={i: i for i in range(2 * n)},
        compiler_params=pltpu.CompilerParams(has_side_effects=DATAFLOW),
    )(*bufs, send_sems, recv_sems, after)
    return list(out[n:])


def _cast_bf16(a, name):
    rows, cols = a.shape
    rt = _pick_tile(rows, 256, 16)

    def body(a_ref, o_ref):
        o_ref[...] = a_ref[...].astype(BF16)

    return pl.pallas_call(
        body, name=name, grid=(rows // rt,),
        in_specs=[pl.BlockSpec((rt, cols), lambda i: (i, 0))],
        out_specs=pl.BlockSpec((rt, cols), lambda i: (i, 0)),
        out_shape=jax.ShapeDtypeStruct(a.shape, BF16),
        compiler_params=_params(("parallel",)),
    )(a)


def _sum_slots(buf, name, own=None):
    n_slots, rows, cols = buf.shape
    rt = _pick_tile(rows, 128, 16)
    n_in = 1 if own is None else 2

    def body(*refs):
        b_ref, o_ref = refs[0], refs[-1]
        me = None if own is None else 2 * lax.axis_index("x") + lax.axis_index("y")
        acc = None
        for s in range(n_slots):
            term = b_ref[s] if own is None else jnp.where(me == s, refs[1][s], b_ref[s])
            acc = term.astype(F32) if acc is None else acc + term.astype(F32)
        o_ref[...] = acc

    return pl.pallas_call(
        body, name=name, grid=(rows // rt,),
        in_specs=[pl.BlockSpec((n_slots, rt, cols), lambda i: (0, i, 0))] * n_in,
        out_specs=pl.BlockSpec((rt, cols), lambda i: (i, 0)),
        out_shape=jax.ShapeDtypeStruct((rows, cols), F32),
        compiler_params=_params(("parallel",), 48),
    )(*([buf] if own is None else [buf, own]))


def _adamw(w, grads, m, v, name):
    rows, cols = w.shape
    rt = _pick_tile(rows, 128, 8)
    n_g = len(grads)
    c1 = 1.0 - ADAM_B1 ** ADAM_STEP
    c2 = 1.0 - ADAM_B2 ** ADAM_STEP

    def body(*refs):
        w_ref = refs[0]
        g_refs = refs[1:1 + n_g]
        m_ref, v_ref, g_out, d_out, m_out, v_out = refs[1 + n_g:]
        g = g_refs[0][...]
        for r in g_refs[1:]:
            g = g + r[...]
        m_new = ADAM_B1 * m_ref[...] + (1.0 - ADAM_B1) * g
        v_new = ADAM_B2 * v_ref[...] + (1.0 - ADAM_B2) * (g * g)
        m_hat = m_new / c1
        v_hat = v_new / c2
        g_out[...] = g
        d_out[...] = -ADAM_LR * (m_hat / (jnp.sqrt(v_hat) + ADAM_EPS) + ADAM_WD * w_ref[...])
        m_out[...] = m_new
        v_out[...] = v_new

    spec = pl.BlockSpec((rt, cols), lambda i: (i, 0))
    shape = jax.ShapeDtypeStruct((rows, cols), F32)
    return pl.pallas_call(
        body, name=name, grid=(rows // rt,),
        in_specs=[spec] * (3 + n_g), out_specs=[spec] * 4, out_shape=[shape] * 4,
        compiler_params=_params(("parallel",), 48),
    )(w, *grads, m, v)


def _in_proj(h, g_pre, w_main, w_lr):
    t_rows = h.shape[0]
    tm = _pick_tile(t_rows, 384, LANES)
    n_main = w_main.shape[1]

    def body(h_ref, g_ref, w_hbm, wlr_ref, proj_ref, ut_ref, lr_ref, w_s, w_sem):
        @pl.when(pl.program_id(0) == 0)
        def _():
            cp = pltpu.make_async_copy(w_hbm, w_s, w_sem)
            cp.start()
            cp.wait()

        hh = h_ref[...]
        rstd = lax.rsqrt(jnp.mean(hh * hh, axis=-1, keepdims=True) + EPS)
        uf = hh * rstd * g_ref[...]
        u = uf.astype(BF16)
        ut_ref[...] = jnp.transpose(uf).astype(BF16)
        lr_ref[...] = _dot(u, wlr_ref[...])
        for j in range(n_main // D):
            cols = slice(j * D, (j + 1) * D)
            proj_ref[:, cols] = _dot(u, w_s[:, cols]).astype(BF16)

    return pl.pallas_call(
        body, name="in_proj", grid=(t_rows // tm,),
        in_specs=[pl.BlockSpec((tm, D), lambda i: (i, 0)),
                  pl.BlockSpec((1, D), lambda i: (0, 0)),
                  pl.BlockSpec(memory_space=pl.ANY),
                  pl.BlockSpec((D, LANES), lambda i: (0, 0))],
        out_specs=[pl.BlockSpec((tm, n_main), lambda i: (i, 0)),
                   pl.BlockSpec((D, tm), lambda i: (0, i)),
                   pl.BlockSpec((tm, LANES), lambda i: (i, 0))],
        out_shape=[jax.ShapeDtypeStruct((t_rows, n_main), BF16),
                   jax.ShapeDtypeStruct((D, t_rows), BF16),
                   jax.ShapeDtypeStruct((t_rows, LANES), F32)],
        scratch_shapes=[pltpu.VMEM((D, n_main), BF16), pltpu.SemaphoreType.DMA],
        compiler_params=_params(("arbitrary",), 56),
    )(h, g_pre, w_main, w_lr)


def _conv_parts(p_ref, w_ref):
    cb = p_ref[:, 0:128].astype(F32)
    cc = p_ref[:, 128:256].astype(F32)
    cx = p_ref[:, 256:384].astype(F32)
    cz = p_ref[:, 384:512].astype(F32)
    rows = cb.shape[0]
    w = w_ref[...]
    p = cc * cx
    conv = pltpu.roll(p, 1, 0) * w[0:1] + p * w[1:2] + pltpu.roll(p, rows - 1, 0) * w[2:3]
    sz = _sigmoid(cz)
    return cb, cc, cx, cz, p, conv, sz, w


def _conv_fwd(proj, conv_w, n_seq, lf):
    def body(p_ref, w_ref, y_ref):
        cb, _, _, cz, _, conv, sz, _ = _conv_parts(p_ref, w_ref)
        y_ref[...] = (cb * conv * (cz * sz)).astype(BF16)

    return pl.pallas_call(
        body, name="conv_fwd", grid=(n_seq, N_CONV_TILES),
        in_specs=[pl.BlockSpec((lf, 512), lambda b, j: (b, j)),
                  pl.BlockSpec((3, 128), lambda b, j: (0, j))],
        out_specs=pl.BlockSpec((lf, 128), lambda b, j: (b, j)),
        out_shape=jax.ShapeDtypeStruct((n_seq * lf, D), BF16),
        compiler_params=_params(("parallel", "parallel"), 48),
    )(proj, conv_w)


def _conv_bwd(proj, conv_w, dyc, n_seq, lf):
    def body(p_ref, w_ref, dy_ref, dp_ref, dw_ref):
        cb, cc, cx, cz, p, conv, sz, w = _conv_parts(p_ref, w_ref)
        rows = cb.shape[0]
        dy = dy_ref[...].astype(F32)
        silu = cz * sz
        dcb = dy * conv * silu
        dconv = dy * cb * silu
        dcz = dy * cb * conv * (sz * (1.0 + cz * (1.0 - sz)))
        d_next = pltpu.roll(dconv, rows - 1, 0)
        d_prev = pltpu.roll(dconv, 1, 0)
        dp = d_next * w[0:1] + dconv * w[1:2] + d_prev * w[2:3]
        dp_ref[:, 0:128] = dcb.astype(BF16)
        dp_ref[:, 128:256] = (dp * cx).astype(BF16)
        dp_ref[:, 256:384] = (dp * cc).astype(BF16)
        dp_ref[:, 384:512] = dcz.astype(BF16)
        dw_ref[0:1, :] = jnp.sum(dconv * pltpu.roll(p, 1, 0), axis=0, keepdims=True)
        dw_ref[1:2, :] = jnp.sum(dconv * p, axis=0, keepdims=True)
        dw_ref[2:3, :] = jnp.sum(dconv * pltpu.roll(p, rows - 1, 0), axis=0, keepdims=True)

    return pl.pallas_call(
        body, name="conv_bwd", grid=(n_seq, N_CONV_TILES),
        in_specs=[pl.BlockSpec((lf, 512), lambda b, j: (b, j)),
                  pl.BlockSpec((3, 128), lambda b, j: (0, j)),
                  pl.BlockSpec((lf, 128), lambda b, j: (b, j))],
        out_specs=[pl.BlockSpec((lf, 512), lambda b, j: (b, j)),
                   pl.BlockSpec((None, 3, 128), lambda b, j: (b, 0, j))],
        out_shape=[jax.ShapeDtypeStruct((n_seq * lf, W_CONV), BF16),
                   jax.ShapeDtypeStruct((n_seq, 3, D), F32)],
        compiler_params=_params(("parallel", "parallel"), 48),
    )(proj, conv_w, dyc)


def _chunk_unroll(n_chunks):
    for u in (11, 3):
        if n_chunks % u == 0:
            return u
    return 1


def _chunk_masks():
    row = lax.broadcasted_iota(jnp.int32, (CHUNK, CHUNK), 0)
    col = lax.broadcasted_iota(jnp.int32, (CHUNK, CHUNK), 1)
    lower = col <= row
    upper = col >= row
    strict_upper = col > row
    return lower, upper, strict_upper


def _gla_gates(lr_bf, wg_ref, bg_ref, lf):
    z = _dot(lr_bf, wg_ref[...]) + bg_ref[...]
    valid = lax.broadcasted_iota(jnp.int32, (lf, HEAD_K), 0) >= PAD_FRONT
    return z, valid


def _gla_states(direction, n_chunks, qkv_ref, g_s, b_s, st_s, tri):
    def local(c, carry):
        rows = pl.ds(pl.multiple_of(c * CHUNK, CHUNK), CHUNK)
        b = _tri_dot(tri, g_s[rows, :])
        b_s[rows, :] = b
        b_end = b[CHUNK - 1:CHUNK, :] if direction == 0 else b[0:1, :]
        k = qkv_ref[rows, 128:256].astype(F32)
        v = qkv_ref[rows, 256:512]
        k_dec = (k * jnp.exp(b_end - b)).astype(BF16)
        st_s[c] = _dot_tn(v, k_dec)
        return carry

    lax.fori_loop(0, n_chunks, local, 0, unroll=_chunk_unroll(n_chunks))

    def scan(i, state):
        c = i if direction == 0 else n_chunks - 1 - i
        decay = _chunk_decay(direction, c, b_s)
        update = st_s[c]
        st_s[c] = state
        return state * decay + update

    lax.fori_loop(0, n_chunks, scan, jnp.zeros((HEAD_V, HEAD_K), F32))


def _chunk_decay(direction, c, b_s):
    if direction == 0:
        grp = b_s[pl.ds(pl.multiple_of(c * CHUNK + CHUNK - 8, 8), 8), :]
        return jnp.exp(grp[7:8, :])
    grp = b_s[pl.ds(pl.multiple_of(c * CHUNK, 8), 8), :]
    return jnp.exp(grp[0:1, :])


def _gla_fwd(proj, lr, wgf, wgb, bgf, bgb, n_seq, lf):
    n_chunks = lf // CHUNK
    scale = HEAD_K ** -0.5

    def body(qkv_ref, lr_ref, wgf_ref, wgb_ref, bgf_ref, bgb_ref, o_ref, g_s, b_s, st_s):
        lower, upper, strict_upper = _chunk_masks()
        lr_bf = lr_ref[...].astype(BF16)
        for direction in (0, 1):
            wg_ref, bg_ref = ((wgf_ref, bgf_ref), (wgb_ref, bgb_ref))[direction]
            z, valid = _gla_gates(lr_bf, wg_ref, bg_ref, lf)
            g_s[...] = jnp.where(valid, _log_sigmoid(z) / GATE_NORM, 0.0)
            tri = (lower if direction == 0 else upper).astype(BF16)
            smask = lower if direction == 0 else strict_upper
            _gla_states(direction, n_chunks, qkv_ref, g_s, b_s, st_s, tri)

            def out(c, carry):
                rows = pl.ds(pl.multiple_of(c * CHUNK, CHUNK), CHUNK)
                b = b_s[rows, :]
                q = qkv_ref[rows, 0:128].astype(F32) * scale
                k = qkv_ref[rows, 128:256].astype(F32)
                v = qkv_ref[rows, 256:512]
                q_in = (q * jnp.exp(b)).astype(BF16)
                k_in = (k * jnp.exp(-b)).astype(BF16)
                s = jnp.where(smask, _dot_nt(q_in, k_in), 0.0).astype(BF16)
                o = _dot(s, v) + _dot_nt(q_in, st_s[c].astype(BF16))
                if direction == 0:
                    o_ref[rows, :] = o
                else:
                    o_ref[rows, :] = o_ref[rows, :] + o
                return carry

            lax.fori_loop(0, n_chunks, out, 0, unroll=_chunk_unroll(n_chunks))

    return pl.pallas_call(
        body, name="gla_fwd", grid=(n_seq, N_HEADS),
        in_specs=[pl.BlockSpec((lf, 512), lambda b, h: (b, N_CONV_TILES + h)),
                  pl.BlockSpec((lf, LANES), lambda b, h: (b, 0)),
                  pl.BlockSpec((None, LANES, HEAD_K), lambda b, h: (h, 0, 0)),
                  pl.BlockSpec((None, LANES, HEAD_K), lambda b, h: (h, 0, 0)),
                  pl.BlockSpec((None, 1, HEAD_K), lambda b, h: (h, 0, 0)),
                  pl.BlockSpec((None, 1, HEAD_K), lambda b, h: (h, 0, 0))],
        out_specs=pl.BlockSpec((lf, HEAD_V), lambda b, h: (b, h)),
        out_shape=jax.ShapeDtypeStruct((n_seq * lf, D), F32),
        scratch_shapes=[pltpu.VMEM((lf, HEAD_K), F32), pltpu.VMEM((lf, HEAD_K), F32),
                        pltpu.VMEM((n_chunks, HEAD_V, HEAD_K), F32)],
        compiler_params=_params(("parallel", "parallel"), 48),
    )(proj, lr, wgf, wgb, bgf, bgb)


def _gla_bwd(proj, lr, d_o, wgf, wgb, bgf, bgb, n_seq, lf):
    n_chunks = lf // CHUNK
    scale = HEAD_K ** -0.5

    def body(qkv_ref, lr_ref, do_ref, wgf_ref, wgb_ref, bgf_ref, bgb_ref,
             dqkv_ref, dlr_ref, dwgf_ref, dwgb_ref, dbg_ref,
             g_s, b_s, fac_s, dg_s, st_s, dst_s, acc_s):
        lower, upper, strict_upper = _chunk_masks()
        lr_bf = lr_ref[...].astype(BF16)
        acc_s[...] = jnp.zeros_like(acc_s)
        dlr = jnp.zeros((lf, LANES), F32)
        for direction in (0, 1):
            wg_ref, bg_ref = ((wgf_ref, bgf_ref), (wgb_ref, bgb_ref))[direction]
            z, valid = _gla_gates(lr_bf, wg_ref, bg_ref, lf)
            g_s[...] = jnp.where(valid, _log_sigmoid(z) / GATE_NORM, 0.0)
            fac_s[...] = jnp.where(valid, _sigmoid(-z) / GATE_NORM, 0.0)
            tri = (lower if direction == 0 else upper).astype(BF16)
            tri_t = (upper if direction == 0 else lower).astype(BF16)
            smask = lower if direction == 0 else strict_upper
            end_row = CHUNK - 1 if direction == 0 else 0
            _gla_states(direction, n_chunks, qkv_ref, g_s, b_s, st_s, tri)

            def state_grad_local(c, carry):
                rows = pl.ds(pl.multiple_of(c * CHUNK, CHUNK), CHUNK)
                q = qkv_ref[rows, 0:128].astype(F32) * scale
                q_in = (q * jnp.exp(b_s[rows, :])).astype(BF16)
                dst_s[c] = _dot_tn(do_ref[rows, :], q_in)
                return carry

            lax.fori_loop(0, n_chunks, state_grad_local, 0, unroll=_chunk_unroll(n_chunks))

            def state_grad_scan(i, grad):
                c = n_chunks - 1 - i if direction == 0 else i
                decay = _chunk_decay(direction, c, b_s)
                local = dst_s[c]
                dst_s[c] = grad
                return local + grad * decay

            lax.fori_loop(0, n_chunks, state_grad_scan, jnp.zeros((HEAD_V, HEAD_K), F32))

            def chunk_grads(c, carry):
                rows = pl.ds(pl.multiple_of(c * CHUNK, CHUNK), CHUNK)
                b = b_s[rows, :]
                b_end = b[end_row:end_row + 1, :]
                q = qkv_ref[rows, 0:128].astype(F32) * scale
                k = qkv_ref[rows, 128:256].astype(F32)
                v = qkv_ref[rows, 256:512]
                d_out = do_ref[rows, :]
                e_pos = jnp.exp(b)
                e_neg = jnp.exp(-b)
                e_end = jnp.exp(b_end - b)
                q_in = q * e_pos
                k_in = k * e_neg
                k_dec = k * e_end
                q_in_bf = q_in.astype(BF16)
                k_in_bf = k_in.astype(BF16)
                state = st_s[c]
                d_state = dst_s[c]
                state_bf = state.astype(BF16)
                d_state_bf = d_state.astype(BF16)
                s = jnp.where(smask, _dot_nt(q_in_bf, k_in_bf), 0.0).astype(BF16)
                ds = jnp.where(smask, _dot_nt(d_out, v), 0.0).astype(BF16)
                dv = _dot_tn(s, d_out) + _dot_nt(k_dec.astype(BF16), d_state_bf)
                dq_in = _dot(ds, k_in_bf) + _dot(d_out, state_bf)
                dk_in = _dot_tn(ds, q_in_bf)
                dk_dec = _dot(v, d_state_bf)
                acc_s[rows, 0:128] = acc_s[rows, 0:128] + dq_in * e_pos * scale
                acc_s[rows, 128:256] = acc_s[rows, 128:256] + dk_in * e_neg + dk_dec * e_end
                acc_s[rows, 256:512] = acc_s[rows, 256:512] + dv
                dkk = dk_dec * k_dec
                db = dq_in * q_in - dk_in * k_in - dkk
                d_decay = jnp.sum(d_state * state, axis=0, keepdims=True)
                db_end = jnp.sum(dkk, axis=0, keepdims=True) + d_decay * jnp.exp(b_end)
                at_end = lax.broadcasted_iota(jnp.int32, (CHUNK, HEAD_K), 0) == end_row
                db = db + jnp.where(at_end, db_end, 0.0)
                dg_s[rows, :] = _tri_dot(tri_t, db)
                return carry

            lax.fori_loop(0, n_chunks, chunk_grads, 0, unroll=_chunk_unroll(n_chunks))

            dz = dg_s[...] * fac_s[...]
            dz_bf = dz.astype(BF16)
            dbg_ref[direction:direction + 1, :] = jnp.sum(dz, axis=0, keepdims=True)
            (dwgf_ref, dwgb_ref)[direction][...] = _dot_tn(lr_bf, dz_bf)
            dlr = dlr + _dot_nt(dz_bf, wg_ref[...])

        dqkv_ref[...] = acc_s[...].astype(BF16)

        @pl.when(pl.program_id(1) == 0)
        def _():
            dlr_ref[...] = dlr

        @pl.when(pl.program_id(1) != 0)
        def _():
            dlr_ref[...] = dlr_ref[...] + dlr

    gate_w = pl.BlockSpec((None, LANES, HEAD_K), lambda b, h: (h, 0, 0))
    gate_b = pl.BlockSpec((None, 1, HEAD_K), lambda b, h: (h, 0, 0))
    return pl.pallas_call(
        body, name="gla_bwd", grid=(n_seq, N_HEADS),
        in_specs=[pl.BlockSpec((lf, 512), lambda b, h: (b, N_CONV_TILES + h)),
                  pl.BlockSpec((lf, LANES), lambda b, h: (b, 0)),
                  pl.BlockSpec((lf, HEAD_V), lambda b, h: (b, h)),
                  gate_w, gate_w, gate_b, gate_b],
        out_specs=[pl.BlockSpec((lf, 512), lambda b, h: (b, h)),
                   pl.BlockSpec((lf, LANES), lambda b, h: (b, 0)),
                   pl.BlockSpec((None, None, LANES, HEAD_K), lambda b, h: (b, h, 0, 0)),
                   pl.BlockSpec((None, None, LANES, HEAD_K), lambda b, h: (b, h, 0, 0)),
                   pl.BlockSpec((None, None, 2, HEAD_K), lambda b, h: (b, h, 0, 0))],
        out_shape=[jax.ShapeDtypeStruct((n_seq * lf, W_GLA), BF16),
                   jax.ShapeDtypeStruct((n_seq * lf, LANES), F32),
                   jax.ShapeDtypeStruct((n_seq, N_HEADS, LANES, HEAD_K), F32),
                   jax.ShapeDtypeStruct((n_seq, N_HEADS, LANES, HEAD_K), F32),
                   jax.ShapeDtypeStruct((n_seq, N_HEADS, 2, HEAD_K), F32)],
        scratch_shapes=[pltpu.VMEM((lf, HEAD_K), F32), pltpu.VMEM((lf, HEAD_K), F32),
                        pltpu.VMEM((lf, HEAD_K), F32), pltpu.VMEM((lf, HEAD_K), F32),
                        pltpu.VMEM((n_chunks, HEAD_V, HEAD_K), F32),
                        pltpu.VMEM((n_chunks, HEAD_V, HEAD_K), F32),
                        pltpu.VMEM((lf, 512), F32)],
        compiler_params=_params(("parallel", "arbitrary"), 56),
    )(proj, lr, d_o, wgf, wgb, bgf, bgb)


def _tail(h, tgt, yc, o, proj, w3, gamma, g_post, lf):
    t_rows = h.shape[0]
    tm = _pick_tile(t_rows, 256, CHUNK)
    n_chunks = lf // CHUNK
    per_tile = tm // CHUNK

    def body(h_ref, tgt_ref, yc_ref, o_ref, r_ref, ma_ref, mb_ref, w_hbm, gamma_ref, gpost_ref,
             dres_ref, yg_ref, merged_ref, dout_ref, dpc_ref, dpg_ref, dyc_ref, do_ref, dtail_ref,
             loss_ref, dgpost_ref, dgamma_ref, w_s, w_sem):
        i = pl.program_id(0)

        @pl.when(i == 0)
        def _():
            cp = pltpu.make_async_copy(w_hbm, w_s, w_sem)
            cp.start()
            cp.wait()
            loss_ref[...] = jnp.zeros_like(loss_ref)
            dgpost_ref[...] = jnp.zeros_like(dgpost_ref)
            dgamma_ref[...] = jnp.zeros_like(dgamma_ref)

        gamma = gamma_ref[...]
        o = o_ref[...]
        r = r_ref[...].astype(F32)
        sr = _sigmoid(r)
        silu_r = r * sr
        n_parts, rstd_parts = [], []
        for hd in range(N_HEADS):
            oh = o[:, hd * HEAD_V:(hd + 1) * HEAD_V]
            rstd = lax.rsqrt(jnp.mean(oh * oh, axis=-1, keepdims=True) + EPS)
            n_parts.append(oh * rstd)
            rstd_parts.append(rstd)
        n = jnp.concatenate(n_parts, axis=-1)
        gamma_t = jnp.concatenate([gamma] * N_HEADS, axis=-1)
        yg = n * gamma_t * silu_r
        yg_bf = yg.astype(BF16)
        yg_ref[...] = yg_bf
        yc = yc_ref[...]
        pc = _dot(yc, w_s[0])
        pg = _dot(yg_bf, w_s[1])
        sa = _sigmoid(ma_ref[...].astype(F32))
        sb = _sigmoid(mb_ref[...].astype(F32))
        merged = (sa * pc + sb * pg).astype(BF16)
        merged_ref[...] = merged
        out = _dot(merged, w_s[2])
        rstd2 = lax.rsqrt(jnp.mean(out * out, axis=-1, keepdims=True) + EPS)
        nn = out * rstd2
        gpost = gpost_ref[...]
        y = h_ref[...] + nn * gpost

        rowi = lax.broadcasted_iota(jnp.int32, (tm, 1), 0)
        keep = jnp.zeros((tm, 1), F32)
        for kk in range(per_tile):
            is_tok = ((i * per_tile + kk) % n_chunks) != 0
            f = jnp.where(is_tok, 1.0, 0.0)
            keep = jnp.where((rowi >= kk * CHUNK) & (rowi < (kk + 1) * CHUNK), f, keep)
        diff = (y - tgt_ref[...]) * keep
        loss_ref[...] += jnp.sum(diff * diff) * (0.5 / D)
        dy = diff * (1.0 / D)
        dres_ref[...] = dy
        dgpost_ref[...] += jnp.sum(dy * nn, axis=0, keepdims=True)
        dn = dy * gpost
        dout = (rstd2 * (dn - nn * jnp.mean(dn * nn, axis=-1, keepdims=True))).astype(BF16)
        dout_ref[...] = dout
        dmerged = _dot_nt(dout, w_s[2])
        dpc = (dmerged * sa).astype(BF16)
        dpg = (dmerged * sb).astype(BF16)
        dpc_ref[...] = dpc
        dpg_ref[...] = dpg
        dtail_ref[:, D:2 * D] = (dmerged * pc * (sa * (1.0 - sa))).astype(BF16)
        dtail_ref[:, 2 * D:3 * D] = (dmerged * pg * (sb * (1.0 - sb))).astype(BF16)
        dyc_ref[...] = _dot_nt(dpc, w_s[0]).astype(BF16)
        dyg = _dot_nt(dpg, w_s[1])
        dtail_ref[:, 0:D] = (dyg * n * gamma_t * (sr * (1.0 + r * (1.0 - sr)))).astype(BF16)
        dgam_full = jnp.sum(dyg * n * silu_r, axis=0, keepdims=True)
        dgam = dgam_full[:, 0:HEAD_V]
        for hd in range(1, N_HEADS):
            dgam = dgam + dgam_full[:, hd * HEAD_V:(hd + 1) * HEAD_V]
        dgamma_ref[...] += dgam
        dng = dyg * gamma_t * silu_r
        do_parts = []
        for hd in range(N_HEADS):
            sl = slice(hd * HEAD_V, (hd + 1) * HEAD_V)
            dnh = dng[:, sl]
            nh = n_parts[hd]
            do_parts.append(rstd_parts[hd] * (dnh - nh * jnp.mean(dnh * nh, axis=-1, keepdims=True)))
        do_ref[...] = jnp.concatenate(do_parts, axis=-1).astype(BF16)

    row = lambda c: pl.BlockSpec((tm, D), lambda i: (i, c))
    const = lambda shape: pl.BlockSpec(shape, lambda i: (0, 0))
    act = jax.ShapeDtypeStruct((t_rows, D), BF16)
    return pl.pallas_call(
        body, name="tail", grid=(t_rows // tm,),
        in_specs=[row(0), row(0), row(0), row(0), row(6), row(7), row(8),
                  pl.BlockSpec(memory_space=pl.ANY), const((1, HEAD_V)), const((1, D))],
        out_specs=[row(0)] * 8 + [pl.BlockSpec((tm, W_TAIL), lambda i: (i, 0)),
                                  const((8, LANES)), const((1, D)), const((1, HEAD_V))],
        out_shape=[jax.ShapeDtypeStruct((t_rows, D), F32)] + [act] * 7
                  + [jax.ShapeDtypeStruct((t_rows, W_TAIL), BF16),
                     jax.ShapeDtypeStruct((8, LANES), F32),
                     jax.ShapeDtypeStruct((1, D), F32),
                     jax.ShapeDtypeStruct((1, HEAD_V), F32)],
        scratch_shapes=[pltpu.VMEM((3, D, D), BF16), pltpu.SemaphoreType.DMA],
        compiler_params=_params(("arbitrary",), 56),
    )(h, tgt, yc, o, proj, proj, proj, w3, gamma, g_post)


def _wgrad(a, b, name, out_dtype=BF16):
    t_rows, m = a.shape
    n = b.shape[1]
    tn = D if n % D == 0 else n
    tk = _pick_tile(t_rows, 528, 16)
    n_k = t_rows // tk

    def body(a_ref, b_ref, o_ref, acc):
        k = pl.program_id(1)

        @pl.when(k == 0)
        def _():
            acc[...] = jnp.zeros_like(acc)

        acc[...] += _dot_tn(a_ref[...].astype(BF16), b_ref[...].astype(BF16))

        @pl.when(k == n_k - 1)
        def _():
            o_ref[...] = acc[...].astype(out_dtype)

    return pl.pallas_call(
        body, name=name, grid=(n // tn, n_k),
        in_specs=[pl.BlockSpec((tk, m), lambda j, k: (k, 0)),
                  pl.BlockSpec((tk, tn), lambda j, k: (k, j))],
        out_specs=pl.BlockSpec((m, tn), lambda j, k: (0, j)),
        out_shape=jax.ShapeDtypeStruct((m, n), out_dtype),
        scratch_shapes=[pltpu.VMEM((m, tn), F32)],
        compiler_params=_params(("parallel", "arbitrary"), 48),
    )(a, b)


def _wgrad_t(a_t, b, name, out_dtype=BF16):
    m, t_rows = a_t.shape
    n = b.shape[1]
    tn = D if n % D == 0 else n
    tk = _pick_tile(t_rows, 768, LANES)
    n_k = t_rows // tk

    def body(a_ref, b_ref, o_ref, acc):
        k = pl.program_id(1)

        @pl.when(k == 0)
        def _():
            acc[...] = jnp.zeros_like(acc)

        acc[...] += _dot(a_ref[...], b_ref[...].astype(BF16))

        @pl.when(k == n_k - 1)
        def _():
            o_ref[...] = acc[...].astype(out_dtype)

    return pl.pallas_call(
        body, name=name, grid=(n // tn, n_k),
        in_specs=[pl.BlockSpec((m, tk), lambda j, k: (0, k)),
                  pl.BlockSpec((tk, tn), lambda j, k: (k, j))],
        out_specs=pl.BlockSpec((m, tn), lambda j, k: (0, j)),
        out_shape=jax.ShapeDtypeStruct((m, n), out_dtype),
        scratch_shapes=[pltpu.VMEM((m, tn), F32)],
        compiler_params=_params(("parallel", "arbitrary"), 48),
    )(a_t, b)


def _dgrad_in(dpc, dpg, dpt, dlr, w_main, w_lr, h, g_pre, dres, token):
    t_rows = h.shape[0]
    tm = _pick_tile(t_rows, 256, 16)
    n_main = w_main.shape[1]

    def body(dpc_ref, dpg_ref, dpt_ref, dlr_ref, w_hbm, wlr_ref, h_ref, g_ref, dres_ref, token_ref,
             dh_ref, dg_ref, w_s, w_sem):
        @pl.when(pl.program_id(0) == 0)
        def _():
            cp = pltpu.make_async_copy(w_hbm, w_s, w_sem)
            cp.start()
            cp.wait()
            dg_ref[...] = jnp.zeros_like(dg_ref)

        du = _dot_nt(dlr_ref[...].astype(BF16), wlr_ref[...])
        du += _dot_nt(dpc_ref[...], w_s[:, 0:W_CONV])
        du += _dot_nt(dpg_ref[...], w_s[:, W_CONV:W_CONV + W_GLA])
        du += _dot_nt(dpt_ref[...], w_s[:, W_CONV + W_GLA:n_main])
        hh = h_ref[...]
        rstd = lax.rsqrt(jnp.mean(hh * hh, axis=-1, keepdims=True) + EPS)
        xhat = hh * rstd
        dg_ref[...] += jnp.sum(du * xhat, axis=0, keepdims=True)
        dx = du * g_ref[...]
        dh_ref[...] = rstd * (dx - xhat * jnp.mean(dx * xhat, axis=-1, keepdims=True)) + dres_ref[...]

    row = lambda width: pl.BlockSpec((tm, width), lambda i: (i, 0))
    return pl.pallas_call(
        body, name="dgrad_in", grid=(t_rows // tm,),
        in_specs=[row(W_CONV), row(W_GLA), row(W_TAIL), row(LANES),
                  pl.BlockSpec(memory_space=pl.ANY),
                  pl.BlockSpec((D, LANES), lambda i: (0, 0)),
                  row(D), pl.BlockSpec((1, D), lambda i: (0, 0)), row(D),
                  pl.BlockSpec((8, LANES), lambda i: (0, 0))],
        out_specs=[row(D), pl.BlockSpec((1, D), lambda i: (0, 0))],
        out_shape=[jax.ShapeDtypeStruct((t_rows, D), F32), jax.ShapeDtypeStruct((1, D), F32)],
        scratch_shapes=[pltpu.VMEM((D, n_main), BF16), pltpu.SemaphoreType.DMA],
        compiler_params=_params(("arbitrary",), 56),
    )(dpc, dpg, dpt, dlr, w_main, w_lr, h, g_pre, dres, token)


OFF_CB, OFF_CC, OFF_CX, OFF_CZ = 0, 1024, 2048, 3072
OFF_Q, OFF_K, OFF_V, OFF_R = 4096, 4608, 5120, 6144
OFF_LR, OFF_MA, OFF_MB = 7168, 7200, 8224


def _main_columns(w):
    rows = w.shape[0]
    conv = w[:, 0:W_CONV].reshape(rows, 4, N_CONV_TILES, 128).transpose(0, 2, 1, 3).reshape(rows, W_CONV)
    q = w[:, OFF_Q:OFF_K].reshape(rows, N_HEADS, HEAD_K)
    k = w[:, OFF_K:OFF_V].reshape(rows, N_HEADS, HEAD_K)
    v = w[:, OFF_V:OFF_R].reshape(rows, N_HEADS, HEAD_V)
    gla = jnp.concatenate([q, k, v], axis=2).reshape(rows, W_GLA)
    return jnp.concatenate([conv, gla, w[:, OFF_R:OFF_R + D], w[:, OFF_MA:OFF_MA + 2 * D]], axis=1)


def _reference_columns(g_conv, g_gla, g_tail, g_lr):
    rows = g_conv.shape[0]
    conv = g_conv.reshape(rows, N_CONV_TILES, 4, 128).transpose(0, 2, 1, 3).reshape(rows, W_CONV)
    gla = g_gla.reshape(rows, N_HEADS, 512)
    q = gla[:, :, 0:128].reshape(rows, N_HEADS * HEAD_K)
    k = gla[:, :, 128:256].reshape(rows, N_HEADS * HEAD_K)
    v = gla[:, :, 256:512].reshape(rows, N_HEADS * HEAD_V)
    return jnp.concatenate([conv, q, k, v, g_tail[:, 0:D], g_lr[:, 0:2 * RANK], g_tail[:, D:3 * D]], axis=1)


def _pack(arrs, rows):
    flat = jnp.concatenate([a.reshape(-1) for a in arrs])
    return jnp.pad(flat, (0, rows * LANES - flat.shape[0])).reshape(rows, LANES)


def _unpack(packed, shapes):
    flat = packed.reshape(-1)
    out, pos = [], 0
    for s in shapes:
        size = 1
        for d in s:
            size *= d
        out.append(flat[pos:pos + size].reshape(s))
        pos += size
    return out


def _rows_for(shapes, mult=8):
    total = 0
    for s in shapes:
        size = 1
        for d in s:
            size *= d
        total += size
    return -(-total // (mult * LANES)) * mult


def kernel(x, meta_tokens, norm_pre, w_in, conv_w, w_gate_fwd, b_gate_fwd, w_gate_bwd, b_gate_bwd, gla_norm, w_out_conv, w_out_gla, w_merge_out, norm_post, loss_target, m_meta_tokens, m_norm_pre, m_w_in, m_conv_w, m_w_gate_fwd, m_b_gate_fwd, m_w_gate_bwd, m_b_gate_bwd, m_gla_norm, m_w_out_conv, m_w_out_gla, m_w_merge_out, m_norm_post, v_meta_tokens, v_norm_pre, v_w_in, v_conv_w, v_w_gate_fwd, v_b_gate_fwd, v_w_gate_bwd, v_b_gate_bwd, v_gla_norm, v_w_out_conv, v_w_out_gla, v_w_merge_out, v_norm_post):
    n_seq, seq, _ = x.shape
    lf = CHUNK + seq
    t_rows = n_seq * lf
    shard = 2 * lax.axis_index("x") + lax.axis_index("y")

    w_in_bf = _cast_bf16(w_in[0], "cast_w_in")
    w_out_bf = _cast_bf16(jnp.concatenate([w_out_conv[0], w_out_gla[0], w_merge_out[0]], axis=0), "cast_w_out")
    small_shapes = [(N_META, D // 4), (3, D // 4), (RANK, HEAD_K), (RANK, HEAD_K)]
    small = _pack([meta_tokens, conv_w[0], w_gate_fwd[0], w_gate_bwd[0]], _rows_for(small_shapes, 16))
    w_in_all, small_all = _gather_via_sibling("gather_w_in", [w_in_bf, small])
    w_out_state, _ = _plane_start("gather_w_out_start", [w_out_bf], "gather", small_all)

    w_full = jnp.transpose(w_in_all, (1, 0, 2)).reshape(D, N_IN)
    w_main = _main_columns(w_full)
    w_lr = jnp.pad(w_full[:, OFF_LR:OFF_LR + 2 * RANK], ((0, 0), (0, LANES - 2 * RANK)))
    smalls = [_unpack(small_all[s], small_shapes) for s in range(4)]
    meta_full = jnp.concatenate([smalls[s][0] for s in range(4)], axis=1)
    conv_full = jnp.concatenate([smalls[s][1] for s in range(4)], axis=1)
    wgf = jnp.stack([jnp.pad(smalls[s][2], ((0, LANES - RANK), (0, 0))) for s in range(4)]).astype(BF16)
    wgb = jnp.stack([jnp.pad(smalls[s][3], ((RANK, LANES - 2 * RANK), (0, 0))) for s in range(4)]).astype(BF16)
    bgf = b_gate_fwd.reshape(N_HEADS, 1, HEAD_K)
    bgb = b_gate_bwd.reshape(N_HEADS, 1, HEAD_K)

    head = jnp.concatenate([jnp.zeros((PAD_FRONT, D), F32), meta_full], axis=0)
    h = jnp.concatenate([jnp.broadcast_to(head[None], (n_seq, CHUNK, D)), x], axis=1).reshape(t_rows, D)
    tgt = jnp.pad(loss_target, ((0, 0), (CHUNK, 0), (0, 0))).reshape(t_rows, D)

    proj, u_t, lr = _in_proj(h, norm_pre, w_main, w_lr)
    yc = _conv_fwd(proj, conv_full, n_seq, lf)
    o = _gla_fwd(proj, lr, wgf, wgb, bgf, bgb, n_seq, lf)
    (w_out_landed,) = _plane_wait("gather_w_out_wait", w_out_state, "gather", o)
    slot_ids = lax.broadcasted_iota(jnp.int32, (4, 1, 1), 0)
    w_out_all = jnp.where(slot_ids == shard, w_out_bf[None], w_out_landed)
    w3 = jnp.transpose(w_out_all.reshape(4, 3, D // 4, D), (1, 0, 2, 3)).reshape(3, D, D)
    (dres, yg, merged, dout, dpc_out, dpg_out, dyc, d_o, dtail, loss_acc, d_gpost, d_gamma) = _tail(
        h, tgt, yc, o, proj, w3, gla_norm, norm_post, lf)
    g_w_oc = _wgrad(yc, dpc_out, "wgrad_out_conv")
    g_w_og = _wgrad(yg, dpg_out, "wgrad_out_gla")
    g_w_mo = _wgrad(merged, dout, "wgrad_merge_out")
    dgla, dlr, dwgf_p, dwgb_p, dbg_p = _gla_bwd(proj, lr, d_o, wgf, wgb, bgf, bgb, n_seq, lf)
    dconv, dconvw_p = _conv_bwd(proj, conv_full, dyc, n_seq, lf)
    g_conv = _wgrad_t(u_t, dconv, "wgrad_in_conv")
    g_gla = _wgrad_t(u_t, dgla, "wgrad_in_gla")
    g_tail = _wgrad_t(u_t, dtail, "wgrad_in_tail")
    g_lr = _wgrad_t(u_t, dlr, "wgrad_in_lr")

    g_in_full = _reference_columns(g_conv, g_gla, g_tail, g_lr)
    g_in_slots = jnp.transpose(g_in_full.reshape(D, 4, SHARD_IN), (1, 0, 2))
    g_out_slots = jnp.concatenate([g.reshape(4, D // 4, D) for g in (g_w_oc, g_w_og, g_w_mo)], axis=1)
    grads_state, grads_token = _plane_start("scatter_grads_start", [g_in_slots, g_out_slots], "scatter", g_lr)
    dh, d_gpre = _dgrad_in(dconv, dgla, dtail, dlr, w_main, w_lr, h, norm_pre, dres, grads_token)
    got_in, got_out = _plane_wait("scatter_grads_wait", grads_state, "scatter", d_gpre)

    dh3 = dh.reshape(n_seq, lf, D)
    grad_x = dh3[:, CHUNK:, :]

    d_meta = jnp.sum(dh3[:, PAD_FRONT:CHUNK, :], axis=0)
    d_convw = jnp.sum(dconvw_p, axis=0)
    d_wgf = jnp.transpose(jnp.sum(dwgf_p, axis=0)[:, 0:RANK, :], (1, 0, 2)).reshape(RANK, N_HEADS * HEAD_K)
    d_wgb = jnp.transpose(jnp.sum(dwgb_p, axis=0)[:, RANK:2 * RANK, :], (1, 0, 2)).reshape(RANK, N_HEADS * HEAD_K)
    d_bg = jnp.sum(dbg_p, axis=0)
    d_bgf = d_bg[:, 0, :].reshape(1, N_HEADS * HEAD_K)
    d_bgb = d_bg[:, 1, :].reshape(1, N_HEADS * HEAD_K)
    part_shapes = [(N_META, D), (3, D), (RANK, 512), (RANK, 512), (1, D), (1, 512), (1, 512), (1, HEAD_V),
                   (1, D), (1, LANES)]
    parts = _pack([d_meta, d_convw, d_wgf, d_wgb, d_gpre, d_bgf, d_bgb, d_gamma, d_gpost, loss_acc[0:1, :]],
                  _rows_for(part_shapes))
    (parts_all,) = _exchange("gather_small_grads", [parts], ALL_FLIPS, (4, 2, 1), "gather")
    (g_meta, g_convw, g_wgf, g_wgb, g_npre, g_bgf, g_bgb, g_gnorm, g_npost, loss_row) = _unpack(
        _sum_slots(parts_all, "sum_small_grads"), part_shapes)
    loss = loss_row[0, 0]

    def col_shard(a, width):
        return lax.dynamic_slice_in_dim(a, shard * width, width, axis=a.ndim - 1)

    upd_shapes = [(N_META, D // 4), (3, D // 4), (RANK, HEAD_K), (RANK, HEAD_K), (1, D), (1, 512), (1, 512),
                  (1, HEAD_V), (1, D)]
    upd_rows = _rows_for(upd_shapes)
    small_w = _pack([meta_tokens, conv_w[0], w_gate_fwd[0], w_gate_bwd[0], norm_pre, b_gate_fwd, b_gate_bwd,
                     gla_norm, norm_post], upd_rows)
    small_g = _pack([col_shard(g_meta, D // 4), col_shard(g_convw, D // 4), col_shard(g_wgf, HEAD_K),
                     col_shard(g_wgb, HEAD_K), g_npre, g_bgf, g_bgb, g_gnorm, g_npost], upd_rows)
    small_m = _pack([m_meta_tokens, m_conv_w[0], m_w_gate_fwd[0], m_w_gate_bwd[0], m_norm_pre, m_b_gate_fwd,
                     m_b_gate_bwd, m_gla_norm, m_norm_post], upd_rows)
    small_v = _pack([v_meta_tokens, v_conv_w[0], v_w_gate_fwd[0], v_w_gate_bwd[0], v_norm_pre, v_b_gate_fwd,
                     v_b_gate_bwd, v_gla_norm, v_norm_post], upd_rows)
    small_out = [_unpack(a, upd_shapes) for a in _adamw(small_w, [small_g], small_m, small_v, "adamw_small")]

    plane_in = _sum_slots(got_in, "sum_w_in_grads", own=g_in_slots)
    plane_out = _sum_slots(got_out, "sum_w_out_grads", own=g_out_slots)
    other_in, other_out = _exchange("swap_plane_sums", [plane_in, plane_out], SIBLING_FLIPS, (0, 0, 0), "swap")
    big_in = _adamw(w_in[0], [plane_in, other_in], m_w_in[0], v_w_in[0], "adamw_w_in")
    w_out_rows = jnp.concatenate([w_out_conv[0], w_out_gla[0], w_merge_out[0]], axis=0)
    m_out_rows = jnp.concatenate([m_w_out_conv[0], m_w_out_gla[0], m_w_merge_out[0]], axis=0)
    v_out_rows = jnp.concatenate([v_w_out_conv[0], v_w_out_gla[0], v_w_merge_out[0]], axis=0)
    big_out = _adamw(w_out_rows, [plane_out, other_out], m_out_rows, v_out_rows, "adamw_w_out")

    results = []
    for kind in range(4):
        sm = small_out[kind]
        w_in_part = big_in[kind][None]
        outs3 = big_out[kind].reshape(3, 1, D // 4, D)
        results.extend([
            sm[0], sm[4], w_in_part, sm[1][None], sm[2][None], sm[5], sm[3][None], sm[6], sm[7],
            outs3[0], outs3[1], outs3[2], sm[8]])
    return (loss, grad_x, *results)
```

```python
import functools

import jax
import jax.numpy as jnp
from jax import lax
from jax.experimental import pallas as pl
from jax.experimental.pallas import tpu as pltpu

F32 = jnp.float32
BF16 = jnp.bfloat16
MESH = pl.DeviceIdType.MESH

D = 1024
N_META = 16
CHUNK = 64
PAD_FRONT = CHUNK - N_META
N_HEADS = 4
HEAD_K = 128
HEAD_V = 256
RANK = 16
EPS = 1e-6
GATE_NORM = 16.0
N_IN = 9248
SHARD_IN = N_IN // 4
LANES = 128
N_CONV_TILES = 8
W_CONV = 4096
W_GLA = 2048
W_TAIL = 3072
N_MAIN = W_CONV + W_GLA + W_TAIL
MIB = 1024 * 1024

ADAM_LR = 0.001
ADAM_B1 = 0.9
ADAM_B2 = 0.999
ADAM_EPS = 1e-08
ADAM_WD = 0.01
ADAM_STEP = 10


def _params(sem=None, vmem_mib=None):
    return pltpu.CompilerParams(
        dimension_semantics=sem,
        vmem_limit_bytes=None if vmem_mib is None else vmem_mib * MIB)


def _pick_tile(n, target, mult):
    best = None
    for t in range(mult, min(n, target) + 1, mult):
        if n % t == 0:
            best = t
    return n if best is None else best


def _sigmoid(v):
    return 1.0 / (1.0 + jnp.exp(-v))


def _log_sigmoid(v):
    return jnp.minimum(v, 0.0) - jnp.log(1.0 + jnp.exp(-jnp.abs(v)))


def _dot(a, b):
    return jnp.dot(a, b, preferred_element_type=F32)


def _dot_nt(a, b):
    return lax.dot_general(a, b, (((1,), (1,)), ((), ())), preferred_element_type=F32)


def _dot_tn(a, b):
    return lax.dot_general(a, b, (((0,), (0,)), ((), ())), preferred_element_type=F32)


def _tri_dot(tri, v):
    hi = v.astype(BF16)
    lo = (v - hi.astype(F32)).astype(BF16)
    return _dot(tri, hi) + _dot(tri, lo)


PLANE_FLIPS = ((1, 0, 0), (0, 1, 0), (1, 1, 0))
ALL_FLIPS = tuple((m >> 2 & 1, m >> 1 & 1, m & 1) for m in range(1, 8))
SIBLING_FLIPS = ((0, 0, 1),)


def _exchange(name, arrs, flips, slot_weights, mode):
    n = len(arrs)
    n_slots = 1
    for w in slot_weights:
        n_slots += w
    if mode == "gather":
        out_shape = [jax.ShapeDtypeStruct((n_slots,) + a.shape, a.dtype) for a in arrs]
    else:
        out_shape = [jax.ShapeDtypeStruct(a.shape, a.dtype) for a in arrs]

    def body(*refs):
        ins, outs = refs[:n], refs[n:2 * n]
        send_sems, recv_sems, local_sems = refs[2 * n:]
        pos = (lax.axis_index("x"), lax.axis_index("y"), lax.axis_index("c"))

        def slot_of(p):
            return p[0] * slot_weights[0] + p[1] * slot_weights[1] + p[2] * slot_weights[2]

        peers = [tuple(1 - pos[a] if f[a] else pos[a] for a in range(3)) for f in flips]
        me = slot_of(pos)
        local = []
        sends = []
        for i in range(n):
            if mode != "swap":
                src = ins[i] if mode == "gather" else ins[i].at[me]
                cp = pltpu.make_async_copy(src, outs[i].at[me], local_sems.at[i])
                cp.start()
                local.append(cp)
            for k, peer in enumerate(peers):
                if mode == "gather":
                    src, dst = ins[i], outs[i].at[me]
                elif mode == "scatter":
                    src, dst = ins[i].at[slot_of(peer)], outs[i].at[me]
                else:
                    src, dst = ins[i], outs[i]
                cp = pltpu.make_async_remote_copy(
                    src_ref=src, dst_ref=dst, send_sem=send_sems.at[i, k], recv_sem=recv_sems.at[i, k],
                    device_id=peer, device_id_type=MESH)
                cp.start()
                sends.append(cp)
        for i in range(n):
            for k, peer in enumerate(peers):
                if mode == "gather":
                    src, dst = ins[i], outs[i].at[slot_of(peer)]
                elif mode == "scatter":
                    src, dst = ins[i].at[me], outs[i].at[slot_of(peer)]
                else:
                    src, dst = ins[i], outs[i]
                arrival = pltpu.make_async_remote_copy(
                    src_ref=src, dst_ref=dst, send_sem=send_sems.at[i, k], recv_sem=recv_sems.at[i, k],
                    device_id=peer, device_id_type=MESH)
                arrival.wait_recv()
        for cp in sends:
            cp.wait_send()
        for cp in local:
            cp.wait()

    hbm = pl.BlockSpec(memory_space=pl.ANY)
    outs = pl.pallas_call(
        body, name=name, out_shape=out_shape,
        in_specs=[hbm] * n, out_specs=[hbm] * n,
        scratch_shapes=[pltpu.SemaphoreType.DMA((n, len(flips))),
                        pltpu.SemaphoreType.DMA((n, len(flips))),
                        pltpu.SemaphoreType.DMA((n,))],
        compiler_params=pltpu.CompilerParams(has_side_effects=True),
    )(*arrs)
    return list(outs)


def _gather_via_sibling(name, arrs):
    n = len(arrs)
    out_shape = [jax.ShapeDtypeStruct((4,) + a.shape, a.dtype) for a in arrs]

    def body(*refs):
        ins, outs = refs[:n], refs[n:2 * n]
        send_sems, recv_sems, local_sems = refs[2 * n:]
        x, y, c = lax.axis_index("x"), lax.axis_index("y"), lax.axis_index("c")
        me = 2 * x + y
        chips = [(1 - x, y), (x, 1 - y), (1 - x, 1 - y)]
        sibling = (x, y, 1 - c)

        def half(ref, which):
            rows = ref.shape[0] // 2
            return ref.at[pl.ds(which * rows, rows)]

        def copy(src, dst, i, k, to):
            return pltpu.make_async_remote_copy(
                src_ref=src, dst_ref=dst, send_sem=send_sems.at[i, k], recv_sem=recv_sems.at[i, k],
                device_id=to, device_id_type=MESH)

        local, sends = [], []
        for i in range(n):
            cp = pltpu.make_async_copy(ins[i], outs[i].at[me], local_sems.at[i])
            cp.start()
            local.append(cp)
            for k, (px, py) in enumerate(chips):
                cp = copy(half(ins[i], c), half(outs[i].at[me], c), i, k, (px, py, c))
                cp.start()
                sends.append(cp)
        for k, (px, py) in enumerate(chips):
            slot = 2 * px + py
            for i in range(n):
                landed = half(outs[i].at[slot], c)
                copy(half(ins[i], c), landed, i, k, (px, py, c)).wait_recv()
                cp = copy(landed, landed, i, 3 + k, sibling)
                cp.start()
                sends.append(cp)
        for k, (px, py) in enumerate(chips):
            slot = 2 * px + py
            for i in range(n):
                passed = half(outs[i].at[slot], 1 - c)
                copy(passed, passed, i, 3 + k, sibling).wait_recv()
        for cp in sends:
            cp.wait_send()
        for cp in local:
            cp.wait()

    hbm = pl.BlockSpec(memory_space=pl.ANY)
    outs = pl.pallas_call(
        body, name=name, out_shape=out_shape,
        in_specs=[hbm] * n, out_specs=[hbm] * n,
        scratch_shapes=[pltpu.SemaphoreType.DMA((n, 6)), pltpu.SemaphoreType.DMA((n, 6)),
                        pltpu.SemaphoreType.DMA((n,))],
        compiler_params=pltpu.CompilerParams(has_side_effects=True),
    )(*arrs)
    return list(outs)


HBM_SPEC = pl.BlockSpec(memory_space=pltpu.HBM)
SEM_SPEC = pl.BlockSpec(memory_space=pltpu.SEMAPHORE)
DATAFLOW = pltpu.SideEffectType.DATAFLOW_SIDE_EFFECTING


def _plane_peers():
    x, y, c = lax.axis_index("x"), lax.axis_index("y"), lax.axis_index("c")
    return 2 * x + y, [((1 - x, y, c), 2 * (1 - x) + y), ((x, 1 - y, c), 2 * x + 1 - y),
                       ((1 - x, 1 - y, c), 2 * (1 - x) + 1 - y)]


def _plane_start(name, arrs, mode, after):
    n = len(arrs)
    lands = [lax.empty(((4,) + a.shape) if mode == "gather" else a.shape, a.dtype) for a in arrs]

    def body(*refs):
        srcs, landing = refs[:n], refs[n:2 * n]
        send_sems, recv_sems = refs[2 * n + 1], refs[2 * n + 2]
        token = refs[-1]
        me, peers = _plane_peers()
        for i in range(n):
            for k, (peer, peer_slot) in enumerate(peers):
                src = srcs[i] if mode == "gather" else srcs[i].at[peer_slot]
                pltpu.make_async_remote_copy(
                    src_ref=src, dst_ref=landing[i].at[me], send_sem=send_sems.at[3 * i + k],
                    recv_sem=recv_sems.at[3 * i + k], device_id=peer, device_id_type=MESH).start()
        token[...] = jnp.zeros_like(token)

    hbm_in = [pltpu.with_memory_space_constraint(a, pltpu.HBM) for a in list(arrs) + lands]
    out = pl.pallas_call(
        body, name=name,
        out_shape=[pltpu.SemaphoreType.DMA((3 * n,)), pltpu.SemaphoreType.DMA((3 * n,))]
                  + [pltpu.HBM(a.shape, a.dtype) for a in hbm_in]
                  + [jax.ShapeDtypeStruct((8, LANES), F32)],
        in_specs=[HBM_SPEC] * (2 * n) + [pl.BlockSpec(memory_space=pl.ANY)],
        out_specs=[SEM_SPEC, SEM_SPEC] + [HBM_SPEC] * (2 * n) + [pl.BlockSpec(memory_space=pltpu.VMEM)],
        input_output_aliases={i: 2 + i for i in range(2 * n)},
        compiler_params=pltpu.CompilerParams(has_side_effects=DATAFLOW),
    )(*hbm_in, after)
    return out[:-1], out[-1]


def _plane_wait(name, state, mode, after):
    send_sems, recv_sems = state[0], state[1]
    bufs = list(state[2:])
    n = len(bufs) // 2

    def body(*refs):
        srcs, landing = refs[:n], refs[n:2 * n]
        send_sems, recv_sems = refs[2 * n], refs[2 * n + 1]
        me, peers = _plane_peers()
        for i in range(n):
            for k, (peer, peer_slot) in enumerate(peers):
                src = srcs[i] if mode == "gather" else srcs[i].at[peer_slot]
                cp = pltpu.make_async_remote_copy(
                    src_ref=src, dst_ref=landing[i].at[peer_slot], send_sem=send_sems.at[3 * i + k],
                    recv_sem=recv_sems.at[3 * i + k], device_id=peer, device_id_type=MESH)
                cp.wait_send()
                cp.wait_recv()

    out = pl.pallas_call(
        body, name=name,
        out_shape=[pltpu.HBM(a.shape, a.dtype) for a in bufs],
        in_specs=[HBM_SPEC] * (2 * n) + [SEM_SPEC, SEM_SPEC, pl.BlockSpec(memory_space=pl.ANY)],
        out_specs=[HBM_SPEC] * (2 * n),
        input_output_aliases={i: i for i in range(2 * n)},
        compiler_params=pltpu.CompilerParams(has_side_effects=DATAFLOW),
    )(*bufs, send_sems, recv_sems, after)
    return list(out[n:])


def _cast_bf16(a, name):
    rows, cols = a.shape
    rt = _pick_tile(rows, 256, 16)

    def body(a_ref, o_ref):
        o_ref[...] = a_ref[...].astype(BF16)

    return pl.pallas_call(
        body, name=name, grid=(rows // rt,),
        in_specs=[pl.BlockSpec((rt, cols), lambda i: (i, 0))],
        out_specs=pl.BlockSpec((rt, cols), lambda i: (i, 0)),
        out_shape=jax.ShapeDtypeStruct(a.shape, BF16),
        compiler_params=_params(("parallel",)),
    )(a)


def _sum_slots(buf, name, own=None):
    n_slots, rows, cols = buf.shape
    rt = _pick_tile(rows, 128, 16)
    n_in = 1 if own is None else 2

    def body(*refs):
        b_ref, o_ref = refs[0], refs[-1]
        me = None if own is None else 2 * lax.axis_index("x") + lax.axis_index("y")
        acc = None
        for s in range(n_slots):
            term = b_ref[s] if own is None else jnp.where(me == s, refs[1][s], b_ref[s])
            acc = term.astype(F32) if acc is None else acc + term.astype(F32)
        o_ref[...] = acc

    return pl.pallas_call(
        body, name=name, grid=(rows // rt,),
        in_specs=[pl.BlockSpec((n_slots, rt, cols), lambda i: (0, i, 0))] * n_in,
        out_specs=pl.BlockSpec((rt, cols), lambda i: (i, 0)),
        out_shape=jax.ShapeDtypeStruct((rows, cols), F32),
        compiler_params=_params(("parallel",), 48),
    )(*([buf] if own is None else [buf, own]))


def _adamw(w, grads, m, v, name):
    rows, cols = w.shape
    rt = _pick_tile(rows, 128, 8)
    n_g = len(grads)
    c1 = 1.0 - ADAM_B1 ** ADAM_STEP
    c2 = 1.0 - ADAM_B2 ** ADAM_STEP

    def body(*refs):
        w_ref = refs[0]
        g_refs = refs[1:1 + n_g]
        m_ref, v_ref, g_out, d_out, m_out, v_out = refs[1 + n_g:]
        g = g_refs[0][...]
        for r in g_refs[1:]:
            g = g + r[...]
        m_new = ADAM_B1 * m_ref[...] + (1.0 - ADAM_B1) * g
        v_new = ADAM_B2 * v_ref[...] + (1.0 - ADAM_B2) * (g * g)
        m_hat = m_new / c1
        v_hat = v_new / c2
        g_out[...] = g
        d_out[...] = -ADAM_LR * (m_hat / (jnp.sqrt(v_hat) + ADAM_EPS) + ADAM_WD * w_ref[...])
        m_out[...] = m_new
        v_out[...] = v_new

    spec = pl.BlockSpec((rt, cols), lambda i: (i, 0))
    shape = jax.ShapeDtypeStruct((rows, cols), F32)
    return pl.pallas_call(
        body, name=name, grid=(rows // rt,),
        in_specs=[spec] * (3 + n_g), out_specs=[spec] * 4, out_shape=[shape] * 4,
        compiler_params=_params(("parallel",), 48),
    )(w, *grads, m, v)


def _in_proj(h, g_pre, w_main, w_lr):
    t_rows = h.shape[0]
    tm = _pick_tile(t_rows, 384, LANES)
    n_main = w_main.shape[1]

    def body(h_ref, g_ref, w_hbm, wlr_ref, proj_ref, ut_ref, lr_ref, w_s, w_sem):
        @pl.when(pl.program_id(0) == 0)
        def _():
            cp = pltpu.make_async_copy(w_hbm, w_s, w_sem)
            cp.start()
            cp.wait()

        hh = h_ref[...]
        rstd = lax.rsqrt(jnp.mean(hh * hh, axis=-1, keepdims=True) + EPS)
        uf = hh * rstd * g_ref[...]
        u = uf.astype(BF16)
        ut_ref[...] = jnp.transpose(uf).astype(BF16)
        lr_ref[...] = _dot(u, wlr_ref[...])
        for j in range(n_main // D):
            cols = slice(j * D, (j + 1) * D)
            proj_ref[:, cols] = _dot(u, w_s[:, cols]).astype(BF16)

    return pl.pallas_call(
        body, name="in_proj", grid=(t_rows // tm,),
        in_specs=[pl.BlockSpec((tm, D), lambda i: (i, 0)),
                  pl.BlockSpec((1, D), lambda i: (0, 0)),
                  pl.BlockSpec(memory_space=pl.ANY),
                  pl.BlockSpec((D, LANES), lambda i: (0, 0))],
        out_specs=[pl.BlockSpec((tm, n_main), lambda i: (i, 0)),
                   pl.BlockSpec((D, tm), lambda i: (0, i)),
                   pl.BlockSpec((tm, LANES), lambda i: (i, 0))],
        out_shape=[jax.ShapeDtypeStruct((t_rows, n_main), BF16),
                   jax.ShapeDtypeStruct((D, t_rows), BF16),
                   jax.ShapeDtypeStruct((t_rows, LANES), F32)],
        scratch_shapes=[pltpu.VMEM((D, n_main), BF16), pltpu.SemaphoreType.DMA],
        compiler_params=_params(("arbitrary",), 56),
    )(h, g_pre, w_main, w_lr)


def _conv_parts(p_ref, w_ref):
    cb = p_ref[:, 0:128].astype(F32)
    cc = p_ref[:, 128:256].astype(F32)
    cx = p_ref[:, 256:384].astype(F32)
    cz = p_ref[:, 384:512].astype(F32)
    rows = cb.shape[0]
    w = w_ref[...]
    p = cc * cx
    conv = pltpu.roll(p, 1, 0) * w[0:1] + p * w[1:2] + pltpu.roll(p, rows - 1, 0) * w[2:3]
    sz = _sigmoid(cz)
    return cb, cc, cx, cz, p, conv, sz, w


def _conv_fwd(proj, conv_w, n_seq, lf):
    def body(p_ref, w_ref, y_ref):
        cb, _, _, cz, _, conv, sz, _ = _conv_parts(p_ref, w_ref)
        y_ref[...] = (cb * conv * (cz * sz)).astype(BF16)

    return pl.pallas_call(
        body, name="conv_fwd", grid=(n_seq, N_CONV_TILES),
        in_specs=[pl.BlockSpec((lf, 512), lambda b, j: (b, j)),
                  pl.BlockSpec((3, 128), lambda b, j: (0, j))],
        out_specs=pl.BlockSpec((lf, 128), lambda b, j: (b, j)),
        out_shape=jax.ShapeDtypeStruct((n_seq * lf, D), BF16),
        compiler_params=_params(("parallel", "parallel"), 48),
    )(proj, conv_w)


def _conv_bwd(proj, conv_w, dyc, n_seq, lf):
    def body(p_ref, w_ref, dy_ref, dp_ref, dw_ref):
        cb, cc, cx, cz, p, conv, sz, w = _conv_parts(p_ref, w_ref)
        rows = cb.shape[0]
        dy = dy_ref[...].astype(F32)
        silu = cz * sz
        dcb = dy * conv * silu
        dconv = dy * cb * silu
        dcz = dy * cb * conv * (sz * (1.0 + cz * (1.0 - sz)))
        d_next = pltpu.roll(dconv, rows - 1, 0)
        d_prev = pltpu.roll(dconv, 1, 0)
        dp = d_next * w[0:1] + dconv * w[1:2] + d_prev * w[2:3]
        dp_ref[:, 0:128] = dcb.astype(BF16)
        dp_ref[:, 128:256] = (dp * cx).astype(BF16)
        dp_ref[:, 256:384] = (dp * cc).astype(BF16)
        dp_ref[:, 384:512] = dcz.astype(BF16)
        dw_ref[0:1, :] = jnp.sum(dconv * pltpu.roll(p, 1, 0), axis=0, keepdims=True)
        dw_ref[1:2, :] = jnp.sum(dconv * p, axis=0, keepdims=True)
        dw_ref[2:3, :] = jnp.sum(dconv * pltpu.roll(p, rows - 1, 0), axis=0, keepdims=True)

    return pl.pallas_call(
        body, name="conv_bwd", grid=(n_seq, N_CONV_TILES),
        in_specs=[pl.BlockSpec((lf, 512), lambda b, j: (b, j)),
                  pl.BlockSpec((3, 128), lambda b, j: (0, j)),
                  pl.BlockSpec((lf, 128), lambda b, j: (b, j))],
        out_specs=[pl.BlockSpec((lf, 512), lambda b, j: (b, j)),
                   pl.BlockSpec((None, 3, 128), lambda b, j: (b, 0, j))],
        out_shape=[jax.ShapeDtypeStruct((n_seq * lf, W_CONV), BF16),
                   jax.ShapeDtypeStruct((n_seq, 3, D), F32)],
        compiler_params=_params(("parallel", "parallel"), 48),
    )(proj, conv_w, dyc)


GROUP = 3
GROUP_ROWS = GROUP * CHUNK


def _row_group(shape):
    row = lax.broadcasted_iota(jnp.int32, shape, 0)
    grp = jnp.zeros(shape, jnp.int32)
    for r in range(1, GROUP):
        grp = grp + (row >= r * CHUNK).astype(jnp.int32)
    return grp


def _lane_group(shape, width):
    lane = lax.broadcasted_iota(jnp.int32, shape, 1)
    grp = jnp.zeros(shape, jnp.int32)
    for r in range(1, GROUP):
        grp = grp + (lane >= r * width).astype(jnp.int32)
    return grp


def _group_masks(direction):
    shape = (GROUP_ROWS, GROUP_ROWS)
    row = lax.broadcasted_iota(jnp.int32, shape, 0)
    col = lax.broadcasted_iota(jnp.int32, shape, 1)
    same = _row_group(shape) == _lane_group(shape, CHUNK)
    lower = same & (col <= row)
    upper = same & (col >= row)
    if direction == 0:
        return lower.astype(BF16), upper.astype(BF16), lower
    return upper.astype(BF16), lower.astype(BF16), same & (col > row)


def _diag_blocks(v):
    w = v.shape[1]
    wide = jnp.concatenate([v] * GROUP, axis=1)
    return jnp.where(_row_group(wide.shape) == _lane_group(wide.shape, w), wide, jnp.zeros_like(wide))


def _pick_diag(wide):
    w = wide.shape[1] // GROUP
    grp = _row_group((GROUP_ROWS, w))
    out = wide[:, 0:w]
    for r in range(1, GROUP):
        out = jnp.where(grp == r, wide[:, r * w:(r + 1) * w], out)
    return out


def _per_chunk_rows(rows_of_chunk):
    w = rows_of_chunk[0].shape[1]
    return jnp.concatenate([jnp.broadcast_to(v, (CHUNK, w)) for v in rows_of_chunk], axis=0)


def _chunk_end_rows(direction, b):
    at = CHUNK - 1 if direction == 0 else 0
    return [b[r * CHUNK + at:r * CHUNK + at + 1, :] for r in range(GROUP)]


def _gla_gates(lr_bf, wg_ref, bg_ref, lf):
    z = _dot(lr_bf, wg_ref[...]) + bg_ref[...]
    valid = lax.broadcasted_iota(jnp.int32, (lf, HEAD_K), 0) >= PAD_FRONT
    return z, valid


def _group_unroll(n_groups):
    return n_groups if n_groups <= 11 else 1


def _group_rows(g):
    return pl.ds(pl.multiple_of(g * GROUP_ROWS, GROUP_ROWS), GROUP_ROWS)


def _chunk_decay(direction, g, r, b_s):
    base = g * GROUP_ROWS + r * CHUNK
    if direction == 0:
        grp = b_s[pl.ds(pl.multiple_of(base + CHUNK - 8, 8), 8), :]
        return jnp.exp(grp[7:8, :])
    grp = b_s[pl.ds(pl.multiple_of(base, 8), 8), :]
    return jnp.exp(grp[0:1, :])


def _state_scan(direction, n_groups, b_s, st_s, reverse):
    ascending = (direction == 0) != reverse

    def step(i, carry):
        g = i if ascending else n_groups - 1 - i
        for rr in range(GROUP):
            r = rr if ascending else GROUP - 1 - rr
            lanes = slice(r * HEAD_K, (r + 1) * HEAD_K)
            decay = _chunk_decay(direction, g, r, b_s)
            local = st_s[g, :, lanes]
            st_s[g, :, lanes] = carry
            carry = (local + carry * decay) if reverse else (carry * decay + local)
        return carry

    lax.fori_loop(0, n_groups, step, jnp.zeros((HEAD_V, HEAD_K), F32))


def _gla_states(direction, n_groups, qkv_ref, g_s, b_s, st_s, tri):
    def local(g, carry):
        rows = _group_rows(g)
        b = _tri_dot(tri, g_s[rows, :])
        b_s[rows, :] = b
        b_end = _per_chunk_rows(_chunk_end_rows(direction, b))
        k = qkv_ref[rows, 128:256].astype(F32)
        v = qkv_ref[rows, 256:512]
        k_dec = (k * jnp.exp(b_end - b)).astype(BF16)
        st_s[g] = _dot_tn(v, _diag_blocks(k_dec))
        return carry

    lax.fori_loop(0, n_groups, local, 0, unroll=_group_unroll(n_groups))
    _state_scan(direction, n_groups, b_s, st_s, False)


def _gla_fwd(proj, lr, wgf, wgb, bgf, bgb, n_seq, lf):
    assert lf % GROUP_ROWS == 0
    n_groups = lf // GROUP_ROWS
    scale = HEAD_K ** -0.5

    def body(qkv_ref, lr_ref, wgf_ref, wgb_ref, bgf_ref, bgb_ref, o_ref, g_s, b_s, st_s):
        lr_bf = lr_ref[...].astype(BF16)
        for direction in (0, 1):
            wg_ref, bg_ref = ((wgf_ref, bgf_ref), (wgb_ref, bgb_ref))[direction]
            z, valid = _gla_gates(lr_bf, wg_ref, bg_ref, lf)
            g_s[...] = jnp.where(valid, _log_sigmoid(z) / GATE_NORM, 0.0)
            tri, _, smask = _group_masks(direction)
            _gla_states(direction, n_groups, qkv_ref, g_s, b_s, st_s, tri)

            def out(g, carry):
                rows = _group_rows(g)
                b = b_s[rows, :]
                q = qkv_ref[rows, 0:128].astype(F32) * scale
                k = qkv_ref[rows, 128:256].astype(F32)
                v = qkv_ref[rows, 256:512]
                q_in = (q * jnp.exp(b)).astype(BF16)
                k_in = (k * jnp.exp(-b)).astype(BF16)
                s = jnp.where(smask, _dot_nt(q_in, k_in), 0.0).astype(BF16)
                o = _dot(s, v) + _dot_nt(_diag_blocks(q_in), st_s[g].astype(BF16))
                if direction == 0:
                    o_ref[rows, :] = o
                else:
                    o_ref[rows, :] = o_ref[rows, :] + o
                return carry

            lax.fori_loop(0, n_groups, out, 0, unroll=_group_unroll(n_groups))

    return pl.pallas_call(
        body, name="gla_fwd", grid=(n_seq, N_HEADS),
        in_specs=[pl.BlockSpec((lf, 512), lambda b, h: (b, N_CONV_TILES + h)),
                  pl.BlockSpec((lf, LANES), lambda b, h: (b, 0)),
                  pl.BlockSpec((None, LANES, HEAD_K), lambda b, h: (h, 0, 0)),
                  pl.BlockSpec((None, LANES, HEAD_K), lambda b, h: (h, 0, 0)),
                  pl.BlockSpec((None, 1, HEAD_K), lambda b, h: (h, 0, 0)),
                  pl.BlockSpec((None, 1, HEAD_K), lambda b, h: (h, 0, 0))],
        out_specs=pl.BlockSpec((lf, HEAD_V), lambda b, h: (b, h)),
        out_shape=jax.ShapeDtypeStruct((n_seq * lf, D), F32),
        scratch_shapes=[pltpu.VMEM((lf, HEAD_K), F32), pltpu.VMEM((lf, HEAD_K), F32),
                        pltpu.VMEM((n_groups, HEAD_V, GROUP * HEAD_K), F32)],
        compiler_params=_params(("parallel", "parallel"), 48),
    )(proj, lr, wgf, wgb, bgf, bgb)


def _gla_bwd(proj, lr, d_o, wgf, wgb, bgf, bgb, n_seq, lf, token):
    assert lf % GROUP_ROWS == 0
    n_groups = lf // GROUP_ROWS
    scale = HEAD_K ** -0.5

    def body(qkv_ref, lr_ref, do_ref, wgf_ref, wgb_ref, bgf_ref, bgb_ref, token_ref,
             dqkv_ref, dlr_ref, dwgf_ref, dwgb_ref, dbg_ref,
             g_s, b_s, fac_s, dg_s, st_s, dst_s, acc_s):
        lr_bf = lr_ref[...].astype(BF16)
        dlr = jnp.zeros((lf, LANES), F32)
        for direction in (0, 1):
            wg_ref, bg_ref = ((wgf_ref, bgf_ref), (wgb_ref, bgb_ref))[direction]
            z, valid = _gla_gates(lr_bf, wg_ref, bg_ref, lf)
            g_s[...] = jnp.where(valid, _log_sigmoid(z) / GATE_NORM, 0.0)
            fac_s[...] = jnp.where(valid, _sigmoid(-z) / GATE_NORM, 0.0)
            tri, tri_t, smask = _group_masks(direction)
            end_row = CHUNK - 1 if direction == 0 else 0
            _gla_states(direction, n_groups, qkv_ref, g_s, b_s, st_s, tri)

            def state_grad_local(g, carry):
                rows = _group_rows(g)
                q = qkv_ref[rows, 0:128].astype(F32) * scale
                q_in = (q * jnp.exp(b_s[rows, :])).astype(BF16)
                dst_s[g] = _dot_tn(do_ref[rows, :], _diag_blocks(q_in))
                return carry

            lax.fori_loop(0, n_groups, state_grad_local, 0, unroll=_group_unroll(n_groups))
            _state_scan(direction, n_groups, b_s, dst_s, True)

            def group_grads(g, carry):
                rows = _group_rows(g)
                b = b_s[rows, :]
                ends = _chunk_end_rows(direction, b)
                b_end = _per_chunk_rows(ends)
                q = qkv_ref[rows, 0:128].astype(F32) * scale
                k = qkv_ref[rows, 128:256].astype(F32)
                v = qkv_ref[rows, 256:512]
                d_out = do_ref[rows, :]
                e_pos = jnp.exp(b)
                e_neg = jnp.exp(-b)
                e_end = jnp.exp(b_end - b)
                q_in = q * e_pos
                k_in = k * e_neg
                k_dec = k * e_end
                q_in_bf = q_in.astype(BF16)
                k_in_bf = k_in.astype(BF16)
                state = st_s[g]
                d_state = dst_s[g]
                state_bf = state.astype(BF16)
                d_state_bf = d_state.astype(BF16)
                s = jnp.where(smask, _dot_nt(q_in_bf, k_in_bf), 0.0).astype(BF16)
                ds = jnp.where(smask, _dot_nt(d_out, v), 0.0).astype(BF16)
                dv = _dot_tn(s, d_out) + _dot_nt(_diag_blocks(k_dec.astype(BF16)), d_state_bf)
                dq_in = _dot(ds, k_in_bf) + _pick_diag(_dot(d_out, state_bf))
                dk_in = _dot_tn(ds, q_in_bf)
                dk_dec = _pick_diag(_dot(v, d_state_bf))
                dq = dq_in * e_pos * scale
                dk = dk_in * e_neg + dk_dec * e_end
                if direction == 0:
                    acc_s[rows, 0:128] = dq
                    acc_s[rows, 128:256] = dk
                    acc_s[rows, 256:512] = dv
                else:
                    dqkv_ref[rows, 0:128] = (acc_s[rows, 0:128] + dq).astype(BF16)
                    dqkv_ref[rows, 128:256] = (acc_s[rows, 128:256] + dk).astype(BF16)
                    dqkv_ref[rows, 256:512] = (acc_s[rows, 256:512] + dv).astype(BF16)
                dkk = dk_dec * k_dec
                db = dq_in * q_in - dk_in * k_in - dkk
                d_decay = jnp.sum(d_state * state, axis=0, keepdims=True)
                db_end = [jnp.sum(dkk[r * CHUNK:(r + 1) * CHUNK, :], axis=0, keepdims=True)
                          + d_decay[:, r * HEAD_K:(r + 1) * HEAD_K] * jnp.exp(ends[r]) for r in range(GROUP)]
                row = lax.broadcasted_iota(jnp.int32, (GROUP_ROWS, HEAD_K), 0)
                at_end = row == end_row
                for r in range(1, GROUP):
                    at_end = at_end | (row == r * CHUNK + end_row)
                db = db + jnp.where(at_end, _per_chunk_rows(db_end), 0.0)
                dg_s[rows, :] = _tri_dot(tri_t, db)
                return carry

            lax.fori_loop(0, n_groups, group_grads, 0, unroll=_group_unroll(n_groups))

            dz = dg_s[...] * fac_s[...]
            dz_bf = dz.astype(BF16)
            dbg_ref[direction:direction + 1, :] = jnp.sum(dz, axis=0, keepdims=True)
            (dwgf_ref, dwgb_ref)[direction][...] = _dot_tn(lr_bf, dz_bf)
            dlr = dlr + _dot_nt(dz_bf, wg_ref[...])

        @pl.when(pl.program_id(1) == 0)
        def _():
            dlr_ref[...] = dlr

        @pl.when(pl.program_id(1) != 0)
        def _():
            dlr_ref[...] = dlr_ref[...] + dlr

    gate_w = pl.BlockSpec((None, LANES, HEAD_K), lambda b, h: (h, 0, 0))
    gate_b = pl.BlockSpec((None, 1, HEAD_K), lambda b, h: (h, 0, 0))
    return pl.pallas_call(
        body, name="gla_bwd", grid=(n_seq, N_HEADS),
        in_specs=[pl.BlockSpec((lf, 512), lambda b, h: (b, N_CONV_TILES + h)),
                  pl.BlockSpec((lf, LANES), lambda b, h: (b, 0)),
                  pl.BlockSpec((lf, HEAD_V), lambda b, h: (b, h)),
                  gate_w, gate_w, gate_b, gate_b,
                  pl.BlockSpec((8, LANES), lambda b, h: (0, 0))],
        out_specs=[pl.BlockSpec((lf, 512), lambda b, h: (b, h)),
                   pl.BlockSpec((lf, LANES), lambda b, h: (b, 0)),
                   pl.BlockSpec((None, None, LANES, HEAD_K), lambda b, h: (b, h, 0, 0)),
                   pl.BlockSpec((None, None, LANES, HEAD_K), lambda b, h: (b, h, 0, 0)),
                   pl.BlockSpec((None, None, 2, HEAD_K), lambda b, h: (b, h, 0, 0))],
        out_shape=[jax.ShapeDtypeStruct((n_seq * lf, W_GLA), BF16),
                   jax.ShapeDtypeStruct((n_seq * lf, LANES), F32),
                   jax.ShapeDtypeStruct((n_seq, N_HEADS, LANES, HEAD_K), F32),
                   jax.ShapeDtypeStruct((n_seq, N_HEADS, LANES, HEAD_K), F32),
                   jax.ShapeDtypeStruct((n_seq, N_HEADS, 2, HEAD_K), F32)],
        scratch_shapes=[pltpu.VMEM((lf, HEAD_K), F32), pltpu.VMEM((lf, HEAD_K), F32),
                        pltpu.VMEM((lf, HEAD_K), F32), pltpu.VMEM((lf, HEAD_K), F32),
                        pltpu.VMEM((n_groups, HEAD_V, GROUP * HEAD_K), F32),
                        pltpu.VMEM((n_groups, HEAD_V, GROUP * HEAD_K), F32),
                        pltpu.VMEM((lf, 512), F32)],
        compiler_params=_params(("parallel", "arbitrary"), 56),
    )(proj, lr, d_o, wgf, wgb, bgf, bgb, token)


def _tail(h, tgt, yc, o, proj, w3, gamma, g_post, lf):
    t_rows = h.shape[0]
    tm = _pick_tile(t_rows, 256, CHUNK)
    n_chunks = lf // CHUNK
    per_tile = tm // CHUNK

    def body(h_ref, tgt_ref, yc_ref, o_ref, r_ref, ma_ref, mb_ref, w_hbm, gamma_ref, gpost_ref,
             dres_ref, yg_ref, merged_ref, dout_ref, dpc_ref, dpg_ref, dyc_ref, do_ref, dtail_ref,
             loss_ref, dgpost_ref, dgamma_ref, w_s, w_sem):
        i = pl.program_id(0)

        @pl.when(i == 0)
        def _():
            cp = pltpu.make_async_copy(w_hbm, w_s, w_sem)
            cp.start()
            cp.wait()
            loss_ref[...] = jnp.zeros_like(loss_ref)
            dgpost_ref[...] = jnp.zeros_like(dgpost_ref)
            dgamma_ref[...] = jnp.zeros_like(dgamma_ref)

        gamma = gamma_ref[...]
        o = o_ref[...]
        r = r_ref[...].astype(F32)
        sr = _sigmoid(r)
        silu_r = r * sr
        n_parts, rstd_parts = [], []
        for hd in range(N_HEADS):
            oh = o[:, hd * HEAD_V:(hd + 1) * HEAD_V]
            rstd = lax.rsqrt(jnp.mean(oh * oh, axis=-1, keepdims=True) + EPS)
            n_parts.append(oh * rstd)
            rstd_parts.append(rstd)
        n = jnp.concatenate(n_parts, axis=-1)
        gamma_t = jnp.concatenate([gamma] * N_HEADS, axis=-1)
        yg = n * gamma_t * silu_r
        yg_bf = yg.astype(BF16)
        yg_ref[...] = yg_bf
        yc = yc_ref[...]
        pc = _dot(yc, w_s[0])
        pg = _dot(yg_bf, w_s[1])
        sa = _sigmoid(ma_ref[...].astype(F32))
        sb = _sigmoid(mb_ref[...].astype(F32))
        merged = (sa * pc + sb * pg).astype(BF16)
        merged_ref[...] = merged
        out = _dot(merged, w_s[2])
        rstd2 = lax.rsqrt(jnp.mean(out * out, axis=-1, keepdims=True) + EPS)
        nn = out * rstd2
        gpost = gpost_ref[...]
        y = h_ref[...] + nn * gpost

        rowi = lax.broadcasted_iota(jnp.int32, (tm, 1), 0)
        keep = jnp.zeros((tm, 1), F32)
        for kk in range(per_tile):
            is_tok = ((i * per_tile + kk) % n_chunks) != 0
            f = jnp.where(is_tok, 1.0, 0.0)
            keep = jnp.where((rowi >= kk * CHUNK) & (rowi < (kk + 1) * CHUNK), f, keep)
        diff = (y - tgt_ref[...]) * keep
        loss_ref[...] += jnp.sum(diff * diff) * (0.5 / D)
        dy = diff * (1.0 / D)
        dres_ref[...] = dy
        dgpost_ref[...] += jnp.sum(dy * nn, axis=0, keepdims=True)
        dn = dy * gpost
        dout = (rstd2 * (dn - nn * jnp.mean(dn * nn, axis=-1, keepdims=True))).astype(BF16)
        dout_ref[...] = dout
        dmerged = _dot_nt(dout, w_s[2])
        dpc = (dmerged * sa).astype(BF16)
        dpg = (dmerged * sb).astype(BF16)
        dpc_ref[...] = dpc
        dpg_ref[...] = dpg
        dtail_ref[:, D:2 * D] = (dmerged * pc * (sa * (1.0 - sa))).astype(BF16)
        dtail_ref[:, 2 * D:3 * D] = (dmerged * pg * (sb * (1.0 - sb))).astype(BF16)
        dyc_ref[...] = _dot_nt(dpc, w_s[0]).astype(BF16)
        dyg = _dot_nt(dpg, w_s[1])
        dtail_ref[:, 0:D] = (dyg * n * gamma_t * (sr * (1.0 + r * (1.0 - sr)))).astype(BF16)
        dgam_full = jnp.sum(dyg * n * silu_r, axis=0, keepdims=True)
        dgam = dgam_full[:, 0:HEAD_V]
        for hd in range(1, N_HEADS):
            dgam = dgam + dgam_full[:, hd * HEAD_V:(hd + 1) * HEAD_V]
        dgamma_ref[...] += dgam
        dng = dyg * gamma_t * silu_r
        do_parts = []
        for hd in range(N_HEADS):
            sl = slice(hd * HEAD_V, (hd + 1) * HEAD_V)
            dnh = dng[:, sl]
            nh = n_parts[hd]
            do_parts.append(rstd_parts[hd] * (dnh - nh * jnp.mean(dnh * nh, axis=-1, keepdims=True)))
        do_ref[...] = jnp.concatenate(do_parts, axis=-1).astype(BF16)

    row = lambda c: pl.BlockSpec((tm, D), lambda i: (i, c))
    const = lambda shape: pl.BlockSpec(shape, lambda i: (0, 0))
    act = jax.ShapeDtypeStruct((t_rows, D), BF16)
    return pl.pallas_call(
        body, name="tail", grid=(t_rows // tm,),
        in_specs=[row(0), row(0), row(0), row(0), row(6), row(7), row(8),
                  pl.BlockSpec(memory_space=pl.ANY), const((1, HEAD_V)), const((1, D))],
        out_specs=[row(0)] * 8 + [pl.BlockSpec((tm, W_TAIL), lambda i: (i, 0)),
                                  const((8, LANES)), const((1, D)), const((1, HEAD_V))],
        out_shape=[jax.ShapeDtypeStruct((t_rows, D), F32)] + [act] * 7
                  + [jax.ShapeDtypeStruct((t_rows, W_TAIL), BF16),
                     jax.ShapeDtypeStruct((8, LANES), F32),
                     jax.ShapeDtypeStruct((1, D), F32),
                     jax.ShapeDtypeStruct((1, HEAD_V), F32)],
        scratch_shapes=[pltpu.VMEM((3, D, D), BF16), pltpu.SemaphoreType.DMA],
        compiler_params=_params(("arbitrary",), 56),
    )(h, tgt, yc, o, proj, proj, proj, w3, gamma, g_post)


def _wgrad(a, b, name, out_dtype=BF16):
    t_rows, m = a.shape
    n = b.shape[1]
    tn = D if n % D == 0 else n
    tk = _pick_tile(t_rows, 528, 16)
    n_k = t_rows // tk

    def body(a_ref, b_ref, o_ref, acc):
        k = pl.program_id(1)

        @pl.when(k == 0)
        def _():
            acc[...] = jnp.zeros_like(acc)

        acc[...] += _dot_tn(a_ref[...].astype(BF16), b_ref[...].astype(BF16))

        @pl.when(k == n_k - 1)
        def _():
            o_ref[...] = acc[...].astype(out_dtype)

    return pl.pallas_call(
        body, name=name, grid=(n // tn, n_k),
        in_specs=[pl.BlockSpec((tk, m), lambda j, k: (k, 0)),
                  pl.BlockSpec((tk, tn), lambda j, k: (k, j))],
        out_specs=pl.BlockSpec((m, tn), lambda j, k: (0, j)),
        out_shape=jax.ShapeDtypeStruct((m, n), out_dtype),
        scratch_shapes=[pltpu.VMEM((m, tn), F32)],
        compiler_params=_params(("parallel", "arbitrary"), 48),
    )(a, b)


def _wgrad_t(a_t, b, name, out_dtype=BF16):
    m, t_rows = a_t.shape
    n = b.shape[1]
    tn = D if n % D == 0 else n
    tk = _pick_tile(t_rows, 768, LANES)
    n_k = t_rows // tk

    def body(a_ref, b_ref, o_ref, acc):
        k = pl.program_id(1)

        @pl.when(k == 0)
        def _():
            acc[...] = jnp.zeros_like(acc)

        acc[...] += _dot(a_ref[...], b_ref[...].astype(BF16))

        @pl.when(k == n_k - 1)
        def _():
            o_ref[...] = acc[...].astype(out_dtype)

    return pl.pallas_call(
        body, name=name, grid=(n // tn, n_k),
        in_specs=[pl.BlockSpec((m, tk), lambda j, k: (0, k)),
                  pl.BlockSpec((tk, tn), lambda j, k: (k, j))],
        out_specs=pl.BlockSpec((m, tn), lambda j, k: (0, j)),
        out_shape=jax.ShapeDtypeStruct((m, n), out_dtype),
        scratch_shapes=[pltpu.VMEM((m, tn), F32)],
        compiler_params=_params(("parallel", "arbitrary"), 48),
    )(a_t, b)


def _dgrad_in(dpc, dpg, dpt, dlr, w_main, w_lr, h, g_pre, dres, token):
    t_rows = h.shape[0]
    tm = _pick_tile(t_rows, 256, 16)
    n_main = w_main.shape[1]

    def body(dpc_ref, dpg_ref, dpt_ref, dlr_ref, w_hbm, wlr_ref, h_ref, g_ref, dres_ref, token_ref,
             dh_ref, dg_ref, w_s, w_sem):
        @pl.when(pl.program_id(0) == 0)
        def _():
            cp = pltpu.make_async_copy(w_hbm, w_s, w_sem)
            cp.start()
            cp.wait()
            dg_ref[...] = jnp.zeros_like(dg_ref)

        du = _dot_nt(dlr_ref[...].astype(BF16), wlr_ref[...])
        du += _dot_nt(dpc_ref[...], w_s[:, 0:W_CONV])
        du += _dot_nt(dpg_ref[...], w_s[:, W_CONV:W_CONV + W_GLA])
        du += _dot_nt(dpt_ref[...], w_s[:, W_CONV + W_GLA:n_main])
        hh = h_ref[...]
        rstd = lax.rsqrt(jnp.mean(hh * hh, axis=-1, keepdims=True) + EPS)
        xhat = hh * rstd
        dg_ref[...] += jnp.sum(du * xhat, axis=0, keepdims=True)
        dx = du * g_ref[...]
        dh_ref[...] = rstd * (dx - xhat * jnp.mean(dx * xhat, axis=-1, keepdims=True)) + dres_ref[...]

    row = lambda width: pl.BlockSpec((tm, width), lambda i: (i, 0))
    return pl.pallas_call(
        body, name="dgrad_in", grid=(t_rows // tm,),
        in_specs=[row(W_CONV), row(W_GLA), row(W_TAIL), row(LANES),
                  pl.BlockSpec(memory_space=pl.ANY),
                  pl.BlockSpec((D, LANES), lambda i: (0, 0)),
                  row(D), pl.BlockSpec((1, D), lambda i: (0, 0)), row(D),
                  pl.BlockSpec((8, LANES), lambda i: (0, 0))],
        out_specs=[row(D), pl.BlockSpec((1, D), lambda i: (0, 0))],
        out_shape=[jax.ShapeDtypeStruct((t_rows, D), F32), jax.ShapeDtypeStruct((1, D), F32)],
        scratch_shapes=[pltpu.VMEM((D, n_main), BF16), pltpu.SemaphoreType.DMA],
        compiler_params=_params(("arbitrary",), 56),
    )(dpc, dpg, dpt, dlr, w_main, w_lr, h, g_pre, dres, token)


OFF_CB, OFF_CC, OFF_CX, OFF_CZ = 0, 1024, 2048, 3072
OFF_Q, OFF_K, OFF_V, OFF_R = 4096, 4608, 5120, 6144
OFF_LR, OFF_MA, OFF_MB = 7168, 7200, 8224


def _main_columns(w):
    rows = w.shape[0]
    conv = w[:, 0:W_CONV].reshape(rows, 4, N_CONV_TILES, 128).transpose(0, 2, 1, 3).reshape(rows, W_CONV)
    q = w[:, OFF_Q:OFF_K].reshape(rows, N_HEADS, HEAD_K)
    k = w[:, OFF_K:OFF_V].reshape(rows, N_HEADS, HEAD_K)
    v = w[:, OFF_V:OFF_R].reshape(rows, N_HEADS, HEAD_V)
    gla = jnp.concatenate([q, k, v], axis=2).reshape(rows, W_GLA)
    return jnp.concatenate([conv, gla, w[:, OFF_R:OFF_R + D], w[:, OFF_MA:OFF_MA + 2 * D]], axis=1)


def _reference_columns(g_conv, g_gla, g_tail, g_lr):
    rows = g_conv.shape[0]
    conv = g_conv.reshape(rows, N_CONV_TILES, 4, 128).transpose(0, 2, 1, 3).reshape(rows, W_CONV)
    gla = g_gla.reshape(rows, N_HEADS, 512)
    q = gla[:, :, 0:128].reshape(rows, N_HEADS * HEAD_K)
    k = gla[:, :, 128:256].reshape(rows, N_HEADS * HEAD_K)
    v = gla[:, :, 256:512].reshape(rows, N_HEADS * HEAD_V)
    return jnp.concatenate([conv, q, k, v, g_tail[:, 0:D], g_lr[:, 0:2 * RANK], g_tail[:, D:3 * D]], axis=1)


def _pack(arrs, rows):
    flat = jnp.concatenate([a.reshape(-1) for a in arrs])
    return jnp.pad(flat, (0, rows * LANES - flat.shape[0])).reshape(rows, LANES)


def _unpack(packed, shapes):
    flat = packed.reshape(-1)
    out, pos = [], 0
    for s in shapes:
        size = 1
        for d in s:
            size *= d
        out.append(flat[pos:pos + size].reshape(s))
        pos += size
    return out


def _rows_for(shapes, mult=8):
    total = 0
    for s in shapes:
        size = 1
        for d in s:
            size *= d
        total += size
    return -(-total // (mult * LANES)) * mult


def kernel(x, meta_tokens, norm_pre, w_in, conv_w, w_gate_fwd, b_gate_fwd, w_gate_bwd, b_gate_bwd, gla_norm, w_out_conv, w_out_gla, w_merge_out, norm_post, loss_target, m_meta_tokens, m_norm_pre, m_w_in, m_conv_w, m_w_gate_fwd, m_b_gate_fwd, m_w_gate_bwd, m_b_gate_bwd, m_gla_norm, m_w_out_conv, m_w_out_gla, m_w_merge_out, m_norm_post, v_meta_tokens, v_norm_pre, v_w_in, v_conv_w, v_w_gate_fwd, v_b_gate_fwd, v_w_gate_bwd, v_b_gate_bwd, v_gla_norm, v_w_out_conv, v_w_out_gla, v_w_merge_out, v_norm_post):
    n_seq, seq, _ = x.shape
    lf = CHUNK + seq
    t_rows = n_seq * lf
    shard = 2 * lax.axis_index("x") + lax.axis_index("y")

    w_in_bf = _cast_bf16(w_in[0], "cast_w_in")
    w_out_bf = _cast_bf16(jnp.concatenate([w_out_conv[0], w_out_gla[0], w_merge_out[0]], axis=0), "cast_w_out")
    small_shapes = [(N_META, D // 4), (3, D // 4), (RANK, HEAD_K), (RANK, HEAD_K)]
    small = _pack([meta_tokens, conv_w[0], w_gate_fwd[0], w_gate_bwd[0]], _rows_for(small_shapes, 16))
    w_in_all, small_all = _gather_via_sibling("gather_w_in", [w_in_bf, small])
    w_out_state, _ = _plane_start("gather_w_out_start", [w_out_bf], "gather", small_all)

    w_full = jnp.transpose(w_in_all, (1, 0, 2)).reshape(D, N_IN)
    w_main = _main_columns(w_full)
    w_lr = jnp.pad(w_full[:, OFF_LR:OFF_LR + 2 * RANK], ((0, 0), (0, LANES - 2 * RANK)))
    smalls = [_unpack(small_all[s], small_shapes) for s in range(4)]
    meta_full = jnp.concatenate([smalls[s][0] for s in range(4)], axis=1)
    conv_full = jnp.concatenate([smalls[s][1] for s in range(4)], axis=1)
    wgf = jnp.stack([jnp.pad(smalls[s][2], ((0, LANES - RANK), (0, 0))) for s in range(4)]).astype(BF16)
    wgb = jnp.stack([jnp.pad(smalls[s][3], ((RANK, LANES - 2 * RANK), (0, 0))) for s in range(4)]).astype(BF16)
    bgf = b_gate_fwd.reshape(N_HEADS, 1, HEAD_K)
    bgb = b_gate_bwd.reshape(N_HEADS, 1, HEAD_K)

    head = jnp.concatenate([jnp.zeros((PAD_FRONT, D), F32), meta_full], axis=0)
    h = jnp.concatenate([jnp.broadcast_to(head[None], (n_seq, CHUNK, D)), x], axis=1).reshape(t_rows, D)
    tgt = jnp.pad(loss_target, ((0, 0), (CHUNK, 0), (0, 0))).reshape(t_rows, D)

    proj, u_t, lr = _in_proj(h, norm_pre, w_main, w_lr)
    yc = _conv_fwd(proj, conv_full, n_seq, lf)
    o = _gla_fwd(proj, lr, wgf, wgb, bgf, bgb, n_seq, lf)
    (w_out_landed,) = _plane_wait("gather_w_out_wait", w_out_state, "gather", o)
    slot_ids = lax.broadcasted_iota(jnp.int32, (4, 1, 1), 0)
    w_out_all = jnp.where(slot_ids == shard, w_out_bf[None], w_out_landed)
    w3 = jnp.transpose(w_out_all.reshape(4, 3, D // 4, D), (1, 0, 2, 3)).reshape(3, D, D)
    (dres, yg, merged, dout, dpc_out, dpg_out, dyc, d_o, dtail, loss_acc, d_gpost, d_gamma) = _tail(
        h, tgt, yc, o, proj, w3, gla_norm, norm_post, lf)
    g_w_oc = _wgrad(yc, dpc_out, "wgrad_out_conv")
    g_w_og = _wgrad(yg, dpg_out, "wgrad_out_gla")
    g_w_mo = _wgrad(merged, dout, "wgrad_merge_out")
    g_out_slots = jnp.concatenate([g.reshape(4, D // 4, D) for g in (g_w_oc, g_w_og, g_w_mo)], axis=1)
    out_state, out_token = _plane_start("scatter_out_grads_start", [g_out_slots], "scatter", g_w_mo)
    dgla, dlr, dwgf_p, dwgb_p, dbg_p = _gla_bwd(proj, lr, d_o, wgf, wgb, bgf, bgb, n_seq, lf, out_token)
    (got_out,) = _plane_wait("scatter_out_grads_wait", out_state, "scatter", dlr)
    dconv, dconvw_p = _conv_bwd(proj, conv_full, dyc, n_seq, lf)
    g_conv = _wgrad_t(u_t, dconv, "wgrad_in_conv")
    g_gla = _wgrad_t(u_t, dgla, "wgrad_in_gla")
    g_tail = _wgrad_t(u_t, dtail, "wgrad_in_tail")
    g_lr = _wgrad_t(u_t, dlr, "wgrad_in_lr")

    g_in_full = _reference_columns(g_conv, g_gla, g_tail, g_lr)
    g_in_slots = jnp.transpose(g_in_full.reshape(D, 4, SHARD_IN), (1, 0, 2))
    in_state, in_token = _plane_start("scatter_in_grads_start", [g_in_slots], "scatter", g_lr)
    dh, d_gpre = _dgrad_in(dconv, dgla, dtail, dlr, w_main, w_lr, h, norm_pre, dres, in_token)
    (got_in,) = _plane_wait("scatter_in_grads_wait", in_state, "scatter", d_gpre)

    dh3 = dh.reshape(n_seq, lf, D)
    grad_x = dh3[:, CHUNK:, :]

    d_meta = jnp.sum(dh3[:, PAD_FRONT:CHUNK, :], axis=0)
    d_convw = jnp.sum(dconvw_p, axis=0)
    d_wgf = jnp.transpose(jnp.sum(dwgf_p, axis=0)[:, 0:RANK, :], (1, 0, 2)).reshape(RANK, N_HEADS * HEAD_K)
    d_wgb = jnp.transpose(jnp.sum(dwgb_p, axis=0)[:, RANK:2 * RANK, :], (1, 0, 2)).reshape(RANK, N_HEADS * HEAD_K)
    d_bg = jnp.sum(dbg_p, axis=0)
    d_bgf = d_bg[:, 0, :].reshape(1, N_HEADS * HEAD_K)
    d_bgb = d_bg[:, 1, :].reshape(1, N_HEADS * HEAD_K)
    part_shapes = [(N_META, D), (3, D), (RANK, 512), (RANK, 512), (1, D), (1, 512), (1, 512), (1, HEAD_V),
                   (1, D), (1, LANES)]
    parts = _pack([d_meta, d_convw, d_wgf, d_wgb, d_gpre, d_bgf, d_bgb, d_gamma, d_gpost, loss_acc[0:1, :]],
                  _rows_for(part_shapes))
    (parts_all,) = _exchange("gather_small_grads", [parts], ALL_FLIPS, (4, 2, 1), "gather")
    (g_meta, g_convw, g_wgf, g_wgb, g_npre, g_bgf, g_bgb, g_gnorm, g_npost, loss_row) = _unpack(
        _sum_slots(parts_all, "sum_small_grads"), part_shapes)
    loss = loss_row[0, 0]

    def col_shard(a, width):
        return lax.dynamic_slice_in_dim(a, shard * width, width, axis=a.ndim - 1)

    upd_shapes = [(N_META, D // 4), (3, D // 4), (RANK, HEAD_K), (RANK, HEAD_K), (1, D), (1, 512), (1, 512),
                  (1, HEAD_V), (1, D)]
    upd_rows = _rows_for(upd_shapes)
    small_w = _pack([meta_tokens, conv_w[0], w_gate_fwd[0], w_gate_bwd[0], norm_pre, b_gate_fwd, b_gate_bwd,
                     gla_norm, norm_post], upd_rows)
    small_g = _pack([col_shard(g_meta, D // 4), col_shard(g_convw, D // 4), col_shard(g_wgf, HEAD_K),
                     col_shard(g_wgb, HEAD_K), g_npre, g_bgf, g_bgb, g_gnorm, g_npost], upd_rows)
    small_m = _pack([m_meta_tokens, m_conv_w[0], m_w_gate_fwd[0], m_w_gate_bwd[0], m_norm_pre, m_b_gate_fwd,
                     m_b_gate_bwd, m_gla_norm, m_norm_post], upd_rows)
    small_v = _pack([v_meta_tokens, v_conv_w[0], v_w_gate_fwd[0], v_w_gate_bwd[0], v_norm_pre, v_b_gate_fwd,
                     v_b_gate_bwd, v_gla_norm, v_norm_post], upd_rows)
    small_out = [_unpack(a, upd_shapes) for a in _adamw(small_w, [small_g], small_m, small_v, "adamw_small")]

    plane_in = _sum_slots(got_in, "sum_w_in_grads", own=g_in_slots)
    plane_out = _sum_slots(got_out, "sum_w_out_grads", own=g_out_slots)
    other_in, other_out = _exchange("swap_plane_sums", [plane_in, plane_out], SIBLING_FLIPS, (0, 0, 0), "swap")
    big_in = _adamw(w_in[0], [plane_in, other_in], m_w_in[0], v_w_in[0], "adamw_w_in")
    w_out_rows = jnp.concatenate([w_out_conv[0], w_out_gla[0], w_merge_out[0]], axis=0)
    m_out_rows = jnp.concatenate([m_w_out_conv[0], m_w_out_gla[0], m_w_merge_out[0]], axis=0)
    v_out_rows = jnp.concatenate([v_w_out_conv[0], v_w_out_gla[0], v_w_merge_out[0]], axis=0)
    big_out = _adamw(w_out_rows, [plane_out, other_out], m_out_rows, v_out_rows, "adamw_w_out")

    results = []
    for kind in range(4):
        sm = small_out[kind]
        w_in_part = big_in[kind][None]
        outs3 = big_out[kind].reshape(3, 1, D // 4, D)
        results.extend([
            sm[0], sm[4], w_in_part, sm[1][None], sm[2][None], sm[5], sm[3][None], sm[6], sm[7],
            outs3[0], outs3[1], outs3[2], sm[8]])
    return (loss, grad_x, *results)
```

```python
import functools

import jax
import jax.numpy as jnp
from jax import lax
from jax.experimental import pallas as pl
from jax.experimental.pallas import tpu as pltpu

F32 = jnp.float32
BF16 = jnp.bfloat16
MESH = pl.DeviceIdType.MESH

D = 1024
N_META = 16
CHUNK = 64
PAD_FRONT = CHUNK - N_META
N_HEADS = 4
HEAD_K = 128
HEAD_V = 256
RANK = 16
EPS = 1e-6
GATE_NORM = 16.0
N_IN = 9248
SHARD_IN = N_IN // 4
LANES = 128
N_CONV_TILES = 8
W_CONV = 4096
W_GLA = 2048
W_TAIL = 3072
N_MAIN = W_CONV + W_GLA + W_TAIL
MIB = 1024 * 1024

ADAM_LR = 0.001
ADAM_B1 = 0.9
ADAM_B2 = 0.999
ADAM_EPS = 1e-08
ADAM_WD = 0.01
ADAM_STEP = 10


def _params(sem=None, vmem_mib=None):
    return pltpu.CompilerParams(
        dimension_semantics=sem,
        vmem_limit_bytes=None if vmem_mib is None else vmem_mib * MIB)


def _pick_tile(n, target, mult):
    best = None
    for t in range(mult, min(n, target) + 1, mult):
        if n % t == 0:
            best = t
    return n if best is None else best


def _sigmoid(v):
    return 1.0 / (1.0 + jnp.exp(-v))


def _log_sigmoid(v):
    return jnp.minimum(v, 0.0) - jnp.log(1.0 + jnp.exp(-jnp.abs(v)))


def _dot(a, b):
    return jnp.dot(a, b, preferred_element_type=F32)


def _dot_nt(a, b):
    return lax.dot_general(a, b, (((1,), (1,)), ((), ())), preferred_element_type=F32)


def _dot_tn(a, b):
    return lax.dot_general(a, b, (((0,), (0,)), ((), ())), preferred_element_type=F32)


def _tri_dot(tri, v):
    hi = v.astype(BF16)
    lo = (v - hi.astype(F32)).astype(BF16)
    return _dot(tri, hi) + _dot(tri, lo)


PLANE_FLIPS = ((1, 0, 0), (0, 1, 0), (1, 1, 0))
ALL_FLIPS = tuple((m >> 2 & 1, m >> 1 & 1, m & 1) for m in range(1, 8))
SIBLING_FLIPS = ((0, 0, 1),)


def _exchange(name, arrs, flips, slot_weights, mode):
    n = len(arrs)
    n_slots = 1
    for w in slot_weights:
        n_slots += w
    if mode == "gather":
        out_shape = [jax.ShapeDtypeStruct((n_slots,) + a.shape, a.dtype) for a in arrs]
    else:
        out_shape = [jax.ShapeDtypeStruct(a.shape, a.dtype) for a in arrs]

    def body(*refs):
        ins, outs = refs[:n], refs[n:2 * n]
        send_sems, recv_sems, local_sems = refs[2 * n:]
        pos = (lax.axis_index("x"), lax.axis_index("y"), lax.axis_index("c"))

        def slot_of(p):
            return p[0] * slot_weights[0] + p[1] * slot_weights[1] + p[2] * slot_weights[2]

        peers = [tuple(1 - pos[a] if f[a] else pos[a] for a in range(3)) for f in flips]
        me = slot_of(pos)
        local = []
        sends = []
        for i in range(n):
            if mode != "swap":
                src = ins[i] if mode == "gather" else ins[i].at[me]
                cp = pltpu.make_async_copy(src, outs[i].at[me], local_sems.at[i])
                cp.start()
                local.append(cp)
            for k, peer in enumerate(peers):
                if mode == "gather":
                    src, dst = ins[i], outs[i].at[me]
                elif mode == "scatter":
                    src, dst = ins[i].at[slot_of(peer)], outs[i].at[me]
                else:
                    src, dst = ins[i], outs[i]
                cp = pltpu.make_async_remote_copy(
                    src_ref=src, dst_ref=dst, send_sem=send_sems.at[i, k], recv_sem=recv_sems.at[i, k],
                    device_id=peer, device_id_type=MESH)
                cp.start()
                sends.append(cp)
        for i in range(n):
            for k, peer in enumerate(peers):
                if mode == "gather":
                    src, dst = ins[i], outs[i].at[slot_of(peer)]
                elif mode == "scatter":
                    src, dst = ins[i].at[me], outs[i].at[slot_of(peer)]
                else:
                    src, dst = ins[i], outs[i]
                arrival = pltpu.make_async_remote_copy(
                    src_ref=src, dst_ref=dst, send_sem=send_sems.at[i, k], recv_sem=recv_sems.at[i, k],
                    device_id=peer, device_id_type=MESH)
                arrival.wait_recv()
        for cp in sends:
            cp.wait_send()
        for cp in local:
            cp.wait()

    hbm = pl.BlockSpec(memory_space=pl.ANY)
    outs = pl.pallas_call(
        body, name=name, out_shape=out_shape,
        in_specs=[hbm] * n, out_specs=[hbm] * n,
        scratch_shapes=[pltpu.SemaphoreType.DMA((n, len(flips))),
                        pltpu.SemaphoreType.DMA((n, len(flips))),
                        pltpu.SemaphoreType.DMA((n,))],
        compiler_params=pltpu.CompilerParams(has_side_effects=True),
    )(*arrs)
    return list(outs)


def _gather_via_sibling(name, arrs):
    n = len(arrs)
    out_shape = [jax.ShapeDtypeStruct((4,) + a.shape, a.dtype) for a in arrs]

    def body(*refs):
        ins, outs = refs[:n], refs[n:2 * n]
        send_sems, recv_sems, local_sems = refs[2 * n:]
        x, y, c = lax.axis_index("x"), lax.axis_index("y"), lax.axis_index("c")
        me = 2 * x + y
        chips = [(1 - x, y), (x, 1 - y), (1 - x, 1 - y)]

        def half(ref, which):
            rows = ref.shape[0]
            cut = rows // 2 // 16 * 16
            return ref.at[pl.ds(0, cut)] if which == 0 else ref.at[pl.ds(cut, rows - cut)]

        def copy(src, dst, i, k, to):
            return pltpu.make_async_remote_copy(
                src_ref=src, dst_ref=dst, send_sem=send_sems.at[i, k], recv_sem=recv_sems.at[i, k],
                device_id=to, device_id_type=MESH)

        def run(mine):
            other = 1 - mine
            local, sends = [], []
            for i in range(n):
                cp = pltpu.make_async_copy(ins[i], outs[i].at[me], local_sems.at[i])
                cp.start()
                local.append(cp)
                for k, (px, py) in enumerate(chips):
                    cp = copy(half(ins[i], mine), half(outs[i].at[me], mine), i, k, (px, py, mine))
                    cp.start()
                    sends.append(cp)
            for k, (px, py) in enumerate(chips):
                slot = 2 * px + py
                for i in range(n):
                    landed = half(outs[i].at[slot], mine)
                    copy(half(ins[i], mine), landed, i, k, (px, py, mine)).wait_recv()
                    cp = copy(landed, landed, i, 3 + k, (x, y, other))
                    cp.start()
                    sends.append(cp)
            for k, (px, py) in enumerate(chips):
                slot = 2 * px + py
                for i in range(n):
                    passed = half(outs[i].at[slot], other)
                    copy(passed, passed, i, 3 + k, (x, y, other)).wait_recv()
            for cp in sends:
                cp.wait_send()
            for cp in local:
                cp.wait()

        for mine in (0, 1):
            pl.when(c == mine)(functools.partial(run, mine))

    hbm = pl.BlockSpec(memory_space=pl.ANY)
    outs = pl.pallas_call(
        body, name=name, out_shape=out_shape,
        in_specs=[hbm] * n, out_specs=[hbm] * n,
        scratch_shapes=[pltpu.SemaphoreType.DMA((n, 6)), pltpu.SemaphoreType.DMA((n, 6)),
                        pltpu.SemaphoreType.DMA((n,))],
        compiler_params=pltpu.CompilerParams(has_side_effects=True),
    )(*arrs)
    return list(outs)


HBM_SPEC = pl.BlockSpec(memory_space=pltpu.HBM)
SEM_SPEC = pl.BlockSpec(memory_space=pltpu.SEMAPHORE)
DATAFLOW = pltpu.SideEffectType.DATAFLOW_SIDE_EFFECTING


def _plane_peers():
    x, y, c = lax.axis_index("x"), lax.axis_index("y"), lax.axis_index("c")
    return 2 * x + y, [((1 - x, y, c), 2 * (1 - x) + y), ((x, 1 - y, c), 2 * x + 1 - y),
                       ((1 - x, 1 - y, c), 2 * (1 - x) + 1 - y)]


def _plane_start(name, arrs, mode, after):
    n = len(arrs)
    lands = [lax.empty(((4,) + a.shape) if mode == "gather" else a.shape, a.dtype) for a in arrs]

    def body(*refs):
        srcs, landing = refs[:n], refs[n:2 * n]
        send_sems, recv_sems = refs[2 * n + 1], refs[2 * n + 2]
        token = refs[-1]
        me, peers = _plane_peers()
        for i in range(n):
            for k, (peer, peer_slot) in enumerate(peers):
                src = srcs[i] if mode == "gather" else srcs[i].at[peer_slot]
                pltpu.make_async_remote_copy(
                    src_ref=src, dst_ref=landing[i].at[me], send_sem=send_sems.at[3 * i + k],
                    recv_sem=recv_sems.at[3 * i + k], device_id=peer, device_id_type=MESH).start()
        token[...] = jnp.zeros_like(token)

    hbm_in = [pltpu.with_memory_space_constraint(a, pltpu.HBM) for a in list(arrs) + lands]
    out = pl.pallas_call(
        body, name=name,
        out_shape=[pltpu.SemaphoreType.DMA((3 * n,)), pltpu.SemaphoreType.DMA((3 * n,))]
                  + [pltpu.HBM(a.shape, a.dtype) for a in hbm_in]
                  + [jax.ShapeDtypeStruct((8, LANES), F32)],
        in_specs=[HBM_SPEC] * (2 * n) + [pl.BlockSpec(memory_space=pl.ANY)],
        out_specs=[SEM_SPEC, SEM_SPEC] + [HBM_SPEC] * (2 * n) + [pl.BlockSpec(memory_space=pltpu.VMEM)],
        input_output_aliases={i: 2 + i for i in range(2 * n)},
        compiler_params=pltpu.CompilerParams(has_side_effects=DATAFLOW),
    )(*hbm_in, after)
    return out[:-1], out[-1]


def _plane_wait(name, state, mode, after):
    send_sems, recv_sems = state[0], state[1]
    bufs = list(state[2:])
    n = len(bufs) // 2

    def body(*refs):
        srcs, landing = refs[:n], refs[n:2 * n]
        send_sems, recv_sems = refs[2 * n], refs[2 * n + 1]
        me, peers = _plane_peers()
        for i in range(n):
            for k, (peer, peer_slot) in enumerate(peers):
                src = srcs[i] if mode == "gather" else srcs[i].at[peer_slot]
                cp = pltpu.make_async_remote_copy(
                    src_ref=src, dst_ref=landing[i].at[peer_slot], send_sem=send_sems.at[3 * i + k],
                    recv_sem=recv_sems.at[3 * i + k], device_id=peer, device_id_type=MESH)
                cp.wait_send()
                cp.wait_recv()

    out = pl.pallas_call(
        body, name=name,
        out_shape=[pltpu.HBM(a.shape, a.dtype) for a in bufs],
        in_specs=[HBM_SPEC] * (2 * n) + [SEM_SPEC, SEM_SPEC, pl.BlockSpec(memory_space=pl.ANY)],
        out_specs=[HBM_SPEC] * (2 * n),
        input_output_aliases={i: i for i in range(2 * n)},
        compiler_params=pltpu.CompilerParams(has_side_effects=DATAFLOW),
    )(*bufs, send_sems, recv_sems, after)
    return list(out[n:])


def _tile_2d(rows, cols, row_mult, max_elems=512 * 1024):
    if rows % row_mult == 0:
        rt = _pick_tile(rows, max(row_mult, max_elems // cols), row_mult)
        return (rt, cols), rows // rt, lambda i: (i, 0)
    ct = _pick_tile(cols, max(LANES, max_elems // rows), LANES)
    return (rows, ct), cols // ct, lambda i: (0, i)


def _cast_bf16(a, name):
    block, steps, index = _tile_2d(a.shape[0], a.shape[1], 16)

    def body(a_ref, o_ref):
        o_ref[...] = a_ref[...].astype(BF16)

    return pl.pallas_call(
        body, name=name, grid=(steps,),
        in_specs=[pl.BlockSpec(block, index)],
        out_specs=pl.BlockSpec(block, index),
        out_shape=jax.ShapeDtypeStruct(a.shape, BF16),
        compiler_params=_params(("parallel",)),
    )(a)


def _sum_slots(buf, name, own=None):
    n_slots, rows, cols = buf.shape
    (br, bc), steps, index = _tile_2d(rows, cols, 16, 320 * 1024)
    n_in = 1 if own is None else 2

    def body(*refs):
        b_ref, o_ref = refs[0], refs[-1]
        me = None if own is None else 2 * lax.axis_index("x") + lax.axis_index("y")
        acc = None
        for s in range(n_slots):
            term = b_ref[s] if own is None else jnp.where(me == s, refs[1][s], b_ref[s])
            acc = term.astype(F32) if acc is None else acc + term.astype(F32)
        o_ref[...] = acc

    return pl.pallas_call(
        body, name=name, grid=(steps,),
        in_specs=[pl.BlockSpec((n_slots, br, bc), lambda i: (0,) + index(i))] * n_in,
        out_specs=pl.BlockSpec((br, bc), index),
        out_shape=jax.ShapeDtypeStruct((rows, cols), F32),
        compiler_params=_params(("parallel",), 48),
    )(*([buf] if own is None else [buf, own]))


def _adamw(w, grads, m, v, name, grad_row=0):
    rows, cols = w.shape
    (rt, _), _, _ = _tile_2d(rows, cols, 8, 160 * 1024)
    assert grad_row % rt == 0
    n_g = len(grads)
    c1 = 1.0 - ADAM_B1 ** ADAM_STEP
    c2 = 1.0 - ADAM_B2 ** ADAM_STEP

    def body(*refs):
        w_ref = refs[0]
        g_refs = refs[1:1 + n_g]
        m_ref, v_ref, g_out, d_out, m_out, v_out = refs[1 + n_g:]
        g = g_refs[0][...]
        for r in g_refs[1:]:
            g = g + r[...]
        m_new = ADAM_B1 * m_ref[...] + (1.0 - ADAM_B1) * g
        v_new = ADAM_B2 * v_ref[...] + (1.0 - ADAM_B2) * (g * g)
        m_hat = m_new / c1
        v_hat = v_new / c2
        g_out[...] = g
        d_out[...] = -ADAM_LR * (m_hat / (jnp.sqrt(v_hat) + ADAM_EPS) + ADAM_WD * w_ref[...])
        m_out[...] = m_new
        v_out[...] = v_new

    spec = pl.BlockSpec((rt, cols), lambda i: (i, 0))
    grad_spec = pl.BlockSpec((rt, cols), lambda i: (i + grad_row // rt, 0))
    shape = jax.ShapeDtypeStruct((rows, cols), F32)
    return pl.pallas_call(
        body, name=name, grid=(rows // rt,),
        in_specs=[spec] + [grad_spec] * n_g + [spec] * 2, out_specs=[spec] * 4, out_shape=[shape] * 4,
        compiler_params=_params(("parallel",), 48),
    )(w, *grads, m, v)


def _in_proj(h, g_pre, w_main_t, w_lr_t):
    t_rows = h.shape[0]
    tm = _pick_tile(t_rows, 384, LANES)
    n_main = w_main_t.shape[0]

    def body(h_ref, g_ref, w_hbm, wlr_ref, proj_ref, ut_ref, lr_ref, w_s, w_sem):
        @pl.when(pl.program_id(0) == 0)
        def _():
            cp = pltpu.make_async_copy(w_hbm, w_s, w_sem)
            cp.start()
            cp.wait()

        hh = h_ref[...]
        rstd = lax.rsqrt(jnp.mean(hh * hh, axis=-1, keepdims=True) + EPS)
        uf = hh * rstd * g_ref[...]
        u = uf.astype(BF16)
        ut_ref[...] = jnp.transpose(uf).astype(BF16)
        lr_ref[...] = _dot_nt(u, wlr_ref[...])
        for j in range(n_main // D):
            cols = slice(j * D, (j + 1) * D)
            proj_ref[:, cols] = _dot_nt(u, w_s[cols, :]).astype(BF16)

    return pl.pallas_call(
        body, name="in_proj", grid=(t_rows // tm,),
        in_specs=[pl.BlockSpec((tm, D), lambda i: (i, 0)),
                  pl.BlockSpec((1, D), lambda i: (0, 0)),
                  pl.BlockSpec(memory_space=pl.ANY),
                  pl.BlockSpec((LANES, D), lambda i: (0, 0))],
        out_specs=[pl.BlockSpec((tm, n_main), lambda i: (i, 0)),
                   pl.BlockSpec((D, tm), lambda i: (0, i)),
                   pl.BlockSpec((tm, LANES), lambda i: (i, 0))],
        out_shape=[jax.ShapeDtypeStruct((t_rows, n_main), BF16),
                   jax.ShapeDtypeStruct((D, t_rows), BF16),
                   jax.ShapeDtypeStruct((t_rows, LANES), F32)],
        scratch_shapes=[pltpu.VMEM((n_main, D), BF16), pltpu.SemaphoreType.DMA],
        compiler_params=_params(("arbitrary",), 56),
    )(h, g_pre, w_main_t, w_lr_t)


def _conv_parts(p_ref, w_ref):
    cb = p_ref[:, 0:128].astype(F32)
    cc = p_ref[:, 128:256].astype(F32)
    cx = p_ref[:, 256:384].astype(F32)
    cz = p_ref[:, 384:512].astype(F32)
    rows = cb.shape[0]
    w = w_ref[...]
    p = cc * cx
    conv = pltpu.roll(p, 1, 0) * w[0:1] + p * w[1:2] + pltpu.roll(p, rows - 1, 0) * w[2:3]
    sz = _sigmoid(cz)
    return cb, cc, cx, cz, p, conv, sz, w


def _conv_fwd(proj, conv_w, n_seq, lf):
    def body(p_ref, w_ref, y_ref):
        cb, _, _, cz, _, conv, sz, _ = _conv_parts(p_ref, w_ref)
        y_ref[...] = (cb * conv * (cz * sz)).astype(BF16)

    return pl.pallas_call(
        body, name="conv_fwd", grid=(n_seq, N_CONV_TILES),
        in_specs=[pl.BlockSpec((lf, 512), lambda b, j: (b, j)),
                  pl.BlockSpec((3, 128), lambda b, j: (0, j))],
        out_specs=pl.BlockSpec((lf, 128), lambda b, j: (b, j)),
        out_shape=jax.ShapeDtypeStruct((n_seq * lf, D), BF16),
        compiler_params=_params(("parallel", "parallel"), 48),
    )(proj, conv_w)


def _conv_bwd(proj, conv_w, dyc, n_seq, lf):
    def body(p_ref, w_ref, dy_ref, dp_ref, dw_ref):
        cb, cc, cx, cz, p, conv, sz, w = _conv_parts(p_ref, w_ref)
        rows = cb.shape[0]
        dy = dy_ref[...].astype(F32)
        silu = cz * sz
        dcb = dy * conv * silu
        dconv = dy * cb * silu
        dcz = dy * cb * conv * (sz * (1.0 + cz * (1.0 - sz)))
        d_next = pltpu.roll(dconv, rows - 1, 0)
        d_prev = pltpu.roll(dconv, 1, 0)
        dp = d_next * w[0:1] + dconv * w[1:2] + d_prev * w[2:3]
        dp_ref[:, 0:128] = dcb.astype(BF16)
        dp_ref[:, 128:256] = (dp * cx).astype(BF16)
        dp_ref[:, 256:384] = (dp * cc).astype(BF16)
        dp_ref[:, 384:512] = dcz.astype(BF16)
        dw_ref[0:1, :] = jnp.sum(dconv * pltpu.roll(p, 1, 0), axis=0, keepdims=True)
        dw_ref[1:2, :] = jnp.sum(dconv * p, axis=0, keepdims=True)
        dw_ref[2:3, :] = jnp.sum(dconv * pltpu.roll(p, rows - 1, 0), axis=0, keepdims=True)

    return pl.pallas_call(
        body, name="conv_bwd", grid=(n_seq, N_CONV_TILES),
        in_specs=[pl.BlockSpec((lf, 512), lambda b, j: (b, j)),
                  pl.BlockSpec((3, 128), lambda b, j: (0, j)),
                  pl.BlockSpec((lf, 128), lambda b, j: (b, j))],
        out_specs=[pl.BlockSpec((lf, 512), lambda b, j: (b, j)),
                   pl.BlockSpec((None, 3, 128), lambda b, j: (b, 0, j))],
        out_shape=[jax.ShapeDtypeStruct((n_seq * lf, W_CONV), BF16),
                   jax.ShapeDtypeStruct((n_seq, 3, D), F32)],
        compiler_params=_params(("parallel", "parallel"), 48),
    )(proj, conv_w, dyc)


GROUP = 3
GROUP_ROWS = GROUP * CHUNK


def _row_group(shape):
    row = lax.broadcasted_iota(jnp.int32, shape, 0)
    grp = jnp.zeros(shape, jnp.int32)
    for r in range(1, GROUP):
        grp = grp + (row >= r * CHUNK).astype(jnp.int32)
    return grp


def _lane_group(shape, width):
    lane = lax.broadcasted_iota(jnp.int32, shape, 1)
    grp = jnp.zeros(shape, jnp.int32)
    for r in range(1, GROUP):
        grp = grp + (lane >= r * width).astype(jnp.int32)
    return grp


def _group_masks(direction):
    shape = (GROUP_ROWS, GROUP_ROWS)
    row = lax.broadcasted_iota(jnp.int32, shape, 0)
    col = lax.broadcasted_iota(jnp.int32, shape, 1)
    same = _row_group(shape) == _lane_group(shape, CHUNK)
    lower = same & (col <= row)
    upper = same & (col >= row)
    if direction == 0:
        return lower.astype(BF16), upper.astype(BF16), lower
    return upper.astype(BF16), lower.astype(BF16), same & (col > row)


def _diag_blocks(v):
    w = v.shape[1]
    wide = jnp.concatenate([v] * GROUP, axis=1)
    return jnp.where(_row_group(wide.shape) == _lane_group(wide.shape, w), wide, jnp.zeros_like(wide))


def _pick_diag(wide):
    w = wide.shape[1] // GROUP
    grp = _row_group((GROUP_ROWS, w))
    out = wide[:, 0:w]
    for r in range(1, GROUP):
        out = jnp.where(grp == r, wide[:, r * w:(r + 1) * w], out)
    return out


def _per_chunk_rows(rows_of_chunk):
    w = rows_of_chunk[0].shape[1]
    return jnp.concatenate([jnp.broadcast_to(v, (CHUNK, w)) for v in rows_of_chunk], axis=0)


def _chunk_end_rows(direction, b):
    at = CHUNK - 1 if direction == 0 else 0
    return [b[r * CHUNK + at:r * CHUNK + at + 1, :] for r in range(GROUP)]


def _gla_gates(lr_bf, wg_ref, bg_ref, lf):
    z = _dot(lr_bf, wg_ref[...]) + bg_ref[...]
    valid = lax.broadcasted_iota(jnp.int32, (lf, HEAD_K), 0) >= PAD_FRONT
    return z, valid


def _group_unroll(n_groups):
    return n_groups if n_groups <= 11 else 1


def _group_rows(g):
    return pl.ds(pl.multiple_of(g * GROUP_ROWS, GROUP_ROWS), GROUP_ROWS)


def _chunk_decay(direction, g, r, b_s):
    base = g * GROUP_ROWS + r * CHUNK
    if direction == 0:
        grp = b_s[pl.ds(pl.multiple_of(base + CHUNK - 8, 8), 8), :]
        return jnp.exp(grp[7:8, :])
    grp = b_s[pl.ds(pl.multiple_of(base, 8), 8), :]
    return jnp.exp(grp[0:1, :])


def _state_scan(direction, n_groups, b_s, st_s, reverse):
    ascending = (direction == 0) != reverse

    def step(i, carry):
        g = i if ascending else n_groups - 1 - i
        for rr in range(GROUP):
            r = rr if ascending else GROUP - 1 - rr
            lanes = slice(r * HEAD_K, (r + 1) * HEAD_K)
            decay = _chunk_decay(direction, g, r, b_s)
            local = st_s[g, :, lanes]
            st_s[g, :, lanes] = carry
            carry = (local + carry * decay) if reverse else (carry * decay + local)
        return carry

    lax.fori_loop(0, n_groups, step, jnp.zeros((HEAD_V, HEAD_K), F32))


def _gla_states(direction, n_groups, qkv_ref, g_s, b_s, st_s, tri):
    def local(g, carry):
        rows = _group_rows(g)
        b = _tri_dot(tri, g_s[rows, :])
        b_s[rows, :] = b
        b_end = _per_chunk_rows(_chunk_end_rows(direction, b))
        k = qkv_ref[rows, 128:256].astype(F32)
        v = qkv_ref[rows, 256:512]
        k_dec = (k * jnp.exp(b_end - b)).astype(BF16)
        st_s[g] = _dot_tn(v, _diag_blocks(k_dec))
        return carry

    lax.fori_loop(0, n_groups, local, 0, unroll=_group_unroll(n_groups))
    _state_scan(direction, n_groups, b_s, st_s, False)


def _gla_fwd(proj, lr, wgf, wgb, bgf, bgb, n_seq, lf):
    assert lf % GROUP_ROWS == 0
    n_groups = lf // GROUP_ROWS
    scale = HEAD_K ** -0.5

    def body(qkv_ref, lr_ref, wgf_ref, wgb_ref, bgf_ref, bgb_ref, o_ref, g_s, b_s, st_s):
        lr_bf = lr_ref[...].astype(BF16)
        for direction in (0, 1):
            wg_ref, bg_ref = ((wgf_ref, bgf_ref), (wgb_ref, bgb_ref))[direction]
            z, valid = _gla_gates(lr_bf, wg_ref, bg_ref, lf)
            g_s[...] = jnp.where(valid, _log_sigmoid(z) / GATE_NORM, 0.0)
            tri, _, smask = _group_masks(direction)
            _gla_states(direction, n_groups, qkv_ref, g_s, b_s, st_s, tri)

            def out(g, carry):
                rows = _group_rows(g)
                b = b_s[rows, :]
                q = qkv_ref[rows, 0:128].astype(F32) * scale
                k = qkv_ref[rows, 128:256].astype(F32)
                v = qkv_ref[rows, 256:512]
                q_in = (q * jnp.exp(b)).astype(BF16)
                k_in = (k * jnp.exp(-b)).astype(BF16)
                s = jnp.where(smask, _dot_nt(q_in, k_in), 0.0).astype(BF16)
                o = _dot(s, v) + _dot_nt(_diag_blocks(q_in), st_s[g].astype(BF16))
                if direction == 0:
                    o_ref[rows, :] = o
                else:
                    o_ref[rows, :] = o_ref[rows, :] + o
                return carry

            lax.fori_loop(0, n_groups, out, 0, unroll=_group_unroll(n_groups))

    return pl.pallas_call(
        body, name="gla_fwd", grid=(n_seq, N_HEADS),
        in_specs=[pl.BlockSpec((lf, 512), lambda b, h: (b, N_CONV_TILES + h)),
                  pl.BlockSpec((lf, LANES), lambda b, h: (b, 0)),
                  pl.BlockSpec((None, LANES, HEAD_K), lambda b, h: (h, 0, 0)),
                  pl.BlockSpec((None, LANES, HEAD_K), lambda b, h: (h, 0, 0)),
                  pl.BlockSpec((None, 1, HEAD_K), lambda b, h: (h, 0, 0)),
                  pl.BlockSpec((None, 1, HEAD_K), lambda b, h: (h, 0, 0))],
        out_specs=pl.BlockSpec((lf, HEAD_V), lambda b, h: (b, h)),
        out_shape=jax.ShapeDtypeStruct((n_seq * lf, D), F32),
        scratch_shapes=[pltpu.VMEM((lf, HEAD_K), F32), pltpu.VMEM((lf, HEAD_K), F32),
                        pltpu.VMEM((n_groups, HEAD_V, GROUP * HEAD_K), F32)],
        compiler_params=_params(("parallel", "parallel"), 48),
    )(proj, lr, wgf, wgb, bgf, bgb)


def _gla_bwd(proj, lr, d_o, wgf, wgb, bgf, bgb, n_seq, lf, token):
    assert lf % GROUP_ROWS == 0
    n_groups = lf // GROUP_ROWS
    scale = HEAD_K ** -0.5

    def body(qkv_ref, lr_ref, do_ref, wgf_ref, wgb_ref, bgf_ref, bgb_ref, token_ref,
             dqkv_ref, dlr_ref, dwgf_ref, dwgb_ref, dbg_ref,
             g_s, b_s, fac_s, dg_s, st_s, dst_s, acc_s):
        lr_bf = lr_ref[...].astype(BF16)
        dlr = jnp.zeros((lf, LANES), F32)
        for direction in (0, 1):
            wg_ref, bg_ref = ((wgf_ref, bgf_ref), (wgb_ref, bgb_ref))[direction]
            z, valid = _gla_gates(lr_bf, wg_ref, bg_ref, lf)
            g_s[...] = jnp.where(valid, _log_sigmoid(z) / GATE_NORM, 0.0)
            fac_s[...] = jnp.where(valid, _sigmoid(-z) / GATE_NORM, 0.0)
            tri, tri_t, smask = _group_masks(direction)
            end_row = CHUNK - 1 if direction == 0 else 0
            _gla_states(direction, n_groups, qkv_ref, g_s, b_s, st_s, tri)

            def state_grad_local(g, carry):
                rows = _group_rows(g)
                q = qkv_ref[rows, 0:128].astype(F32) * scale
                q_in = (q * jnp.exp(b_s[rows, :])).astype(BF16)
                dst_s[g] = _dot_tn(do_ref[rows, :], _diag_blocks(q_in))
                return carry

            lax.fori_loop(0, n_groups, state_grad_local, 0, unroll=_group_unroll(n_groups))
            _state_scan(direction, n_groups, b_s, dst_s, True)

            def group_grads(g, carry):
                rows = _group_rows(g)
                b = b_s[rows, :]
                ends = _chunk_end_rows(direction, b)
                b_end = _per_chunk_rows(ends)
                q = qkv_ref[rows, 0:128].astype(F32) * scale
                k = qkv_ref[rows, 128:256].astype(F32)
                v = qkv_ref[rows, 256:512]
                d_out = do_ref[rows, :]
                e_pos = jnp.exp(b)
                e_neg = jnp.exp(-b)
                e_end = jnp.exp(b_end - b)
                q_in = q * e_pos
                k_in = k * e_neg
                k_dec = k * e_end
                q_in_bf = q_in.astype(BF16)
                k_in_bf = k_in.astype(BF16)
                state = st_s[g]
                d_state = dst_s[g]
                state_bf = state.astype(BF16)
                d_state_bf = d_state.astype(BF16)
                s = jnp.where(smask, _dot_nt(q_in_bf, k_in_bf), 0.0).astype(BF16)
                ds = jnp.where(smask, _dot_nt(d_out, v), 0.0).astype(BF16)
                dv = _dot_tn(s, d_out) + _dot_nt(_diag_blocks(k_dec.astype(BF16)), d_state_bf)
                dq_in = _dot(ds, k_in_bf) + _pick_diag(_dot(d_out, state_bf))
                dk_in = _dot_tn(ds, q_in_bf)
                dk_dec = _pick_diag(_dot(v, d_state_bf))
                dq = dq_in * e_pos * scale
                dk = dk_in * e_neg + dk_dec * e_end
                if direction == 0:
                    acc_s[rows, 0:128] = dq
                    acc_s[rows, 128:256] = dk
                    acc_s[rows, 256:512] = dv
                else:
                    dqkv_ref[rows, 0:128] = (acc_s[rows, 0:128] + dq).astype(BF16)
                    dqkv_ref[rows, 128:256] = (acc_s[rows, 128:256] + dk).astype(BF16)
                    dqkv_ref[rows, 256:512] = (acc_s[rows, 256:512] + dv).astype(BF16)
                dkk = dk_dec * k_dec
                db = dq_in * q_in - dk_in * k_in - dkk
                d_decay = jnp.sum(d_state * state, axis=0, keepdims=True)
                db_end = [jnp.sum(dkk[r * CHUNK:(r + 1) * CHUNK, :], axis=0, keepdims=True)
                          + d_decay[:, r * HEAD_K:(r + 1) * HEAD_K] * jnp.exp(ends[r]) for r in range(GROUP)]
                row = lax.broadcasted_iota(jnp.int32, (GROUP_ROWS, HEAD_K), 0)
                at_end = row == end_row
                for r in range(1, GROUP):
                    at_end = at_end | (row == r * CHUNK + end_row)
                db = db + jnp.where(at_end, _per_chunk_rows(db_end), 0.0)
                dg_s[rows, :] = _tri_dot(tri_t, db)
                return carry

            lax.fori_loop(0, n_groups, group_grads, 0, unroll=_group_unroll(n_groups))

            dz = dg_s[...] * fac_s[...]
            dz_bf = dz.astype(BF16)
            dbg_ref[direction:direction + 1, :] = jnp.sum(dz, axis=0, keepdims=True)
            (dwgf_ref, dwgb_ref)[direction][...] = _dot_tn(lr_bf, dz_bf)
            dlr = dlr + _dot_nt(dz_bf, wg_ref[...])

        @pl.when(pl.program_id(1) == 0)
        def _():
            dlr_ref[...] = dlr

        @pl.when(pl.program_id(1) != 0)
        def _():
            dlr_ref[...] = dlr_ref[...] + dlr

    gate_w = pl.BlockSpec((None, LANES, HEAD_K), lambda b, h: (h, 0, 0))
    gate_b = pl.BlockSpec((None, 1, HEAD_K), lambda b, h: (h, 0, 0))
    return pl.pallas_call(
        body, name="gla_bwd", grid=(n_seq, N_HEADS),
        in_specs=[pl.BlockSpec((lf, 512), lambda b, h: (b, N_CONV_TILES + h)),
                  pl.BlockSpec((lf, LANES), lambda b, h: (b, 0)),
                  pl.BlockSpec((lf, HEAD_V), lambda b, h: (b, h)),
                  gate_w, gate_w, gate_b, gate_b,
                  pl.BlockSpec((8, LANES), lambda b, h: (0, 0))],
        out_specs=[pl.BlockSpec((lf, 512), lambda b, h: (b, h)),
                   pl.BlockSpec((lf, LANES), lambda b, h: (b, 0)),
                   pl.BlockSpec((None, None, LANES, HEAD_K), lambda b, h: (b, h, 0, 0)),
                   pl.BlockSpec((None, None, LANES, HEAD_K), lambda b, h: (b, h, 0, 0)),
                   pl.BlockSpec((None, None, 2, HEAD_K), lambda b, h: (b, h, 0, 0))],
        out_shape=[jax.ShapeDtypeStruct((n_seq * lf, W_GLA), BF16),
                   jax.ShapeDtypeStruct((n_seq * lf, LANES), F32),
                   jax.ShapeDtypeStruct((n_seq, N_HEADS, LANES, HEAD_K), F32),
                   jax.ShapeDtypeStruct((n_seq, N_HEADS, LANES, HEAD_K), F32),
                   jax.ShapeDtypeStruct((n_seq, N_HEADS, 2, HEAD_K), F32)],
        scratch_shapes=[pltpu.VMEM((lf, HEAD_K), F32), pltpu.VMEM((lf, HEAD_K), F32),
                        pltpu.VMEM((lf, HEAD_K), F32), pltpu.VMEM((lf, HEAD_K), F32),
                        pltpu.VMEM((n_groups, HEAD_V, GROUP * HEAD_K), F32),
                        pltpu.VMEM((n_groups, HEAD_V, GROUP * HEAD_K), F32),
                        pltpu.VMEM((lf, 512), F32)],
        compiler_params=_params(("parallel", "arbitrary"), 56),
    )(proj, lr, d_o, wgf, wgb, bgf, bgb, token)


def _tail(h, tgt, yc, o, proj, w3, gamma, g_post, lf):
    t_rows = h.shape[0]
    tm = _pick_tile(t_rows, 256, CHUNK)
    n_chunks = lf // CHUNK
    per_tile = tm // CHUNK

    def body(h_ref, tgt_ref, yc_ref, o_ref, r_ref, ma_ref, mb_ref, w_hbm, gamma_ref, gpost_ref,
             dres_ref, yg_ref, merged_ref, dout_ref, dpc_ref, dpg_ref, dyc_ref, do_ref, dtail_ref,
             loss_ref, dgpost_ref, dgamma_ref, w_s, w_sem):
        i = pl.program_id(0)

        @pl.when(i == 0)
        def _():
            cp = pltpu.make_async_copy(w_hbm, w_s, w_sem)
            cp.start()
            cp.wait()
            loss_ref[...] = jnp.zeros_like(loss_ref)
            dgpost_ref[...] = jnp.zeros_like(dgpost_ref)
            dgamma_ref[...] = jnp.zeros_like(dgamma_ref)

        gamma = gamma_ref[...]
        o = o_ref[...]
        r = r_ref[...].astype(F32)
        sr = _sigmoid(r)
        silu_r = r * sr
        n_parts, rstd_parts = [], []
        for hd in range(N_HEADS):
            oh = o[:, hd * HEAD_V:(hd + 1) * HEAD_V]
            rstd = lax.rsqrt(jnp.mean(oh * oh, axis=-1, keepdims=True) + EPS)
            n_parts.append(oh * rstd)
            rstd_parts.append(rstd)
        n = jnp.concatenate(n_parts, axis=-1)
        gamma_t = jnp.concatenate([gamma] * N_HEADS, axis=-1)
        yg = n * gamma_t * silu_r
        yg_bf = yg.astype(BF16)
        yg_ref[...] = yg_bf
        yc = yc_ref[...]
        pc = _dot(yc, w_s[0])
        pg = _dot(yg_bf, w_s[1])
        sa = _sigmoid(ma_ref[...].astype(F32))
        sb = _sigmoid(mb_ref[...].astype(F32))
        merged = (sa * pc + sb * pg).astype(BF16)
        merged_ref[...] = merged
        out = _dot(merged, w_s[2])
        rstd2 = lax.rsqrt(jnp.mean(out * out, axis=-1, keepdims=True) + EPS)
        nn = out * rstd2
        gpost = gpost_ref[...]
        y = h_ref[...] + nn * gpost

        rowi = lax.broadcasted_iota(jnp.int32, (tm, 1), 0)
        keep = jnp.zeros((tm, 1), F32)
        for kk in range(per_tile):
            is_tok = ((i * per_tile + kk) % n_chunks) != 0
            f = jnp.where(is_tok, 1.0, 0.0)
            keep = jnp.where((rowi >= kk * CHUNK) & (rowi < (kk + 1) * CHUNK), f, keep)
        diff = (y - tgt_ref[...]) * keep
        loss_ref[...] += jnp.sum(diff * diff) * (0.5 / D)
        dy = diff * (1.0 / D)
        dres_ref[...] = dy
        dgpost_ref[...] += jnp.sum(dy * nn, axis=0, keepdims=True)
        dn = dy * gpost
        dout = (rstd2 * (dn - nn * jnp.mean(dn * nn, axis=-1, keepdims=True))).astype(BF16)
        dout_ref[...] = dout
        dmerged = _dot_nt(dout, w_s[2])
        dpc = (dmerged * sa).astype(BF16)
        dpg = (dmerged * sb).astype(BF16)
        dpc_ref[...] = dpc
        dpg_ref[...] = dpg
        dtail_ref[:, D:2 * D] = (dmerged * pc * (sa * (1.0 - sa))).astype(BF16)
        dtail_ref[:, 2 * D:3 * D] = (dmerged * pg * (sb * (1.0 - sb))).astype(BF16)
        dyc_ref[...] = _dot_nt(dpc, w_s[0]).astype(BF16)
        dyg = _dot_nt(dpg, w_s[1])
        dtail_ref[:, 0:D] = (dyg * n * gamma_t * (sr * (1.0 + r * (1.0 - sr)))).astype(BF16)
        dgam_full = jnp.sum(dyg * n * silu_r, axis=0, keepdims=True)
        dgam = dgam_full[:, 0:HEAD_V]
        for hd in range(1, N_HEADS):
            dgam = dgam + dgam_full[:, hd * HEAD_V:(hd + 1) * HEAD_V]
        dgamma_ref[...] += dgam
        dng = dyg * gamma_t * silu_r
        do_parts = []
        for hd in range(N_HEADS):
            sl = slice(hd * HEAD_V, (hd + 1) * HEAD_V)
            dnh = dng[:, sl]
            nh = n_parts[hd]
            do_parts.append(rstd_parts[hd] * (dnh - nh * jnp.mean(dnh * nh, axis=-1, keepdims=True)))
        do_ref[...] = jnp.concatenate(do_parts, axis=-1).astype(BF16)

    row = lambda c: pl.BlockSpec((tm, D), lambda i: (i, c))
    const = lambda shape: pl.BlockSpec(shape, lambda i: (0, 0))
    act = jax.ShapeDtypeStruct((t_rows, D), BF16)
    return pl.pallas_call(
        body, name="tail", grid=(t_rows // tm,),
        in_specs=[row(0), row(0), row(0), row(0), row(6), row(7), row(8),
                  pl.BlockSpec(memory_space=pl.ANY), const((1, HEAD_V)), const((1, D))],
        out_specs=[row(0)] * 8 + [pl.BlockSpec((tm, W_TAIL), lambda i: (i, 0)),
                                  const((8, LANES)), const((1, D)), const((1, HEAD_V))],
        out_shape=[jax.ShapeDtypeStruct((t_rows, D), F32)] + [act] * 7
                  + [jax.ShapeDtypeStruct((t_rows, W_TAIL), BF16),
                     jax.ShapeDtypeStruct((8, LANES), F32),
                     jax.ShapeDtypeStruct((1, D), F32),
                     jax.ShapeDtypeStruct((1, HEAD_V), F32)],
        scratch_shapes=[pltpu.VMEM((3, D, D), BF16), pltpu.SemaphoreType.DMA],
        compiler_params=_params(("arbitrary",), 56),
    )(h, tgt, yc, o, proj, proj, proj, w3, gamma, g_post)


def _wgrad(a, b, name, out_dtype=BF16):
    t_rows, m = a.shape
    n = b.shape[1]
    tn = D if n % D == 0 else n
    tk = _pick_tile(t_rows, 528, 16)
    n_k = t_rows // tk

    def body(a_ref, b_ref, o_ref, acc):
        k = pl.program_id(1)

        @pl.when(k == 0)
        def _():
            acc[...] = jnp.zeros_like(acc)

        acc[...] += _dot_tn(a_ref[...].astype(BF16), b_ref[...].astype(BF16))

        @pl.when(k == n_k - 1)
        def _():
            o_ref[...] = acc[...].astype(out_dtype)

    return pl.pallas_call(
        body, name=name, grid=(n // tn, n_k),
        in_specs=[pl.BlockSpec((tk, m), lambda j, k: (k, 0)),
                  pl.BlockSpec((tk, tn), lambda j, k: (k, j))],
        out_specs=pl.BlockSpec((m, tn), lambda j, k: (0, j)),
        out_shape=jax.ShapeDtypeStruct((m, n), out_dtype),
        scratch_shapes=[pltpu.VMEM((m, tn), F32)],
        compiler_params=_params(("parallel", "arbitrary"), 48),
    )(a, b)


def _wgrad_t(a_t, b, name, out_dtype=BF16):
    m, t_rows = a_t.shape
    n = b.shape[1]
    tn = D if n % D == 0 else n
    tk = _pick_tile(t_rows, 768, LANES)
    n_k = t_rows // tk

    def body(a_ref, b_ref, o_ref, acc):
        k = pl.program_id(1)

        @pl.when(k == 0)
        def _():
            acc[...] = jnp.zeros_like(acc)

        acc[...] += _dot(a_ref[...], b_ref[...].astype(BF16))

        @pl.when(k == n_k - 1)
        def _():
            o_ref[...] = jnp.transpose(acc[...]).astype(out_dtype)

    return pl.pallas_call(
        body, name=name, grid=(n // tn, n_k),
        in_specs=[pl.BlockSpec((m, tk), lambda j, k: (0, k)),
                  pl.BlockSpec((tk, tn), lambda j, k: (k, j))],
        out_specs=pl.BlockSpec((tn, m), lambda j, k: (j, 0)),
        out_shape=jax.ShapeDtypeStruct((n, m), out_dtype),
        scratch_shapes=[pltpu.VMEM((m, tn), F32)],
        compiler_params=_params(("parallel", "arbitrary"), 48),
    )(a_t, b)


def _dgrad_in(dpc, dpg, dpt, dlr, w_main_t, w_lr_t, h, g_pre, dres, token):
    t_rows = h.shape[0]
    tm = _pick_tile(t_rows, 256, 16)
    n_main = w_main_t.shape[0]

    def body(dpc_ref, dpg_ref, dpt_ref, dlr_ref, w_hbm, wlr_ref, h_ref, g_ref, dres_ref, token_ref,
             dh_ref, dg_ref, w_s, w_sem):
        @pl.when(pl.program_id(0) == 0)
        def _():
            cp = pltpu.make_async_copy(w_hbm, w_s, w_sem)
            cp.start()
            cp.wait()
            dg_ref[...] = jnp.zeros_like(dg_ref)

        du = _dot(dlr_ref[...].astype(BF16), wlr_ref[...])
        du += _dot(dpc_ref[...], w_s[0:W_CONV, :])
        du += _dot(dpg_ref[...], w_s[W_CONV:W_CONV + W_GLA, :])
        du += _dot(dpt_ref[...], w_s[W_CONV + W_GLA:n_main, :])
        hh = h_ref[...]
        rstd = lax.rsqrt(jnp.mean(hh * hh, axis=-1, keepdims=True) + EPS)
        xhat = hh * rstd
        dg_ref[...] += jnp.sum(du * xhat, axis=0, keepdims=True)
        dx = du * g_ref[...]
        dh_ref[...] = rstd * (dx - xhat * jnp.mean(dx * xhat, axis=-1, keepdims=True)) + dres_ref[...]

    row = lambda width: pl.BlockSpec((tm, width), lambda i: (i, 0))
    return pl.pallas_call(
        body, name="dgrad_in", grid=(t_rows // tm,),
        in_specs=[row(W_CONV), row(W_GLA), row(W_TAIL), row(LANES),
                  pl.BlockSpec(memory_space=pl.ANY),
                  pl.BlockSpec((LANES, D), lambda i: (0, 0)),
                  row(D), pl.BlockSpec((1, D), lambda i: (0, 0)), row(D),
                  pl.BlockSpec((8, LANES), lambda i: (0, 0))],
        out_specs=[row(D), pl.BlockSpec((1, D), lambda i: (0, 0))],
        out_shape=[jax.ShapeDtypeStruct((t_rows, D), F32), jax.ShapeDtypeStruct((1, D), F32)],
        scratch_shapes=[pltpu.VMEM((n_main, D), BF16), pltpu.SemaphoreType.DMA],
        compiler_params=_params(("arbitrary",), 56),
    )(dpc, dpg, dpt, dlr, w_main_t, w_lr_t, h, g_pre, dres, token)


OFF_CB, OFF_CC, OFF_CX, OFF_CZ = 0, 1024, 2048, 3072
OFF_Q, OFF_K, OFF_V, OFF_R = 4096, 4608, 5120, 6144
OFF_LR, OFF_MA, OFF_MB = 7168, 7200, 8224


def _main_rows(w_t):
    conv = w_t[0:W_CONV].reshape(4, N_CONV_TILES, 128, D).transpose(1, 0, 2, 3).reshape(W_CONV, D)
    q = w_t[OFF_Q:OFF_K].reshape(N_HEADS, HEAD_K, D)
    k = w_t[OFF_K:OFF_V].reshape(N_HEADS, HEAD_K, D)
    v = w_t[OFF_V:OFF_R].reshape(N_HEADS, HEAD_V, D)
    gla = jnp.concatenate([q, k, v], axis=1).reshape(W_GLA, D)
    return jnp.concatenate([conv, gla, w_t[OFF_R:OFF_R + D], w_t[OFF_MA:OFF_MA + 2 * D]], axis=0)


def _reference_rows(g_conv, g_gla, g_tail, g_lr):
    conv = g_conv.reshape(N_CONV_TILES, 4, 128, D).transpose(1, 0, 2, 3).reshape(W_CONV, D)
    gla = g_gla.reshape(N_HEADS, 512, D)
    q = gla[:, 0:128].reshape(N_HEADS * HEAD_K, D)
    k = gla[:, 128:256].reshape(N_HEADS * HEAD_K, D)
    v = gla[:, 256:512].reshape(N_HEADS * HEAD_V, D)
    return jnp.concatenate([conv, q, k, v, g_tail[0:D], g_lr[0:2 * RANK], g_tail[D:3 * D]], axis=0)


def _pack(arrs, rows):
    flat = jnp.concatenate([a.reshape(-1) for a in arrs])
    return jnp.pad(flat, (0, rows * LANES - flat.shape[0])).reshape(rows, LANES)


def _unpack(packed, shapes):
    flat = packed.reshape(-1)
    out, pos = [], 0
    for s in shapes:
        size = 1
        for d in s:
            size *= d
        out.append(flat[pos:pos + size].reshape(s))
        pos += size
    return out


def _rows_for(shapes, mult=8):
    total = 0
    for s in shapes:
        size = 1
        for d in s:
            size *= d
        total += size
    return -(-total // (mult * LANES)) * mult


def kernel(x, meta_tokens, norm_pre, w_in, conv_w, w_gate_fwd, b_gate_fwd, w_gate_bwd, b_gate_bwd, gla_norm, w_out_conv, w_out_gla, w_merge_out, norm_post, loss_target, m_meta_tokens, m_norm_pre, m_w_in, m_conv_w, m_w_gate_fwd, m_b_gate_fwd, m_w_gate_bwd, m_b_gate_bwd, m_gla_norm, m_w_out_conv, m_w_out_gla, m_w_merge_out, m_norm_post, v_meta_tokens, v_norm_pre, v_w_in, v_conv_w, v_w_gate_fwd, v_b_gate_fwd, v_w_gate_bwd, v_b_gate_bwd, v_gla_norm, v_w_out_conv, v_w_out_gla, v_w_merge_out, v_norm_post):
    n_seq, seq, _ = x.shape
    lf = CHUNK + seq
    t_rows = n_seq * lf
    shard = 2 * lax.axis_index("x") + lax.axis_index("y")

    w_in_bf = _cast_bf16(jnp.transpose(w_in[0]), "cast_w_in")
    w_out_bf = _cast_bf16(jnp.concatenate([w_out_conv[0], w_out_gla[0], w_merge_out[0]], axis=0), "cast_w_out")
    small_shapes = [(N_META, D // 4), (3, D // 4), (RANK, HEAD_K), (RANK, HEAD_K)]
    small = _pack([meta_tokens, conv_w[0], w_gate_fwd[0], w_gate_bwd[0]], _rows_for(small_shapes, 16))
    w_in_all, small_all = _gather_via_sibling("gather_w_in", [w_in_bf, small])
    w_out_state, _ = _plane_start("gather_w_out_start", [w_out_bf], "gather", small_all)

    w_full_t = w_in_all.reshape(N_IN, D)
    w_main = _main_rows(w_full_t)
    w_lr = jnp.pad(w_full_t[OFF_LR:OFF_LR + 2 * RANK], ((0, LANES - 2 * RANK), (0, 0)))
    smalls = [_unpack(small_all[s], small_shapes) for s in range(4)]
    meta_full = jnp.concatenate([smalls[s][0] for s in range(4)], axis=1)
    conv_full = jnp.concatenate([smalls[s][1] for s in range(4)], axis=1)
    wgf = jnp.stack([jnp.pad(smalls[s][2], ((0, LANES - RANK), (0, 0))) for s in range(4)]).astype(BF16)
    wgb = jnp.stack([jnp.pad(smalls[s][3], ((RANK, LANES - 2 * RANK), (0, 0))) for s in range(4)]).astype(BF16)
    bgf = b_gate_fwd.reshape(N_HEADS, 1, HEAD_K)
    bgb = b_gate_bwd.reshape(N_HEADS, 1, HEAD_K)

    head = jnp.concatenate([jnp.zeros((PAD_FRONT, D), F32), meta_full], axis=0)
    h = jnp.concatenate([jnp.broadcast_to(head[None], (n_seq, CHUNK, D)), x], axis=1).reshape(t_rows, D)
    tgt = jnp.pad(loss_target, ((0, 0), (CHUNK, 0), (0, 0))).reshape(t_rows, D)

    proj, u_t, lr = _in_proj(h, norm_pre, w_main, w_lr)
    yc = _conv_fwd(proj, conv_full, n_seq, lf)
    o = _gla_fwd(proj, lr, wgf, wgb, bgf, bgb, n_seq, lf)
    (w_out_landed,) = _plane_wait("gather_w_out_wait", w_out_state, "gather", o)
    slot_ids = lax.broadcasted_iota(jnp.int32, (4, 1, 1), 0)
    w_out_all = jnp.where(slot_ids == shard, w_out_bf[None], w_out_landed)
    w3 = jnp.transpose(w_out_all.reshape(4, 3, D // 4, D), (1, 0, 2, 3)).reshape(3, D, D)
    (dres, yg, merged, dout, dpc_out, dpg_out, dyc, d_o, dtail, loss_acc, d_gpost, d_gamma) = _tail(
        h, tgt, yc, o, proj, w3, gla_norm, norm_post, lf)
    g_w_oc = _wgrad(yc, dpc_out, "wgrad_out_conv")
    g_w_og = _wgrad(yg, dpg_out, "wgrad_out_gla")
    g_w_mo = _wgrad(merged, dout, "wgrad_merge_out")
    g_out_slots = jnp.concatenate([g.reshape(4, D // 4, D) for g in (g_w_oc, g_w_og, g_w_mo)], axis=1)
    out_state, out_token = _plane_start("scatter_out_grads_start", [g_out_slots], "scatter", g_w_mo)
    dgla, dlr, dwgf_p, dwgb_p, dbg_p = _gla_bwd(proj, lr, d_o, wgf, wgb, bgf, bgb, n_seq, lf, out_token)
    (got_out,) = _plane_wait("scatter_out_grads_wait", out_state, "scatter", dlr)
    dconv, dconvw_p = _conv_bwd(proj, conv_full, dyc, n_seq, lf)
    g_conv = _wgrad_t(u_t, dconv, "wgrad_in_conv")
    g_gla = _wgrad_t(u_t, dgla, "wgrad_in_gla")
    g_tail = _wgrad_t(u_t, dtail, "wgrad_in_tail")
    g_lr = _wgrad_t(u_t, dlr, "wgrad_in_lr")

    g_in_slots = _reference_rows(g_conv, g_gla, g_tail, g_lr).reshape(4, SHARD_IN, D)
    in_state, in_token = _plane_start("scatter_in_grads_start", [g_in_slots], "scatter", g_lr)
    dh, d_gpre = _dgrad_in(dconv, dgla, dtail, dlr, w_main, w_lr, h, norm_pre, dres, in_token)
    (got_in,) = _plane_wait("scatter_in_grads_wait", in_state, "scatter", d_gpre)

    dh3 = dh.reshape(n_seq, lf, D)
    grad_x = dh3[:, CHUNK:, :]

    d_meta = jnp.sum(dh3[:, PAD_FRONT:CHUNK, :], axis=0)
    d_convw = jnp.sum(dconvw_p, axis=0)
    d_wgf = jnp.transpose(jnp.sum(dwgf_p, axis=0)[:, 0:RANK, :], (1, 0, 2)).reshape(RANK, N_HEADS * HEAD_K)
    d_wgb = jnp.transpose(jnp.sum(dwgb_p, axis=0)[:, RANK:2 * RANK, :], (1, 0, 2)).reshape(RANK, N_HEADS * HEAD_K)
    d_bg = jnp.sum(dbg_p, axis=0)
    d_bgf = d_bg[:, 0, :].reshape(1, N_HEADS * HEAD_K)
    d_bgb = d_bg[:, 1, :].reshape(1, N_HEADS * HEAD_K)
    part_shapes = [(N_META, D), (3, D), (RANK, 512), (RANK, 512), (1, D), (1, 512), (1, 512), (1, HEAD_V),
                   (1, D), (1, LANES)]
    parts = _pack([d_meta, d_convw, d_wgf, d_wgb, d_gpre, d_bgf, d_bgb, d_gamma, d_gpost, loss_acc[0:1, :]],
                  _rows_for(part_shapes))
    (parts_all,) = _exchange("gather_small_grads", [parts], ALL_FLIPS, (4, 2, 1), "gather")
    (g_meta, g_convw, g_wgf, g_wgb, g_npre, g_bgf, g_bgb, g_gnorm, g_npost, loss_row) = _unpack(
        _sum_slots(parts_all, "sum_small_grads"), part_shapes)
    loss = loss_row[0, 0]

    def col_shard(a, width):
        return lax.dynamic_slice_in_dim(a, shard * width, width, axis=a.ndim - 1)

    upd_shapes = [(N_META, D // 4), (3, D // 4), (RANK, HEAD_K), (RANK, HEAD_K), (1, D), (1, 512), (1, 512),
                  (1, HEAD_V), (1, D)]
    upd_rows = _rows_for(upd_shapes)
    small_w = _pack([meta_tokens, conv_w[0], w_gate_fwd[0], w_gate_bwd[0], norm_pre, b_gate_fwd, b_gate_bwd,
                     gla_norm, norm_post], upd_rows)
    small_g = _pack([col_shard(g_meta, D // 4), col_shard(g_convw, D // 4), col_shard(g_wgf, HEAD_K),
                     col_shard(g_wgb, HEAD_K), g_npre, g_bgf, g_bgb, g_gnorm, g_npost], upd_rows)
    small_m = _pack([m_meta_tokens, m_conv_w[0], m_w_gate_fwd[0], m_w_gate_bwd[0], m_norm_pre, m_b_gate_fwd,
                     m_b_gate_bwd, m_gla_norm, m_norm_post], upd_rows)
    small_v = _pack([v_meta_tokens, v_conv_w[0], v_w_gate_fwd[0], v_w_gate_bwd[0], v_norm_pre, v_b_gate_fwd,
                     v_b_gate_bwd, v_gla_norm, v_norm_post], upd_rows)
    small_out = [_unpack(a, upd_shapes) for a in _adamw(small_w, [small_g], small_m, small_v, "adamw_small")]

    plane_in = _sum_slots(got_in, "sum_w_in_grads", own=g_in_slots)
    plane_out = _sum_slots(got_out, "sum_w_out_grads", own=g_out_slots)
    other_in, other_out = _exchange("swap_plane_sums", [plane_in, plane_out], SIBLING_FLIPS, (0, 0, 0), "swap")
    big_in = _adamw(jnp.transpose(w_in[0]), [plane_in, other_in], jnp.transpose(m_w_in[0]), jnp.transpose(v_w_in[0]),
                    "adamw_w_in")
    out_params = ((w_out_conv, m_w_out_conv, v_w_out_conv), (w_out_gla, m_w_out_gla, v_w_out_gla),
                  (w_merge_out, m_w_merge_out, v_w_merge_out))
    big_out = [_adamw(w[0], [plane_out, other_out], m[0], v[0], f"adamw_w_out_{i}", grad_row=i * (D // 4))
               for i, (w, m, v) in enumerate(out_params)]

    results = []
    for kind in range(4):
        sm = small_out[kind]
        w_in_part = jnp.transpose(big_in[kind])[None]
        outs3 = [big_out[i][kind][None] for i in range(3)]
        results.extend([
            sm[0], sm[4], w_in_part, sm[1][None], sm[2][None], sm[5], sm[3][None], sm[6], sm[7],
            outs3[0], outs3[1], outs3[2], sm[8]])
    return (loss, grad_x, *results)
```

```python
import functools

import jax
import jax.numpy as jnp
from jax import lax
from jax.experimental import pallas as pl
from jax.experimental.pallas import tpu as pltpu

F32 = jnp.float32
BF16 = jnp.bfloat16
MESH = pl.DeviceIdType.MESH

D = 1024
N_META = 16
CHUNK = 64
PAD_FRONT = CHUNK - N_META
N_HEADS = 4
HEAD_K = 128
HEAD_V = 256
RANK = 16
EPS = 1e-6
GATE_NORM = 16.0
N_IN = 9248
SHARD_IN = N_IN // 4
LANES = 128
N_CONV_TILES = 8
W_CONV = 4096
W_GLA = 2048
W_TAIL = 3072
N_MAIN = W_CONV + W_GLA + W_TAIL
OFF_Q, OFF_K, OFF_V, OFF_R = 4096, 4608, 5120, 6144
OFF_LR, OFF_MA, OFF_MB = 7168, 7200, 8224
MIB = 1024 * 1024

ADAM_LR = 0.001
ADAM_B1 = 0.9
ADAM_B2 = 0.999
ADAM_EPS = 1e-08
ADAM_WD = 0.01
ADAM_STEP = 10


def _params(sem=None, vmem_mib=None):
    return pltpu.CompilerParams(
        dimension_semantics=sem,
        vmem_limit_bytes=None if vmem_mib is None else vmem_mib * MIB)


def _pick_tile(n, target, mult):
    best = None
    for t in range(mult, min(n, target) + 1, mult):
        if n % t == 0:
            best = t
    return n if best is None else best


def _sigmoid(v):
    return 1.0 / (1.0 + jnp.exp(-v))


def _log_sigmoid(v):
    return jnp.minimum(v, 0.0) - jnp.log(1.0 + jnp.exp(-jnp.abs(v)))


def _dot(a, b):
    return jnp.dot(a, b, preferred_element_type=F32)


def _dot_nt(a, b):
    return lax.dot_general(a, b, (((1,), (1,)), ((), ())), preferred_element_type=F32)


def _dot_tn(a, b):
    return lax.dot_general(a, b, (((0,), (0,)), ((), ())), preferred_element_type=F32)


def _tri_dot(tri, v):
    hi = v.astype(BF16)
    lo = (v - hi.astype(F32)).astype(BF16)
    return _dot(tri, hi) + _dot(tri, lo)


PLANE_FLIPS = ((1, 0, 0), (0, 1, 0), (1, 1, 0))
ALL_FLIPS = tuple((m >> 2 & 1, m >> 1 & 1, m & 1) for m in range(1, 8))
SIBLING_FLIPS = ((0, 0, 1),)


def _exchange(name, arrs, flips, slot_weights, mode):
    n = len(arrs)
    n_slots = 1
    for w in slot_weights:
        n_slots += w
    if mode == "gather":
        out_shape = [jax.ShapeDtypeStruct((n_slots,) + a.shape, a.dtype) for a in arrs]
    else:
        out_shape = [jax.ShapeDtypeStruct(a.shape, a.dtype) for a in arrs]

    def body(*refs):
        ins, outs = refs[:n], refs[n:2 * n]
        send_sems, recv_sems, local_sems = refs[2 * n:]
        pos = (lax.axis_index("x"), lax.axis_index("y"), lax.axis_index("c"))

        def slot_of(p):
            return p[0] * slot_weights[0] + p[1] * slot_weights[1] + p[2] * slot_weights[2]

        peers = [tuple(1 - pos[a] if f[a] else pos[a] for a in range(3)) for f in flips]
        me = slot_of(pos)
        local = []
        sends = []
        for i in range(n):
            if mode != "swap":
                src = ins[i] if mode == "gather" else ins[i].at[me]
                cp = pltpu.make_async_copy(src, outs[i].at[me], local_sems.at[i])
                cp.start()
                local.append(cp)
            for k, peer in enumerate(peers):
                if mode == "gather":
                    src, dst = ins[i], outs[i].at[me]
                elif mode == "scatter":
                    src, dst = ins[i].at[slot_of(peer)], outs[i].at[me]
                else:
                    src, dst = ins[i], outs[i]
                cp = pltpu.make_async_remote_copy(
                    src_ref=src, dst_ref=dst, send_sem=send_sems.at[i, k], recv_sem=recv_sems.at[i, k],
                    device_id=peer, device_id_type=MESH)
                cp.start()
                sends.append(cp)
        for i in range(n):
            for k, peer in enumerate(peers):
                if mode == "gather":
                    src, dst = ins[i], outs[i].at[slot_of(peer)]
                elif mode == "scatter":
                    src, dst = ins[i].at[me], outs[i].at[slot_of(peer)]
                else:
                    src, dst = ins[i], outs[i]
                arrival = pltpu.make_async_remote_copy(
                    src_ref=src, dst_ref=dst, send_sem=send_sems.at[i, k], recv_sem=recv_sems.at[i, k],
                    device_id=peer, device_id_type=MESH)
                arrival.wait_recv()
        for cp in sends:
            cp.wait_send()
        for cp in local:
            cp.wait()

    hbm = pl.BlockSpec(memory_space=pl.ANY)
    outs = pl.pallas_call(
        body, name=name, out_shape=out_shape,
        in_specs=[hbm] * n, out_specs=[hbm] * n,
        scratch_shapes=[pltpu.SemaphoreType.DMA((n, len(flips))),
                        pltpu.SemaphoreType.DMA((n, len(flips))),
                        pltpu.SemaphoreType.DMA((n,))],
        compiler_params=pltpu.CompilerParams(has_side_effects=True),
    )(*arrs)
    return list(outs)


def _gather_via_sibling(name, arrs, tokens, target):
    n = len(arrs)
    n_seq, seq, _ = tokens.shape
    frame = jax.ShapeDtypeStruct((n_seq, CHUNK + seq, D), F32)
    out_shape = [jax.ShapeDtypeStruct((4,) + a.shape, a.dtype) for a in arrs] + [frame, frame]
    n_frame_copies = 4 * n_seq + 4 * n_seq

    def body(*refs):
        ins, outs = refs[:n], refs[n + 3:2 * n + 3]
        tok_ref, tgt_ref, zero_ref = refs[n:n + 3]
        h_ref, tf_ref = refs[2 * n + 3:2 * n + 5]
        send_sems, recv_sems, local_sems, frame_sems = refs[2 * n + 5:]
        x, y, c = lax.axis_index("x"), lax.axis_index("y"), lax.axis_index("c")
        frame_copies = []
        for b in range(n_seq):
            frame_copies += [
                (tok_ref.at[b], h_ref.at[b, pl.ds(CHUNK, seq)]),
                (zero_ref.at[pl.ds(0, PAD_FRONT)], h_ref.at[b, pl.ds(0, PAD_FRONT)]),
                (tgt_ref.at[b], tf_ref.at[b, pl.ds(CHUNK, seq)]),
                (zero_ref, tf_ref.at[b, pl.ds(0, CHUNK)])]
        frame_copies = [pltpu.make_async_copy(src, dst, frame_sems.at[i]) for i, (src, dst) in enumerate(frame_copies)]
        for cp in frame_copies:
            cp.start()
        me = 2 * x + y
        chips = [(1 - x, y), (x, 1 - y), (1 - x, 1 - y)]

        def half(ref, which):
            rows = ref.shape[0]
            cut = rows // 2 // 16 * 16
            return ref.at[pl.ds(0, cut)] if which == 0 else ref.at[pl.ds(cut, rows - cut)]

        def copy(src, dst, i, k, to):
            return pltpu.make_async_remote_copy(
                src_ref=src, dst_ref=dst, send_sem=send_sems.at[i, k], recv_sem=recv_sems.at[i, k],
                device_id=to, device_id_type=MESH)

        def run(mine):
            other = 1 - mine
            local, sends = [], []
            for i in range(n):
                cp = pltpu.make_async_copy(ins[i], outs[i].at[me], local_sems.at[i])
                cp.start()
                local.append(cp)
                for k, (px, py) in enumerate(chips):
                    cp = copy(half(ins[i], mine), half(outs[i].at[me], mine), i, k, (px, py, mine))
                    cp.start()
                    sends.append(cp)
            for k, (px, py) in enumerate(chips):
                slot = 2 * px + py
                for i in range(n):
                    landed = half(outs[i].at[slot], mine)
                    copy(half(ins[i], mine), landed, i, k, (px, py, mine)).wait_recv()
                    cp = copy(landed, landed, i, 3 + k, (x, y, other))
                    cp.start()
                    sends.append(cp)
            for k, (px, py) in enumerate(chips):
                slot = 2 * px + py
                for i in range(n):
                    passed = half(outs[i].at[slot], other)
                    copy(passed, passed, i, 3 + k, (x, y, other)).wait_recv()
            for cp in sends:
                cp.wait_send()
            for cp in local:
                cp.wait()

        for mine in (0, 1):
            pl.when(c == mine)(functools.partial(run, mine))

        width = D // 4
        meta_copies = [
            pltpu.make_async_copy(outs[1].at[s, pl.ds(0, N_META)],
                                  h_ref.at[b, pl.ds(PAD_FRONT, N_META), pl.ds(s * width, width)],
                                  frame_sems.at[4 * n_seq + 4 * b + s])
            for b in range(n_seq) for s in range(4)]
        for cp in meta_copies:
            cp.start()
        for cp in meta_copies + frame_copies:
            cp.wait()

    hbm = pl.BlockSpec(memory_space=pl.ANY)
    outs = pl.pallas_call(
        body, name=name, out_shape=out_shape,
        in_specs=[hbm] * (n + 3), out_specs=[hbm] * (n + 2),
        scratch_shapes=[pltpu.SemaphoreType.DMA((n, 6)), pltpu.SemaphoreType.DMA((n, 6)),
                        pltpu.SemaphoreType.DMA((n,)), pltpu.SemaphoreType.DMA((n_frame_copies,))],
        compiler_params=pltpu.CompilerParams(has_side_effects=True),
    )(*arrs, tokens, target, jnp.zeros((CHUNK, D), F32))
    return list(outs)


HBM_SPEC = pl.BlockSpec(memory_space=pltpu.HBM)
SEM_SPEC = pl.BlockSpec(memory_space=pltpu.SEMAPHORE)
DATAFLOW = pltpu.SideEffectType.DATAFLOW_SIDE_EFFECTING


def _plane_peers():
    x, y, c = lax.axis_index("x"), lax.axis_index("y"), lax.axis_index("c")
    return 2 * x + y, [((1 - x, y, c), 2 * (1 - x) + y), ((x, 1 - y, c), 2 * x + 1 - y),
                       ((1 - x, 1 - y, c), 2 * (1 - x) + 1 - y)]


def _plane_start(name, arrs, mode, after):
    n = len(arrs)
    lands = [lax.empty(((4,) + a.shape) if mode == "gather" else a.shape, a.dtype) for a in arrs]

    def body(*refs):
        srcs, landing = refs[:n], refs[n:2 * n]
        send_sems, recv_sems = refs[2 * n + 1], refs[2 * n + 2]
        token = refs[-1]
        me, peers = _plane_peers()
        for i in range(n):
            for k, (peer, peer_slot) in enumerate(peers):
                src = srcs[i] if mode == "gather" else srcs[i].at[peer_slot]
                pltpu.make_async_remote_copy(
                    src_ref=src, dst_ref=landing[i].at[me], send_sem=send_sems.at[3 * i + k],
                    recv_sem=recv_sems.at[3 * i + k], device_id=peer, device_id_type=MESH).start()
        token[...] = jnp.zeros_like(token)

    hbm_in = [pltpu.with_memory_space_constraint(a, pltpu.HBM) for a in list(arrs) + lands]
    out = pl.pallas_call(
        body, name=name,
        out_shape=[pltpu.SemaphoreType.DMA((3 * n,)), pltpu.SemaphoreType.DMA((3 * n,))]
                  + [pltpu.HBM(a.shape, a.dtype) for a in hbm_in]
                  + [jax.ShapeDtypeStruct((8, LANES), F32)],
        in_specs=[HBM_SPEC] * (2 * n) + [pl.BlockSpec(memory_space=pl.ANY)],
        out_specs=[SEM_SPEC, SEM_SPEC] + [HBM_SPEC] * (2 * n) + [pl.BlockSpec(memory_space=pltpu.VMEM)],
        input_output_aliases={i: 2 + i for i in range(2 * n)},
        compiler_params=pltpu.CompilerParams(has_side_effects=DATAFLOW),
    )(*hbm_in, after)
    return out[:-1], out[-1]


def _plane_wait(name, state, mode, after):
    send_sems, recv_sems = state[0], state[1]
    bufs = list(state[2:])
    n = len(bufs) // 2

    def body(*refs):
        srcs, landing = refs[:n], refs[n:2 * n]
        send_sems, recv_sems = refs[2 * n], refs[2 * n + 1]
        me, peers = _plane_peers()
        for i in range(n):
            for k, (peer, peer_slot) in enumerate(peers):
                src = srcs[i] if mode == "gather" else srcs[i].at[peer_slot]
                cp = pltpu.make_async_remote_copy(
                    src_ref=src, dst_ref=landing[i].at[peer_slot], send_sem=send_sems.at[3 * i + k],
                    recv_sem=recv_sems.at[3 * i + k], device_id=peer, device_id_type=MESH)
                cp.wait_send()
                cp.wait_recv()

    out = pl.pallas_call(
        body, name=name,
        out_shape=[pltpu.HBM(a.shape, a.dtype) for a in bufs],
        in_specs=[HBM_SPEC] * (2 * n) + [SEM_SPEC, SEM_SPEC, pl.BlockSpec(memory_space=pl.ANY)],
        out_specs=[HBM_SPEC] * (2 * n),
        input_output_aliases={i: i for i in range(2 * n)},
        compiler_params=pltpu.CompilerParams(has_side_effects=DATAFLOW),
    )(*bufs, send_sems, recv_sems, after)
    return list(out[n:])


def _tile_2d(rows, cols, row_mult, max_elems=512 * 1024):
    if rows % row_mult == 0:
        rt = _pick_tile(rows, max(row_mult, max_elems // cols), row_mult)
        return (rt, cols), rows // rt, lambda i: (i, 0)
    ct = _pick_tile(cols, max(LANES, max_elems // rows), LANES)
    return (rows, ct), cols // ct, lambda i: (0, i)


def _cast_bf16(a, name):
    block, steps, index = _tile_2d(a.shape[0], a.shape[1], 16)

    def body(a_ref, o_ref):
        o_ref[...] = a_ref[...].astype(BF16)

    return pl.pallas_call(
        body, name=name, grid=(steps,),
        in_specs=[pl.BlockSpec(block, index)],
        out_specs=pl.BlockSpec(block, index),
        out_shape=jax.ShapeDtypeStruct(a.shape, BF16),
        compiler_params=_params(("parallel",)),
    )(a)


def _sum_slots(buf, name, own=None):
    n_slots, rows, cols = buf.shape
    (br, bc), steps, index = _tile_2d(rows, cols, 16, 320 * 1024)
    n_in = 1 if own is None else 2

    def body(*refs):
        b_ref, o_ref = refs[0], refs[-1]
        me = None if own is None else 2 * lax.axis_index("x") + lax.axis_index("y")
        acc = None
        for s in range(n_slots):
            term = b_ref[s] if own is None else jnp.where(me == s, refs[1][s], b_ref[s])
            acc = term.astype(F32) if acc is None else acc + term.astype(F32)
        o_ref[...] = acc

    return pl.pallas_call(
        body, name=name, grid=(steps,),
        in_specs=[pl.BlockSpec((n_slots, br, bc), lambda i: (0,) + index(i))] * n_in,
        out_specs=pl.BlockSpec((br, bc), index),
        out_shape=jax.ShapeDtypeStruct((rows, cols), F32),
        compiler_params=_params(("parallel",), 48),
    )(*([buf] if own is None else [buf, own]))


def _adamw(w, grads, m, v, name, grad_row=0):
    rows, cols = w.shape
    (rt, _), _, _ = _tile_2d(rows, cols, 8, 160 * 1024)
    assert grad_row % rt == 0
    n_g = len(grads)
    c1 = 1.0 - ADAM_B1 ** ADAM_STEP
    c2 = 1.0 - ADAM_B2 ** ADAM_STEP

    def body(*refs):
        w_ref = refs[0]
        g_refs = refs[1:1 + n_g]
        m_ref, v_ref, g_out, d_out, m_out, v_out = refs[1 + n_g:]
        g = g_refs[0][...]
        for r in g_refs[1:]:
            g = g + r[...]
        m_new = ADAM_B1 * m_ref[...] + (1.0 - ADAM_B1) * g
        v_new = ADAM_B2 * v_ref[...] + (1.0 - ADAM_B2) * (g * g)
        m_hat = m_new / c1
        v_hat = v_new / c2
        g_out[...] = g
        d_out[...] = -ADAM_LR * (m_hat / (jnp.sqrt(v_hat) + ADAM_EPS) + ADAM_WD * w_ref[...])
        m_out[...] = m_new
        v_out[...] = v_new

    spec = pl.BlockSpec((rt, cols), lambda i: (i, 0))
    grad_spec = pl.BlockSpec((rt, cols), lambda i: (i + grad_row // rt, 0))
    shape = jax.ShapeDtypeStruct((rows, cols), F32)
    return pl.pallas_call(
        body, name=name, grid=(rows // rt,),
        in_specs=[spec] + [grad_spec] * n_g + [spec] * 2, out_specs=[spec] * 4, out_shape=[shape] * 4,
        compiler_params=_params(("parallel",), 48),
    )(w, *grads, m, v)


def _weight_pieces():
    pieces = []
    for j in range(N_CONV_TILES):
        for g in range(4):
            pieces.append((512 * j + 128 * g, D * g + 128 * j, 128))
    for hd in range(N_HEADS):
        base = W_CONV + 512 * hd
        pieces.append((base, OFF_Q + HEAD_K * hd, HEAD_K))
        pieces.append((base + HEAD_K, OFF_K + HEAD_K * hd, HEAD_K))
        pieces.append((base + 2 * HEAD_K, OFF_V + HEAD_V * hd, HEAD_V))
    pieces.append((W_CONV + W_GLA, OFF_R, D))
    pieces.append((W_CONV + W_GLA + D, OFF_MA, 2 * D))
    return pieces


N_WEIGHT_COPIES = len(_weight_pieces()) + 1


def _load_weights(w_hbm, w_s, wlr_s, sems):
    copies = [pltpu.make_async_copy(w_hbm.at[pl.ds(src, n)], w_s.at[pl.ds(dst, n)], sems.at[i])
              for i, (dst, src, n) in enumerate(_weight_pieces())]
    copies.append(pltpu.make_async_copy(w_hbm.at[pl.ds(OFF_LR, LANES)], wlr_s, sems.at[N_WEIGHT_COPIES - 1]))
    for cp in copies:
        cp.start()
    for cp in copies:
        cp.wait()


def _in_proj(h, g_pre, w_full_t):
    t_rows = h.shape[0]
    tm = _pick_tile(t_rows, 384, LANES)
    n_main = N_MAIN

    def body(h_ref, g_ref, w_hbm, proj_ref, ut_ref, lr_ref, w_s, wlr_s, w_sems):
        @pl.when(pl.program_id(0) == 0)
        def _():
            _load_weights(w_hbm, w_s, wlr_s, w_sems)

        hh = h_ref[...]
        rstd = lax.rsqrt(jnp.mean(hh * hh, axis=-1, keepdims=True) + EPS)
        uf = hh * rstd * g_ref[...]
        u = uf.astype(BF16)
        ut_ref[...] = jnp.transpose(uf).astype(BF16)
        lr_ref[...] = _dot_nt(u, wlr_s[...])
        for j in range(n_main // D):
            cols = slice(j * D, (j + 1) * D)
            proj_ref[:, cols] = _dot_nt(u, w_s[cols, :]).astype(BF16)

    return pl.pallas_call(
        body, name="in_proj", grid=(t_rows // tm,),
        in_specs=[pl.BlockSpec((tm, D), lambda i: (i, 0)),
                  pl.BlockSpec((1, D), lambda i: (0, 0)),
                  pl.BlockSpec(memory_space=pl.ANY)],
        out_specs=[pl.BlockSpec((tm, n_main), lambda i: (i, 0)),
                   pl.BlockSpec((D, tm), lambda i: (0, i)),
                   pl.BlockSpec((tm, LANES), lambda i: (i, 0))],
        out_shape=[jax.ShapeDtypeStruct((t_rows, n_main), BF16),
                   jax.ShapeDtypeStruct((D, t_rows), BF16),
                   jax.ShapeDtypeStruct((t_rows, LANES), F32)],
        scratch_shapes=[pltpu.VMEM((n_main, D), BF16), pltpu.VMEM((LANES, D), BF16),
                        pltpu.SemaphoreType.DMA((N_WEIGHT_COPIES,))],
        compiler_params=_params(("arbitrary",), 56),
    )(h, g_pre, w_full_t)


def _conv_parts(p_ref, w_ref):
    cb = p_ref[:, 0:128].astype(F32)
    cc = p_ref[:, 128:256].astype(F32)
    cx = p_ref[:, 256:384].astype(F32)
    cz = p_ref[:, 384:512].astype(F32)
    rows = cb.shape[0]
    w = w_ref[...]
    p = cc * cx
    conv = pltpu.roll(p, 1, 0) * w[0:1] + p * w[1:2] + pltpu.roll(p, rows - 1, 0) * w[2:3]
    sz = _sigmoid(cz)
    return cb, cc, cx, cz, p, conv, sz, w


def _conv_fwd(proj, conv_w, n_seq, lf):
    def body(p_ref, w_ref, y_ref):
        cb, _, _, cz, _, conv, sz, _ = _conv_parts(p_ref, w_ref)
        y_ref[...] = (cb * conv * (cz * sz)).astype(BF16)

    return pl.pallas_call(
        body, name="conv_fwd", grid=(n_seq, N_CONV_TILES),
        in_specs=[pl.BlockSpec((lf, 512), lambda b, j: (b, j)),
                  pl.BlockSpec((3, 128), lambda b, j: (0, j))],
        out_specs=pl.BlockSpec((lf, 128), lambda b, j: (b, j)),
        out_shape=jax.ShapeDtypeStruct((n_seq * lf, D), BF16),
        compiler_params=_params(("parallel", "parallel"), 48),
    )(proj, conv_w)


def _conv_bwd(proj, conv_w, dyc, n_seq, lf):
    def body(p_ref, w_ref, dy_ref, dp_ref, dw_ref):
        cb, cc, cx, cz, p, conv, sz, w = _conv_parts(p_ref, w_ref)
        rows = cb.shape[0]
        dy = dy_ref[...].astype(F32)
        silu = cz * sz
        dcb = dy * conv * silu
        dconv = dy * cb * silu
        dcz = dy * cb * conv * (sz * (1.0 + cz * (1.0 - sz)))
        d_next = pltpu.roll(dconv, rows - 1, 0)
        d_prev = pltpu.roll(dconv, 1, 0)
        dp = d_next * w[0:1] + dconv * w[1:2] + d_prev * w[2:3]
        dp_ref[:, 0:128] = dcb.astype(BF16)
        dp_ref[:, 128:256] = (dp * cx).astype(BF16)
        dp_ref[:, 256:384] = (dp * cc).astype(BF16)
        dp_ref[:, 384:512] = dcz.astype(BF16)
        dw_ref[0:1, :] = jnp.sum(dconv * pltpu.roll(p, 1, 0), axis=0, keepdims=True)
        dw_ref[1:2, :] = jnp.sum(dconv * p, axis=0, keepdims=True)
        dw_ref[2:3, :] = jnp.sum(dconv * pltpu.roll(p, rows - 1, 0), axis=0, keepdims=True)

    return pl.pallas_call(
        body, name="conv_bwd", grid=(n_seq, N_CONV_TILES),
        in_specs=[pl.BlockSpec((lf, 512), lambda b, j: (b, j)),
                  pl.BlockSpec((3, 128), lambda b, j: (0, j)),
                  pl.BlockSpec((lf, 128), lambda b, j: (b, j))],
        out_specs=[pl.BlockSpec((lf, 512), lambda b, j: (b, j)),
                   pl.BlockSpec((None, 3, 128), lambda b, j: (b, 0, j))],
        out_shape=[jax.ShapeDtypeStruct((n_seq * lf, W_CONV), BF16),
                   jax.ShapeDtypeStruct((n_seq, 3, D), F32)],
        compiler_params=_params(("parallel", "parallel"), 48),
    )(proj, conv_w, dyc)


GROUP = 3
GROUP_ROWS = GROUP * CHUNK


def _row_group(shape):
    row = lax.broadcasted_iota(jnp.int32, shape, 0)
    grp = jnp.zeros(shape, jnp.int32)
    for r in range(1, GROUP):
        grp = grp + (row >= r * CHUNK).astype(jnp.int32)
    return grp


def _lane_group(shape, width):
    lane = lax.broadcasted_iota(jnp.int32, shape, 1)
    grp = jnp.zeros(shape, jnp.int32)
    for r in range(1, GROUP):
        grp = grp + (lane >= r * width).astype(jnp.int32)
    return grp


def _group_masks(direction):
    shape = (GROUP_ROWS, GROUP_ROWS)
    row = lax.broadcasted_iota(jnp.int32, shape, 0)
    col = lax.broadcasted_iota(jnp.int32, shape, 1)
    same = _row_group(shape) == _lane_group(shape, CHUNK)
    lower = same & (col <= row)
    upper = same & (col >= row)
    if direction == 0:
        return lower.astype(BF16), upper.astype(BF16), lower
    return upper.astype(BF16), lower.astype(BF16), same & (col > row)


def _diag_blocks(v):
    w = v.shape[1]
    wide = jnp.concatenate([v] * GROUP, axis=1)
    return jnp.where(_row_group(wide.shape) == _lane_group(wide.shape, w), wide, jnp.zeros_like(wide))


def _pick_diag(wide):
    w = wide.shape[1] // GROUP
    grp = _row_group((GROUP_ROWS, w))
    out = wide[:, 0:w]
    for r in range(1, GROUP):
        out = jnp.where(grp == r, wide[:, r * w:(r + 1) * w], out)
    return out


def _per_chunk_rows(rows_of_chunk):
    w = rows_of_chunk[0].shape[1]
    return jnp.concatenate([jnp.broadcast_to(v, (CHUNK, w)) for v in rows_of_chunk], axis=0)


def _chunk_end_rows(direction, b):
    at = CHUNK - 1 if direction == 0 else 0
    return [b[r * CHUNK + at:r * CHUNK + at + 1, :] for r in range(GROUP)]


def _gla_gates(lr_bf, wg_ref, bg_ref, lf):
    z = _dot(lr_bf, wg_ref[...]) + bg_ref[...]
    valid = lax.broadcasted_iota(jnp.int32, (lf, HEAD_K), 0) >= PAD_FRONT
    return z, valid


def _group_unroll(n_groups):
    return n_groups if n_groups <= 11 else 1


def _group_rows(g):
    return pl.ds(pl.multiple_of(g * GROUP_ROWS, GROUP_ROWS), GROUP_ROWS)


def _chunk_decay(direction, g, r, b_s):
    base = g * GROUP_ROWS + r * CHUNK
    if direction == 0:
        grp = b_s[pl.ds(pl.multiple_of(base + CHUNK - 8, 8), 8), :]
        return jnp.exp(grp[7:8, :])
    grp = b_s[pl.ds(pl.multiple_of(base, 8), 8), :]
    return jnp.exp(grp[0:1, :])


def _state_scan(direction, n_groups, b_s, st_s, reverse):
    ascending = (direction == 0) != reverse

    def step(i, carry):
        g = i if ascending else n_groups - 1 - i
        for rr in range(GROUP):
            r = rr if ascending else GROUP - 1 - rr
            lanes = slice(r * HEAD_K, (r + 1) * HEAD_K)
            decay = _chunk_decay(direction, g, r, b_s)
            local = st_s[g, :, lanes]
            st_s[g, :, lanes] = carry
            carry = (local + carry * decay) if reverse else (carry * decay + local)
        return carry

    lax.fori_loop(0, n_groups, step, jnp.zeros((HEAD_V, HEAD_K), F32))


def _gla_states(direction, n_groups, qkv_ref, g_s, b_s, st_s, tri):
    def local(g, carry):
        rows = _group_rows(g)
        b = _tri_dot(tri, g_s[rows, :])
        b_s[rows, :] = b
        b_end = _per_chunk_rows(_chunk_end_rows(direction, b))
        k = qkv_ref[rows, 128:256].astype(F32)
        v = qkv_ref[rows, 256:512]
        k_dec = (k * jnp.exp(b_end - b)).astype(BF16)
        st_s[g] = _dot_tn(v, _diag_blocks(k_dec))
        return carry

    lax.fori_loop(0, n_groups, local, 0, unroll=_group_unroll(n_groups))
    _state_scan(direction, n_groups, b_s, st_s, False)


def _gla_fwd(proj, lr, wgf, wgb, bgf, bgb, n_seq, lf):
    assert lf % GROUP_ROWS == 0
    n_groups = lf // GROUP_ROWS
    scale = HEAD_K ** -0.5

    def body(qkv_ref, lr_ref, wgf_ref, wgb_ref, bgf_ref, bgb_ref, o_ref, g_s, b_s, st_s):
        lr_bf = lr_ref[...].astype(BF16)
        for direction in (0, 1):
            wg_ref, bg_ref = ((wgf_ref, bgf_ref), (wgb_ref, bgb_ref))[direction]
            z, valid = _gla_gates(lr_bf, wg_ref, bg_ref, lf)
            g_s[...] = jnp.where(valid, _log_sigmoid(z) / GATE_NORM, 0.0)
            tri, _, smask = _group_masks(direction)
            _gla_states(direction, n_groups, qkv_ref, g_s, b_s, st_s, tri)

            def out(g, carry):
                rows = _group_rows(g)
                b = b_s[rows, :]
                q = qkv_ref[rows, 0:128].astype(F32) * scale
                k = qkv_ref[rows, 128:256].astype(F32)
                v = qkv_ref[rows, 256:512]
                q_in = (q * jnp.exp(b)).astype(BF16)
                k_in = (k * jnp.exp(-b)).astype(BF16)
                s = jnp.where(smask, _dot_nt(q_in, k_in), 0.0).astype(BF16)
                o = _dot(s, v) + _dot_nt(_diag_blocks(q_in), st_s[g].astype(BF16))
                if direction == 0:
                    o_ref[rows, :] = o
                else:
                    o_ref[rows, :] = o_ref[rows, :] + o
                return carry

            lax.fori_loop(0, n_groups, out, 0, unroll=_group_unroll(n_groups))

    return pl.pallas_call(
        body, name="gla_fwd", grid=(n_seq, N_HEADS),
        in_specs=[pl.BlockSpec((lf, 512), lambda b, h: (b, N_CONV_TILES + h)),
                  pl.BlockSpec((lf, LANES), lambda b, h: (b, 0)),
                  pl.BlockSpec((None, LANES, HEAD_K), lambda b, h: (h, 0, 0)),
                  pl.BlockSpec((None, LANES, HEAD_K), lambda b, h: (h, 0, 0)),
                  pl.BlockSpec((None, 1, HEAD_K), lambda b, h: (h, 0, 0)),
                  pl.BlockSpec((None, 1, HEAD_K), lambda b, h: (h, 0, 0))],
        out_specs=pl.BlockSpec((lf, HEAD_V), lambda b, h: (b, h)),
        out_shape=jax.ShapeDtypeStruct((n_seq * lf, D), F32),
        scratch_shapes=[pltpu.VMEM((lf, HEAD_K), F32), pltpu.VMEM((lf, HEAD_K), F32),
                        pltpu.VMEM((n_groups, HEAD_V, GROUP * HEAD_K), F32)],
        compiler_params=_params(("parallel", "parallel"), 48),
    )(proj, lr, wgf, wgb, bgf, bgb)


def _gla_bwd(proj, lr, d_o, wgf, wgb, bgf, bgb, n_seq, lf, token):
    assert lf % GROUP_ROWS == 0
    n_groups = lf // GROUP_ROWS
    scale = HEAD_K ** -0.5

    def body(qkv_ref, lr_ref, do_ref, wgf_ref, wgb_ref, bgf_ref, bgb_ref, token_ref,
             dqkv_ref, dlr_ref, dwgf_ref, dwgb_ref, dbg_ref,
             g_s, b_s, fac_s, dg_s, st_s, dst_s, acc_s):
        lr_bf = lr_ref[...].astype(BF16)
        dlr = jnp.zeros((lf, LANES), F32)
        for direction in (0, 1):
            wg_ref, bg_ref = ((wgf_ref, bgf_ref), (wgb_ref, bgb_ref))[direction]
            z, valid = _gla_gates(lr_bf, wg_ref, bg_ref, lf)
            g_s[...] = jnp.where(valid, _log_sigmoid(z) / GATE_NORM, 0.0)
            fac_s[...] = jnp.where(valid, _sigmoid(-z) / GATE_NORM, 0.0)
            tri, tri_t, smask = _group_masks(direction)
            end_row = CHUNK - 1 if direction == 0 else 0
            _gla_states(direction, n_groups, qkv_ref, g_s, b_s, st_s, tri)

            def state_grad_local(g, carry):
                rows = _group_rows(g)
                q = qkv_ref[rows, 0:128].astype(F32) * scale
                q_in = (q * jnp.exp(b_s[rows, :])).astype(BF16)
                dst_s[g] = _dot_tn(do_ref[rows, :], _diag_blocks(q_in))
                return carry

            lax.fori_loop(0, n_groups, state_grad_local, 0, unroll=_group_unroll(n_groups))
            _state_scan(direction, n_groups, b_s, dst_s, True)

            def group_grads(g, carry):
                rows = _group_rows(g)
                b = b_s[rows, :]
                ends = _chunk_end_rows(direction, b)
                b_end = _per_chunk_rows(ends)
                q = qkv_ref[rows, 0:128].astype(F32) * scale
                k = qkv_ref[rows, 128:256].astype(F32)
                v = qkv_ref[rows, 256:512]
                d_out = do_ref[rows, :]
                e_pos = jnp.exp(b)
                e_neg = jnp.exp(-b)
                e_end = jnp.exp(b_end - b)
                q_in = q * e_pos
                k_in = k * e_neg
                k_dec = k * e_end
                q_in_bf = q_in.astype(BF16)
                k_in_bf = k_in.astype(BF16)
                state = st_s[g]
                d_state = dst_s[g]
                state_bf = state.astype(BF16)
                d_state_bf = d_state.astype(BF16)
                s = jnp.where(smask, _dot_nt(q_in_bf, k_in_bf), 0.0).astype(BF16)
                ds = jnp.where(smask, _dot_nt(d_out, v), 0.0).astype(BF16)
                dv = _dot_tn(s, d_out) + _dot_nt(_diag_blocks(k_dec.astype(BF16)), d_state_bf)
                dq_in = _dot(ds, k_in_bf) + _pick_diag(_dot(d_out, state_bf))
                dk_in = _dot_tn(ds, q_in_bf)
                dk_dec = _pick_diag(_dot(v, d_state_bf))
                dq = dq_in * e_pos * scale
                dk = dk_in * e_neg + dk_dec * e_end
                if direction == 0:
                    acc_s[rows, 0:128] = dq
                    acc_s[rows, 128:256] = dk
                    acc_s[rows, 256:512] = dv
                else:
                    dqkv_ref[rows, 0:128] = (acc_s[rows, 0:128] + dq).astype(BF16)
                    dqkv_ref[rows, 128:256] = (acc_s[rows, 128:256] + dk).astype(BF16)
                    dqkv_ref[rows, 256:512] = (acc_s[rows, 256:512] + dv).astype(BF16)
                dkk = dk_dec * k_dec
                db = dq_in * q_in - dk_in * k_in - dkk
                d_decay = jnp.sum(d_state * state, axis=0, keepdims=True)
                db_end = [jnp.sum(dkk[r * CHUNK:(r + 1) * CHUNK, :], axis=0, keepdims=True)
                          + d_decay[:, r * HEAD_K:(r + 1) * HEAD_K] * jnp.exp(ends[r]) for r in range(GROUP)]
                row = lax.broadcasted_iota(jnp.int32, (GROUP_ROWS, HEAD_K), 0)
                at_end = row == end_row
                for r in range(1, GROUP):
                    at_end = at_end | (row == r * CHUNK + end_row)
                db = db + jnp.where(at_end, _per_chunk_rows(db_end), 0.0)
                dg_s[rows, :] = _tri_dot(tri_t, db)
                return carry

            lax.fori_loop(0, n_groups, group_grads, 0, unroll=_group_unroll(n_groups))

            dz = dg_s[...] * fac_s[...]
            dz_bf = dz.astype(BF16)
            dbg_ref[direction:direction + 1, :] = jnp.sum(dz, axis=0, keepdims=True)
            (dwgf_ref, dwgb_ref)[direction][...] = _dot_tn(lr_bf, dz_bf)
            dlr = dlr + _dot_nt(dz_bf, wg_ref[...])

        @pl.when(pl.program_id(1) == 0)
        def _():
            dlr_ref[...] = dlr

        @pl.when(pl.program_id(1) != 0)
        def _():
            dlr_ref[...] = dlr_ref[...] + dlr

    gate_w = pl.BlockSpec((None, LANES, HEAD_K), lambda b, h: (h, 0, 0))
    gate_b = pl.BlockSpec((None, 1, HEAD_K), lambda b, h: (h, 0, 0))
    return pl.pallas_call(
        body, name="gla_bwd", grid=(n_seq, N_HEADS),
        in_specs=[pl.BlockSpec((lf, 512), lambda b, h: (b, N_CONV_TILES + h)),
                  pl.BlockSpec((lf, LANES), lambda b, h: (b, 0)),
                  pl.BlockSpec((lf, HEAD_V), lambda b, h: (b, h)),
                  gate_w, gate_w, gate_b, gate_b,
                  pl.BlockSpec((8, LANES), lambda b, h: (0, 0))],
        out_specs=[pl.BlockSpec((lf, 512), lambda b, h: (b, h)),
                   pl.BlockSpec((lf, LANES), lambda b, h: (b, 0)),
                   pl.BlockSpec((None, None, LANES, HEAD_K), lambda b, h: (b, h, 0, 0)),
                   pl.BlockSpec((None, None, LANES, HEAD_K), lambda b, h: (b, h, 0, 0)),
                   pl.BlockSpec((None, None, 2, HEAD_K), lambda b, h: (b, h, 0, 0))],
        out_shape=[jax.ShapeDtypeStruct((n_seq * lf, W_GLA), BF16),
                   jax.ShapeDtypeStruct((n_seq * lf, LANES), F32),
                   jax.ShapeDtypeStruct((n_seq, N_HEADS, LANES, HEAD_K), F32),
                   jax.ShapeDtypeStruct((n_seq, N_HEADS, LANES, HEAD_K), F32),
                   jax.ShapeDtypeStruct((n_seq, N_HEADS, 2, HEAD_K), F32)],
        scratch_shapes=[pltpu.VMEM((lf, HEAD_K), F32), pltpu.VMEM((lf, HEAD_K), F32),
                        pltpu.VMEM((lf, HEAD_K), F32), pltpu.VMEM((lf, HEAD_K), F32),
                        pltpu.VMEM((n_groups, HEAD_V, GROUP * HEAD_K), F32),
                        pltpu.VMEM((n_groups, HEAD_V, GROUP * HEAD_K), F32),
                        pltpu.VMEM((lf, 512), F32)],
        compiler_params=_params(("parallel", "arbitrary"), 56),
    )(proj, lr, d_o, wgf, wgb, bgf, bgb, token)


def _tail(h, tgt, yc, o, proj, w3, gamma, g_post, lf):
    t_rows = h.shape[0]
    tm = _pick_tile(t_rows, 256, CHUNK)
    n_chunks = lf // CHUNK
    per_tile = tm // CHUNK

    def body(h_ref, tgt_ref, yc_ref, o_ref, r_ref, ma_ref, mb_ref, w_hbm, gamma_ref, gpost_ref,
             dres_ref, yg_ref, merged_ref, dout_ref, dpc_ref, dpg_ref, dyc_ref, do_ref, dtail_ref,
             loss_ref, dgpost_ref, dgamma_ref, w_s, w_sem):
        i = pl.program_id(0)

        @pl.when(i == 0)
        def _():
            cp = pltpu.make_async_copy(w_hbm, w_s, w_sem)
            cp.start()
            cp.wait()
            loss_ref[...] = jnp.zeros_like(loss_ref)
            dgpost_ref[...] = jnp.zeros_like(dgpost_ref)
            dgamma_ref[...] = jnp.zeros_like(dgamma_ref)

        gamma = gamma_ref[...]
        o = o_ref[...]
        r = r_ref[...].astype(F32)
        sr = _sigmoid(r)
        silu_r = r * sr
        n_parts, rstd_parts = [], []
        for hd in range(N_HEADS):
            oh = o[:, hd * HEAD_V:(hd + 1) * HEAD_V]
            rstd = lax.rsqrt(jnp.mean(oh * oh, axis=-1, keepdims=True) + EPS)
            n_parts.append(oh * rstd)
            rstd_parts.append(rstd)
        n = jnp.concatenate(n_parts, axis=-1)
        gamma_t = jnp.concatenate([gamma] * N_HEADS, axis=-1)
        yg = n * gamma_t * silu_r
        yg_bf = yg.astype(BF16)
        yg_ref[...] = yg_bf
        yc = yc_ref[...]
        pc = _dot(yc, w_s[0])
        pg = _dot(yg_bf, w_s[1])
        sa = _sigmoid(ma_ref[...].astype(F32))
        sb = _sigmoid(mb_ref[...].astype(F32))
        merged = (sa * pc + sb * pg).astype(BF16)
        merged_ref[...] = merged
        out = _dot(merged, w_s[2])
        rstd2 = lax.rsqrt(jnp.mean(out * out, axis=-1, keepdims=True) + EPS)
        nn = out * rstd2
        gpost = gpost_ref[...]
        y = h_ref[...] + nn * gpost

        rowi = lax.broadcasted_iota(jnp.int32, (tm, 1), 0)
        keep = jnp.zeros((tm, 1), F32)
        for kk in range(per_tile):
            is_tok = ((i * per_tile + kk) % n_chunks) != 0
            f = jnp.where(is_tok, 1.0, 0.0)
            keep = jnp.where((rowi >= kk * CHUNK) & (rowi < (kk + 1) * CHUNK), f, keep)
        diff = (y - tgt_ref[...]) * keep
        loss_ref[...] += jnp.sum(diff * diff) * (0.5 / D)
        dy = diff * (1.0 / D)
        dres_ref[...] = dy
        dgpost_ref[...] += jnp.sum(dy * nn, axis=0, keepdims=True)
        dn = dy * gpost
        dout_f = rstd2 * (dn - nn * jnp.mean(dn * nn, axis=-1, keepdims=True))
        dout = dout_f.astype(BF16)
        dout_ref[...] = jnp.transpose(dout_f).astype(BF16)
        dmerged = _dot_nt(dout, w_s[2])
        dpc_f = dmerged * sa
        dpg_f = dmerged * sb
        dpc = dpc_f.astype(BF16)
        dpg = dpg_f.astype(BF16)
        dpc_ref[...] = jnp.transpose(dpc_f).astype(BF16)
        dpg_ref[...] = jnp.transpose(dpg_f).astype(BF16)
        dtail_ref[:, D:2 * D] = (dmerged * pc * (sa * (1.0 - sa))).astype(BF16)
        dtail_ref[:, 2 * D:3 * D] = (dmerged * pg * (sb * (1.0 - sb))).astype(BF16)
        dyc_ref[...] = _dot_nt(dpc, w_s[0]).astype(BF16)
        dyg = _dot_nt(dpg, w_s[1])
        dtail_ref[:, 0:D] = (dyg * n * gamma_t * (sr * (1.0 + r * (1.0 - sr)))).astype(BF16)
        dgam_full = jnp.sum(dyg * n * silu_r, axis=0, keepdims=True)
        dgam = dgam_full[:, 0:HEAD_V]
        for hd in range(1, N_HEADS):
            dgam = dgam + dgam_full[:, hd * HEAD_V:(hd + 1) * HEAD_V]
        dgamma_ref[...] += dgam
        dng = dyg * gamma_t * silu_r
        do_parts = []
        for hd in range(N_HEADS):
            sl = slice(hd * HEAD_V, (hd + 1) * HEAD_V)
            dnh = dng[:, sl]
            nh = n_parts[hd]
            do_parts.append(rstd_parts[hd] * (dnh - nh * jnp.mean(dnh * nh, axis=-1, keepdims=True)))
        do_ref[...] = jnp.concatenate(do_parts, axis=-1).astype(BF16)

    row = lambda c: pl.BlockSpec((tm, D), lambda i: (i, c))
    col = pl.BlockSpec((D, tm), lambda i: (0, i))
    const = lambda shape: pl.BlockSpec(shape, lambda i: (0, 0))
    act = jax.ShapeDtypeStruct((t_rows, D), BF16)
    act_t = jax.ShapeDtypeStruct((D, t_rows), BF16)
    return pl.pallas_call(
        body, name="tail", grid=(t_rows // tm,),
        in_specs=[row(0), row(0), row(0), row(0), row(6), row(7), row(8),
                  pl.BlockSpec(memory_space=pl.ANY), const((1, HEAD_V)), const((1, D))],
        out_specs=[row(0)] * 3 + [col] * 3 + [row(0)] * 2
                  + [pl.BlockSpec((tm, W_TAIL), lambda i: (i, 0)),
                     const((8, LANES)), const((1, D)), const((1, HEAD_V))],
        out_shape=[jax.ShapeDtypeStruct((t_rows, D), F32)] + [act] * 2 + [act_t] * 3 + [act] * 2
                  + [jax.ShapeDtypeStruct((t_rows, W_TAIL), BF16),
                     jax.ShapeDtypeStruct((8, LANES), F32),
                     jax.ShapeDtypeStruct((1, D), F32),
                     jax.ShapeDtypeStruct((1, HEAD_V), F32)],
        scratch_shapes=[pltpu.VMEM((3, D, D), BF16), pltpu.SemaphoreType.DMA],
        compiler_params=_params(("arbitrary",), 56),
    )(h, tgt, yc, o, proj, proj, proj, w3, gamma, g_post)


def _wgrad_t(a_t, b, name, out_dtype=BF16):
    m, t_rows = a_t.shape
    n = b.shape[1]
    tn = D if n % D == 0 else n
    tk = _pick_tile(t_rows, 768, LANES)
    n_k = t_rows // tk

    def body(a_ref, b_ref, o_ref, acc):
        k = pl.program_id(1)

        @pl.when(k == 0)
        def _():
            acc[...] = jnp.zeros_like(acc)

        acc[...] += _dot(a_ref[...], b_ref[...].astype(BF16))

        @pl.when(k == n_k - 1)
        def _():
            o_ref[...] = jnp.transpose(acc[...]).astype(out_dtype)

    return pl.pallas_call(
        body, name=name, grid=(n // tn, n_k),
        in_specs=[pl.BlockSpec((m, tk), lambda j, k: (0, k)),
                  pl.BlockSpec((tk, tn), lambda j, k: (k, j))],
        out_specs=pl.BlockSpec((tn, m), lambda j, k: (j, 0)),
        out_shape=jax.ShapeDtypeStruct((n, m), out_dtype),
        scratch_shapes=[pltpu.VMEM((m, tn), F32)],
        compiler_params=_params(("parallel", "arbitrary"), 48),
    )(a_t, b)


def _dgrad_in(dpc, dpg, dpt, dlr, w_full_t, h, g_pre, dres, token):
    t_rows = h.shape[0]
    tm = _pick_tile(t_rows, 256, 16)
    n_main = N_MAIN

    def body(dpc_ref, dpg_ref, dpt_ref, dlr_ref, w_hbm, h_ref, g_ref, dres_ref, token_ref,
             dh_ref, dg_ref, w_s, wlr_s, w_sems):
        @pl.when(pl.program_id(0) == 0)
        def _():
            _load_weights(w_hbm, w_s, wlr_s, w_sems)
            dg_ref[...] = jnp.zeros_like(dg_ref)

        du = _dot(dlr_ref[...].astype(BF16), wlr_s[...])
        du += _dot(dpc_ref[...], w_s[0:W_CONV, :])
        du += _dot(dpg_ref[...], w_s[W_CONV:W_CONV + W_GLA, :])
        du += _dot(dpt_ref[...], w_s[W_CONV + W_GLA:n_main, :])
        hh = h_ref[...]
        rstd = lax.rsqrt(jnp.mean(hh * hh, axis=-1, keepdims=True) + EPS)
        xhat = hh * rstd
        dg_ref[...] += jnp.sum(du * xhat, axis=0, keepdims=True)
        dx = du * g_ref[...]
        dh_ref[...] = rstd * (dx - xhat * jnp.mean(dx * xhat, axis=-1, keepdims=True)) + dres_ref[...]

    row = lambda width: pl.BlockSpec((tm, width), lambda i: (i, 0))
    return pl.pallas_call(
        body, name="dgrad_in", grid=(t_rows // tm,),
        in_specs=[row(W_CONV), row(W_GLA), row(W_TAIL), row(LANES),
                  pl.BlockSpec(memory_space=pl.ANY),
                  row(D), pl.BlockSpec((1, D), lambda i: (0, 0)), row(D),
                  pl.BlockSpec((8, LANES), lambda i: (0, 0))],
        out_specs=[row(D), pl.BlockSpec((1, D), lambda i: (0, 0))],
        out_shape=[jax.ShapeDtypeStruct((t_rows, D), F32), jax.ShapeDtypeStruct((1, D), F32)],
        scratch_shapes=[pltpu.VMEM((n_main, D), BF16), pltpu.VMEM((LANES, D), BF16),
                        pltpu.SemaphoreType.DMA((N_WEIGHT_COPIES,))],
        compiler_params=_params(("arbitrary",), 56),
    )(dpc, dpg, dpt, dlr, w_full_t, h, g_pre, dres, token)


def _reference_rows(g_conv, g_gla, g_tail, g_lr):
    conv = g_conv.reshape(N_CONV_TILES, 4, 128, D).transpose(1, 0, 2, 3).reshape(W_CONV, D)
    gla = g_gla.reshape(N_HEADS, 512, D)
    q = gla[:, 0:128].reshape(N_HEADS * HEAD_K, D)
    k = gla[:, 128:256].reshape(N_HEADS * HEAD_K, D)
    v = gla[:, 256:512].reshape(N_HEADS * HEAD_V, D)
    return jnp.concatenate([conv, q, k, v, g_tail[0:D], g_lr[0:2 * RANK], g_tail[D:3 * D]], axis=0)


def _pack(arrs, rows):
    flat = jnp.concatenate([a.reshape(-1) for a in arrs])
    return jnp.pad(flat, (0, rows * LANES - flat.shape[0])).reshape(rows, LANES)


def _unpack(packed, shapes):
    flat = packed.reshape(-1)
    out, pos = [], 0
    for s in shapes:
        size = 1
        for d in s:
            size *= d
        out.append(flat[pos:pos + size].reshape(s))
        pos += size
    return out


def _rows_for(shapes, mult=8):
    total = 0
    for s in shapes:
        size = 1
        for d in s:
            size *= d
        total += size
    return -(-total // (mult * LANES)) * mult


def kernel(x, meta_tokens, norm_pre, w_in, conv_w, w_gate_fwd, b_gate_fwd, w_gate_bwd, b_gate_bwd, gla_norm, w_out_conv, w_out_gla, w_merge_out, norm_post, loss_target, m_meta_tokens, m_norm_pre, m_w_in, m_conv_w, m_w_gate_fwd, m_b_gate_fwd, m_w_gate_bwd, m_b_gate_bwd, m_gla_norm, m_w_out_conv, m_w_out_gla, m_w_merge_out, m_norm_post, v_meta_tokens, v_norm_pre, v_w_in, v_conv_w, v_w_gate_fwd, v_b_gate_fwd, v_w_gate_bwd, v_b_gate_bwd, v_gla_norm, v_w_out_conv, v_w_out_gla, v_w_merge_out, v_norm_post):
    n_seq, seq, _ = x.shape
    lf = CHUNK + seq
    t_rows = n_seq * lf
    shard = 2 * lax.axis_index("x") + lax.axis_index("y")

    w_in_bf = _cast_bf16(jnp.transpose(w_in[0]), "cast_w_in")
    w_out_bf = _cast_bf16(jnp.concatenate([w_out_conv[0], w_out_gla[0], w_merge_out[0]], axis=0), "cast_w_out")
    small_shapes = [(3, D // 4), (RANK, HEAD_K), (RANK, HEAD_K)]
    small = _pack([conv_w[0], w_gate_fwd[0], w_gate_bwd[0]], _rows_for(small_shapes, 32))
    meta_rows = jnp.pad(meta_tokens, ((0, N_META), (0, 0)))
    w_in_all, _, small_all, h3, tgt3 = _gather_via_sibling(
        "gather_w_in", [w_in_bf, meta_rows, small], x, loss_target)
    w_out_state, _ = _plane_start("gather_w_out_start", [w_out_bf], "gather", small_all)

    w_full_t = w_in_all.reshape(N_IN, D)
    smalls = [_unpack(small_all[s], small_shapes) for s in range(4)]
    conv_full = jnp.concatenate([smalls[s][0] for s in range(4)], axis=1)
    wgf = jnp.stack([jnp.pad(smalls[s][1], ((0, LANES - RANK), (0, 0))) for s in range(4)]).astype(BF16)
    wgb = jnp.stack([jnp.pad(smalls[s][2], ((RANK, LANES - 2 * RANK), (0, 0))) for s in range(4)]).astype(BF16)
    bgf = b_gate_fwd.reshape(N_HEADS, 1, HEAD_K)
    bgb = b_gate_bwd.reshape(N_HEADS, 1, HEAD_K)

    h = h3.reshape(t_rows, D)
    tgt = tgt3.reshape(t_rows, D)

    proj, u_t, lr = _in_proj(h, norm_pre, w_full_t)
    yc = _conv_fwd(proj, conv_full, n_seq, lf)
    o = _gla_fwd(proj, lr, wgf, wgb, bgf, bgb, n_seq, lf)
    (w_out_landed,) = _plane_wait("gather_w_out_wait", w_out_state, "gather", o)
    slot_ids = lax.broadcasted_iota(jnp.int32, (4, 1, 1), 0)
    w_out_all = jnp.where(slot_ids == shard, w_out_bf[None], w_out_landed)
    w3 = jnp.transpose(w_out_all.reshape(4, 3, D // 4, D), (1, 0, 2, 3)).reshape(3, D, D)
    (dres, yg, merged, dout_t, dpc_t, dpg_t, dyc, d_o, dtail, loss_acc, d_gpost, d_gamma) = _tail(
        h, tgt, yc, o, proj, w3, gla_norm, norm_post, lf)
    g_w_oc = _wgrad_t(dpc_t, yc, "wgrad_out_conv")
    g_w_og = _wgrad_t(dpg_t, yg, "wgrad_out_gla")
    g_w_mo = _wgrad_t(dout_t, merged, "wgrad_merge_out")
    g_out_slots = jnp.concatenate([g.reshape(4, D // 4, D) for g in (g_w_oc, g_w_og, g_w_mo)], axis=1)
    out_state, out_token = _plane_start("scatter_out_grads_start", [g_out_slots], "scatter", g_w_mo)
    dgla, dlr, dwgf_p, dwgb_p, dbg_p = _gla_bwd(proj, lr, d_o, wgf, wgb, bgf, bgb, n_seq, lf, out_token)
    (got_out,) = _plane_wait("scatter_out_grads_wait", out_state, "scatter", dlr)
    dconv, dconvw_p = _conv_bwd(proj, conv_full, dyc, n_seq, lf)
    g_conv = _wgrad_t(u_t, dconv, "wgrad_in_conv")
    g_gla = _wgrad_t(u_t, dgla, "wgrad_in_gla")
    g_tail = _wgrad_t(u_t, dtail, "wgrad_in_tail")
    g_lr = _wgrad_t(u_t, dlr, "wgrad_in_lr")

    g_in_slots = _reference_rows(g_conv, g_gla, g_tail, g_lr).reshape(4, SHARD_IN, D)
    in_state, in_token = _plane_start("scatter_in_grads_start", [g_in_slots], "scatter", g_lr)
    dh, d_gpre = _dgrad_in(dconv, dgla, dtail, dlr, w_full_t, h, norm_pre, dres, in_token)
    (got_in,) = _plane_wait("scatter_in_grads_wait", in_state, "scatter", d_gpre)

    dh3 = dh.reshape(n_seq, lf, D)
    grad_x = dh3[:, CHUNK:, :]

    d_meta = jnp.sum(dh3[:, PAD_FRONT:CHUNK, :], axis=0)
    d_convw = jnp.sum(dconvw_p, axis=0)
    d_wgf = jnp.transpose(jnp.sum(dwgf_p, axis=0)[:, 0:RANK, :], (1, 0, 2)).reshape(RANK, N_HEADS * HEAD_K)
    d_wgb = jnp.transpose(jnp.sum(dwgb_p, axis=0)[:, RANK:2 * RANK, :], (1, 0, 2)).reshape(RANK, N_HEADS * HEAD_K)
    d_bg = jnp.sum(dbg_p, axis=0)
    d_bgf = d_bg[:, 0, :].reshape(1, N_HEADS * HEAD_K)
    d_bgb = d_bg[:, 1, :].reshape(1, N_HEADS * HEAD_K)
    part_shapes = [(N_META, D), (3, D), (RANK, 512), (RANK, 512), (1, D), (1, 512), (1, 512), (1, HEAD_V),
                   (1, D), (1, LANES)]
    parts = _pack([d_meta, d_convw, d_wgf, d_wgb, d_gpre, d_bgf, d_bgb, d_gamma, d_gpost, loss_acc[0:1, :]],
                  _rows_for(part_shapes))
    (parts_all,) = _exchange("gather_small_grads", [parts], ALL_FLIPS, (4, 2, 1), "gather")
    (g_meta, g_convw, g_wgf, g_wgb, g_npre, g_bgf, g_bgb, g_gnorm, g_npost, loss_row) = _unpack(
        _sum_slots(parts_all, "sum_small_grads"), part_shapes)
    loss = loss_row[0, 0]

    def col_shard(a, width):
        return lax.dynamic_slice_in_dim(a, shard * width, width, axis=a.ndim - 1)

    upd_shapes = [(N_META, D // 4), (3, D // 4), (RANK, HEAD_K), (RANK, HEAD_K), (1, D), (1, 512), (1, 512),
                  (1, HEAD_V), (1, D)]
    upd_rows = _rows_for(upd_shapes)
    small_w = _pack([meta_tokens, conv_w[0], w_gate_fwd[0], w_gate_bwd[0], norm_pre, b_gate_fwd, b_gate_bwd,
                     gla_norm, norm_post], upd_rows)
    small_g = _pack([col_shard(g_meta, D // 4), col_shard(g_convw, D // 4), col_shard(g_wgf, HEAD_K),
                     col_shard(g_wgb, HEAD_K), g_npre, g_bgf, g_bgb, g_gnorm, g_npost], upd_rows)
    small_m = _pack([m_meta_tokens, m_conv_w[0], m_w_gate_fwd[0], m_w_gate_bwd[0], m_norm_pre, m_b_gate_fwd,
                     m_b_gate_bwd, m_gla_norm, m_norm_post], upd_rows)
    small_v = _pack([v_meta_tokens, v_conv_w[0], v_w_gate_fwd[0], v_w_gate_bwd[0], v_norm_pre, v_b_gate_fwd,
                     v_b_gate_bwd, v_gla_norm, v_norm_post], upd_rows)
    small_out = [_unpack(a, upd_shapes) for a in _adamw(small_w, [small_g], small_m, small_v, "adamw_small")]

    plane_in = _sum_slots(got_in, "sum_w_in_grads", own=g_in_slots)
    plane_out = _sum_slots(got_out, "sum_w_out_grads", own=g_out_slots)
    other_in, other_out = _exchange("swap_plane_sums", [plane_in, plane_out], SIBLING_FLIPS, (0, 0, 0), "swap")
    big_in = _adamw(jnp.transpose(w_in[0]), [plane_in, other_in], jnp.transpose(m_w_in[0]), jnp.transpose(v_w_in[0]),
                    "adamw_w_in")
    out_params = ((w_out_conv, m_w_out_conv, v_w_out_conv), (w_out_gla, m_w_out_gla, v_w_out_gla),
                  (w_merge_out, m_w_merge_out, v_w_merge_out))
    big_out = [_adamw(w[0], [plane_out, other_out], m[0], v[0], f"adamw_w_out_{i}", grad_row=i * (D // 4))
               for i, (w, m, v) in enumerate(out_params)]

    results = []
    for kind in range(4):
        sm = small_out[kind]
        w_in_part = jnp.transpose(big_in[kind])[None]
        outs3 = [big_out[i][kind][None] for i in range(3)]
        results.extend([
            sm[0], sm[4], w_in_part, sm[1][None], sm[2][None], sm[5], sm[3][None], sm[6], sm[7],
            outs3[0], outs3[1], outs3[2], sm[8]])
    return (loss, grad_x, *results)
```

```python
import functools

import jax
import jax.numpy as jnp
from jax import lax
from jax.experimental import pallas as pl
from jax.experimental.pallas import tpu as pltpu

F32 = jnp.float32
BF16 = jnp.bfloat16
MESH = pl.DeviceIdType.MESH

D = 1024
N_META = 16
CHUNK = 64
PAD_FRONT = CHUNK - N_META
N_HEADS = 4
HEAD_K = 128
HEAD_V = 256
RANK = 16
EPS = 1e-6
GATE_NORM = 16.0
N_IN = 9248
SHARD_IN = N_IN // 4
LANES = 128
N_CONV_TILES = 8
W_CONV = 4096
W_GLA = 2048
W_TAIL = 3072
N_MAIN = W_CONV + W_GLA + W_TAIL
OFF_Q, OFF_K, OFF_V, OFF_R = 4096, 4608, 5120, 6144
OFF_LR, OFF_MA, OFF_MB = 7168, 7200, 8224
MIB = 1024 * 1024

ADAM_LR = 0.001
ADAM_B1 = 0.9
ADAM_B2 = 0.999
ADAM_EPS = 1e-08
ADAM_WD = 0.01
ADAM_STEP = 10


def _params(sem=None, vmem_mib=None):
    return pltpu.CompilerParams(
        dimension_semantics=sem,
        vmem_limit_bytes=None if vmem_mib is None else vmem_mib * MIB)


def _pick_tile(n, target, mult):
    best = None
    for t in range(mult, min(n, target) + 1, mult):
        if n % t == 0:
            best = t
    return n if best is None else best


def _sigmoid(v):
    return 1.0 / (1.0 + jnp.exp(-v))


def _log_sigmoid(v):
    return jnp.minimum(v, 0.0) - jnp.log(1.0 + jnp.exp(-jnp.abs(v)))


def _dot(a, b):
    return jnp.dot(a, b, preferred_element_type=F32)


def _dot_nt(a, b):
    return lax.dot_general(a, b, (((1,), (1,)), ((), ())), preferred_element_type=F32)


def _dot_tn(a, b):
    return lax.dot_general(a, b, (((0,), (0,)), ((), ())), preferred_element_type=F32)


def _tri_dot(tri, v):
    hi = v.astype(BF16)
    lo = (v - hi.astype(F32)).astype(BF16)
    return _dot(tri, hi) + _dot(tri, lo)


PLANE_FLIPS = ((1, 0, 0), (0, 1, 0), (1, 1, 0))
ALL_FLIPS = tuple((m >> 2 & 1, m >> 1 & 1, m & 1) for m in range(1, 8))
SIBLING_FLIPS = ((0, 0, 1),)


def _exchange(name, arrs, flips, slot_weights, mode):
    n = len(arrs)
    n_slots = 1
    for w in slot_weights:
        n_slots += w
    if mode == "gather":
        out_shape = [jax.ShapeDtypeStruct((n_slots,) + a.shape, a.dtype) for a in arrs]
    else:
        out_shape = [jax.ShapeDtypeStruct(a.shape, a.dtype) for a in arrs]

    def body(*refs):
        ins, outs = refs[:n], refs[n:2 * n]
        send_sems, recv_sems, local_sems = refs[2 * n:]
        pos = (lax.axis_index("x"), lax.axis_index("y"), lax.axis_index("c"))

        def slot_of(p):
            return p[0] * slot_weights[0] + p[1] * slot_weights[1] + p[2] * slot_weights[2]

        peers = [tuple(1 - pos[a] if f[a] else pos[a] for a in range(3)) for f in flips]
        me = slot_of(pos)
        local = []
        sends = []
        for i in range(n):
            if mode != "swap":
                src = ins[i] if mode == "gather" else ins[i].at[me]
                cp = pltpu.make_async_copy(src, outs[i].at[me], local_sems.at[i])
                cp.start()
                local.append(cp)
            for k, peer in enumerate(peers):
                if mode == "gather":
                    src, dst = ins[i], outs[i].at[me]
                elif mode == "scatter":
                    src, dst = ins[i].at[slot_of(peer)], outs[i].at[me]
                else:
                    src, dst = ins[i], outs[i]
                cp = pltpu.make_async_remote_copy(
                    src_ref=src, dst_ref=dst, send_sem=send_sems.at[i, k], recv_sem=recv_sems.at[i, k],
                    device_id=peer, device_id_type=MESH)
                cp.start()
                sends.append(cp)
        for i in range(n):
            for k, peer in enumerate(peers):
                if mode == "gather":
                    src, dst = ins[i], outs[i].at[slot_of(peer)]
                elif mode == "scatter":
                    src, dst = ins[i].at[me], outs[i].at[slot_of(peer)]
                else:
                    src, dst = ins[i], outs[i]
                arrival = pltpu.make_async_remote_copy(
                    src_ref=src, dst_ref=dst, send_sem=send_sems.at[i, k], recv_sem=recv_sems.at[i, k],
                    device_id=peer, device_id_type=MESH)
                arrival.wait_recv()
        for cp in sends:
            cp.wait_send()
        for cp in local:
            cp.wait()

    hbm = pl.BlockSpec(memory_space=pl.ANY)
    outs = pl.pallas_call(
        body, name=name, out_shape=out_shape,
        in_specs=[hbm] * n, out_specs=[hbm] * n,
        scratch_shapes=[pltpu.SemaphoreType.DMA((n, len(flips))),
                        pltpu.SemaphoreType.DMA((n, len(flips))),
                        pltpu.SemaphoreType.DMA((n,))],
        compiler_params=pltpu.CompilerParams(has_side_effects=True),
    )(*arrs)
    return list(outs)


def _gather_via_sibling(name, arrs):
    n = len(arrs)
    out_shape = [jax.ShapeDtypeStruct((4,) + a.shape, a.dtype) for a in arrs]

    def body(*refs):
        ins, outs = refs[:n], refs[n:2 * n]
        send_sems, recv_sems, local_sems = refs[2 * n:]
        x, y, c = lax.axis_index("x"), lax.axis_index("y"), lax.axis_index("c")
        me = 2 * x + y
        chips = [(1 - x, y), (x, 1 - y), (1 - x, 1 - y)]

        def half(ref, which):
            rows = ref.shape[0]
            cut = rows // 2 // 16 * 16
            return ref.at[pl.ds(0, cut)] if which == 0 else ref.at[pl.ds(cut, rows - cut)]

        def copy(src, dst, i, k, to):
            return pltpu.make_async_remote_copy(
                src_ref=src, dst_ref=dst, send_sem=send_sems.at[i, k], recv_sem=recv_sems.at[i, k],
                device_id=to, device_id_type=MESH)

        def run(mine):
            other = 1 - mine
            local, sends = [], []
            for i in range(n):
                cp = pltpu.make_async_copy(ins[i], outs[i].at[me], local_sems.at[i])
                cp.start()
                local.append(cp)
                for k, (px, py) in enumerate(chips):
                    cp = copy(half(ins[i], mine), half(outs[i].at[me], mine), i, k, (px, py, mine))
                    cp.start()
                    sends.append(cp)
            for k, (px, py) in enumerate(chips):
                slot = 2 * px + py
                for i in range(n):
                    landed = half(outs[i].at[slot], mine)
                    copy(half(ins[i], mine), landed, i, k, (px, py, mine)).wait_recv()
                    cp = copy(landed, landed, i, 3 + k, (x, y, other))
                    cp.start()
                    sends.append(cp)
            for k, (px, py) in enumerate(chips):
                slot = 2 * px + py
                for i in range(n):
                    passed = half(outs[i].at[slot], other)
                    copy(passed, passed, i, 3 + k, (x, y, other)).wait_recv()
            for cp in sends:
                cp.wait_send()
            for cp in local:
                cp.wait()

        for mine in (0, 1):
            pl.when(c == mine)(functools.partial(run, mine))

    hbm = pl.BlockSpec(memory_space=pl.ANY)
    outs = pl.pallas_call(
        body, name=name, out_shape=out_shape,
        in_specs=[hbm] * n, out_specs=[hbm] * n,
        scratch_shapes=[pltpu.SemaphoreType.DMA((n, 6)), pltpu.SemaphoreType.DMA((n, 6)),
                        pltpu.SemaphoreType.DMA((n,))],
        compiler_params=pltpu.CompilerParams(has_side_effects=True),
    )(*arrs)
    return list(outs)


HBM_SPEC = pl.BlockSpec(memory_space=pltpu.HBM)
SEM_SPEC = pl.BlockSpec(memory_space=pltpu.SEMAPHORE)
DATAFLOW = pltpu.SideEffectType.DATAFLOW_SIDE_EFFECTING


def _plane_peers():
    x, y, c = lax.axis_index("x"), lax.axis_index("y"), lax.axis_index("c")
    return 2 * x + y, [((1 - x, y, c), 2 * (1 - x) + y), ((x, 1 - y, c), 2 * x + 1 - y),
                       ((1 - x, 1 - y, c), 2 * (1 - x) + 1 - y)]


def _plane_start(name, arrs, mode, after):
    n = len(arrs)
    lands = [lax.empty(((4,) + a.shape) if mode == "gather" else a.shape, a.dtype) for a in arrs]

    def body(*refs):
        srcs, landing = refs[:n], refs[n:2 * n]
        send_sems, recv_sems = refs[2 * n + 1], refs[2 * n + 2]
        token = refs[-1]
        me, peers = _plane_peers()
        for i in range(n):
            for k, (peer, peer_slot) in enumerate(peers):
                src = srcs[i] if mode == "gather" else srcs[i].at[peer_slot]
                pltpu.make_async_remote_copy(
                    src_ref=src, dst_ref=landing[i].at[me], send_sem=send_sems.at[3 * i + k],
                    recv_sem=recv_sems.at[3 * i + k], device_id=peer, device_id_type=MESH).start()
        token[...] = jnp.zeros_like(token)

    hbm_in = [pltpu.with_memory_space_constraint(a, pltpu.HBM) for a in list(arrs) + lands]
    out = pl.pallas_call(
        body, name=name,
        out_shape=[pltpu.SemaphoreType.DMA((3 * n,)), pltpu.SemaphoreType.DMA((3 * n,))]
                  + [pltpu.HBM(a.shape, a.dtype) for a in hbm_in]
                  + [jax.ShapeDtypeStruct((8, LANES), F32)],
        in_specs=[HBM_SPEC] * (2 * n) + [pl.BlockSpec(memory_space=pl.ANY)],
        out_specs=[SEM_SPEC, SEM_SPEC] + [HBM_SPEC] * (2 * n) + [pl.BlockSpec(memory_space=pltpu.VMEM)],
        input_output_aliases={i: 2 + i for i in range(2 * n)},
        compiler_params=pltpu.CompilerParams(has_side_effects=DATAFLOW),
    )(*hbm_in, after)
    return out[:-1], out[-1]


def _plane_wait(name, state, mode, after):
    send_sems, recv_sems = state[0], state[1]
    bufs = list(state[2:])
    n = len(bufs) // 2

    def body(*refs):
        srcs, landing = refs[:n], refs[n:2 * n]
        send_sems, recv_sems = refs[2 * n], refs[2 * n + 1]
        me, peers = _plane_peers()
        for i in range(n):
            for k, (peer, peer_slot) in enumerate(peers):
                src = srcs[i] if mode == "gather" else srcs[i].at[peer_slot]
                cp = pltpu.make_async_remote_copy(
                    src_ref=src, dst_ref=landing[i].at[peer_slot], send_sem=send_sems.at[3 * i + k],
                    recv_sem=recv_sems.at[3 * i + k], device_id=peer, device_id_type=MESH)
                cp.wait_send()
                cp.wait_recv()

    out = pl.pallas_call(
        body, name=name,
        out_shape=[pltpu.HBM(a.shape, a.dtype) for a in bufs],
        in_specs=[HBM_SPEC] * (2 * n) + [SEM_SPEC, SEM_SPEC, pl.BlockSpec(memory_space=pl.ANY)],
        out_specs=[HBM_SPEC] * (2 * n),
        input_output_aliases={i: i for i in range(2 * n)},
        compiler_params=pltpu.CompilerParams(has_side_effects=DATAFLOW),
    )(*bufs, send_sems, recv_sems, after)
    return list(out[n:])


def _tile_2d(rows, cols, row_mult, max_elems=512 * 1024):
    if rows % row_mult == 0:
        rt = _pick_tile(rows, max(row_mult, max_elems // cols), row_mult)
        return (rt, cols), rows // rt, lambda i: (i, 0)
    ct = _pick_tile(cols, max(LANES, max_elems // rows), LANES)
    return (rows, ct), cols // ct, lambda i: (0, i)


def _cast_bf16(a, name):
    block, steps, index = _tile_2d(a.shape[0], a.shape[1], 16)

    def body(a_ref, o_ref):
        o_ref[...] = a_ref[...].astype(BF16)

    return pl.pallas_call(
        body, name=name, grid=(steps,),
        in_specs=[pl.BlockSpec(block, index)],
        out_specs=pl.BlockSpec(block, index),
        out_shape=jax.ShapeDtypeStruct(a.shape, BF16),
        compiler_params=_params(("parallel",)),
    )(a)


def _sum_slots(buf, name, own=None):
    n_slots, rows, cols = buf.shape
    (br, bc), steps, index = _tile_2d(rows, cols, 16, 320 * 1024)
    n_in = 1 if own is None else 2

    def body(*refs):
        b_ref, o_ref = refs[0], refs[-1]
        me = None if own is None else 2 * lax.axis_index("x") + lax.axis_index("y")
        acc = None
        for s in range(n_slots):
            term = b_ref[s] if own is None else jnp.where(me == s, refs[1][s], b_ref[s])
            acc = term.astype(F32) if acc is None else acc + term.astype(F32)
        o_ref[...] = acc

    return pl.pallas_call(
        body, name=name, grid=(steps,),
        in_specs=[pl.BlockSpec((n_slots, br, bc), lambda i: (0,) + index(i))] * n_in,
        out_specs=pl.BlockSpec((br, bc), index),
        out_shape=jax.ShapeDtypeStruct((rows, cols), F32),
        compiler_params=_params(("parallel",), 48),
    )(*([buf] if own is None else [buf, own]))


def _adamw(w, grads, m, v, name, grad_row=0):
    rows, cols = w.shape
    (rt, _), _, _ = _tile_2d(rows, cols, 8, 160 * 1024)
    assert grad_row % rt == 0
    n_g = len(grads)
    c1 = 1.0 - ADAM_B1 ** ADAM_STEP
    c2 = 1.0 - ADAM_B2 ** ADAM_STEP

    def body(*refs):
        w_ref = refs[0]
        g_refs = refs[1:1 + n_g]
        m_ref, v_ref, g_out, d_out, m_out, v_out = refs[1 + n_g:]
        g = g_refs[0][...]
        for r in g_refs[1:]:
            g = g + r[...]
        m_new = ADAM_B1 * m_ref[...] + (1.0 - ADAM_B1) * g
        v_new = ADAM_B2 * v_ref[...] + (1.0 - ADAM_B2) * (g * g)
        m_hat = m_new / c1
        v_hat = v_new / c2
        g_out[...] = g
        d_out[...] = -ADAM_LR * (m_hat / (jnp.sqrt(v_hat) + ADAM_EPS) + ADAM_WD * w_ref[...])
        m_out[...] = m_new
        v_out[...] = v_new

    spec = pl.BlockSpec((rt, cols), lambda i: (i, 0))
    grad_spec = pl.BlockSpec((rt, cols), lambda i: (i + grad_row // rt, 0))
    shape = jax.ShapeDtypeStruct((rows, cols), F32)
    return pl.pallas_call(
        body, name=name, grid=(rows // rt,),
        in_specs=[spec] + [grad_spec] * n_g + [spec] * 2, out_specs=[spec] * 4, out_shape=[shape] * 4,
        compiler_params=_params(("parallel",), 48),
    )(w, *grads, m, v)


def _weight_pieces():
    pieces = []
    for j in range(N_CONV_TILES):
        for g in range(4):
            pieces.append((512 * j + 128 * g, D * g + 128 * j, 128))
    for hd in range(N_HEADS):
        base = W_CONV + 512 * hd
        pieces.append((base, OFF_Q + HEAD_K * hd, HEAD_K))
        pieces.append((base + HEAD_K, OFF_K + HEAD_K * hd, HEAD_K))
        pieces.append((base + 2 * HEAD_K, OFF_V + HEAD_V * hd, HEAD_V))
    pieces.append((W_CONV + W_GLA, OFF_R, D))
    pieces.append((W_CONV + W_GLA + D, OFF_MA, 2 * D))
    return pieces


N_WEIGHT_COPIES = len(_weight_pieces()) + 1


def _load_weights(w_hbm, w_s, wlr_s, sems):
    copies = [pltpu.make_async_copy(w_hbm.at[pl.ds(src, n)], w_s.at[pl.ds(dst, n)], sems.at[i])
              for i, (dst, src, n) in enumerate(_weight_pieces())]
    copies.append(pltpu.make_async_copy(w_hbm.at[pl.ds(OFF_LR, LANES)], wlr_s, sems.at[N_WEIGHT_COPIES - 1]))
    for cp in copies:
        cp.start()
    for cp in copies:
        cp.wait()


def _in_proj(h, g_pre, w_full_t):
    t_rows = h.shape[0]
    tm = _pick_tile(t_rows, 384, LANES)
    n_main = N_MAIN

    def body(h_ref, g_ref, w_hbm, proj_ref, ut_ref, lr_ref, w_s, wlr_s, w_sems):
        @pl.when(pl.program_id(0) == 0)
        def _():
            _load_weights(w_hbm, w_s, wlr_s, w_sems)

        hh = h_ref[...]
        rstd = lax.rsqrt(jnp.mean(hh * hh, axis=-1, keepdims=True) + EPS)
        uf = hh * rstd * g_ref[...]
        u = uf.astype(BF16)
        ut_ref[...] = jnp.transpose(uf).astype(BF16)
        lr_ref[...] = _dot_nt(u, wlr_s[...])
        for j in range(n_main // D):
            cols = slice(j * D, (j + 1) * D)
            proj_ref[:, cols] = _dot_nt(u, w_s[cols, :]).astype(BF16)

    return pl.pallas_call(
        body, name="in_proj", grid=(t_rows // tm,),
        in_specs=[pl.BlockSpec((tm, D), lambda i: (i, 0)),
                  pl.BlockSpec((1, D), lambda i: (0, 0)),
                  pl.BlockSpec(memory_space=pl.ANY)],
        out_specs=[pl.BlockSpec((tm, n_main), lambda i: (i, 0)),
                   pl.BlockSpec((D, tm), lambda i: (0, i)),
                   pl.BlockSpec((tm, LANES), lambda i: (i, 0))],
        out_shape=[jax.ShapeDtypeStruct((t_rows, n_main), BF16),
                   jax.ShapeDtypeStruct((D, t_rows), BF16),
                   jax.ShapeDtypeStruct((t_rows, LANES), F32)],
        scratch_shapes=[pltpu.VMEM((n_main, D), BF16), pltpu.VMEM((LANES, D), BF16),
                        pltpu.SemaphoreType.DMA((N_WEIGHT_COPIES,))],
        compiler_params=_params(("arbitrary",), 56),
    )(h, g_pre, w_full_t)


def _conv_parts(p_ref, w_ref):
    cb = p_ref[:, 0:128].astype(F32)
    cc = p_ref[:, 128:256].astype(F32)
    cx = p_ref[:, 256:384].astype(F32)
    cz = p_ref[:, 384:512].astype(F32)
    rows = cb.shape[0]
    w = w_ref[...]
    p = cc * cx
    conv = pltpu.roll(p, 1, 0) * w[0:1] + p * w[1:2] + pltpu.roll(p, rows - 1, 0) * w[2:3]
    sz = _sigmoid(cz)
    return cb, cc, cx, cz, p, conv, sz, w


def _conv_fwd(proj, conv_w, n_seq, lf):
    def body(p_ref, w_ref, y_ref):
        cb, _, _, cz, _, conv, sz, _ = _conv_parts(p_ref, w_ref)
        y_ref[...] = (cb * conv * (cz * sz)).astype(BF16)

    return pl.pallas_call(
        body, name="conv_fwd", grid=(n_seq, N_CONV_TILES),
        in_specs=[pl.BlockSpec((lf, 512), lambda b, j: (b, j)),
                  pl.BlockSpec((3, 128), lambda b, j: (0, j))],
        out_specs=pl.BlockSpec((lf, 128), lambda b, j: (b, j)),
        out_shape=jax.ShapeDtypeStruct((n_seq * lf, D), BF16),
        compiler_params=_params(("parallel", "parallel"), 48),
    )(proj, conv_w)


def _conv_bwd(proj, conv_w, dyc, n_seq, lf):
    def body(p_ref, w_ref, dy_ref, dp_ref, dw_ref):
        cb, cc, cx, cz, p, conv, sz, w = _conv_parts(p_ref, w_ref)
        rows = cb.shape[0]
        dy = dy_ref[...].astype(F32)
        silu = cz * sz
        dcb = dy * conv * silu
        dconv = dy * cb * silu
        dcz = dy * cb * conv * (sz * (1.0 + cz * (1.0 - sz)))
        d_next = pltpu.roll(dconv, rows - 1, 0)
        d_prev = pltpu.roll(dconv, 1, 0)
        dp = d_next * w[0:1] + dconv * w[1:2] + d_prev * w[2:3]
        dp_ref[:, 0:128] = dcb.astype(BF16)
        dp_ref[:, 128:256] = (dp * cx).astype(BF16)
        dp_ref[:, 256:384] = (dp * cc).astype(BF16)
        dp_ref[:, 384:512] = dcz.astype(BF16)
        dw_ref[0:1, :] = jnp.sum(dconv * pltpu.roll(p, 1, 0), axis=0, keepdims=True)
        dw_ref[1:2, :] = jnp.sum(dconv * p, axis=0, keepdims=True)
        dw_ref[2:3, :] = jnp.sum(dconv * pltpu.roll(p, rows - 1, 0), axis=0, keepdims=True)

    return pl.pallas_call(
        body, name="conv_bwd", grid=(n_seq, N_CONV_TILES),
        in_specs=[pl.BlockSpec((lf, 512), lambda b, j: (b, j)),
                  pl.BlockSpec((3, 128), lambda b, j: (0, j)),
                  pl.BlockSpec((lf, 128), lambda b, j: (b, j))],
        out_specs=[pl.BlockSpec((lf, 512), lambda b, j: (b, j)),
                   pl.BlockSpec((None, 3, 128), lambda b, j: (b, 0, j))],
        out_shape=[jax.ShapeDtypeStruct((n_seq * lf, W_CONV), BF16),
                   jax.ShapeDtypeStruct((n_seq, 3, D), F32)],
        compiler_params=_params(("parallel", "parallel"), 48),
    )(proj, conv_w, dyc)


GROUP = 3
GROUP_ROWS = GROUP * CHUNK


def _row_group(shape):
    row = lax.broadcasted_iota(jnp.int32, shape, 0)
    grp = jnp.zeros(shape, jnp.int32)
    for r in range(1, GROUP):
        grp = grp + (row >= r * CHUNK).astype(jnp.int32)
    return grp


def _lane_group(shape, width):
    lane = lax.broadcasted_iota(jnp.int32, shape, 1)
    grp = jnp.zeros(shape, jnp.int32)
    for r in range(1, GROUP):
        grp = grp + (lane >= r * width).astype(jnp.int32)
    return grp


def _group_masks(direction):
    shape = (GROUP_ROWS, GROUP_ROWS)
    row = lax.broadcasted_iota(jnp.int32, shape, 0)
    col = lax.broadcasted_iota(jnp.int32, shape, 1)
    same = _row_group(shape) == _lane_group(shape, CHUNK)
    lower = same & (col <= row)
    upper = same & (col >= row)
    if direction == 0:
        return lower.astype(BF16), upper.astype(BF16), lower
    return upper.astype(BF16), lower.astype(BF16), same & (col > row)


def _diag_blocks(v):
    w = v.shape[1]
    wide = jnp.concatenate([v] * GROUP, axis=1)
    return jnp.where(_row_group(wide.shape) == _lane_group(wide.shape, w), wide, jnp.zeros_like(wide))


def _pick_diag(wide):
    w = wide.shape[1] // GROUP
    grp = _row_group((GROUP_ROWS, w))
    out = wide[:, 0:w]
    for r in range(1, GROUP):
        out = jnp.where(grp == r, wide[:, r * w:(r + 1) * w], out)
    return out


def _per_chunk_rows(rows_of_chunk):
    w = rows_of_chunk[0].shape[1]
    return jnp.concatenate([jnp.broadcast_to(v, (CHUNK, w)) for v in rows_of_chunk], axis=0)


def _chunk_end_rows(direction, b):
    at = CHUNK - 1 if direction == 0 else 0
    return [b[r * CHUNK + at:r * CHUNK + at + 1, :] for r in range(GROUP)]


def _gla_gates(lr_bf, wg_ref, bg_ref, lf):
    z = _dot(lr_bf, wg_ref[...]) + bg_ref[...]
    valid = lax.broadcasted_iota(jnp.int32, (lf, HEAD_K), 0) >= PAD_FRONT
    return z, valid


def _group_unroll(n_groups):
    return n_groups if n_groups <= 11 else 1


def _group_rows(g):
    return pl.ds(pl.multiple_of(g * GROUP_ROWS, GROUP_ROWS), GROUP_ROWS)


def _chunk_decay(direction, g, r, b_s):
    base = g * GROUP_ROWS + r * CHUNK
    if direction == 0:
        grp = b_s[pl.ds(pl.multiple_of(base + CHUNK - 8, 8), 8), :]
        return jnp.exp(grp[7:8, :])
    grp = b_s[pl.ds(pl.multiple_of(base, 8), 8), :]
    return jnp.exp(grp[0:1, :])


def _state_scan(direction, n_groups, b_s, st_s, reverse):
    ascending = (direction == 0) != reverse

    def step(i, carry):
        g = i if ascending else n_groups - 1 - i
        for rr in range(GROUP):
            r = rr if ascending else GROUP - 1 - rr
            lanes = slice(r * HEAD_K, (r + 1) * HEAD_K)
            decay = _chunk_decay(direction, g, r, b_s)
            local = st_s[g, :, lanes]
            st_s[g, :, lanes] = carry
            carry = (local + carry * decay) if reverse else (carry * decay + local)
        return carry

    lax.fori_loop(0, n_groups, step, jnp.zeros((HEAD_V, HEAD_K), F32))


def _gla_states(direction, n_groups, qkv_ref, g_s, b_s, st_s, tri):
    def local(g, carry):
        rows = _group_rows(g)
        b = _tri_dot(tri, g_s[rows, :])
        b_s[rows, :] = b
        b_end = _per_chunk_rows(_chunk_end_rows(direction, b))
        k = qkv_ref[rows, 128:256].astype(F32)
        v = qkv_ref[rows, 256:512]
        k_dec = (k * jnp.exp(b_end - b)).astype(BF16)
        st_s[g] = _dot_tn(v, _diag_blocks(k_dec))
        return carry

    lax.fori_loop(0, n_groups, local, 0, unroll=_group_unroll(n_groups))
    _state_scan(direction, n_groups, b_s, st_s, False)


def _gla_fwd(proj, lr, wgf, wgb, bgf, bgb, n_seq, lf):
    assert lf % GROUP_ROWS == 0
    n_groups = lf // GROUP_ROWS
    scale = HEAD_K ** -0.5

    def body(qkv_ref, lr_ref, wgf_ref, wgb_ref, bgf_ref, bgb_ref, o_ref, g_s, b_s, st_s):
        lr_bf = lr_ref[...].astype(BF16)
        for direction in (0, 1):
            wg_ref, bg_ref = ((wgf_ref, bgf_ref), (wgb_ref, bgb_ref))[direction]
            z, valid = _gla_gates(lr_bf, wg_ref, bg_ref, lf)
            g_s[...] = jnp.where(valid, _log_sigmoid(z) / GATE_NORM, 0.0)
            tri, _, smask = _group_masks(direction)
            _gla_states(direction, n_groups, qkv_ref, g_s, b_s, st_s, tri)

            def out(g, carry):
                rows = _group_rows(g)
                b = b_s[rows, :]
                q = qkv_ref[rows, 0:128].astype(F32) * scale
                k = qkv_ref[rows, 128:256].astype(F32)
                v = qkv_ref[rows, 256:512]
                q_in = (q * jnp.exp(b)).astype(BF16)
                k_in = (k * jnp.exp(-b)).astype(BF16)
                s = jnp.where(smask, _dot_nt(q_in, k_in), 0.0).astype(BF16)
                o = _dot(s, v) + _dot_nt(_diag_blocks(q_in), st_s[g].astype(BF16))
                if direction == 0:
                    o_ref[rows, :] = o
                else:
                    o_ref[rows, :] = o_ref[rows, :] + o
                return carry

            lax.fori_loop(0, n_groups, out, 0, unroll=_group_unroll(n_groups))

    return pl.pallas_call(
        body, name="gla_fwd", grid=(n_seq, N_HEADS),
        in_specs=[pl.BlockSpec((lf, 512), lambda b, h: (b, N_CONV_TILES + h)),
                  pl.BlockSpec((lf, LANES), lambda b, h: (b, 0)),
                  pl.BlockSpec((None, LANES, HEAD_K), lambda b, h: (h, 0, 0)),
                  pl.BlockSpec((None, LANES, HEAD_K), lambda b, h: (h, 0, 0)),
                  pl.BlockSpec((None, 1, HEAD_K), lambda b, h: (h, 0, 0)),
                  pl.BlockSpec((None, 1, HEAD_K), lambda b, h: (h, 0, 0))],
        out_specs=pl.BlockSpec((lf, HEAD_V), lambda b, h: (b, h)),
        out_shape=jax.ShapeDtypeStruct((n_seq * lf, D), F32),
        scratch_shapes=[pltpu.VMEM((lf, HEAD_K), F32), pltpu.VMEM((lf, HEAD_K), F32),
                        pltpu.VMEM((n_groups, HEAD_V, GROUP * HEAD_K), F32)],
        compiler_params=_params(("parallel", "parallel"), 48),
    )(proj, lr, wgf, wgb, bgf, bgb)


def _gla_bwd(proj, lr, d_o, wgf, wgb, bgf, bgb, n_seq, lf, token):
    assert lf % GROUP_ROWS == 0
    n_groups = lf // GROUP_ROWS
    scale = HEAD_K ** -0.5

    def body(qkv_ref, lr_ref, do_ref, wgf_ref, wgb_ref, bgf_ref, bgb_ref, token_ref,
             dqkv_ref, dlr_ref, dwgf_ref, dwgb_ref, dbg_ref,
             g_s, b_s, fac_s, dg_s, st_s, dst_s, acc_s):
        lr_bf = lr_ref[...].astype(BF16)
        dlr = jnp.zeros((lf, LANES), F32)
        for direction in (0, 1):
            wg_ref, bg_ref = ((wgf_ref, bgf_ref), (wgb_ref, bgb_ref))[direction]
            z, valid = _gla_gates(lr_bf, wg_ref, bg_ref, lf)
            g_s[...] = jnp.where(valid, _log_sigmoid(z) / GATE_NORM, 0.0)
            fac_s[...] = jnp.where(valid, _sigmoid(-z) / GATE_NORM, 0.0)
            tri, tri_t, smask = _group_masks(direction)
            end_row = CHUNK - 1 if direction == 0 else 0
            _gla_states(direction, n_groups, qkv_ref, g_s, b_s, st_s, tri)

            def state_grad_local(g, carry):
                rows = _group_rows(g)
                q = qkv_ref[rows, 0:128].astype(F32) * scale
                q_in = (q * jnp.exp(b_s[rows, :])).astype(BF16)
                dst_s[g] = _dot_tn(do_ref[rows, :], _diag_blocks(q_in))
                return carry

            lax.fori_loop(0, n_groups, state_grad_local, 0, unroll=_group_unroll(n_groups))
            _state_scan(direction, n_groups, b_s, dst_s, True)

            def group_grads(g, carry):
                rows = _group_rows(g)
                b = b_s[rows, :]
                ends = _chunk_end_rows(direction, b)
                b_end = _per_chunk_rows(ends)
                q = qkv_ref[rows, 0:128].astype(F32) * scale
                k = qkv_ref[rows, 128:256].astype(F32)
                v = qkv_ref[rows, 256:512]
                d_out = do_ref[rows, :]
                e_pos = jnp.exp(b)
                e_neg = jnp.exp(-b)
                e_end = jnp.exp(b_end - b)
                q_in = q * e_pos
                k_in = k * e_neg
                k_dec = k * e_end
                q_in_bf = q_in.astype(BF16)
                k_in_bf = k_in.astype(BF16)
                state = st_s[g]
                d_state = dst_s[g]
                state_bf = state.astype(BF16)
                d_state_bf = d_state.astype(BF16)
                s = jnp.where(smask, _dot_nt(q_in_bf, k_in_bf), 0.0).astype(BF16)
                ds = jnp.where(smask, _dot_nt(d_out, v), 0.0).astype(BF16)
                dv = _dot_tn(s, d_out) + _dot_nt(_diag_blocks(k_dec.astype(BF16)), d_state_bf)
                dq_in = _dot(ds, k_in_bf) + _pick_diag(_dot(d_out, state_bf))
                dk_in = _dot_tn(ds, q_in_bf)
                dk_dec = _pick_diag(_dot(v, d_state_bf))
                dq = dq_in * e_pos * scale
                dk = dk_in * e_neg + dk_dec * e_end
                if direction == 0:
                    acc_s[rows, 0:128] = dq
                    acc_s[rows, 128:256] = dk
                    acc_s[rows, 256:512] = dv
                else:
                    dqkv_ref[rows, 0:128] = (acc_s[rows, 0:128] + dq).astype(BF16)
                    dqkv_ref[rows, 128:256] = (acc_s[rows, 128:256] + dk).astype(BF16)
                    dqkv_ref[rows, 256:512] = (acc_s[rows, 256:512] + dv).astype(BF16)
                dkk = dk_dec * k_dec
                db = dq_in * q_in - dk_in * k_in - dkk
                d_decay = jnp.sum(d_state * state, axis=0, keepdims=True)
                db_end = [jnp.sum(dkk[r * CHUNK:(r + 1) * CHUNK, :], axis=0, keepdims=True)
                          + d_decay[:, r * HEAD_K:(r + 1) * HEAD_K] * jnp.exp(ends[r]) for r in range(GROUP)]
                row = lax.broadcasted_iota(jnp.int32, (GROUP_ROWS, HEAD_K), 0)
                at_end = row == end_row
                for r in range(1, GROUP):
                    at_end = at_end | (row == r * CHUNK + end_row)
                db = db + jnp.where(at_end, _per_chunk_rows(db_end), 0.0)
                dg_s[rows, :] = _tri_dot(tri_t, db)
                return carry

            lax.fori_loop(0, n_groups, group_grads, 0, unroll=_group_unroll(n_groups))

            dz = dg_s[...] * fac_s[...]
            dz_bf = dz.astype(BF16)
            dbg_ref[direction:direction + 1, :] = jnp.sum(dz, axis=0, keepdims=True)
            (dwgf_ref, dwgb_ref)[direction][...] = _dot_tn(lr_bf, dz_bf)
            dlr = dlr + _dot_nt(dz_bf, wg_ref[...])

        @pl.when(pl.program_id(1) == 0)
        def _():
            dlr_ref[...] = dlr

        @pl.when(pl.program_id(1) != 0)
        def _():
            dlr_ref[...] = dlr_ref[...] + dlr

    gate_w = pl.BlockSpec((None, LANES, HEAD_K), lambda b, h: (h, 0, 0))
    gate_b = pl.BlockSpec((None, 1, HEAD_K), lambda b, h: (h, 0, 0))
    return pl.pallas_call(
        body, name="gla_bwd", grid=(n_seq, N_HEADS),
        in_specs=[pl.BlockSpec((lf, 512), lambda b, h: (b, N_CONV_TILES + h)),
                  pl.BlockSpec((lf, LANES), lambda b, h: (b, 0)),
                  pl.BlockSpec((lf, HEAD_V), lambda b, h: (b, h)),
                  gate_w, gate_w, gate_b, gate_b,
                  pl.BlockSpec((8, LANES), lambda b, h: (0, 0))],
        out_specs=[pl.BlockSpec((lf, 512), lambda b, h: (b, h)),
                   pl.BlockSpec((lf, LANES), lambda b, h: (b, 0)),
                   pl.BlockSpec((None, None, LANES, HEAD_K), lambda b, h: (b, h, 0, 0)),
                   pl.BlockSpec((None, None, LANES, HEAD_K), lambda b, h: (b, h, 0, 0)),
                   pl.BlockSpec((None, None, 2, HEAD_K), lambda b, h: (b, h, 0, 0))],
        out_shape=[jax.ShapeDtypeStruct((n_seq * lf, W_GLA), BF16),
                   jax.ShapeDtypeStruct((n_seq * lf, LANES), F32),
                   jax.ShapeDtypeStruct((n_seq, N_HEADS, LANES, HEAD_K), F32),
                   jax.ShapeDtypeStruct((n_seq, N_HEADS, LANES, HEAD_K), F32),
                   jax.ShapeDtypeStruct((n_seq, N_HEADS, 2, HEAD_K), F32)],
        scratch_shapes=[pltpu.VMEM((lf, HEAD_K), F32), pltpu.VMEM((lf, HEAD_K), F32),
                        pltpu.VMEM((lf, HEAD_K), F32), pltpu.VMEM((lf, HEAD_K), F32),
                        pltpu.VMEM((n_groups, HEAD_V, GROUP * HEAD_K), F32),
                        pltpu.VMEM((n_groups, HEAD_V, GROUP * HEAD_K), F32),
                        pltpu.VMEM((lf, 512), F32)],
        compiler_params=_params(("parallel", "arbitrary"), 56),
    )(proj, lr, d_o, wgf, wgb, bgf, bgb, token)


def _tail(h, tgt, yc, o, proj, w3, gamma, g_post, lf):
    t_rows = h.shape[0]
    tm = _pick_tile(t_rows, 256, CHUNK)
    n_chunks = lf // CHUNK
    per_tile = tm // CHUNK

    def body(h_ref, tgt_ref, yc_ref, o_ref, r_ref, ma_ref, mb_ref, w_hbm, gamma_ref, gpost_ref,
             dres_ref, yg_ref, merged_ref, dout_ref, dpc_ref, dpg_ref, dyc_ref, do_ref, dtail_ref,
             loss_ref, dgpost_ref, dgamma_ref, w_s, w_sem):
        i = pl.program_id(0)

        @pl.when(i == 0)
        def _():
            cp = pltpu.make_async_copy(w_hbm, w_s, w_sem)
            cp.start()
            cp.wait()
            loss_ref[...] = jnp.zeros_like(loss_ref)
            dgpost_ref[...] = jnp.zeros_like(dgpost_ref)
            dgamma_ref[...] = jnp.zeros_like(dgamma_ref)

        gamma = gamma_ref[...]
        o = o_ref[...]
        r = r_ref[...].astype(F32)
        sr = _sigmoid(r)
        silu_r = r * sr
        n_parts, rstd_parts = [], []
        for hd in range(N_HEADS):
            oh = o[:, hd * HEAD_V:(hd + 1) * HEAD_V]
            rstd = lax.rsqrt(jnp.mean(oh * oh, axis=-1, keepdims=True) + EPS)
            n_parts.append(oh * rstd)
            rstd_parts.append(rstd)
        n = jnp.concatenate(n_parts, axis=-1)
        gamma_t = jnp.concatenate([gamma] * N_HEADS, axis=-1)
        yg = n * gamma_t * silu_r
        yg_bf = yg.astype(BF16)
        yg_ref[...] = yg_bf
        yc = yc_ref[...]
        pc = _dot(yc, w_s[0])
        pg = _dot(yg_bf, w_s[1])
        sa = _sigmoid(ma_ref[...].astype(F32))
        sb = _sigmoid(mb_ref[...].astype(F32))
        merged = (sa * pc + sb * pg).astype(BF16)
        merged_ref[...] = merged
        out = _dot(merged, w_s[2])
        rstd2 = lax.rsqrt(jnp.mean(out * out, axis=-1, keepdims=True) + EPS)
        nn = out * rstd2
        gpost = gpost_ref[...]
        y = h_ref[...] + nn * gpost

        rowi = lax.broadcasted_iota(jnp.int32, (tm, 1), 0)
        keep = jnp.zeros((tm, 1), F32)
        for kk in range(per_tile):
            is_tok = ((i * per_tile + kk) % n_chunks) != 0
            f = jnp.where(is_tok, 1.0, 0.0)
            keep = jnp.where((rowi >= kk * CHUNK) & (rowi < (kk + 1) * CHUNK), f, keep)
        diff = (y - tgt_ref[...]) * keep
        loss_ref[...] += jnp.sum(diff * diff) * (0.5 / D)
        dy = diff * (1.0 / D)
        dres_ref[...] = dy
        dgpost_ref[...] += jnp.sum(dy * nn, axis=0, keepdims=True)
        dn = dy * gpost
        dout_f = rstd2 * (dn - nn * jnp.mean(dn * nn, axis=-1, keepdims=True))
        dout = dout_f.astype(BF16)
        dout_ref[...] = jnp.transpose(dout_f).astype(BF16)
        dmerged = _dot_nt(dout, w_s[2])
        dpc_f = dmerged * sa
        dpg_f = dmerged * sb
        dpc = dpc_f.astype(BF16)
        dpg = dpg_f.astype(BF16)
        dpc_ref[...] = jnp.transpose(dpc_f).astype(BF16)
        dpg_ref[...] = jnp.transpose(dpg_f).astype(BF16)
        dtail_ref[:, D:2 * D] = (dmerged * pc * (sa * (1.0 - sa))).astype(BF16)
        dtail_ref[:, 2 * D:3 * D] = (dmerged * pg * (sb * (1.0 - sb))).astype(BF16)
        dyc_ref[...] = _dot_nt(dpc, w_s[0]).astype(BF16)
        dyg = _dot_nt(dpg, w_s[1])
        dtail_ref[:, 0:D] = (dyg * n * gamma_t * (sr * (1.0 + r * (1.0 - sr)))).astype(BF16)
        dgam_full = jnp.sum(dyg * n * silu_r, axis=0, keepdims=True)
        dgam = dgam_full[:, 0:HEAD_V]
        for hd in range(1, N_HEADS):
            dgam = dgam + dgam_full[:, hd * HEAD_V:(hd + 1) * HEAD_V]
        dgamma_ref[...] += dgam
        dng = dyg * gamma_t * silu_r
        do_parts = []
        for hd in range(N_HEADS):
            sl = slice(hd * HEAD_V, (hd + 1) * HEAD_V)
            dnh = dng[:, sl]
            nh = n_parts[hd]
            do_parts.append(rstd_parts[hd] * (dnh - nh * jnp.mean(dnh * nh, axis=-1, keepdims=True)))
        do_ref[...] = jnp.concatenate(do_parts, axis=-1).astype(BF16)

    row = lambda c: pl.BlockSpec((tm, D), lambda i: (i, c))
    col = pl.BlockSpec((D, tm), lambda i: (0, i))
    const = lambda shape: pl.BlockSpec(shape, lambda i: (0, 0))
    act = jax.ShapeDtypeStruct((t_rows, D), BF16)
    act_t = jax.ShapeDtypeStruct((D, t_rows), BF16)
    return pl.pallas_call(
        body, name="tail", grid=(t_rows // tm,),
        in_specs=[row(0), row(0), row(0), row(0), row(6), row(7), row(8),
                  pl.BlockSpec(memory_space=pl.ANY), const((1, HEAD_V)), const((1, D))],
        out_specs=[row(0)] * 3 + [col] * 3 + [row(0)] * 2
                  + [pl.BlockSpec((tm, W_TAIL), lambda i: (i, 0)),
                     const((8, LANES)), const((1, D)), const((1, HEAD_V))],
        out_shape=[jax.ShapeDtypeStruct((t_rows, D), F32)] + [act] * 2 + [act_t] * 3 + [act] * 2
                  + [jax.ShapeDtypeStruct((t_rows, W_TAIL), BF16),
                     jax.ShapeDtypeStruct((8, LANES), F32),
                     jax.ShapeDtypeStruct((1, D), F32),
                     jax.ShapeDtypeStruct((1, HEAD_V), F32)],
        scratch_shapes=[pltpu.VMEM((3, D, D), BF16), pltpu.SemaphoreType.DMA],
        compiler_params=_params(("arbitrary",), 56),
    )(h, tgt, yc, o, proj, proj, proj, w3, gamma, g_post)


def _wgrad_t(a_t, b, name, out_dtype=BF16):
    m, t_rows = a_t.shape
    n = b.shape[1]
    tn = D if n % D == 0 else n
    tk = _pick_tile(t_rows, 768, LANES)
    n_k = t_rows // tk

    def body(a_ref, b_ref, o_ref, acc):
        k = pl.program_id(1)

        @pl.when(k == 0)
        def _():
            acc[...] = jnp.zeros_like(acc)

        acc[...] += _dot(a_ref[...], b_ref[...].astype(BF16))

        @pl.when(k == n_k - 1)
        def _():
            o_ref[...] = jnp.transpose(acc[...]).astype(out_dtype)

    return pl.pallas_call(
        body, name=name, grid=(n // tn, n_k),
        in_specs=[pl.BlockSpec((m, tk), lambda j, k: (0, k)),
                  pl.BlockSpec((tk, tn), lambda j, k: (k, j))],
        out_specs=pl.BlockSpec((tn, m), lambda j, k: (j, 0)),
        out_shape=jax.ShapeDtypeStruct((n, m), out_dtype),
        scratch_shapes=[pltpu.VMEM((m, tn), F32)],
        compiler_params=_params(("parallel", "arbitrary"), 48),
    )(a_t, b)


def _dgrad_in(dpc, dpg, dpt, dlr, w_full_t, h, g_pre, dres, token):
    t_rows = h.shape[0]
    tm = _pick_tile(t_rows, 256, 16)
    n_main = N_MAIN

    def body(dpc_ref, dpg_ref, dpt_ref, dlr_ref, w_hbm, h_ref, g_ref, dres_ref, token_ref,
             dh_ref, dg_ref, w_s, wlr_s, w_sems):
        @pl.when(pl.program_id(0) == 0)
        def _():
            _load_weights(w_hbm, w_s, wlr_s, w_sems)
            dg_ref[...] = jnp.zeros_like(dg_ref)

        du = _dot(dlr_ref[...].astype(BF16), wlr_s[...])
        du += _dot(dpc_ref[...], w_s[0:W_CONV, :])
        du += _dot(dpg_ref[...], w_s[W_CONV:W_CONV + W_GLA, :])
        du += _dot(dpt_ref[...], w_s[W_CONV + W_GLA:n_main, :])
        hh = h_ref[...]
        rstd = lax.rsqrt(jnp.mean(hh * hh, axis=-1, keepdims=True) + EPS)
        xhat = hh * rstd
        dg_ref[...] += jnp.sum(du * xhat, axis=0, keepdims=True)
        dx = du * g_ref[...]
        dh_ref[...] = rstd * (dx - xhat * jnp.mean(dx * xhat, axis=-1, keepdims=True)) + dres_ref[...]

    row = lambda width: pl.BlockSpec((tm, width), lambda i: (i, 0))
    return pl.pallas_call(
        body, name="dgrad_in", grid=(t_rows // tm,),
        in_specs=[row(W_CONV), row(W_GLA), row(W_TAIL), row(LANES),
                  pl.BlockSpec(memory_space=pl.ANY),
                  row(D), pl.BlockSpec((1, D), lambda i: (0, 0)), row(D),
                  pl.BlockSpec((8, LANES), lambda i: (0, 0))],
        out_specs=[row(D), pl.BlockSpec((1, D), lambda i: (0, 0))],
        out_shape=[jax.ShapeDtypeStruct((t_rows, D), F32), jax.ShapeDtypeStruct((1, D), F32)],
        scratch_shapes=[pltpu.VMEM((n_main, D), BF16), pltpu.VMEM((LANES, D), BF16),
                        pltpu.SemaphoreType.DMA((N_WEIGHT_COPIES,))],
        compiler_params=_params(("arbitrary",), 56),
    )(dpc, dpg, dpt, dlr, w_full_t, h, g_pre, dres, token)


def _reference_rows(g_conv, g_gla, g_tail, g_lr):
    conv = g_conv.reshape(N_CONV_TILES, 4, 128, D).transpose(1, 0, 2, 3).reshape(W_CONV, D)
    gla = g_gla.reshape(N_HEADS, 512, D)
    q = gla[:, 0:128].reshape(N_HEADS * HEAD_K, D)
    k = gla[:, 128:256].reshape(N_HEADS * HEAD_K, D)
    v = gla[:, 256:512].reshape(N_HEADS * HEAD_V, D)
    return jnp.concatenate([conv, q, k, v, g_tail[0:D], g_lr[0:2 * RANK], g_tail[D:3 * D]], axis=0)


def _pack(arrs, rows):
    flat = jnp.concatenate([a.reshape(-1) for a in arrs])
    return jnp.pad(flat, (0, rows * LANES - flat.shape[0])).reshape(rows, LANES)


def _unpack(packed, shapes):
    flat = packed.reshape(-1)
    out, pos = [], 0
    for s in shapes:
        size = 1
        for d in s:
            size *= d
        out.append(flat[pos:pos + size].reshape(s))
        pos += size
    return out


def _rows_for(shapes, mult=8):
    total = 0
    for s in shapes:
        size = 1
        for d in s:
            size *= d
        total += size
    return -(-total // (mult * LANES)) * mult


def kernel(x, meta_tokens, norm_pre, w_in, conv_w, w_gate_fwd, b_gate_fwd, w_gate_bwd, b_gate_bwd, gla_norm, w_out_conv, w_out_gla, w_merge_out, norm_post, loss_target, m_meta_tokens, m_norm_pre, m_w_in, m_conv_w, m_w_gate_fwd, m_b_gate_fwd, m_w_gate_bwd, m_b_gate_bwd, m_gla_norm, m_w_out_conv, m_w_out_gla, m_w_merge_out, m_norm_post, v_meta_tokens, v_norm_pre, v_w_in, v_conv_w, v_w_gate_fwd, v_b_gate_fwd, v_w_gate_bwd, v_b_gate_bwd, v_gla_norm, v_w_out_conv, v_w_out_gla, v_w_merge_out, v_norm_post):
    n_seq, seq, _ = x.shape
    lf = CHUNK + seq
    t_rows = n_seq * lf
    shard = 2 * lax.axis_index("x") + lax.axis_index("y")

    w_in_bf = _cast_bf16(jnp.transpose(w_in[0]), "cast_w_in")
    w_out_bf = _cast_bf16(jnp.concatenate([w_out_conv[0], w_out_gla[0], w_merge_out[0]], axis=0), "cast_w_out")
    small_shapes = [(N_META, D // 4), (3, D // 4), (RANK, HEAD_K), (RANK, HEAD_K)]
    small = _pack([meta_tokens, conv_w[0], w_gate_fwd[0], w_gate_bwd[0]], _rows_for(small_shapes, 32))
    w_in_all, small_all = _gather_via_sibling("gather_w_in", [w_in_bf, small])
    w_out_state, _ = _plane_start("gather_w_out_start", [w_out_bf], "gather", small_all)

    w_full_t = w_in_all.reshape(N_IN, D)
    smalls = [_unpack(small_all[s], small_shapes) for s in range(4)]
    meta_full = jnp.concatenate([smalls[s][0] for s in range(4)], axis=1)
    conv_full = jnp.concatenate([smalls[s][1] for s in range(4)], axis=1)
    wgf = jnp.stack([jnp.pad(smalls[s][2], ((0, LANES - RANK), (0, 0))) for s in range(4)]).astype(BF16)
    wgb = jnp.stack([jnp.pad(smalls[s][3], ((RANK, LANES - 2 * RANK), (0, 0))) for s in range(4)]).astype(BF16)
    bgf = b_gate_fwd.reshape(N_HEADS, 1, HEAD_K)
    bgb = b_gate_bwd.reshape(N_HEADS, 1, HEAD_K)

    head = jnp.concatenate([jnp.zeros((PAD_FRONT, D), F32), meta_full], axis=0)
    h = jnp.concatenate([jnp.broadcast_to(head[None], (n_seq, CHUNK, D)), x], axis=1).reshape(t_rows, D)
    tgt = jnp.pad(loss_target, ((0, 0), (CHUNK, 0), (0, 0))).reshape(t_rows, D)

    proj, u_t, lr = _in_proj(h, norm_pre, w_full_t)
    yc = _conv_fwd(proj, conv_full, n_seq, lf)
    o = _gla_fwd(proj, lr, wgf, wgb, bgf, bgb, n_seq, lf)
    (w_out_landed,) = _plane_wait("gather_w_out_wait", w_out_state, "gather", o)
    slot_ids = lax.broadcasted_iota(jnp.int32, (4, 1, 1), 0)
    w_out_all = jnp.where(slot_ids == shard, w_out_bf[None], w_out_landed)
    w3 = jnp.transpose(w_out_all.reshape(4, 3, D // 4, D), (1, 0, 2, 3)).reshape(3, D, D)
    (dres, yg, merged, dout_t, dpc_t, dpg_t, dyc, d_o, dtail, loss_acc, d_gpost, d_gamma) = _tail(
        h, tgt, yc, o, proj, w3, gla_norm, norm_post, lf)
    g_w_oc = _wgrad_t(dpc_t, yc, "wgrad_out_conv")
    g_w_og = _wgrad_t(dpg_t, yg, "wgrad_out_gla")
    g_w_mo = _wgrad_t(dout_t, merged, "wgrad_merge_out")
    g_out_slots = jnp.concatenate([g.reshape(4, D // 4, D) for g in (g_w_oc, g_w_og, g_w_mo)], axis=1)
    out_state, out_token = _plane_start("scatter_out_grads_start", [g_out_slots], "scatter", g_w_mo)
    dgla, dlr, dwgf_p, dwgb_p, dbg_p = _gla_bwd(proj, lr, d_o, wgf, wgb, bgf, bgb, n_seq, lf, out_token)
    (got_out,) = _plane_wait("scatter_out_grads_wait", out_state, "scatter", dlr)
    dconv, dconvw_p = _conv_bwd(proj, conv_full, dyc, n_seq, lf)
    g_conv = _wgrad_t(u_t, dconv, "wgrad_in_conv")
    g_gla = _wgrad_t(u_t, dgla, "wgrad_in_gla")
    g_tail = _wgrad_t(u_t, dtail, "wgrad_in_tail")
    g_lr = _wgrad_t(u_t, dlr, "wgrad_in_lr")

    g_in_slots = _reference_rows(g_conv, g_gla, g_tail, g_lr).reshape(4, SHARD_IN, D)
    in_state, in_token = _plane_start("scatter_in_grads_start", [g_in_slots], "scatter", g_lr)
    dh, d_gpre = _dgrad_in(dconv, dgla, dtail, dlr, w_full_t, h, norm_pre, dres, in_token)
    (got_in,) = _plane_wait("scatter_in_grads_wait", in_state, "scatter", d_gpre)

    dh3 = dh.reshape(n_seq, lf, D)
    grad_x = dh3[:, CHUNK:, :]

    d_meta = jnp.sum(dh3[:, PAD_FRONT:CHUNK, :], axis=0)
    d_convw = jnp.sum(dconvw_p, axis=0)
    d_wgf = jnp.transpose(jnp.sum(dwgf_p, axis=0)[:, 0:RANK, :], (1, 0, 2)).reshape(RANK, N_HEADS * HEAD_K)
    d_wgb = jnp.transpose(jnp.sum(dwgb_p, axis=0)[:, RANK:2 * RANK, :], (1, 0, 2)).reshape(RANK, N_HEADS * HEAD_K)
    d_bg = jnp.sum(dbg_p, axis=0)
    d_bgf = d_bg[:, 0, :].reshape(1, N_HEADS * HEAD_K)
    d_bgb = d_bg[:, 1, :].reshape(1, N_HEADS * HEAD_K)
    part_shapes = [(N_META, D), (3, D), (RANK, 512), (RANK, 512), (1, D), (1, 512), (1, 512), (1, HEAD_V),
                   (1, D), (1, LANES)]
    parts = _pack([d_meta, d_convw, d_wgf, d_wgb, d_gpre, d_bgf, d_bgb, d_gamma, d_gpost, loss_acc[0:1, :]],
                  _rows_for(part_shapes))
    (parts_all,) = _exchange("gather_small_grads", [parts], ALL_FLIPS, (4, 2, 1), "gather")
    (g_meta, g_convw, g_wgf, g_wgb, g_npre, g_bgf, g_bgb, g_gnorm, g_npost, loss_row) = _unpack(
        _sum_slots(parts_all, "sum_small_grads"), part_shapes)
    loss = loss_row[0, 0]

    def col_shard(a, width):
        return lax.dynamic_slice_in_dim(a, shard * width, width, axis=a.ndim - 1)

    upd_shapes = [(N_META, D // 4), (3, D // 4), (RANK, HEAD_K), (RANK, HEAD_K), (1, D), (1, 512), (1, 512),
                  (1, HEAD_V), (1, D)]
    upd_rows = _rows_for(upd_shapes)
    small_w = _pack([meta_tokens, conv_w[0], w_gate_fwd[0], w_gate_bwd[0], norm_pre, b_gate_fwd, b_gate_bwd,
                     gla_norm, norm_post], upd_rows)
    small_g = _pack([col_shard(g_meta, D // 4), col_shard(g_convw, D // 4), col_shard(g_wgf, HEAD_K),
                     col_shard(g_wgb, HEAD_K), g_npre, g_bgf, g_bgb, g_gnorm, g_npost], upd_rows)
    small_m = _pack([m_meta_tokens, m_conv_w[0], m_w_gate_fwd[0], m_w_gate_bwd[0], m_norm_pre, m_b_gate_fwd,
                     m_b_gate_bwd, m_gla_norm, m_norm_post], upd_rows)
    small_v = _pack([v_meta_tokens, v_conv_w[0], v_w_gate_fwd[0], v_w_gate_bwd[0], v_norm_pre, v_b_gate_fwd,
                     v_b_gate_bwd, v_gla_norm, v_norm_post], upd_rows)
    small_out = [_unpack(a, upd_shapes) for a in _adamw(small_w, [small_g], small_m, small_v, "adamw_small")]

    plane_in = _sum_slots(got_in, "sum_w_in_grads", own=g_in_slots)
    plane_out = _sum_slots(got_out, "sum_w_out_grads", own=g_out_slots)
    other_in, other_out = _exchange("swap_plane_sums", [plane_in, plane_out], SIBLING_FLIPS, (0, 0, 0), "swap")
    big_in = _adamw(jnp.transpose(w_in[0]), [plane_in, other_in], jnp.transpose(m_w_in[0]), jnp.transpose(v_w_in[0]),
                    "adamw_w_in")
    out_params = ((w_out_conv, m_w_out_conv, v_w_out_conv), (w_out_gla, m_w_out_gla, v_w_out_gla),
                  (w_merge_out, m_w_merge_out, v_w_merge_out))
    big_out = [_adamw(w[0], [plane_out, other_out], m[0], v[0], f"adamw_w_out_{i}", grad_row=i * (D // 4))
               for i, (w, m, v) in enumerate(out_params)]

    results = []
    for kind in range(4):
        sm = small_out[kind]
        w_in_part = jnp.transpose(big_in[kind])[None]
        outs3 = [big_out[i][kind][None] for i in range(3)]
        results.extend([
            sm[0], sm[4], w_in_part, sm[1][None], sm[2][None], sm[5], sm[3][None], sm[6], sm[7],
            outs3[0], outs3[1], outs3[2], sm[8]])
    return (loss, grad_x, *results)
```

```python
import functools

import jax
import jax.numpy as jnp
from jax import lax
from jax.experimental import pallas as pl
from jax.experimental.pallas import tpu as pltpu

F32 = jnp.float32
BF16 = jnp.bfloat16
MESH = pl.DeviceIdType.MESH

D = 1024
N_META = 16
CHUNK = 64
PAD_FRONT = CHUNK - N_META
N_HEADS = 4
HEAD_K = 128
HEAD_V = 256
RANK = 16
EPS = 1e-6
GATE_NORM = 16.0
N_IN = 9248
SHARD_IN = N_IN // 4
LANES = 128
N_CONV_TILES = 8
W_CONV = 4096
W_GLA = 2048
W_TAIL = 3072
N_MAIN = W_CONV + W_GLA + W_TAIL
OFF_Q, OFF_K, OFF_V, OFF_R = 4096, 4608, 5120, 6144
OFF_LR, OFF_MA, OFF_MB = 7168, 7200, 8224
MIB = 1024 * 1024

ADAM_LR = 0.001
ADAM_B1 = 0.9
ADAM_B2 = 0.999
ADAM_EPS = 1e-08
ADAM_WD = 0.01
ADAM_STEP = 10


def _params(sem=None, vmem_mib=None):
    return pltpu.CompilerParams(
        dimension_semantics=sem,
        vmem_limit_bytes=None if vmem_mib is None else vmem_mib * MIB)


def _pick_tile(n, target, mult):
    best = None
    for t in range(mult, min(n, target) + 1, mult):
        if n % t == 0:
            best = t
    return n if best is None else best


def _sigmoid(v):
    return 1.0 / (1.0 + jnp.exp(-v))


def _log_sigmoid(v):
    return jnp.minimum(v, 0.0) - jnp.log(1.0 + jnp.exp(-jnp.abs(v)))


def _dot(a, b):
    return jnp.dot(a, b, preferred_element_type=F32)


def _dot_nt(a, b):
    return lax.dot_general(a, b, (((1,), (1,)), ((), ())), preferred_element_type=F32)


def _dot_tn(a, b):
    return lax.dot_general(a, b, (((0,), (0,)), ((), ())), preferred_element_type=F32)


def _tri_dot(tri, v):
    hi = v.astype(BF16)
    lo = (v - hi.astype(F32)).astype(BF16)
    return _dot(tri, hi) + _dot(tri, lo)


PLANE_FLIPS = ((1, 0, 0), (0, 1, 0), (1, 1, 0))
ALL_FLIPS = tuple((m >> 2 & 1, m >> 1 & 1, m & 1) for m in range(1, 8))
SIBLING_FLIPS = ((0, 0, 1),)


def _exchange(name, arrs, flips, slot_weights, mode):
    n = len(arrs)
    n_slots = 1
    for w in slot_weights:
        n_slots += w
    if mode == "gather":
        out_shape = [jax.ShapeDtypeStruct((n_slots,) + a.shape, a.dtype) for a in arrs]
    else:
        out_shape = [jax.ShapeDtypeStruct(a.shape, a.dtype) for a in arrs]

    def body(*refs):
        ins, outs = refs[:n], refs[n:2 * n]
        send_sems, recv_sems, local_sems = refs[2 * n:]
        pos = (lax.axis_index("x"), lax.axis_index("y"), lax.axis_index("c"))

        def slot_of(p):
            return p[0] * slot_weights[0] + p[1] * slot_weights[1] + p[2] * slot_weights[2]

        peers = [tuple(1 - pos[a] if f[a] else pos[a] for a in range(3)) for f in flips]
        me = slot_of(pos)
        local = []
        sends = []
        for i in range(n):
            if mode != "swap":
                src = ins[i] if mode == "gather" else ins[i].at[me]
                cp = pltpu.make_async_copy(src, outs[i].at[me], local_sems.at[i])
                cp.start()
                local.append(cp)
            for k, peer in enumerate(peers):
                if mode == "gather":
                    src, dst = ins[i], outs[i].at[me]
                elif mode == "scatter":
                    src, dst = ins[i].at[slot_of(peer)], outs[i].at[me]
                else:
                    src, dst = ins[i], outs[i]
                cp = pltpu.make_async_remote_copy(
                    src_ref=src, dst_ref=dst, send_sem=send_sems.at[i, k], recv_sem=recv_sems.at[i, k],
                    device_id=peer, device_id_type=MESH)
                cp.start()
                sends.append(cp)
        for i in range(n):
            for k, peer in enumerate(peers):
                if mode == "gather":
                    src, dst = ins[i], outs[i].at[slot_of(peer)]
                elif mode == "scatter":
                    src, dst = ins[i].at[me], outs[i].at[slot_of(peer)]
                else:
                    src, dst = ins[i], outs[i]
                arrival = pltpu.make_async_remote_copy(
                    src_ref=src, dst_ref=dst, send_sem=send_sems.at[i, k], recv_sem=recv_sems.at[i, k],
                    device_id=peer, device_id_type=MESH)
                arrival.wait_recv()
        for cp in sends:
            cp.wait_send()
        for cp in local:
            cp.wait()

    hbm = pl.BlockSpec(memory_space=pl.ANY)
    outs = pl.pallas_call(
        body, name=name, out_shape=out_shape,
        in_specs=[hbm] * n, out_specs=[hbm] * n,
        scratch_shapes=[pltpu.SemaphoreType.DMA((n, len(flips))),
                        pltpu.SemaphoreType.DMA((n, len(flips))),
                        pltpu.SemaphoreType.DMA((n,))],
        compiler_params=pltpu.CompilerParams(has_side_effects=True),
    )(*arrs)
    return list(outs)


def _gather_via_sibling(name, arrs, slotted):
    n = len(arrs)
    out_shape = [jax.ShapeDtypeStruct(a.shape if slotted[i] else (4,) + a.shape, a.dtype)
                 for i, a in enumerate(arrs)]

    def body(*refs):
        ins, outs = refs[:n], refs[n:2 * n]
        send_sems, recv_sems, local_sems = refs[2 * n:]
        x, y, c = lax.axis_index("x"), lax.axis_index("y"), lax.axis_index("c")
        me = 2 * x + y
        chips = [(1 - x, y), (x, 1 - y), (1 - x, 1 - y)]

        def half(ref, which):
            rows = ref.shape[0]
            cut = rows // 2 // 16 * 16
            return ref.at[pl.ds(0, cut)] if which == 0 else ref.at[pl.ds(cut, rows - cut)]

        def copy(src, dst, i, k, to):
            return pltpu.make_async_remote_copy(
                src_ref=src, dst_ref=dst, send_sem=send_sems.at[i, k], recv_sem=recv_sems.at[i, k],
                device_id=to, device_id_type=MESH)

        def run(mine):
            other = 1 - mine
            local, sends = [], []
            for i in range(n):
                own = outs[i].at[me] if slotted[i] else ins[i]
                if not slotted[i]:
                    cp = pltpu.make_async_copy(ins[i], outs[i].at[me], local_sems.at[i])
                    cp.start()
                    local.append(cp)
                for k, (px, py) in enumerate(chips):
                    cp = copy(half(own, mine), half(outs[i].at[me], mine), i, k, (px, py, mine))
                    cp.start()
                    sends.append(cp)
            for k, (px, py) in enumerate(chips):
                slot = 2 * px + py
                for i in range(n):
                    landed = half(outs[i].at[slot], mine)
                    copy(landed, landed, i, k, (px, py, mine)).wait_recv()
                    cp = copy(landed, landed, i, 3 + k, (x, y, other))
                    cp.start()
                    sends.append(cp)
            for k, (px, py) in enumerate(chips):
                slot = 2 * px + py
                for i in range(n):
                    passed = half(outs[i].at[slot], other)
                    copy(passed, passed, i, 3 + k, (x, y, other)).wait_recv()
            for cp in sends:
                cp.wait_send()
            for cp in local:
                cp.wait()

        for mine in (0, 1):
            pl.when(c == mine)(functools.partial(run, mine))

    hbm = pl.BlockSpec(memory_space=pl.ANY)
    outs = pl.pallas_call(
        body, name=name, out_shape=out_shape,
        in_specs=[hbm] * n, out_specs=[hbm] * n,
        scratch_shapes=[pltpu.SemaphoreType.DMA((n, 6)), pltpu.SemaphoreType.DMA((n, 6)),
                        pltpu.SemaphoreType.DMA((n,))],
        input_output_aliases={i: i for i in range(n) if slotted[i]},
        compiler_params=pltpu.CompilerParams(has_side_effects=True),
    )(*arrs)
    return list(outs)


HBM_SPEC = pl.BlockSpec(memory_space=pltpu.HBM)
SEM_SPEC = pl.BlockSpec(memory_space=pltpu.SEMAPHORE)
DATAFLOW = pltpu.SideEffectType.DATAFLOW_SIDE_EFFECTING


def _split_peers(mode):
    x, y, c = lax.axis_index("x"), lax.axis_index("y"), lax.axis_index("c")
    if mode == "swap":
        return 0, [((x, y, 1 - c), 0)]
    return 2 * x + y, [((1 - x, y, c), 2 * (1 - x) + y), ((x, 1 - y, c), 2 * x + 1 - y),
                       ((1 - x, 1 - y, c), 2 * (1 - x) + 1 - y)]


def _split_refs(mode, src, landing, me, peer_slot):
    if mode == "gather":
        return src, landing.at[me]
    if mode == "scatter":
        return src.at[peer_slot], landing.at[me]
    return src, landing


def _plane_start(name, arrs, mode, after):
    n = len(arrs)
    n_peers = 1 if mode == "swap" else 3
    lands = [lax.empty(((4,) + a.shape) if mode == "gather" else a.shape, a.dtype) for a in arrs]

    def body(*refs):
        srcs, landing = refs[:n], refs[n:2 * n]
        send_sems, recv_sems = refs[2 * n + 1], refs[2 * n + 2]
        token = refs[-1]
        me, peers = _split_peers(mode)
        for i in range(n):
            for k, (peer, peer_slot) in enumerate(peers):
                src, dst = _split_refs(mode, srcs[i], landing[i], me, peer_slot)
                pltpu.make_async_remote_copy(
                    src_ref=src, dst_ref=dst, send_sem=send_sems.at[n_peers * i + k],
                    recv_sem=recv_sems.at[n_peers * i + k], device_id=peer, device_id_type=MESH).start()
        token[...] = jnp.zeros_like(token)

    hbm_in = [pltpu.with_memory_space_constraint(a, pltpu.HBM) for a in list(arrs) + lands]
    out = pl.pallas_call(
        body, name=name,
        out_shape=[pltpu.SemaphoreType.DMA((n_peers * n,)), pltpu.SemaphoreType.DMA((n_peers * n,))]
                  + [pltpu.HBM(a.shape, a.dtype) for a in hbm_in]
                  + [jax.ShapeDtypeStruct((8, LANES), F32)],
        in_specs=[HBM_SPEC] * (2 * n) + [pl.BlockSpec(memory_space=pl.ANY)],
        out_specs=[SEM_SPEC, SEM_SPEC] + [HBM_SPEC] * (2 * n) + [pl.BlockSpec(memory_space=pltpu.VMEM)],
        input_output_aliases={i: 2 + i for i in range(2 * n)},
        compiler_params=pltpu.CompilerParams(has_side_effects=DATAFLOW),
    )(*hbm_in, after)
    return out[:-1], out[-1]


def _plane_wait(name, state, mode, after):
    send_sems, recv_sems = state[0], state[1]
    bufs = list(state[2:])
    n = len(bufs) // 2
    n_peers = 1 if mode == "swap" else 3

    def body(*refs):
        srcs, landing = refs[:n], refs[n:2 * n]
        send_sems, recv_sems = refs[2 * n], refs[2 * n + 1]
        me, peers = _split_peers(mode)
        for i in range(n):
            for k, (peer, peer_slot) in enumerate(peers):
                src, _ = _split_refs(mode, srcs[i], landing[i], me, peer_slot)
                arrived = landing[i] if mode == "swap" else landing[i].at[peer_slot]
                cp = pltpu.make_async_remote_copy(
                    src_ref=src, dst_ref=arrived, send_sem=send_sems.at[n_peers * i + k],
                    recv_sem=recv_sems.at[n_peers * i + k], device_id=peer, device_id_type=MESH)
                cp.wait_send()
                cp.wait_recv()

    out = pl.pallas_call(
        body, name=name,
        out_shape=[pltpu.HBM(a.shape, a.dtype) for a in bufs],
        in_specs=[HBM_SPEC] * (2 * n) + [SEM_SPEC, SEM_SPEC, pl.BlockSpec(memory_space=pl.ANY)],
        out_specs=[HBM_SPEC] * (2 * n),
        input_output_aliases={i: i for i in range(2 * n)},
        compiler_params=pltpu.CompilerParams(has_side_effects=DATAFLOW),
    )(*bufs, send_sems, recv_sems, after)
    return list(out[n:])


def _tile_2d(rows, cols, row_mult, max_elems=512 * 1024):
    if rows % row_mult == 0:
        rt = _pick_tile(rows, max(row_mult, max_elems // cols), row_mult)
        return (rt, cols), rows // rt, lambda i: (i, 0)
    ct = _pick_tile(cols, max(LANES, max_elems // rows), LANES)
    return (rows, ct), cols // ct, lambda i: (0, i)


def _cast_bf16(a, name):
    block, steps, index = _tile_2d(a.shape[0], a.shape[1], 16)

    def body(a_ref, o_ref):
        o_ref[...] = a_ref[...].astype(BF16)

    return pl.pallas_call(
        body, name=name, grid=(steps,),
        in_specs=[pl.BlockSpec(block, index)],
        out_specs=pl.BlockSpec(block, index),
        out_shape=jax.ShapeDtypeStruct(a.shape, BF16),
        compiler_params=_params(("parallel",)),
    )(a)


def _cast_into_slot(a, slot, name):
    block, steps, index = _tile_2d(a.shape[0], a.shape[1], 16)

    def body(slot_ref, a_ref, o_ref):
        o_ref[...] = a_ref[...].astype(BF16)

    return pl.pallas_call(
        body, name=name,
        grid_spec=pltpu.PrefetchScalarGridSpec(
            num_scalar_prefetch=1, grid=(steps,),
            in_specs=[pl.BlockSpec(block, lambda i, s: index(i))],
            out_specs=pl.BlockSpec((None,) + block, lambda i, s: (s[0],) + index(i))),
        out_shape=jax.ShapeDtypeStruct((4,) + a.shape, BF16),
        compiler_params=_params(("arbitrary",)),
    )(slot, a)


def _sum_slots(buf, name, own=None, slot=None):
    n_slots, rows, cols = buf.shape
    (br, bc), steps, index = _tile_2d(rows, cols, 16, 320 * 1024)

    def body(*refs):
        if own is None:
            b_ref, o_ref = refs
        else:
            slot_ref, b_ref, own_ref, o_ref = refs
        acc = None
        for s in range(n_slots):
            term = b_ref[s] if own is None else jnp.where(slot_ref[0] == s, own_ref[...], b_ref[s])
            acc = term.astype(F32) if acc is None else acc + term.astype(F32)
        o_ref[...] = acc

    out_shape = jax.ShapeDtypeStruct((rows, cols), F32)
    if own is None:
        return pl.pallas_call(
            body, name=name, grid=(steps,),
            in_specs=[pl.BlockSpec((n_slots, br, bc), lambda i: (0,) + index(i))],
            out_specs=pl.BlockSpec((br, bc), index), out_shape=out_shape,
            compiler_params=_params(("parallel",), 48),
        )(buf)
    return pl.pallas_call(
        body, name=name,
        grid_spec=pltpu.PrefetchScalarGridSpec(
            num_scalar_prefetch=1, grid=(steps,),
            in_specs=[pl.BlockSpec((n_slots, br, bc), lambda i, s: (0,) + index(i)),
                      pl.BlockSpec((None, br, bc), lambda i, s: (s[0],) + index(i))],
            out_specs=pl.BlockSpec((br, bc), lambda i, s: index(i))),
        out_shape=out_shape,
        compiler_params=_params(("arbitrary",), 48),
    )(slot, buf, own)


def _adamw(w, grads, m, v, name, grad_row=0):
    rows, cols = w.shape
    (rt, _), _, _ = _tile_2d(rows, cols, 8, 160 * 1024)
    assert grad_row % rt == 0
    n_g = len(grads)
    c1 = 1.0 - ADAM_B1 ** ADAM_STEP
    c2 = 1.0 - ADAM_B2 ** ADAM_STEP

    def body(*refs):
        w_ref = refs[0]
        g_refs = refs[1:1 + n_g]
        m_ref, v_ref, g_out, d_out, m_out, v_out = refs[1 + n_g:]
        g = g_refs[0][...]
        for r in g_refs[1:]:
            g = g + r[...]
        m_new = ADAM_B1 * m_ref[...] + (1.0 - ADAM_B1) * g
        v_new = ADAM_B2 * v_ref[...] + (1.0 - ADAM_B2) * (g * g)
        m_hat = m_new / c1
        v_hat = v_new / c2
        g_out[...] = g
        d_out[...] = -ADAM_LR * (m_hat / (jnp.sqrt(v_hat) + ADAM_EPS) + ADAM_WD * w_ref[...])
        m_out[...] = m_new
        v_out[...] = v_new

    spec = pl.BlockSpec((rt, cols), lambda i: (i, 0))
    grad_spec = pl.BlockSpec((rt, cols), lambda i: (i + grad_row // rt, 0))
    shape = jax.ShapeDtypeStruct((rows, cols), F32)
    return pl.pallas_call(
        body, name=name, grid=(rows // rt,),
        in_specs=[spec] + [grad_spec] * n_g + [spec] * 2, out_specs=[spec] * 4, out_shape=[shape] * 4,
        compiler_params=_params(("parallel",), 48),
    )(w, *grads, m, v)


def _weight_pieces():
    pieces = []
    for j in range(N_CONV_TILES):
        for g in range(4):
            pieces.append((512 * j + 128 * g, D * g + 128 * j, 128))
    for hd in range(N_HEADS):
        base = W_CONV + 512 * hd
        pieces.append((base, OFF_Q + HEAD_K * hd, HEAD_K))
        pieces.append((base + HEAD_K, OFF_K + HEAD_K * hd, HEAD_K))
        pieces.append((base + 2 * HEAD_K, OFF_V + HEAD_V * hd, HEAD_V))
    pieces.append((W_CONV + W_GLA, OFF_R, D))
    pieces.append((W_CONV + W_GLA + D, OFF_MA, 2 * D))
    return pieces


N_WEIGHT_COPIES = len(_weight_pieces()) + 1


def _load_weights(w_hbm, w_s, wlr_s, sems):
    copies = [pltpu.make_async_copy(w_hbm.at[pl.ds(src, n)], w_s.at[pl.ds(dst, n)], sems.at[i])
              for i, (dst, src, n) in enumerate(_weight_pieces())]
    copies.append(pltpu.make_async_copy(w_hbm.at[pl.ds(OFF_LR, LANES)], wlr_s, sems.at[N_WEIGHT_COPIES - 1]))
    for cp in copies:
        cp.start()
    for cp in copies:
        cp.wait()


def _in_proj(h, g_pre, w_full_t):
    t_rows = h.shape[0]
    tm = _pick_tile(t_rows, 384, LANES)
    n_main = N_MAIN

    def body(h_ref, g_ref, w_hbm, proj_ref, ut_ref, lr_ref, w_s, wlr_s, w_sems):
        @pl.when(pl.program_id(0) == 0)
        def _():
            _load_weights(w_hbm, w_s, wlr_s, w_sems)

        hh = h_ref[...]
        rstd = lax.rsqrt(jnp.mean(hh * hh, axis=-1, keepdims=True) + EPS)
        uf = hh * rstd * g_ref[...]
        u = uf.astype(BF16)
        ut_ref[...] = jnp.transpose(uf).astype(BF16)
        lr_ref[...] = _dot_nt(u, wlr_s[...])
        for j in range(n_main // D):
            cols = slice(j * D, (j + 1) * D)
            proj_ref[:, cols] = _dot_nt(u, w_s[cols, :]).astype(BF16)

    return pl.pallas_call(
        body, name="in_proj", grid=(t_rows // tm,),
        in_specs=[pl.BlockSpec((tm, D), lambda i: (i, 0)),
                  pl.BlockSpec((1, D), lambda i: (0, 0)),
                  pl.BlockSpec(memory_space=pl.ANY)],
        out_specs=[pl.BlockSpec((tm, n_main), lambda i: (i, 0)),
                   pl.BlockSpec((D, tm), lambda i: (0, i)),
                   pl.BlockSpec((tm, LANES), lambda i: (i, 0))],
        out_shape=[jax.ShapeDtypeStruct((t_rows, n_main), BF16),
                   jax.ShapeDtypeStruct((D, t_rows), BF16),
                   jax.ShapeDtypeStruct((t_rows, LANES), F32)],
        scratch_shapes=[pltpu.VMEM((n_main, D), BF16), pltpu.VMEM((LANES, D), BF16),
                        pltpu.SemaphoreType.DMA((N_WEIGHT_COPIES,))],
        compiler_params=_params(("arbitrary",), 56),
    )(h, g_pre, w_full_t)


def _conv_parts(p_ref, w_ref):
    cb = p_ref[:, 0:128].astype(F32)
    cc = p_ref[:, 128:256].astype(F32)
    cx = p_ref[:, 256:384].astype(F32)
    cz = p_ref[:, 384:512].astype(F32)
    rows = cb.shape[0]
    w = w_ref[...]
    p = cc * cx
    conv = pltpu.roll(p, 1, 0) * w[0:1] + p * w[1:2] + pltpu.roll(p, rows - 1, 0) * w[2:3]
    sz = _sigmoid(cz)
    return cb, cc, cx, cz, p, conv, sz, w


def _conv_fwd(proj, conv_w, n_seq, lf):
    def body(p_ref, w_ref, y_ref):
        cb, _, _, cz, _, conv, sz, _ = _conv_parts(p_ref, w_ref)
        y_ref[...] = (cb * conv * (cz * sz)).astype(BF16)

    return pl.pallas_call(
        body, name="conv_fwd", grid=(n_seq, N_CONV_TILES),
        in_specs=[pl.BlockSpec((lf, 512), lambda b, j: (b, j)),
                  pl.BlockSpec((3, 128), lambda b, j: (0, j))],
        out_specs=pl.BlockSpec((lf, 128), lambda b, j: (b, j)),
        out_shape=jax.ShapeDtypeStruct((n_seq * lf, D), BF16),
        compiler_params=_params(("parallel", "parallel"), 48),
    )(proj, conv_w)


def _conv_bwd(proj, conv_w, dyc, n_seq, lf):
    def body(p_ref, w_ref, dy_ref, dp_ref, dw_ref):
        cb, cc, cx, cz, p, conv, sz, w = _conv_parts(p_ref, w_ref)
        rows = cb.shape[0]
        dy = dy_ref[...].astype(F32)
        silu = cz * sz
        dcb = dy * conv * silu
        dconv = dy * cb * silu
        dcz = dy * cb * conv * (sz * (1.0 + cz * (1.0 - sz)))
        d_next = pltpu.roll(dconv, rows - 1, 0)
        d_prev = pltpu.roll(dconv, 1, 0)
        dp = d_next * w[0:1] + dconv * w[1:2] + d_prev * w[2:3]
        dp_ref[:, 0:128] = dcb.astype(BF16)
        dp_ref[:, 128:256] = (dp * cx).astype(BF16)
        dp_ref[:, 256:384] = (dp * cc).astype(BF16)
        dp_ref[:, 384:512] = dcz.astype(BF16)
        dw_ref[0:1, :] = jnp.sum(dconv * pltpu.roll(p, 1, 0), axis=0, keepdims=True)
        dw_ref[1:2, :] = jnp.sum(dconv * p, axis=0, keepdims=True)
        dw_ref[2:3, :] = jnp.sum(dconv * pltpu.roll(p, rows - 1, 0), axis=0, keepdims=True)

    return pl.pallas_call(
        body, name="conv_bwd", grid=(n_seq, N_CONV_TILES),
        in_specs=[pl.BlockSpec((lf, 512), lambda b, j: (b, j)),
                  pl.BlockSpec((3, 128), lambda b, j: (0, j)),
                  pl.BlockSpec((lf, 128), lambda b, j: (b, j))],
        out_specs=[pl.BlockSpec((lf, 512), lambda b, j: (b, j)),
                   pl.BlockSpec((None, 3, 128), lambda b, j: (b, 0, j))],
        out_shape=[jax.ShapeDtypeStruct((n_seq * lf, W_CONV), BF16),
                   jax.ShapeDtypeStruct((n_seq, 3, D), F32)],
        compiler_params=_params(("parallel", "parallel"), 48),
    )(proj, conv_w, dyc)


GROUP = 3
GROUP_ROWS = GROUP * CHUNK


def _row_group(shape):
    row = lax.broadcasted_iota(jnp.int32, shape, 0)
    grp = jnp.zeros(shape, jnp.int32)
    for r in range(1, GROUP):
        grp = grp + (row >= r * CHUNK).astype(jnp.int32)
    return grp


def _lane_group(shape, width):
    lane = lax.broadcasted_iota(jnp.int32, shape, 1)
    grp = jnp.zeros(shape, jnp.int32)
    for r in range(1, GROUP):
        grp = grp + (lane >= r * width).astype(jnp.int32)
    return grp


def _group_masks(direction):
    shape = (GROUP_ROWS, GROUP_ROWS)
    row = lax.broadcasted_iota(jnp.int32, shape, 0)
    col = lax.broadcasted_iota(jnp.int32, shape, 1)
    same = _row_group(shape) == _lane_group(shape, CHUNK)
    lower = same & (col <= row)
    upper = same & (col >= row)
    if direction == 0:
        return lower.astype(BF16), upper.astype(BF16), lower
    return upper.astype(BF16), lower.astype(BF16), same & (col > row)


def _diag_blocks(v):
    w = v.shape[1]
    wide = jnp.concatenate([v] * GROUP, axis=1)
    return jnp.where(_row_group(wide.shape) == _lane_group(wide.shape, w), wide, jnp.zeros_like(wide))


def _pick_diag(wide):
    w = wide.shape[1] // GROUP
    grp = _row_group((GROUP_ROWS, w))
    out = wide[:, 0:w]
    for r in range(1, GROUP):
        out = jnp.where(grp == r, wide[:, r * w:(r + 1) * w], out)
    return out


def _per_chunk_rows(rows_of_chunk):
    w = rows_of_chunk[0].shape[1]
    return jnp.concatenate([jnp.broadcast_to(v, (CHUNK, w)) for v in rows_of_chunk], axis=0)


def _chunk_end_rows(direction, b):
    at = CHUNK - 1 if direction == 0 else 0
    return [b[r * CHUNK + at:r * CHUNK + at + 1, :] for r in range(GROUP)]


def _gla_gates(lr_bf, wg_ref, bg_ref, lf):
    z = _dot(lr_bf, wg_ref[...]) + bg_ref[...]
    valid = lax.broadcasted_iota(jnp.int32, (lf, HEAD_K), 0) >= PAD_FRONT
    return z, valid


def _group_unroll(n_groups):
    return n_groups if n_groups <= 11 else 1


def _group_rows(g):
    return pl.ds(pl.multiple_of(g * GROUP_ROWS, GROUP_ROWS), GROUP_ROWS)


def _chunk_decay(direction, g, r, b_s):
    base = g * GROUP_ROWS + r * CHUNK
    if direction == 0:
        grp = b_s[pl.ds(pl.multiple_of(base + CHUNK - 8, 8), 8), :]
        return jnp.exp(grp[7:8, :])
    grp = b_s[pl.ds(pl.multiple_of(base, 8), 8), :]
    return jnp.exp(grp[0:1, :])


def _state_scan(direction, n_groups, b_s, st_s, reverse):
    ascending = (direction == 0) != reverse

    def step(i, carry):
        g = i if ascending else n_groups - 1 - i
        for rr in range(GROUP):
            r = rr if ascending else GROUP - 1 - rr
            lanes = slice(r * HEAD_K, (r + 1) * HEAD_K)
            decay = _chunk_decay(direction, g, r, b_s)
            local = st_s[g, :, lanes]
            st_s[g, :, lanes] = carry
            carry = (local + carry * decay) if reverse else (carry * decay + local)
        return carry

    lax.fori_loop(0, n_groups, step, jnp.zeros((HEAD_V, HEAD_K), F32))


def _gla_states(direction, n_groups, qkv_ref, g_s, b_s, st_s, tri):
    def local(g, carry):
        rows = _group_rows(g)
        b = _tri_dot(tri, g_s[rows, :])
        b_s[rows, :] = b
        b_end = _per_chunk_rows(_chunk_end_rows(direction, b))
        k = qkv_ref[rows, 128:256].astype(F32)
        v = qkv_ref[rows, 256:512]
        k_dec = (k * jnp.exp(b_end - b)).astype(BF16)
        st_s[g] = _dot_tn(v, _diag_blocks(k_dec))
        return carry

    lax.fori_loop(0, n_groups, local, 0, unroll=_group_unroll(n_groups))
    _state_scan(direction, n_groups, b_s, st_s, False)


def _gla_fwd(proj, lr, wgf, wgb, bgf, bgb, n_seq, lf):
    assert lf % GROUP_ROWS == 0
    n_groups = lf // GROUP_ROWS
    scale = HEAD_K ** -0.5

    def body(qkv_ref, lr_ref, wgf_ref, wgb_ref, bgf_ref, bgb_ref, o_ref, g_s, b_s, st_s):
        lr_bf = lr_ref[...].astype(BF16)
        for direction in (0, 1):
            wg_ref, bg_ref = ((wgf_ref, bgf_ref), (wgb_ref, bgb_ref))[direction]
            z, valid = _gla_gates(lr_bf, wg_ref, bg_ref, lf)
            g_s[...] = jnp.where(valid, _log_sigmoid(z) / GATE_NORM, 0.0)
            tri, _, smask = _group_masks(direction)
            _gla_states(direction, n_groups, qkv_ref, g_s, b_s, st_s, tri)

            def out(g, carry):
                rows = _group_rows(g)
                b = b_s[rows, :]
                q = qkv_ref[rows, 0:128].astype(F32) * scale
                k = qkv_ref[rows, 128:256].astype(F32)
                v = qkv_ref[rows, 256:512]
                q_in = (q * jnp.exp(b)).astype(BF16)
                k_in = (k * jnp.exp(-b)).astype(BF16)
                s = jnp.where(smask, _dot_nt(q_in, k_in), 0.0).astype(BF16)
                o = _dot(s, v) + _dot_nt(_diag_blocks(q_in), st_s[g].astype(BF16))
                if direction == 0:
                    o_ref[rows, :] = o
                else:
                    o_ref[rows, :] = o_ref[rows, :] + o
                return carry

            lax.fori_loop(0, n_groups, out, 0, unroll=_group_unroll(n_groups))

    return pl.pallas_call(
        body, name="gla_fwd", grid=(n_seq, N_HEADS),
        in_specs=[pl.BlockSpec((lf, 512), lambda b, h: (b, N_CONV_TILES + h)),
                  pl.BlockSpec((lf, LANES), lambda b, h: (b, 0)),
                  pl.BlockSpec((None, LANES, HEAD_K), lambda b, h: (h, 0, 0)),
                  pl.BlockSpec((None, LANES, HEAD_K), lambda b, h: (h, 0, 0)),
                  pl.BlockSpec((None, 1, HEAD_K), lambda b, h: (h, 0, 0)),
                  pl.BlockSpec((None, 1, HEAD_K), lambda b, h: (h, 0, 0))],
        out_specs=pl.BlockSpec((lf, HEAD_V), lambda b, h: (b, h)),
        out_shape=jax.ShapeDtypeStruct((n_seq * lf, D), F32),
        scratch_shapes=[pltpu.VMEM((lf, HEAD_K), F32), pltpu.VMEM((lf, HEAD_K), F32),
                        pltpu.VMEM((n_groups, HEAD_V, GROUP * HEAD_K), F32)],
        compiler_params=_params(("parallel", "parallel"), 48),
    )(proj, lr, wgf, wgb, bgf, bgb)


def _gla_bwd(proj, lr, d_o, wgf, wgb, bgf, bgb, n_seq, lf, token):
    assert lf % GROUP_ROWS == 0
    n_groups = lf // GROUP_ROWS
    scale = HEAD_K ** -0.5

    def body(qkv_ref, lr_ref, do_ref, wgf_ref, wgb_ref, bgf_ref, bgb_ref, token_ref,
             dqkv_ref, dlr_ref, dwgf_ref, dwgb_ref, dbg_ref,
             g_s, b_s, fac_s, dg_s, st_s, dst_s, acc_s):
        lr_bf = lr_ref[...].astype(BF16)
        dlr = jnp.zeros((lf, LANES), F32)
        for direction in (0, 1):
            wg_ref, bg_ref = ((wgf_ref, bgf_ref), (wgb_ref, bgb_ref))[direction]
            z, valid = _gla_gates(lr_bf, wg_ref, bg_ref, lf)
            g_s[...] = jnp.where(valid, _log_sigmoid(z) / GATE_NORM, 0.0)
            fac_s[...] = jnp.where(valid, _sigmoid(-z) / GATE_NORM, 0.0)
            tri, tri_t, smask = _group_masks(direction)
            end_row = CHUNK - 1 if direction == 0 else 0
            _gla_states(direction, n_groups, qkv_ref, g_s, b_s, st_s, tri)

            def state_grad_local(g, carry):
                rows = _group_rows(g)
                q = qkv_ref[rows, 0:128].astype(F32) * scale
                q_in = (q * jnp.exp(b_s[rows, :])).astype(BF16)
                dst_s[g] = _dot_tn(do_ref[rows, :], _diag_blocks(q_in))
                return carry

            lax.fori_loop(0, n_groups, state_grad_local, 0, unroll=_group_unroll(n_groups))
            _state_scan(direction, n_groups, b_s, dst_s, True)

            def group_grads(g, carry):
                rows = _group_rows(g)
                b = b_s[rows, :]
                ends = _chunk_end_rows(direction, b)
                b_end = _per_chunk_rows(ends)
                q = qkv_ref[rows, 0:128].astype(F32) * scale
                k = qkv_ref[rows, 128:256].astype(F32)
                v = qkv_ref[rows, 256:512]
                d_out = do_ref[rows, :]
                e_pos = jnp.exp(b)
                e_neg = jnp.exp(-b)
                e_end = jnp.exp(b_end - b)
                q_in = q * e_pos
                k_in = k * e_neg
                k_dec = k * e_end
                q_in_bf = q_in.astype(BF16)
                k_in_bf = k_in.astype(BF16)
                state = st_s[g]
                d_state = dst_s[g]
                state_bf = state.astype(BF16)
                d_state_bf = d_state.astype(BF16)
                s = jnp.where(smask, _dot_nt(q_in_bf, k_in_bf), 0.0).astype(BF16)
                ds = jnp.where(smask, _dot_nt(d_out, v), 0.0).astype(BF16)
                dv = _dot_tn(s, d_out) + _dot_nt(_diag_blocks(k_dec.astype(BF16)), d_state_bf)
                dq_in = _dot(ds, k_in_bf) + _pick_diag(_dot(d_out, state_bf))
                dk_in = _dot_tn(ds, q_in_bf)
                dk_dec = _pick_diag(_dot(v, d_state_bf))
                dq = dq_in * e_pos * scale
                dk = dk_in * e_neg + dk_dec * e_end
                if direction == 0:
                    acc_s[rows, 0:128] = dq
                    acc_s[rows, 128:256] = dk
                    acc_s[rows, 256:512] = dv
                else:
                    dqkv_ref[rows, 0:128] = (acc_s[rows, 0:128] + dq).astype(BF16)
                    dqkv_ref[rows, 128:256] = (acc_s[rows, 128:256] + dk).astype(BF16)
                    dqkv_ref[rows, 256:512] = (acc_s[rows, 256:512] + dv).astype(BF16)
                dkk = dk_dec * k_dec
                db = dq_in * q_in - dk_in * k_in - dkk
                d_decay = jnp.sum(d_state * state, axis=0, keepdims=True)
                db_end = [jnp.sum(dkk[r * CHUNK:(r + 1) * CHUNK, :], axis=0, keepdims=True)
                          + d_decay[:, r * HEAD_K:(r + 1) * HEAD_K] * jnp.exp(ends[r]) for r in range(GROUP)]
                row = lax.broadcasted_iota(jnp.int32, (GROUP_ROWS, HEAD_K), 0)
                at_end = row == end_row
                for r in range(1, GROUP):
                    at_end = at_end | (row == r * CHUNK + end_row)
                db = db + jnp.where(at_end, _per_chunk_rows(db_end), 0.0)
                dg_s[rows, :] = _tri_dot(tri_t, db)
                return carry

            lax.fori_loop(0, n_groups, group_grads, 0, unroll=_group_unroll(n_groups))

            dz = dg_s[...] * fac_s[...]
            dz_bf = dz.astype(BF16)
            dbg_ref[direction:direction + 1, :] = jnp.sum(dz, axis=0, keepdims=True)
            (dwgf_ref, dwgb_ref)[direction][...] = _dot_tn(lr_bf, dz_bf)
            dlr = dlr + _dot_nt(dz_bf, wg_ref[...])

        @pl.when(pl.program_id(1) == 0)
        def _():
            dlr_ref[...] = dlr

        @pl.when(pl.program_id(1) != 0)
        def _():
            dlr_ref[...] = dlr_ref[...] + dlr

    gate_w = pl.BlockSpec((None, LANES, HEAD_K), lambda b, h: (h, 0, 0))
    gate_b = pl.BlockSpec((None, 1, HEAD_K), lambda b, h: (h, 0, 0))
    return pl.pallas_call(
        body, name="gla_bwd", grid=(n_seq, N_HEADS),
        in_specs=[pl.BlockSpec((lf, 512), lambda b, h: (b, N_CONV_TILES + h)),
                  pl.BlockSpec((lf, LANES), lambda b, h: (b, 0)),
                  pl.BlockSpec((lf, HEAD_V), lambda b, h: (b, h)),
                  gate_w, gate_w, gate_b, gate_b,
                  pl.BlockSpec((8, LANES), lambda b, h: (0, 0))],
        out_specs=[pl.BlockSpec((lf, 512), lambda b, h: (b, h)),
                   pl.BlockSpec((lf, LANES), lambda b, h: (b, 0)),
                   pl.BlockSpec((None, None, LANES, HEAD_K), lambda b, h: (b, h, 0, 0)),
                   pl.BlockSpec((None, None, LANES, HEAD_K), lambda b, h: (b, h, 0, 0)),
                   pl.BlockSpec((None, None, 2, HEAD_K), lambda b, h: (b, h, 0, 0))],
        out_shape=[jax.ShapeDtypeStruct((n_seq * lf, W_GLA), BF16),
                   jax.ShapeDtypeStruct((n_seq * lf, LANES), F32),
                   jax.ShapeDtypeStruct((n_seq, N_HEADS, LANES, HEAD_K), F32),
                   jax.ShapeDtypeStruct((n_seq, N_HEADS, LANES, HEAD_K), F32),
                   jax.ShapeDtypeStruct((n_seq, N_HEADS, 2, HEAD_K), F32)],
        scratch_shapes=[pltpu.VMEM((lf, HEAD_K), F32), pltpu.VMEM((lf, HEAD_K), F32),
                        pltpu.VMEM((lf, HEAD_K), F32), pltpu.VMEM((lf, HEAD_K), F32),
                        pltpu.VMEM((n_groups, HEAD_V, GROUP * HEAD_K), F32),
                        pltpu.VMEM((n_groups, HEAD_V, GROUP * HEAD_K), F32),
                        pltpu.VMEM((lf, 512), F32)],
        compiler_params=_params(("parallel", "arbitrary"), 56),
    )(proj, lr, d_o, wgf, wgb, bgf, bgb, token)


def _tail(h, tgt, yc, o, proj, w3, gamma, g_post, lf):
    t_rows = h.shape[0]
    tm = _pick_tile(t_rows, 256, CHUNK)
    n_chunks = lf // CHUNK
    per_tile = tm // CHUNK

    def body(h_ref, tgt_ref, yc_ref, o_ref, r_ref, ma_ref, mb_ref, w_hbm, gamma_ref, gpost_ref,
             dres_ref, yg_ref, merged_ref, dout_ref, dpc_ref, dpg_ref, dyc_ref, do_ref, dtail_ref,
             loss_ref, dgpost_ref, dgamma_ref, w_s, w_sem):
        i = pl.program_id(0)

        @pl.when(i == 0)
        def _():
            cp = pltpu.make_async_copy(w_hbm, w_s, w_sem)
            cp.start()
            cp.wait()
            loss_ref[...] = jnp.zeros_like(loss_ref)
            dgpost_ref[...] = jnp.zeros_like(dgpost_ref)
            dgamma_ref[...] = jnp.zeros_like(dgamma_ref)

        gamma = gamma_ref[...]
        o = o_ref[...]
        r = r_ref[...].astype(F32)
        sr = _sigmoid(r)
        silu_r = r * sr
        n_parts, rstd_parts = [], []
        for hd in range(N_HEADS):
            oh = o[:, hd * HEAD_V:(hd + 1) * HEAD_V]
            rstd = lax.rsqrt(jnp.mean(oh * oh, axis=-1, keepdims=True) + EPS)
            n_parts.append(oh * rstd)
            rstd_parts.append(rstd)
        n = jnp.concatenate(n_parts, axis=-1)
        gamma_t = jnp.concatenate([gamma] * N_HEADS, axis=-1)
        yg = n * gamma_t * silu_r
        yg_bf = yg.astype(BF16)
        yg_ref[...] = yg_bf
        yc = yc_ref[...]
        pc = _dot(yc, w_s[0])
        pg = _dot(yg_bf, w_s[1])
        sa = _sigmoid(ma_ref[...].astype(F32))
        sb = _sigmoid(mb_ref[...].astype(F32))
        merged = (sa * pc + sb * pg).astype(BF16)
        merged_ref[...] = merged
        out = _dot(merged, w_s[2])
        rstd2 = lax.rsqrt(jnp.mean(out * out, axis=-1, keepdims=True) + EPS)
        nn = out * rstd2
        gpost = gpost_ref[...]
        y = h_ref[...] + nn * gpost

        rowi = lax.broadcasted_iota(jnp.int32, (tm, 1), 0)
        keep = jnp.zeros((tm, 1), F32)
        for kk in range(per_tile):
            is_tok = ((i * per_tile + kk) % n_chunks) != 0
            f = jnp.where(is_tok, 1.0, 0.0)
            keep = jnp.where((rowi >= kk * CHUNK) & (rowi < (kk + 1) * CHUNK), f, keep)
        diff = (y - tgt_ref[...]) * keep
        loss_ref[...] += jnp.sum(diff * diff) * (0.5 / D)
        dy = diff * (1.0 / D)
        dres_ref[...] = dy
        dgpost_ref[...] += jnp.sum(dy * nn, axis=0, keepdims=True)
        dn = dy * gpost
        dout_f = rstd2 * (dn - nn * jnp.mean(dn * nn, axis=-1, keepdims=True))
        dout = dout_f.astype(BF16)
        dout_ref[...] = jnp.transpose(dout_f).astype(BF16)
        dmerged = _dot_nt(dout, w_s[2])
        dpc_f = dmerged * sa
        dpg_f = dmerged * sb
        dpc = dpc_f.astype(BF16)
        dpg = dpg_f.astype(BF16)
        dpc_ref[...] = jnp.transpose(dpc_f).astype(BF16)
        dpg_ref[...] = jnp.transpose(dpg_f).astype(BF16)
        dtail_ref[:, D:2 * D] = (dmerged * pc * (sa * (1.0 - sa))).astype(BF16)
        dtail_ref[:, 2 * D:3 * D] = (dmerged * pg * (sb * (1.0 - sb))).astype(BF16)
        dyc_ref[...] = _dot_nt(dpc, w_s[0]).astype(BF16)
        dyg = _dot_nt(dpg, w_s[1])
        dtail_ref[:, 0:D] = (dyg * n * gamma_t * (sr * (1.0 + r * (1.0 - sr)))).astype(BF16)
        dgam_full = jnp.sum(dyg * n * silu_r, axis=0, keepdims=True)
        dgam = dgam_full[:, 0:HEAD_V]
        for hd in range(1, N_HEADS):
            dgam = dgam + dgam_full[:, hd * HEAD_V:(hd + 1) * HEAD_V]
        dgamma_ref[...] += dgam
        dng = dyg * gamma_t * silu_r
        do_parts = []
        for hd in range(N_HEADS):
            sl = slice(hd * HEAD_V, (hd + 1) * HEAD_V)
            dnh = dng[:, sl]
            nh = n_parts[hd]
            do_parts.append(rstd_parts[hd] * (dnh - nh * jnp.mean(dnh * nh, axis=-1, keepdims=True)))
        do_ref[...] = jnp.concatenate(do_parts, axis=-1).astype(BF16)

    row = lambda c: pl.BlockSpec((tm, D), lambda i: (i, c))
    col = pl.BlockSpec((D, tm), lambda i: (0, i))
    const = lambda shape: pl.BlockSpec(shape, lambda i: (0, 0))
    act = jax.ShapeDtypeStruct((t_rows, D), BF16)
    act_t = jax.ShapeDtypeStruct((D, t_rows), BF16)
    return pl.pallas_call(
        body, name="tail", grid=(t_rows // tm,),
        in_specs=[row(0), row(0), row(0), row(0), row(6), row(7), row(8),
                  pl.BlockSpec(memory_space=pl.ANY), const((1, HEAD_V)), const((1, D))],
        out_specs=[row(0)] * 3 + [col] * 3 + [row(0)] * 2
                  + [pl.BlockSpec((tm, W_TAIL), lambda i: (i, 0)),
                     const((8, LANES)), const((1, D)), const((1, HEAD_V))],
        out_shape=[jax.ShapeDtypeStruct((t_rows, D), F32)] + [act] * 2 + [act_t] * 3 + [act] * 2
                  + [jax.ShapeDtypeStruct((t_rows, W_TAIL), BF16),
                     jax.ShapeDtypeStruct((8, LANES), F32),
                     jax.ShapeDtypeStruct((1, D), F32),
                     jax.ShapeDtypeStruct((1, HEAD_V), F32)],
        scratch_shapes=[pltpu.VMEM((3, D, D), BF16), pltpu.SemaphoreType.DMA],
        compiler_params=_params(("arbitrary",), 56),
    )(h, tgt, yc, o, proj, proj, proj, w3, gamma, g_post)


def _wgrad_t(a_t, b, name, out_dtype=BF16):
    m, t_rows = a_t.shape
    n = b.shape[1]
    tn = D if n % D == 0 else n
    tk = _pick_tile(t_rows, 768, LANES)
    n_k = t_rows // tk

    def body(a_ref, b_ref, o_ref, acc):
        k = pl.program_id(1)

        @pl.when(k == 0)
        def _():
            acc[...] = jnp.zeros_like(acc)

        acc[...] += _dot(a_ref[...], b_ref[...].astype(BF16))

        @pl.when(k == n_k - 1)
        def _():
            o_ref[...] = jnp.transpose(acc[...]).astype(out_dtype)

    return pl.pallas_call(
        body, name=name, grid=(n // tn, n_k),
        in_specs=[pl.BlockSpec((m, tk), lambda j, k: (0, k)),
                  pl.BlockSpec((tk, tn), lambda j, k: (k, j))],
        out_specs=pl.BlockSpec((tn, m), lambda j, k: (j, 0)),
        out_shape=jax.ShapeDtypeStruct((n, m), out_dtype),
        scratch_shapes=[pltpu.VMEM((m, tn), F32)],
        compiler_params=_params(("parallel", "arbitrary"), 48),
    )(a_t, b)


def _dgrad_in(dpc, dpg, dpt, dlr, w_full_t, h, g_pre, dres, token):
    t_rows = h.shape[0]
    tm = _pick_tile(t_rows, 256, 16)
    n_main = N_MAIN

    def body(dpc_ref, dpg_ref, dpt_ref, dlr_ref, w_hbm, h_ref, g_ref, dres_ref, token_ref,
             dh_ref, dg_ref, w_s, wlr_s, w_sems):
        @pl.when(pl.program_id(0) == 0)
        def _():
            _load_weights(w_hbm, w_s, wlr_s, w_sems)
            dg_ref[...] = jnp.zeros_like(dg_ref)

        du = _dot(dlr_ref[...].astype(BF16), wlr_s[...])
        du += _dot(dpc_ref[...], w_s[0:W_CONV, :])
        du += _dot(dpg_ref[...], w_s[W_CONV:W_CONV + W_GLA, :])
        du += _dot(dpt_ref[...], w_s[W_CONV + W_GLA:n_main, :])
        hh = h_ref[...]
        rstd = lax.rsqrt(jnp.mean(hh * hh, axis=-1, keepdims=True) + EPS)
        xhat = hh * rstd
        dg_ref[...] += jnp.sum(du * xhat, axis=0, keepdims=True)
        dx = du * g_ref[...]
        dh_ref[...] = rstd * (dx - xhat * jnp.mean(dx * xhat, axis=-1, keepdims=True)) + dres_ref[...]

    row = lambda width: pl.BlockSpec((tm, width), lambda i: (i, 0))
    return pl.pallas_call(
        body, name="dgrad_in", grid=(t_rows // tm,),
        in_specs=[row(W_CONV), row(W_GLA), row(W_TAIL), row(LANES),
                  pl.BlockSpec(memory_space=pl.ANY),
                  row(D), pl.BlockSpec((1, D), lambda i: (0, 0)), row(D),
                  pl.BlockSpec((8, LANES), lambda i: (0, 0))],
        out_specs=[row(D), pl.BlockSpec((1, D), lambda i: (0, 0))],
        out_shape=[jax.ShapeDtypeStruct((t_rows, D), F32), jax.ShapeDtypeStruct((1, D), F32)],
        scratch_shapes=[pltpu.VMEM((n_main, D), BF16), pltpu.VMEM((LANES, D), BF16),
                        pltpu.SemaphoreType.DMA((N_WEIGHT_COPIES,))],
        compiler_params=_params(("arbitrary",), 56),
    )(dpc, dpg, dpt, dlr, w_full_t, h, g_pre, dres, token)


def _reference_rows(g_conv, g_gla, g_tail, g_lr):
    conv = g_conv.reshape(N_CONV_TILES, 4, 128, D).transpose(1, 0, 2, 3).reshape(W_CONV, D)
    gla = g_gla.reshape(N_HEADS, 512, D)
    q = gla[:, 0:128].reshape(N_HEADS * HEAD_K, D)
    k = gla[:, 128:256].reshape(N_HEADS * HEAD_K, D)
    v = gla[:, 256:512].reshape(N_HEADS * HEAD_V, D)
    return jnp.concatenate([conv, q, k, v, g_tail[0:D], g_lr[0:2 * RANK], g_tail[D:3 * D]], axis=0)


def _pack(arrs, rows):
    flat = jnp.concatenate([a.reshape(-1) for a in arrs])
    return jnp.pad(flat, (0, rows * LANES - flat.shape[0])).reshape(rows, LANES)


def _unpack(packed, shapes):
    flat = packed.reshape(-1)
    out, pos = [], 0
    for s in shapes:
        size = 1
        for d in s:
            size *= d
        out.append(flat[pos:pos + size].reshape(s))
        pos += size
    return out


def _rows_for(shapes, mult=8):
    total = 0
    for s in shapes:
        size = 1
        for d in s:
            size *= d
        total += size
    return -(-total // (mult * LANES)) * mult


def kernel(x, meta_tokens, norm_pre, w_in, conv_w, w_gate_fwd, b_gate_fwd, w_gate_bwd, b_gate_bwd, gla_norm, w_out_conv, w_out_gla, w_merge_out, norm_post, loss_target, m_meta_tokens, m_norm_pre, m_w_in, m_conv_w, m_w_gate_fwd, m_b_gate_fwd, m_w_gate_bwd, m_b_gate_bwd, m_gla_norm, m_w_out_conv, m_w_out_gla, m_w_merge_out, m_norm_post, v_meta_tokens, v_norm_pre, v_w_in, v_conv_w, v_w_gate_fwd, v_b_gate_fwd, v_w_gate_bwd, v_b_gate_bwd, v_gla_norm, v_w_out_conv, v_w_out_gla, v_w_merge_out, v_norm_post):
    n_seq, seq, _ = x.shape
    lf = CHUNK + seq
    t_rows = n_seq * lf
    shard = 2 * lax.axis_index("x") + lax.axis_index("y")
    shard_arr = jnp.reshape(shard, (1,)).astype(jnp.int32)

    w_in_slots = _cast_into_slot(jnp.transpose(w_in[0]), shard_arr, "cast_w_in")
    w_out_bf = _cast_bf16(jnp.concatenate([w_out_conv[0], w_out_gla[0], w_merge_out[0]], axis=0), "cast_w_out")
    small_shapes = [(N_META, D // 4), (3, D // 4), (RANK, HEAD_K), (RANK, HEAD_K)]
    small = _pack([meta_tokens, conv_w[0], w_gate_fwd[0], w_gate_bwd[0]], _rows_for(small_shapes, 32))
    w_in_all, small_all = _gather_via_sibling("gather_w_in", [w_in_slots, small], (True, False))
    w_out_state, _ = _plane_start("gather_w_out_start", [w_out_bf], "gather", small_all)

    w_full_t = w_in_all.reshape(N_IN, D)
    smalls = [_unpack(small_all[s], small_shapes) for s in range(4)]
    meta_full = jnp.concatenate([smalls[s][0] for s in range(4)], axis=1)
    conv_full = jnp.concatenate([smalls[s][1] for s in range(4)], axis=1)
    wgf = jnp.stack([jnp.pad(smalls[s][2], ((0, LANES - RANK), (0, 0))) for s in range(4)]).astype(BF16)
    wgb = jnp.stack([jnp.pad(smalls[s][3], ((RANK, LANES - 2 * RANK), (0, 0))) for s in range(4)]).astype(BF16)
    bgf = b_gate_fwd.reshape(N_HEADS, 1, HEAD_K)
    bgb = b_gate_bwd.reshape(N_HEADS, 1, HEAD_K)

    head = jnp.concatenate([jnp.zeros((PAD_FRONT, D), F32), meta_full], axis=0)
    h = jnp.concatenate([jnp.broadcast_to(head[None], (n_seq, CHUNK, D)), x], axis=1).reshape(t_rows, D)
    tgt = jnp.pad(loss_target, ((0, 0), (CHUNK, 0), (0, 0))).reshape(t_rows, D)

    proj, u_t, lr = _in_proj(h, norm_pre, w_full_t)
    yc = _conv_fwd(proj, conv_full, n_seq, lf)
    o = _gla_fwd(proj, lr, wgf, wgb, bgf, bgb, n_seq, lf)
    (w_out_landed,) = _plane_wait("gather_w_out_wait", w_out_state, "gather", o)
    slot_ids = lax.broadcasted_iota(jnp.int32, (4, 1, 1), 0)
    w_out_all = jnp.where(slot_ids == shard, w_out_bf[None], w_out_landed)
    w3 = jnp.transpose(w_out_all.reshape(4, 3, D // 4, D), (1, 0, 2, 3)).reshape(3, D, D)
    (dres, yg, merged, dout_t, dpc_t, dpg_t, dyc, d_o, dtail, loss_acc, d_gpost, d_gamma) = _tail(
        h, tgt, yc, o, proj, w3, gla_norm, norm_post, lf)
    g_w_oc = _wgrad_t(dpc_t, yc, "wgrad_out_conv")
    g_w_og = _wgrad_t(dpg_t, yg, "wgrad_out_gla")
    g_w_mo = _wgrad_t(dout_t, merged, "wgrad_merge_out")
    g_out_slots = jnp.concatenate([g.reshape(4, D // 4, D) for g in (g_w_oc, g_w_og, g_w_mo)], axis=1)
    out_state, out_token = _plane_start("scatter_out_grads_start", [g_out_slots], "scatter", g_w_mo)
    dgla, dlr, dwgf_p, dwgb_p, dbg_p = _gla_bwd(proj, lr, d_o, wgf, wgb, bgf, bgb, n_seq, lf, out_token)
    (got_out,) = _plane_wait("scatter_out_grads_wait", out_state, "scatter", dlr)
    dconv, dconvw_p = _conv_bwd(proj, conv_full, dyc, n_seq, lf)
    g_conv = _wgrad_t(u_t, dconv, "wgrad_in_conv")
    g_gla = _wgrad_t(u_t, dgla, "wgrad_in_gla")
    g_tail = _wgrad_t(u_t, dtail, "wgrad_in_tail")
    g_lr = _wgrad_t(u_t, dlr, "wgrad_in_lr")

    g_in_slots = _reference_rows(g_conv, g_gla, g_tail, g_lr).reshape(4, SHARD_IN, D)
    in_state, in_token = _plane_start("scatter_in_grads_start", [g_in_slots], "scatter", g_lr)
    dh, d_gpre = _dgrad_in(dconv, dgla, dtail, dlr, w_full_t, h, norm_pre, dres, in_token)
    (got_in,) = _plane_wait("scatter_in_grads_wait", in_state, "scatter", d_gpre)

    plane_in = _sum_slots(got_in, "sum_w_in_grads", own=g_in_slots, slot=shard_arr)
    plane_out = _sum_slots(got_out, "sum_w_out_grads", own=g_out_slots, slot=shard_arr)
    swap_state, swap_token = _plane_start("swap_plane_sums_start", [plane_in, plane_out], "swap", plane_out)

    dh3 = dh.reshape(n_seq, lf, D)
    grad_x = dh3[:, CHUNK:, :]

    d_meta = jnp.sum(dh3[:, PAD_FRONT:CHUNK, :], axis=0)
    d_convw = jnp.sum(dconvw_p, axis=0)
    d_wgf = jnp.transpose(jnp.sum(dwgf_p, axis=0)[:, 0:RANK, :], (1, 0, 2)).reshape(RANK, N_HEADS * HEAD_K)
    d_wgb = jnp.transpose(jnp.sum(dwgb_p, axis=0)[:, RANK:2 * RANK, :], (1, 0, 2)).reshape(RANK, N_HEADS * HEAD_K)
    d_bg = jnp.sum(dbg_p, axis=0)
    d_bgf = d_bg[:, 0, :].reshape(1, N_HEADS * HEAD_K)
    d_bgb = d_bg[:, 1, :].reshape(1, N_HEADS * HEAD_K)
    part_shapes = [(N_META, D), (3, D), (RANK, 512), (RANK, 512), (1, D), (1, 512), (1, 512), (1, HEAD_V),
                   (1, D), (1, LANES)]
    loss_part = loss_acc[0:1, :] + swap_token[0:1, :]
    parts = _pack([d_meta, d_convw, d_wgf, d_wgb, d_gpre, d_bgf, d_bgb, d_gamma, d_gpost, loss_part],
                  _rows_for(part_shapes))
    (parts_all,) = _exchange("gather_small_grads", [parts], ALL_FLIPS, (4, 2, 1), "gather")
    (g_meta, g_convw, g_wgf, g_wgb, g_npre, g_bgf, g_bgb, g_gnorm, g_npost, loss_row) = _unpack(
        _sum_slots(parts_all, "sum_small_grads"), part_shapes)
    loss = loss_row[0, 0]

    def col_shard(a, width):
        return lax.dynamic_slice_in_dim(a, shard * width, width, axis=a.ndim - 1)

    upd_shapes = [(N_META, D // 4), (3, D // 4), (RANK, HEAD_K), (RANK, HEAD_K), (1, D), (1, 512), (1, 512),
                  (1, HEAD_V), (1, D)]
    upd_rows = _rows_for(upd_shapes)
    small_w = _pack([meta_tokens, conv_w[0], w_gate_fwd[0], w_gate_bwd[0], norm_pre, b_gate_fwd, b_gate_bwd,
                     gla_norm, norm_post], upd_rows)
    small_g = _pack([col_shard(g_meta, D // 4), col_shard(g_convw, D // 4), col_shard(g_wgf, HEAD_K),
                     col_shard(g_wgb, HEAD_K), g_npre, g_bgf, g_bgb, g_gnorm, g_npost], upd_rows)
    small_m = _pack([m_meta_tokens, m_conv_w[0], m_w_gate_fwd[0], m_w_gate_bwd[0], m_norm_pre, m_b_gate_fwd,
                     m_b_gate_bwd, m_gla_norm, m_norm_post], upd_rows)
    small_v = _pack([v_meta_tokens, v_conv_w[0], v_w_gate_fwd[0], v_w_gate_bwd[0], v_norm_pre, v_b_gate_fwd,
                     v_b_gate_bwd, v_gla_norm, v_norm_post], upd_rows)
    small_out = [_unpack(a, upd_shapes) for a in _adamw(small_w, [small_g], small_m, small_v, "adamw_small")]

    other_in, other_out = _plane_wait("swap_plane_sums_wait", swap_state, "swap", small_out[0][0])
    big_in = _adamw(jnp.transpose(w_in[0]), [plane_in, other_in], jnp.transpose(m_w_in[0]), jnp.transpose(v_w_in[0]),
                    "adamw_w_in")
    out_params = ((w_out_conv, m_w_out_conv, v_w_out_conv), (w_out_gla, m_w_out_gla, v_w_out_gla),
                  (w_merge_out, m_w_merge_out, v_w_merge_out))
    big_out = [_adamw(w[0], [plane_out, other_out], m[0], v[0], f"adamw_w_out_{i}", grad_row=i * (D // 4))
               for i, (w, m, v) in enumerate(out_params)]

    results = []
    for kind in range(4):
        sm = small_out[kind]
        w_in_part = jnp.transpose(big_in[kind])[None]
        outs3 = [big_out[i][kind][None] for i in range(3)]
        results.extend([
            sm[0], sm[4], w_in_part, sm[1][None], sm[2][None], sm[5], sm[3][None], sm[6], sm[7],
            outs3[0], outs3[1], outs3[2], sm[8]])
    return (loss, grad_x, *results)
```

```python
import functools

import jax
import jax.numpy as jnp
from jax import lax
from jax.experimental import pallas as pl
from jax.experimental.pallas import tpu as pltpu

F32 = jnp.float32
BF16 = jnp.bfloat16
MESH = pl.DeviceIdType.MESH

D = 1024
N_META = 16
CHUNK = 64
PAD_FRONT = CHUNK - N_META
N_HEADS = 4
HEAD_K = 128
HEAD_V = 256
RANK = 16
EPS = 1e-6
GATE_NORM = 16.0
N_IN = 9248
SHARD_IN = N_IN // 4
LANES = 128
N_CONV_TILES = 8
W_CONV = 4096
W_GLA = 2048
W_TAIL = 3072
N_MAIN = W_CONV + W_GLA + W_TAIL
OFF_Q, OFF_K, OFF_V, OFF_R = 4096, 4608, 5120, 6144
OFF_LR, OFF_MA, OFF_MB = 7168, 7200, 8224
MIB = 1024 * 1024

ADAM_LR = 0.001
ADAM_B1 = 0.9
ADAM_B2 = 0.999
ADAM_EPS = 1e-08
ADAM_WD = 0.01
ADAM_STEP = 10


def _params(sem=None, vmem_mib=None):
    return pltpu.CompilerParams(
        dimension_semantics=sem,
        vmem_limit_bytes=None if vmem_mib is None else vmem_mib * MIB)


def _pick_tile(n, target, mult):
    best = None
    for t in range(mult, min(n, target) + 1, mult):
        if n % t == 0:
            best = t
    return n if best is None else best


def _sigmoid(v):
    return 1.0 / (1.0 + jnp.exp(-v))


def _log_sigmoid(v):
    return jnp.minimum(v, 0.0) - jnp.log(1.0 + jnp.exp(-jnp.abs(v)))


def _dot(a, b):
    return jnp.dot(a, b, preferred_element_type=F32)


def _dot_nt(a, b):
    return lax.dot_general(a, b, (((1,), (1,)), ((), ())), preferred_element_type=F32)


def _dot_tn(a, b):
    return lax.dot_general(a, b, (((0,), (0,)), ((), ())), preferred_element_type=F32)


PLANE_FLIPS = ((1, 0, 0), (0, 1, 0), (1, 1, 0))
ALL_FLIPS = tuple((m >> 2 & 1, m >> 1 & 1, m & 1) for m in range(1, 8))
SIBLING_FLIPS = ((0, 0, 1),)


def _exchange(name, arrs, flips, slot_weights, mode):
    n = len(arrs)
    n_slots = 1
    for w in slot_weights:
        n_slots += w
    if mode == "gather":
        out_shape = [jax.ShapeDtypeStruct((n_slots,) + a.shape, a.dtype) for a in arrs]
    else:
        out_shape = [jax.ShapeDtypeStruct(a.shape, a.dtype) for a in arrs]

    def body(*refs):
        ins, outs = refs[:n], refs[n:2 * n]
        send_sems, recv_sems, local_sems = refs[2 * n:]
        pos = (lax.axis_index("x"), lax.axis_index("y"), lax.axis_index("c"))

        def slot_of(p):
            return p[0] * slot_weights[0] + p[1] * slot_weights[1] + p[2] * slot_weights[2]

        peers = [tuple(1 - pos[a] if f[a] else pos[a] for a in range(3)) for f in flips]
        me = slot_of(pos)
        local = []
        sends = []
        for i in range(n):
            if mode != "swap":
                src = ins[i] if mode == "gather" else ins[i].at[me]
                cp = pltpu.make_async_copy(src, outs[i].at[me], local_sems.at[i])
                cp.start()
                local.append(cp)
            for k, peer in enumerate(peers):
                if mode == "gather":
                    src, dst = ins[i], outs[i].at[me]
                elif mode == "scatter":
                    src, dst = ins[i].at[slot_of(peer)], outs[i].at[me]
                else:
                    src, dst = ins[i], outs[i]
                cp = pltpu.make_async_remote_copy(
                    src_ref=src, dst_ref=dst, send_sem=send_sems.at[i, k], recv_sem=recv_sems.at[i, k],
                    device_id=peer, device_id_type=MESH)
                cp.start()
                sends.append(cp)
        for i in range(n):
            for k, peer in enumerate(peers):
                if mode == "gather":
                    src, dst = ins[i], outs[i].at[slot_of(peer)]
                elif mode == "scatter":
                    src, dst = ins[i].at[me], outs[i].at[slot_of(peer)]
                else:
                    src, dst = ins[i], outs[i]
                arrival = pltpu.make_async_remote_copy(
                    src_ref=src, dst_ref=dst, send_sem=send_sems.at[i, k], recv_sem=recv_sems.at[i, k],
                    device_id=peer, device_id_type=MESH)
                arrival.wait_recv()
        for cp in sends:
            cp.wait_send()
        for cp in local:
            cp.wait()

    hbm = pl.BlockSpec(memory_space=pl.ANY)
    outs = pl.pallas_call(
        body, name=name, out_shape=out_shape,
        in_specs=[hbm] * n, out_specs=[hbm] * n,
        scratch_shapes=[pltpu.SemaphoreType.DMA((n, len(flips))),
                        pltpu.SemaphoreType.DMA((n, len(flips))),
                        pltpu.SemaphoreType.DMA((n,))],
        compiler_params=pltpu.CompilerParams(has_side_effects=True),
    )(*arrs)
    return list(outs)


def _gather_via_sibling(name, arrs, slotted):
    n = len(arrs)
    out_shape = [jax.ShapeDtypeStruct(a.shape if slotted[i] else (4,) + a.shape, a.dtype)
                 for i, a in enumerate(arrs)]

    def body(*refs):
        ins, outs = refs[:n], refs[n:2 * n]
        send_sems, recv_sems, local_sems = refs[2 * n:]
        x, y, c = lax.axis_index("x"), lax.axis_index("y"), lax.axis_index("c")
        me = 2 * x + y
        chips = [(1 - x, y), (x, 1 - y), (1 - x, 1 - y)]

        def half(ref, which):
            rows = ref.shape[0]
            cut = rows // 2 // 16 * 16
            return ref.at[pl.ds(0, cut)] if which == 0 else ref.at[pl.ds(cut, rows - cut)]

        def copy(src, dst, i, k, to):
            return pltpu.make_async_remote_copy(
                src_ref=src, dst_ref=dst, send_sem=send_sems.at[i, k], recv_sem=recv_sems.at[i, k],
                device_id=to, device_id_type=MESH)

        def run(mine):
            other = 1 - mine
            local, sends = [], []
            for i in range(n):
                own = outs[i].at[me] if slotted[i] else ins[i]
                if not slotted[i]:
                    cp = pltpu.make_async_copy(ins[i], outs[i].at[me], local_sems.at[i])
                    cp.start()
                    local.append(cp)
                for k, (px, py) in enumerate(chips):
                    cp = copy(half(own, mine), half(outs[i].at[me], mine), i, k, (px, py, mine))
                    cp.start()
                    sends.append(cp)
            for k, (px, py) in enumerate(chips):
                slot = 2 * px + py
                for i in range(n):
                    landed = half(outs[i].at[slot], mine)
                    copy(landed, landed, i, k, (px, py, mine)).wait_recv()
                    cp = copy(landed, landed, i, 3 + k, (x, y, other))
                    cp.start()
                    sends.append(cp)
            for k, (px, py) in enumerate(chips):
                slot = 2 * px + py
                for i in range(n):
                    passed = half(outs[i].at[slot], other)
                    copy(passed, passed, i, 3 + k, (x, y, other)).wait_recv()
            for cp in sends:
                cp.wait_send()
            for cp in local:
                cp.wait()

        for mine in (0, 1):
            pl.when(c == mine)(functools.partial(run, mine))

    hbm = pl.BlockSpec(memory_space=pl.ANY)
    outs = pl.pallas_call(
        body, name=name, out_shape=out_shape,
        in_specs=[hbm] * n, out_specs=[hbm] * n,
        scratch_shapes=[pltpu.SemaphoreType.DMA((n, 6)), pltpu.SemaphoreType.DMA((n, 6)),
                        pltpu.SemaphoreType.DMA((n,))],
        input_output_aliases={i: i for i in range(n) if slotted[i]},
        compiler_params=pltpu.CompilerParams(has_side_effects=True),
    )(*arrs)
    return list(outs)


HBM_SPEC = pl.BlockSpec(memory_space=pltpu.HBM)
SEM_SPEC = pl.BlockSpec(memory_space=pltpu.SEMAPHORE)
DATAFLOW = pltpu.SideEffectType.DATAFLOW_SIDE_EFFECTING


def _split_peers(mode):
    x, y, c = lax.axis_index("x"), lax.axis_index("y"), lax.axis_index("c")
    if mode == "swap":
        return 0, [((x, y, 1 - c), 0)]
    return 2 * x + y, [((1 - x, y, c), 2 * (1 - x) + y), ((x, 1 - y, c), 2 * x + 1 - y),
                       ((1 - x, 1 - y, c), 2 * (1 - x) + 1 - y)]


def _split_refs(mode, src, landing, me, peer_slot):
    if mode == "gather":
        return src, landing.at[me]
    if mode == "scatter":
        return src.at[peer_slot], landing.at[me]
    return src, landing


def _plane_start(name, arrs, mode, after):
    n = len(arrs)
    n_peers = 1 if mode == "swap" else 3
    lands = [lax.empty(((4,) + a.shape) if mode == "gather" else a.shape, a.dtype) for a in arrs]

    def body(*refs):
        srcs, landing = refs[:n], refs[n:2 * n]
        send_sems, recv_sems = refs[2 * n + 1], refs[2 * n + 2]
        token = refs[-1]
        me, peers = _split_peers(mode)
        for i in range(n):
            for k, (peer, peer_slot) in enumerate(peers):
                src, dst = _split_refs(mode, srcs[i], landing[i], me, peer_slot)
                pltpu.make_async_remote_copy(
                    src_ref=src, dst_ref=dst, send_sem=send_sems.at[n_peers * i + k],
                    recv_sem=recv_sems.at[n_peers * i + k], device_id=peer, device_id_type=MESH).start()
        token[...] = jnp.zeros_like(token)

    hbm_in = [pltpu.with_memory_space_constraint(a, pltpu.HBM) for a in list(arrs) + lands]
    out = pl.pallas_call(
        body, name=name,
        out_shape=[pltpu.SemaphoreType.DMA((n_peers * n,)), pltpu.SemaphoreType.DMA((n_peers * n,))]
                  + [pltpu.HBM(a.shape, a.dtype) for a in hbm_in]
                  + [jax.ShapeDtypeStruct((8, LANES), F32)],
        in_specs=[HBM_SPEC] * (2 * n) + [pl.BlockSpec(memory_space=pl.ANY)],
        out_specs=[SEM_SPEC, SEM_SPEC] + [HBM_SPEC] * (2 * n) + [pl.BlockSpec(memory_space=pltpu.VMEM)],
        input_output_aliases={i: 2 + i for i in range(2 * n)},
        compiler_params=pltpu.CompilerParams(has_side_effects=DATAFLOW),
    )(*hbm_in, after)
    return out[:-1], out[-1]


def _plane_wait(name, state, mode, after):
    send_sems, recv_sems = state[0], state[1]
    bufs = list(state[2:])
    n = len(bufs) // 2
    n_peers = 1 if mode == "swap" else 3

    def body(*refs):
        srcs, landing = refs[:n], refs[n:2 * n]
        send_sems, recv_sems = refs[2 * n], refs[2 * n + 1]
        me, peers = _split_peers(mode)
        for i in range(n):
            for k, (peer, peer_slot) in enumerate(peers):
                src, _ = _split_refs(mode, srcs[i], landing[i], me, peer_slot)
                arrived = landing[i] if mode == "swap" else landing[i].at[peer_slot]
                cp = pltpu.make_async_remote_copy(
                    src_ref=src, dst_ref=arrived, send_sem=send_sems.at[n_peers * i + k],
                    recv_sem=recv_sems.at[n_peers * i + k], device_id=peer, device_id_type=MESH)
                cp.wait_send()
                cp.wait_recv()

    out = pl.pallas_call(
        body, name=name,
        out_shape=[pltpu.HBM(a.shape, a.dtype) for a in bufs],
        in_specs=[HBM_SPEC] * (2 * n) + [SEM_SPEC, SEM_SPEC, pl.BlockSpec(memory_space=pl.ANY)],
        out_specs=[HBM_SPEC] * (2 * n),
        input_output_aliases={i: i for i in range(2 * n)},
        compiler_params=pltpu.CompilerParams(has_side_effects=DATAFLOW),
    )(*bufs, send_sems, recv_sems, after)
    return list(out[n:])


def _tile_2d(rows, cols, row_mult, max_elems=512 * 1024):
    if rows % row_mult == 0:
        rt = _pick_tile(rows, max(row_mult, max_elems // cols), row_mult)
        return (rt, cols), rows // rt, lambda i: (i, 0)
    ct = _pick_tile(cols, max(LANES, max_elems // rows), LANES)
    return (rows, ct), cols // ct, lambda i: (0, i)


def _cast_bf16(a, name):
    block, steps, index = _tile_2d(a.shape[0], a.shape[1], 16)

    def body(a_ref, o_ref):
        o_ref[...] = a_ref[...].astype(BF16)

    return pl.pallas_call(
        body, name=name, grid=(steps,),
        in_specs=[pl.BlockSpec(block, index)],
        out_specs=pl.BlockSpec(block, index),
        out_shape=jax.ShapeDtypeStruct(a.shape, BF16),
        compiler_params=_params(("parallel",)),
    )(a)


def _cast_into_slot(a, slot, name):
    block, steps, index = _tile_2d(a.shape[0], a.shape[1], 16)

    def body(slot_ref, a_ref, o_ref):
        o_ref[...] = a_ref[...].astype(BF16)

    return pl.pallas_call(
        body, name=name,
        grid_spec=pltpu.PrefetchScalarGridSpec(
            num_scalar_prefetch=1, grid=(steps,),
            in_specs=[pl.BlockSpec(block, lambda i, s: index(i))],
            out_specs=pl.BlockSpec((None,) + block, lambda i, s: (s[0],) + index(i))),
        out_shape=jax.ShapeDtypeStruct((4,) + a.shape, BF16),
        compiler_params=_params(("arbitrary",)),
    )(slot, a)


def _sum_slots(buf, name, own=None, slot=None):
    n_slots, rows, cols = buf.shape
    (br, bc), steps, index = _tile_2d(rows, cols, 16, 320 * 1024)

    def body(*refs):
        if own is None:
            b_ref, o_ref = refs
        else:
            slot_ref, b_ref, own_ref, o_ref = refs
        acc = None
        for s in range(n_slots):
            term = b_ref[s] if own is None else jnp.where(slot_ref[0] == s, own_ref[...], b_ref[s])
            acc = term.astype(F32) if acc is None else acc + term.astype(F32)
        o_ref[...] = acc

    out_shape = jax.ShapeDtypeStruct((rows, cols), F32)
    if own is None:
        return pl.pallas_call(
            body, name=name, grid=(steps,),
            in_specs=[pl.BlockSpec((n_slots, br, bc), lambda i: (0,) + index(i))],
            out_specs=pl.BlockSpec((br, bc), index), out_shape=out_shape,
            compiler_params=_params(("parallel",), 48),
        )(buf)
    return pl.pallas_call(
        body, name=name,
        grid_spec=pltpu.PrefetchScalarGridSpec(
            num_scalar_prefetch=1, grid=(steps,),
            in_specs=[pl.BlockSpec((n_slots, br, bc), lambda i, s: (0,) + index(i)),
                      pl.BlockSpec((None, br, bc), lambda i, s: (s[0],) + index(i))],
            out_specs=pl.BlockSpec((br, bc), lambda i, s: index(i))),
        out_shape=out_shape,
        compiler_params=_params(("arbitrary",), 48),
    )(slot, buf, own)


def _adamw(w, grads, m, v, name, grad_row=0):
    rows, cols = w.shape
    (rt, _), _, _ = _tile_2d(rows, cols, 8, 160 * 1024)
    assert grad_row % rt == 0
    n_g = len(grads)
    c1 = 1.0 - ADAM_B1 ** ADAM_STEP
    c2 = 1.0 - ADAM_B2 ** ADAM_STEP

    def body(*refs):
        w_ref = refs[0]
        g_refs = refs[1:1 + n_g]
        m_ref, v_ref, g_out, d_out, m_out, v_out = refs[1 + n_g:]
        g = g_refs[0][...]
        for r in g_refs[1:]:
            g = g + r[...]
        m_new = ADAM_B1 * m_ref[...] + (1.0 - ADAM_B1) * g
        v_new = ADAM_B2 * v_ref[...] + (1.0 - ADAM_B2) * (g * g)
        m_hat = m_new / c1
        v_hat = v_new / c2
        g_out[...] = g
        d_out[...] = -ADAM_LR * (m_hat / (jnp.sqrt(v_hat) + ADAM_EPS) + ADAM_WD * w_ref[...])
        m_out[...] = m_new
        v_out[...] = v_new

    spec = pl.BlockSpec((rt, cols), lambda i: (i, 0))
    grad_spec = pl.BlockSpec((rt, cols), lambda i: (i + grad_row // rt, 0))
    shape = jax.ShapeDtypeStruct((rows, cols), F32)
    return pl.pallas_call(
        body, name=name, grid=(rows // rt,),
        in_specs=[spec] + [grad_spec] * n_g + [spec] * 2, out_specs=[spec] * 4, out_shape=[shape] * 4,
        compiler_params=_params(("parallel",), 48),
    )(w, *grads, m, v)


def _weight_pieces():
    pieces = []
    for j in range(N_CONV_TILES):
        for g in range(4):
            pieces.append((512 * j + 128 * g, D * g + 128 * j, 128))
    for hd in range(N_HEADS):
        base = W_CONV + 512 * hd
        pieces.append((base, OFF_Q + HEAD_K * hd, HEAD_K))
        pieces.append((base + HEAD_K, OFF_K + HEAD_K * hd, HEAD_K))
        pieces.append((base + 2 * HEAD_K, OFF_V + HEAD_V * hd, HEAD_V))
    pieces.append((W_CONV + W_GLA, OFF_R, D))
    pieces.append((W_CONV + W_GLA + D, OFF_MA, 2 * D))
    return pieces


N_WEIGHT_COPIES = len(_weight_pieces()) + 1


def _load_weights(w_hbm, w_s, wlr_s, sems):
    copies = [pltpu.make_async_copy(w_hbm.at[pl.ds(src, n)], w_s.at[pl.ds(dst, n)], sems.at[i])
              for i, (dst, src, n) in enumerate(_weight_pieces())]
    copies.append(pltpu.make_async_copy(w_hbm.at[pl.ds(OFF_LR, LANES)], wlr_s, sems.at[N_WEIGHT_COPIES - 1]))
    for cp in copies:
        cp.start()
    for cp in copies:
        cp.wait()


def _in_proj(h, g_pre, w_full_t):
    t_rows = h.shape[0]
    tm = _pick_tile(t_rows, 384, LANES)
    n_main = N_MAIN

    def body(h_ref, g_ref, w_hbm, proj_ref, ut_ref, lr_ref, w_s, wlr_s, w_sems):
        @pl.when(pl.program_id(0) == 0)
        def _():
            _load_weights(w_hbm, w_s, wlr_s, w_sems)

        hh = h_ref[...]
        rstd = lax.rsqrt(jnp.mean(hh * hh, axis=-1, keepdims=True) + EPS)
        uf = hh * rstd * g_ref[...]
        u = uf.astype(BF16)
        ut_ref[...] = jnp.transpose(uf).astype(BF16)
        lr_ref[...] = _dot_nt(u, wlr_s[...])
        for j in range(n_main // D):
            cols = slice(j * D, (j + 1) * D)
            proj_ref[:, cols] = _dot_nt(u, w_s[cols, :]).astype(BF16)

    return pl.pallas_call(
        body, name="in_proj", grid=(t_rows // tm,),
        in_specs=[pl.BlockSpec((tm, D), lambda i: (i, 0)),
                  pl.BlockSpec((1, D), lambda i: (0, 0)),
                  pl.BlockSpec(memory_space=pl.ANY)],
        out_specs=[pl.BlockSpec((tm, n_main), lambda i: (i, 0)),
                   pl.BlockSpec((D, tm), lambda i: (0, i)),
                   pl.BlockSpec((tm, LANES), lambda i: (i, 0))],
        out_shape=[jax.ShapeDtypeStruct((t_rows, n_main), BF16),
                   jax.ShapeDtypeStruct((D, t_rows), BF16),
                   jax.ShapeDtypeStruct((t_rows, LANES), F32)],
        scratch_shapes=[pltpu.VMEM((n_main, D), BF16), pltpu.VMEM((LANES, D), BF16),
                        pltpu.SemaphoreType.DMA((N_WEIGHT_COPIES,))],
        compiler_params=_params(("arbitrary",), 56),
    )(h, g_pre, w_full_t)


def _conv_parts(p_ref, w_ref):
    cb = p_ref[:, 0:128].astype(F32)
    cc = p_ref[:, 128:256].astype(F32)
    cx = p_ref[:, 256:384].astype(F32)
    cz = p_ref[:, 384:512].astype(F32)
    rows = cb.shape[0]
    w = w_ref[...]
    p = cc * cx
    conv = pltpu.roll(p, 1, 0) * w[0:1] + p * w[1:2] + pltpu.roll(p, rows - 1, 0) * w[2:3]
    sz = _sigmoid(cz)
    return cb, cc, cx, cz, p, conv, sz, w


def _conv_fwd(proj, conv_w, n_seq, lf):
    def body(p_ref, w_ref, y_ref):
        cb, _, _, cz, _, conv, sz, _ = _conv_parts(p_ref, w_ref)
        y_ref[...] = (cb * conv * (cz * sz)).astype(BF16)

    return pl.pallas_call(
        body, name="conv_fwd", grid=(n_seq, N_CONV_TILES),
        in_specs=[pl.BlockSpec((lf, 512), lambda b, j: (b, j)),
                  pl.BlockSpec((3, 128), lambda b, j: (0, j))],
        out_specs=pl.BlockSpec((lf, 128), lambda b, j: (b, j)),
        out_shape=jax.ShapeDtypeStruct((n_seq * lf, D), BF16),
        compiler_params=_params(("parallel", "parallel"), 48),
    )(proj, conv_w)


def _conv_bwd(proj, conv_w, dyc, n_seq, lf):
    def body(p_ref, w_ref, dy_ref, dp_ref, dw_ref):
        cb, cc, cx, cz, p, conv, sz, w = _conv_parts(p_ref, w_ref)
        rows = cb.shape[0]
        dy = dy_ref[...].astype(F32)
        silu = cz * sz
        dcb = dy * conv * silu
        dconv = dy * cb * silu
        dcz = dy * cb * conv * (sz * (1.0 + cz * (1.0 - sz)))
        d_next = pltpu.roll(dconv, rows - 1, 0)
        d_prev = pltpu.roll(dconv, 1, 0)
        dp = d_next * w[0:1] + dconv * w[1:2] + d_prev * w[2:3]
        dp_ref[:, 0:128] = dcb.astype(BF16)
        dp_ref[:, 128:256] = (dp * cx).astype(BF16)
        dp_ref[:, 256:384] = (dp * cc).astype(BF16)
        dp_ref[:, 384:512] = dcz.astype(BF16)
        dw_ref[0:1, :] = jnp.sum(dconv * pltpu.roll(p, 1, 0), axis=0, keepdims=True)
        dw_ref[1:2, :] = jnp.sum(dconv * p, axis=0, keepdims=True)
        dw_ref[2:3, :] = jnp.sum(dconv * pltpu.roll(p, rows - 1, 0), axis=0, keepdims=True)

    return pl.pallas_call(
        body, name="conv_bwd", grid=(n_seq, N_CONV_TILES),
        in_specs=[pl.BlockSpec((lf, 512), lambda b, j: (b, j)),
                  pl.BlockSpec((3, 128), lambda b, j: (0, j)),
                  pl.BlockSpec((lf, 128), lambda b, j: (b, j))],
        out_specs=[pl.BlockSpec((lf, 512), lambda b, j: (b, j)),
                   pl.BlockSpec((None, 3, 128), lambda b, j: (b, 0, j))],
        out_shape=[jax.ShapeDtypeStruct((n_seq * lf, W_CONV), BF16),
                   jax.ShapeDtypeStruct((n_seq, 3, D), F32)],
        compiler_params=_params(("parallel", "parallel"), 48),
    )(proj, conv_w, dyc)


GROUP = 3
GROUP_ROWS = GROUP * CHUNK


def _row_group(shape):
    row = lax.broadcasted_iota(jnp.int32, shape, 0)
    grp = jnp.zeros(shape, jnp.int32)
    for r in range(1, GROUP):
        grp = grp + (row >= r * CHUNK).astype(jnp.int32)
    return grp


def _lane_group(shape, width):
    lane = lax.broadcasted_iota(jnp.int32, shape, 1)
    grp = jnp.zeros(shape, jnp.int32)
    for r in range(1, GROUP):
        grp = grp + (lane >= r * width).astype(jnp.int32)
    return grp


def _score_mask(direction):
    shape = (GROUP_ROWS, GROUP_ROWS)
    row = lax.broadcasted_iota(jnp.int32, shape, 0)
    col = lax.broadcasted_iota(jnp.int32, shape, 1)
    same = _row_group(shape) == _lane_group(shape, CHUNK)
    return same & ((col <= row) if direction == 0 else (col > row))


def _diag_blocks(v):
    w = v.shape[1]
    wide = jnp.concatenate([v] * GROUP, axis=1)
    return jnp.where(_row_group(wide.shape) == _lane_group(wide.shape, w), wide, jnp.zeros_like(wide))


def _per_chunk_dot(lhs, state, transposed):
    outs = []
    for r in range(GROUP):
        rows = lhs[r * CHUNK:(r + 1) * CHUNK, :]
        blk = state[:, r * HEAD_K:(r + 1) * HEAD_K]
        outs.append(_dot_nt(rows, blk) if transposed else _dot(rows, blk))
    return jnp.concatenate(outs, axis=0)


def _chunk_cumsum(v, suffix):
    pos = lax.broadcasted_iota(jnp.int32, v.shape, 0) & (CHUNK - 1)
    shift = 1
    while shift < CHUNK:
        if suffix:
            moved = pltpu.roll(v, GROUP_ROWS - shift, 0)
            v = v + jnp.where(pos < CHUNK - shift, moved, 0.0)
        else:
            moved = pltpu.roll(v, shift, 0)
            v = v + jnp.where(pos >= shift, moved, 0.0)
        shift *= 2
    return v


def _per_chunk_rows(rows_of_chunk):
    w = rows_of_chunk[0].shape[1]
    return jnp.concatenate([jnp.broadcast_to(v, (CHUNK, w)) for v in rows_of_chunk], axis=0)


def _chunk_end_rows(direction, b):
    at = CHUNK - 1 if direction == 0 else 0
    return [b[r * CHUNK + at:r * CHUNK + at + 1, :] for r in range(GROUP)]


def _gla_gates(lr_bf, wg_ref, bg_ref, lf):
    z = _dot(lr_bf, wg_ref[...]) + bg_ref[...]
    valid = lax.broadcasted_iota(jnp.int32, (lf, HEAD_K), 0) >= PAD_FRONT
    return z, valid


def _group_unroll(n_groups):
    return n_groups if n_groups <= 11 else 1


def _group_rows(g):
    return pl.ds(pl.multiple_of(g * GROUP_ROWS, GROUP_ROWS), GROUP_ROWS)


def _chunk_decay(direction, g, r, b_s):
    base = g * GROUP_ROWS + r * CHUNK
    if direction == 0:
        grp = b_s[pl.ds(pl.multiple_of(base + CHUNK - 8, 8), 8), :]
        return jnp.exp(grp[7:8, :])
    grp = b_s[pl.ds(pl.multiple_of(base, 8), 8), :]
    return jnp.exp(grp[0:1, :])


def _state_scan(direction, n_groups, b_s, st_s, reverse):
    ascending = (direction == 0) != reverse

    def step(i, carry):
        g = i if ascending else n_groups - 1 - i
        for rr in range(GROUP):
            r = rr if ascending else GROUP - 1 - rr
            lanes = slice(r * HEAD_K, (r + 1) * HEAD_K)
            decay = _chunk_decay(direction, g, r, b_s)
            local = st_s[g, :, lanes]
            st_s[g, :, lanes] = carry
            carry = (local + carry * decay) if reverse else (carry * decay + local)
        return carry

    lax.fori_loop(0, n_groups, step, jnp.zeros((HEAD_V, HEAD_K), F32))


def _gla_states(direction, n_groups, qkv_ref, g_s, b_s, st_s):
    def local(g, carry):
        rows = _group_rows(g)
        b = _chunk_cumsum(g_s[rows, :], direction == 1)
        b_s[rows, :] = b
        b_end = _per_chunk_rows(_chunk_end_rows(direction, b))
        k = qkv_ref[rows, 128:256].astype(F32)
        v = qkv_ref[rows, 256:512]
        k_dec = (k * jnp.exp(b_end - b)).astype(BF16)
        st_s[g] = _dot_tn(v, _diag_blocks(k_dec))
        return carry

    lax.fori_loop(0, n_groups, local, 0, unroll=_group_unroll(n_groups))
    _state_scan(direction, n_groups, b_s, st_s, False)


def _gla_fwd(proj, lr, wgf, wgb, bgf, bgb, n_seq, lf):
    assert lf % GROUP_ROWS == 0
    n_groups = lf // GROUP_ROWS
    scale = HEAD_K ** -0.5

    def body(qkv_ref, lr_ref, wgf_ref, wgb_ref, bgf_ref, bgb_ref, o_ref, g_s, b_s, st_s):
        lr_bf = lr_ref[...].astype(BF16)
        for direction in (0, 1):
            wg_ref, bg_ref = ((wgf_ref, bgf_ref), (wgb_ref, bgb_ref))[direction]
            z, valid = _gla_gates(lr_bf, wg_ref, bg_ref, lf)
            g_s[...] = jnp.where(valid, _log_sigmoid(z) / GATE_NORM, 0.0)
            smask = _score_mask(direction)
            _gla_states(direction, n_groups, qkv_ref, g_s, b_s, st_s)

            def out(g, carry):
                rows = _group_rows(g)
                b = b_s[rows, :]
                q = qkv_ref[rows, 0:128].astype(F32) * scale
                k = qkv_ref[rows, 128:256].astype(F32)
                v = qkv_ref[rows, 256:512]
                q_in = (q * jnp.exp(b)).astype(BF16)
                k_in = (k * jnp.exp(-b)).astype(BF16)
                s = jnp.where(smask, _dot_nt(q_in, k_in), 0.0).astype(BF16)
                o = _dot(s, v) + _per_chunk_dot(q_in, st_s[g].astype(BF16), True)
                if direction == 0:
                    o_ref[rows, :] = o
                else:
                    o_ref[rows, :] = o_ref[rows, :] + o
                return carry

            lax.fori_loop(0, n_groups, out, 0, unroll=_group_unroll(n_groups))

    return pl.pallas_call(
        body, name="gla_fwd", grid=(n_seq, N_HEADS),
        in_specs=[pl.BlockSpec((lf, 512), lambda b, h: (b, N_CONV_TILES + h)),
                  pl.BlockSpec((lf, LANES), lambda b, h: (b, 0)),
                  pl.BlockSpec((None, LANES, HEAD_K), lambda b, h: (h, 0, 0)),
                  pl.BlockSpec((None, LANES, HEAD_K), lambda b, h: (h, 0, 0)),
                  pl.BlockSpec((None, 1, HEAD_K), lambda b, h: (h, 0, 0)),
                  pl.BlockSpec((None, 1, HEAD_K), lambda b, h: (h, 0, 0))],
        out_specs=pl.BlockSpec((lf, HEAD_V), lambda b, h: (b, h)),
        out_shape=jax.ShapeDtypeStruct((n_seq * lf, D), F32),
        scratch_shapes=[pltpu.VMEM((lf, HEAD_K), F32), pltpu.VMEM((lf, HEAD_K), F32),
                        pltpu.VMEM((n_groups, HEAD_V, GROUP * HEAD_K), F32)],
        compiler_params=_params(("parallel", "parallel"), 48),
    )(proj, lr, wgf, wgb, bgf, bgb)


def _gla_bwd(proj, lr, d_o, wgf, wgb, bgf, bgb, n_seq, lf, token):
    assert lf % GROUP_ROWS == 0
    n_groups = lf // GROUP_ROWS
    scale = HEAD_K ** -0.5

    def body(qkv_ref, lr_ref, do_ref, wgf_ref, wgb_ref, bgf_ref, bgb_ref, token_ref,
             dqkv_ref, dlr_ref, dwgf_ref, dwgb_ref, dbg_ref,
             g_s, b_s, fac_s, dg_s, st_s, dst_s, acc_s):
        lr_bf = lr_ref[...].astype(BF16)
        dlr = jnp.zeros((lf, LANES), F32)
        for direction in (0, 1):
            wg_ref, bg_ref = ((wgf_ref, bgf_ref), (wgb_ref, bgb_ref))[direction]
            z, valid = _gla_gates(lr_bf, wg_ref, bg_ref, lf)
            g_s[...] = jnp.where(valid, _log_sigmoid(z) / GATE_NORM, 0.0)
            fac_s[...] = jnp.where(valid, _sigmoid(-z) / GATE_NORM, 0.0)
            smask = _score_mask(direction)
            end_row = CHUNK - 1 if direction == 0 else 0
            _gla_states(direction, n_groups, qkv_ref, g_s, b_s, st_s)

            def state_grad_local(g, carry):
                rows = _group_rows(g)
                q = qkv_ref[rows, 0:128].astype(F32) * scale
                q_in = (q * jnp.exp(b_s[rows, :])).astype(BF16)
                dst_s[g] = _dot_tn(do_ref[rows, :], _diag_blocks(q_in))
                return carry

            lax.fori_loop(0, n_groups, state_grad_local, 0, unroll=_group_unroll(n_groups))
            _state_scan(direction, n_groups, b_s, dst_s, True)

            def group_grads(g, carry):
                rows = _group_rows(g)
                b = b_s[rows, :]
                ends = _chunk_end_rows(direction, b)
                b_end = _per_chunk_rows(ends)
                q = qkv_ref[rows, 0:128].astype(F32) * scale
                k = qkv_ref[rows, 128:256].astype(F32)
                v = qkv_ref[rows, 256:512]
                d_out = do_ref[rows, :]
                e_pos = jnp.exp(b)
                e_neg = jnp.exp(-b)
                e_end = jnp.exp(b_end - b)
                q_in = q * e_pos
                k_in = k * e_neg
                k_dec = k * e_end
                q_in_bf = q_in.astype(BF16)
                k_in_bf = k_in.astype(BF16)
                state = st_s[g]
                d_state = dst_s[g]
                state_bf = state.astype(BF16)
                d_state_bf = d_state.astype(BF16)
                s = jnp.where(smask, _dot_nt(q_in_bf, k_in_bf), 0.0).astype(BF16)
                ds = jnp.where(smask, _dot_nt(d_out, v), 0.0).astype(BF16)
                dv = _dot_tn(s, d_out) + _per_chunk_dot(k_dec.astype(BF16), d_state_bf, True)
                dq_in = _dot(ds, k_in_bf) + _per_chunk_dot(d_out, state_bf, False)
                dk_in = _dot_tn(ds, q_in_bf)
                dk_dec = _per_chunk_dot(v, d_state_bf, False)
                dq = dq_in * e_pos * scale
                dk = dk_in * e_neg + dk_dec * e_end
                if direction == 0:
                    acc_s[rows, 0:128] = dq
                    acc_s[rows, 128:256] = dk
                    acc_s[rows, 256:512] = dv
                else:
                    dqkv_ref[rows, 0:128] = (acc_s[rows, 0:128] + dq).astype(BF16)
                    dqkv_ref[rows, 128:256] = (acc_s[rows, 128:256] + dk).astype(BF16)
                    dqkv_ref[rows, 256:512] = (acc_s[rows, 256:512] + dv).astype(BF16)
                dkk = dk_dec * k_dec
                db = dq_in * q_in - dk_in * k_in - dkk
                d_decay = jnp.sum(d_state * state, axis=0, keepdims=True)
                db_end = [jnp.sum(dkk[r * CHUNK:(r + 1) * CHUNK, :], axis=0, keepdims=True)
                          + d_decay[:, r * HEAD_K:(r + 1) * HEAD_K] * jnp.exp(ends[r]) for r in range(GROUP)]
                row = lax.broadcasted_iota(jnp.int32, (GROUP_ROWS, HEAD_K), 0)
                at_end = row == end_row
                for r in range(1, GROUP):
                    at_end = at_end | (row == r * CHUNK + end_row)
                db = db + jnp.where(at_end, _per_chunk_rows(db_end), 0.0)
                dg_s[rows, :] = _chunk_cumsum(db, direction == 0)
                return carry

            lax.fori_loop(0, n_groups, group_grads, 0, unroll=_group_unroll(n_groups))

            dz = dg_s[...] * fac_s[...]
            dz_bf = dz.astype(BF16)
            dbg_ref[direction:direction + 1, :] = jnp.sum(dz, axis=0, keepdims=True)
            (dwgf_ref, dwgb_ref)[direction][...] = _dot_tn(lr_bf, dz_bf)
            dlr = dlr + _dot_nt(dz_bf, wg_ref[...])

        @pl.when(pl.program_id(1) == 0)
        def _():
            dlr_ref[...] = dlr

        @pl.when(pl.program_id(1) != 0)
        def _():
            dlr_ref[...] = dlr_ref[...] + dlr

    gate_w = pl.BlockSpec((None, LANES, HEAD_K), lambda b, h: (h, 0, 0))
    gate_b = pl.BlockSpec((None, 1, HEAD_K), lambda b, h: (h, 0, 0))
    return pl.pallas_call(
        body, name="gla_bwd", grid=(n_seq, N_HEADS),
        in_specs=[pl.BlockSpec((lf, 512), lambda b, h: (b, N_CONV_TILES + h)),
                  pl.BlockSpec((lf, LANES), lambda b, h: (b, 0)),
                  pl.BlockSpec((lf, HEAD_V), lambda b, h: (b, h)),
                  gate_w, gate_w, gate_b, gate_b,
                  pl.BlockSpec((8, LANES), lambda b, h: (0, 0))],
        out_specs=[pl.BlockSpec((lf, 512), lambda b, h: (b, h)),
                   pl.BlockSpec((lf, LANES), lambda b, h: (b, 0)),
                   pl.BlockSpec((None, None, LANES, HEAD_K), lambda b, h: (b, h, 0, 0)),
                   pl.BlockSpec((None, None, LANES, HEAD_K), lambda b, h: (b, h, 0, 0)),
                   pl.BlockSpec((None, None, 2, HEAD_K), lambda b, h: (b, h, 0, 0))],
        out_shape=[jax.ShapeDtypeStruct((n_seq * lf, W_GLA), BF16),
                   jax.ShapeDtypeStruct((n_seq * lf, LANES), F32),
                   jax.ShapeDtypeStruct((n_seq, N_HEADS, LANES, HEAD_K), F32),
                   jax.ShapeDtypeStruct((n_seq, N_HEADS, LANES, HEAD_K), F32),
                   jax.ShapeDtypeStruct((n_seq, N_HEADS, 2, HEAD_K), F32)],
        scratch_shapes=[pltpu.VMEM((lf, HEAD_K), F32), pltpu.VMEM((lf, HEAD_K), F32),
                        pltpu.VMEM((lf, HEAD_K), F32), pltpu.VMEM((lf, HEAD_K), F32),
                        pltpu.VMEM((n_groups, HEAD_V, GROUP * HEAD_K), F32),
                        pltpu.VMEM((n_groups, HEAD_V, GROUP * HEAD_K), F32),
                        pltpu.VMEM((lf, 512), F32)],
        compiler_params=_params(("parallel", "arbitrary"), 56),
    )(proj, lr, d_o, wgf, wgb, bgf, bgb, token)


def _tail(h, tgt, yc, o, proj, w3, gamma, g_post, lf):
    t_rows = h.shape[0]
    tm = _pick_tile(t_rows, 256, CHUNK)
    n_chunks = lf // CHUNK
    per_tile = tm // CHUNK

    def body(h_ref, tgt_ref, yc_ref, o_ref, r_ref, ma_ref, mb_ref, w_hbm, gamma_ref, gpost_ref,
             dres_ref, yg_ref, merged_ref, dout_ref, dpc_ref, dpg_ref, dyc_ref, do_ref, dtail_ref,
             loss_ref, dgpost_ref, dgamma_ref, w_s, w_sem):
        i = pl.program_id(0)

        @pl.when(i == 0)
        def _():
            cp = pltpu.make_async_copy(w_hbm, w_s, w_sem)
            cp.start()
            cp.wait()
            loss_ref[...] = jnp.zeros_like(loss_ref)
            dgpost_ref[...] = jnp.zeros_like(dgpost_ref)
            dgamma_ref[...] = jnp.zeros_like(dgamma_ref)

        gamma = gamma_ref[...]
        o = o_ref[...]
        r = r_ref[...].astype(F32)
        sr = _sigmoid(r)
        silu_r = r * sr
        n_parts, rstd_parts = [], []
        for hd in range(N_HEADS):
            oh = o[:, hd * HEAD_V:(hd + 1) * HEAD_V]
            rstd = lax.rsqrt(jnp.mean(oh * oh, axis=-1, keepdims=True) + EPS)
            n_parts.append(oh * rstd)
            rstd_parts.append(rstd)
        n = jnp.concatenate(n_parts, axis=-1)
        gamma_t = jnp.concatenate([gamma] * N_HEADS, axis=-1)
        yg = n * gamma_t * silu_r
        yg_bf = yg.astype(BF16)
        yg_ref[...] = yg_bf
        yc = yc_ref[...]
        pc = _dot(yc, w_s[0])
        pg = _dot(yg_bf, w_s[1])
        sa = _sigmoid(ma_ref[...].astype(F32))
        sb = _sigmoid(mb_ref[...].astype(F32))
        merged = (sa * pc + sb * pg).astype(BF16)
        merged_ref[...] = merged
        out = _dot(merged, w_s[2])
        rstd2 = lax.rsqrt(jnp.mean(out * out, axis=-1, keepdims=True) + EPS)
        nn = out * rstd2
        gpost = gpost_ref[...]
        y = h_ref[...] + nn * gpost

        rowi = lax.broadcasted_iota(jnp.int32, (tm, 1), 0)
        keep = jnp.zeros((tm, 1), F32)
        for kk in range(per_tile):
            is_tok = ((i * per_tile + kk) % n_chunks) != 0
            f = jnp.where(is_tok, 1.0, 0.0)
            keep = jnp.where((rowi >= kk * CHUNK) & (rowi < (kk + 1) * CHUNK), f, keep)
        diff = (y - tgt_ref[...]) * keep
        loss_ref[...] += jnp.sum(diff * diff) * (0.5 / D)
        dy = diff * (1.0 / D)
        dres_ref[...] = dy
        dgpost_ref[...] += jnp.sum(dy * nn, axis=0, keepdims=True)
        dn = dy * gpost
        dout_f = rstd2 * (dn - nn * jnp.mean(dn * nn, axis=-1, keepdims=True))
        dout = dout_f.astype(BF16)
        dout_ref[...] = jnp.transpose(dout_f).astype(BF16)
        dmerged = _dot_nt(dout, w_s[2])
        dpc_f = dmerged * sa
        dpg_f = dmerged * sb
        dpc = dpc_f.astype(BF16)
        dpg = dpg_f.astype(BF16)
        dpc_ref[...] = jnp.transpose(dpc_f).astype(BF16)
        dpg_ref[...] = jnp.transpose(dpg_f).astype(BF16)
        dtail_ref[:, D:2 * D] = (dmerged * pc * (sa * (1.0 - sa))).astype(BF16)
        dtail_ref[:, 2 * D:3 * D] = (dmerged * pg * (sb * (1.0 - sb))).astype(BF16)
        dyc_ref[...] = _dot_nt(dpc, w_s[0]).astype(BF16)
        dyg = _dot_nt(dpg, w_s[1])
        dtail_ref[:, 0:D] = (dyg * n * gamma_t * (sr * (1.0 + r * (1.0 - sr)))).astype(BF16)
        dgam_full = jnp.sum(dyg * n * silu_r, axis=0, keepdims=True)
        dgam = dgam_full[:, 0:HEAD_V]
        for hd in range(1, N_HEADS):
            dgam = dgam + dgam_full[:, hd * HEAD_V:(hd + 1) * HEAD_V]
        dgamma_ref[...] += dgam
        dng = dyg * gamma_t * silu_r
        do_parts = []
        for hd in range(N_HEADS):
            sl = slice(hd * HEAD_V, (hd + 1) * HEAD_V)
            dnh = dng[:, sl]
            nh = n_parts[hd]
            do_parts.append(rstd_parts[hd] * (dnh - nh * jnp.mean(dnh * nh, axis=-1, keepdims=True)))
        do_ref[...] = jnp.concatenate(do_parts, axis=-1).astype(BF16)

    row = lambda c: pl.BlockSpec((tm, D), lambda i: (i, c))
    col = pl.BlockSpec((D, tm), lambda i: (0, i))
    const = lambda shape: pl.BlockSpec(shape, lambda i: (0, 0))
    act = jax.ShapeDtypeStruct((t_rows, D), BF16)
    act_t = jax.ShapeDtypeStruct((D, t_rows), BF16)
    return pl.pallas_call(
        body, name="tail", grid=(t_rows // tm,),
        in_specs=[row(0), row(0), row(0), row(0), row(6), row(7), row(8),
                  pl.BlockSpec(memory_space=pl.ANY), const((1, HEAD_V)), const((1, D))],
        out_specs=[row(0)] * 3 + [col] * 3 + [row(0)] * 2
                  + [pl.BlockSpec((tm, W_TAIL), lambda i: (i, 0)),
                     const((8, LANES)), const((1, D)), const((1, HEAD_V))],
        out_shape=[jax.ShapeDtypeStruct((t_rows, D), F32)] + [act] * 2 + [act_t] * 3 + [act] * 2
                  + [jax.ShapeDtypeStruct((t_rows, W_TAIL), BF16),
                     jax.ShapeDtypeStruct((8, LANES), F32),
                     jax.ShapeDtypeStruct((1, D), F32),
                     jax.ShapeDtypeStruct((1, HEAD_V), F32)],
        scratch_shapes=[pltpu.VMEM((3, D, D), BF16), pltpu.SemaphoreType.DMA],
        compiler_params=_params(("arbitrary",), 56),
    )(h, tgt, yc, o, proj, proj, proj, w3, gamma, g_post)


def _wgrad_t(a_t, b, name, out_dtype=BF16):
    m, t_rows = a_t.shape
    n = b.shape[1]
    tn = D if n % D == 0 else n
    tk = _pick_tile(t_rows, 768, LANES)
    n_k = t_rows // tk

    def body(a_ref, b_ref, o_ref, acc):
        k = pl.program_id(1)

        @pl.when(k == 0)
        def _():
            acc[...] = jnp.zeros_like(acc)

        acc[...] += _dot(a_ref[...], b_ref[...].astype(BF16))

        @pl.when(k == n_k - 1)
        def _():
            o_ref[...] = jnp.transpose(acc[...]).astype(out_dtype)

    return pl.pallas_call(
        body, name=name, grid=(n // tn, n_k),
        in_specs=[pl.BlockSpec((m, tk), lambda j, k: (0, k)),
                  pl.BlockSpec((tk, tn), lambda j, k: (k, j))],
        out_specs=pl.BlockSpec((tn, m), lambda j, k: (j, 0)),
        out_shape=jax.ShapeDtypeStruct((n, m), out_dtype),
        scratch_shapes=[pltpu.VMEM((m, tn), F32)],
        compiler_params=_params(("parallel", "arbitrary"), 48),
    )(a_t, b)


def _dgrad_in(dpc, dpg, dpt, dlr, w_full_t, h, g_pre, dres, token):
    t_rows = h.shape[0]
    tm = _pick_tile(t_rows, 256, 16)
    n_main = N_MAIN

    def body(dpc_ref, dpg_ref, dpt_ref, dlr_ref, w_hbm, h_ref, g_ref, dres_ref, token_ref,
             dh_ref, dg_ref, w_s, wlr_s, w_sems):
        @pl.when(pl.program_id(0) == 0)
        def _():
            _load_weights(w_hbm, w_s, wlr_s, w_sems)
            dg_ref[...] = jnp.zeros_like(dg_ref)

        du = _dot(dlr_ref[...].astype(BF16), wlr_s[...])
        du += _dot(dpc_ref[...], w_s[0:W_CONV, :])
        du += _dot(dpg_ref[...], w_s[W_CONV:W_CONV + W_GLA, :])
        du += _dot(dpt_ref[...], w_s[W_CONV + W_GLA:n_main, :])
        hh = h_ref[...]
        rstd = lax.rsqrt(jnp.mean(hh * hh, axis=-1, keepdims=True) + EPS)
        xhat = hh * rstd
        dg_ref[...] += jnp.sum(du * xhat, axis=0, keepdims=True)
        dx = du * g_ref[...]
        dh_ref[...] = rstd * (dx - xhat * jnp.mean(dx * xhat, axis=-1, keepdims=True)) + dres_ref[...]

    row = lambda width: pl.BlockSpec((tm, width), lambda i: (i, 0))
    return pl.pallas_call(
        body, name="dgrad_in", grid=(t_rows // tm,),
        in_specs=[row(W_CONV), row(W_GLA), row(W_TAIL), row(LANES),
                  pl.BlockSpec(memory_space=pl.ANY),
                  row(D), pl.BlockSpec((1, D), lambda i: (0, 0)), row(D),
                  pl.BlockSpec((8, LANES), lambda i: (0, 0))],
        out_specs=[row(D), pl.BlockSpec((1, D), lambda i: (0, 0))],
        out_shape=[jax.ShapeDtypeStruct((t_rows, D), F32), jax.ShapeDtypeStruct((1, D), F32)],
        scratch_shapes=[pltpu.VMEM((n_main, D), BF16), pltpu.VMEM((LANES, D), BF16),
                        pltpu.SemaphoreType.DMA((N_WEIGHT_COPIES,))],
        compiler_params=_params(("arbitrary",), 56),
    )(dpc, dpg, dpt, dlr, w_full_t, h, g_pre, dres, token)


def _reference_rows(g_conv, g_gla, g_tail, g_lr):
    conv = g_conv.reshape(N_CONV_TILES, 4, 128, D).transpose(1, 0, 2, 3).reshape(W_CONV, D)
    gla = g_gla.reshape(N_HEADS, 512, D)
    q = gla[:, 0:128].reshape(N_HEADS * HEAD_K, D)
    k = gla[:, 128:256].reshape(N_HEADS * HEAD_K, D)
    v = gla[:, 256:512].reshape(N_HEADS * HEAD_V, D)
    return jnp.concatenate([conv, q, k, v, g_tail[0:D], g_lr[0:2 * RANK], g_tail[D:3 * D]], axis=0)


def _pack(arrs, rows):
    flat = jnp.concatenate([a.reshape(-1) for a in arrs])
    return jnp.pad(flat, (0, rows * LANES - flat.shape[0])).reshape(rows, LANES)


def _unpack(packed, shapes):
    flat = packed.reshape(-1)
    out, pos = [], 0
    for s in shapes:
        size = 1
        for d in s:
            size *= d
        out.append(flat[pos:pos + size].reshape(s))
        pos += size
    return out


def _rows_for(shapes, mult=8):
    total = 0
    for s in shapes:
        size = 1
        for d in s:
            size *= d
        total += size
    return -(-total // (mult * LANES)) * mult


def kernel(x, meta_tokens, norm_pre, w_in, conv_w, w_gate_fwd, b_gate_fwd, w_gate_bwd, b_gate_bwd, gla_norm, w_out_conv, w_out_gla, w_merge_out, norm_post, loss_target, m_meta_tokens, m_norm_pre, m_w_in, m_conv_w, m_w_gate_fwd, m_b_gate_fwd, m_w_gate_bwd, m_b_gate_bwd, m_gla_norm, m_w_out_conv, m_w_out_gla, m_w_merge_out, m_norm_post, v_meta_tokens, v_norm_pre, v_w_in, v_conv_w, v_w_gate_fwd, v_b_gate_fwd, v_w_gate_bwd, v_b_gate_bwd, v_gla_norm, v_w_out_conv, v_w_out_gla, v_w_merge_out, v_norm_post):
    n_seq, seq, _ = x.shape
    lf = CHUNK + seq
    t_rows = n_seq * lf
    shard = 2 * lax.axis_index("x") + lax.axis_index("y")
    shard_arr = jnp.reshape(shard, (1,)).astype(jnp.int32)

    w_in_slots = _cast_into_slot(jnp.transpose(w_in[0]), shard_arr, "cast_w_in")
    w_out_bf = _cast_bf16(jnp.concatenate([w_out_conv[0], w_out_gla[0], w_merge_out[0]], axis=0), "cast_w_out")
    small_shapes = [(N_META, D // 4), (3, D // 4), (RANK, HEAD_K), (RANK, HEAD_K)]
    small = _pack([meta_tokens, conv_w[0], w_gate_fwd[0], w_gate_bwd[0]], _rows_for(small_shapes, 32))
    w_in_all, small_all = _gather_via_sibling("gather_w_in", [w_in_slots, small], (True, False))
    w_out_state, _ = _plane_start("gather_w_out_start", [w_out_bf], "gather", small_all)

    w_full_t = w_in_all.reshape(N_IN, D)
    smalls = [_unpack(small_all[s], small_shapes) for s in range(4)]
    meta_full = jnp.concatenate([smalls[s][0] for s in range(4)], axis=1)
    conv_full = jnp.concatenate([smalls[s][1] for s in range(4)], axis=1)
    wgf = jnp.stack([jnp.pad(smalls[s][2], ((0, LANES - RANK), (0, 0))) for s in range(4)]).astype(BF16)
    wgb = jnp.stack([jnp.pad(smalls[s][3], ((RANK, LANES - 2 * RANK), (0, 0))) for s in range(4)]).astype(BF16)
    bgf = b_gate_fwd.reshape(N_HEADS, 1, HEAD_K)
    bgb = b_gate_bwd.reshape(N_HEADS, 1, HEAD_K)

    head = jnp.concatenate([jnp.zeros((PAD_FRONT, D), F32), meta_full], axis=0)
    h = jnp.concatenate([jnp.broadcast_to(head[None], (n_seq, CHUNK, D)), x], axis=1).reshape(t_rows, D)
    tgt = jnp.pad(loss_target, ((0, 0), (CHUNK, 0), (0, 0))).reshape(t_rows, D)

    proj, u_t, lr = _in_proj(h, norm_pre, w_full_t)
    yc = _conv_fwd(proj, conv_full, n_seq, lf)
    o = _gla_fwd(proj, lr, wgf, wgb, bgf, bgb, n_seq, lf)
    (w_out_landed,) = _plane_wait("gather_w_out_wait", w_out_state, "gather", o)
    slot_ids = lax.broadcasted_iota(jnp.int32, (4, 1, 1), 0)
    w_out_all = jnp.where(slot_ids == shard, w_out_bf[None], w_out_landed)
    w3 = jnp.transpose(w_out_all.reshape(4, 3, D // 4, D), (1, 0, 2, 3)).reshape(3, D, D)
    (dres, yg, merged, dout_t, dpc_t, dpg_t, dyc, d_o, dtail, loss_acc, d_gpost, d_gamma) = _tail(
        h, tgt, yc, o, proj, w3, gla_norm, norm_post, lf)
    g_w_oc = _wgrad_t(dpc_t, yc, "wgrad_out_conv")
    g_w_og = _wgrad_t(dpg_t, yg, "wgrad_out_gla")
    g_w_mo = _wgrad_t(dout_t, merged, "wgrad_merge_out")
    g_out_slots = jnp.concatenate([g.reshape(4, D // 4, D) for g in (g_w_oc, g_w_og, g_w_mo)], axis=1)
    out_state, out_token = _plane_start("scatter_out_grads_start", [g_out_slots], "scatter", g_w_mo)
    dgla, dlr, dwgf_p, dwgb_p, dbg_p = _gla_bwd(proj, lr, d_o, wgf, wgb, bgf, bgb, n_seq, lf, out_token)
    (got_out,) = _plane_wait("scatter_out_grads_wait", out_state, "scatter", dlr)
    dconv, dconvw_p = _conv_bwd(proj, conv_full, dyc, n_seq, lf)
    g_conv = _wgrad_t(u_t, dconv, "wgrad_in_conv")
    g_gla = _wgrad_t(u_t, dgla, "wgrad_in_gla")
    g_tail = _wgrad_t(u_t, dtail, "wgrad_in_tail")
    g_lr = _wgrad_t(u_t, dlr, "wgrad_in_lr")

    g_in_slots = _reference_rows(g_conv, g_gla, g_tail, g_lr).reshape(4, SHARD_IN, D)
    in_state, in_token = _plane_start("scatter_in_grads_start", [g_in_slots], "scatter", g_lr)
    dh, d_gpre = _dgrad_in(dconv, dgla, dtail, dlr, w_full_t, h, norm_pre, dres, in_token)
    (got_in,) = _plane_wait("scatter_in_grads_wait", in_state, "scatter", d_gpre)

    plane_in = _sum_slots(got_in, "sum_w_in_grads", own=g_in_slots, slot=shard_arr)
    plane_out = _sum_slots(got_out, "sum_w_out_grads", own=g_out_slots, slot=shard_arr)
    swap_state, swap_token = _plane_start("swap_plane_sums_start", [plane_in, plane_out], "swap", plane_out)

    dh3 = dh.reshape(n_seq, lf, D)
    grad_x = dh3[:, CHUNK:, :]

    d_meta = jnp.sum(dh3[:, PAD_FRONT:CHUNK, :], axis=0)
    d_convw = jnp.sum(dconvw_p, axis=0)
    d_wgf = jnp.transpose(jnp.sum(dwgf_p, axis=0)[:, 0:RANK, :], (1, 0, 2)).reshape(RANK, N_HEADS * HEAD_K)
    d_wgb = jnp.transpose(jnp.sum(dwgb_p, axis=0)[:, RANK:2 * RANK, :], (1, 0, 2)).reshape(RANK, N_HEADS * HEAD_K)
    d_bg = jnp.sum(dbg_p, axis=0)
    d_bgf = d_bg[:, 0, :].reshape(1, N_HEADS * HEAD_K)
    d_bgb = d_bg[:, 1, :].reshape(1, N_HEADS * HEAD_K)
    part_shapes = [(N_META, D), (3, D), (RANK, 512), (RANK, 512), (1, D), (1, 512), (1, 512), (1, HEAD_V),
                   (1, D), (1, LANES)]
    loss_part = loss_acc[0:1, :] + swap_token[0:1, :]
    parts = _pack([d_meta, d_convw, d_wgf, d_wgb, d_gpre, d_bgf, d_bgb, d_gamma, d_gpost, loss_part],
                  _rows_for(part_shapes))
    (parts_all,) = _exchange("gather_small_grads", [parts], ALL_FLIPS, (4, 2, 1), "gather")
    (g_meta, g_convw, g_wgf, g_wgb, g_npre, g_bgf, g_bgb, g_gnorm, g_npost, loss_row) = _unpack(
        _sum_slots(parts_all, "sum_small_grads"), part_shapes)
    loss = loss_row[0, 0]

    def col_shard(a, width):
        return lax.dynamic_slice_in_dim(a, shard * width, width, axis=a.ndim - 1)

    upd_shapes = [(N_META, D // 4), (3, D // 4), (RANK, HEAD_K), (RANK, HEAD_K), (1, D), (1, 512), (1, 512),
                  (1, HEAD_V), (1, D)]
    upd_rows = _rows_for(upd_shapes)
    small_w = _pack([meta_tokens, conv_w[0], w_gate_fwd[0], w_gate_bwd[0], norm_pre, b_gate_fwd, b_gate_bwd,
                     gla_norm, norm_post], upd_rows)
    small_g = _pack([col_shard(g_meta, D // 4), col_shard(g_convw, D // 4), col_shard(g_wgf, HEAD_K),
                     col_shard(g_wgb, HEAD_K), g_npre, g_bgf, g_bgb, g_gnorm, g_npost], upd_rows)
    small_m = _pack([m_meta_tokens, m_conv_w[0], m_w_gate_fwd[0], m_w_gate_bwd[0], m_norm_pre, m_b_gate_fwd,
                     m_b_gate_bwd, m_gla_norm, m_norm_post], upd_rows)
    small_v = _pack([v_meta_tokens, v_conv_w[0], v_w_gate_fwd[0], v_w_gate_bwd[0], v_norm_pre, v_b_gate_fwd,
                     v_b_gate_bwd, v_gla_norm, v_norm_post], upd_rows)
    small_out = [_unpack(a, upd_shapes) for a in _adamw(small_w, [small_g], small_m, small_v, "adamw_small")]

    other_in, other_out = _plane_wait("swap_plane_sums_wait", swap_state, "swap", small_out[0][0])
    big_in = _adamw(jnp.transpose(w_in[0]), [plane_in, other_in], jnp.transpose(m_w_in[0]), jnp.transpose(v_w_in[0]),
                    "adamw_w_in")
    out_params = ((w_out_conv, m_w_out_conv, v_w_out_conv), (w_out_gla, m_w_out_gla, v_w_out_gla),
                  (w_merge_out, m_w_merge_out, v_w_merge_out))
    big_out = [_adamw(w[0], [plane_out, other_out], m[0], v[0], f"adamw_w_out_{i}", grad_row=i * (D // 4))
               for i, (w, m, v) in enumerate(out_params)]

    results = []
    for kind in range(4):
        sm = small_out[kind]
        w_in_part = jnp.transpose(big_in[kind])[None]
        outs3 = [big_out[i][kind][None] for i in range(3)]
        results.extend([
            sm[0], sm[4], w_in_part, sm[1][None], sm[2][None], sm[5], sm[3][None], sm[6], sm[7],
            outs3[0], outs3[1], outs3[2], sm[8]])
    return (loss, grad_x, *results)
```

```python
import functools

import jax
import jax.numpy as jnp
from jax import lax
from jax.experimental import pallas as pl
from jax.experimental.pallas import tpu as pltpu

F32 = jnp.float32
BF16 = jnp.bfloat16
MESH = pl.DeviceIdType.MESH

D = 1024
N_META = 16
CHUNK = 64
PAD_FRONT = CHUNK - N_META
N_HEADS = 4
HEAD_K = 128
HEAD_V = 256
RANK = 16
EPS = 1e-6
GATE_NORM = 16.0
N_IN = 9248
SHARD_IN = N_IN // 4
LANES = 128
N_CONV_TILES = 8
W_CONV = 4096
W_GLA = 2048
W_TAIL = 3072
N_MAIN = W_CONV + W_GLA + W_TAIL
OFF_Q, OFF_K, OFF_V, OFF_R = 4096, 4608, 5120, 6144
OFF_LR, OFF_MA, OFF_MB = 7168, 7200, 8224
MIB = 1024 * 1024

ADAM_LR = 0.001
ADAM_B1 = 0.9
ADAM_B2 = 0.999
ADAM_EPS = 1e-08
ADAM_WD = 0.01
ADAM_STEP = 10


def _params(sem=None, vmem_mib=None):
    return pltpu.CompilerParams(
        dimension_semantics=sem,
        vmem_limit_bytes=None if vmem_mib is None else vmem_mib * MIB)


def _pick_tile(n, target, mult):
    best = None
    for t in range(mult, min(n, target) + 1, mult):
        if n % t == 0:
            best = t
    return n if best is None else best


def _sigmoid(v):
    return 1.0 / (1.0 + jnp.exp(-v))


def _log_sigmoid(v):
    return jnp.minimum(v, 0.0) - jnp.log(1.0 + jnp.exp(-jnp.abs(v)))


def _dot(a, b):
    return jnp.dot(a, b, preferred_element_type=F32)


def _dot_nt(a, b):
    return lax.dot_general(a, b, (((1,), (1,)), ((), ())), preferred_element_type=F32)


def _dot_tn(a, b):
    return lax.dot_general(a, b, (((0,), (0,)), ((), ())), preferred_element_type=F32)


PLANE_FLIPS = ((1, 0, 0), (0, 1, 0), (1, 1, 0))
ALL_FLIPS = tuple((m >> 2 & 1, m >> 1 & 1, m & 1) for m in range(1, 8))
SIBLING_FLIPS = ((0, 0, 1),)


def _exchange(name, arrs, flips, slot_weights, mode):
    n = len(arrs)
    n_slots = 1
    for w in slot_weights:
        n_slots += w
    if mode == "gather":
        out_shape = [jax.ShapeDtypeStruct((n_slots,) + a.shape, a.dtype) for a in arrs]
    else:
        out_shape = [jax.ShapeDtypeStruct(a.shape, a.dtype) for a in arrs]

    def body(*refs):
        ins, outs = refs[:n], refs[n:2 * n]
        send_sems, recv_sems, local_sems = refs[2 * n:]
        pos = (lax.axis_index("x"), lax.axis_index("y"), lax.axis_index("c"))

        def slot_of(p):
            return p[0] * slot_weights[0] + p[1] * slot_weights[1] + p[2] * slot_weights[2]

        peers = [tuple(1 - pos[a] if f[a] else pos[a] for a in range(3)) for f in flips]
        me = slot_of(pos)
        local = []
        sends = []
        for i in range(n):
            if mode != "swap":
                src = ins[i] if mode == "gather" else ins[i].at[me]
                cp = pltpu.make_async_copy(src, outs[i].at[me], local_sems.at[i])
                cp.start()
                local.append(cp)
            for k, peer in enumerate(peers):
                if mode == "gather":
                    src, dst = ins[i], outs[i].at[me]
                elif mode == "scatter":
                    src, dst = ins[i].at[slot_of(peer)], outs[i].at[me]
                else:
                    src, dst = ins[i], outs[i]
                cp = pltpu.make_async_remote_copy(
                    src_ref=src, dst_ref=dst, send_sem=send_sems.at[i, k], recv_sem=recv_sems.at[i, k],
                    device_id=peer, device_id_type=MESH)
                cp.start()
                sends.append(cp)
        for i in range(n):
            for k, peer in enumerate(peers):
                if mode == "gather":
                    src, dst = ins[i], outs[i].at[slot_of(peer)]
                elif mode == "scatter":
                    src, dst = ins[i].at[me], outs[i].at[slot_of(peer)]
                else:
                    src, dst = ins[i], outs[i]
                arrival = pltpu.make_async_remote_copy(
                    src_ref=src, dst_ref=dst, send_sem=send_sems.at[i, k], recv_sem=recv_sems.at[i, k],
                    device_id=peer, device_id_type=MESH)
                arrival.wait_recv()
        for cp in sends:
            cp.wait_send()
        for cp in local:
            cp.wait()

    hbm = pl.BlockSpec(memory_space=pl.ANY)
    outs = pl.pallas_call(
        body, name=name, out_shape=out_shape,
        in_specs=[hbm] * n, out_specs=[hbm] * n,
        scratch_shapes=[pltpu.SemaphoreType.DMA((n, len(flips))),
                        pltpu.SemaphoreType.DMA((n, len(flips))),
                        pltpu.SemaphoreType.DMA((n,))],
        compiler_params=pltpu.CompilerParams(has_side_effects=True),
    )(*arrs)
    return list(outs)


def _gather_via_sibling(name, arrs, slotted):
    n = len(arrs)
    out_shape = [jax.ShapeDtypeStruct(a.shape if slotted[i] else (4,) + a.shape, a.dtype)
                 for i, a in enumerate(arrs)]

    def body(*refs):
        ins, outs = refs[:n], refs[n:2 * n]
        send_sems, recv_sems, local_sems = refs[2 * n:]
        x, y, c = lax.axis_index("x"), lax.axis_index("y"), lax.axis_index("c")
        me = 2 * x + y
        chips = [(1 - x, y), (x, 1 - y), (1 - x, 1 - y)]

        def half(ref, which):
            rows = ref.shape[0]
            cut = rows // 2 // 16 * 16
            return ref.at[pl.ds(0, cut)] if which == 0 else ref.at[pl.ds(cut, rows - cut)]

        def copy(src, dst, i, k, to):
            return pltpu.make_async_remote_copy(
                src_ref=src, dst_ref=dst, send_sem=send_sems.at[i, k], recv_sem=recv_sems.at[i, k],
                device_id=to, device_id_type=MESH)

        def run(mine):
            other = 1 - mine
            local, sends = [], []
            whole = [(not slotted[i]) and arrs[i].shape[0] < 32 for i in range(n)]
            for i in range(n):
                own = outs[i].at[me] if slotted[i] else ins[i]
                if not slotted[i]:
                    cp = pltpu.make_async_copy(ins[i], outs[i].at[me], local_sems.at[i])
                    cp.start()
                    local.append(cp)
                for k, (px, py) in enumerate(chips):
                    if whole[i]:
                        cp = copy(own, outs[i].at[me], i, k, (px, py, mine))
                    else:
                        cp = copy(half(own, mine), half(outs[i].at[me], mine), i, k, (px, py, mine))
                    cp.start()
                    sends.append(cp)
            for k, (px, py) in enumerate(chips):
                slot = 2 * px + py
                for i in range(n):
                    if whole[i]:
                        copy(outs[i].at[slot], outs[i].at[slot], i, k, (px, py, mine)).wait_recv()
                        continue
                    landed = half(outs[i].at[slot], mine)
                    copy(landed, landed, i, k, (px, py, mine)).wait_recv()
                    cp = copy(landed, landed, i, 3 + k, (x, y, other))
                    cp.start()
                    sends.append(cp)
            for k, (px, py) in enumerate(chips):
                slot = 2 * px + py
                for i in range(n):
                    if whole[i]:
                        continue
                    passed = half(outs[i].at[slot], other)
                    copy(passed, passed, i, 3 + k, (x, y, other)).wait_recv()
            for cp in sends:
                cp.wait_send()
            for cp in local:
                cp.wait()

        for mine in (0, 1):
            pl.when(c == mine)(functools.partial(run, mine))

    hbm = pl.BlockSpec(memory_space=pl.ANY)
    outs = pl.pallas_call(
        body, name=name, out_shape=out_shape,
        in_specs=[hbm] * n, out_specs=[hbm] * n,
        scratch_shapes=[pltpu.SemaphoreType.DMA((n, 6)), pltpu.SemaphoreType.DMA((n, 6)),
                        pltpu.SemaphoreType.DMA((n,))],
        input_output_aliases={i: i for i in range(n) if slotted[i]},
        compiler_params=pltpu.CompilerParams(has_side_effects=True),
    )(*arrs)
    return list(outs)


HBM_SPEC = pl.BlockSpec(memory_space=pltpu.HBM)
SEM_SPEC = pl.BlockSpec(memory_space=pltpu.SEMAPHORE)
DATAFLOW = pltpu.SideEffectType.DATAFLOW_SIDE_EFFECTING


def _split_peers(mode):
    x, y, c = lax.axis_index("x"), lax.axis_index("y"), lax.axis_index("c")
    if mode == "swap":
        return 0, [((x, y, 1 - c), 0)]
    return 2 * x + y, [((1 - x, y, c), 2 * (1 - x) + y), ((x, 1 - y, c), 2 * x + 1 - y),
                       ((1 - x, 1 - y, c), 2 * (1 - x) + 1 - y)]


def _split_refs(mode, src, landing, me, peer_slot):
    if mode == "gather":
        return src, landing.at[me]
    if mode == "scatter":
        return src.at[peer_slot], landing.at[me]
    return src, landing


def _plane_start(name, arrs, mode, after):
    n = len(arrs)
    n_peers = 1 if mode == "swap" else 3
    lands = [lax.empty(((4,) + a.shape) if mode == "gather" else a.shape, a.dtype) for a in arrs]

    def body(*refs):
        srcs, landing = refs[:n], refs[n:2 * n]
        send_sems, recv_sems = refs[2 * n + 1], refs[2 * n + 2]
        token = refs[-1]
        me, peers = _split_peers(mode)
        for i in range(n):
            for k, (peer, peer_slot) in enumerate(peers):
                src, dst = _split_refs(mode, srcs[i], landing[i], me, peer_slot)
                pltpu.make_async_remote_copy(
                    src_ref=src, dst_ref=dst, send_sem=send_sems.at[n_peers * i + k],
                    recv_sem=recv_sems.at[n_peers * i + k], device_id=peer, device_id_type=MESH).start()
        token[...] = jnp.zeros_like(token)

    hbm_in = [pltpu.with_memory_space_constraint(a, pltpu.HBM) for a in list(arrs) + lands]
    out = pl.pallas_call(
        body, name=name,
        out_shape=[pltpu.SemaphoreType.DMA((n_peers * n,)), pltpu.SemaphoreType.DMA((n_peers * n,))]
                  + [pltpu.HBM(a.shape, a.dtype) for a in hbm_in]
                  + [jax.ShapeDtypeStruct((8, LANES), F32)],
        in_specs=[HBM_SPEC] * (2 * n) + [pl.BlockSpec(memory_space=pl.ANY)],
        out_specs=[SEM_SPEC, SEM_SPEC] + [HBM_SPEC] * (2 * n) + [pl.BlockSpec(memory_space=pltpu.VMEM)],
        input_output_aliases={i: 2 + i for i in range(2 * n)},
        compiler_params=pltpu.CompilerParams(has_side_effects=DATAFLOW),
    )(*hbm_in, after)
    return out[:-1], out[-1]


def _plane_wait(name, state, mode, after):
    send_sems, recv_sems = state[0], state[1]
    bufs = list(state[2:])
    n = len(bufs) // 2
    n_peers = 1 if mode == "swap" else 3

    def body(*refs):
        srcs, landing = refs[:n], refs[n:2 * n]
        send_sems, recv_sems = refs[2 * n], refs[2 * n + 1]
        me, peers = _split_peers(mode)
        for i in range(n):
            for k, (peer, peer_slot) in enumerate(peers):
                src, _ = _split_refs(mode, srcs[i], landing[i], me, peer_slot)
                arrived = landing[i] if mode == "swap" else landing[i].at[peer_slot]
                cp = pltpu.make_async_remote_copy(
                    src_ref=src, dst_ref=arrived, send_sem=send_sems.at[n_peers * i + k],
                    recv_sem=recv_sems.at[n_peers * i + k], device_id=peer, device_id_type=MESH)
                cp.wait_send()
                cp.wait_recv()

    out = pl.pallas_call(
        body, name=name,
        out_shape=[pltpu.HBM(a.shape, a.dtype) for a in bufs],
        in_specs=[HBM_SPEC] * (2 * n) + [SEM_SPEC, SEM_SPEC, pl.BlockSpec(memory_space=pl.ANY)],
        out_specs=[HBM_SPEC] * (2 * n),
        input_output_aliases={i: i for i in range(2 * n)},
        compiler_params=pltpu.CompilerParams(has_side_effects=DATAFLOW),
    )(*bufs, send_sems, recv_sems, after)
    return list(out[n:])


def _tile_2d(rows, cols, row_mult, max_elems=512 * 1024):
    if rows % row_mult == 0:
        rt = _pick_tile(rows, max(row_mult, max_elems // cols), row_mult)
        return (rt, cols), rows // rt, lambda i: (i, 0)
    ct = _pick_tile(cols, max(LANES, max_elems // rows), LANES)
    return (rows, ct), cols // ct, lambda i: (0, i)


def _cast_bf16(a, name):
    block, steps, index = _tile_2d(a.shape[0], a.shape[1], 16)

    def body(a_ref, o_ref):
        o_ref[...] = a_ref[...].astype(BF16)

    return pl.pallas_call(
        body, name=name, grid=(steps,),
        in_specs=[pl.BlockSpec(block, index)],
        out_specs=pl.BlockSpec(block, index),
        out_shape=jax.ShapeDtypeStruct(a.shape, BF16),
        compiler_params=_params(("parallel",)),
    )(a)


def _cast_into_slot(a, slot, name):
    block, steps, index = _tile_2d(a.shape[0], a.shape[1], 16)

    def body(slot_ref, a_ref, o_ref):
        o_ref[...] = a_ref[...].astype(BF16)

    return pl.pallas_call(
        body, name=name,
        grid_spec=pltpu.PrefetchScalarGridSpec(
            num_scalar_prefetch=1, grid=(steps,),
            in_specs=[pl.BlockSpec(block, lambda i, s: index(i))],
            out_specs=pl.BlockSpec((None,) + block, lambda i, s: (s[0],) + index(i))),
        out_shape=jax.ShapeDtypeStruct((4,) + a.shape, BF16),
        compiler_params=_params(("arbitrary",)),
    )(slot, a)


def _sum_slots(buf, name, own=None, slot=None):
    n_slots, rows, cols = buf.shape
    (br, bc), steps, index = _tile_2d(rows, cols, 16, 320 * 1024)

    def body(*refs):
        if own is None:
            b_ref, o_ref = refs
        else:
            slot_ref, b_ref, own_ref, o_ref = refs
        acc = None
        for s in range(n_slots):
            term = b_ref[s] if own is None else jnp.where(slot_ref[0] == s, own_ref[...], b_ref[s])
            acc = term.astype(F32) if acc is None else acc + term.astype(F32)
        o_ref[...] = acc

    out_shape = jax.ShapeDtypeStruct((rows, cols), F32)
    if own is None:
        return pl.pallas_call(
            body, name=name, grid=(steps,),
            in_specs=[pl.BlockSpec((n_slots, br, bc), lambda i: (0,) + index(i))],
            out_specs=pl.BlockSpec((br, bc), index), out_shape=out_shape,
            compiler_params=_params(("parallel",), 48),
        )(buf)
    return pl.pallas_call(
        body, name=name,
        grid_spec=pltpu.PrefetchScalarGridSpec(
            num_scalar_prefetch=1, grid=(steps,),
            in_specs=[pl.BlockSpec((n_slots, br, bc), lambda i, s: (0,) + index(i)),
                      pl.BlockSpec((None, br, bc), lambda i, s: (s[0],) + index(i))],
            out_specs=pl.BlockSpec((br, bc), lambda i, s: index(i))),
        out_shape=out_shape,
        compiler_params=_params(("arbitrary",), 48),
    )(slot, buf, own)


def _sum_small(bufs, name):
    n = len(bufs)

    def body(*refs):
        for b_ref, o_ref in zip(refs[:n], refs[n:]):
            acc = b_ref[0]
            for s in range(1, b_ref.shape[0]):
                acc = acc + b_ref[s]
            o_ref[...] = acc

    vmem = pl.BlockSpec(memory_space=pltpu.VMEM)
    return pl.pallas_call(
        body, name=name, in_specs=[vmem] * n, out_specs=[vmem] * n,
        out_shape=[jax.ShapeDtypeStruct(b.shape[1:], b.dtype) for b in bufs],
    )(*bufs)


def _adam_update(w, g, m, v):
    c1 = 1.0 - ADAM_B1 ** ADAM_STEP
    c2 = 1.0 - ADAM_B2 ** ADAM_STEP
    m_new = ADAM_B1 * m + (1.0 - ADAM_B1) * g
    v_new = ADAM_B2 * v + (1.0 - ADAM_B2) * (g * g)
    m_hat = m_new / c1
    v_hat = v_new / c2
    return -ADAM_LR * (m_hat / (jnp.sqrt(v_hat) + ADAM_EPS) + ADAM_WD * w), m_new, v_new


def _adamw_small(params, slot, name):
    n = len(params)

    def spec_of(shape):
        lead = (None,) * (len(shape) - 2)
        return pl.BlockSpec(lead + tuple(shape[-2:]), lambda i, s, k=len(shape): (0,) * k)

    in_specs, operands, out_specs, out_shape = [], [], [], []
    for w, g, m, v in params:
        shard = g.shape[-1] != w.shape[-1]
        g_spec = pl.BlockSpec(tuple(w.shape[-2:]), (lambda i, s: (0, s[0])) if shard else (lambda i, s: (0, 0)))
        in_specs += [spec_of(w.shape), g_spec, spec_of(m.shape), spec_of(v.shape)]
        operands += [w, g, m, v]
        out_specs += [spec_of(w.shape)] * 4
        out_shape += [jax.ShapeDtypeStruct(w.shape, F32)] * 4

    def body(slot_ref, *refs):
        ins, outs = refs[:4 * n], refs[4 * n:]
        for p in range(n):
            w_ref, g_ref, m_ref, v_ref = ins[4 * p:4 * p + 4]
            g = g_ref[...]
            delta, m_new, v_new = _adam_update(w_ref[...], g, m_ref[...], v_ref[...])
            for o_ref, val in zip(outs[4 * p:4 * p + 4], (g, delta, m_new, v_new)):
                o_ref[...] = val

    out = pl.pallas_call(
        body, name=name,
        grid_spec=pltpu.PrefetchScalarGridSpec(num_scalar_prefetch=1, grid=(1,), in_specs=in_specs, out_specs=out_specs),
        out_shape=out_shape,
    )(slot, *operands)
    return [tuple(out[4 * p:4 * p + 4]) for p in range(n)]


def _adamw(w, grads, m, v, name, grad_row=0):
    rows, cols = w.shape
    (rt, _), _, _ = _tile_2d(rows, cols, 8, 160 * 1024)
    assert grad_row % rt == 0
    n_g = len(grads)

    def body(*refs):
        w_ref = refs[0]
        g_refs = refs[1:1 + n_g]
        m_ref, v_ref, g_out, d_out, m_out, v_out = refs[1 + n_g:]
        g = g_refs[0][...]
        for r in g_refs[1:]:
            g = g + r[...]
        g_out[...] = g
        d_out[...], m_out[...], v_out[...] = _adam_update(w_ref[...], g, m_ref[...], v_ref[...])

    spec = pl.BlockSpec((rt, cols), lambda i: (i, 0))
    grad_spec = pl.BlockSpec((rt, cols), lambda i: (i + grad_row // rt, 0))
    shape = jax.ShapeDtypeStruct((rows, cols), F32)
    return pl.pallas_call(
        body, name=name, grid=(rows // rt,),
        in_specs=[spec] + [grad_spec] * n_g + [spec] * 2, out_specs=[spec] * 4, out_shape=[shape] * 4,
        compiler_params=_params(("parallel",), 48),
    )(w, *grads, m, v)


def _weight_pieces():
    pieces = []
    for j in range(N_CONV_TILES):
        for g in range(4):
            pieces.append((512 * j + 128 * g, D * g + 128 * j, 128))
    for hd in range(N_HEADS):
        base = W_CONV + 512 * hd
        pieces.append((base, OFF_Q + HEAD_K * hd, HEAD_K))
        pieces.append((base + HEAD_K, OFF_K + HEAD_K * hd, HEAD_K))
        pieces.append((base + 2 * HEAD_K, OFF_V + HEAD_V * hd, HEAD_V))
    pieces.append((W_CONV + W_GLA, OFF_R, D))
    pieces.append((W_CONV + W_GLA + D, OFF_MA, 2 * D))
    return pieces


N_WEIGHT_COPIES = len(_weight_pieces()) + 1


def _load_weights(w_hbm, w_s, wlr_s, sems):
    copies = [pltpu.make_async_copy(w_hbm.at[pl.ds(src, n)], w_s.at[pl.ds(dst, n)], sems.at[i])
              for i, (dst, src, n) in enumerate(_weight_pieces())]
    copies.append(pltpu.make_async_copy(w_hbm.at[pl.ds(OFF_LR, LANES)], wlr_s, sems.at[N_WEIGHT_COPIES - 1]))
    for cp in copies:
        cp.start()
    for cp in copies:
        cp.wait()


def _in_proj(h, g_pre, w_full_t):
    t_rows = h.shape[0]
    tm = _pick_tile(t_rows, 384, LANES)
    n_main = N_MAIN

    def body(h_ref, g_ref, w_hbm, proj_ref, ut_ref, lr_ref, w_s, wlr_s, w_sems):
        @pl.when(pl.program_id(0) == 0)
        def _():
            _load_weights(w_hbm, w_s, wlr_s, w_sems)

        hh = h_ref[...]
        rstd = lax.rsqrt(jnp.mean(hh * hh, axis=-1, keepdims=True) + EPS)
        uf = hh * rstd * g_ref[...]
        u = uf.astype(BF16)
        ut_ref[...] = jnp.transpose(uf).astype(BF16)
        lr_ref[...] = _dot_nt(u, wlr_s[...])
        for j in range(n_main // D):
            cols = slice(j * D, (j + 1) * D)
            proj_ref[:, cols] = _dot_nt(u, w_s[cols, :]).astype(BF16)

    return pl.pallas_call(
        body, name="in_proj", grid=(t_rows // tm,),
        in_specs=[pl.BlockSpec((tm, D), lambda i: (i, 0)),
                  pl.BlockSpec((1, D), lambda i: (0, 0)),
                  pl.BlockSpec(memory_space=pl.ANY)],
        out_specs=[pl.BlockSpec((tm, n_main), lambda i: (i, 0)),
                   pl.BlockSpec((D, tm), lambda i: (0, i)),
                   pl.BlockSpec((tm, LANES), lambda i: (i, 0))],
        out_shape=[jax.ShapeDtypeStruct((t_rows, n_main), BF16),
                   jax.ShapeDtypeStruct((D, t_rows), BF16),
                   jax.ShapeDtypeStruct((t_rows, LANES), F32)],
        scratch_shapes=[pltpu.VMEM((n_main, D), BF16), pltpu.VMEM((LANES, D), BF16),
                        pltpu.SemaphoreType.DMA((N_WEIGHT_COPIES,))],
        compiler_params=_params(("arbitrary",), 56),
    )(h, g_pre, w_full_t)


def _conv_parts(p_ref, w_ref):
    cb = p_ref[:, 0:128].astype(F32)
    cc = p_ref[:, 128:256].astype(F32)
    cx = p_ref[:, 256:384].astype(F32)
    cz = p_ref[:, 384:512].astype(F32)
    rows = cb.shape[0]
    w = w_ref[...]
    p = cc * cx
    conv = pltpu.roll(p, 1, 0) * w[0:1] + p * w[1:2] + pltpu.roll(p, rows - 1, 0) * w[2:3]
    sz = _sigmoid(cz)
    return cb, cc, cx, cz, p, conv, sz, w


def _conv_fwd(proj, conv_w, n_seq, lf):
    def body(p_ref, w_ref, y_ref):
        cb, _, _, cz, _, conv, sz, _ = _conv_parts(p_ref, w_ref)
        y_ref[...] = (cb * conv * (cz * sz)).astype(BF16)

    return pl.pallas_call(
        body, name="conv_fwd", grid=(n_seq, N_CONV_TILES),
        in_specs=[pl.BlockSpec((lf, 512), lambda b, j: (b, j)),
                  pl.BlockSpec((3, 128), lambda b, j: (0, j))],
        out_specs=pl.BlockSpec((lf, 128), lambda b, j: (b, j)),
        out_shape=jax.ShapeDtypeStruct((n_seq * lf, D), BF16),
        compiler_params=_params(("parallel", "parallel"), 48),
    )(proj, conv_w)


def _conv_bwd(proj, conv_w, dyc, n_seq, lf):
    def body(p_ref, w_ref, dy_ref, dp_ref, dw_ref):
        cb, cc, cx, cz, p, conv, sz, w = _conv_parts(p_ref, w_ref)
        rows = cb.shape[0]
        dy = dy_ref[...].astype(F32)
        silu = cz * sz
        dcb = dy * conv * silu
        dconv = dy * cb * silu
        dcz = dy * cb * conv * (sz * (1.0 + cz * (1.0 - sz)))
        d_next = pltpu.roll(dconv, rows - 1, 0)
        d_prev = pltpu.roll(dconv, 1, 0)
        dp = d_next * w[0:1] + dconv * w[1:2] + d_prev * w[2:3]
        dp_ref[:, 0:128] = dcb.astype(BF16)
        dp_ref[:, 128:256] = (dp * cx).astype(BF16)
        dp_ref[:, 256:384] = (dp * cc).astype(BF16)
        dp_ref[:, 384:512] = dcz.astype(BF16)
        dw_ref[0:1, :] = jnp.sum(dconv * pltpu.roll(p, 1, 0), axis=0, keepdims=True)
        dw_ref[1:2, :] = jnp.sum(dconv * p, axis=0, keepdims=True)
        dw_ref[2:3, :] = jnp.sum(dconv * pltpu.roll(p, rows - 1, 0), axis=0, keepdims=True)

    return pl.pallas_call(
        body, name="conv_bwd", grid=(n_seq, N_CONV_TILES),
        in_specs=[pl.BlockSpec((lf, 512), lambda b, j: (b, j)),
                  pl.BlockSpec((3, 128), lambda b, j: (0, j)),
                  pl.BlockSpec((lf, 128), lambda b, j: (b, j))],
        out_specs=[pl.BlockSpec((lf, 512), lambda b, j: (b, j)),
                   pl.BlockSpec((None, 3, 128), lambda b, j: (b, 0, j))],
        out_shape=[jax.ShapeDtypeStruct((n_seq * lf, W_CONV), BF16),
                   jax.ShapeDtypeStruct((n_seq, 3, D), F32)],
        compiler_params=_params(("parallel", "parallel"), 48),
    )(proj, conv_w, dyc)


GROUP = 3
GROUP_ROWS = GROUP * CHUNK


def _row_group(shape):
    row = lax.broadcasted_iota(jnp.int32, shape, 0)
    grp = jnp.zeros(shape, jnp.int32)
    for r in range(1, GROUP):
        grp = grp + (row >= r * CHUNK).astype(jnp.int32)
    return grp


def _lane_group(shape, width):
    lane = lax.broadcasted_iota(jnp.int32, shape, 1)
    grp = jnp.zeros(shape, jnp.int32)
    for r in range(1, GROUP):
        grp = grp + (lane >= r * width).astype(jnp.int32)
    return grp


def _score_mask(direction):
    shape = (GROUP_ROWS, GROUP_ROWS)
    row = lax.broadcasted_iota(jnp.int32, shape, 0)
    col = lax.broadcasted_iota(jnp.int32, shape, 1)
    same = _row_group(shape) == _lane_group(shape, CHUNK)
    return same & ((col <= row) if direction == 0 else (col > row))


def _diag_blocks(v):
    w = v.shape[1]
    wide = jnp.concatenate([v] * GROUP, axis=1)
    return jnp.where(_row_group(wide.shape) == _lane_group(wide.shape, w), wide, jnp.zeros_like(wide))


def _per_chunk_dot(lhs, state, transposed):
    outs = []
    for r in range(GROUP):
        rows = lhs[r * CHUNK:(r + 1) * CHUNK, :]
        blk = state[:, r * HEAD_K:(r + 1) * HEAD_K]
        outs.append(_dot_nt(rows, blk) if transposed else _dot(rows, blk))
    return jnp.concatenate(outs, axis=0)


def _chunk_cumsum(v, suffix):
    pos = lax.broadcasted_iota(jnp.int32, v.shape, 0) & (CHUNK - 1)
    shift = 1
    while shift < CHUNK:
        if suffix:
            moved = pltpu.roll(v, GROUP_ROWS - shift, 0)
            v = v + jnp.where(pos < CHUNK - shift, moved, 0.0)
        else:
            moved = pltpu.roll(v, shift, 0)
            v = v + jnp.where(pos >= shift, moved, 0.0)
        shift *= 2
    return v


def _per_chunk_rows(rows_of_chunk):
    w = rows_of_chunk[0].shape[1]
    return jnp.concatenate([jnp.broadcast_to(v, (CHUNK, w)) for v in rows_of_chunk], axis=0)


def _chunk_end_rows(direction, b):
    at = CHUNK - 1 if direction == 0 else 0
    return [b[r * CHUNK + at:r * CHUNK + at + 1, :] for r in range(GROUP)]


def _gla_gates(lr_bf, wg_ref, bg_ref, lf):
    z = _dot(lr_bf, wg_ref[...]) + bg_ref[...]
    valid = lax.broadcasted_iota(jnp.int32, (lf, HEAD_K), 0) >= PAD_FRONT
    return z, valid


def _group_unroll(n_groups):
    return n_groups if n_groups <= 11 else 1


def _group_rows(g):
    return pl.ds(pl.multiple_of(g * GROUP_ROWS, GROUP_ROWS), GROUP_ROWS)


def _chunk_decay(direction, g, r, b_s):
    base = g * GROUP_ROWS + r * CHUNK
    if direction == 0:
        grp = b_s[pl.ds(pl.multiple_of(base + CHUNK - 8, 8), 8), :]
        return jnp.exp(grp[7:8, :])
    grp = b_s[pl.ds(pl.multiple_of(base, 8), 8), :]
    return jnp.exp(grp[0:1, :])


def _state_scan(direction, n_groups, b_s, st_s, reverse):
    ascending = (direction == 0) != reverse

    def step(i, carry):
        g = i if ascending else n_groups - 1 - i
        for rr in range(GROUP):
            r = rr if ascending else GROUP - 1 - rr
            lanes = slice(r * HEAD_K, (r + 1) * HEAD_K)
            decay = _chunk_decay(direction, g, r, b_s)
            local = st_s[g, :, lanes]
            st_s[g, :, lanes] = carry
            carry = (local + carry * decay) if reverse else (carry * decay + local)
        return carry

    lax.fori_loop(0, n_groups, step, jnp.zeros((HEAD_V, HEAD_K), F32))


def _gla_states(direction, n_groups, qkv_ref, g_s, b_s, st_s):
    def local(g, carry):
        rows = _group_rows(g)
        b = _chunk_cumsum(g_s[rows, :], direction == 1)
        b_s[rows, :] = b
        b_end = _per_chunk_rows(_chunk_end_rows(direction, b))
        k = qkv_ref[rows, 128:256].astype(F32)
        v = qkv_ref[rows, 256:512]
        k_dec = (k * jnp.exp(b_end - b)).astype(BF16)
        st_s[g] = _dot_tn(v, _diag_blocks(k_dec))
        return carry

    lax.fori_loop(0, n_groups, local, 0, unroll=_group_unroll(n_groups))
    _state_scan(direction, n_groups, b_s, st_s, False)


def _gla_fwd(proj, lr, wgf, wgb, bgf, bgb, n_seq, lf):
    assert lf % GROUP_ROWS == 0
    n_groups = lf // GROUP_ROWS
    scale = HEAD_K ** -0.5

    def body(qkv_ref, lr_ref, wgf_ref, wgb_ref, bgf_ref, bgb_ref, o_ref, g_s, b_s, st_s):
        lr_bf = lr_ref[...].astype(BF16)
        for direction in (0, 1):
            wg_ref, bg_ref = ((wgf_ref, bgf_ref), (wgb_ref, bgb_ref))[direction]
            z, valid = _gla_gates(lr_bf, wg_ref, bg_ref, lf)
            g_s[...] = jnp.where(valid, _log_sigmoid(z) / GATE_NORM, 0.0)
            smask = _score_mask(direction)
            _gla_states(direction, n_groups, qkv_ref, g_s, b_s, st_s)

            def out(g, carry):
                rows = _group_rows(g)
                b = b_s[rows, :]
                q = qkv_ref[rows, 0:128].astype(F32) * scale
                k = qkv_ref[rows, 128:256].astype(F32)
                v = qkv_ref[rows, 256:512]
                q_in = (q * jnp.exp(b)).astype(BF16)
                k_in = (k * jnp.exp(-b)).astype(BF16)
                s = jnp.where(smask, _dot_nt(q_in, k_in), 0.0).astype(BF16)
                o = _dot(s, v) + _per_chunk_dot(q_in, st_s[g].astype(BF16), True)
                if direction == 0:
                    o_ref[rows, :] = o
                else:
                    o_ref[rows, :] = o_ref[rows, :] + o
                return carry

            lax.fori_loop(0, n_groups, out, 0, unroll=_group_unroll(n_groups))

    return pl.pallas_call(
        body, name="gla_fwd", grid=(n_seq, N_HEADS),
        in_specs=[pl.BlockSpec((lf, 512), lambda b, h: (b, N_CONV_TILES + h)),
                  pl.BlockSpec((lf, LANES), lambda b, h: (b, 0)),
                  pl.BlockSpec((None, LANES, HEAD_K), lambda b, h: (h, 0, 0)),
                  pl.BlockSpec((None, LANES, HEAD_K), lambda b, h: (h, 0, 0)),
                  pl.BlockSpec((None, 1, HEAD_K), lambda b, h: (h, 0, 0)),
                  pl.BlockSpec((None, 1, HEAD_K), lambda b, h: (h, 0, 0))],
        out_specs=pl.BlockSpec((lf, HEAD_V), lambda b, h: (b, h)),
        out_shape=jax.ShapeDtypeStruct((n_seq * lf, D), F32),
        scratch_shapes=[pltpu.VMEM((lf, HEAD_K), F32), pltpu.VMEM((lf, HEAD_K), F32),
                        pltpu.VMEM((n_groups, HEAD_V, GROUP * HEAD_K), F32)],
        compiler_params=_params(("parallel", "parallel"), 48),
    )(proj, lr, wgf, wgb, bgf, bgb)


def _gla_bwd(proj, lr, d_o, wgf, wgb, bgf, bgb, n_seq, lf, token):
    assert lf % GROUP_ROWS == 0
    n_groups = lf // GROUP_ROWS
    scale = HEAD_K ** -0.5

    def body(qkv_ref, lr_ref, do_ref, wgf_ref, wgb_ref, bgf_ref, bgb_ref, token_ref,
             dqkv_ref, dlr_ref, dwgf_ref, dwgb_ref, dbg_ref,
             g_s, b_s, fac_s, dg_s, st_s, dst_s, acc_s):
        lr_bf = lr_ref[...].astype(BF16)
        dlr = jnp.zeros((lf, LANES), F32)
        for direction in (0, 1):
            wg_ref, bg_ref = ((wgf_ref, bgf_ref), (wgb_ref, bgb_ref))[direction]
            z, valid = _gla_gates(lr_bf, wg_ref, bg_ref, lf)
            g_s[...] = jnp.where(valid, _log_sigmoid(z) / GATE_NORM, 0.0)
            fac_s[...] = jnp.where(valid, _sigmoid(-z) / GATE_NORM, 0.0)
            smask = _score_mask(direction)
            end_row = CHUNK - 1 if direction == 0 else 0
            _gla_states(direction, n_groups, qkv_ref, g_s, b_s, st_s)

            def state_grad_local(g, carry):
                rows = _group_rows(g)
                q = qkv_ref[rows, 0:128].astype(F32) * scale
                q_in = (q * jnp.exp(b_s[rows, :])).astype(BF16)
                dst_s[g] = _dot_tn(do_ref[rows, :], _diag_blocks(q_in))
                return carry

            lax.fori_loop(0, n_groups, state_grad_local, 0, unroll=_group_unroll(n_groups))
            _state_scan(direction, n_groups, b_s, dst_s, True)

            def group_grads(g, carry):
                rows = _group_rows(g)
                b = b_s[rows, :]
                ends = _chunk_end_rows(direction, b)
                b_end = _per_chunk_rows(ends)
                q = qkv_ref[rows, 0:128].astype(F32) * scale
                k = qkv_ref[rows, 128:256].astype(F32)
                v = qkv_ref[rows, 256:512]
                d_out = do_ref[rows, :]
                e_pos = jnp.exp(b)
                e_neg = jnp.exp(-b)
                e_end = jnp.exp(b_end - b)
                q_in = q * e_pos
                k_in = k * e_neg
                k_dec = k * e_end
                q_in_bf = q_in.astype(BF16)
                k_in_bf = k_in.astype(BF16)
                state = st_s[g]
                d_state = dst_s[g]
                state_bf = state.astype(BF16)
                d_state_bf = d_state.astype(BF16)
                s = jnp.where(smask, _dot_nt(q_in_bf, k_in_bf), 0.0).astype(BF16)
                ds = jnp.where(smask, _dot_nt(d_out, v), 0.0).astype(BF16)
                dv = _dot_tn(s, d_out) + _per_chunk_dot(k_dec.astype(BF16), d_state_bf, True)
                dq_in = _dot(ds, k_in_bf) + _per_chunk_dot(d_out, state_bf, False)
                dk_in = _dot_tn(ds, q_in_bf)
                dk_dec = _per_chunk_dot(v, d_state_bf, False)
                dq = dq_in * e_pos * scale
                dk = dk_in * e_neg + dk_dec * e_end
                if direction == 0:
                    acc_s[rows, 0:128] = dq
                    acc_s[rows, 128:256] = dk
                    acc_s[rows, 256:512] = dv
                else:
                    dqkv_ref[rows, 0:128] = (acc_s[rows, 0:128] + dq).astype(BF16)
                    dqkv_ref[rows, 128:256] = (acc_s[rows, 128:256] + dk).astype(BF16)
                    dqkv_ref[rows, 256:512] = (acc_s[rows, 256:512] + dv).astype(BF16)
                dkk = dk_dec * k_dec
                db = dq_in * q_in - dk_in * k_in - dkk
                d_decay = jnp.sum(d_state * state, axis=0, keepdims=True)
                db_end = [jnp.sum(dkk[r * CHUNK:(r + 1) * CHUNK, :], axis=0, keepdims=True)
                          + d_decay[:, r * HEAD_K:(r + 1) * HEAD_K] * jnp.exp(ends[r]) for r in range(GROUP)]
                row = lax.broadcasted_iota(jnp.int32, (GROUP_ROWS, HEAD_K), 0)
                at_end = row == end_row
                for r in range(1, GROUP):
                    at_end = at_end | (row == r * CHUNK + end_row)
                db = db + jnp.where(at_end, _per_chunk_rows(db_end), 0.0)
                dg_s[rows, :] = _chunk_cumsum(db, direction == 0)
                return carry

            lax.fori_loop(0, n_groups, group_grads, 0, unroll=_group_unroll(n_groups))

            dz = dg_s[...] * fac_s[...]
            dz_bf = dz.astype(BF16)
            dbg_ref[direction:direction + 1, :] = jnp.sum(dz, axis=0, keepdims=True)
            (dwgf_ref, dwgb_ref)[direction][...] = _dot_tn(lr_bf, dz_bf)
            dlr = dlr + _dot_nt(dz_bf, wg_ref[...])

        @pl.when(pl.program_id(1) == 0)
        def _():
            dlr_ref[...] = dlr

        @pl.when(pl.program_id(1) != 0)
        def _():
            dlr_ref[...] = dlr_ref[...] + dlr

    gate_w = pl.BlockSpec((None, LANES, HEAD_K), lambda b, h: (h, 0, 0))
    gate_b = pl.BlockSpec((None, 1, HEAD_K), lambda b, h: (h, 0, 0))
    return pl.pallas_call(
        body, name="gla_bwd", grid=(n_seq, N_HEADS),
        in_specs=[pl.BlockSpec((lf, 512), lambda b, h: (b, N_CONV_TILES + h)),
                  pl.BlockSpec((lf, LANES), lambda b, h: (b, 0)),
                  pl.BlockSpec((lf, HEAD_V), lambda b, h: (b, h)),
                  gate_w, gate_w, gate_b, gate_b,
                  pl.BlockSpec((8, LANES), lambda b, h: (0, 0))],
        out_specs=[pl.BlockSpec((lf, 512), lambda b, h: (b, h)),
                   pl.BlockSpec((lf, LANES), lambda b, h: (b, 0)),
                   pl.BlockSpec((None, None, LANES, HEAD_K), lambda b, h: (b, h, 0, 0)),
                   pl.BlockSpec((None, None, LANES, HEAD_K), lambda b, h: (b, h, 0, 0)),
                   pl.BlockSpec((None, None, 2, HEAD_K), lambda b, h: (b, h, 0, 0))],
        out_shape=[jax.ShapeDtypeStruct((n_seq * lf, W_GLA), BF16),
                   jax.ShapeDtypeStruct((n_seq * lf, LANES), F32),
                   jax.ShapeDtypeStruct((n_seq, N_HEADS, LANES, HEAD_K), F32),
                   jax.ShapeDtypeStruct((n_seq, N_HEADS, LANES, HEAD_K), F32),
                   jax.ShapeDtypeStruct((n_seq, N_HEADS, 2, HEAD_K), F32)],
        scratch_shapes=[pltpu.VMEM((lf, HEAD_K), F32), pltpu.VMEM((lf, HEAD_K), F32),
                        pltpu.VMEM((lf, HEAD_K), F32), pltpu.VMEM((lf, HEAD_K), F32),
                        pltpu.VMEM((n_groups, HEAD_V, GROUP * HEAD_K), F32),
                        pltpu.VMEM((n_groups, HEAD_V, GROUP * HEAD_K), F32),
                        pltpu.VMEM((lf, 512), F32)],
        compiler_params=_params(("parallel", "arbitrary"), 56),
    )(proj, lr, d_o, wgf, wgb, bgf, bgb, token)


def _tail(h, tgt, yc, o, proj, w3, gamma, g_post, lf):
    t_rows = h.shape[0]
    tm = _pick_tile(t_rows, 256, CHUNK)
    n_chunks = lf // CHUNK
    per_tile = tm // CHUNK

    def body(h_ref, tgt_ref, yc_ref, o_ref, r_ref, ma_ref, mb_ref, w_hbm, gamma_ref, gpost_ref,
             dres_ref, yg_ref, merged_ref, dout_ref, dpc_ref, dpg_ref, dyc_ref, do_ref, dtail_ref,
             loss_ref, dgpost_ref, dgamma_ref, w_s, w_sem):
        i = pl.program_id(0)

        @pl.when(i == 0)
        def _():
            cp = pltpu.make_async_copy(w_hbm, w_s, w_sem)
            cp.start()
            cp.wait()
            loss_ref[...] = jnp.zeros_like(loss_ref)
            dgpost_ref[...] = jnp.zeros_like(dgpost_ref)
            dgamma_ref[...] = jnp.zeros_like(dgamma_ref)

        gamma = gamma_ref[...]
        o = o_ref[...]
        r = r_ref[...].astype(F32)
        sr = _sigmoid(r)
        silu_r = r * sr
        n_parts, rstd_parts = [], []
        for hd in range(N_HEADS):
            oh = o[:, hd * HEAD_V:(hd + 1) * HEAD_V]
            rstd = lax.rsqrt(jnp.mean(oh * oh, axis=-1, keepdims=True) + EPS)
            n_parts.append(oh * rstd)
            rstd_parts.append(rstd)
        n = jnp.concatenate(n_parts, axis=-1)
        gamma_t = jnp.concatenate([gamma] * N_HEADS, axis=-1)
        yg = n * gamma_t * silu_r
        yg_bf = yg.astype(BF16)
        yg_ref[...] = yg_bf
        yc = yc_ref[...]
        pc = _dot(yc, w_s[0])
        pg = _dot(yg_bf, w_s[1])
        sa = _sigmoid(ma_ref[...].astype(F32))
        sb = _sigmoid(mb_ref[...].astype(F32))
        merged = (sa * pc + sb * pg).astype(BF16)
        merged_ref[...] = merged
        out = _dot(merged, w_s[2])
        rstd2 = lax.rsqrt(jnp.mean(out * out, axis=-1, keepdims=True) + EPS)
        nn = out * rstd2
        gpost = gpost_ref[...]
        y = h_ref[...] + nn * gpost

        rowi = lax.broadcasted_iota(jnp.int32, (tm, 1), 0)
        keep = jnp.zeros((tm, 1), F32)
        for kk in range(per_tile):
            is_tok = ((i * per_tile + kk) % n_chunks) != 0
            f = jnp.where(is_tok, 1.0, 0.0)
            keep = jnp.where((rowi >= kk * CHUNK) & (rowi < (kk + 1) * CHUNK), f, keep)
        diff = (y - tgt_ref[...]) * keep
        loss_ref[...] += jnp.sum(diff * diff) * (0.5 / D)
        dy = diff * (1.0 / D)
        dres_ref[...] = dy
        dgpost_ref[...] += jnp.sum(dy * nn, axis=0, keepdims=True)
        dn = dy * gpost
        dout_f = rstd2 * (dn - nn * jnp.mean(dn * nn, axis=-1, keepdims=True))
        dout = dout_f.astype(BF16)
        dout_ref[...] = jnp.transpose(dout_f).astype(BF16)
        dmerged = _dot_nt(dout, w_s[2])
        dpc_f = dmerged * sa
        dpg_f = dmerged * sb
        dpc = dpc_f.astype(BF16)
        dpg = dpg_f.astype(BF16)
        dpc_ref[...] = jnp.transpose(dpc_f).astype(BF16)
        dpg_ref[...] = jnp.transpose(dpg_f).astype(BF16)
        dtail_ref[:, D:2 * D] = (dmerged * pc * (sa * (1.0 - sa))).astype(BF16)
        dtail_ref[:, 2 * D:3 * D] = (dmerged * pg * (sb * (1.0 - sb))).astype(BF16)
        dyc_ref[...] = _dot_nt(dpc, w_s[0]).astype(BF16)
        dyg = _dot_nt(dpg, w_s[1])
        dtail_ref[:, 0:D] = (dyg * n * gamma_t * (sr * (1.0 + r * (1.0 - sr)))).astype(BF16)
        dgam_full = jnp.sum(dyg * n * silu_r, axis=0, keepdims=True)
        dgam = dgam_full[:, 0:HEAD_V]
        for hd in range(1, N_HEADS):
            dgam = dgam + dgam_full[:, hd * HEAD_V:(hd + 1) * HEAD_V]
        dgamma_ref[...] += dgam
        dng = dyg * gamma_t * silu_r
        do_parts = []
        for hd in range(N_HEADS):
            sl = slice(hd * HEAD_V, (hd + 1) * HEAD_V)
            dnh = dng[:, sl]
            nh = n_parts[hd]
            do_parts.append(rstd_parts[hd] * (dnh - nh * jnp.mean(dnh * nh, axis=-1, keepdims=True)))
        do_ref[...] = jnp.concatenate(do_parts, axis=-1).astype(BF16)

    row = lambda c: pl.BlockSpec((tm, D), lambda i: (i, c))
    col = pl.BlockSpec((D, tm), lambda i: (0, i))
    const = lambda shape: pl.BlockSpec(shape, lambda i: (0, 0))
    act = jax.ShapeDtypeStruct((t_rows, D), BF16)
    act_t = jax.ShapeDtypeStruct((D, t_rows), BF16)
    return pl.pallas_call(
        body, name="tail", grid=(t_rows // tm,),
        in_specs=[row(0), row(0), row(0), row(0), row(6), row(7), row(8),
                  pl.BlockSpec(memory_space=pl.ANY), const((1, HEAD_V)), const((1, D))],
        out_specs=[row(0)] * 3 + [col] * 3 + [row(0)] * 2
                  + [pl.BlockSpec((tm, W_TAIL), lambda i: (i, 0)),
                     const((8, LANES)), const((1, D)), const((1, HEAD_V))],
        out_shape=[jax.ShapeDtypeStruct((t_rows, D), F32)] + [act] * 2 + [act_t] * 3 + [act] * 2
                  + [jax.ShapeDtypeStruct((t_rows, W_TAIL), BF16),
                     jax.ShapeDtypeStruct((8, LANES), F32),
                     jax.ShapeDtypeStruct((1, D), F32),
                     jax.ShapeDtypeStruct((1, HEAD_V), F32)],
        scratch_shapes=[pltpu.VMEM((3, D, D), BF16), pltpu.SemaphoreType.DMA],
        compiler_params=_params(("arbitrary",), 56),
    )(h, tgt, yc, o, proj, proj, proj, w3, gamma, g_post)


def _wgrad_t(a_t, b, name, out_dtype=BF16):
    m, t_rows = a_t.shape
    n = b.shape[1]
    tn = D if n % D == 0 else n
    tk = _pick_tile(t_rows, 768, LANES)
    n_k = t_rows // tk

    def body(a_ref, b_ref, o_ref, acc):
        k = pl.program_id(1)

        @pl.when(k == 0)
        def _():
            acc[...] = jnp.zeros_like(acc)

        acc[...] += _dot(a_ref[...], b_ref[...].astype(BF16))

        @pl.when(k == n_k - 1)
        def _():
            o_ref[...] = jnp.transpose(acc[...]).astype(out_dtype)

    return pl.pallas_call(
        body, name=name, grid=(n // tn, n_k),
        in_specs=[pl.BlockSpec((m, tk), lambda j, k: (0, k)),
                  pl.BlockSpec((tk, tn), lambda j, k: (k, j))],
        out_specs=pl.BlockSpec((tn, m), lambda j, k: (j, 0)),
        out_shape=jax.ShapeDtypeStruct((n, m), out_dtype),
        scratch_shapes=[pltpu.VMEM((m, tn), F32)],
        compiler_params=_params(("parallel", "arbitrary"), 48),
    )(a_t, b)


def _dgrad_in(dpc, dpg, dpt, dlr, w_full_t, h, g_pre, dres, token):
    t_rows = h.shape[0]
    tm = _pick_tile(t_rows, 256, 16)
    n_main = N_MAIN

    def body(dpc_ref, dpg_ref, dpt_ref, dlr_ref, w_hbm, h_ref, g_ref, dres_ref, token_ref,
             dh_ref, dg_ref, w_s, wlr_s, w_sems):
        @pl.when(pl.program_id(0) == 0)
        def _():
            _load_weights(w_hbm, w_s, wlr_s, w_sems)
            dg_ref[...] = jnp.zeros_like(dg_ref)

        du = _dot(dlr_ref[...].astype(BF16), wlr_s[...])
        du += _dot(dpc_ref[...], w_s[0:W_CONV, :])
        du += _dot(dpg_ref[...], w_s[W_CONV:W_CONV + W_GLA, :])
        du += _dot(dpt_ref[...], w_s[W_CONV + W_GLA:n_main, :])
        hh = h_ref[...]
        rstd = lax.rsqrt(jnp.mean(hh * hh, axis=-1, keepdims=True) + EPS)
        xhat = hh * rstd
        dg_ref[...] += jnp.sum(du * xhat, axis=0, keepdims=True)
        dx = du * g_ref[...]
        dh_ref[...] = rstd * (dx - xhat * jnp.mean(dx * xhat, axis=-1, keepdims=True)) + dres_ref[...]

    row = lambda width: pl.BlockSpec((tm, width), lambda i: (i, 0))
    return pl.pallas_call(
        body, name="dgrad_in", grid=(t_rows // tm,),
        in_specs=[row(W_CONV), row(W_GLA), row(W_TAIL), row(LANES),
                  pl.BlockSpec(memory_space=pl.ANY),
                  row(D), pl.BlockSpec((1, D), lambda i: (0, 0)), row(D),
                  pl.BlockSpec((8, LANES), lambda i: (0, 0))],
        out_specs=[row(D), pl.BlockSpec((1, D), lambda i: (0, 0))],
        out_shape=[jax.ShapeDtypeStruct((t_rows, D), F32), jax.ShapeDtypeStruct((1, D), F32)],
        scratch_shapes=[pltpu.VMEM((n_main, D), BF16), pltpu.VMEM((LANES, D), BF16),
                        pltpu.SemaphoreType.DMA((N_WEIGHT_COPIES,))],
        compiler_params=_params(("arbitrary",), 56),
    )(dpc, dpg, dpt, dlr, w_full_t, h, g_pre, dres, token)


def _reference_rows(g_conv, g_gla, g_tail, g_lr):
    conv = g_conv.reshape(N_CONV_TILES, 4, 128, D).transpose(1, 0, 2, 3).reshape(W_CONV, D)
    gla = g_gla.reshape(N_HEADS, 512, D)
    q = gla[:, 0:128].reshape(N_HEADS * HEAD_K, D)
    k = gla[:, 128:256].reshape(N_HEADS * HEAD_K, D)
    v = gla[:, 256:512].reshape(N_HEADS * HEAD_V, D)
    return jnp.concatenate([conv, q, k, v, g_tail[0:D], g_lr[0:2 * RANK], g_tail[D:3 * D]], axis=0)


def kernel(x, meta_tokens, norm_pre, w_in, conv_w, w_gate_fwd, b_gate_fwd, w_gate_bwd, b_gate_bwd, gla_norm, w_out_conv, w_out_gla, w_merge_out, norm_post, loss_target, m_meta_tokens, m_norm_pre, m_w_in, m_conv_w, m_w_gate_fwd, m_b_gate_fwd, m_w_gate_bwd, m_b_gate_bwd, m_gla_norm, m_w_out_conv, m_w_out_gla, m_w_merge_out, m_norm_post, v_meta_tokens, v_norm_pre, v_w_in, v_conv_w, v_w_gate_fwd, v_b_gate_fwd, v_w_gate_bwd, v_b_gate_bwd, v_gla_norm, v_w_out_conv, v_w_out_gla, v_w_merge_out, v_norm_post):
    n_seq, seq, _ = x.shape
    lf = CHUNK + seq
    t_rows = n_seq * lf
    shard = 2 * lax.axis_index("x") + lax.axis_index("y")
    shard_arr = jnp.reshape(shard, (1,)).astype(jnp.int32)

    w_in_slots = _cast_into_slot(jnp.transpose(w_in[0]), shard_arr, "cast_w_in")
    w_out_bf = _cast_bf16(jnp.concatenate([w_out_conv[0], w_out_gla[0], w_merge_out[0]], axis=0), "cast_w_out")
    w_in_all, meta_all, conv_all, wgf_all, wgb_all = _gather_via_sibling(
        "gather_w_in", [w_in_slots, meta_tokens, conv_w[0], w_gate_fwd[0], w_gate_bwd[0]],
        (True, False, False, False, False))
    w_out_state, _ = _plane_start("gather_w_out_start", [w_out_bf], "gather", wgb_all)

    w_full_t = w_in_all.reshape(N_IN, D)
    meta_full = jnp.transpose(meta_all, (1, 0, 2)).reshape(N_META, D)
    conv_full = jnp.transpose(conv_all, (1, 0, 2)).reshape(3, D)
    wgf = jnp.pad(wgf_all, ((0, 0), (0, LANES - RANK), (0, 0))).astype(BF16)
    wgb = jnp.pad(wgb_all, ((0, 0), (RANK, LANES - 2 * RANK), (0, 0))).astype(BF16)
    bgf = b_gate_fwd.reshape(N_HEADS, 1, HEAD_K)
    bgb = b_gate_bwd.reshape(N_HEADS, 1, HEAD_K)

    head = jnp.concatenate([jnp.zeros((PAD_FRONT, D), F32), meta_full], axis=0)
    h = jnp.concatenate([jnp.broadcast_to(head[None], (n_seq, CHUNK, D)), x], axis=1).reshape(t_rows, D)
    tgt = jnp.pad(loss_target, ((0, 0), (CHUNK, 0), (0, 0))).reshape(t_rows, D)

    proj, u_t, lr = _in_proj(h, norm_pre, w_full_t)
    yc = _conv_fwd(proj, conv_full, n_seq, lf)
    o = _gla_fwd(proj, lr, wgf, wgb, bgf, bgb, n_seq, lf)
    (w_out_landed,) = _plane_wait("gather_w_out_wait", w_out_state, "gather", o)
    slot_ids = lax.broadcasted_iota(jnp.int32, (4, 1, 1), 0)
    w_out_all = jnp.where(slot_ids == shard, w_out_bf[None], w_out_landed)
    w3 = jnp.transpose(w_out_all.reshape(4, 3, D // 4, D), (1, 0, 2, 3)).reshape(3, D, D)
    (dres, yg, merged, dout_t, dpc_t, dpg_t, dyc, d_o, dtail, loss_acc, d_gpost, d_gamma) = _tail(
        h, tgt, yc, o, proj, w3, gla_norm, norm_post, lf)
    g_w_oc = _wgrad_t(dpc_t, yc, "wgrad_out_conv")
    g_w_og = _wgrad_t(dpg_t, yg, "wgrad_out_gla")
    g_w_mo = _wgrad_t(dout_t, merged, "wgrad_merge_out")
    g_out_slots = jnp.concatenate([g.reshape(4, D // 4, D) for g in (g_w_oc, g_w_og, g_w_mo)], axis=1)
    out_state, out_token = _plane_start("scatter_out_grads_start", [g_out_slots], "scatter", g_w_mo)
    dgla, dlr, dwgf_p, dwgb_p, dbg_p = _gla_bwd(proj, lr, d_o, wgf, wgb, bgf, bgb, n_seq, lf, out_token)
    (got_out,) = _plane_wait("scatter_out_grads_wait", out_state, "scatter", dlr)
    dconv, dconvw_p = _conv_bwd(proj, conv_full, dyc, n_seq, lf)
    g_conv = _wgrad_t(u_t, dconv, "wgrad_in_conv")
    g_gla = _wgrad_t(u_t, dgla, "wgrad_in_gla")
    g_tail = _wgrad_t(u_t, dtail, "wgrad_in_tail")
    g_lr = _wgrad_t(u_t, dlr, "wgrad_in_lr")

    g_in_slots = _reference_rows(g_conv, g_gla, g_tail, g_lr).reshape(4, SHARD_IN, D)
    in_state, in_token = _plane_start("scatter_in_grads_start", [g_in_slots], "scatter", g_lr)
    dh, d_gpre = _dgrad_in(dconv, dgla, dtail, dlr, w_full_t, h, norm_pre, dres, in_token)
    (got_in,) = _plane_wait("scatter_in_grads_wait", in_state, "scatter", d_gpre)

    plane_in = _sum_slots(got_in, "sum_w_in_grads", own=g_in_slots, slot=shard_arr)
    plane_out = _sum_slots(got_out, "sum_w_out_grads", own=g_out_slots, slot=shard_arr)
    swap_state, swap_token = _plane_start("swap_plane_sums_start", [plane_in, plane_out], "swap", plane_out)

    dh3 = dh.reshape(n_seq, lf, D)
    grad_x = dh3[:, CHUNK:, :]

    d_meta = jnp.sum(dh3[:, PAD_FRONT:CHUNK, :], axis=0)
    d_convw = jnp.sum(dconvw_p, axis=0)
    d_wgf = jnp.transpose(jnp.sum(dwgf_p, axis=0)[:, 0:RANK, :], (1, 0, 2)).reshape(RANK, N_HEADS * HEAD_K)
    d_wgb = jnp.transpose(jnp.sum(dwgb_p, axis=0)[:, RANK:2 * RANK, :], (1, 0, 2)).reshape(RANK, N_HEADS * HEAD_K)
    d_bg = jnp.sum(dbg_p, axis=0)
    d_bgf = d_bg[:, 0, :].reshape(1, N_HEADS * HEAD_K)
    d_bgb = d_bg[:, 1, :].reshape(1, N_HEADS * HEAD_K)
    loss_part = loss_acc[0:1, :] + swap_token[0:1, :]
    partials = [d_meta, d_convw, d_wgf, d_wgb, d_gpre, d_bgf, d_bgb, d_gamma, d_gpost, loss_part]
    (g_meta, g_convw, g_wgf, g_wgb, g_npre, g_bgf, g_bgb, g_gnorm, g_npost, loss_row) = _sum_small(
        _exchange("gather_small_grads", partials, ALL_FLIPS, (4, 2, 1), "gather"), "sum_small_grads")
    loss = loss_row[0, 0]
    small_out = _adamw_small(
        [(meta_tokens, g_meta, m_meta_tokens, v_meta_tokens), (norm_pre, g_npre, m_norm_pre, v_norm_pre),
         (conv_w, g_convw, m_conv_w, v_conv_w), (w_gate_fwd, g_wgf, m_w_gate_fwd, v_w_gate_fwd),
         (b_gate_fwd, g_bgf, m_b_gate_fwd, v_b_gate_fwd), (w_gate_bwd, g_wgb, m_w_gate_bwd, v_w_gate_bwd),
         (b_gate_bwd, g_bgb, m_b_gate_bwd, v_b_gate_bwd), (gla_norm, g_gnorm, m_gla_norm, v_gla_norm),
         (norm_post, g_npost, m_norm_post, v_norm_post)], shard_arr, "adamw_small")

    other_in, other_out = _plane_wait("swap_plane_sums_wait", swap_state, "swap", small_out[0][0])
    big_in = _adamw(jnp.transpose(w_in[0]), [plane_in, other_in], jnp.transpose(m_w_in[0]), jnp.transpose(v_w_in[0]),
                    "adamw_w_in")
    out_params = ((w_out_conv, m_w_out_conv, v_w_out_conv), (w_out_gla, m_w_out_gla, v_w_out_gla),
                  (w_merge_out, m_w_merge_out, v_w_merge_out))
    big_out = [_adamw(w[0], [plane_out, other_out], m[0], v[0], f"adamw_w_out_{i}", grad_row=i * (D // 4))
               for i, (w, m, v) in enumerate(out_params)]

    results = []
    for kind in range(4):
        small_kind = [p[kind] for p in small_out]
        w_in_part = jnp.transpose(big_in[kind])[None]
        outs3 = [big_out[i][kind][None] for i in range(3)]
        results.extend(small_kind[0:2] + [w_in_part] + small_kind[2:8] + outs3 + small_kind[8:9])
    return (loss, grad_x, *results)
```

```python
import functools

import jax
import jax.numpy as jnp
from jax import lax
from jax.experimental import pallas as pl
from jax.experimental.pallas import tpu as pltpu

F32 = jnp.float32
BF16 = jnp.bfloat16
MESH = pl.DeviceIdType.MESH

D = 1024
N_META = 16
CHUNK = 64
PAD_FRONT = CHUNK - N_META
N_HEADS = 4
HEAD_K = 128
HEAD_V = 256
RANK = 16
EPS = 1e-6
GATE_NORM = 16.0
N_IN = 9248
SHARD_IN = N_IN // 4
LANES = 128
N_CONV_TILES = 8
W_CONV = 4096
W_GLA = 2048
W_TAIL = 3072
N_MAIN = W_CONV + W_GLA + W_TAIL
OFF_Q, OFF_K, OFF_V, OFF_R = 4096, 4608, 5120, 6144
OFF_LR, OFF_MA, OFF_MB = 7168, 7200, 8224
MIB = 1024 * 1024

ADAM_LR = 0.001
ADAM_B1 = 0.9
ADAM_B2 = 0.999
ADAM_EPS = 1e-08
ADAM_WD = 0.01
ADAM_STEP = 10


def _params(sem=None, vmem_mib=None):
    return pltpu.CompilerParams(
        dimension_semantics=sem,
        vmem_limit_bytes=None if vmem_mib is None else vmem_mib * MIB)


def _pick_tile(n, target, mult):
    best = None
    for t in range(mult, min(n, target) + 1, mult):
        if n % t == 0:
            best = t
    return n if best is None else best


def _sigmoid(v):
    return 1.0 / (1.0 + jnp.exp(-v))


def _log_sigmoid(v):
    return jnp.minimum(v, 0.0) - jnp.log(1.0 + jnp.exp(-jnp.abs(v)))


def _dot(a, b):
    return jnp.dot(a, b, preferred_element_type=F32)


def _dot_nt(a, b):
    return lax.dot_general(a, b, (((1,), (1,)), ((), ())), preferred_element_type=F32)


def _dot_tn(a, b):
    return lax.dot_general(a, b, (((0,), (0,)), ((), ())), preferred_element_type=F32)


PLANE_FLIPS = ((1, 0, 0), (0, 1, 0), (1, 1, 0))
ALL_FLIPS = tuple((m >> 2 & 1, m >> 1 & 1, m & 1) for m in range(1, 8))
SIBLING_FLIPS = ((0, 0, 1),)


def _exchange(name, arrs, flips, slot_weights, mode):
    n = len(arrs)
    n_slots = 1
    for w in slot_weights:
        n_slots += w
    if mode == "gather":
        out_shape = [jax.ShapeDtypeStruct((n_slots,) + a.shape, a.dtype) for a in arrs]
    else:
        out_shape = [jax.ShapeDtypeStruct(a.shape, a.dtype) for a in arrs]

    def body(*refs):
        ins, outs = refs[:n], refs[n:2 * n]
        send_sems, recv_sems, local_sems = refs[2 * n:]
        pos = (lax.axis_index("x"), lax.axis_index("y"), lax.axis_index("c"))

        def slot_of(p):
            return p[0] * slot_weights[0] + p[1] * slot_weights[1] + p[2] * slot_weights[2]

        peers = [tuple(1 - pos[a] if f[a] else pos[a] for a in range(3)) for f in flips]
        me = slot_of(pos)
        local = []
        sends = []
        for i in range(n):
            if mode != "swap":
                src = ins[i] if mode == "gather" else ins[i].at[me]
                cp = pltpu.make_async_copy(src, outs[i].at[me], local_sems.at[i])
                cp.start()
                local.append(cp)
            for k, peer in enumerate(peers):
                if mode == "gather":
                    src, dst = ins[i], outs[i].at[me]
                elif mode == "scatter":
                    src, dst = ins[i].at[slot_of(peer)], outs[i].at[me]
                else:
                    src, dst = ins[i], outs[i]
                cp = pltpu.make_async_remote_copy(
                    src_ref=src, dst_ref=dst, send_sem=send_sems.at[i, k], recv_sem=recv_sems.at[i, k],
                    device_id=peer, device_id_type=MESH)
                cp.start()
                sends.append(cp)
        for i in range(n):
            for k, peer in enumerate(peers):
                if mode == "gather":
                    src, dst = ins[i], outs[i].at[slot_of(peer)]
                elif mode == "scatter":
                    src, dst = ins[i].at[me], outs[i].at[slot_of(peer)]
                else:
                    src, dst = ins[i], outs[i]
                arrival = pltpu.make_async_remote_copy(
                    src_ref=src, dst_ref=dst, send_sem=send_sems.at[i, k], recv_sem=recv_sems.at[i, k],
                    device_id=peer, device_id_type=MESH)
                arrival.wait_recv()
        for cp in sends:
            cp.wait_send()
        for cp in local:
            cp.wait()

    hbm = pl.BlockSpec(memory_space=pl.ANY)
    outs = pl.pallas_call(
        body, name=name, out_shape=out_shape,
        in_specs=[hbm] * n, out_specs=[hbm] * n,
        scratch_shapes=[pltpu.SemaphoreType.DMA((n, len(flips))),
                        pltpu.SemaphoreType.DMA((n, len(flips))),
                        pltpu.SemaphoreType.DMA((n,))],
        compiler_params=pltpu.CompilerParams(has_side_effects=True),
    )(*arrs)
    return list(outs)


def _gather_via_sibling(name, arrs, slotted):
    n = len(arrs)
    out_shape = [jax.ShapeDtypeStruct(a.shape if slotted[i] else (4,) + a.shape, a.dtype)
                 for i, a in enumerate(arrs)]

    def body(*refs):
        ins, outs = refs[:n], refs[n:2 * n]
        send_sems, recv_sems, local_sems = refs[2 * n:]
        x, y, c = lax.axis_index("x"), lax.axis_index("y"), lax.axis_index("c")
        me = 2 * x + y
        chips = [(1 - x, y), (x, 1 - y), (1 - x, 1 - y)]

        def half(ref, which):
            rows = ref.shape[0]
            cut = rows // 2 // 16 * 16
            return ref.at[pl.ds(0, cut)] if which == 0 else ref.at[pl.ds(cut, rows - cut)]

        def copy(src, dst, i, k, to):
            return pltpu.make_async_remote_copy(
                src_ref=src, dst_ref=dst, send_sem=send_sems.at[i, k], recv_sem=recv_sems.at[i, k],
                device_id=to, device_id_type=MESH)

        def run(mine):
            other = 1 - mine
            local, sends = [], []
            whole = [(not slotted[i]) and arrs[i].shape[0] < 32 for i in range(n)]
            for i in range(n):
                own = outs[i].at[me] if slotted[i] else ins[i]
                if not slotted[i]:
                    cp = pltpu.make_async_copy(ins[i], outs[i].at[me], local_sems.at[i])
                    cp.start()
                    local.append(cp)
                for k, (px, py) in enumerate(chips):
                    if whole[i]:
                        cp = copy(own, outs[i].at[me], i, k, (px, py, mine))
                    else:
                        cp = copy(half(own, mine), half(outs[i].at[me], mine), i, k, (px, py, mine))
                    cp.start()
                    sends.append(cp)
            for k, (px, py) in enumerate(chips):
                slot = 2 * px + py
                for i in range(n):
                    if whole[i]:
                        copy(outs[i].at[slot], outs[i].at[slot], i, k, (px, py, mine)).wait_recv()
                        continue
                    landed = half(outs[i].at[slot], mine)
                    copy(landed, landed, i, k, (px, py, mine)).wait_recv()
                    cp = copy(landed, landed, i, 3 + k, (x, y, other))
                    cp.start()
                    sends.append(cp)
            for k, (px, py) in enumerate(chips):
                slot = 2 * px + py
                for i in range(n):
                    if whole[i]:
                        continue
                    passed = half(outs[i].at[slot], other)
                    copy(passed, passed, i, 3 + k, (x, y, other)).wait_recv()
            for cp in sends:
                cp.wait_send()
            for cp in local:
                cp.wait()

        for mine in (0, 1):
            pl.when(c == mine)(functools.partial(run, mine))

    hbm = pl.BlockSpec(memory_space=pl.ANY)
    outs = pl.pallas_call(
        body, name=name, out_shape=out_shape,
        in_specs=[hbm] * n, out_specs=[hbm] * n,
        scratch_shapes=[pltpu.SemaphoreType.DMA((n, 6)), pltpu.SemaphoreType.DMA((n, 6)),
                        pltpu.SemaphoreType.DMA((n,))],
        input_output_aliases={i: i for i in range(n) if slotted[i]},
        compiler_params=pltpu.CompilerParams(has_side_effects=True),
    )(*arrs)
    return list(outs)


HBM_SPEC = pl.BlockSpec(memory_space=pltpu.HBM)
SEM_SPEC = pl.BlockSpec(memory_space=pltpu.SEMAPHORE)
DATAFLOW = pltpu.SideEffectType.DATAFLOW_SIDE_EFFECTING


def _split_peers(mode):
    x, y, c = lax.axis_index("x"), lax.axis_index("y"), lax.axis_index("c")
    if mode == "swap":
        return 0, [((x, y, 1 - c), 0)]
    return 2 * x + y, [((1 - x, y, c), 2 * (1 - x) + y), ((x, 1 - y, c), 2 * x + 1 - y),
                       ((1 - x, 1 - y, c), 2 * (1 - x) + 1 - y)]


def _split_refs(mode, src, landing, me, peer_slot):
    if mode == "gather":
        return src, landing.at[me]
    if mode == "scatter":
        return src.at[peer_slot], landing.at[me]
    return src, landing


def _plane_start(name, arrs, mode, after):
    n = len(arrs)
    n_peers = 1 if mode == "swap" else 3
    lands = [lax.empty(((4,) + a.shape) if mode == "gather" else a.shape, a.dtype) for a in arrs]

    def body(*refs):
        srcs, landing = refs[:n], refs[n:2 * n]
        send_sems, recv_sems = refs[2 * n + 1], refs[2 * n + 2]
        token = refs[-1]
        me, peers = _split_peers(mode)
        for i in range(n):
            for k, (peer, peer_slot) in enumerate(peers):
                src, dst = _split_refs(mode, srcs[i], landing[i], me, peer_slot)
                pltpu.make_async_remote_copy(
                    src_ref=src, dst_ref=dst, send_sem=send_sems.at[n_peers * i + k],
                    recv_sem=recv_sems.at[n_peers * i + k], device_id=peer, device_id_type=MESH).start()
        token[...] = jnp.zeros_like(token)

    hbm_in = [pltpu.with_memory_space_constraint(a, pltpu.HBM) for a in list(arrs) + lands]
    out = pl.pallas_call(
        body, name=name,
        out_shape=[pltpu.SemaphoreType.DMA((n_peers * n,)), pltpu.SemaphoreType.DMA((n_peers * n,))]
                  + [pltpu.HBM(a.shape, a.dtype) for a in hbm_in]
                  + [jax.ShapeDtypeStruct((8, LANES), F32)],
        in_specs=[HBM_SPEC] * (2 * n) + [pl.BlockSpec(memory_space=pl.ANY)],
        out_specs=[SEM_SPEC, SEM_SPEC] + [HBM_SPEC] * (2 * n) + [pl.BlockSpec(memory_space=pltpu.VMEM)],
        input_output_aliases={i: 2 + i for i in range(2 * n)},
        compiler_params=pltpu.CompilerParams(has_side_effects=DATAFLOW),
    )(*hbm_in, after)
    return out[:-1], out[-1]


def _plane_wait(name, state, mode, after):
    send_sems, recv_sems = state[0], state[1]
    bufs = list(state[2:])
    n = len(bufs) // 2
    n_peers = 1 if mode == "swap" else 3

    def body(*refs):
        srcs, landing = refs[:n], refs[n:2 * n]
        send_sems, recv_sems = refs[2 * n], refs[2 * n + 1]
        me, peers = _split_peers(mode)
        for i in range(n):
            for k, (peer, peer_slot) in enumerate(peers):
                src, _ = _split_refs(mode, srcs[i], landing[i], me, peer_slot)
                arrived = landing[i] if mode == "swap" else landing[i].at[peer_slot]
                cp = pltpu.make_async_remote_copy(
                    src_ref=src, dst_ref=arrived, send_sem=send_sems.at[n_peers * i + k],
                    recv_sem=recv_sems.at[n_peers * i + k], device_id=peer, device_id_type=MESH)
                cp.wait_send()
                cp.wait_recv()

    out = pl.pallas_call(
        body, name=name,
        out_shape=[pltpu.HBM(a.shape, a.dtype) for a in bufs],
        in_specs=[HBM_SPEC] * (2 * n) + [SEM_SPEC, SEM_SPEC, pl.BlockSpec(memory_space=pl.ANY)],
        out_specs=[HBM_SPEC] * (2 * n),
        input_output_aliases={i: i for i in range(2 * n)},
        compiler_params=pltpu.CompilerParams(has_side_effects=DATAFLOW),
    )(*bufs, send_sems, recv_sems, after)
    return list(out[n:])


def _tile_2d(rows, cols, row_mult, max_elems=512 * 1024):
    if rows % row_mult == 0:
        rt = _pick_tile(rows, max(row_mult, max_elems // cols), row_mult)
        return (rt, cols), rows // rt, lambda i: (i, 0)
    ct = _pick_tile(cols, max(LANES, max_elems // rows), LANES)
    return (rows, ct), cols // ct, lambda i: (0, i)


def _cast_bf16(a, name):
    block, steps, index = _tile_2d(a.shape[0], a.shape[1], 16)

    def body(a_ref, o_ref):
        o_ref[...] = a_ref[...].astype(BF16)

    return pl.pallas_call(
        body, name=name, grid=(steps,),
        in_specs=[pl.BlockSpec(block, index)],
        out_specs=pl.BlockSpec(block, index),
        out_shape=jax.ShapeDtypeStruct(a.shape, BF16),
        compiler_params=_params(("parallel",)),
    )(a)


def _cast_into_slot(a, slot, name):
    block, steps, index = _tile_2d(a.shape[0], a.shape[1], 16)

    def body(slot_ref, a_ref, o_ref):
        o_ref[...] = a_ref[...].astype(BF16)

    return pl.pallas_call(
        body, name=name,
        grid_spec=pltpu.PrefetchScalarGridSpec(
            num_scalar_prefetch=1, grid=(steps,),
            in_specs=[pl.BlockSpec(block, lambda i, s: index(i))],
            out_specs=pl.BlockSpec((None,) + block, lambda i, s: (s[0],) + index(i))),
        out_shape=jax.ShapeDtypeStruct((4,) + a.shape, BF16),
        compiler_params=_params(("arbitrary",)),
    )(slot, a)


def _sum_slots(buf, name, own=None, slot=None):
    n_slots, rows, cols = buf.shape
    (br, bc), steps, index = _tile_2d(rows, cols, 16, 320 * 1024)

    def body(*refs):
        if own is None:
            b_ref, o_ref = refs
        else:
            slot_ref, b_ref, own_ref, o_ref = refs
        acc = None
        for s in range(n_slots):
            term = b_ref[s] if own is None else jnp.where(slot_ref[0] == s, own_ref[...], b_ref[s])
            acc = term.astype(F32) if acc is None else acc + term.astype(F32)
        o_ref[...] = acc

    out_shape = jax.ShapeDtypeStruct((rows, cols), F32)
    if own is None:
        return pl.pallas_call(
            body, name=name, grid=(steps,),
            in_specs=[pl.BlockSpec((n_slots, br, bc), lambda i: (0,) + index(i))],
            out_specs=pl.BlockSpec((br, bc), index), out_shape=out_shape,
            compiler_params=_params(("parallel",), 48),
        )(buf)
    return pl.pallas_call(
        body, name=name,
        grid_spec=pltpu.PrefetchScalarGridSpec(
            num_scalar_prefetch=1, grid=(steps,),
            in_specs=[pl.BlockSpec((n_slots, br, bc), lambda i, s: (0,) + index(i)),
                      pl.BlockSpec((None, br, bc), lambda i, s: (s[0],) + index(i))],
            out_specs=pl.BlockSpec((br, bc), lambda i, s: index(i))),
        out_shape=out_shape,
        compiler_params=_params(("arbitrary",), 48),
    )(slot, buf, own)


def _sum_small(bufs, name):
    n = len(bufs)

    def body(*refs):
        for b_ref, o_ref in zip(refs[:n], refs[n:]):
            acc = b_ref[0]
            for s in range(1, b_ref.shape[0]):
                acc = acc + b_ref[s]
            o_ref[...] = acc

    vmem = pl.BlockSpec(memory_space=pltpu.VMEM)
    return pl.pallas_call(
        body, name=name, in_specs=[vmem] * n, out_specs=[vmem] * n,
        out_shape=[jax.ShapeDtypeStruct(b.shape[1:], b.dtype) for b in bufs],
    )(*bufs)


def _adam_update(w, g, m, v):
    c1 = 1.0 - ADAM_B1 ** ADAM_STEP
    c2 = 1.0 - ADAM_B2 ** ADAM_STEP
    m_new = ADAM_B1 * m + (1.0 - ADAM_B1) * g
    v_new = ADAM_B2 * v + (1.0 - ADAM_B2) * (g * g)
    m_hat = m_new / c1
    v_hat = v_new / c2
    return -ADAM_LR * (m_hat / (jnp.sqrt(v_hat) + ADAM_EPS) + ADAM_WD * w), m_new, v_new


def _adamw_small(params, slot, name):
    n = len(params)

    def spec_of(shape):
        lead = (None,) * (len(shape) - 2)
        return pl.BlockSpec(lead + tuple(shape[-2:]), lambda i, s, k=len(shape): (0,) * k)

    in_specs, operands, out_specs, out_shape = [], [], [], []
    for w, g, m, v in params:
        shard = g.shape[-1] != w.shape[-1]
        g_spec = pl.BlockSpec(tuple(w.shape[-2:]), (lambda i, s: (0, s[0])) if shard else (lambda i, s: (0, 0)))
        in_specs += [spec_of(w.shape), g_spec, spec_of(m.shape), spec_of(v.shape)]
        operands += [w, g, m, v]
        out_specs += [spec_of(w.shape)] * 4
        out_shape += [jax.ShapeDtypeStruct(w.shape, F32)] * 4

    def body(slot_ref, *refs):
        ins, outs = refs[:4 * n], refs[4 * n:]
        for p in range(n):
            w_ref, g_ref, m_ref, v_ref = ins[4 * p:4 * p + 4]
            g = g_ref[...]
            delta, m_new, v_new = _adam_update(w_ref[...], g, m_ref[...], v_ref[...])
            for o_ref, val in zip(outs[4 * p:4 * p + 4], (g, delta, m_new, v_new)):
                o_ref[...] = val

    out = pl.pallas_call(
        body, name=name,
        grid_spec=pltpu.PrefetchScalarGridSpec(num_scalar_prefetch=1, grid=(1,), in_specs=in_specs, out_specs=out_specs),
        out_shape=out_shape,
    )(slot, *operands)
    return [tuple(out[4 * p:4 * p + 4]) for p in range(n)]


def _adamw(w, grads, m, v, name, grad_row=0):
    rows, cols = w.shape
    (rt, _), _, _ = _tile_2d(rows, cols, 8, 160 * 1024)
    assert grad_row % rt == 0
    n_g = len(grads)

    def body(*refs):
        w_ref = refs[0]
        g_refs = refs[1:1 + n_g]
        m_ref, v_ref, g_out, d_out, m_out, v_out = refs[1 + n_g:]
        g = g_refs[0][...]
        for r in g_refs[1:]:
            g = g + r[...]
        g_out[...] = g
        d_out[...], m_out[...], v_out[...] = _adam_update(w_ref[...], g, m_ref[...], v_ref[...])

    spec = pl.BlockSpec((rt, cols), lambda i: (i, 0))
    grad_spec = pl.BlockSpec((rt, cols), lambda i: (i + grad_row // rt, 0))
    shape = jax.ShapeDtypeStruct((rows, cols), F32)
    return pl.pallas_call(
        body, name=name, grid=(rows // rt,),
        in_specs=[spec] + [grad_spec] * n_g + [spec] * 2, out_specs=[spec] * 4, out_shape=[shape] * 4,
        compiler_params=_params(("parallel",), 48),
    )(w, *grads, m, v)


def _weight_pieces():
    pieces = []
    for j in range(N_CONV_TILES):
        for g in range(4):
            pieces.append((512 * j + 128 * g, D * g + 128 * j, 128))
    for hd in range(N_HEADS):
        base = W_CONV + 512 * hd
        pieces.append((base, OFF_Q + HEAD_K * hd, HEAD_K))
        pieces.append((base + HEAD_K, OFF_K + HEAD_K * hd, HEAD_K))
        pieces.append((base + 2 * HEAD_K, OFF_V + HEAD_V * hd, HEAD_V))
    pieces.append((W_CONV + W_GLA, OFF_R, D))
    pieces.append((W_CONV + W_GLA + D, OFF_MA, 2 * D))
    return pieces


N_WEIGHT_COPIES = len(_weight_pieces()) + 1


def _load_weights(w_hbm, w_s, wlr_s, sems):
    copies = [pltpu.make_async_copy(w_hbm.at[pl.ds(src, n)], w_s.at[pl.ds(dst, n)], sems.at[i])
              for i, (dst, src, n) in enumerate(_weight_pieces())]
    copies.append(pltpu.make_async_copy(w_hbm.at[pl.ds(OFF_LR, LANES)], wlr_s, sems.at[N_WEIGHT_COPIES - 1]))
    for cp in copies:
        cp.start()
    for cp in copies:
        cp.wait()


def _in_proj(h, g_pre, w_full_t):
    t_rows = h.shape[0]
    tm = _pick_tile(t_rows, 384, LANES)
    n_main = N_MAIN

    def body(h_ref, g_ref, w_hbm, proj_ref, ut_ref, lr_ref, w_s, wlr_s, w_sems):
        @pl.when(pl.program_id(0) == 0)
        def _():
            _load_weights(w_hbm, w_s, wlr_s, w_sems)

        hh = h_ref[...]
        rstd = lax.rsqrt(jnp.mean(hh * hh, axis=-1, keepdims=True) + EPS)
        uf = hh * rstd * g_ref[...]
        u = uf.astype(BF16)
        ut_ref[...] = jnp.transpose(uf).astype(BF16)
        lr_ref[...] = _dot_nt(u, wlr_s[...])
        for j in range(n_main // D):
            cols = slice(j * D, (j + 1) * D)
            proj_ref[:, cols] = _dot_nt(u, w_s[cols, :]).astype(BF16)

    return pl.pallas_call(
        body, name="in_proj", grid=(t_rows // tm,),
        in_specs=[pl.BlockSpec((tm, D), lambda i: (i, 0)),
                  pl.BlockSpec((1, D), lambda i: (0, 0)),
                  pl.BlockSpec(memory_space=pl.ANY)],
        out_specs=[pl.BlockSpec((tm, n_main), lambda i: (i, 0)),
                   pl.BlockSpec((D, tm), lambda i: (0, i)),
                   pl.BlockSpec((tm, LANES), lambda i: (i, 0))],
        out_shape=[jax.ShapeDtypeStruct((t_rows, n_main), BF16),
                   jax.ShapeDtypeStruct((D, t_rows), BF16),
                   jax.ShapeDtypeStruct((t_rows, LANES), F32)],
        scratch_shapes=[pltpu.VMEM((n_main, D), BF16), pltpu.VMEM((LANES, D), BF16),
                        pltpu.SemaphoreType.DMA((N_WEIGHT_COPIES,))],
        compiler_params=_params(("arbitrary",), 56),
    )(h, g_pre, w_full_t)


def _conv_parts(p_ref, w_ref):
    cb = p_ref[:, 0:128].astype(F32)
    cc = p_ref[:, 128:256].astype(F32)
    cx = p_ref[:, 256:384].astype(F32)
    cz = p_ref[:, 384:512].astype(F32)
    rows = cb.shape[0]
    w = w_ref[...]
    p = cc * cx
    conv = pltpu.roll(p, 1, 0) * w[0:1] + p * w[1:2] + pltpu.roll(p, rows - 1, 0) * w[2:3]
    sz = _sigmoid(cz)
    return cb, cc, cx, cz, p, conv, sz, w


def _conv_fwd(proj, conv_w, n_seq, lf):
    def body(p_ref, w_ref, y_ref):
        cb, _, _, cz, _, conv, sz, _ = _conv_parts(p_ref, w_ref)
        y_ref[...] = (cb * conv * (cz * sz)).astype(BF16)

    return pl.pallas_call(
        body, name="conv_fwd", grid=(n_seq, N_CONV_TILES),
        in_specs=[pl.BlockSpec((lf, 512), lambda b, j: (b, j)),
                  pl.BlockSpec((3, 128), lambda b, j: (0, j))],
        out_specs=pl.BlockSpec((lf, 128), lambda b, j: (b, j)),
        out_shape=jax.ShapeDtypeStruct((n_seq * lf, D), BF16),
        compiler_params=_params(("parallel", "parallel"), 48),
    )(proj, conv_w)


def _conv_bwd(proj, conv_w, dyc, n_seq, lf):
    def body(p_ref, w_ref, dy_ref, dp_ref, dw_ref):
        cb, cc, cx, cz, p, conv, sz, w = _conv_parts(p_ref, w_ref)
        rows = cb.shape[0]
        dy = dy_ref[...].astype(F32)
        silu = cz * sz
        dcb = dy * conv * silu
        dconv = dy * cb * silu
        dcz = dy * cb * conv * (sz * (1.0 + cz * (1.0 - sz)))
        d_next = pltpu.roll(dconv, rows - 1, 0)
        d_prev = pltpu.roll(dconv, 1, 0)
        dp = d_next * w[0:1] + dconv * w[1:2] + d_prev * w[2:3]
        dp_ref[:, 0:128] = dcb.astype(BF16)
        dp_ref[:, 128:256] = (dp * cx).astype(BF16)
        dp_ref[:, 256:384] = (dp * cc).astype(BF16)
        dp_ref[:, 384:512] = dcz.astype(BF16)
        dw_ref[0:1, :] = jnp.sum(dconv * pltpu.roll(p, 1, 0), axis=0, keepdims=True)
        dw_ref[1:2, :] = jnp.sum(dconv * p, axis=0, keepdims=True)
        dw_ref[2:3, :] = jnp.sum(dconv * pltpu.roll(p, rows - 1, 0), axis=0, keepdims=True)

    return pl.pallas_call(
        body, name="conv_bwd", grid=(n_seq, N_CONV_TILES),
        in_specs=[pl.BlockSpec((lf, 512), lambda b, j: (b, j)),
                  pl.BlockSpec((3, 128), lambda b, j: (0, j)),
                  pl.BlockSpec((lf, 128), lambda b, j: (b, j))],
        out_specs=[pl.BlockSpec((lf, 512), lambda b, j: (b, j)),
                   pl.BlockSpec((None, 3, 128), lambda b, j: (b, 0, j))],
        out_shape=[jax.ShapeDtypeStruct((n_seq * lf, W_CONV), BF16),
                   jax.ShapeDtypeStruct((n_seq, 3, D), F32)],
        compiler_params=_params(("parallel", "parallel"), 48),
    )(proj, conv_w, dyc)


GROUP = 3
GROUP_ROWS = GROUP * CHUNK


def _row_group(shape):
    row = lax.broadcasted_iota(jnp.int32, shape, 0)
    grp = jnp.zeros(shape, jnp.int32)
    for r in range(1, GROUP):
        grp = grp + (row >= r * CHUNK).astype(jnp.int32)
    return grp


def _lane_group(shape, width):
    lane = lax.broadcasted_iota(jnp.int32, shape, 1)
    grp = jnp.zeros(shape, jnp.int32)
    for r in range(1, GROUP):
        grp = grp + (lane >= r * width).astype(jnp.int32)
    return grp


def _score_mask(direction):
    shape = (GROUP_ROWS, GROUP_ROWS)
    row = lax.broadcasted_iota(jnp.int32, shape, 0)
    col = lax.broadcasted_iota(jnp.int32, shape, 1)
    same = _row_group(shape) == _lane_group(shape, CHUNK)
    return same & ((col <= row) if direction == 0 else (col > row))


def _diag_blocks(v):
    w = v.shape[1]
    wide = jnp.concatenate([v] * GROUP, axis=1)
    return jnp.where(_row_group(wide.shape) == _lane_group(wide.shape, w), wide, jnp.zeros_like(wide))


def _per_chunk_dot(lhs, state, transposed):
    outs = []
    for r in range(GROUP):
        rows = lhs[r * CHUNK:(r + 1) * CHUNK, :]
        blk = state[:, r * HEAD_K:(r + 1) * HEAD_K]
        outs.append(_dot_nt(rows, blk) if transposed else _dot(rows, blk))
    return jnp.concatenate(outs, axis=0)


def _chunk_cumsum(v, suffix):
    pos = lax.broadcasted_iota(jnp.int32, v.shape, 0) & (CHUNK - 1)
    shift = 1
    while shift < CHUNK:
        if suffix:
            moved = pltpu.roll(v, GROUP_ROWS - shift, 0)
            v = v + jnp.where(pos < CHUNK - shift, moved, 0.0)
        else:
            moved = pltpu.roll(v, shift, 0)
            v = v + jnp.where(pos >= shift, moved, 0.0)
        shift *= 2
    return v


def _per_chunk_rows(rows_of_chunk):
    w = rows_of_chunk[0].shape[1]
    return jnp.concatenate([jnp.broadcast_to(v, (CHUNK, w)) for v in rows_of_chunk], axis=0)


def _chunk_end_rows(direction, b):
    at = CHUNK - 1 if direction == 0 else 0
    return [b[r * CHUNK + at:r * CHUNK + at + 1, :] for r in range(GROUP)]


def _gla_gates(lr_bf, wg_ref, bg_ref, lf):
    z = _dot(lr_bf, wg_ref[...]) + bg_ref[...]
    valid = lax.broadcasted_iota(jnp.int32, (lf, HEAD_K), 0) >= PAD_FRONT
    return z, valid


def _group_unroll(n_groups):
    return n_groups if n_groups <= 11 else 1


def _group_rows(g):
    return pl.ds(pl.multiple_of(g * GROUP_ROWS, GROUP_ROWS), GROUP_ROWS)


def _chunk_decay(direction, g, r, b_s):
    base = g * GROUP_ROWS + r * CHUNK
    if direction == 0:
        grp = b_s[pl.ds(pl.multiple_of(base + CHUNK - 8, 8), 8), :]
        return jnp.exp(grp[7:8, :])
    grp = b_s[pl.ds(pl.multiple_of(base, 8), 8), :]
    return jnp.exp(grp[0:1, :])


def _state_scan(direction, n_groups, b_s, st_s, reverse):
    ascending = (direction == 0) != reverse

    def step(i, carry):
        g = i if ascending else n_groups - 1 - i
        for rr in range(GROUP):
            r = rr if ascending else GROUP - 1 - rr
            lanes = slice(r * HEAD_K, (r + 1) * HEAD_K)
            decay = _chunk_decay(direction, g, r, b_s)
            local = st_s[g, :, lanes]
            st_s[g, :, lanes] = carry
            carry = (local + carry * decay) if reverse else (carry * decay + local)
        return carry

    lax.fori_loop(0, n_groups, step, jnp.zeros((HEAD_V, HEAD_K), F32), unroll=_group_unroll(n_groups))


def _gla_states(direction, n_groups, qkv_ref, g_s, b_s, st_s):
    def local(g, carry):
        rows = _group_rows(g)
        b = _chunk_cumsum(g_s[rows, :], direction == 1)
        b_s[rows, :] = b
        b_end = _per_chunk_rows(_chunk_end_rows(direction, b))
        k = qkv_ref[rows, 128:256].astype(F32)
        v = qkv_ref[rows, 256:512]
        k_dec = (k * jnp.exp(b_end - b)).astype(BF16)
        st_s[g] = _dot_tn(v, _diag_blocks(k_dec))
        return carry

    lax.fori_loop(0, n_groups, local, 0, unroll=_group_unroll(n_groups))
    _state_scan(direction, n_groups, b_s, st_s, False)


def _gla_fwd(proj, lr, wgf, wgb, bgf, bgb, n_seq, lf):
    assert lf % GROUP_ROWS == 0
    n_groups = lf // GROUP_ROWS
    scale = HEAD_K ** -0.5

    def body(qkv_ref, lr_ref, wgf_ref, wgb_ref, bgf_ref, bgb_ref, o_ref, g_s, b_s, st_s):
        lr_bf = lr_ref[...].astype(BF16)
        for direction in (0, 1):
            wg_ref, bg_ref = ((wgf_ref, bgf_ref), (wgb_ref, bgb_ref))[direction]
            z, valid = _gla_gates(lr_bf, wg_ref, bg_ref, lf)
            g_s[...] = jnp.where(valid, _log_sigmoid(z) / GATE_NORM, 0.0)
            smask = _score_mask(direction)
            _gla_states(direction, n_groups, qkv_ref, g_s, b_s, st_s)

            def out(g, carry):
                rows = _group_rows(g)
                b = b_s[rows, :]
                q = qkv_ref[rows, 0:128].astype(F32) * scale
                k = qkv_ref[rows, 128:256].astype(F32)
                v = qkv_ref[rows, 256:512]
                q_in = (q * jnp.exp(b)).astype(BF16)
                k_in = (k * jnp.exp(-b)).astype(BF16)
                s = jnp.where(smask, _dot_nt(q_in, k_in), 0.0).astype(BF16)
                o = _dot(s, v) + _per_chunk_dot(q_in, st_s[g].astype(BF16), True)
                if direction == 0:
                    o_ref[rows, :] = o
                else:
                    o_ref[rows, :] = o_ref[rows, :] + o
                return carry

            lax.fori_loop(0, n_groups, out, 0, unroll=_group_unroll(n_groups))

    return pl.pallas_call(
        body, name="gla_fwd", grid=(n_seq, N_HEADS),
        in_specs=[pl.BlockSpec((lf, 512), lambda b, h: (b, N_CONV_TILES + h)),
                  pl.BlockSpec((lf, LANES), lambda b, h: (b, 0)),
                  pl.BlockSpec((None, LANES, HEAD_K), lambda b, h: (h, 0, 0)),
                  pl.BlockSpec((None, LANES, HEAD_K), lambda b, h: (h, 0, 0)),
                  pl.BlockSpec((None, 1, HEAD_K), lambda b, h: (h, 0, 0)),
                  pl.BlockSpec((None, 1, HEAD_K), lambda b, h: (h, 0, 0))],
        out_specs=pl.BlockSpec((lf, HEAD_V), lambda b, h: (b, h)),
        out_shape=jax.ShapeDtypeStruct((n_seq * lf, D), F32),
        scratch_shapes=[pltpu.VMEM((lf, HEAD_K), F32), pltpu.VMEM((lf, HEAD_K), F32),
                        pltpu.VMEM((n_groups, HEAD_V, GROUP * HEAD_K), F32)],
        compiler_params=_params(("parallel", "parallel"), 48),
    )(proj, lr, wgf, wgb, bgf, bgb)


def _gla_bwd(proj, lr, d_o, wgf, wgb, bgf, bgb, n_seq, lf, token):
    assert lf % GROUP_ROWS == 0
    n_groups = lf // GROUP_ROWS
    scale = HEAD_K ** -0.5

    def body(qkv_ref, lr_ref, do_ref, wgf_ref, wgb_ref, bgf_ref, bgb_ref, token_ref,
             dqkv_ref, dlr_ref, dwgf_ref, dwgb_ref, dbg_ref,
             g_s, b_s, fac_s, dg_s, st_s, dst_s, acc_s):
        lr_bf = lr_ref[...].astype(BF16)
        dlr = jnp.zeros((lf, LANES), F32)
        for direction in (0, 1):
            wg_ref, bg_ref = ((wgf_ref, bgf_ref), (wgb_ref, bgb_ref))[direction]
            z, valid = _gla_gates(lr_bf, wg_ref, bg_ref, lf)
            g_s[...] = jnp.where(valid, _log_sigmoid(z) / GATE_NORM, 0.0)
            fac_s[...] = jnp.where(valid, _sigmoid(-z) / GATE_NORM, 0.0)
            smask = _score_mask(direction)
            end_row = CHUNK - 1 if direction == 0 else 0
            _gla_states(direction, n_groups, qkv_ref, g_s, b_s, st_s)

            def state_grad_local(g, carry):
                rows = _group_rows(g)
                q = qkv_ref[rows, 0:128].astype(F32) * scale
                q_in = (q * jnp.exp(b_s[rows, :])).astype(BF16)
                dst_s[g] = _dot_tn(do_ref[rows, :], _diag_blocks(q_in))
                return carry

            lax.fori_loop(0, n_groups, state_grad_local, 0, unroll=_group_unroll(n_groups))
            _state_scan(direction, n_groups, b_s, dst_s, True)

            def group_grads(g, carry):
                rows = _group_rows(g)
                b = b_s[rows, :]
                ends = _chunk_end_rows(direction, b)
                b_end = _per_chunk_rows(ends)
                q = qkv_ref[rows, 0:128].astype(F32) * scale
                k = qkv_ref[rows, 128:256].astype(F32)
                v = qkv_ref[rows, 256:512]
                d_out = do_ref[rows, :]
                e_pos = jnp.exp(b)
                e_neg = jnp.exp(-b)
                e_end = jnp.exp(b_end - b)
                q_in = q * e_pos
                k_in = k * e_neg
                k_dec = k * e_end
                q_in_bf = q_in.astype(BF16)
                k_in_bf = k_in.astype(BF16)
                state = st_s[g]
                d_state = dst_s[g]
                state_bf = state.astype(BF16)
                d_state_bf = d_state.astype(BF16)
                s = jnp.where(smask, _dot_nt(q_in_bf, k_in_bf), 0.0).astype(BF16)
                ds = jnp.where(smask, _dot_nt(d_out, v), 0.0).astype(BF16)
                dv = _dot_tn(s, d_out) + _per_chunk_dot(k_dec.astype(BF16), d_state_bf, True)
                dq_in = _dot(ds, k_in_bf) + _per_chunk_dot(d_out, state_bf, False)
                dk_in = _dot_tn(ds, q_in_bf)
                dk_dec = _per_chunk_dot(v, d_state_bf, False)
                dq = dq_in * e_pos * scale
                dk = dk_in * e_neg + dk_dec * e_end
                if direction == 0:
                    acc_s[rows, 0:128] = dq
                    acc_s[rows, 128:256] = dk
                    acc_s[rows, 256:512] = dv
                else:
                    dqkv_ref[rows, 0:128] = (acc_s[rows, 0:128] + dq).astype(BF16)
                    dqkv_ref[rows, 128:256] = (acc_s[rows, 128:256] + dk).astype(BF16)
                    dqkv_ref[rows, 256:512] = (acc_s[rows, 256:512] + dv).astype(BF16)
                dkk = dk_dec * k_dec
                db = dq_in * q_in - dk_in * k_in - dkk
                d_decay = jnp.sum(d_state * state, axis=0, keepdims=True)
                db_end = [jnp.sum(dkk[r * CHUNK:(r + 1) * CHUNK, :], axis=0, keepdims=True)
                          + d_decay[:, r * HEAD_K:(r + 1) * HEAD_K] * jnp.exp(ends[r]) for r in range(GROUP)]
                row = lax.broadcasted_iota(jnp.int32, (GROUP_ROWS, HEAD_K), 0)
                at_end = row == end_row
                for r in range(1, GROUP):
                    at_end = at_end | (row == r * CHUNK + end_row)
                db = db + jnp.where(at_end, _per_chunk_rows(db_end), 0.0)
                dg_s[rows, :] = _chunk_cumsum(db, direction == 0)
                return carry

            lax.fori_loop(0, n_groups, group_grads, 0, unroll=_group_unroll(n_groups))

            dz = dg_s[...] * fac_s[...]
            dz_bf = dz.astype(BF16)
            dbg_ref[direction:direction + 1, :] = jnp.sum(dz, axis=0, keepdims=True)
            (dwgf_ref, dwgb_ref)[direction][...] = _dot_tn(lr_bf, dz_bf)
            dlr = dlr + _dot_nt(dz_bf, wg_ref[...])

        @pl.when(pl.program_id(1) == 0)
        def _():
            dlr_ref[...] = dlr

        @pl.when(pl.program_id(1) != 0)
        def _():
            dlr_ref[...] = dlr_ref[...] + dlr

    gate_w = pl.BlockSpec((None, LANES, HEAD_K), lambda b, h: (h, 0, 0))
    gate_b = pl.BlockSpec((None, 1, HEAD_K), lambda b, h: (h, 0, 0))
    return pl.pallas_call(
        body, name="gla_bwd", grid=(n_seq, N_HEADS),
        in_specs=[pl.BlockSpec((lf, 512), lambda b, h: (b, N_CONV_TILES + h)),
                  pl.BlockSpec((lf, LANES), lambda b, h: (b, 0)),
                  pl.BlockSpec((lf, HEAD_V), lambda b, h: (b, h)),
                  gate_w, gate_w, gate_b, gate_b,
                  pl.BlockSpec((8, LANES), lambda b, h: (0, 0))],
        out_specs=[pl.BlockSpec((lf, 512), lambda b, h: (b, h)),
                   pl.BlockSpec((lf, LANES), lambda b, h: (b, 0)),
                   pl.BlockSpec((None, None, LANES, HEAD_K), lambda b, h: (b, h, 0, 0)),
                   pl.BlockSpec((None, None, LANES, HEAD_K), lambda b, h: (b, h, 0, 0)),
                   pl.BlockSpec((None, None, 2, HEAD_K), lambda b, h: (b, h, 0, 0))],
        out_shape=[jax.ShapeDtypeStruct((n_seq * lf, W_GLA), BF16),
                   jax.ShapeDtypeStruct((n_seq * lf, LANES), F32),
                   jax.ShapeDtypeStruct((n_seq, N_HEADS, LANES, HEAD_K), F32),
                   jax.ShapeDtypeStruct((n_seq, N_HEADS, LANES, HEAD_K), F32),
                   jax.ShapeDtypeStruct((n_seq, N_HEADS, 2, HEAD_K), F32)],
        scratch_shapes=[pltpu.VMEM((lf, HEAD_K), F32), pltpu.VMEM((lf, HEAD_K), F32),
                        pltpu.VMEM((lf, HEAD_K), F32), pltpu.VMEM((lf, HEAD_K), F32),
                        pltpu.VMEM((n_groups, HEAD_V, GROUP * HEAD_K), F32),
                        pltpu.VMEM((n_groups, HEAD_V, GROUP * HEAD_K), F32),
                        pltpu.VMEM((lf, 512), F32)],
        compiler_params=_params(("parallel", "arbitrary"), 56),
    )(proj, lr, d_o, wgf, wgb, bgf, bgb, token)


def _tail(h, tgt, yc, o, proj, w3, gamma, g_post, lf):
    t_rows = h.shape[0]
    tm = _pick_tile(t_rows, 256, CHUNK)
    n_chunks = lf // CHUNK
    per_tile = tm // CHUNK

    def body(h_ref, tgt_ref, yc_ref, o_ref, r_ref, ma_ref, mb_ref, w_hbm, gamma_ref, gpost_ref,
             dres_ref, yg_ref, merged_ref, dout_ref, dpc_ref, dpg_ref, dyc_ref, do_ref, dtail_ref,
             loss_ref, dgpost_ref, dgamma_ref, w_s, w_sem):
        i = pl.program_id(0)

        @pl.when(i == 0)
        def _():
            cp = pltpu.make_async_copy(w_hbm, w_s, w_sem)
            cp.start()
            cp.wait()
            loss_ref[...] = jnp.zeros_like(loss_ref)
            dgpost_ref[...] = jnp.zeros_like(dgpost_ref)
            dgamma_ref[...] = jnp.zeros_like(dgamma_ref)

        gamma = gamma_ref[...]
        o = o_ref[...]
        r = r_ref[...].astype(F32)
        sr = _sigmoid(r)
        silu_r = r * sr
        n_parts, rstd_parts = [], []
        for hd in range(N_HEADS):
            oh = o[:, hd * HEAD_V:(hd + 1) * HEAD_V]
            rstd = lax.rsqrt(jnp.mean(oh * oh, axis=-1, keepdims=True) + EPS)
            n_parts.append(oh * rstd)
            rstd_parts.append(rstd)
        n = jnp.concatenate(n_parts, axis=-1)
        gamma_t = jnp.concatenate([gamma] * N_HEADS, axis=-1)
        yg = n * gamma_t * silu_r
        yg_bf = yg.astype(BF16)
        yg_ref[...] = yg_bf
        yc = yc_ref[...]
        pc = _dot(yc, w_s[0])
        pg = _dot(yg_bf, w_s[1])
        sa = _sigmoid(ma_ref[...].astype(F32))
        sb = _sigmoid(mb_ref[...].astype(F32))
        merged = (sa * pc + sb * pg).astype(BF16)
        merged_ref[...] = merged
        out = _dot(merged, w_s[2])
        rstd2 = lax.rsqrt(jnp.mean(out * out, axis=-1, keepdims=True) + EPS)
        nn = out * rstd2
        gpost = gpost_ref[...]
        y = h_ref[...] + nn * gpost

        rowi = lax.broadcasted_iota(jnp.int32, (tm, 1), 0)
        keep = jnp.zeros((tm, 1), F32)
        for kk in range(per_tile):
            is_tok = ((i * per_tile + kk) % n_chunks) != 0
            f = jnp.where(is_tok, 1.0, 0.0)
            keep = jnp.where((rowi >= kk * CHUNK) & (rowi < (kk + 1) * CHUNK), f, keep)
        diff = (y - tgt_ref[...]) * keep
        loss_ref[...] += jnp.sum(diff * diff) * (0.5 / D)
        dy = diff * (1.0 / D)
        dres_ref[...] = dy
        dgpost_ref[...] += jnp.sum(dy * nn, axis=0, keepdims=True)
        dn = dy * gpost
        dout_f = rstd2 * (dn - nn * jnp.mean(dn * nn, axis=-1, keepdims=True))
        dout = dout_f.astype(BF16)
        dout_ref[...] = jnp.transpose(dout_f).astype(BF16)
        dmerged = _dot_nt(dout, w_s[2])
        dpc_f = dmerged * sa
        dpg_f = dmerged * sb
        dpc = dpc_f.astype(BF16)
        dpg = dpg_f.astype(BF16)
        dpc_ref[...] = jnp.transpose(dpc_f).astype(BF16)
        dpg_ref[...] = jnp.transpose(dpg_f).astype(BF16)
        dtail_ref[:, D:2 * D] = (dmerged * pc * (sa * (1.0 - sa))).astype(BF16)
        dtail_ref[:, 2 * D:3 * D] = (dmerged * pg * (sb * (1.0 - sb))).astype(BF16)
        dyc_ref[...] = _dot_nt(dpc, w_s[0]).astype(BF16)
        dyg = _dot_nt(dpg, w_s[1])
        dtail_ref[:, 0:D] = (dyg * n * gamma_t * (sr * (1.0 + r * (1.0 - sr)))).astype(BF16)
        dgam_full = jnp.sum(dyg * n * silu_r, axis=0, keepdims=True)
        dgam = dgam_full[:, 0:HEAD_V]
        for hd in range(1, N_HEADS):
            dgam = dgam + dgam_full[:, hd * HEAD_V:(hd + 1) * HEAD_V]
        dgamma_ref[...] += dgam
        dng = dyg * gamma_t * silu_r
        do_parts = []
        for hd in range(N_HEADS):
            sl = slice(hd * HEAD_V, (hd + 1) * HEAD_V)
            dnh = dng[:, sl]
            nh = n_parts[hd]
            do_parts.append(rstd_parts[hd] * (dnh - nh * jnp.mean(dnh * nh, axis=-1, keepdims=True)))
        do_ref[...] = jnp.concatenate(do_parts, axis=-1).astype(BF16)

    row = lambda c: pl.BlockSpec((tm, D), lambda i: (i, c))
    col = pl.BlockSpec((D, tm), lambda i: (0, i))
    const = lambda shape: pl.BlockSpec(shape, lambda i: (0, 0))
    act = jax.ShapeDtypeStruct((t_rows, D), BF16)
    act_t = jax.ShapeDtypeStruct((D, t_rows), BF16)
    return pl.pallas_call(
        body, name="tail", grid=(t_rows // tm,),
        in_specs=[row(0), row(0), row(0), row(0), row(6), row(7), row(8),
                  pl.BlockSpec(memory_space=pl.ANY), const((1, HEAD_V)), const((1, D))],
        out_specs=[row(0)] * 3 + [col] * 3 + [row(0)] * 2
                  + [pl.BlockSpec((tm, W_TAIL), lambda i: (i, 0)),
                     const((8, LANES)), const((1, D)), const((1, HEAD_V))],
        out_shape=[jax.ShapeDtypeStruct((t_rows, D), F32)] + [act] * 2 + [act_t] * 3 + [act] * 2
                  + [jax.ShapeDtypeStruct((t_rows, W_TAIL), BF16),
                     jax.ShapeDtypeStruct((8, LANES), F32),
                     jax.ShapeDtypeStruct((1, D), F32),
                     jax.ShapeDtypeStruct((1, HEAD_V), F32)],
        scratch_shapes=[pltpu.VMEM((3, D, D), BF16), pltpu.SemaphoreType.DMA],
        compiler_params=_params(("arbitrary",), 56),
    )(h, tgt, yc, o, proj, proj, proj, w3, gamma, g_post)


def _wgrad_t(a_t, b, name, out_dtype=BF16):
    m, t_rows = a_t.shape
    n = b.shape[1]
    tn = D if n % D == 0 else n
    tk = _pick_tile(t_rows, 768, LANES)
    n_k = t_rows // tk

    def body(a_ref, b_ref, o_ref, acc):
        k = pl.program_id(1)

        @pl.when(k == 0)
        def _():
            acc[...] = jnp.zeros_like(acc)

        acc[...] += _dot(a_ref[...], b_ref[...].astype(BF16))

        @pl.when(k == n_k - 1)
        def _():
            o_ref[...] = jnp.transpose(acc[...]).astype(out_dtype)

    return pl.pallas_call(
        body, name=name, grid=(n // tn, n_k),
        in_specs=[pl.BlockSpec((m, tk), lambda j, k: (0, k)),
                  pl.BlockSpec((tk, tn), lambda j, k: (k, j))],
        out_specs=pl.BlockSpec((tn, m), lambda j, k: (j, 0)),
        out_shape=jax.ShapeDtypeStruct((n, m), out_dtype),
        scratch_shapes=[pltpu.VMEM((m, tn), F32)],
        compiler_params=_params(("parallel", "arbitrary"), 48),
    )(a_t, b)


def _dgrad_in(dpc, dpg, dpt, dlr, w_full_t, h, g_pre, dres, token):
    t_rows = h.shape[0]
    tm = _pick_tile(t_rows, 256, 16)
    n_main = N_MAIN

    def body(dpc_ref, dpg_ref, dpt_ref, dlr_ref, w_hbm, h_ref, g_ref, dres_ref, token_ref,
             dh_ref, dg_ref, w_s, wlr_s, w_sems):
        @pl.when(pl.program_id(0) == 0)
        def _():
            _load_weights(w_hbm, w_s, wlr_s, w_sems)
            dg_ref[...] = jnp.zeros_like(dg_ref)

        du = _dot(dlr_ref[...].astype(BF16), wlr_s[...])
        du += _dot(dpc_ref[...], w_s[0:W_CONV, :])
        du += _dot(dpg_ref[...], w_s[W_CONV:W_CONV + W_GLA, :])
        du += _dot(dpt_ref[...], w_s[W_CONV + W_GLA:n_main, :])
        hh = h_ref[...]
        rstd = lax.rsqrt(jnp.mean(hh * hh, axis=-1, keepdims=True) + EPS)
        xhat = hh * rstd
        dg_ref[...] += jnp.sum(du * xhat, axis=0, keepdims=True)
        dx = du * g_ref[...]
        dh_ref[...] = rstd * (dx - xhat * jnp.mean(dx * xhat, axis=-1, keepdims=True)) + dres_ref[...]

    row = lambda width: pl.BlockSpec((tm, width), lambda i: (i, 0))
    return pl.pallas_call(
        body, name="dgrad_in", grid=(t_rows // tm,),
        in_specs=[row(W_CONV), row(W_GLA), row(W_TAIL), row(LANES),
                  pl.BlockSpec(memory_space=pl.ANY),
                  row(D), pl.BlockSpec((1, D), lambda i: (0, 0)), row(D),
                  pl.BlockSpec((8, LANES), lambda i: (0, 0))],
        out_specs=[row(D), pl.BlockSpec((1, D), lambda i: (0, 0))],
        out_shape=[jax.ShapeDtypeStruct((t_rows, D), F32), jax.ShapeDtypeStruct((1, D), F32)],
        scratch_shapes=[pltpu.VMEM((n_main, D), BF16), pltpu.VMEM((LANES, D), BF16),
                        pltpu.SemaphoreType.DMA((N_WEIGHT_COPIES,))],
        compiler_params=_params(("arbitrary",), 56),
    )(dpc, dpg, dpt, dlr, w_full_t, h, g_pre, dres, token)


def _reference_rows(g_conv, g_gla, g_tail, g_lr):
    conv = g_conv.reshape(N_CONV_TILES, 4, 128, D).transpose(1, 0, 2, 3).reshape(W_CONV, D)
    gla = g_gla.reshape(N_HEADS, 512, D)
    q = gla[:, 0:128].reshape(N_HEADS * HEAD_K, D)
    k = gla[:, 128:256].reshape(N_HEADS * HEAD_K, D)
    v = gla[:, 256:512].reshape(N_HEADS * HEAD_V, D)
    return jnp.concatenate([conv, q, k, v, g_tail[0:D], g_lr[0:2 * RANK], g_tail[D:3 * D]], axis=0)


def kernel(x, meta_tokens, norm_pre, w_in, conv_w, w_gate_fwd, b_gate_fwd, w_gate_bwd, b_gate_bwd, gla_norm, w_out_conv, w_out_gla, w_merge_out, norm_post, loss_target, m_meta_tokens, m_norm_pre, m_w_in, m_conv_w, m_w_gate_fwd, m_b_gate_fwd, m_w_gate_bwd, m_b_gate_bwd, m_gla_norm, m_w_out_conv, m_w_out_gla, m_w_merge_out, m_norm_post, v_meta_tokens, v_norm_pre, v_w_in, v_conv_w, v_w_gate_fwd, v_b_gate_fwd, v_w_gate_bwd, v_b_gate_bwd, v_gla_norm, v_w_out_conv, v_w_out_gla, v_w_merge_out, v_norm_post):
    n_seq, seq, _ = x.shape
    lf = CHUNK + seq
    t_rows = n_seq * lf
    shard = 2 * lax.axis_index("x") + lax.axis_index("y")
    shard_arr = jnp.reshape(shard, (1,)).astype(jnp.int32)

    w_in_slots = _cast_into_slot(jnp.transpose(w_in[0]), shard_arr, "cast_w_in")
    w_out_bf = _cast_bf16(jnp.concatenate([w_out_conv[0], w_out_gla[0], w_merge_out[0]], axis=0), "cast_w_out")
    w_in_all, meta_all, conv_all, wgf_all, wgb_all = _gather_via_sibling(
        "gather_w_in", [w_in_slots, meta_tokens, conv_w[0], w_gate_fwd[0], w_gate_bwd[0]],
        (True, False, False, False, False))
    w_out_state, _ = _plane_start("gather_w_out_start", [w_out_bf], "gather", wgb_all)

    w_full_t = w_in_all.reshape(N_IN, D)
    meta_full = jnp.transpose(meta_all, (1, 0, 2)).reshape(N_META, D)
    conv_full = jnp.transpose(conv_all, (1, 0, 2)).reshape(3, D)
    wgf = jnp.pad(wgf_all, ((0, 0), (0, LANES - RANK), (0, 0))).astype(BF16)
    wgb = jnp.pad(wgb_all, ((0, 0), (RANK, LANES - 2 * RANK), (0, 0))).astype(BF16)
    bgf = b_gate_fwd.reshape(N_HEADS, 1, HEAD_K)
    bgb = b_gate_bwd.reshape(N_HEADS, 1, HEAD_K)

    head = jnp.concatenate([jnp.zeros((PAD_FRONT, D), F32), meta_full], axis=0)
    h = jnp.concatenate([jnp.broadcast_to(head[None], (n_seq, CHUNK, D)), x], axis=1).reshape(t_rows, D)
    tgt = jnp.pad(loss_target, ((0, 0), (CHUNK, 0), (0, 0))).reshape(t_rows, D)

    proj, u_t, lr = _in_proj(h, norm_pre, w_full_t)
    yc = _conv_fwd(proj, conv_full, n_seq, lf)
    o = _gla_fwd(proj, lr, wgf, wgb, bgf, bgb, n_seq, lf)
    (w_out_landed,) = _plane_wait("gather_w_out_wait", w_out_state, "gather", o)
    slot_ids = lax.broadcasted_iota(jnp.int32, (4, 1, 1), 0)
    w_out_all = jnp.where(slot_ids == shard, w_out_bf[None], w_out_landed)
    w3 = jnp.transpose(w_out_all.reshape(4, 3, D // 4, D), (1, 0, 2, 3)).reshape(3, D, D)
    (dres, yg, merged, dout_t, dpc_t, dpg_t, dyc, d_o, dtail, loss_acc, d_gpost, d_gamma) = _tail(
        h, tgt, yc, o, proj, w3, gla_norm, norm_post, lf)
    g_w_oc = _wgrad_t(dpc_t, yc, "wgrad_out_conv")
    g_w_og = _wgrad_t(dpg_t, yg, "wgrad_out_gla")
    g_w_mo = _wgrad_t(dout_t, merged, "wgrad_merge_out")
    g_out_slots = jnp.concatenate([g.reshape(4, D // 4, D) for g in (g_w_oc, g_w_og, g_w_mo)], axis=1)
    out_state, out_token = _plane_start("scatter_out_grads_start", [g_out_slots], "scatter", g_w_mo)
    dgla, dlr, dwgf_p, dwgb_p, dbg_p = _gla_bwd(proj, lr, d_o, wgf, wgb, bgf, bgb, n_seq, lf, out_token)
    (got_out,) = _plane_wait("scatter_out_grads_wait", out_state, "scatter", dlr)
    dconv, dconvw_p = _conv_bwd(proj, conv_full, dyc, n_seq, lf)
    g_conv = _wgrad_t(u_t, dconv, "wgrad_in_conv")
    g_gla = _wgrad_t(u_t, dgla, "wgrad_in_gla")
    g_tail = _wgrad_t(u_t, dtail, "wgrad_in_tail")
    g_lr = _wgrad_t(u_t, dlr, "wgrad_in_lr")

    g_in_slots = _reference_rows(g_conv, g_gla, g_tail, g_lr).reshape(4, SHARD_IN, D)
    in_state, in_token = _plane_start("scatter_in_grads_start", [g_in_slots], "scatter", g_lr)
    dh, d_gpre = _dgrad_in(dconv, dgla, dtail, dlr, w_full_t, h, norm_pre, dres, in_token)
    (got_in,) = _plane_wait("scatter_in_grads_wait", in_state, "scatter", d_gpre)

    plane_in = _sum_slots(got_in, "sum_w_in_grads", own=g_in_slots, slot=shard_arr)
    plane_out = _sum_slots(got_out, "sum_w_out_grads", own=g_out_slots, slot=shard_arr)
    swap_state, swap_token = _plane_start("swap_plane_sums_start", [plane_in, plane_out], "swap", plane_out)

    dh3 = dh.reshape(n_seq, lf, D)
    grad_x = dh3[:, CHUNK:, :]

    d_meta = jnp.sum(dh3[:, PAD_FRONT:CHUNK, :], axis=0)
    d_convw = jnp.sum(dconvw_p, axis=0)
    d_wgf = jnp.transpose(jnp.sum(dwgf_p, axis=0)[:, 0:RANK, :], (1, 0, 2)).reshape(RANK, N_HEADS * HEAD_K)
    d_wgb = jnp.transpose(jnp.sum(dwgb_p, axis=0)[:, RANK:2 * RANK, :], (1, 0, 2)).reshape(RANK, N_HEADS * HEAD_K)
    d_bg = jnp.sum(dbg_p, axis=0)
    d_bgf = d_bg[:, 0, :].reshape(1, N_HEADS * HEAD_K)
    d_bgb = d_bg[:, 1, :].reshape(1, N_HEADS * HEAD_K)
    loss_part = loss_acc[0:1, :] + swap_token[0:1, :]
    partials = [d_meta, d_convw, d_wgf, d_wgb, d_gpre, d_bgf, d_bgb, d_gamma, d_gpost, loss_part]
    (g_meta, g_convw, g_wgf, g_wgb, g_npre, g_bgf, g_bgb, g_gnorm, g_npost, loss_row) = _sum_small(
        _exchange("gather_small_grads", partials, ALL_FLIPS, (4, 2, 1), "gather"), "sum_small_grads")
    loss = loss_row[0, 0]
    small_out = _adamw_small(
        [(meta_tokens, g_meta, m_meta_tokens, v_meta_tokens), (norm_pre, g_npre, m_norm_pre, v_norm_pre),
         (conv_w, g_convw, m_conv_w, v_conv_w), (w_gate_fwd, g_wgf, m_w_gate_fwd, v_w_gate_fwd),
         (b_gate_fwd, g_bgf, m_b_gate_fwd, v_b_gate_fwd), (w_gate_bwd, g_wgb, m_w_gate_bwd, v_w_gate_bwd),
         (b_gate_bwd, g_bgb, m_b_gate_bwd, v_b_gate_bwd), (gla_norm, g_gnorm, m_gla_norm, v_gla_norm),
         (norm_post, g_npost, m_norm_post, v_norm_post)], shard_arr, "adamw_small")

    other_in, other_out = _plane_wait("swap_plane_sums_wait", swap_state, "swap", small_out[0][0])
    big_in = _adamw(jnp.transpose(w_in[0]), [plane_in, other_in], jnp.transpose(m_w_in[0]), jnp.transpose(v_w_in[0]),
                    "adamw_w_in")
    out_params = ((w_out_conv, m_w_out_conv, v_w_out_conv), (w_out_gla, m_w_out_gla, v_w_out_gla),
                  (w_merge_out, m_w_merge_out, v_w_merge_out))
    big_out = [_adamw(w[0], [plane_out, other_out], m[0], v[0], f"adamw_w_out_{i}", grad_row=i * (D // 4))
               for i, (w, m, v) in enumerate(out_params)]

    results = []
    for kind in range(4):
        small_kind = [p[kind] for p in small_out]
        w_in_part = jnp.transpose(big_in[kind])[None]
        outs3 = [big_out[i][kind][None] for i in range(3)]
        results.extend(small_kind[0:2] + [w_in_part] + small_kind[2:8] + outs3 + small_kind[8:9])
    return (loss, grad_x, *results)
```

```python
import functools

import jax
import jax.numpy as jnp
from jax import lax
from jax.experimental import pallas as pl
from jax.experimental.pallas import tpu as pltpu

F32 = jnp.float32
BF16 = jnp.bfloat16
MESH = pl.DeviceIdType.MESH

D = 1024
N_META = 16
CHUNK = 64
PAD_FRONT = CHUNK - N_META
N_HEADS = 4
HEAD_K = 128
HEAD_V = 256
RANK = 16
EPS = 1e-6
GATE_NORM = 16.0
N_IN = 9248
SHARD_IN = N_IN // 4
LANES = 128
N_CONV_TILES = 8
W_CONV = 4096
W_GLA = 2048
W_TAIL = 3072
N_MAIN = W_CONV + W_GLA + W_TAIL
OFF_Q, OFF_K, OFF_V, OFF_R = 4096, 4608, 5120, 6144
OFF_LR, OFF_MA, OFF_MB = 7168, 7200, 8224
MIB = 1024 * 1024

ADAM_LR = 0.001
ADAM_B1 = 0.9
ADAM_B2 = 0.999
ADAM_EPS = 1e-08
ADAM_WD = 0.01
ADAM_STEP = 10


def _params(sem=None, vmem_mib=None):
    return pltpu.CompilerParams(
        dimension_semantics=sem,
        vmem_limit_bytes=None if vmem_mib is None else vmem_mib * MIB)


def _pick_tile(n, target, mult):
    best = None
    for t in range(mult, min(n, target) + 1, mult):
        if n % t == 0:
            best = t
    return n if best is None else best


def _sigmoid(v):
    return 1.0 / (1.0 + jnp.exp(-v))


def _log_sigmoid(v):
    return jnp.minimum(v, 0.0) - jnp.log(1.0 + jnp.exp(-jnp.abs(v)))


def _dot(a, b):
    return jnp.dot(a, b, preferred_element_type=F32)


def _dot_nt(a, b):
    return lax.dot_general(a, b, (((1,), (1,)), ((), ())), preferred_element_type=F32)


def _dot_tn(a, b):
    return lax.dot_general(a, b, (((0,), (0,)), ((), ())), preferred_element_type=F32)


PLANE_FLIPS = ((1, 0, 0), (0, 1, 0), (1, 1, 0))
ALL_FLIPS = tuple((m >> 2 & 1, m >> 1 & 1, m & 1) for m in range(1, 8))
SIBLING_FLIPS = ((0, 0, 1),)


def _exchange(name, arrs, flips, slot_weights, mode):
    n = len(arrs)
    n_slots = 1
    for w in slot_weights:
        n_slots += w
    if mode == "gather":
        out_shape = [jax.ShapeDtypeStruct((n_slots,) + a.shape, a.dtype) for a in arrs]
    else:
        out_shape = [jax.ShapeDtypeStruct(a.shape, a.dtype) for a in arrs]

    def body(*refs):
        ins, outs = refs[:n], refs[n:2 * n]
        send_sems, recv_sems, local_sems = refs[2 * n:]
        pos = (lax.axis_index("x"), lax.axis_index("y"), lax.axis_index("c"))

        def slot_of(p):
            return p[0] * slot_weights[0] + p[1] * slot_weights[1] + p[2] * slot_weights[2]

        peers = [tuple(1 - pos[a] if f[a] else pos[a] for a in range(3)) for f in flips]
        me = slot_of(pos)
        local = []
        sends = []
        for i in range(n):
            if mode != "swap":
                src = ins[i] if mode == "gather" else ins[i].at[me]
                cp = pltpu.make_async_copy(src, outs[i].at[me], local_sems.at[i])
                cp.start()
                local.append(cp)
            for k, peer in enumerate(peers):
                if mode == "gather":
                    src, dst = ins[i], outs[i].at[me]
                elif mode == "scatter":
                    src, dst = ins[i].at[slot_of(peer)], outs[i].at[me]
                else:
                    src, dst = ins[i], outs[i]
                cp = pltpu.make_async_remote_copy(
                    src_ref=src, dst_ref=dst, send_sem=send_sems.at[i, k], recv_sem=recv_sems.at[i, k],
                    device_id=peer, device_id_type=MESH)
                cp.start()
                sends.append(cp)
        for i in range(n):
            for k, peer in enumerate(peers):
                if mode == "gather":
                    src, dst = ins[i], outs[i].at[slot_of(peer)]
                elif mode == "scatter":
                    src, dst = ins[i].at[me], outs[i].at[slot_of(peer)]
                else:
                    src, dst = ins[i], outs[i]
                arrival = pltpu.make_async_remote_copy(
                    src_ref=src, dst_ref=dst, send_sem=send_sems.at[i, k], recv_sem=recv_sems.at[i, k],
                    device_id=peer, device_id_type=MESH)
                arrival.wait_recv()
        for cp in sends:
            cp.wait_send()
        for cp in local:
            cp.wait()

    hbm = pl.BlockSpec(memory_space=pl.ANY)
    outs = pl.pallas_call(
        body, name=name, out_shape=out_shape,
        in_specs=[hbm] * n, out_specs=[hbm] * n,
        scratch_shapes=[pltpu.SemaphoreType.DMA((n, len(flips))),
                        pltpu.SemaphoreType.DMA((n, len(flips))),
                        pltpu.SemaphoreType.DMA((n,))],
        compiler_params=pltpu.CompilerParams(has_side_effects=True),
    )(*arrs)
    return list(outs)


def _gather_via_sibling(name, arrs, slotted):
    n = len(arrs)
    out_shape = [jax.ShapeDtypeStruct(a.shape if slotted[i] else (4,) + a.shape, a.dtype)
                 for i, a in enumerate(arrs)]

    def body(*refs):
        ins, outs = refs[:n], refs[n:2 * n]
        send_sems, recv_sems, local_sems = refs[2 * n:]
        x, y, c = lax.axis_index("x"), lax.axis_index("y"), lax.axis_index("c")
        me = 2 * x + y
        chips = [(1 - x, y), (x, 1 - y), (1 - x, 1 - y)]

        def half(ref, which):
            rows = ref.shape[0]
            cut = rows // 2 // 16 * 16
            return ref.at[pl.ds(0, cut)] if which == 0 else ref.at[pl.ds(cut, rows - cut)]

        def copy(src, dst, i, k, to):
            return pltpu.make_async_remote_copy(
                src_ref=src, dst_ref=dst, send_sem=send_sems.at[i, k], recv_sem=recv_sems.at[i, k],
                device_id=to, device_id_type=MESH)

        def run(mine):
            other = 1 - mine
            local, sends = [], []
            whole = [(not slotted[i]) and arrs[i].shape[0] < 32 for i in range(n)]
            for i in range(n):
                own = outs[i].at[me] if slotted[i] else ins[i]
                if not slotted[i]:
                    cp = pltpu.make_async_copy(ins[i], outs[i].at[me], local_sems.at[i])
                    cp.start()
                    local.append(cp)
                for k, (px, py) in enumerate(chips):
                    if whole[i]:
                        cp = copy(own, outs[i].at[me], i, k, (px, py, mine))
                    elif k < 2:
                        cp = copy(half(own, mine), half(outs[i].at[me], mine), i, k, (px, py, mine))
                    else:
                        continue
                    cp.start()
                    sends.append(cp)
            via = mine
            for k in (via, 1 - via, 2):
                px, py = chips[k]
                slot = 2 * px + py
                source = (px, py, mine) if k < 2 else chips[1 - via] + (mine,)
                for i in range(n):
                    if whole[i]:
                        copy(outs[i].at[slot], outs[i].at[slot], i, k, (px, py, mine)).wait_recv()
                        continue
                    landed = half(outs[i].at[slot], mine)
                    copy(landed, landed, i, k, source).wait_recv()
                    if k == via:
                        cp = copy(landed, landed, i, 2, chips[1 - via] + (mine,))
                        cp.start()
                        sends.append(cp)
                    cp = copy(landed, landed, i, 3 + k, (x, y, other))
                    cp.start()
                    sends.append(cp)
            for k, (px, py) in enumerate(chips):
                slot = 2 * px + py
                for i in range(n):
                    if whole[i]:
                        continue
                    passed = half(outs[i].at[slot], other)
                    copy(passed, passed, i, 3 + k, (x, y, other)).wait_recv()
            for cp in sends:
                cp.wait_send()
            for cp in local:
                cp.wait()

        for mine in (0, 1):
            pl.when(c == mine)(functools.partial(run, mine))

    hbm = pl.BlockSpec(memory_space=pl.ANY)
    outs = pl.pallas_call(
        body, name=name, out_shape=out_shape,
        in_specs=[hbm] * n, out_specs=[hbm] * n,
        scratch_shapes=[pltpu.SemaphoreType.DMA((n, 6)), pltpu.SemaphoreType.DMA((n, 6)),
                        pltpu.SemaphoreType.DMA((n,))],
        input_output_aliases={i: i for i in range(n) if slotted[i]},
        compiler_params=pltpu.CompilerParams(has_side_effects=True),
    )(*arrs)
    return list(outs)


HBM_SPEC = pl.BlockSpec(memory_space=pltpu.HBM)
SEM_SPEC = pl.BlockSpec(memory_space=pltpu.SEMAPHORE)
DATAFLOW = pltpu.SideEffectType.DATAFLOW_SIDE_EFFECTING


def _split_peers(mode):
    x, y, c = lax.axis_index("x"), lax.axis_index("y"), lax.axis_index("c")
    if mode == "swap":
        return 0, [((x, y, 1 - c), 0)]
    return 2 * x + y, [((1 - x, y, c), 2 * (1 - x) + y), ((x, 1 - y, c), 2 * x + 1 - y),
                       ((1 - x, 1 - y, c), 2 * (1 - x) + 1 - y)]


def _split_refs(mode, src, landing, me, peer_slot):
    if mode == "gather":
        return src, landing.at[me]
    if mode == "scatter":
        return src.at[peer_slot], landing.at[me]
    return src, landing


def _plane_start(name, arrs, mode, after):
    n = len(arrs)
    n_peers = 1 if mode == "swap" else 3
    lands = [lax.empty(((4,) + a.shape) if mode == "gather" else a.shape, a.dtype) for a in arrs]

    def body(*refs):
        srcs, landing = refs[:n], refs[n:2 * n]
        send_sems, recv_sems = refs[2 * n + 1], refs[2 * n + 2]
        token = refs[-1]
        me, peers = _split_peers(mode)
        for i in range(n):
            for k, (peer, peer_slot) in enumerate(peers):
                src, dst = _split_refs(mode, srcs[i], landing[i], me, peer_slot)
                pltpu.make_async_remote_copy(
                    src_ref=src, dst_ref=dst, send_sem=send_sems.at[n_peers * i + k],
                    recv_sem=recv_sems.at[n_peers * i + k], device_id=peer, device_id_type=MESH).start()
        token[...] = jnp.zeros_like(token)

    hbm_in = [pltpu.with_memory_space_constraint(a, pltpu.HBM) for a in list(arrs) + lands]
    out = pl.pallas_call(
        body, name=name,
        out_shape=[pltpu.SemaphoreType.DMA((n_peers * n,)), pltpu.SemaphoreType.DMA((n_peers * n,))]
                  + [pltpu.HBM(a.shape, a.dtype) for a in hbm_in]
                  + [jax.ShapeDtypeStruct((8, LANES), F32)],
        in_specs=[HBM_SPEC] * (2 * n) + [pl.BlockSpec(memory_space=pl.ANY)],
        out_specs=[SEM_SPEC, SEM_SPEC] + [HBM_SPEC] * (2 * n) + [pl.BlockSpec(memory_space=pltpu.VMEM)],
        input_output_aliases={i: 2 + i for i in range(2 * n)},
        compiler_params=pltpu.CompilerParams(has_side_effects=DATAFLOW),
    )(*hbm_in, after)
    return out[:-1], out[-1]


def _plane_wait(name, state, mode, after):
    send_sems, recv_sems = state[0], state[1]
    bufs = list(state[2:])
    n = len(bufs) // 2
    n_peers = 1 if mode == "swap" else 3

    def body(*refs):
        srcs, landing = refs[:n], refs[n:2 * n]
        send_sems, recv_sems = refs[2 * n], refs[2 * n + 1]
        me, peers = _split_peers(mode)
        for i in range(n):
            for k, (peer, peer_slot) in enumerate(peers):
                src, _ = _split_refs(mode, srcs[i], landing[i], me, peer_slot)
                arrived = landing[i] if mode == "swap" else landing[i].at[peer_slot]
                cp = pltpu.make_async_remote_copy(
                    src_ref=src, dst_ref=arrived, send_sem=send_sems.at[n_peers * i + k],
                    recv_sem=recv_sems.at[n_peers * i + k], device_id=peer, device_id_type=MESH)
                cp.wait_send()
                cp.wait_recv()

    out = pl.pallas_call(
        body, name=name,
        out_shape=[pltpu.HBM(a.shape, a.dtype) for a in bufs],
        in_specs=[HBM_SPEC] * (2 * n) + [SEM_SPEC, SEM_SPEC, pl.BlockSpec(memory_space=pl.ANY)],
        out_specs=[HBM_SPEC] * (2 * n),
        input_output_aliases={i: i for i in range(2 * n)},
        compiler_params=pltpu.CompilerParams(has_side_effects=DATAFLOW),
    )(*bufs, send_sems, recv_sems, after)
    return list(out[n:])


def _tile_2d(rows, cols, row_mult, max_elems=512 * 1024):
    if rows % row_mult == 0:
        rt = _pick_tile(rows, max(row_mult, max_elems // cols), row_mult)
        return (rt, cols), rows // rt, lambda i: (i, 0)
    ct = _pick_tile(cols, max(LANES, max_elems // rows), LANES)
    return (rows, ct), cols // ct, lambda i: (0, i)


def _cast_bf16(a, name):
    block, steps, index = _tile_2d(a.shape[0], a.shape[1], 16)

    def body(a_ref, o_ref):
        o_ref[...] = a_ref[...].astype(BF16)

    return pl.pallas_call(
        body, name=name, grid=(steps,),
        in_specs=[pl.BlockSpec(block, index)],
        out_specs=pl.BlockSpec(block, index),
        out_shape=jax.ShapeDtypeStruct(a.shape, BF16),
        compiler_params=_params(("parallel",)),
    )(a)


def _cast_into_slot(a, slot, name):
    block, steps, index = _tile_2d(a.shape[0], a.shape[1], 16)

    def body(slot_ref, a_ref, o_ref):
        o_ref[...] = a_ref[...].astype(BF16)

    return pl.pallas_call(
        body, name=name,
        grid_spec=pltpu.PrefetchScalarGridSpec(
            num_scalar_prefetch=1, grid=(steps,),
            in_specs=[pl.BlockSpec(block, lambda i, s: index(i))],
            out_specs=pl.BlockSpec((None,) + block, lambda i, s: (s[0],) + index(i))),
        out_shape=jax.ShapeDtypeStruct((4,) + a.shape, BF16),
        compiler_params=_params(("arbitrary",)),
    )(slot, a)


def _sum_slots(buf, name, own=None, slot=None):
    n_slots, rows, cols = buf.shape
    (br, bc), steps, index = _tile_2d(rows, cols, 16, 320 * 1024)

    def body(*refs):
        if own is None:
            b_ref, o_ref = refs
        else:
            slot_ref, b_ref, own_ref, o_ref = refs
        acc = None
        for s in range(n_slots):
            term = b_ref[s] if own is None else jnp.where(slot_ref[0] == s, own_ref[...], b_ref[s])
            acc = term.astype(F32) if acc is None else acc + term.astype(F32)
        o_ref[...] = acc

    out_shape = jax.ShapeDtypeStruct((rows, cols), F32)
    if own is None:
        return pl.pallas_call(
            body, name=name, grid=(steps,),
            in_specs=[pl.BlockSpec((n_slots, br, bc), lambda i: (0,) + index(i))],
            out_specs=pl.BlockSpec((br, bc), index), out_shape=out_shape,
            compiler_params=_params(("parallel",), 48),
        )(buf)
    return pl.pallas_call(
        body, name=name,
        grid_spec=pltpu.PrefetchScalarGridSpec(
            num_scalar_prefetch=1, grid=(steps,),
            in_specs=[pl.BlockSpec((n_slots, br, bc), lambda i, s: (0,) + index(i)),
                      pl.BlockSpec((None, br, bc), lambda i, s: (s[0],) + index(i))],
            out_specs=pl.BlockSpec((br, bc), lambda i, s: index(i))),
        out_shape=out_shape,
        compiler_params=_params(("arbitrary",), 48),
    )(slot, buf, own)


def _sum_small(bufs, name):
    n = len(bufs)

    def body(*refs):
        for b_ref, o_ref in zip(refs[:n], refs[n:]):
            acc = b_ref[0]
            for s in range(1, b_ref.shape[0]):
                acc = acc + b_ref[s]
            o_ref[...] = acc

    vmem = pl.BlockSpec(memory_space=pltpu.VMEM)
    return pl.pallas_call(
        body, name=name, in_specs=[vmem] * n, out_specs=[vmem] * n,
        out_shape=[jax.ShapeDtypeStruct(b.shape[1:], b.dtype) for b in bufs],
    )(*bufs)


def _adam_update(w, g, m, v):
    c1 = 1.0 - ADAM_B1 ** ADAM_STEP
    c2 = 1.0 - ADAM_B2 ** ADAM_STEP
    m_new = ADAM_B1 * m + (1.0 - ADAM_B1) * g
    v_new = ADAM_B2 * v + (1.0 - ADAM_B2) * (g * g)
    m_hat = m_new / c1
    v_hat = v_new / c2
    return -ADAM_LR * (m_hat / (jnp.sqrt(v_hat) + ADAM_EPS) + ADAM_WD * w), m_new, v_new


def _adamw_small(params, slot, name):
    n = len(params)

    def spec_of(shape):
        lead = (None,) * (len(shape) - 2)
        return pl.BlockSpec(lead + tuple(shape[-2:]), lambda i, s, k=len(shape): (0,) * k)

    in_specs, operands, out_specs, out_shape = [], [], [], []
    for w, g, m, v in params:
        shard = g.shape[-1] != w.shape[-1]
        g_spec = pl.BlockSpec(tuple(w.shape[-2:]), (lambda i, s: (0, s[0])) if shard else (lambda i, s: (0, 0)))
        in_specs += [spec_of(w.shape), g_spec, spec_of(m.shape), spec_of(v.shape)]
        operands += [w, g, m, v]
        out_specs += [spec_of(w.shape)] * 4
        out_shape += [jax.ShapeDtypeStruct(w.shape, F32)] * 4

    def body(slot_ref, *refs):
        ins, outs = refs[:4 * n], refs[4 * n:]
        for p in range(n):
            w_ref, g_ref, m_ref, v_ref = ins[4 * p:4 * p + 4]
            g = g_ref[...]
            delta, m_new, v_new = _adam_update(w_ref[...], g, m_ref[...], v_ref[...])
            for o_ref, val in zip(outs[4 * p:4 * p + 4], (g, delta, m_new, v_new)):
                o_ref[...] = val

    out = pl.pallas_call(
        body, name=name,
        grid_spec=pltpu.PrefetchScalarGridSpec(num_scalar_prefetch=1, grid=(1,), in_specs=in_specs, out_specs=out_specs),
        out_shape=out_shape,
    )(slot, *operands)
    return [tuple(out[4 * p:4 * p + 4]) for p in range(n)]


def _adamw(w, grads, m, v, name, grad_row=0):
    rows, cols = w.shape
    (rt, _), _, _ = _tile_2d(rows, cols, 8, 160 * 1024)
    assert grad_row % rt == 0
    n_g = len(grads)

    def body(*refs):
        w_ref = refs[0]
        g_refs = refs[1:1 + n_g]
        m_ref, v_ref, g_out, d_out, m_out, v_out = refs[1 + n_g:]
        g = g_refs[0][...]
        for r in g_refs[1:]:
            g = g + r[...]
        g_out[...] = g
        d_out[...], m_out[...], v_out[...] = _adam_update(w_ref[...], g, m_ref[...], v_ref[...])

    spec = pl.BlockSpec((rt, cols), lambda i: (i, 0))
    grad_spec = pl.BlockSpec((rt, cols), lambda i: (i + grad_row // rt, 0))
    shape = jax.ShapeDtypeStruct((rows, cols), F32)
    return pl.pallas_call(
        body, name=name, grid=(rows // rt,),
        in_specs=[spec] + [grad_spec] * n_g + [spec] * 2, out_specs=[spec] * 4, out_shape=[shape] * 4,
        compiler_params=_params(("parallel",), 48),
    )(w, *grads, m, v)


def _weight_pieces():
    pieces = []
    for j in range(N_CONV_TILES):
        for g in range(4):
            pieces.append((512 * j + 128 * g, D * g + 128 * j, 128))
    for hd in range(N_HEADS):
        base = W_CONV + 512 * hd
        pieces.append((base, OFF_Q + HEAD_K * hd, HEAD_K))
        pieces.append((base + HEAD_K, OFF_K + HEAD_K * hd, HEAD_K))
        pieces.append((base + 2 * HEAD_K, OFF_V + HEAD_V * hd, HEAD_V))
    pieces.append((W_CONV + W_GLA, OFF_R, D))
    pieces.append((W_CONV + W_GLA + D, OFF_MA, 2 * D))
    return pieces


N_WEIGHT_COPIES = len(_weight_pieces()) + 1


def _load_weights(w_hbm, w_s, wlr_s, sems):
    copies = [pltpu.make_async_copy(w_hbm.at[pl.ds(src, n)], w_s.at[pl.ds(dst, n)], sems.at[i])
              for i, (dst, src, n) in enumerate(_weight_pieces())]
    copies.append(pltpu.make_async_copy(w_hbm.at[pl.ds(OFF_LR, LANES)], wlr_s, sems.at[N_WEIGHT_COPIES - 1]))
    for cp in copies:
        cp.start()
    for cp in copies:
        cp.wait()


def _in_proj(h, g_pre, w_full_t):
    t_rows = h.shape[0]
    tm = _pick_tile(t_rows, 384, LANES)
    n_main = N_MAIN

    def body(h_ref, g_ref, w_hbm, proj_ref, ut_ref, lr_ref, w_s, wlr_s, w_sems):
        @pl.when(pl.program_id(0) == 0)
        def _():
            _load_weights(w_hbm, w_s, wlr_s, w_sems)

        hh = h_ref[...]
        rstd = lax.rsqrt(jnp.mean(hh * hh, axis=-1, keepdims=True) + EPS)
        uf = hh * rstd * g_ref[...]
        u = uf.astype(BF16)
        ut_ref[...] = jnp.transpose(uf).astype(BF16)
        lr_ref[...] = _dot_nt(u, wlr_s[...])
        for j in range(n_main // D):
            cols = slice(j * D, (j + 1) * D)
            proj_ref[:, cols] = _dot_nt(u, w_s[cols, :]).astype(BF16)

    return pl.pallas_call(
        body, name="in_proj", grid=(t_rows // tm,),
        in_specs=[pl.BlockSpec((tm, D), lambda i: (i, 0)),
                  pl.BlockSpec((1, D), lambda i: (0, 0)),
                  pl.BlockSpec(memory_space=pl.ANY)],
        out_specs=[pl.BlockSpec((tm, n_main), lambda i: (i, 0)),
                   pl.BlockSpec((D, tm), lambda i: (0, i)),
                   pl.BlockSpec((tm, LANES), lambda i: (i, 0))],
        out_shape=[jax.ShapeDtypeStruct((t_rows, n_main), BF16),
                   jax.ShapeDtypeStruct((D, t_rows), BF16),
                   jax.ShapeDtypeStruct((t_rows, LANES), F32)],
        scratch_shapes=[pltpu.VMEM((n_main, D), BF16), pltpu.VMEM((LANES, D), BF16),
                        pltpu.SemaphoreType.DMA((N_WEIGHT_COPIES,))],
        compiler_params=_params(("arbitrary",), 56),
    )(h, g_pre, w_full_t)


def _conv_parts(p_ref, w_ref):
    cb = p_ref[:, 0:128].astype(F32)
    cc = p_ref[:, 128:256].astype(F32)
    cx = p_ref[:, 256:384].astype(F32)
    cz = p_ref[:, 384:512].astype(F32)
    rows = cb.shape[0]
    w = w_ref[...]
    p = cc * cx
    conv = pltpu.roll(p, 1, 0) * w[0:1] + p * w[1:2] + pltpu.roll(p, rows - 1, 0) * w[2:3]
    sz = _sigmoid(cz)
    return cb, cc, cx, cz, p, conv, sz, w


def _conv_fwd(proj, conv_w, n_seq, lf):
    def body(p_ref, w_ref, y_ref):
        cb, _, _, cz, _, conv, sz, _ = _conv_parts(p_ref, w_ref)
        y_ref[...] = (cb * conv * (cz * sz)).astype(BF16)

    return pl.pallas_call(
        body, name="conv_fwd", grid=(n_seq, N_CONV_TILES),
        in_specs=[pl.BlockSpec((lf, 512), lambda b, j: (b, j)),
                  pl.BlockSpec((3, 128), lambda b, j: (0, j))],
        out_specs=pl.BlockSpec((lf, 128), lambda b, j: (b, j)),
        out_shape=jax.ShapeDtypeStruct((n_seq * lf, D), BF16),
        compiler_params=_params(("parallel", "parallel"), 48),
    )(proj, conv_w)


def _conv_bwd(proj, conv_w, dyc, n_seq, lf):
    def body(p_ref, w_ref, dy_ref, dp_ref, dw_ref):
        cb, cc, cx, cz, p, conv, sz, w = _conv_parts(p_ref, w_ref)
        rows = cb.shape[0]
        dy = dy_ref[...].astype(F32)
        silu = cz * sz
        dcb = dy * conv * silu
        dconv = dy * cb * silu
        dcz = dy * cb * conv * (sz * (1.0 + cz * (1.0 - sz)))
        d_next = pltpu.roll(dconv, rows - 1, 0)
        d_prev = pltpu.roll(dconv, 1, 0)
        dp = d_next * w[0:1] + dconv * w[1:2] + d_prev * w[2:3]
        dp_ref[:, 0:128] = dcb.astype(BF16)
        dp_ref[:, 128:256] = (dp * cx).astype(BF16)
        dp_ref[:, 256:384] = (dp * cc).astype(BF16)
        dp_ref[:, 384:512] = dcz.astype(BF16)
        dw_ref[0:1, :] = jnp.sum(dconv * pltpu.roll(p, 1, 0), axis=0, keepdims=True)
        dw_ref[1:2, :] = jnp.sum(dconv * p, axis=0, keepdims=True)
        dw_ref[2:3, :] = jnp.sum(dconv * pltpu.roll(p, rows - 1, 0), axis=0, keepdims=True)

    return pl.pallas_call(
        body, name="conv_bwd", grid=(n_seq, N_CONV_TILES),
        in_specs=[pl.BlockSpec((lf, 512), lambda b, j: (b, j)),
                  pl.BlockSpec((3, 128), lambda b, j: (0, j)),
                  pl.BlockSpec((lf, 128), lambda b, j: (b, j))],
        out_specs=[pl.BlockSpec((lf, 512), lambda b, j: (b, j)),
                   pl.BlockSpec((None, 3, 128), lambda b, j: (b, 0, j))],
        out_shape=[jax.ShapeDtypeStruct((n_seq * lf, W_CONV), BF16),
                   jax.ShapeDtypeStruct((n_seq, 3, D), F32)],
        compiler_params=_params(("parallel", "parallel"), 48),
    )(proj, conv_w, dyc)


GROUP = 3
GROUP_ROWS = GROUP * CHUNK


def _row_group(shape):
    row = lax.broadcasted_iota(jnp.int32, shape, 0)
    grp = jnp.zeros(shape, jnp.int32)
    for r in range(1, GROUP):
        grp = grp + (row >= r * CHUNK).astype(jnp.int32)
    return grp


def _lane_group(shape, width):
    lane = lax.broadcasted_iota(jnp.int32, shape, 1)
    grp = jnp.zeros(shape, jnp.int32)
    for r in range(1, GROUP):
        grp = grp + (lane >= r * width).astype(jnp.int32)
    return grp


def _score_mask(direction):
    shape = (GROUP_ROWS, GROUP_ROWS)
    row = lax.broadcasted_iota(jnp.int32, shape, 0)
    col = lax.broadcasted_iota(jnp.int32, shape, 1)
    same = _row_group(shape) == _lane_group(shape, CHUNK)
    return same & ((col <= row) if direction == 0 else (col > row))


def _diag_blocks(v):
    w = v.shape[1]
    wide = jnp.concatenate([v] * GROUP, axis=1)
    return jnp.where(_row_group(wide.shape) == _lane_group(wide.shape, w), wide, jnp.zeros_like(wide))


def _per_chunk_dot(lhs, state, transposed):
    outs = []
    for r in range(GROUP):
        rows = lhs[r * CHUNK:(r + 1) * CHUNK, :]
        blk = state[:, r * HEAD_K:(r + 1) * HEAD_K]
        outs.append(_dot_nt(rows, blk) if transposed else _dot(rows, blk))
    return jnp.concatenate(outs, axis=0)


def _chunk_cumsum(v, suffix):
    pos = lax.broadcasted_iota(jnp.int32, v.shape, 0) & (CHUNK - 1)
    shift = 1
    while shift < CHUNK:
        if suffix:
            moved = pltpu.roll(v, GROUP_ROWS - shift, 0)
            v = v + jnp.where(pos < CHUNK - shift, moved, 0.0)
        else:
            moved = pltpu.roll(v, shift, 0)
            v = v + jnp.where(pos >= shift, moved, 0.0)
        shift *= 2
    return v


def _per_chunk_rows(rows_of_chunk):
    w = rows_of_chunk[0].shape[1]
    return jnp.concatenate([jnp.broadcast_to(v, (CHUNK, w)) for v in rows_of_chunk], axis=0)


def _chunk_end_rows(direction, b):
    at = CHUNK - 1 if direction == 0 else 0
    return [b[r * CHUNK + at:r * CHUNK + at + 1, :] for r in range(GROUP)]


def _gla_gates(lr_bf, wg_ref, bg_ref, lf):
    z = _dot(lr_bf, wg_ref[...]) + bg_ref[...]
    valid = lax.broadcasted_iota(jnp.int32, (lf, HEAD_K), 0) >= PAD_FRONT
    return z, valid


def _group_unroll(n_groups):
    return n_groups if n_groups <= 11 else 1


def _group_rows(g):
    return pl.ds(pl.multiple_of(g * GROUP_ROWS, GROUP_ROWS), GROUP_ROWS)


def _chunk_decay(direction, g, r, b_s):
    base = g * GROUP_ROWS + r * CHUNK
    if direction == 0:
        grp = b_s[pl.ds(pl.multiple_of(base + CHUNK - 8, 8), 8), :]
        return jnp.exp(grp[7:8, :])
    grp = b_s[pl.ds(pl.multiple_of(base, 8), 8), :]
    return jnp.exp(grp[0:1, :])


def _state_scan(direction, n_groups, b_s, st_s, reverse):
    ascending = (direction == 0) != reverse

    def step(i, carry):
        g = i if ascending else n_groups - 1 - i
        for rr in range(GROUP):
            r = rr if ascending else GROUP - 1 - rr
            lanes = slice(r * HEAD_K, (r + 1) * HEAD_K)
            decay = _chunk_decay(direction, g, r, b_s)
            local = st_s[g, :, lanes]
            st_s[g, :, lanes] = carry
            carry = (local + carry * decay) if reverse else (carry * decay + local)
        return carry

    lax.fori_loop(0, n_groups, step, jnp.zeros((HEAD_V, HEAD_K), F32), unroll=_group_unroll(n_groups))


def _gla_states(direction, n_groups, qkv_ref, g_s, b_s, st_s):
    def local(g, carry):
        rows = _group_rows(g)
        b = _chunk_cumsum(g_s[rows, :], direction == 1)
        b_s[rows, :] = b
        b_end = _per_chunk_rows(_chunk_end_rows(direction, b))
        k = qkv_ref[rows, 128:256].astype(F32)
        v = qkv_ref[rows, 256:512]
        k_dec = (k * jnp.exp(b_end - b)).astype(BF16)
        st_s[g] = _dot_tn(v, _diag_blocks(k_dec))
        return carry

    lax.fori_loop(0, n_groups, local, 0, unroll=_group_unroll(n_groups))
    _state_scan(direction, n_groups, b_s, st_s, False)


def _gla_fwd(proj, lr, wgf, wgb, bgf, bgb, n_seq, lf):
    assert lf % GROUP_ROWS == 0
    n_groups = lf // GROUP_ROWS
    scale = HEAD_K ** -0.5

    def body(qkv_ref, lr_ref, wgf_ref, wgb_ref, bgf_ref, bgb_ref, o_ref, g_s, b_s, st_s):
        lr_bf = lr_ref[...].astype(BF16)
        for direction in (0, 1):
            wg_ref, bg_ref = ((wgf_ref, bgf_ref), (wgb_ref, bgb_ref))[direction]
            z, valid = _gla_gates(lr_bf, wg_ref, bg_ref, lf)
            g_s[...] = jnp.where(valid, _log_sigmoid(z) / GATE_NORM, 0.0)
            smask = _score_mask(direction)
            _gla_states(direction, n_groups, qkv_ref, g_s, b_s, st_s)

            def out(g, carry):
                rows = _group_rows(g)
                b = b_s[rows, :]
                q = qkv_ref[rows, 0:128].astype(F32) * scale
                k = qkv_ref[rows, 128:256].astype(F32)
                v = qkv_ref[rows, 256:512]
                q_in = (q * jnp.exp(b)).astype(BF16)
                k_in = (k * jnp.exp(-b)).astype(BF16)
                s = jnp.where(smask, _dot_nt(q_in, k_in), 0.0).astype(BF16)
                o = _dot(s, v) + _per_chunk_dot(q_in, st_s[g].astype(BF16), True)
                if direction == 0:
                    o_ref[rows, :] = o
                else:
                    o_ref[rows, :] = o_ref[rows, :] + o
                return carry

            lax.fori_loop(0, n_groups, out, 0, unroll=_group_unroll(n_groups))

    return pl.pallas_call(
        body, name="gla_fwd", grid=(n_seq, N_HEADS),
        in_specs=[pl.BlockSpec((lf, 512), lambda b, h: (b, N_CONV_TILES + h)),
                  pl.BlockSpec((lf, LANES), lambda b, h: (b, 0)),
                  pl.BlockSpec((None, LANES, HEAD_K), lambda b, h: (h, 0, 0)),
                  pl.BlockSpec((None, LANES, HEAD_K), lambda b, h: (h, 0, 0)),
                  pl.BlockSpec((None, 1, HEAD_K), lambda b, h: (h, 0, 0)),
                  pl.BlockSpec((None, 1, HEAD_K), lambda b, h: (h, 0, 0))],
        out_specs=pl.BlockSpec((lf, HEAD_V), lambda b, h: (b, h)),
        out_shape=jax.ShapeDtypeStruct((n_seq * lf, D), F32),
        scratch_shapes=[pltpu.VMEM((lf, HEAD_K), F32), pltpu.VMEM((lf, HEAD_K), F32),
                        pltpu.VMEM((n_groups, HEAD_V, GROUP * HEAD_K), F32)],
        compiler_params=_params(("parallel", "parallel"), 48),
    )(proj, lr, wgf, wgb, bgf, bgb)


def _gla_bwd(proj, lr, d_o, wgf, wgb, bgf, bgb, n_seq, lf, token):
    assert lf % GROUP_ROWS == 0
    n_groups = lf // GROUP_ROWS
    scale = HEAD_K ** -0.5

    def body(qkv_ref, lr_ref, do_ref, wgf_ref, wgb_ref, bgf_ref, bgb_ref, token_ref,
             dqkv_ref, dlr_ref, dwgf_ref, dwgb_ref, dbg_ref,
             g_s, b_s, fac_s, dg_s, st_s, dst_s, acc_s):
        lr_bf = lr_ref[...].astype(BF16)
        dlr = jnp.zeros((lf, LANES), F32)
        for direction in (0, 1):
            wg_ref, bg_ref = ((wgf_ref, bgf_ref), (wgb_ref, bgb_ref))[direction]
            z, valid = _gla_gates(lr_bf, wg_ref, bg_ref, lf)
            g_s[...] = jnp.where(valid, _log_sigmoid(z) / GATE_NORM, 0.0)
            fac_s[...] = jnp.where(valid, _sigmoid(-z) / GATE_NORM, 0.0)
            smask = _score_mask(direction)
            end_row = CHUNK - 1 if direction == 0 else 0
            _gla_states(direction, n_groups, qkv_ref, g_s, b_s, st_s)

            def state_grad_local(g, carry):
                rows = _group_rows(g)
                q = qkv_ref[rows, 0:128].astype(F32) * scale
                q_in = (q * jnp.exp(b_s[rows, :])).astype(BF16)
                dst_s[g] = _dot_tn(do_ref[rows, :], _diag_blocks(q_in))
                return carry

            lax.fori_loop(0, n_groups, state_grad_local, 0, unroll=_group_unroll(n_groups))
            _state_scan(direction, n_groups, b_s, dst_s, True)

            def group_grads(g, carry):
                rows = _group_rows(g)
                b = b_s[rows, :]
                ends = _chunk_end_rows(direction, b)
                b_end = _per_chunk_rows(ends)
                q = qkv_ref[rows, 0:128].astype(F32) * scale
                k = qkv_ref[rows, 128:256].astype(F32)
                v = qkv_ref[rows, 256:512]
                d_out = do_ref[rows, :]
                e_pos = jnp.exp(b)
                e_neg = jnp.exp(-b)
                e_end = jnp.exp(b_end - b)
                q_in = q * e_pos
                k_in = k * e_neg
                k_dec = k * e_end
                q_in_bf = q_in.astype(BF16)
                k_in_bf = k_in.astype(BF16)
                state = st_s[g]
                d_state = dst_s[g]
                state_bf = state.astype(BF16)
                d_state_bf = d_state.astype(BF16)
                s = jnp.where(smask, _dot_nt(q_in_bf, k_in_bf), 0.0).astype(BF16)
                ds = jnp.where(smask, _dot_nt(d_out, v), 0.0).astype(BF16)
                dv = _dot_tn(s, d_out) + _per_chunk_dot(k_dec.astype(BF16), d_state_bf, True)
                dq_in = _dot(ds, k_in_bf) + _per_chunk_dot(d_out, state_bf, False)
                dk_in = _dot_tn(ds, q_in_bf)
                dk_dec = _per_chunk_dot(v, d_state_bf, False)
                dq = dq_in * e_pos * scale
                dk = dk_in * e_neg + dk_dec * e_end
                if direction == 0:
                    acc_s[rows, 0:128] = dq
                    acc_s[rows, 128:256] = dk
                    acc_s[rows, 256:512] = dv
                else:
                    dqkv_ref[rows, 0:128] = (acc_s[rows, 0:128] + dq).astype(BF16)
                    dqkv_ref[rows, 128:256] = (acc_s[rows, 128:256] + dk).astype(BF16)
                    dqkv_ref[rows, 256:512] = (acc_s[rows, 256:512] + dv).astype(BF16)
                dkk = dk_dec * k_dec
                db = dq_in * q_in - dk_in * k_in - dkk
                d_decay = jnp.sum(d_state * state, axis=0, keepdims=True)
                db_end = [jnp.sum(dkk[r * CHUNK:(r + 1) * CHUNK, :], axis=0, keepdims=True)
                          + d_decay[:, r * HEAD_K:(r + 1) * HEAD_K] * jnp.exp(ends[r]) for r in range(GROUP)]
                row = lax.broadcasted_iota(jnp.int32, (GROUP_ROWS, HEAD_K), 0)
                at_end = row == end_row
                for r in range(1, GROUP):
                    at_end = at_end | (row == r * CHUNK + end_row)
                db = db + jnp.where(at_end, _per_chunk_rows(db_end), 0.0)
                dg_s[rows, :] = _chunk_cumsum(db, direction == 0)
                return carry

            lax.fori_loop(0, n_groups, group_grads, 0, unroll=_group_unroll(n_groups))

            dz = dg_s[...] * fac_s[...]
            dz_bf = dz.astype(BF16)
            dbg_ref[direction:direction + 1, :] = jnp.sum(dz, axis=0, keepdims=True)
            (dwgf_ref, dwgb_ref)[direction][...] = _dot_tn(lr_bf, dz_bf)
            dlr = dlr + _dot_nt(dz_bf, wg_ref[...])

        @pl.when(pl.program_id(1) == 0)
        def _():
            dlr_ref[...] = dlr

        @pl.when(pl.program_id(1) != 0)
        def _():
            dlr_ref[...] = dlr_ref[...] + dlr

    gate_w = pl.BlockSpec((None, LANES, HEAD_K), lambda b, h: (h, 0, 0))
    gate_b = pl.BlockSpec((None, 1, HEAD_K), lambda b, h: (h, 0, 0))
    return pl.pallas_call(
        body, name="gla_bwd", grid=(n_seq, N_HEADS),
        in_specs=[pl.BlockSpec((lf, 512), lambda b, h: (b, N_CONV_TILES + h)),
                  pl.BlockSpec((lf, LANES), lambda b, h: (b, 0)),
                  pl.BlockSpec((lf, HEAD_V), lambda b, h: (b, h)),
                  gate_w, gate_w, gate_b, gate_b,
                  pl.BlockSpec((8, LANES), lambda b, h: (0, 0))],
        out_specs=[pl.BlockSpec((lf, 512), lambda b, h: (b, h)),
                   pl.BlockSpec((lf, LANES), lambda b, h: (b, 0)),
                   pl.BlockSpec((None, None, LANES, HEAD_K), lambda b, h: (b, h, 0, 0)),
                   pl.BlockSpec((None, None, LANES, HEAD_K), lambda b, h: (b, h, 0, 0)),
                   pl.BlockSpec((None, None, 2, HEAD_K), lambda b, h: (b, h, 0, 0))],
        out_shape=[jax.ShapeDtypeStruct((n_seq * lf, W_GLA), BF16),
                   jax.ShapeDtypeStruct((n_seq * lf, LANES), F32),
                   jax.ShapeDtypeStruct((n_seq, N_HEADS, LANES, HEAD_K), F32),
                   jax.ShapeDtypeStruct((n_seq, N_HEADS, LANES, HEAD_K), F32),
                   jax.ShapeDtypeStruct((n_seq, N_HEADS, 2, HEAD_K), F32)],
        scratch_shapes=[pltpu.VMEM((lf, HEAD_K), F32), pltpu.VMEM((lf, HEAD_K), F32),
                        pltpu.VMEM((lf, HEAD_K), F32), pltpu.VMEM((lf, HEAD_K), F32),
                        pltpu.VMEM((n_groups, HEAD_V, GROUP * HEAD_K), F32),
                        pltpu.VMEM((n_groups, HEAD_V, GROUP * HEAD_K), F32),
                        pltpu.VMEM((lf, 512), F32)],
        compiler_params=_params(("parallel", "arbitrary"), 56),
    )(proj, lr, d_o, wgf, wgb, bgf, bgb, token)


def _tail(h, tgt, yc, o, proj, w3, gamma, g_post, lf):
    t_rows = h.shape[0]
    tm = _pick_tile(t_rows, 256, CHUNK)
    n_chunks = lf // CHUNK
    per_tile = tm // CHUNK

    def body(h_ref, tgt_ref, yc_ref, o_ref, r_ref, ma_ref, mb_ref, w_hbm, gamma_ref, gpost_ref,
             dres_ref, yg_ref, merged_ref, dout_ref, dpc_ref, dpg_ref, dyc_ref, do_ref, dtail_ref,
             loss_ref, dgpost_ref, dgamma_ref, w_s, w_sem):
        i = pl.program_id(0)

        @pl.when(i == 0)
        def _():
            cp = pltpu.make_async_copy(w_hbm, w_s, w_sem)
            cp.start()
            cp.wait()
            loss_ref[...] = jnp.zeros_like(loss_ref)
            dgpost_ref[...] = jnp.zeros_like(dgpost_ref)
            dgamma_ref[...] = jnp.zeros_like(dgamma_ref)

        gamma = gamma_ref[...]
        o = o_ref[...]
        r = r_ref[...].astype(F32)
        sr = _sigmoid(r)
        silu_r = r * sr
        n_parts, rstd_parts = [], []
        for hd in range(N_HEADS):
            oh = o[:, hd * HEAD_V:(hd + 1) * HEAD_V]
            rstd = lax.rsqrt(jnp.mean(oh * oh, axis=-1, keepdims=True) + EPS)
            n_parts.append(oh * rstd)
            rstd_parts.append(rstd)
        n = jnp.concatenate(n_parts, axis=-1)
        gamma_t = jnp.concatenate([gamma] * N_HEADS, axis=-1)
        yg = n * gamma_t * silu_r
        yg_bf = yg.astype(BF16)
        yg_ref[...] = yg_bf
        yc = yc_ref[...]
        pc = _dot(yc, w_s[0])
        pg = _dot(yg_bf, w_s[1])
        sa = _sigmoid(ma_ref[...].astype(F32))
        sb = _sigmoid(mb_ref[...].astype(F32))
        merged = (sa * pc + sb * pg).astype(BF16)
        merged_ref[...] = merged
        out = _dot(merged, w_s[2])
        rstd2 = lax.rsqrt(jnp.mean(out * out, axis=-1, keepdims=True) + EPS)
        nn = out * rstd2
        gpost = gpost_ref[...]
        y = h_ref[...] + nn * gpost

        rowi = lax.broadcasted_iota(jnp.int32, (tm, 1), 0)
        keep = jnp.zeros((tm, 1), F32)
        for kk in range(per_tile):
            is_tok = ((i * per_tile + kk) % n_chunks) != 0
            f = jnp.where(is_tok, 1.0, 0.0)
            keep = jnp.where((rowi >= kk * CHUNK) & (rowi < (kk + 1) * CHUNK), f, keep)
        diff = (y - tgt_ref[...]) * keep
        loss_ref[...] += jnp.sum(diff * diff) * (0.5 / D)
        dy = diff * (1.0 / D)
        dres_ref[...] = dy
        dgpost_ref[...] += jnp.sum(dy * nn, axis=0, keepdims=True)
        dn = dy * gpost
        dout_f = rstd2 * (dn - nn * jnp.mean(dn * nn, axis=-1, keepdims=True))
        dout = dout_f.astype(BF16)
        dout_ref[...] = jnp.transpose(dout_f).astype(BF16)
        dmerged = _dot_nt(dout, w_s[2])
        dpc_f = dmerged * sa
        dpg_f = dmerged * sb
        dpc = dpc_f.astype(BF16)
        dpg = dpg_f.astype(BF16)
        dpc_ref[...] = jnp.transpose(dpc_f).astype(BF16)
        dpg_ref[...] = jnp.transpose(dpg_f).astype(BF16)
        dtail_ref[:, D:2 * D] = (dmerged * pc * (sa * (1.0 - sa))).astype(BF16)
        dtail_ref[:, 2 * D:3 * D] = (dmerged * pg * (sb * (1.0 - sb))).astype(BF16)
        dyc_ref[...] = _dot_nt(dpc, w_s[0]).astype(BF16)
        dyg = _dot_nt(dpg, w_s[1])
        dtail_ref[:, 0:D] = (dyg * n * gamma_t * (sr * (1.0 + r * (1.0 - sr)))).astype(BF16)
        dgam_full = jnp.sum(dyg * n * silu_r, axis=0, keepdims=True)
        dgam = dgam_full[:, 0:HEAD_V]
        for hd in range(1, N_HEADS):
            dgam = dgam + dgam_full[:, hd * HEAD_V:(hd + 1) * HEAD_V]
        dgamma_ref[...] += dgam
        dng = dyg * gamma_t * silu_r
        do_parts = []
        for hd in range(N_HEADS):
            sl = slice(hd * HEAD_V, (hd + 1) * HEAD_V)
            dnh = dng[:, sl]
            nh = n_parts[hd]
            do_parts.append(rstd_parts[hd] * (dnh - nh * jnp.mean(dnh * nh, axis=-1, keepdims=True)))
        do_ref[...] = jnp.concatenate(do_parts, axis=-1).astype(BF16)

    row = lambda c: pl.BlockSpec((tm, D), lambda i: (i, c))
    col = pl.BlockSpec((D, tm), lambda i: (0, i))
    const = lambda shape: pl.BlockSpec(shape, lambda i: (0, 0))
    act = jax.ShapeDtypeStruct((t_rows, D), BF16)
    act_t = jax.ShapeDtypeStruct((D, t_rows), BF16)
    return pl.pallas_call(
        body, name="tail", grid=(t_rows // tm,),
        in_specs=[row(0), row(0), row(0), row(0), row(6), row(7), row(8),
                  pl.BlockSpec(memory_space=pl.ANY), const((1, HEAD_V)), const((1, D))],
        out_specs=[row(0)] * 3 + [col] * 3 + [row(0)] * 2
                  + [pl.BlockSpec((tm, W_TAIL), lambda i: (i, 0)),
                     const((8, LANES)), const((1, D)), const((1, HEAD_V))],
        out_shape=[jax.ShapeDtypeStruct((t_rows, D), F32)] + [act] * 2 + [act_t] * 3 + [act] * 2
                  + [jax.ShapeDtypeStruct((t_rows, W_TAIL), BF16),
                     jax.ShapeDtypeStruct((8, LANES), F32),
                     jax.ShapeDtypeStruct((1, D), F32),
                     jax.ShapeDtypeStruct((1, HEAD_V), F32)],
        scratch_shapes=[pltpu.VMEM((3, D, D), BF16), pltpu.SemaphoreType.DMA],
        compiler_params=_params(("arbitrary",), 56),
    )(h, tgt, yc, o, proj, proj, proj, w3, gamma, g_post)


def _wgrad_t(a_t, b, name, out_dtype=BF16):
    m, t_rows = a_t.shape
    n = b.shape[1]
    tn = D if n % D == 0 else n
    tk = _pick_tile(t_rows, 768, LANES)
    n_k = t_rows // tk

    def body(a_ref, b_ref, o_ref, acc):
        k = pl.program_id(1)

        @pl.when(k == 0)
        def _():
            acc[...] = jnp.zeros_like(acc)

        acc[...] += _dot(a_ref[...], b_ref[...].astype(BF16))

        @pl.when(k == n_k - 1)
        def _():
            o_ref[...] = jnp.transpose(acc[...]).astype(out_dtype)

    return pl.pallas_call(
        body, name=name, grid=(n // tn, n_k),
        in_specs=[pl.BlockSpec((m, tk), lambda j, k: (0, k)),
                  pl.BlockSpec((tk, tn), lambda j, k: (k, j))],
        out_specs=pl.BlockSpec((tn, m), lambda j, k: (j, 0)),
        out_shape=jax.ShapeDtypeStruct((n, m), out_dtype),
        scratch_shapes=[pltpu.VMEM((m, tn), F32)],
        compiler_params=_params(("parallel", "arbitrary"), 48),
    )(a_t, b)


def _dgrad_in(dpc, dpg, dpt, dlr, w_full_t, h, g_pre, dres, token):
    t_rows = h.shape[0]
    tm = _pick_tile(t_rows, 384, 16)
    n_main = N_MAIN

    def body(dpc_ref, dpg_ref, dpt_ref, dlr_ref, w_hbm, h_ref, g_ref, dres_ref, token_ref,
             dh_ref, dg_ref, w_s, wlr_s, w_sems):
        @pl.when(pl.program_id(0) == 0)
        def _():
            _load_weights(w_hbm, w_s, wlr_s, w_sems)
            dg_ref[...] = jnp.zeros_like(dg_ref)

        du = _dot(dlr_ref[...].astype(BF16), wlr_s[...])
        du += _dot(dpc_ref[...], w_s[0:W_CONV, :])
        du += _dot(dpg_ref[...], w_s[W_CONV:W_CONV + W_GLA, :])
        du += _dot(dpt_ref[...], w_s[W_CONV + W_GLA:n_main, :])
        hh = h_ref[...]
        rstd = lax.rsqrt(jnp.mean(hh * hh, axis=-1, keepdims=True) + EPS)
        xhat = hh * rstd
        dg_ref[...] += jnp.sum(du * xhat, axis=0, keepdims=True)
        dx = du * g_ref[...]
        dh_ref[...] = rstd * (dx - xhat * jnp.mean(dx * xhat, axis=-1, keepdims=True)) + dres_ref[...]

    row = lambda width: pl.BlockSpec((tm, width), lambda i: (i, 0))
    return pl.pallas_call(
        body, name="dgrad_in", grid=(t_rows // tm,),
        in_specs=[row(W_CONV), row(W_GLA), row(W_TAIL), row(LANES),
                  pl.BlockSpec(memory_space=pl.ANY),
                  row(D), pl.BlockSpec((1, D), lambda i: (0, 0)), row(D),
                  pl.BlockSpec((8, LANES), lambda i: (0, 0))],
        out_specs=[row(D), pl.BlockSpec((1, D), lambda i: (0, 0))],
        out_shape=[jax.ShapeDtypeStruct((t_rows, D), F32), jax.ShapeDtypeStruct((1, D), F32)],
        scratch_shapes=[pltpu.VMEM((n_main, D), BF16), pltpu.VMEM((LANES, D), BF16),
                        pltpu.SemaphoreType.DMA((N_WEIGHT_COPIES,))],
        compiler_params=_params(("arbitrary",), 56),
    )(dpc, dpg, dpt, dlr, w_full_t, h, g_pre, dres, token)


def _reference_rows(g_conv, g_gla, g_tail, g_lr):
    conv = g_conv.reshape(N_CONV_TILES, 4, 128, D).transpose(1, 0, 2, 3).reshape(W_CONV, D)
    gla = g_gla.reshape(N_HEADS, 512, D)
    q = gla[:, 0:128].reshape(N_HEADS * HEAD_K, D)
    k = gla[:, 128:256].reshape(N_HEADS * HEAD_K, D)
    v = gla[:, 256:512].reshape(N_HEADS * HEAD_V, D)
    return jnp.concatenate([conv, q, k, v, g_tail[0:D], g_lr[0:2 * RANK], g_tail[D:3 * D]], axis=0)


def kernel(x, meta_tokens, norm_pre, w_in, conv_w, w_gate_fwd, b_gate_fwd, w_gate_bwd, b_gate_bwd, gla_norm, w_out_conv, w_out_gla, w_merge_out, norm_post, loss_target, m_meta_tokens, m_norm_pre, m_w_in, m_conv_w, m_w_gate_fwd, m_b_gate_fwd, m_w_gate_bwd, m_b_gate_bwd, m_gla_norm, m_w_out_conv, m_w_out_gla, m_w_merge_out, m_norm_post, v_meta_tokens, v_norm_pre, v_w_in, v_conv_w, v_w_gate_fwd, v_b_gate_fwd, v_w_gate_bwd, v_b_gate_bwd, v_gla_norm, v_w_out_conv, v_w_out_gla, v_w_merge_out, v_norm_post):
    n_seq, seq, _ = x.shape
    lf = CHUNK + seq
    t_rows = n_seq * lf
    shard = 2 * lax.axis_index("x") + lax.axis_index("y")
    shard_arr = jnp.reshape(shard, (1,)).astype(jnp.int32)

    w_in_slots = _cast_into_slot(jnp.transpose(w_in[0]), shard_arr, "cast_w_in")
    w_out_bf = _cast_bf16(jnp.concatenate([w_out_conv[0], w_out_gla[0], w_merge_out[0]], axis=0), "cast_w_out")
    w_in_all, meta_all, conv_all, wgf_all, wgb_all = _gather_via_sibling(
        "gather_w_in", [w_in_slots, meta_tokens, conv_w[0], w_gate_fwd[0], w_gate_bwd[0]],
        (True, False, False, False, False))
    w_out_state, _ = _plane_start("gather_w_out_start", [w_out_bf], "gather", wgb_all)

    w_full_t = w_in_all.reshape(N_IN, D)
    meta_full = jnp.transpose(meta_all, (1, 0, 2)).reshape(N_META, D)
    conv_full = jnp.transpose(conv_all, (1, 0, 2)).reshape(3, D)
    wgf = jnp.pad(wgf_all, ((0, 0), (0, LANES - RANK), (0, 0))).astype(BF16)
    wgb = jnp.pad(wgb_all, ((0, 0), (RANK, LANES - 2 * RANK), (0, 0))).astype(BF16)
    bgf = b_gate_fwd.reshape(N_HEADS, 1, HEAD_K)
    bgb = b_gate_bwd.reshape(N_HEADS, 1, HEAD_K)

    head = jnp.concatenate([jnp.zeros((PAD_FRONT, D), F32), meta_full], axis=0)
    h = jnp.concatenate([jnp.broadcast_to(head[None], (n_seq, CHUNK, D)), x], axis=1).reshape(t_rows, D)
    tgt = jnp.pad(loss_target, ((0, 0), (CHUNK, 0), (0, 0))).reshape(t_rows, D)

    proj, u_t, lr = _in_proj(h, norm_pre, w_full_t)
    yc = _conv_fwd(proj, conv_full, n_seq, lf)
    o = _gla_fwd(proj, lr, wgf, wgb, bgf, bgb, n_seq, lf)
    (w_out_landed,) = _plane_wait("gather_w_out_wait", w_out_state, "gather", o)
    slot_ids = lax.broadcasted_iota(jnp.int32, (4, 1, 1), 0)
    w_out_all = jnp.where(slot_ids == shard, w_out_bf[None], w_out_landed)
    w3 = jnp.transpose(w_out_all.reshape(4, 3, D // 4, D), (1, 0, 2, 3)).reshape(3, D, D)
    (dres, yg, merged, dout_t, dpc_t, dpg_t, dyc, d_o, dtail, loss_acc, d_gpost, d_gamma) = _tail(
        h, tgt, yc, o, proj, w3, gla_norm, norm_post, lf)
    g_w_oc = _wgrad_t(dpc_t, yc, "wgrad_out_conv")
    g_w_og = _wgrad_t(dpg_t, yg, "wgrad_out_gla")
    g_w_mo = _wgrad_t(dout_t, merged, "wgrad_merge_out")
    g_out_slots = jnp.concatenate([g.reshape(4, D // 4, D) for g in (g_w_oc, g_w_og, g_w_mo)], axis=1)
    out_state, out_token = _plane_start("scatter_out_grads_start", [g_out_slots], "scatter", g_w_mo)
    dgla, dlr, dwgf_p, dwgb_p, dbg_p = _gla_bwd(proj, lr, d_o, wgf, wgb, bgf, bgb, n_seq, lf, out_token)
    (got_out,) = _plane_wait("scatter_out_grads_wait", out_state, "scatter", dlr)
    dconv, dconvw_p = _conv_bwd(proj, conv_full, dyc, n_seq, lf)
    g_conv = _wgrad_t(u_t, dconv, "wgrad_in_conv")
    g_gla = _wgrad_t(u_t, dgla, "wgrad_in_gla")
    g_tail = _wgrad_t(u_t, dtail, "wgrad_in_tail")
    g_lr = _wgrad_t(u_t, dlr, "wgrad_in_lr")

    g_in_slots = _reference_rows(g_conv, g_gla, g_tail, g_lr).reshape(4, SHARD_IN, D)
    in_state, in_token = _plane_start("scatter_in_grads_start", [g_in_slots], "scatter", g_lr)
    dh, d_gpre = _dgrad_in(dconv, dgla, dtail, dlr, w_full_t, h, norm_pre, dres, in_token)
    (got_in,) = _plane_wait("scatter_in_grads_wait", in_state, "scatter", d_gpre)

    plane_in = _sum_slots(got_in, "sum_w_in_grads", own=g_in_slots, slot=shard_arr)
    plane_out = _sum_slots(got_out, "sum_w_out_grads", own=g_out_slots, slot=shard_arr)
    swap_state, swap_token = _plane_start("swap_plane_sums_start", [plane_in, plane_out], "swap", plane_out)

    dh3 = dh.reshape(n_seq, lf, D)
    grad_x = dh3[:, CHUNK:, :]

    d_meta = jnp.sum(dh3[:, PAD_FRONT:CHUNK, :], axis=0)
    d_convw = jnp.sum(dconvw_p, axis=0)
    d_wgf = jnp.transpose(jnp.sum(dwgf_p, axis=0)[:, 0:RANK, :], (1, 0, 2)).reshape(RANK, N_HEADS * HEAD_K)
    d_wgb = jnp.transpose(jnp.sum(dwgb_p, axis=0)[:, RANK:2 * RANK, :], (1, 0, 2)).reshape(RANK, N_HEADS * HEAD_K)
    d_bg = jnp.sum(dbg_p, axis=0)
    d_bgf = d_bg[:, 0, :].reshape(1, N_HEADS * HEAD_K)
    d_bgb = d_bg[:, 1, :].reshape(1, N_HEADS * HEAD_K)
    loss_part = loss_acc[0:1, :] + swap_token[0:1, :]
    partials = [d_meta, d_convw, d_wgf, d_wgb, d_gpre, d_bgf, d_bgb, d_gamma, d_gpost, loss_part]
    (g_meta, g_convw, g_wgf, g_wgb, g_npre, g_bgf, g_bgb, g_gnorm, g_npost, loss_row) = _sum_small(
        _exchange("gather_small_grads", partials, ALL_FLIPS, (4, 2, 1), "gather"), "sum_small_grads")
    loss = loss_row[0, 0]
    small_out = _adamw_small(
        [(meta_tokens, g_meta, m_meta_tokens, v_meta_tokens), (norm_pre, g_npre, m_norm_pre, v_norm_pre),
         (conv_w, g_convw, m_conv_w, v_conv_w), (w_gate_fwd, g_wgf, m_w_gate_fwd, v_w_gate_fwd),
         (b_gate_fwd, g_bgf, m_b_gate_fwd, v_b_gate_fwd), (w_gate_bwd, g_wgb, m_w_gate_bwd, v_w_gate_bwd),
         (b_gate_bwd, g_bgb, m_b_gate_bwd, v_b_gate_bwd), (gla_norm, g_gnorm, m_gla_norm, v_gla_norm),
         (norm_post, g_npost, m_norm_post, v_norm_post)], shard_arr, "adamw_small")

    other_in, other_out = _plane_wait("swap_plane_sums_wait", swap_state, "swap", small_out[0][0])
    big_in = _adamw(jnp.transpose(w_in[0]), [plane_in, other_in], jnp.transpose(m_w_in[0]), jnp.transpose(v_w_in[0]),
                    "adamw_w_in")
    out_params = ((w_out_conv, m_w_out_conv, v_w_out_conv), (w_out_gla, m_w_out_gla, v_w_out_gla),
                  (w_merge_out, m_w_merge_out, v_w_merge_out))
    big_out = [_adamw(w[0], [plane_out, other_out], m[0], v[0], f"adamw_w_out_{i}", grad_row=i * (D // 4))
               for i, (w, m, v) in enumerate(out_params)]

    results = []
    for kind in range(4):
        small_kind = [p[kind] for p in small_out]
        w_in_part = jnp.transpose(big_in[kind])[None]
        outs3 = [big_out[i][kind][None] for i in range(3)]
        results.extend(small_kind[0:2] + [w_in_part] + small_kind[2:8] + outs3 + small_kind[8:9])
    return (loss, grad_x, *results)
```

```python
import functools

import jax
import jax.numpy as jnp
from jax import lax
from jax.experimental import pallas as pl
from jax.experimental.pallas import tpu as pltpu

F32 = jnp.float32
BF16 = jnp.bfloat16
MESH = pl.DeviceIdType.MESH

D = 1024
N_META = 16
CHUNK = 64
PAD_FRONT = CHUNK - N_META
N_HEADS = 4
HEAD_K = 128
HEAD_V = 256
RANK = 16
EPS = 1e-6
GATE_NORM = 16.0
N_IN = 9248
SHARD_IN = N_IN // 4
LANES = 128
N_CONV_TILES = 8
W_CONV = 4096
W_GLA = 2048
W_TAIL = 3072
N_MAIN = W_CONV + W_GLA + W_TAIL
OFF_Q, OFF_K, OFF_V, OFF_R = 4096, 4608, 5120, 6144
OFF_LR, OFF_MA, OFF_MB = 7168, 7200, 8224
MIB = 1024 * 1024

ADAM_LR = 0.001
ADAM_B1 = 0.9
ADAM_B2 = 0.999
ADAM_EPS = 1e-08
ADAM_WD = 0.01
ADAM_STEP = 10


def _params(sem=None, vmem_mib=None):
    return pltpu.CompilerParams(
        dimension_semantics=sem,
        vmem_limit_bytes=None if vmem_mib is None else vmem_mib * MIB)


def _pick_tile(n, target, mult):
    best = None
    for t in range(mult, min(n, target) + 1, mult):
        if n % t == 0:
            best = t
    return n if best is None else best


def _sigmoid(v):
    return 1.0 / (1.0 + jnp.exp(-v))


def _log_sigmoid(v):
    return jnp.minimum(v, 0.0) - jnp.log(1.0 + jnp.exp(-jnp.abs(v)))


def _dot(a, b):
    return jnp.dot(a, b, preferred_element_type=F32)


def _dot_nt(a, b):
    return lax.dot_general(a, b, (((1,), (1,)), ((), ())), preferred_element_type=F32)


def _dot_tn(a, b):
    return lax.dot_general(a, b, (((0,), (0,)), ((), ())), preferred_element_type=F32)


PLANE_FLIPS = ((1, 0, 0), (0, 1, 0), (1, 1, 0))
ALL_FLIPS = tuple((m >> 2 & 1, m >> 1 & 1, m & 1) for m in range(1, 8))
SIBLING_FLIPS = ((0, 0, 1),)


def _exchange(name, arrs, flips, slot_weights, mode):
    n = len(arrs)
    n_slots = 1
    for w in slot_weights:
        n_slots += w
    if mode == "gather":
        out_shape = [jax.ShapeDtypeStruct((n_slots,) + a.shape, a.dtype) for a in arrs]
    else:
        out_shape = [jax.ShapeDtypeStruct(a.shape, a.dtype) for a in arrs]

    def body(*refs):
        ins, outs = refs[:n], refs[n:2 * n]
        send_sems, recv_sems, local_sems = refs[2 * n:]
        pos = (lax.axis_index("x"), lax.axis_index("y"), lax.axis_index("c"))

        def slot_of(p):
            return p[0] * slot_weights[0] + p[1] * slot_weights[1] + p[2] * slot_weights[2]

        peers = [tuple(1 - pos[a] if f[a] else pos[a] for a in range(3)) for f in flips]
        me = slot_of(pos)
        local = []
        sends = []
        for i in range(n):
            if mode != "swap":
                src = ins[i] if mode == "gather" else ins[i].at[me]
                cp = pltpu.make_async_copy(src, outs[i].at[me], local_sems.at[i])
                cp.start()
                local.append(cp)
            for k, peer in enumerate(peers):
                if mode == "gather":
                    src, dst = ins[i], outs[i].at[me]
                elif mode == "scatter":
                    src, dst = ins[i].at[slot_of(peer)], outs[i].at[me]
                else:
                    src, dst = ins[i], outs[i]
                cp = pltpu.make_async_remote_copy(
                    src_ref=src, dst_ref=dst, send_sem=send_sems.at[i, k], recv_sem=recv_sems.at[i, k],
                    device_id=peer, device_id_type=MESH)
                cp.start()
                sends.append(cp)
        for i in range(n):
            for k, peer in enumerate(peers):
                if mode == "gather":
                    src, dst = ins[i], outs[i].at[slot_of(peer)]
                elif mode == "scatter":
                    src, dst = ins[i].at[me], outs[i].at[slot_of(peer)]
                else:
                    src, dst = ins[i], outs[i]
                arrival = pltpu.make_async_remote_copy(
                    src_ref=src, dst_ref=dst, send_sem=send_sems.at[i, k], recv_sem=recv_sems.at[i, k],
                    device_id=peer, device_id_type=MESH)
                arrival.wait_recv()
        for cp in sends:
            cp.wait_send()
        for cp in local:
            cp.wait()

    hbm = pl.BlockSpec(memory_space=pl.ANY)
    outs = pl.pallas_call(
        body, name=name, out_shape=out_shape,
        in_specs=[hbm] * n, out_specs=[hbm] * n,
        scratch_shapes=[pltpu.SemaphoreType.DMA((n, len(flips))),
                        pltpu.SemaphoreType.DMA((n, len(flips))),
                        pltpu.SemaphoreType.DMA((n,))],
        compiler_params=pltpu.CompilerParams(has_side_effects=True),
    )(*arrs)
    return list(outs)


def _gather_via_sibling(name, arrs, slotted):
    n = len(arrs)
    out_shape = [jax.ShapeDtypeStruct(a.shape if slotted[i] else (4,) + a.shape, a.dtype)
                 for i, a in enumerate(arrs)]

    def body(*refs):
        ins, outs = refs[:n], refs[n:2 * n]
        send_sems, recv_sems, local_sems = refs[2 * n:]
        x, y, c = lax.axis_index("x"), lax.axis_index("y"), lax.axis_index("c")
        me = 2 * x + y
        chips = [(1 - x, y), (x, 1 - y), (1 - x, 1 - y)]

        def half(ref, which):
            rows = ref.shape[0]
            cut = rows // 2 // 16 * 16
            return ref.at[pl.ds(0, cut)] if which == 0 else ref.at[pl.ds(cut, rows - cut)]

        def copy(src, dst, i, k, to):
            return pltpu.make_async_remote_copy(
                src_ref=src, dst_ref=dst, send_sem=send_sems.at[i, k], recv_sem=recv_sems.at[i, k],
                device_id=to, device_id_type=MESH)

        def run(mine):
            other = 1 - mine
            local, sends = [], []
            whole = [(not slotted[i]) and arrs[i].shape[0] < 32 for i in range(n)]
            for i in range(n):
                own = outs[i].at[me] if slotted[i] else ins[i]
                if not slotted[i]:
                    cp = pltpu.make_async_copy(ins[i], outs[i].at[me], local_sems.at[i])
                    cp.start()
                    local.append(cp)
                for k, (px, py) in enumerate(chips):
                    if whole[i]:
                        cp = copy(own, outs[i].at[me], i, k, (px, py, mine))
                    elif k < 2:
                        cp = copy(half(own, mine), half(outs[i].at[me], mine), i, k, (px, py, mine))
                    else:
                        continue
                    cp.start()
                    sends.append(cp)
            via = mine
            for k in (via, 1 - via, 2):
                px, py = chips[k]
                slot = 2 * px + py
                source = (px, py, mine) if k < 2 else chips[1 - via] + (mine,)
                for i in range(n):
                    if whole[i]:
                        copy(outs[i].at[slot], outs[i].at[slot], i, k, (px, py, mine)).wait_recv()
                        continue
                    landed = half(outs[i].at[slot], mine)
                    copy(landed, landed, i, k, source).wait_recv()
                    if k == via:
                        cp = copy(landed, landed, i, 2, chips[1 - via] + (mine,))
                        cp.start()
                        sends.append(cp)
                    cp = copy(landed, landed, i, 3 + k, (x, y, other))
                    cp.start()
                    sends.append(cp)
            for k, (px, py) in enumerate(chips):
                slot = 2 * px + py
                for i in range(n):
                    if whole[i]:
                        continue
                    passed = half(outs[i].at[slot], other)
                    copy(passed, passed, i, 3 + k, (x, y, other)).wait_recv()
            for cp in sends:
                cp.wait_send()
            for cp in local:
                cp.wait()

        for mine in (0, 1):
            pl.when(c == mine)(functools.partial(run, mine))

    hbm = pl.BlockSpec(memory_space=pl.ANY)
    outs = pl.pallas_call(
        body, name=name, out_shape=out_shape,
        in_specs=[hbm] * n, out_specs=[hbm] * n,
        scratch_shapes=[pltpu.SemaphoreType.DMA((n, 6)), pltpu.SemaphoreType.DMA((n, 6)),
                        pltpu.SemaphoreType.DMA((n,))],
        input_output_aliases={i: i for i in range(n) if slotted[i]},
        compiler_params=pltpu.CompilerParams(has_side_effects=True),
    )(*arrs)
    return list(outs)


HBM_SPEC = pl.BlockSpec(memory_space=pltpu.HBM)
SEM_SPEC = pl.BlockSpec(memory_space=pltpu.SEMAPHORE)
DATAFLOW = pltpu.SideEffectType.DATAFLOW_SIDE_EFFECTING


def _split_peers(mode):
    x, y, c = lax.axis_index("x"), lax.axis_index("y"), lax.axis_index("c")
    if mode == "swap":
        return 0, [((x, y, 1 - c), 0)]
    return 2 * x + y, [((1 - x, y, c), 2 * (1 - x) + y), ((x, 1 - y, c), 2 * x + 1 - y),
                       ((1 - x, 1 - y, c), 2 * (1 - x) + 1 - y)]


def _split_refs(mode, src, landing, me, peer_slot):
    if mode == "gather":
        return src, landing.at[me]
    if mode == "scatter":
        return src.at[peer_slot], landing.at[me]
    return src, landing


def _plane_start(name, arrs, mode, after):
    n = len(arrs)
    n_peers = 1 if mode == "swap" else 3
    lands = [lax.empty(((4,) + a.shape) if mode == "gather" else a.shape, a.dtype) for a in arrs]

    def body(*refs):
        srcs, landing = refs[:n], refs[n:2 * n]
        send_sems, recv_sems = refs[2 * n + 1], refs[2 * n + 2]
        token = refs[-1]
        me, peers = _split_peers(mode)
        for i in range(n):
            for k, (peer, peer_slot) in enumerate(peers):
                src, dst = _split_refs(mode, srcs[i], landing[i], me, peer_slot)
                pltpu.make_async_remote_copy(
                    src_ref=src, dst_ref=dst, send_sem=send_sems.at[n_peers * i + k],
                    recv_sem=recv_sems.at[n_peers * i + k], device_id=peer, device_id_type=MESH).start()
        token[...] = jnp.zeros_like(token)

    hbm_in = [pltpu.with_memory_space_constraint(a, pltpu.HBM) for a in list(arrs) + lands]
    out = pl.pallas_call(
        body, name=name,
        out_shape=[pltpu.SemaphoreType.DMA((n_peers * n,)), pltpu.SemaphoreType.DMA((n_peers * n,))]
                  + [pltpu.HBM(a.shape, a.dtype) for a in lands]
                  + [jax.ShapeDtypeStruct((8, LANES), F32)],
        in_specs=[HBM_SPEC] * (2 * n) + [pl.BlockSpec(memory_space=pl.ANY)],
        out_specs=[SEM_SPEC, SEM_SPEC] + [HBM_SPEC] * n + [pl.BlockSpec(memory_space=pltpu.VMEM)],
        input_output_aliases={n + i: 2 + i for i in range(n)},
        compiler_params=pltpu.CompilerParams(has_side_effects=DATAFLOW),
    )(*hbm_in, after)
    return out[:-1], out[-1]


def _plane_wait(name, state, mode, after):
    send_sems, recv_sems = state[0], state[1]
    bufs = list(state[2:])
    n = len(bufs)
    n_peers = 1 if mode == "swap" else 3

    def body(*refs):
        landing = refs[:n]
        send_sems, recv_sems = refs[n], refs[n + 1]
        _, peers = _split_peers(mode)
        for i in range(n):
            for k, (peer, peer_slot) in enumerate(peers):
                arrived = landing[i] if mode == "swap" else landing[i].at[peer_slot]
                cp = pltpu.make_async_remote_copy(
                    src_ref=arrived, dst_ref=arrived, send_sem=send_sems.at[n_peers * i + k],
                    recv_sem=recv_sems.at[n_peers * i + k], device_id=peer, device_id_type=MESH)
                cp.wait_send()
                cp.wait_recv()

    out = pl.pallas_call(
        body, name=name,
        out_shape=[pltpu.HBM(a.shape, a.dtype) for a in bufs],
        in_specs=[HBM_SPEC] * n + [SEM_SPEC, SEM_SPEC, pl.BlockSpec(memory_space=pl.ANY)],
        out_specs=[HBM_SPEC] * n,
        input_output_aliases={i: i for i in range(n)},
        compiler_params=pltpu.CompilerParams(has_side_effects=DATAFLOW),
    )(*bufs, send_sems, recv_sems, after)
    return list(out)


def _tile_2d(rows, cols, row_mult, max_elems=512 * 1024):
    if rows % row_mult == 0:
        rt = _pick_tile(rows, max(row_mult, max_elems // cols), row_mult)
        return (rt, cols), rows // rt, lambda i: (i, 0)
    ct = _pick_tile(cols, max(LANES, max_elems // rows), LANES)
    return (rows, ct), cols // ct, lambda i: (0, i)


def _cast_bf16(a, name):
    block, steps, index = _tile_2d(a.shape[0], a.shape[1], 16)

    def body(a_ref, o_ref):
        o_ref[...] = a_ref[...].astype(BF16)

    return pl.pallas_call(
        body, name=name, grid=(steps,),
        in_specs=[pl.BlockSpec(block, index)],
        out_specs=pl.BlockSpec(block, index),
        out_shape=jax.ShapeDtypeStruct(a.shape, BF16),
        compiler_params=_params(("parallel",)),
    )(a)


def _cast_into_slot(a, slot, name):
    block, steps, index = _tile_2d(a.shape[0], a.shape[1], 16)

    def body(slot_ref, a_ref, o_ref):
        o_ref[...] = a_ref[...].astype(BF16)

    return pl.pallas_call(
        body, name=name,
        grid_spec=pltpu.PrefetchScalarGridSpec(
            num_scalar_prefetch=1, grid=(steps,),
            in_specs=[pl.BlockSpec(block, lambda i, s: index(i))],
            out_specs=pl.BlockSpec((None,) + block, lambda i, s: (s[0],) + index(i))),
        out_shape=jax.ShapeDtypeStruct((4,) + a.shape, BF16),
        compiler_params=_params(("arbitrary",)),
    )(slot, a)


def _sum_slots(buf, name, own=None, slot=None):
    n_slots, rows, cols = buf.shape
    (br, bc), steps, index = _tile_2d(rows, cols, 16, 320 * 1024)

    def body(*refs):
        if own is None:
            b_ref, o_ref = refs
        else:
            slot_ref, b_ref, own_ref, o_ref = refs
        acc = None
        for s in range(n_slots):
            term = b_ref[s] if own is None else jnp.where(slot_ref[0] == s, own_ref[...], b_ref[s])
            acc = term.astype(F32) if acc is None else acc + term.astype(F32)
        o_ref[...] = acc

    out_shape = jax.ShapeDtypeStruct((rows, cols), F32)
    if own is None:
        return pl.pallas_call(
            body, name=name, grid=(steps,),
            in_specs=[pl.BlockSpec((n_slots, br, bc), lambda i: (0,) + index(i))],
            out_specs=pl.BlockSpec((br, bc), index), out_shape=out_shape,
            compiler_params=_params(("parallel",), 48),
        )(buf)
    return pl.pallas_call(
        body, name=name,
        grid_spec=pltpu.PrefetchScalarGridSpec(
            num_scalar_prefetch=1, grid=(steps,),
            in_specs=[pl.BlockSpec((n_slots, br, bc), lambda i, s: (0,) + index(i)),
                      pl.BlockSpec((None, br, bc), lambda i, s: (s[0],) + index(i))],
            out_specs=pl.BlockSpec((br, bc), lambda i, s: index(i))),
        out_shape=out_shape,
        compiler_params=_params(("arbitrary",), 48),
    )(slot, buf, own)


def _sum_small(bufs, name):
    n = len(bufs)

    def body(*refs):
        for b_ref, o_ref in zip(refs[:n], refs[n:]):
            acc = b_ref[0]
            for s in range(1, b_ref.shape[0]):
                acc = acc + b_ref[s]
            o_ref[...] = acc

    vmem = pl.BlockSpec(memory_space=pltpu.VMEM)
    return pl.pallas_call(
        body, name=name, in_specs=[vmem] * n, out_specs=[vmem] * n,
        out_shape=[jax.ShapeDtypeStruct(b.shape[1:], b.dtype) for b in bufs],
    )(*bufs)


def _adam_update(w, g, m, v):
    c1 = 1.0 - ADAM_B1 ** ADAM_STEP
    c2 = 1.0 - ADAM_B2 ** ADAM_STEP
    m_new = ADAM_B1 * m + (1.0 - ADAM_B1) * g
    v_new = ADAM_B2 * v + (1.0 - ADAM_B2) * (g * g)
    m_hat = m_new / c1
    v_hat = v_new / c2
    return -ADAM_LR * (m_hat / (jnp.sqrt(v_hat) + ADAM_EPS) + ADAM_WD * w), m_new, v_new


def _adamw_small(params, slot, name):
    n = len(params)

    def spec_of(shape):
        lead = (None,) * (len(shape) - 2)
        return pl.BlockSpec(lead + tuple(shape[-2:]), lambda i, s, k=len(shape): (0,) * k)

    in_specs, operands, out_specs, out_shape = [], [], [], []
    for w, g, m, v in params:
        shard = g.shape[-1] != w.shape[-1]
        g_spec = pl.BlockSpec(tuple(w.shape[-2:]), (lambda i, s: (0, s[0])) if shard else (lambda i, s: (0, 0)))
        in_specs += [spec_of(w.shape), g_spec, spec_of(m.shape), spec_of(v.shape)]
        operands += [w, g, m, v]
        out_specs += [spec_of(w.shape)] * 4
        out_shape += [jax.ShapeDtypeStruct(w.shape, F32)] * 4

    def body(slot_ref, *refs):
        ins, outs = refs[:4 * n], refs[4 * n:]
        for p in range(n):
            w_ref, g_ref, m_ref, v_ref = ins[4 * p:4 * p + 4]
            g = g_ref[...]
            delta, m_new, v_new = _adam_update(w_ref[...], g, m_ref[...], v_ref[...])
            for o_ref, val in zip(outs[4 * p:4 * p + 4], (g, delta, m_new, v_new)):
                o_ref[...] = val

    out = pl.pallas_call(
        body, name=name,
        grid_spec=pltpu.PrefetchScalarGridSpec(num_scalar_prefetch=1, grid=(1,), in_specs=in_specs, out_specs=out_specs),
        out_shape=out_shape,
    )(slot, *operands)
    return [tuple(out[4 * p:4 * p + 4]) for p in range(n)]


def _adamw(w, grads, m, v, name, grad_row=0):
    rows, cols = w.shape
    (rt, _), _, _ = _tile_2d(rows, cols, 8, 160 * 1024)
    assert grad_row % rt == 0
    n_g = len(grads)

    def body(*refs):
        w_ref = refs[0]
        g_refs = refs[1:1 + n_g]
        m_ref, v_ref, g_out, d_out, m_out, v_out = refs[1 + n_g:]
        g = g_refs[0][...]
        for r in g_refs[1:]:
            g = g + r[...]
        g_out[...] = g
        d_out[...], m_out[...], v_out[...] = _adam_update(w_ref[...], g, m_ref[...], v_ref[...])

    spec = pl.BlockSpec((rt, cols), lambda i: (i, 0))
    grad_spec = pl.BlockSpec((rt, cols), lambda i: (i + grad_row // rt, 0))
    shape = jax.ShapeDtypeStruct((rows, cols), F32)
    return pl.pallas_call(
        body, name=name, grid=(rows // rt,),
        in_specs=[spec] + [grad_spec] * n_g + [spec] * 2, out_specs=[spec] * 4, out_shape=[shape] * 4,
        compiler_params=_params(("parallel",), 48),
    )(w, *grads, m, v)


def _weight_pieces():
    pieces = []
    for j in range(N_CONV_TILES):
        for g in range(4):
            pieces.append((512 * j + 128 * g, D * g + 128 * j, 128))
    for hd in range(N_HEADS):
        base = W_CONV + 512 * hd
        pieces.append((base, OFF_Q + HEAD_K * hd, HEAD_K))
        pieces.append((base + HEAD_K, OFF_K + HEAD_K * hd, HEAD_K))
        pieces.append((base + 2 * HEAD_K, OFF_V + HEAD_V * hd, HEAD_V))
    pieces.append((W_CONV + W_GLA, OFF_R, D))
    pieces.append((W_CONV + W_GLA + D, OFF_MA, 2 * D))
    return pieces


N_WEIGHT_COPIES = len(_weight_pieces()) + 1


def _load_weights(w_hbm, w_s, wlr_s, sems):
    copies = [pltpu.make_async_copy(w_hbm.at[pl.ds(src, n)], w_s.at[pl.ds(dst, n)], sems.at[i])
              for i, (dst, src, n) in enumerate(_weight_pieces())]
    copies.append(pltpu.make_async_copy(w_hbm.at[pl.ds(OFF_LR, LANES)], wlr_s, sems.at[N_WEIGHT_COPIES - 1]))
    for cp in copies:
        cp.start()
    for cp in copies:
        cp.wait()


def _in_proj(h, g_pre, w_full_t):
    t_rows = h.shape[0]
    tm = _pick_tile(t_rows, 384, LANES)
    n_main = N_MAIN

    def body(h_ref, g_ref, w_hbm, proj_ref, ut_ref, lr_ref, w_s, wlr_s, w_sems):
        @pl.when(pl.program_id(0) == 0)
        def _():
            _load_weights(w_hbm, w_s, wlr_s, w_sems)

        hh = h_ref[...]
        rstd = lax.rsqrt(jnp.mean(hh * hh, axis=-1, keepdims=True) + EPS)
        uf = hh * rstd * g_ref[...]
        u = uf.astype(BF16)
        ut_ref[...] = jnp.transpose(uf).astype(BF16)
        lr_ref[...] = _dot_nt(u, wlr_s[...])
        for j in range(n_main // D):
            cols = slice(j * D, (j + 1) * D)
            proj_ref[:, cols] = _dot_nt(u, w_s[cols, :]).astype(BF16)

    return pl.pallas_call(
        body, name="in_proj", grid=(t_rows // tm,),
        in_specs=[pl.BlockSpec((tm, D), lambda i: (i, 0)),
                  pl.BlockSpec((1, D), lambda i: (0, 0)),
                  pl.BlockSpec(memory_space=pl.ANY)],
        out_specs=[pl.BlockSpec((tm, n_main), lambda i: (i, 0)),
                   pl.BlockSpec((D, tm), lambda i: (0, i)),
                   pl.BlockSpec((tm, LANES), lambda i: (i, 0))],
        out_shape=[jax.ShapeDtypeStruct((t_rows, n_main), BF16),
                   jax.ShapeDtypeStruct((D, t_rows), BF16),
                   jax.ShapeDtypeStruct((t_rows, LANES), F32)],
        scratch_shapes=[pltpu.VMEM((n_main, D), BF16), pltpu.VMEM((LANES, D), BF16),
                        pltpu.SemaphoreType.DMA((N_WEIGHT_COPIES,))],
        compiler_params=_params(("arbitrary",), 56),
    )(h, g_pre, w_full_t)


def _conv_parts(p_ref, w_ref):
    cb = p_ref[:, 0:128].astype(F32)
    cc = p_ref[:, 128:256].astype(F32)
    cx = p_ref[:, 256:384].astype(F32)
    cz = p_ref[:, 384:512].astype(F32)
    rows = cb.shape[0]
    w = w_ref[...]
    p = cc * cx
    conv = pltpu.roll(p, 1, 0) * w[0:1] + p * w[1:2] + pltpu.roll(p, rows - 1, 0) * w[2:3]
    sz = _sigmoid(cz)
    return cb, cc, cx, cz, p, conv, sz, w


def _conv_fwd(proj, conv_w, n_seq, lf):
    def body(p_ref, w_ref, y_ref):
        cb, _, _, cz, _, conv, sz, _ = _conv_parts(p_ref, w_ref)
        y_ref[...] = (cb * conv * (cz * sz)).astype(BF16)

    return pl.pallas_call(
        body, name="conv_fwd", grid=(n_seq, N_CONV_TILES),
        in_specs=[pl.BlockSpec((lf, 512), lambda b, j: (b, j)),
                  pl.BlockSpec((3, 128), lambda b, j: (0, j))],
        out_specs=pl.BlockSpec((lf, 128), lambda b, j: (b, j)),
        out_shape=jax.ShapeDtypeStruct((n_seq * lf, D), BF16),
        compiler_params=_params(("parallel", "parallel"), 48),
    )(proj, conv_w)


def _conv_bwd(proj, conv_w, dyc, n_seq, lf):
    def body(p_ref, w_ref, dy_ref, dp_ref, dw_ref):
        cb, cc, cx, cz, p, conv, sz, w = _conv_parts(p_ref, w_ref)
        rows = cb.shape[0]
        dy = dy_ref[...].astype(F32)
        silu = cz * sz
        dcb = dy * conv * silu
        dconv = dy * cb * silu
        dcz = dy * cb * conv * (sz * (1.0 + cz * (1.0 - sz)))
        d_next = pltpu.roll(dconv, rows - 1, 0)
        d_prev = pltpu.roll(dconv, 1, 0)
        dp = d_next * w[0:1] + dconv * w[1:2] + d_prev * w[2:3]
        dp_ref[:, 0:128] = dcb.astype(BF16)
        dp_ref[:, 128:256] = (dp * cx).astype(BF16)
        dp_ref[:, 256:384] = (dp * cc).astype(BF16)
        dp_ref[:, 384:512] = dcz.astype(BF16)
        dw_ref[0:1, :] = jnp.sum(dconv * pltpu.roll(p, 1, 0), axis=0, keepdims=True)
        dw_ref[1:2, :] = jnp.sum(dconv * p, axis=0, keepdims=True)
        dw_ref[2:3, :] = jnp.sum(dconv * pltpu.roll(p, rows - 1, 0), axis=0, keepdims=True)

    return pl.pallas_call(
        body, name="conv_bwd", grid=(n_seq, N_CONV_TILES),
        in_specs=[pl.BlockSpec((lf, 512), lambda b, j: (b, j)),
                  pl.BlockSpec((3, 128), lambda b, j: (0, j)),
                  pl.BlockSpec((lf, 128), lambda b, j: (b, j))],
        out_specs=[pl.BlockSpec((lf, 512), lambda b, j: (b, j)),
                   pl.BlockSpec((None, 3, 128), lambda b, j: (b, 0, j))],
        out_shape=[jax.ShapeDtypeStruct((n_seq * lf, W_CONV), BF16),
                   jax.ShapeDtypeStruct((n_seq, 3, D), F32)],
        compiler_params=_params(("parallel", "parallel"), 48),
    )(proj, conv_w, dyc)


GROUP = 3
GROUP_ROWS = GROUP * CHUNK


def _row_group(shape):
    row = lax.broadcasted_iota(jnp.int32, shape, 0)
    grp = jnp.zeros(shape, jnp.int32)
    for r in range(1, GROUP):
        grp = grp + (row >= r * CHUNK).astype(jnp.int32)
    return grp


def _lane_group(shape, width):
    lane = lax.broadcasted_iota(jnp.int32, shape, 1)
    grp = jnp.zeros(shape, jnp.int32)
    for r in range(1, GROUP):
        grp = grp + (lane >= r * width).astype(jnp.int32)
    return grp


def _score_mask(direction):
    shape = (GROUP_ROWS, GROUP_ROWS)
    row = lax.broadcasted_iota(jnp.int32, shape, 0)
    col = lax.broadcasted_iota(jnp.int32, shape, 1)
    same = _row_group(shape) == _lane_group(shape, CHUNK)
    return same & ((col <= row) if direction == 0 else (col > row))


def _diag_blocks(v):
    w = v.shape[1]
    wide = jnp.concatenate([v] * GROUP, axis=1)
    return jnp.where(_row_group(wide.shape) == _lane_group(wide.shape, w), wide, jnp.zeros_like(wide))


def _per_chunk_dot(lhs, state, transposed):
    outs = []
    for r in range(GROUP):
        rows = lhs[r * CHUNK:(r + 1) * CHUNK, :]
        blk = state[:, r * HEAD_K:(r + 1) * HEAD_K]
        outs.append(_dot_nt(rows, blk) if transposed else _dot(rows, blk))
    return jnp.concatenate(outs, axis=0)


def _chunk_cumsum(v, suffix):
    pos = lax.broadcasted_iota(jnp.int32, v.shape, 0) & (CHUNK - 1)
    shift = 1
    while shift < CHUNK:
        if suffix:
            moved = pltpu.roll(v, GROUP_ROWS - shift, 0)
            v = v + jnp.where(pos < CHUNK - shift, moved, 0.0)
        else:
            moved = pltpu.roll(v, shift, 0)
            v = v + jnp.where(pos >= shift, moved, 0.0)
        shift *= 2
    return v


def _per_chunk_rows(rows_of_chunk):
    w = rows_of_chunk[0].shape[1]
    return jnp.concatenate([jnp.broadcast_to(v, (CHUNK, w)) for v in rows_of_chunk], axis=0)


def _chunk_end_rows(direction, b):
    at = CHUNK - 1 if direction == 0 else 0
    return [b[r * CHUNK + at:r * CHUNK + at + 1, :] for r in range(GROUP)]


def _gla_gates(lr_bf, wg_ref, bg_ref, lf):
    z = _dot(lr_bf, wg_ref[...]) + bg_ref[...]
    valid = lax.broadcasted_iota(jnp.int32, (lf, HEAD_K), 0) >= PAD_FRONT
    return z, valid


def _group_unroll(n_groups):
    return n_groups if n_groups <= 11 else 1


def _group_rows(g):
    return pl.ds(pl.multiple_of(g * GROUP_ROWS, GROUP_ROWS), GROUP_ROWS)


def _chunk_decay(direction, g, r, b_s):
    base = g * GROUP_ROWS + r * CHUNK
    if direction == 0:
        grp = b_s[pl.ds(pl.multiple_of(base + CHUNK - 8, 8), 8), :]
        return jnp.exp(grp[7:8, :])
    grp = b_s[pl.ds(pl.multiple_of(base, 8), 8), :]
    return jnp.exp(grp[0:1, :])


def _state_scan(direction, n_groups, b_s, st_s, reverse):
    ascending = (direction == 0) != reverse

    def step(i, carry):
        g = i if ascending else n_groups - 1 - i
        for rr in range(GROUP):
            r = rr if ascending else GROUP - 1 - rr
            lanes = slice(r * HEAD_K, (r + 1) * HEAD_K)
            decay = _chunk_decay(direction, g, r, b_s)
            local = st_s[g, :, lanes]
            st_s[g, :, lanes] = carry
            carry = (local + carry * decay) if reverse else (carry * decay + local)
        return carry

    lax.fori_loop(0, n_groups, step, jnp.zeros((HEAD_V, HEAD_K), F32), unroll=_group_unroll(n_groups))


def _gla_states(direction, n_groups, qkv_ref, g_s, b_s, st_s):
    def local(g, carry):
        rows = _group_rows(g)
        b = _chunk_cumsum(g_s[rows, :], direction == 1)
        b_s[rows, :] = b
        b_end = _per_chunk_rows(_chunk_end_rows(direction, b))
        k = qkv_ref[rows, 128:256].astype(F32)
        v = qkv_ref[rows, 256:512]
        k_dec = (k * jnp.exp(b_end - b)).astype(BF16)
        st_s[g] = _dot_tn(v, _diag_blocks(k_dec))
        return carry

    lax.fori_loop(0, n_groups, local, 0, unroll=_group_unroll(n_groups))
    _state_scan(direction, n_groups, b_s, st_s, False)


def _gla_fwd(proj, lr, wgf, wgb, bgf, bgb, n_seq, lf):
    assert lf % GROUP_ROWS == 0
    n_groups = lf // GROUP_ROWS
    scale = HEAD_K ** -0.5

    def body(qkv_ref, lr_ref, wgf_ref, wgb_ref, bgf_ref, bgb_ref, o_ref, g_s, b_s, st_s):
        lr_bf = lr_ref[...].astype(BF16)
        for direction in (0, 1):
            wg_ref, bg_ref = ((wgf_ref, bgf_ref), (wgb_ref, bgb_ref))[direction]
            z, valid = _gla_gates(lr_bf, wg_ref, bg_ref, lf)
            g_s[...] = jnp.where(valid, _log_sigmoid(z) / GATE_NORM, 0.0)
            smask = _score_mask(direction)
            _gla_states(direction, n_groups, qkv_ref, g_s, b_s, st_s)

            def out(g, carry):
                rows = _group_rows(g)
                b = b_s[rows, :]
                q = qkv_ref[rows, 0:128].astype(F32) * scale
                k = qkv_ref[rows, 128:256].astype(F32)
                v = qkv_ref[rows, 256:512]
                q_in = (q * jnp.exp(b)).astype(BF16)
                k_in = (k * jnp.exp(-b)).astype(BF16)
                s = jnp.where(smask, _dot_nt(q_in, k_in), 0.0).astype(BF16)
                o = _dot(s, v) + _per_chunk_dot(q_in, st_s[g].astype(BF16), True)
                if direction == 0:
                    o_ref[rows, :] = o
                else:
                    o_ref[rows, :] = o_ref[rows, :] + o
                return carry

            lax.fori_loop(0, n_groups, out, 0, unroll=_group_unroll(n_groups))

    return pl.pallas_call(
        body, name="gla_fwd", grid=(n_seq, N_HEADS),
        in_specs=[pl.BlockSpec((lf, 512), lambda b, h: (b, N_CONV_TILES + h)),
                  pl.BlockSpec((lf, LANES), lambda b, h: (b, 0)),
                  pl.BlockSpec((None, LANES, HEAD_K), lambda b, h: (h, 0, 0)),
                  pl.BlockSpec((None, LANES, HEAD_K), lambda b, h: (h, 0, 0)),
                  pl.BlockSpec((None, 1, HEAD_K), lambda b, h: (h, 0, 0)),
                  pl.BlockSpec((None, 1, HEAD_K), lambda b, h: (h, 0, 0))],
        out_specs=pl.BlockSpec((lf, HEAD_V), lambda b, h: (b, h)),
        out_shape=jax.ShapeDtypeStruct((n_seq * lf, D), F32),
        scratch_shapes=[pltpu.VMEM((lf, HEAD_K), F32), pltpu.VMEM((lf, HEAD_K), F32),
                        pltpu.VMEM((n_groups, HEAD_V, GROUP * HEAD_K), F32)],
        compiler_params=_params(("parallel", "parallel"), 48),
    )(proj, lr, wgf, wgb, bgf, bgb)


def _gla_bwd(proj, lr, d_o, wgf, wgb, bgf, bgb, n_seq, lf, token):
    assert lf % GROUP_ROWS == 0
    n_groups = lf // GROUP_ROWS
    scale = HEAD_K ** -0.5

    def body(qkv_ref, lr_ref, do_ref, wgf_ref, wgb_ref, bgf_ref, bgb_ref, token_ref,
             dqkv_ref, dlr_ref, dwgf_ref, dwgb_ref, dbg_ref,
             g_s, b_s, fac_s, dg_s, st_s, dst_s, acc_s):
        lr_bf = lr_ref[...].astype(BF16)
        dlr = jnp.zeros((lf, LANES), F32)
        for direction in (0, 1):
            wg_ref, bg_ref = ((wgf_ref, bgf_ref), (wgb_ref, bgb_ref))[direction]
            z, valid = _gla_gates(lr_bf, wg_ref, bg_ref, lf)
            g_s[...] = jnp.where(valid, _log_sigmoid(z) / GATE_NORM, 0.0)
            fac_s[...] = jnp.where(valid, _sigmoid(-z) / GATE_NORM, 0.0)
            smask = _score_mask(direction)
            end_row = CHUNK - 1 if direction == 0 else 0
            _gla_states(direction, n_groups, qkv_ref, g_s, b_s, st_s)

            def state_grad_local(g, carry):
                rows = _group_rows(g)
                q = qkv_ref[rows, 0:128].astype(F32) * scale
                q_in = (q * jnp.exp(b_s[rows, :])).astype(BF16)
                dst_s[g] = _dot_tn(do_ref[rows, :], _diag_blocks(q_in))
                return carry

            lax.fori_loop(0, n_groups, state_grad_local, 0, unroll=_group_unroll(n_groups))
            _state_scan(direction, n_groups, b_s, dst_s, True)

            def group_grads(g, carry):
                rows = _group_rows(g)
                b = b_s[rows, :]
                ends = _chunk_end_rows(direction, b)
                b_end = _per_chunk_rows(ends)
                q = qkv_ref[rows, 0:128].astype(F32) * scale
                k = qkv_ref[rows, 128:256].astype(F32)
                v = qkv_ref[rows, 256:512]
                d_out = do_ref[rows, :]
                e_pos = jnp.exp(b)
                e_neg = jnp.exp(-b)
                e_end = jnp.exp(b_end - b)
                q_in = q * e_pos
                k_in = k * e_neg
                k_dec = k * e_end
                q_in_bf = q_in.astype(BF16)
                k_in_bf = k_in.astype(BF16)
                state = st_s[g]
                d_state = dst_s[g]
                state_bf = state.astype(BF16)
                d_state_bf = d_state.astype(BF16)
                s = jnp.where(smask, _dot_nt(q_in_bf, k_in_bf), 0.0).astype(BF16)
                ds = jnp.where(smask, _dot_nt(d_out, v), 0.0).astype(BF16)
                dv = _dot_tn(s, d_out) + _per_chunk_dot(k_dec.astype(BF16), d_state_bf, True)
                dq_in = _dot(ds, k_in_bf) + _per_chunk_dot(d_out, state_bf, False)
                dk_in = _dot_tn(ds, q_in_bf)
                dk_dec = _per_chunk_dot(v, d_state_bf, False)
                dq = dq_in * e_pos * scale
                dk = dk_in * e_neg + dk_dec * e_end
                if direction == 0:
                    acc_s[rows, 0:128] = dq
                    acc_s[rows, 128:256] = dk
                    acc_s[rows, 256:512] = dv
                else:
                    dqkv_ref[rows, 0:128] = (acc_s[rows, 0:128] + dq).astype(BF16)
                    dqkv_ref[rows, 128:256] = (acc_s[rows, 128:256] + dk).astype(BF16)
                    dqkv_ref[rows, 256:512] = (acc_s[rows, 256:512] + dv).astype(BF16)
                dkk = dk_dec * k_dec
                db = dq_in * q_in - dk_in * k_in - dkk
                d_decay = jnp.sum(d_state * state, axis=0, keepdims=True)
                db_end = [jnp.sum(dkk[r * CHUNK:(r + 1) * CHUNK, :], axis=0, keepdims=True)
                          + d_decay[:, r * HEAD_K:(r + 1) * HEAD_K] * jnp.exp(ends[r]) for r in range(GROUP)]
                row = lax.broadcasted_iota(jnp.int32, (GROUP_ROWS, HEAD_K), 0)
                at_end = row == end_row
                for r in range(1, GROUP):
                    at_end = at_end | (row == r * CHUNK + end_row)
                db = db + jnp.where(at_end, _per_chunk_rows(db_end), 0.0)
                dg_s[rows, :] = _chunk_cumsum(db, direction == 0)
                return carry

            lax.fori_loop(0, n_groups, group_grads, 0, unroll=_group_unroll(n_groups))

            dz = dg_s[...] * fac_s[...]
            dz_bf = dz.astype(BF16)
            dbg_ref[direction:direction + 1, :] = jnp.sum(dz, axis=0, keepdims=True)
            (dwgf_ref, dwgb_ref)[direction][...] = _dot_tn(lr_bf, dz_bf)
            dlr = dlr + _dot_nt(dz_bf, wg_ref[...])

        @pl.when(pl.program_id(1) == 0)
        def _():
            dlr_ref[...] = dlr

        @pl.when(pl.program_id(1) != 0)
        def _():
            dlr_ref[...] = dlr_ref[...] + dlr

    gate_w = pl.BlockSpec((None, LANES, HEAD_K), lambda b, h: (h, 0, 0))
    gate_b = pl.BlockSpec((None, 1, HEAD_K), lambda b, h: (h, 0, 0))
    return pl.pallas_call(
        body, name="gla_bwd", grid=(n_seq, N_HEADS),
        in_specs=[pl.BlockSpec((lf, 512), lambda b, h: (b, N_CONV_TILES + h)),
                  pl.BlockSpec((lf, LANES), lambda b, h: (b, 0)),
                  pl.BlockSpec((lf, HEAD_V), lambda b, h: (b, h)),
                  gate_w, gate_w, gate_b, gate_b,
                  pl.BlockSpec((8, LANES), lambda b, h: (0, 0))],
        out_specs=[pl.BlockSpec((lf, 512), lambda b, h: (b, h)),
                   pl.BlockSpec((lf, LANES), lambda b, h: (b, 0)),
                   pl.BlockSpec((None, None, LANES, HEAD_K), lambda b, h: (b, h, 0, 0)),
                   pl.BlockSpec((None, None, LANES, HEAD_K), lambda b, h: (b, h, 0, 0)),
                   pl.BlockSpec((None, None, 2, HEAD_K), lambda b, h: (b, h, 0, 0))],
        out_shape=[jax.ShapeDtypeStruct((n_seq * lf, W_GLA), BF16),
                   jax.ShapeDtypeStruct((n_seq * lf, LANES), F32),
                   jax.ShapeDtypeStruct((n_seq, N_HEADS, LANES, HEAD_K), F32),
                   jax.ShapeDtypeStruct((n_seq, N_HEADS, LANES, HEAD_K), F32),
                   jax.ShapeDtypeStruct((n_seq, N_HEADS, 2, HEAD_K), F32)],
        scratch_shapes=[pltpu.VMEM((lf, HEAD_K), F32), pltpu.VMEM((lf, HEAD_K), F32),
                        pltpu.VMEM((lf, HEAD_K), F32), pltpu.VMEM((lf, HEAD_K), F32),
                        pltpu.VMEM((n_groups, HEAD_V, GROUP * HEAD_K), F32),
                        pltpu.VMEM((n_groups, HEAD_V, GROUP * HEAD_K), F32),
                        pltpu.VMEM((lf, 512), F32)],
        compiler_params=_params(("parallel", "arbitrary"), 56),
    )(proj, lr, d_o, wgf, wgb, bgf, bgb, token)


def _tail(h, tgt, yc, o, proj, w3, gamma, g_post, lf):
    t_rows = h.shape[0]
    tm = _pick_tile(t_rows, 256, CHUNK)
    n_chunks = lf // CHUNK
    per_tile = tm // CHUNK

    def body(h_ref, tgt_ref, yc_ref, o_ref, r_ref, ma_ref, mb_ref, w_hbm, gamma_ref, gpost_ref,
             dres_ref, yg_ref, merged_ref, dout_ref, dpc_ref, dpg_ref, dyc_ref, do_ref, dtail_ref,
             loss_ref, dgpost_ref, dgamma_ref, w_s, w_sem):
        i = pl.program_id(0)

        @pl.when(i == 0)
        def _():
            cp = pltpu.make_async_copy(w_hbm, w_s, w_sem)
            cp.start()
            cp.wait()
            loss_ref[...] = jnp.zeros_like(loss_ref)
            dgpost_ref[...] = jnp.zeros_like(dgpost_ref)
            dgamma_ref[...] = jnp.zeros_like(dgamma_ref)

        gamma = gamma_ref[...]
        o = o_ref[...]
        r = r_ref[...].astype(F32)
        sr = _sigmoid(r)
        silu_r = r * sr
        n_parts, rstd_parts = [], []
        for hd in range(N_HEADS):
            oh = o[:, hd * HEAD_V:(hd + 1) * HEAD_V]
            rstd = lax.rsqrt(jnp.mean(oh * oh, axis=-1, keepdims=True) + EPS)
            n_parts.append(oh * rstd)
            rstd_parts.append(rstd)
        n = jnp.concatenate(n_parts, axis=-1)
        gamma_t = jnp.concatenate([gamma] * N_HEADS, axis=-1)
        yg = n * gamma_t * silu_r
        yg_bf = yg.astype(BF16)
        yg_ref[...] = yg_bf
        yc = yc_ref[...]
        pc = _dot(yc, w_s[0])
        pg = _dot(yg_bf, w_s[1])
        sa = _sigmoid(ma_ref[...].astype(F32))
        sb = _sigmoid(mb_ref[...].astype(F32))
        merged = (sa * pc + sb * pg).astype(BF16)
        merged_ref[...] = merged
        out = _dot(merged, w_s[2])
        rstd2 = lax.rsqrt(jnp.mean(out * out, axis=-1, keepdims=True) + EPS)
        nn = out * rstd2
        gpost = gpost_ref[...]
        y = h_ref[...] + nn * gpost

        rowi = lax.broadcasted_iota(jnp.int32, (tm, 1), 0)
        keep = jnp.zeros((tm, 1), F32)
        for kk in range(per_tile):
            is_tok = ((i * per_tile + kk) % n_chunks) != 0
            f = jnp.where(is_tok, 1.0, 0.0)
            keep = jnp.where((rowi >= kk * CHUNK) & (rowi < (kk + 1) * CHUNK), f, keep)
        diff = (y - tgt_ref[...]) * keep
        loss_ref[...] += jnp.sum(diff * diff) * (0.5 / D)
        dy = diff * (1.0 / D)
        dres_ref[...] = dy
        dgpost_ref[...] += jnp.sum(dy * nn, axis=0, keepdims=True)
        dn = dy * gpost
        dout_f = rstd2 * (dn - nn * jnp.mean(dn * nn, axis=-1, keepdims=True))
        dout = dout_f.astype(BF16)
        dout_ref[...] = jnp.transpose(dout_f).astype(BF16)
        dmerged = _dot_nt(dout, w_s[2])
        dpc_f = dmerged * sa
        dpg_f = dmerged * sb
        dpc = dpc_f.astype(BF16)
        dpg = dpg_f.astype(BF16)
        dpc_ref[...] = jnp.transpose(dpc_f).astype(BF16)
        dpg_ref[...] = jnp.transpose(dpg_f).astype(BF16)
        dtail_ref[:, D:2 * D] = (dmerged * pc * (sa * (1.0 - sa))).astype(BF16)
        dtail_ref[:, 2 * D:3 * D] = (dmerged * pg * (sb * (1.0 - sb))).astype(BF16)
        dyc_ref[...] = _dot_nt(dpc, w_s[0]).astype(BF16)
        dyg = _dot_nt(dpg, w_s[1])
        dtail_ref[:, 0:D] = (dyg * n * gamma_t * (sr * (1.0 + r * (1.0 - sr)))).astype(BF16)
        dgam_full = jnp.sum(dyg * n * silu_r, axis=0, keepdims=True)
        dgam = dgam_full[:, 0:HEAD_V]
        for hd in range(1, N_HEADS):
            dgam = dgam + dgam_full[:, hd * HEAD_V:(hd + 1) * HEAD_V]
        dgamma_ref[...] += dgam
        dng = dyg * gamma_t * silu_r
        do_parts = []
        for hd in range(N_HEADS):
            sl = slice(hd * HEAD_V, (hd + 1) * HEAD_V)
            dnh = dng[:, sl]
            nh = n_parts[hd]
            do_parts.append(rstd_parts[hd] * (dnh - nh * jnp.mean(dnh * nh, axis=-1, keepdims=True)))
        do_ref[...] = jnp.concatenate(do_parts, axis=-1).astype(BF16)

    row = lambda c: pl.BlockSpec((tm, D), lambda i: (i, c))
    col = pl.BlockSpec((D, tm), lambda i: (0, i))
    const = lambda shape: pl.BlockSpec(shape, lambda i: (0, 0))
    act = jax.ShapeDtypeStruct((t_rows, D), BF16)
    act_t = jax.ShapeDtypeStruct((D, t_rows), BF16)
    return pl.pallas_call(
        body, name="tail", grid=(t_rows // tm,),
        in_specs=[row(0), row(0), row(0), row(0), row(6), row(7), row(8),
                  pl.BlockSpec(memory_space=pl.ANY), const((1, HEAD_V)), const((1, D))],
        out_specs=[row(0)] * 3 + [col] * 3 + [row(0)] * 2
                  + [pl.BlockSpec((tm, W_TAIL), lambda i: (i, 0)),
                     const((8, LANES)), const((1, D)), const((1, HEAD_V))],
        out_shape=[jax.ShapeDtypeStruct((t_rows, D), F32)] + [act] * 2 + [act_t] * 3 + [act] * 2
                  + [jax.ShapeDtypeStruct((t_rows, W_TAIL), BF16),
                     jax.ShapeDtypeStruct((8, LANES), F32),
                     jax.ShapeDtypeStruct((1, D), F32),
                     jax.ShapeDtypeStruct((1, HEAD_V), F32)],
        scratch_shapes=[pltpu.VMEM((3, D, D), BF16), pltpu.SemaphoreType.DMA],
        compiler_params=_params(("arbitrary",), 56),
    )(h, tgt, yc, o, proj, proj, proj, w3, gamma, g_post)


def _wgrad_t(a_t, b, name, out_dtype=BF16):
    m, t_rows = a_t.shape
    n = b.shape[1]
    tn = D if n % D == 0 else n
    tk = _pick_tile(t_rows, 768, LANES)
    n_k = t_rows // tk

    def body(a_ref, b_ref, o_ref, acc):
        k = pl.program_id(1)

        @pl.when(k == 0)
        def _():
            acc[...] = jnp.zeros_like(acc)

        acc[...] += _dot(a_ref[...], b_ref[...].astype(BF16))

        @pl.when(k == n_k - 1)
        def _():
            o_ref[...] = jnp.transpose(acc[...]).astype(out_dtype)

    return pl.pallas_call(
        body, name=name, grid=(n // tn, n_k),
        in_specs=[pl.BlockSpec((m, tk), lambda j, k: (0, k)),
                  pl.BlockSpec((tk, tn), lambda j, k: (k, j))],
        out_specs=pl.BlockSpec((tn, m), lambda j, k: (j, 0)),
        out_shape=jax.ShapeDtypeStruct((n, m), out_dtype),
        scratch_shapes=[pltpu.VMEM((m, tn), F32)],
        compiler_params=_params(("parallel", "arbitrary"), 48),
    )(a_t, b)


def _dgrad_in(dpc, dpg, dpt, dlr, w_full_t, h, g_pre, dres, token):
    t_rows = h.shape[0]
    tm = _pick_tile(t_rows, 384, 16)
    n_main = N_MAIN

    def body(dpc_ref, dpg_ref, dpt_ref, dlr_ref, w_hbm, h_ref, g_ref, dres_ref, token_ref,
             dh_ref, dg_ref, w_s, wlr_s, w_sems):
        @pl.when(pl.program_id(0) == 0)
        def _():
            _load_weights(w_hbm, w_s, wlr_s, w_sems)
            dg_ref[...] = jnp.zeros_like(dg_ref)

        du = _dot(dlr_ref[...].astype(BF16), wlr_s[...])
        du += _dot(dpc_ref[...], w_s[0:W_CONV, :])
        du += _dot(dpg_ref[...], w_s[W_CONV:W_CONV + W_GLA, :])
        du += _dot(dpt_ref[...], w_s[W_CONV + W_GLA:n_main, :])
        hh = h_ref[...]
        rstd = lax.rsqrt(jnp.mean(hh * hh, axis=-1, keepdims=True) + EPS)
        xhat = hh * rstd
        dg_ref[...] += jnp.sum(du * xhat, axis=0, keepdims=True)
        dx = du * g_ref[...]
        dh_ref[...] = rstd * (dx - xhat * jnp.mean(dx * xhat, axis=-1, keepdims=True)) + dres_ref[...]

    row = lambda width: pl.BlockSpec((tm, width), lambda i: (i, 0))
    return pl.pallas_call(
        body, name="dgrad_in", grid=(t_rows // tm,),
        in_specs=[row(W_CONV), row(W_GLA), row(W_TAIL), row(LANES),
                  pl.BlockSpec(memory_space=pl.ANY),
                  row(D), pl.BlockSpec((1, D), lambda i: (0, 0)), row(D),
                  pl.BlockSpec((8, LANES), lambda i: (0, 0))],
        out_specs=[row(D), pl.BlockSpec((1, D), lambda i: (0, 0))],
        out_shape=[jax.ShapeDtypeStruct((t_rows, D), F32), jax.ShapeDtypeStruct((1, D), F32)],
        scratch_shapes=[pltpu.VMEM((n_main, D), BF16), pltpu.VMEM((LANES, D), BF16),
                        pltpu.SemaphoreType.DMA((N_WEIGHT_COPIES,))],
        compiler_params=_params(("arbitrary",), 56),
    )(dpc, dpg, dpt, dlr, w_full_t, h, g_pre, dres, token)


def _reference_rows(g_conv, g_gla, g_tail, g_lr):
    conv = g_conv.reshape(N_CONV_TILES, 4, 128, D).transpose(1, 0, 2, 3).reshape(W_CONV, D)
    gla = g_gla.reshape(N_HEADS, 512, D)
    q = gla[:, 0:128].reshape(N_HEADS * HEAD_K, D)
    k = gla[:, 128:256].reshape(N_HEADS * HEAD_K, D)
    v = gla[:, 256:512].reshape(N_HEADS * HEAD_V, D)
    return jnp.concatenate([conv, q, k, v, g_tail[0:D], g_lr[0:2 * RANK], g_tail[D:3 * D]], axis=0)


def kernel(x, meta_tokens, norm_pre, w_in, conv_w, w_gate_fwd, b_gate_fwd, w_gate_bwd, b_gate_bwd, gla_norm, w_out_conv, w_out_gla, w_merge_out, norm_post, loss_target, m_meta_tokens, m_norm_pre, m_w_in, m_conv_w, m_w_gate_fwd, m_b_gate_fwd, m_w_gate_bwd, m_b_gate_bwd, m_gla_norm, m_w_out_conv, m_w_out_gla, m_w_merge_out, m_norm_post, v_meta_tokens, v_norm_pre, v_w_in, v_conv_w, v_w_gate_fwd, v_b_gate_fwd, v_w_gate_bwd, v_b_gate_bwd, v_gla_norm, v_w_out_conv, v_w_out_gla, v_w_merge_out, v_norm_post):
    n_seq, seq, _ = x.shape
    lf = CHUNK + seq
    t_rows = n_seq * lf
    shard = 2 * lax.axis_index("x") + lax.axis_index("y")
    shard_arr = jnp.reshape(shard, (1,)).astype(jnp.int32)

    w_in_slots = _cast_into_slot(jnp.transpose(w_in[0]), shard_arr, "cast_w_in")
    w_out_bf = _cast_bf16(jnp.concatenate([w_out_conv[0], w_out_gla[0], w_merge_out[0]], axis=0), "cast_w_out")
    w_in_all, meta_all, conv_all, wgf_all, wgb_all = _gather_via_sibling(
        "gather_w_in", [w_in_slots, meta_tokens, conv_w[0], w_gate_fwd[0], w_gate_bwd[0]],
        (True, False, False, False, False))
    w_out_state, _ = _plane_start("gather_w_out_start", [w_out_bf], "gather", wgb_all)

    w_full_t = w_in_all.reshape(N_IN, D)
    meta_full = jnp.transpose(meta_all, (1, 0, 2)).reshape(N_META, D)
    conv_full = jnp.transpose(conv_all, (1, 0, 2)).reshape(3, D)
    wgf = jnp.pad(wgf_all, ((0, 0), (0, LANES - RANK), (0, 0))).astype(BF16)
    wgb = jnp.pad(wgb_all, ((0, 0), (RANK, LANES - 2 * RANK), (0, 0))).astype(BF16)
    bgf = b_gate_fwd.reshape(N_HEADS, 1, HEAD_K)
    bgb = b_gate_bwd.reshape(N_HEADS, 1, HEAD_K)

    head = jnp.concatenate([jnp.zeros((PAD_FRONT, D), F32), meta_full], axis=0)
    h = jnp.concatenate([jnp.broadcast_to(head[None], (n_seq, CHUNK, D)), x], axis=1).reshape(t_rows, D)
    tgt = jnp.pad(loss_target, ((0, 0), (CHUNK, 0), (0, 0))).reshape(t_rows, D)

    proj, u_t, lr = _in_proj(h, norm_pre, w_full_t)
    yc = _conv_fwd(proj, conv_full, n_seq, lf)
    o = _gla_fwd(proj, lr, wgf, wgb, bgf, bgb, n_seq, lf)
    (w_out_landed,) = _plane_wait("gather_w_out_wait", w_out_state, "gather", o)
    slot_ids = lax.broadcasted_iota(jnp.int32, (4, 1, 1), 0)
    w_out_all = jnp.where(slot_ids == shard, w_out_bf[None], w_out_landed)
    w3 = jnp.transpose(w_out_all.reshape(4, 3, D // 4, D), (1, 0, 2, 3)).reshape(3, D, D)
    (dres, yg, merged, dout_t, dpc_t, dpg_t, dyc, d_o, dtail, loss_acc, d_gpost, d_gamma) = _tail(
        h, tgt, yc, o, proj, w3, gla_norm, norm_post, lf)
    g_w_oc = _wgrad_t(dpc_t, yc, "wgrad_out_conv")
    g_w_og = _wgrad_t(dpg_t, yg, "wgrad_out_gla")
    g_w_mo = _wgrad_t(dout_t, merged, "wgrad_merge_out")
    g_out_slots = jnp.concatenate([g.reshape(4, D // 4, D) for g in (g_w_oc, g_w_og, g_w_mo)], axis=1)
    out_state, out_token = _plane_start("scatter_out_grads_start", [g_out_slots], "scatter", g_w_mo)
    dgla, dlr, dwgf_p, dwgb_p, dbg_p = _gla_bwd(proj, lr, d_o, wgf, wgb, bgf, bgb, n_seq, lf, out_token)
    (got_out,) = _plane_wait("scatter_out_grads_wait", out_state, "scatter", dlr)
    dconv, dconvw_p = _conv_bwd(proj, conv_full, dyc, n_seq, lf)
    g_conv = _wgrad_t(u_t, dconv, "wgrad_in_conv")
    g_gla = _wgrad_t(u_t, dgla, "wgrad_in_gla")
    g_tail = _wgrad_t(u_t, dtail, "wgrad_in_tail")
    g_lr = _wgrad_t(u_t, dlr, "wgrad_in_lr")

    g_in_slots = _reference_rows(g_conv, g_gla, g_tail, g_lr).reshape(4, SHARD_IN, D)
    in_state, in_token = _plane_start("scatter_in_grads_start", [g_in_slots], "scatter", g_lr)
    dh, d_gpre = _dgrad_in(dconv, dgla, dtail, dlr, w_full_t, h, norm_pre, dres, in_token)
    (got_in,) = _plane_wait("scatter_in_grads_wait", in_state, "scatter", d_gpre)

    plane_in = _sum_slots(got_in, "sum_w_in_grads", own=g_in_slots, slot=shard_arr)
    plane_out = _sum_slots(got_out, "sum_w_out_grads", own=g_out_slots, slot=shard_arr)
    swap_state, swap_token = _plane_start("swap_plane_sums_start", [plane_in, plane_out], "swap", plane_out)

    dh3 = dh.reshape(n_seq, lf, D)
    grad_x = dh3[:, CHUNK:, :]

    d_meta = jnp.sum(dh3[:, PAD_FRONT:CHUNK, :], axis=0)
    d_convw = jnp.sum(dconvw_p, axis=0)
    d_wgf = jnp.transpose(jnp.sum(dwgf_p, axis=0)[:, 0:RANK, :], (1, 0, 2)).reshape(RANK, N_HEADS * HEAD_K)
    d_wgb = jnp.transpose(jnp.sum(dwgb_p, axis=0)[:, RANK:2 * RANK, :], (1, 0, 2)).reshape(RANK, N_HEADS * HEAD_K)
    d_bg = jnp.sum(dbg_p, axis=0)
    d_bgf = d_bg[:, 0, :].reshape(1, N_HEADS * HEAD_K)
    d_bgb = d_bg[:, 1, :].reshape(1, N_HEADS * HEAD_K)
    loss_part = loss_acc[0:1, :] + swap_token[0:1, :]
    partials = [d_meta, d_convw, d_wgf, d_wgb, d_gpre, d_bgf, d_bgb, d_gamma, d_gpost, loss_part]
    (g_meta, g_convw, g_wgf, g_wgb, g_npre, g_bgf, g_bgb, g_gnorm, g_npost, loss_row) = _sum_small(
        _exchange("gather_small_grads", partials, ALL_FLIPS, (4, 2, 1), "gather"), "sum_small_grads")
    loss = loss_row[0, 0]
    small_out = _adamw_small(
        [(meta_tokens, g_meta, m_meta_tokens, v_meta_tokens), (norm_pre, g_npre, m_norm_pre, v_norm_pre),
         (conv_w, g_convw, m_conv_w, v_conv_w), (w_gate_fwd, g_wgf, m_w_gate_fwd, v_w_gate_fwd),
         (b_gate_fwd, g_bgf, m_b_gate_fwd, v_b_gate_fwd), (w_gate_bwd, g_wgb, m_w_gate_bwd, v_w_gate_bwd),
         (b_gate_bwd, g_bgb, m_b_gate_bwd, v_b_gate_bwd), (gla_norm, g_gnorm, m_gla_norm, v_gla_norm),
         (norm_post, g_npost, m_norm_post, v_norm_post)], shard_arr, "adamw_small")

    other_in, other_out = _plane_wait("swap_plane_sums_wait", swap_state, "swap", small_out[0][0])
    big_in = _adamw(jnp.transpose(w_in[0]), [plane_in, other_in], jnp.transpose(m_w_in[0]), jnp.transpose(v_w_in[0]),
                    "adamw_w_in")
    out_params = ((w_out_conv, m_w_out_conv, v_w_out_conv), (w_out_gla, m_w_out_gla, v_w_out_gla),
                  (w_merge_out, m_w_merge_out, v_w_merge_out))
    big_out = [_adamw(w[0], [plane_out, other_out], m[0], v[0], f"adamw_w_out_{i}", grad_row=i * (D // 4))
               for i, (w, m, v) in enumerate(out_params)]

    results = []
    for kind in range(4):
        small_kind = [p[kind] for p in small_out]
        w_in_part = jnp.transpose(big_in[kind])[None]
        outs3 = [big_out[i][kind][None] for i in range(3)]
        results.extend(small_kind[0:2] + [w_in_part] + small_kind[2:8] + outs3 + small_kind[8:9])
    return (loss, grad_x, *results)
```

```python
import functools

import jax
import jax.numpy as jnp
from jax import lax
from jax.experimental import pallas as pl
from jax.experimental.pallas import tpu as pltpu

F32 = jnp.float32
BF16 = jnp.bfloat16
MESH = pl.DeviceIdType.MESH

D = 1024
N_META = 16
CHUNK = 64
PAD_FRONT = CHUNK - N_META
N_HEADS = 4
HEAD_K = 128
HEAD_V = 256
RANK = 16
EPS = 1e-6
GATE_NORM = 16.0
N_IN = 9248
SHARD_IN = N_IN // 4
LANES = 128
N_CONV_TILES = 8
W_CONV = 4096
W_GLA = 2048
W_TAIL = 3072
N_MAIN = W_CONV + W_GLA + W_TAIL
OFF_Q, OFF_K, OFF_V, OFF_R = 4096, 4608, 5120, 6144
OFF_LR, OFF_MA = 7168, 7200
MIB = 1024 * 1024

ADAM_LR = 0.001
ADAM_B1 = 0.9
ADAM_B2 = 0.999
ADAM_EPS = 1e-08
ADAM_WD = 0.01
ADAM_STEP = 10


def _params(sem=None, vmem_mib=None):
    return pltpu.CompilerParams(
        dimension_semantics=sem,
        vmem_limit_bytes=None if vmem_mib is None else vmem_mib * MIB)


def _pick_tile(n, target, mult):
    best = None
    for t in range(mult, min(n, target) + 1, mult):
        if n % t == 0:
            best = t
    return n if best is None else best


def _sigmoid(v):
    return 1.0 / (1.0 + jnp.exp(-v))


def _log_sigmoid(v):
    return jnp.minimum(v, 0.0) - jnp.log(1.0 + jnp.exp(-jnp.abs(v)))


def _dot(a, b):
    return jnp.dot(a, b, preferred_element_type=F32)


def _dot_nt(a, b):
    return lax.dot_general(a, b, (((1,), (1,)), ((), ())), preferred_element_type=F32)


def _dot_tn(a, b):
    return lax.dot_general(a, b, (((0,), (0,)), ((), ())), preferred_element_type=F32)


ALL_FLIPS = tuple((m >> 2 & 1, m >> 1 & 1, m & 1) for m in range(1, 8))


def _exchange(name, arrs, flips, slot_weights, mode):
    n = len(arrs)
    n_slots = 1
    for w in slot_weights:
        n_slots += w
    if mode == "gather":
        out_shape = [jax.ShapeDtypeStruct((n_slots,) + a.shape, a.dtype) for a in arrs]
    else:
        out_shape = [jax.ShapeDtypeStruct(a.shape, a.dtype) for a in arrs]

    def body(*refs):
        ins, outs = refs[:n], refs[n:2 * n]
        send_sems, recv_sems, local_sems = refs[2 * n:]
        pos = (lax.axis_index("x"), lax.axis_index("y"), lax.axis_index("c"))

        def slot_of(p):
            return p[0] * slot_weights[0] + p[1] * slot_weights[1] + p[2] * slot_weights[2]

        peers = [tuple(1 - pos[a] if f[a] else pos[a] for a in range(3)) for f in flips]
        me = slot_of(pos)
        local = []
        sends = []
        for i in range(n):
            if mode != "swap":
                src = ins[i] if mode == "gather" else ins[i].at[me]
                cp = pltpu.make_async_copy(src, outs[i].at[me], local_sems.at[i])
                cp.start()
                local.append(cp)
            for k, peer in enumerate(peers):
                if mode == "gather":
                    src, dst = ins[i], outs[i].at[me]
                elif mode == "scatter":
                    src, dst = ins[i].at[slot_of(peer)], outs[i].at[me]
                else:
                    src, dst = ins[i], outs[i]
                cp = pltpu.make_async_remote_copy(
                    src_ref=src, dst_ref=dst, send_sem=send_sems.at[i, k], recv_sem=recv_sems.at[i, k],
                    device_id=peer, device_id_type=MESH)
                cp.start()
                sends.append(cp)
        for i in range(n):
            for k, peer in enumerate(peers):
                if mode == "gather":
                    src, dst = ins[i], outs[i].at[slot_of(peer)]
                elif mode == "scatter":
                    src, dst = ins[i].at[me], outs[i].at[slot_of(peer)]
                else:
                    src, dst = ins[i], outs[i]
                arrival = pltpu.make_async_remote_copy(
                    src_ref=src, dst_ref=dst, send_sem=send_sems.at[i, k], recv_sem=recv_sems.at[i, k],
                    device_id=peer, device_id_type=MESH)
                arrival.wait_recv()
        for cp in sends:
            cp.wait_send()
        for cp in local:
            cp.wait()

    hbm = pl.BlockSpec(memory_space=pl.ANY)
    outs = pl.pallas_call(
        body, name=name, out_shape=out_shape,
        in_specs=[hbm] * n, out_specs=[hbm] * n,
        scratch_shapes=[pltpu.SemaphoreType.DMA((n, len(flips))),
                        pltpu.SemaphoreType.DMA((n, len(flips))),
                        pltpu.SemaphoreType.DMA((n,))],
        compiler_params=pltpu.CompilerParams(has_side_effects=True),
    )(*arrs)
    return list(outs)


def _gather_via_sibling(name, arrs, slotted):
    n = len(arrs)
    out_shape = [jax.ShapeDtypeStruct(a.shape if slotted[i] else (4,) + a.shape, a.dtype)
                 for i, a in enumerate(arrs)]

    def body(*refs):
        ins, outs = refs[:n], refs[n:2 * n]
        send_sems, recv_sems, local_sems = refs[2 * n:]
        x, y, c = lax.axis_index("x"), lax.axis_index("y"), lax.axis_index("c")
        me = 2 * x + y
        chips = [(1 - x, y), (x, 1 - y), (1 - x, 1 - y)]

        def half(ref, which):
            rows = ref.shape[0]
            cut = rows // 2 // 16 * 16
            return ref.at[pl.ds(0, cut)] if which == 0 else ref.at[pl.ds(cut, rows - cut)]

        def copy(src, dst, i, k, to):
            return pltpu.make_async_remote_copy(
                src_ref=src, dst_ref=dst, send_sem=send_sems.at[i, k], recv_sem=recv_sems.at[i, k],
                device_id=to, device_id_type=MESH)

        def run(mine):
            other = 1 - mine
            local, sends = [], []
            whole = [(not slotted[i]) and arrs[i].shape[0] < 32 for i in range(n)]
            for i in range(n):
                own = outs[i].at[me] if slotted[i] else ins[i]
                if not slotted[i]:
                    cp = pltpu.make_async_copy(ins[i], outs[i].at[me], local_sems.at[i])
                    cp.start()
                    local.append(cp)
                for k, (px, py) in enumerate(chips):
                    if whole[i]:
                        cp = copy(own, outs[i].at[me], i, k, (px, py, mine))
                    elif k < 2:
                        cp = copy(half(own, mine), half(outs[i].at[me], mine), i, k, (px, py, mine))
                    else:
                        continue
                    cp.start()
                    sends.append(cp)
            via = mine
            for k in (via, 1 - via, 2):
                px, py = chips[k]
                slot = 2 * px + py
                source = (px, py, mine) if k < 2 else chips[1 - via] + (mine,)
                for i in range(n):
                    if whole[i]:
                        copy(outs[i].at[slot], outs[i].at[slot], i, k, (px, py, mine)).wait_recv()
                        continue
                    landed = half(outs[i].at[slot], mine)
                    copy(landed, landed, i, k, source).wait_recv()
                    if k == via:
                        cp = copy(landed, landed, i, 2, chips[1 - via] + (mine,))
                        cp.start()
                        sends.append(cp)
                    cp = copy(landed, landed, i, 3 + k, (x, y, other))
                    cp.start()
                    sends.append(cp)
            for k, (px, py) in enumerate(chips):
                slot = 2 * px + py
                for i in range(n):
                    if whole[i]:
                        continue
                    passed = half(outs[i].at[slot], other)
                    copy(passed, passed, i, 3 + k, (x, y, other)).wait_recv()
            for cp in sends:
                cp.wait_send()
            for cp in local:
                cp.wait()

        for mine in (0, 1):
            pl.when(c == mine)(functools.partial(run, mine))

    hbm = pl.BlockSpec(memory_space=pl.ANY)
    outs = pl.pallas_call(
        body, name=name, out_shape=out_shape,
        in_specs=[hbm] * n, out_specs=[hbm] * n,
        scratch_shapes=[pltpu.SemaphoreType.DMA((n, 6)), pltpu.SemaphoreType.DMA((n, 6)),
                        pltpu.SemaphoreType.DMA((n,))],
        input_output_aliases={i: i for i in range(n) if slotted[i]},
        compiler_params=pltpu.CompilerParams(has_side_effects=True),
    )(*arrs)
    return list(outs)


HBM_SPEC = pl.BlockSpec(memory_space=pltpu.HBM)
SEM_SPEC = pl.BlockSpec(memory_space=pltpu.SEMAPHORE)
DATAFLOW = pltpu.SideEffectType.DATAFLOW_SIDE_EFFECTING


def _split_peers(mode):
    x, y, c = lax.axis_index("x"), lax.axis_index("y"), lax.axis_index("c")
    if mode == "swap":
        return 0, [((x, y, 1 - c), 0)]
    return 2 * x + y, [((1 - x, y, c), 2 * (1 - x) + y), ((x, 1 - y, c), 2 * x + 1 - y),
                       ((1 - x, 1 - y, c), 2 * (1 - x) + 1 - y)]


def _split_refs(mode, src, landing, me, peer_slot):
    if mode == "gather":
        return src, landing.at[me]
    if mode == "scatter":
        return src.at[peer_slot], landing.at[me]
    return src, landing


def _plane_start(name, arrs, mode, after):
    n = len(arrs)
    n_peers = 1 if mode == "swap" else 3
    lands = [lax.empty(((4,) + a.shape) if mode == "gather" else a.shape, a.dtype) for a in arrs]

    def body(*refs):
        srcs, landing = refs[:n], refs[n:2 * n]
        send_sems, recv_sems = refs[2 * n + 1], refs[2 * n + 2]
        token = refs[-1]
        me, peers = _split_peers(mode)
        for i in range(n):
            for k, (peer, peer_slot) in enumerate(peers):
                src, dst = _split_refs(mode, srcs[i], landing[i], me, peer_slot)
                pltpu.make_async_remote_copy(
                    src_ref=src, dst_ref=dst, send_sem=send_sems.at[n_peers * i + k],
                    recv_sem=recv_sems.at[n_peers * i + k], device_id=peer, device_id_type=MESH).start()
        token[...] = jnp.zeros_like(token)

    hbm_in = [pltpu.with_memory_space_constraint(a, pltpu.HBM) for a in list(arrs) + lands]
    out = pl.pallas_call(
        body, name=name,
        out_shape=[pltpu.SemaphoreType.DMA((n_peers * n,)), pltpu.SemaphoreType.DMA((n_peers * n,))]
                  + [pltpu.HBM(a.shape, a.dtype) for a in lands]
                  + [jax.ShapeDtypeStruct((8, LANES), F32)],
        in_specs=[HBM_SPEC] * (2 * n) + [pl.BlockSpec(memory_space=pl.ANY)],
        out_specs=[SEM_SPEC, SEM_SPEC] + [HBM_SPEC] * n + [pl.BlockSpec(memory_space=pltpu.VMEM)],
        input_output_aliases={n + i: 2 + i for i in range(n)},
        compiler_params=pltpu.CompilerParams(has_side_effects=DATAFLOW),
    )(*hbm_in, after)
    return out[:-1], out[-1]


def _plane_wait(name, state, mode, after):
    send_sems, recv_sems = state[0], state[1]
    bufs = list(state[2:])
    n = len(bufs)
    n_peers = 1 if mode == "swap" else 3

    def body(*refs):
        landing = refs[:n]
        send_sems, recv_sems = refs[n], refs[n + 1]
        _, peers = _split_peers(mode)
        for i in range(n):
            for k, (peer, peer_slot) in enumerate(peers):
                arrived = landing[i] if mode == "swap" else landing[i].at[peer_slot]
                cp = pltpu.make_async_remote_copy(
                    src_ref=arrived, dst_ref=arrived, send_sem=send_sems.at[n_peers * i + k],
                    recv_sem=recv_sems.at[n_peers * i + k], device_id=peer, device_id_type=MESH)
                cp.wait_send()
                cp.wait_recv()

    out = pl.pallas_call(
        body, name=name,
        out_shape=[pltpu.HBM(a.shape, a.dtype) for a in bufs],
        in_specs=[HBM_SPEC] * n + [SEM_SPEC, SEM_SPEC, pl.BlockSpec(memory_space=pl.ANY)],
        out_specs=[HBM_SPEC] * n,
        input_output_aliases={i: i for i in range(n)},
        compiler_params=pltpu.CompilerParams(has_side_effects=DATAFLOW),
    )(*bufs, send_sems, recv_sems, after)
    return list(out)


def _tile_2d(rows, cols, row_mult, max_elems=512 * 1024):
    if rows % row_mult == 0:
        rt = _pick_tile(rows, max(row_mult, max_elems // cols), row_mult)
        return (rt, cols), rows // rt, lambda i: (i, 0)
    ct = _pick_tile(cols, max(LANES, max_elems // rows), LANES)
    return (rows, ct), cols // ct, lambda i: (0, i)


def _cast_bf16(a, name):
    block, steps, index = _tile_2d(a.shape[0], a.shape[1], 16)

    def body(a_ref, o_ref):
        o_ref[...] = a_ref[...].astype(BF16)

    return pl.pallas_call(
        body, name=name, grid=(steps,),
        in_specs=[pl.BlockSpec(block, index)],
        out_specs=pl.BlockSpec(block, index),
        out_shape=jax.ShapeDtypeStruct(a.shape, BF16),
        compiler_params=_params(("parallel",)),
    )(a)


def _cast_into_slot(a, slot, name):
    block, steps, index = _tile_2d(a.shape[0], a.shape[1], 16)

    def body(slot_ref, a_ref, o_ref):
        o_ref[...] = a_ref[...].astype(BF16)

    return pl.pallas_call(
        body, name=name,
        grid_spec=pltpu.PrefetchScalarGridSpec(
            num_scalar_prefetch=1, grid=(steps,),
            in_specs=[pl.BlockSpec(block, lambda i, s: index(i))],
            out_specs=pl.BlockSpec((None,) + block, lambda i, s: (s[0],) + index(i))),
        out_shape=jax.ShapeDtypeStruct((4,) + a.shape, BF16),
        compiler_params=_params(("arbitrary",)),
    )(slot, a)


def _sum_slots(buf, name, own=None, slot=None):
    n_slots, rows, cols = buf.shape
    (br, bc), steps, index = _tile_2d(rows, cols, 16, 320 * 1024)

    def body(*refs):
        if own is None:
            b_ref, o_ref = refs
        else:
            slot_ref, b_ref, own_ref, o_ref = refs
        acc = None
        for s in range(n_slots):
            term = b_ref[s] if own is None else jnp.where(slot_ref[0] == s, own_ref[...], b_ref[s])
            acc = term.astype(F32) if acc is None else acc + term.astype(F32)
        o_ref[...] = acc

    out_shape = jax.ShapeDtypeStruct((rows, cols), F32)
    if own is None:
        return pl.pallas_call(
            body, name=name, grid=(steps,),
            in_specs=[pl.BlockSpec((n_slots, br, bc), lambda i: (0,) + index(i))],
            out_specs=pl.BlockSpec((br, bc), index), out_shape=out_shape,
            compiler_params=_params(("parallel",), 48),
        )(buf)
    return pl.pallas_call(
        body, name=name,
        grid_spec=pltpu.PrefetchScalarGridSpec(
            num_scalar_prefetch=1, grid=(steps,),
            in_specs=[pl.BlockSpec((n_slots, br, bc), lambda i, s: (0,) + index(i)),
                      pl.BlockSpec((None, br, bc), lambda i, s: (s[0],) + index(i))],
            out_specs=pl.BlockSpec((br, bc), lambda i, s: index(i))),
        out_shape=out_shape,
        compiler_params=_params(("arbitrary",), 48),
    )(slot, buf, own)


def _sum_small(bufs, name):
    n = len(bufs)

    def body(*refs):
        for b_ref, o_ref in zip(refs[:n], refs[n:]):
            acc = b_ref[0]
            for s in range(1, b_ref.shape[0]):
                acc = acc + b_ref[s]
            o_ref[...] = acc

    vmem = pl.BlockSpec(memory_space=pltpu.VMEM)
    return pl.pallas_call(
        body, name=name, in_specs=[vmem] * n, out_specs=[vmem] * n,
        out_shape=[jax.ShapeDtypeStruct(b.shape[1:], b.dtype) for b in bufs],
    )(*bufs)


def _adam_update(w, g, m, v):
    c1 = 1.0 - ADAM_B1 ** ADAM_STEP
    c2 = 1.0 - ADAM_B2 ** ADAM_STEP
    m_new = ADAM_B1 * m + (1.0 - ADAM_B1) * g
    v_new = ADAM_B2 * v + (1.0 - ADAM_B2) * (g * g)
    m_hat = m_new / c1
    v_hat = v_new / c2
    return -ADAM_LR * (m_hat / (jnp.sqrt(v_hat) + ADAM_EPS) + ADAM_WD * w), m_new, v_new


def _adamw_small(params, slot, name):
    n = len(params)

    def spec_of(shape):
        lead = (None,) * (len(shape) - 2)
        return pl.BlockSpec(lead + tuple(shape[-2:]), lambda i, s, k=len(shape): (0,) * k)

    in_specs, operands, out_specs, out_shape = [], [], [], []
    for w, g, m, v in params:
        shard = g.shape[-1] != w.shape[-1]
        g_spec = pl.BlockSpec(tuple(w.shape[-2:]), (lambda i, s: (0, s[0])) if shard else (lambda i, s: (0, 0)))
        in_specs += [spec_of(w.shape), g_spec, spec_of(m.shape), spec_of(v.shape)]
        operands += [w, g, m, v]
        out_specs += [spec_of(w.shape)] * 4
        out_shape += [jax.ShapeDtypeStruct(w.shape, F32)] * 4

    def body(slot_ref, *refs):
        ins, outs = refs[:4 * n], refs[4 * n:]
        for p in range(n):
            w_ref, g_ref, m_ref, v_ref = ins[4 * p:4 * p + 4]
            g = g_ref[...]
            delta, m_new, v_new = _adam_update(w_ref[...], g, m_ref[...], v_ref[...])
            for o_ref, val in zip(outs[4 * p:4 * p + 4], (g, delta, m_new, v_new)):
                o_ref[...] = val

    out = pl.pallas_call(
        body, name=name,
        grid_spec=pltpu.PrefetchScalarGridSpec(num_scalar_prefetch=1, grid=(1,), in_specs=in_specs, out_specs=out_specs),
        out_shape=out_shape,
    )(slot, *operands)
    return [tuple(out[4 * p:4 * p + 4]) for p in range(n)]


def _adamw(w, grads, m, v, name, grad_row=0):
    rows, cols = w.shape
    (rt, _), _, _ = _tile_2d(rows, cols, 8, 160 * 1024)
    assert grad_row % rt == 0
    n_g = len(grads)

    def body(*refs):
        w_ref = refs[0]
        g_refs = refs[1:1 + n_g]
        m_ref, v_ref, g_out, d_out, m_out, v_out = refs[1 + n_g:]
        g = g_refs[0][...]
        for r in g_refs[1:]:
            g = g + r[...]
        g_out[...] = g
        d_out[...], m_out[...], v_out[...] = _adam_update(w_ref[...], g, m_ref[...], v_ref[...])

    spec = pl.BlockSpec((rt, cols), lambda i: (i, 0))
    grad_spec = pl.BlockSpec((rt, cols), lambda i: (i + grad_row // rt, 0))
    shape = jax.ShapeDtypeStruct((rows, cols), F32)
    return pl.pallas_call(
        body, name=name, grid=(rows // rt,),
        in_specs=[spec] + [grad_spec] * n_g + [spec] * 2, out_specs=[spec] * 4, out_shape=[shape] * 4,
        compiler_params=_params(("parallel",), 48),
    )(w, *grads, m, v)


def _weight_pieces():
    pieces = []
    for j in range(N_CONV_TILES):
        for g in range(4):
            pieces.append((512 * j + 128 * g, D * g + 128 * j, 128))
    for hd in range(N_HEADS):
        base = W_CONV + 512 * hd
        pieces.append((base, OFF_Q + HEAD_K * hd, HEAD_K))
        pieces.append((base + HEAD_K, OFF_K + HEAD_K * hd, HEAD_K))
        pieces.append((base + 2 * HEAD_K, OFF_V + HEAD_V * hd, HEAD_V))
    pieces.append((W_CONV + W_GLA, OFF_R, D))
    pieces.append((W_CONV + W_GLA + D, OFF_MA, 2 * D))
    return pieces


N_WEIGHT_COPIES = len(_weight_pieces()) + 1


def _load_weights(w_hbm, w_s, wlr_s, sems):
    copies = [pltpu.make_async_copy(w_hbm.at[pl.ds(src, n)], w_s.at[pl.ds(dst, n)], sems.at[i])
              for i, (dst, src, n) in enumerate(_weight_pieces())]
    copies.append(pltpu.make_async_copy(w_hbm.at[pl.ds(OFF_LR, LANES)], wlr_s, sems.at[N_WEIGHT_COPIES - 1]))
    for cp in copies:
        cp.start()
    for cp in copies:
        cp.wait()


def _in_proj(h, g_pre, w_full_t):
    t_rows = h.shape[0]
    tm = _pick_tile(t_rows, 384, LANES)
    n_main = N_MAIN

    def body(h_ref, g_ref, w_hbm, proj_ref, ut_ref, lr_ref, w_s, wlr_s, w_sems):
        @pl.when(pl.program_id(0) == 0)
        def _():
            _load_weights(w_hbm, w_s, wlr_s, w_sems)

        hh = h_ref[...]
        rstd = lax.rsqrt(jnp.mean(hh * hh, axis=-1, keepdims=True) + EPS)
        uf = hh * rstd * g_ref[...]
        u = uf.astype(BF16)
        ut_ref[...] = jnp.transpose(uf).astype(BF16)
        lr_ref[...] = _dot_nt(u, wlr_s[...])
        for j in range(n_main // D):
            cols = slice(j * D, (j + 1) * D)
            proj_ref[:, cols] = _dot_nt(u, w_s[cols, :]).astype(BF16)

    return pl.pallas_call(
        body, name="in_proj", grid=(t_rows // tm,),
        in_specs=[pl.BlockSpec((tm, D), lambda i: (i, 0)),
                  pl.BlockSpec((1, D), lambda i: (0, 0)),
                  pl.BlockSpec(memory_space=pl.ANY)],
        out_specs=[pl.BlockSpec((tm, n_main), lambda i: (i, 0)),
                   pl.BlockSpec((D, tm), lambda i: (0, i)),
                   pl.BlockSpec((tm, LANES), lambda i: (i, 0))],
        out_shape=[jax.ShapeDtypeStruct((t_rows, n_main), BF16),
                   jax.ShapeDtypeStruct((D, t_rows), BF16),
                   jax.ShapeDtypeStruct((t_rows, LANES), F32)],
        scratch_shapes=[pltpu.VMEM((n_main, D), BF16), pltpu.VMEM((LANES, D), BF16),
                        pltpu.SemaphoreType.DMA((N_WEIGHT_COPIES,))],
        compiler_params=_params(("arbitrary",), 56),
    )(h, g_pre, w_full_t)


def _conv_parts(p_ref, w_ref):
    cb = p_ref[:, 0:128].astype(F32)
    cc = p_ref[:, 128:256].astype(F32)
    cx = p_ref[:, 256:384].astype(F32)
    cz = p_ref[:, 384:512].astype(F32)
    rows = cb.shape[0]
    w = w_ref[...]
    p = cc * cx
    conv = pltpu.roll(p, 1, 0) * w[0:1] + p * w[1:2] + pltpu.roll(p, rows - 1, 0) * w[2:3]
    sz = _sigmoid(cz)
    return cb, cc, cx, cz, p, conv, sz, w


def _conv_fwd(proj, conv_w, n_seq, lf):
    def body(p_ref, w_ref, y_ref):
        cb, _, _, cz, _, conv, sz, _ = _conv_parts(p_ref, w_ref)
        y_ref[...] = (cb * conv * (cz * sz)).astype(BF16)

    return pl.pallas_call(
        body, name="conv_fwd", grid=(n_seq, N_CONV_TILES),
        in_specs=[pl.BlockSpec((lf, 512), lambda b, j: (b, j)),
                  pl.BlockSpec((3, 128), lambda b, j: (0, j))],
        out_specs=pl.BlockSpec((lf, 128), lambda b, j: (b, j)),
        out_shape=jax.ShapeDtypeStruct((n_seq * lf, D), BF16),
        compiler_params=_params(("parallel", "parallel"), 48),
    )(proj, conv_w)


def _conv_bwd(proj, conv_w, dyc, n_seq, lf):
    def body(p_ref, w_ref, dy_ref, dp_ref, dw_ref):
        cb, cc, cx, cz, p, conv, sz, w = _conv_parts(p_ref, w_ref)
        rows = cb.shape[0]
        dy = dy_ref[...].astype(F32)
        silu = cz * sz
        dcb = dy * conv * silu
        dconv = dy * cb * silu
        dcz = dy * cb * conv * (sz * (1.0 + cz * (1.0 - sz)))
        d_next = pltpu.roll(dconv, rows - 1, 0)
        d_prev = pltpu.roll(dconv, 1, 0)
        dp = d_next * w[0:1] + dconv * w[1:2] + d_prev * w[2:3]
        dp_ref[:, 0:128] = dcb.astype(BF16)
        dp_ref[:, 128:256] = (dp * cx).astype(BF16)
        dp_ref[:, 256:384] = (dp * cc).astype(BF16)
        dp_ref[:, 384:512] = dcz.astype(BF16)
        dw_ref[0:1, :] = jnp.sum(dconv * pltpu.roll(p, 1, 0), axis=0, keepdims=True)
        dw_ref[1:2, :] = jnp.sum(dconv * p, axis=0, keepdims=True)
        dw_ref[2:3, :] = jnp.sum(dconv * pltpu.roll(p, rows - 1, 0), axis=0, keepdims=True)

    return pl.pallas_call(
        body, name="conv_bwd", grid=(n_seq, N_CONV_TILES),
        in_specs=[pl.BlockSpec((lf, 512), lambda b, j: (b, j)),
                  pl.BlockSpec((3, 128), lambda b, j: (0, j)),
                  pl.BlockSpec((lf, 128), lambda b, j: (b, j))],
        out_specs=[pl.BlockSpec((lf, 512), lambda b, j: (b, j)),
                   pl.BlockSpec((None, 3, 128), lambda b, j: (b, 0, j))],
        out_shape=[jax.ShapeDtypeStruct((n_seq * lf, W_CONV), BF16),
                   jax.ShapeDtypeStruct((n_seq, 3, D), F32)],
        compiler_params=_params(("parallel", "parallel"), 48),
    )(proj, conv_w, dyc)


GROUP = 3
GROUP_ROWS = GROUP * CHUNK


def _row_group(shape):
    row = lax.broadcasted_iota(jnp.int32, shape, 0)
    grp = jnp.zeros(shape, jnp.int32)
    for r in range(1, GROUP):
        grp = grp + (row >= r * CHUNK).astype(jnp.int32)
    return grp


def _lane_group(shape, width):
    lane = lax.broadcasted_iota(jnp.int32, shape, 1)
    grp = jnp.zeros(shape, jnp.int32)
    for r in range(1, GROUP):
        grp = grp + (lane >= r * width).astype(jnp.int32)
    return grp


def _score_mask(direction):
    shape = (GROUP_ROWS, GROUP_ROWS)
    row = lax.broadcasted_iota(jnp.int32, shape, 0)
    col = lax.broadcasted_iota(jnp.int32, shape, 1)
    same = _row_group(shape) == _lane_group(shape, CHUNK)
    return same & ((col <= row) if direction == 0 else (col > row))


def _diag_blocks(v):
    w = v.shape[1]
    wide = jnp.concatenate([v] * GROUP, axis=1)
    return jnp.where(_row_group(wide.shape) == _lane_group(wide.shape, w), wide, jnp.zeros_like(wide))


def _per_chunk_dot(lhs, state, transposed):
    outs = []
    for r in range(GROUP):
        rows = lhs[r * CHUNK:(r + 1) * CHUNK, :]
        blk = state[:, r * HEAD_K:(r + 1) * HEAD_K]
        outs.append(_dot_nt(rows, blk) if transposed else _dot(rows, blk))
    return jnp.concatenate(outs, axis=0)


def _chunk_cumsum(v, suffix):
    pos = lax.broadcasted_iota(jnp.int32, v.shape, 0) & (CHUNK - 1)
    shift = 1
    while shift < CHUNK:
        if suffix:
            moved = pltpu.roll(v, GROUP_ROWS - shift, 0)
            v = v + jnp.where(pos < CHUNK - shift, moved, 0.0)
        else:
            moved = pltpu.roll(v, shift, 0)
            v = v + jnp.where(pos >= shift, moved, 0.0)
        shift *= 2
    return v


def _per_chunk_rows(rows_of_chunk):
    w = rows_of_chunk[0].shape[1]
    return jnp.concatenate([jnp.broadcast_to(v, (CHUNK, w)) for v in rows_of_chunk], axis=0)


def _chunk_end_rows(direction, b):
    at = CHUNK - 1 if direction == 0 else 0
    return [b[r * CHUNK + at:r * CHUNK + at + 1, :] for r in range(GROUP)]


def _gla_gates(lr_bf, wg_ref, bg_ref, lf):
    z = _dot(lr_bf, wg_ref[...]) + bg_ref[...]
    valid = lax.broadcasted_iota(jnp.int32, (lf, HEAD_K), 0) >= PAD_FRONT
    return z, valid


def _group_unroll(n_groups):
    return n_groups if n_groups <= 11 else 1


def _group_rows(g):
    return pl.ds(pl.multiple_of(g * GROUP_ROWS, GROUP_ROWS), GROUP_ROWS)


def _chunk_decay(direction, g, r, b_s):
    base = g * GROUP_ROWS + r * CHUNK
    if direction == 0:
        grp = b_s[pl.ds(pl.multiple_of(base + CHUNK - 8, 8), 8), :]
        return jnp.exp(grp[7:8, :])
    grp = b_s[pl.ds(pl.multiple_of(base, 8), 8), :]
    return jnp.exp(grp[0:1, :])


def _state_scan(direction, n_groups, b_s, st_s, reverse):
    ascending = (direction == 0) != reverse

    def step(i, carry):
        g = i if ascending else n_groups - 1 - i
        for rr in range(GROUP):
            r = rr if ascending else GROUP - 1 - rr
            lanes = slice(r * HEAD_K, (r + 1) * HEAD_K)
            decay = _chunk_decay(direction, g, r, b_s)
            local = st_s[g, :, lanes]
            st_s[g, :, lanes] = carry
            carry = (local + carry * decay) if reverse else (carry * decay + local)
        return carry

    lax.fori_loop(0, n_groups, step, jnp.zeros((HEAD_V, HEAD_K), F32), unroll=_group_unroll(n_groups))


def _gla_states(direction, n_groups, qkv_ref, g_s, b_s, st_s):
    def local(g, carry):
        rows = _group_rows(g)
        b = _chunk_cumsum(g_s[rows, :], direction == 1)
        b_s[rows, :] = b
        b_end = _per_chunk_rows(_chunk_end_rows(direction, b))
        k = qkv_ref[rows, 128:256].astype(F32)
        v = qkv_ref[rows, 256:512]
        k_dec = (k * jnp.exp(b_end - b)).astype(BF16)
        st_s[g] = _dot_tn(v, _diag_blocks(k_dec))
        return carry

    lax.fori_loop(0, n_groups, local, 0, unroll=_group_unroll(n_groups))
    _state_scan(direction, n_groups, b_s, st_s, False)


def _gla_fwd(proj, lr, wgf, wgb, bgf, bgb, n_seq, lf):
    assert lf % GROUP_ROWS == 0
    n_groups = lf // GROUP_ROWS
    scale = HEAD_K ** -0.5

    def body(qkv_ref, lr_ref, wgf_ref, wgb_ref, bgf_ref, bgb_ref, o_ref, g_s, b_s2, st_s2):
        lr_bf = lr_ref[...].astype(BF16)
        for direction in (0, 1):
            wg_ref, bg_ref = ((wgf_ref, bgf_ref), (wgb_ref, bgb_ref))[direction]
            z, valid = _gla_gates(lr_bf, wg_ref, bg_ref, lf)
            g_s[...] = jnp.where(valid, _log_sigmoid(z) / GATE_NORM, 0.0)
            _gla_states(direction, n_groups, qkv_ref, g_s, b_s2.at[direction], st_s2.at[direction])
        masks = [_score_mask(0), _score_mask(1)]

        def out(g, carry):
            rows = _group_rows(g)
            q = qkv_ref[rows, 0:128].astype(F32) * scale
            k = qkv_ref[rows, 128:256].astype(F32)
            v = qkv_ref[rows, 256:512]
            o = None
            for direction in (0, 1):
                b = b_s2[direction, rows, :]
                q_in = (q * jnp.exp(b)).astype(BF16)
                k_in = (k * jnp.exp(-b)).astype(BF16)
                s = jnp.where(masks[direction], _dot_nt(q_in, k_in), 0.0).astype(BF16)
                part = _dot(s, v) + _per_chunk_dot(q_in, st_s2[direction, g].astype(BF16), True)
                o = part if o is None else o + part
            o_ref[rows, :] = o
            return carry

        lax.fori_loop(0, n_groups, out, 0, unroll=_group_unroll(n_groups))

    return pl.pallas_call(
        body, name="gla_fwd", grid=(n_seq, N_HEADS),
        in_specs=[pl.BlockSpec((lf, 512), lambda b, h: (b, N_CONV_TILES + h)),
                  pl.BlockSpec((lf, LANES), lambda b, h: (b, 0)),
                  pl.BlockSpec((None, LANES, HEAD_K), lambda b, h: (h, 0, 0)),
                  pl.BlockSpec((None, LANES, HEAD_K), lambda b, h: (h, 0, 0)),
                  pl.BlockSpec((None, 1, HEAD_K), lambda b, h: (h, 0, 0)),
                  pl.BlockSpec((None, 1, HEAD_K), lambda b, h: (h, 0, 0))],
        out_specs=pl.BlockSpec((lf, HEAD_V), lambda b, h: (b, h)),
        out_shape=jax.ShapeDtypeStruct((n_seq * lf, D), F32),
        scratch_shapes=[pltpu.VMEM((lf, HEAD_K), F32), pltpu.VMEM((2, lf, HEAD_K), F32),
                        pltpu.VMEM((2, n_groups, HEAD_V, GROUP * HEAD_K), F32)],
        compiler_params=_params(("parallel", "parallel"), 48),
    )(proj, lr, wgf, wgb, bgf, bgb)


def _gla_bwd(proj, lr, d_o, wgf, wgb, bgf, bgb, n_seq, lf, token):
    assert lf % GROUP_ROWS == 0
    n_groups = lf // GROUP_ROWS
    scale = HEAD_K ** -0.5

    def body(qkv_ref, lr_ref, do_ref, wgf_ref, wgb_ref, bgf_ref, bgb_ref, token_ref,
             dqkv_ref, dlr_ref, dwgf_ref, dwgb_ref, dbg_ref,
             g_s, b_s2, fac_s2, dg_s2, st_s2, dst_s2):
        lr_bf = lr_ref[...].astype(BF16)
        gates = ((wgf_ref, bgf_ref), (wgb_ref, bgb_ref))
        for direction in (0, 1):
            wg_ref, bg_ref = gates[direction]
            b_s, st_s, dst_s = b_s2.at[direction], st_s2.at[direction], dst_s2.at[direction]
            z, valid = _gla_gates(lr_bf, wg_ref, bg_ref, lf)
            g_s[...] = jnp.where(valid, _log_sigmoid(z) / GATE_NORM, 0.0)
            fac_s2[direction] = jnp.where(valid, _sigmoid(-z) / GATE_NORM, 0.0)
            _gla_states(direction, n_groups, qkv_ref, g_s, b_s, st_s)

            def state_grad_local(g, carry):
                rows = _group_rows(g)
                q = qkv_ref[rows, 0:128].astype(F32) * scale
                q_in = (q * jnp.exp(b_s[rows, :])).astype(BF16)
                dst_s[g] = _dot_tn(do_ref[rows, :], _diag_blocks(q_in))
                return carry

            lax.fori_loop(0, n_groups, state_grad_local, 0, unroll=_group_unroll(n_groups))
            _state_scan(direction, n_groups, b_s, dst_s, True)

        masks = [_score_mask(0), _score_mask(1)]

        def group_grads(g, carry):
            rows = _group_rows(g)
            q = qkv_ref[rows, 0:128].astype(F32) * scale
            k = qkv_ref[rows, 128:256].astype(F32)
            v = qkv_ref[rows, 256:512]
            d_out = do_ref[rows, :]
            dq_sum = dk_sum = dv_sum = None
            for direction in (0, 1):
                end_row = CHUNK - 1 if direction == 0 else 0
                b = b_s2[direction, rows, :]
                ends = _chunk_end_rows(direction, b)
                b_end = _per_chunk_rows(ends)
                e_pos = jnp.exp(b)
                e_neg = jnp.exp(-b)
                e_end = jnp.exp(b_end - b)
                q_in = q * e_pos
                k_in = k * e_neg
                k_dec = k * e_end
                q_in_bf = q_in.astype(BF16)
                k_in_bf = k_in.astype(BF16)
                state = st_s2[direction, g]
                d_state = dst_s2[direction, g]
                state_bf = state.astype(BF16)
                d_state_bf = d_state.astype(BF16)
                s = jnp.where(masks[direction], _dot_nt(q_in_bf, k_in_bf), 0.0).astype(BF16)
                ds = jnp.where(masks[direction], _dot_nt(d_out, v), 0.0).astype(BF16)
                dv = _dot_tn(s, d_out) + _per_chunk_dot(k_dec.astype(BF16), d_state_bf, True)
                dq_in = _dot(ds, k_in_bf) + _per_chunk_dot(d_out, state_bf, False)
                dk_in = _dot_tn(ds, q_in_bf)
                dk_dec = _per_chunk_dot(v, d_state_bf, False)
                dq = dq_in * e_pos * scale
                dk = dk_in * e_neg + dk_dec * e_end
                dq_sum = dq if dq_sum is None else dq_sum + dq
                dk_sum = dk if dk_sum is None else dk_sum + dk
                dv_sum = dv if dv_sum is None else dv_sum + dv
                dkk = dk_dec * k_dec
                db = dq_in * q_in - dk_in * k_in - dkk
                d_decay = jnp.sum(d_state * state, axis=0, keepdims=True)
                db_end = [jnp.sum(dkk[r * CHUNK:(r + 1) * CHUNK, :], axis=0, keepdims=True)
                          + d_decay[:, r * HEAD_K:(r + 1) * HEAD_K] * jnp.exp(ends[r]) for r in range(GROUP)]
                row = lax.broadcasted_iota(jnp.int32, (GROUP_ROWS, HEAD_K), 0)
                at_end = row == end_row
                for r in range(1, GROUP):
                    at_end = at_end | (row == r * CHUNK + end_row)
                db = db + jnp.where(at_end, _per_chunk_rows(db_end), 0.0)
                dg_s2[direction, rows, :] = _chunk_cumsum(db, direction == 0)
            dqkv_ref[rows, 0:128] = dq_sum.astype(BF16)
            dqkv_ref[rows, 128:256] = dk_sum.astype(BF16)
            dqkv_ref[rows, 256:512] = dv_sum.astype(BF16)
            return carry

        lax.fori_loop(0, n_groups, group_grads, 0, unroll=_group_unroll(n_groups))

        dlr = jnp.zeros((lf, LANES), F32)
        for direction in (0, 1):
            dz = dg_s2[direction] * fac_s2[direction]
            dz_bf = dz.astype(BF16)
            dbg_ref[direction:direction + 1, :] = jnp.sum(dz, axis=0, keepdims=True)
            (dwgf_ref, dwgb_ref)[direction][...] = _dot_tn(lr_bf, dz_bf)
            dlr = dlr + _dot_nt(dz_bf, gates[direction][0][...])

        @pl.when(pl.program_id(1) == 0)
        def _():
            dlr_ref[...] = dlr

        @pl.when(pl.program_id(1) != 0)
        def _():
            dlr_ref[...] = dlr_ref[...] + dlr

    gate_w = pl.BlockSpec((None, LANES, HEAD_K), lambda b, h: (h, 0, 0))
    gate_b = pl.BlockSpec((None, 1, HEAD_K), lambda b, h: (h, 0, 0))
    return pl.pallas_call(
        body, name="gla_bwd", grid=(n_seq, N_HEADS),
        in_specs=[pl.BlockSpec((lf, 512), lambda b, h: (b, N_CONV_TILES + h)),
                  pl.BlockSpec((lf, LANES), lambda b, h: (b, 0)),
                  pl.BlockSpec((lf, HEAD_V), lambda b, h: (b, h)),
                  gate_w, gate_w, gate_b, gate_b,
                  pl.BlockSpec((8, LANES), lambda b, h: (0, 0))],
        out_specs=[pl.BlockSpec((lf, 512), lambda b, h: (b, h)),
                   pl.BlockSpec((lf, LANES), lambda b, h: (b, 0)),
                   pl.BlockSpec((None, None, LANES, HEAD_K), lambda b, h: (b, h, 0, 0)),
                   pl.BlockSpec((None, None, LANES, HEAD_K), lambda b, h: (b, h, 0, 0)),
                   pl.BlockSpec((None, None, 2, HEAD_K), lambda b, h: (b, h, 0, 0))],
        out_shape=[jax.ShapeDtypeStruct((n_seq * lf, W_GLA), BF16),
                   jax.ShapeDtypeStruct((n_seq * lf, LANES), F32),
                   jax.ShapeDtypeStruct((n_seq, N_HEADS, LANES, HEAD_K), F32),
                   jax.ShapeDtypeStruct((n_seq, N_HEADS, LANES, HEAD_K), F32),
                   jax.ShapeDtypeStruct((n_seq, N_HEADS, 2, HEAD_K), F32)],
        scratch_shapes=[pltpu.VMEM((lf, HEAD_K), F32), pltpu.VMEM((2, lf, HEAD_K), F32),
                        pltpu.VMEM((2, lf, HEAD_K), F32), pltpu.VMEM((2, lf, HEAD_K), F32),
                        pltpu.VMEM((2, n_groups, HEAD_V, GROUP * HEAD_K), F32),
                        pltpu.VMEM((2, n_groups, HEAD_V, GROUP * HEAD_K), F32)],
        compiler_params=_params(("parallel", "arbitrary"), 56),
    )(proj, lr, d_o, wgf, wgb, bgf, bgb, token)


def _tail(h, tgt, yc, o, proj, w3, gamma, g_post, lf):
    t_rows = h.shape[0]
    tm = _pick_tile(t_rows, 256, CHUNK)
    n_chunks = lf // CHUNK
    per_tile = tm // CHUNK

    def body(h_ref, tgt_ref, yc_ref, o_ref, r_ref, ma_ref, mb_ref, w_hbm, gamma_ref, gpost_ref,
             dres_ref, yg_ref, merged_ref, dout_ref, dpc_ref, dpg_ref, dyc_ref, do_ref, dtail_ref,
             loss_ref, dgpost_ref, dgamma_ref, w_s, w_sem):
        i = pl.program_id(0)

        @pl.when(i == 0)
        def _():
            cp = pltpu.make_async_copy(w_hbm, w_s, w_sem)
            cp.start()
            cp.wait()
            loss_ref[...] = jnp.zeros_like(loss_ref)
            dgpost_ref[...] = jnp.zeros_like(dgpost_ref)
            dgamma_ref[...] = jnp.zeros_like(dgamma_ref)

        gamma = gamma_ref[...]
        o = o_ref[...]
        r = r_ref[...].astype(F32)
        sr = _sigmoid(r)
        silu_r = r * sr
        n_parts, rstd_parts = [], []
        for hd in range(N_HEADS):
            oh = o[:, hd * HEAD_V:(hd + 1) * HEAD_V]
            rstd = lax.rsqrt(jnp.mean(oh * oh, axis=-1, keepdims=True) + EPS)
            n_parts.append(oh * rstd)
            rstd_parts.append(rstd)
        n = jnp.concatenate(n_parts, axis=-1)
        gamma_t = jnp.concatenate([gamma] * N_HEADS, axis=-1)
        yg = n * gamma_t * silu_r
        yg_bf = yg.astype(BF16)
        yg_ref[...] = yg_bf
        yc = yc_ref[...]
        pc = _dot(yc, w_s[0])
        pg = _dot(yg_bf, w_s[1])
        sa = _sigmoid(ma_ref[...].astype(F32))
        sb = _sigmoid(mb_ref[...].astype(F32))
        merged = (sa * pc + sb * pg).astype(BF16)
        merged_ref[...] = merged
        out = _dot(merged, w_s[2])
        rstd2 = lax.rsqrt(jnp.mean(out * out, axis=-1, keepdims=True) + EPS)
        nn = out * rstd2
        gpost = gpost_ref[...]
        y = h_ref[...] + nn * gpost

        rowi = lax.broadcasted_iota(jnp.int32, (tm, 1), 0)
        keep = jnp.zeros((tm, 1), F32)
        for kk in range(per_tile):
            is_tok = ((i * per_tile + kk) % n_chunks) != 0
            f = jnp.where(is_tok, 1.0, 0.0)
            keep = jnp.where((rowi >= kk * CHUNK) & (rowi < (kk + 1) * CHUNK), f, keep)
        diff = (y - tgt_ref[...]) * keep
        loss_ref[...] += jnp.sum(diff * diff) * (0.5 / D)
        dy = diff * (1.0 / D)
        dres_ref[...] = dy
        dgpost_ref[...] += jnp.sum(dy * nn, axis=0, keepdims=True)
        dn = dy * gpost
        dout_f = rstd2 * (dn - nn * jnp.mean(dn * nn, axis=-1, keepdims=True))
        dout = dout_f.astype(BF16)
        dout_ref[...] = jnp.transpose(dout_f).astype(BF16)
        dmerged = _dot_nt(dout, w_s[2])
        dpc_f = dmerged * sa
        dpg_f = dmerged * sb
        dpc = dpc_f.astype(BF16)
        dpg = dpg_f.astype(BF16)
        dpc_ref[...] = jnp.transpose(dpc_f).astype(BF16)
        dpg_ref[...] = jnp.transpose(dpg_f).astype(BF16)
        dtail_ref[:, D:2 * D] = (dmerged * pc * (sa * (1.0 - sa))).astype(BF16)
        dtail_ref[:, 2 * D:3 * D] = (dmerged * pg * (sb * (1.0 - sb))).astype(BF16)
        dyc_ref[...] = _dot_nt(dpc, w_s[0]).astype(BF16)
        dyg = _dot_nt(dpg, w_s[1])
        dtail_ref[:, 0:D] = (dyg * n * gamma_t * (sr * (1.0 + r * (1.0 - sr)))).astype(BF16)
        dgam_full = jnp.sum(dyg * n * silu_r, axis=0, keepdims=True)
        dgam = dgam_full[:, 0:HEAD_V]
        for hd in range(1, N_HEADS):
            dgam = dgam + dgam_full[:, hd * HEAD_V:(hd + 1) * HEAD_V]
        dgamma_ref[...] += dgam
        dng = dyg * gamma_t * silu_r
        do_parts = []
        for hd in range(N_HEADS):
            sl = slice(hd * HEAD_V, (hd + 1) * HEAD_V)
            dnh = dng[:, sl]
            nh = n_parts[hd]
            do_parts.append(rstd_parts[hd] * (dnh - nh * jnp.mean(dnh * nh, axis=-1, keepdims=True)))
        do_ref[...] = jnp.concatenate(do_parts, axis=-1).astype(BF16)

    row = lambda c: pl.BlockSpec((tm, D), lambda i: (i, c))
    col = pl.BlockSpec((D, tm), lambda i: (0, i))
    const = lambda shape: pl.BlockSpec(shape, lambda i: (0, 0))
    act = jax.ShapeDtypeStruct((t_rows, D), BF16)
    act_t = jax.ShapeDtypeStruct((D, t_rows), BF16)
    return pl.pallas_call(
        body, name="tail", grid=(t_rows // tm,),
        in_specs=[row(0), row(0), row(0), row(0), row(6), row(7), row(8),
                  pl.BlockSpec(memory_space=pl.ANY), const((1, HEAD_V)), const((1, D))],
        out_specs=[row(0)] * 3 + [col] * 3 + [row(0)] * 2
                  + [pl.BlockSpec((tm, W_TAIL), lambda i: (i, 0)),
                     const((8, LANES)), const((1, D)), const((1, HEAD_V))],
        out_shape=[jax.ShapeDtypeStruct((t_rows, D), F32)] + [act] * 2 + [act_t] * 3 + [act] * 2
                  + [jax.ShapeDtypeStruct((t_rows, W_TAIL), BF16),
                     jax.ShapeDtypeStruct((8, LANES), F32),
                     jax.ShapeDtypeStruct((1, D), F32),
                     jax.ShapeDtypeStruct((1, HEAD_V), F32)],
        scratch_shapes=[pltpu.VMEM((3, D, D), BF16), pltpu.SemaphoreType.DMA],
        compiler_params=_params(("arbitrary",), 56),
    )(h, tgt, yc, o, proj, proj, proj, w3, gamma, g_post)


def _wgrad_t(a_t, b, name, out_dtype=BF16):
    m, t_rows = a_t.shape
    n = b.shape[1]
    tn = D if n % D == 0 else n
    tk = _pick_tile(t_rows, 768, LANES)
    n_k = t_rows // tk

    def body(a_ref, b_ref, o_ref, acc):
        k = pl.program_id(1)

        @pl.when(k == 0)
        def _():
            acc[...] = jnp.zeros_like(acc)

        acc[...] += _dot(a_ref[...], b_ref[...].astype(BF16))

        @pl.when(k == n_k - 1)
        def _():
            o_ref[...] = jnp.transpose(acc[...]).astype(out_dtype)

    return pl.pallas_call(
        body, name=name, grid=(n // tn, n_k),
        in_specs=[pl.BlockSpec((m, tk), lambda j, k: (0, k)),
                  pl.BlockSpec((tk, tn), lambda j, k: (k, j))],
        out_specs=pl.BlockSpec((tn, m), lambda j, k: (j, 0)),
        out_shape=jax.ShapeDtypeStruct((n, m), out_dtype),
        scratch_shapes=[pltpu.VMEM((m, tn), F32)],
        compiler_params=_params(("parallel", "arbitrary"), 48),
    )(a_t, b)


def _dgrad_in(dpc, dpg, dpt, dlr, w_full_t, h, g_pre, dres, token):
    t_rows = h.shape[0]
    tm = _pick_tile(t_rows, 384, 16)
    n_main = N_MAIN

    def body(dpc_ref, dpg_ref, dpt_ref, dlr_ref, w_hbm, h_ref, g_ref, dres_ref, token_ref,
             dh_ref, dg_ref, w_s, wlr_s, w_sems):
        @pl.when(pl.program_id(0) == 0)
        def _():
            _load_weights(w_hbm, w_s, wlr_s, w_sems)
            dg_ref[...] = jnp.zeros_like(dg_ref)

        du = _dot(dlr_ref[...].astype(BF16), wlr_s[...])
        du += _dot(dpc_ref[...], w_s[0:W_CONV, :])
        du += _dot(dpg_ref[...], w_s[W_CONV:W_CONV + W_GLA, :])
        du += _dot(dpt_ref[...], w_s[W_CONV + W_GLA:n_main, :])
        hh = h_ref[...]
        rstd = lax.rsqrt(jnp.mean(hh * hh, axis=-1, keepdims=True) + EPS)
        xhat = hh * rstd
        dg_ref[...] += jnp.sum(du * xhat, axis=0, keepdims=True)
        dx = du * g_ref[...]
        dh_ref[...] = rstd * (dx - xhat * jnp.mean(dx * xhat, axis=-1, keepdims=True)) + dres_ref[...]

    row = lambda width: pl.BlockSpec((tm, width), lambda i: (i, 0))
    return pl.pallas_call(
        body, name="dgrad_in", grid=(t_rows // tm,),
        in_specs=[row(W_CONV), row(W_GLA), row(W_TAIL), row(LANES),
                  pl.BlockSpec(memory_space=pl.ANY),
                  row(D), pl.BlockSpec((1, D), lambda i: (0, 0)), row(D),
                  pl.BlockSpec((8, LANES), lambda i: (0, 0))],
        out_specs=[row(D), pl.BlockSpec((1, D), lambda i: (0, 0))],
        out_shape=[jax.ShapeDtypeStruct((t_rows, D), F32), jax.ShapeDtypeStruct((1, D), F32)],
        scratch_shapes=[pltpu.VMEM((n_main, D), BF16), pltpu.VMEM((LANES, D), BF16),
                        pltpu.SemaphoreType.DMA((N_WEIGHT_COPIES,))],
        compiler_params=_params(("arbitrary",), 56),
    )(dpc, dpg, dpt, dlr, w_full_t, h, g_pre, dres, token)


def _reference_rows(g_conv, g_gla, g_tail, g_lr):
    conv = g_conv.reshape(N_CONV_TILES, 4, 128, D).transpose(1, 0, 2, 3).reshape(W_CONV, D)
    gla = g_gla.reshape(N_HEADS, 512, D)
    q = gla[:, 0:128].reshape(N_HEADS * HEAD_K, D)
    k = gla[:, 128:256].reshape(N_HEADS * HEAD_K, D)
    v = gla[:, 256:512].reshape(N_HEADS * HEAD_V, D)
    return jnp.concatenate([conv, q, k, v, g_tail[0:D], g_lr[0:2 * RANK], g_tail[D:3 * D]], axis=0)


def kernel(x, meta_tokens, norm_pre, w_in, conv_w, w_gate_fwd, b_gate_fwd, w_gate_bwd, b_gate_bwd, gla_norm, w_out_conv, w_out_gla, w_merge_out, norm_post, loss_target, m_meta_tokens, m_norm_pre, m_w_in, m_conv_w, m_w_gate_fwd, m_b_gate_fwd, m_w_gate_bwd, m_b_gate_bwd, m_gla_norm, m_w_out_conv, m_w_out_gla, m_w_merge_out, m_norm_post, v_meta_tokens, v_norm_pre, v_w_in, v_conv_w, v_w_gate_fwd, v_b_gate_fwd, v_w_gate_bwd, v_b_gate_bwd, v_gla_norm, v_w_out_conv, v_w_out_gla, v_w_merge_out, v_norm_post):
    n_seq, seq, _ = x.shape
    lf = CHUNK + seq
    t_rows = n_seq * lf
    shard = 2 * lax.axis_index("x") + lax.axis_index("y")
    shard_arr = jnp.reshape(shard, (1,)).astype(jnp.int32)

    w_in_slots = _cast_into_slot(jnp.transpose(w_in[0]), shard_arr, "cast_w_in")
    w_out_bf = _cast_bf16(jnp.concatenate([w_out_conv[0], w_out_gla[0], w_merge_out[0]], axis=0), "cast_w_out")
    w_in_all, meta_all, conv_all, wgf_all, wgb_all = _gather_via_sibling(
        "gather_w_in", [w_in_slots, meta_tokens, conv_w[0], w_gate_fwd[0], w_gate_bwd[0]],
        (True, False, False, False, False))
    w_out_state, _ = _plane_start("gather_w_out_start", [w_out_bf], "gather", wgb_all)

    w_full_t = w_in_all.reshape(N_IN, D)
    meta_full = jnp.transpose(meta_all, (1, 0, 2)).reshape(N_META, D)
    conv_full = jnp.transpose(conv_all, (1, 0, 2)).reshape(3, D)
    wgf = jnp.pad(wgf_all, ((0, 0), (0, LANES - RANK), (0, 0))).astype(BF16)
    wgb = jnp.pad(wgb_all, ((0, 0), (RANK, LANES - 2 * RANK), (0, 0))).astype(BF16)
    bgf = b_gate_fwd.reshape(N_HEADS, 1, HEAD_K)
    bgb = b_gate_bwd.reshape(N_HEADS, 1, HEAD_K)

    head = jnp.concatenate([jnp.zeros((PAD_FRONT, D), F32), meta_full], axis=0)
    h = jnp.concatenate([jnp.broadcast_to(head[None], (n_seq, CHUNK, D)), x], axis=1).reshape(t_rows, D)
    tgt = jnp.pad(loss_target, ((0, 0), (CHUNK, 0), (0, 0))).reshape(t_rows, D)

    proj, u_t, lr = _in_proj(h, norm_pre, w_full_t)
    yc = _conv_fwd(proj, conv_full, n_seq, lf)
    o = _gla_fwd(proj, lr, wgf, wgb, bgf, bgb, n_seq, lf)
    (w_out_landed,) = _plane_wait("gather_w_out_wait", w_out_state, "gather", o)
    slot_ids = lax.broadcasted_iota(jnp.int32, (4, 1, 1), 0)
    w_out_all = jnp.where(slot_ids == shard, w_out_bf[None], w_out_landed)
    w3 = jnp.transpose(w_out_all.reshape(4, 3, D // 4, D), (1, 0, 2, 3)).reshape(3, D, D)
    (dres, yg, merged, dout_t, dpc_t, dpg_t, dyc, d_o, dtail, loss_acc, d_gpost, d_gamma) = _tail(
        h, tgt, yc, o, proj, w3, gla_norm, norm_post, lf)
    g_w_oc = _wgrad_t(dpc_t, yc, "wgrad_out_conv")
    g_w_og = _wgrad_t(dpg_t, yg, "wgrad_out_gla")
    g_w_mo = _wgrad_t(dout_t, merged, "wgrad_merge_out")
    g_out_slots = jnp.concatenate([g.reshape(4, D // 4, D) for g in (g_w_oc, g_w_og, g_w_mo)], axis=1)
    out_state, out_token = _plane_start("scatter_out_grads_start", [g_out_slots], "scatter", g_w_mo)
    dgla, dlr, dwgf_p, dwgb_p, dbg_p = _gla_bwd(proj, lr, d_o, wgf, wgb, bgf, bgb, n_seq, lf, out_token)
    (got_out,) = _plane_wait("scatter_out_grads_wait", out_state, "scatter", dlr)
    dconv, dconvw_p = _conv_bwd(proj, conv_full, dyc, n_seq, lf)
    g_conv = _wgrad_t(u_t, dconv, "wgrad_in_conv")
    g_gla = _wgrad_t(u_t, dgla, "wgrad_in_gla")
    g_tail = _wgrad_t(u_t, dtail, "wgrad_in_tail")
    g_lr = _wgrad_t(u_t, dlr, "wgrad_in_lr")

    g_in_slots = _reference_rows(g_conv, g_gla, g_tail, g_lr).reshape(4, SHARD_IN, D)
    in_state, in_token = _plane_start("scatter_in_grads_start", [g_in_slots], "scatter", g_lr)
    dh, d_gpre = _dgrad_in(dconv, dgla, dtail, dlr, w_full_t, h, norm_pre, dres, in_token)
    (got_in,) = _plane_wait("scatter_in_grads_wait", in_state, "scatter", d_gpre)

    plane_in = _sum_slots(got_in, "sum_w_in_grads", own=g_in_slots, slot=shard_arr)
    plane_out = _sum_slots(got_out, "sum_w_out_grads", own=g_out_slots, slot=shard_arr)
    swap_state, swap_token = _plane_start("swap_plane_sums_start", [plane_in, plane_out], "swap", plane_out)

    dh3 = dh.reshape(n_seq, lf, D)
    grad_x = dh3[:, CHUNK:, :]

    d_meta = jnp.sum(dh3[:, PAD_FRONT:CHUNK, :], axis=0)
    d_convw = jnp.sum(dconvw_p, axis=0)
    d_wgf = jnp.transpose(jnp.sum(dwgf_p, axis=0)[:, 0:RANK, :], (1, 0, 2)).reshape(RANK, N_HEADS * HEAD_K)
    d_wgb = jnp.transpose(jnp.sum(dwgb_p, axis=0)[:, RANK:2 * RANK, :], (1, 0, 2)).reshape(RANK, N_HEADS * HEAD_K)
    d_bg = jnp.sum(dbg_p, axis=0)
    d_bgf = d_bg[:, 0, :].reshape(1, N_HEADS * HEAD_K)
    d_bgb = d_bg[:, 1, :].reshape(1, N_HEADS * HEAD_K)
    loss_part = loss_acc[0:1, :] + swap_token[0:1, :]
    partials = [d_meta, d_convw, d_wgf, d_wgb, d_gpre, d_bgf, d_bgb, d_gamma, d_gpost, loss_part]
    (g_meta, g_convw, g_wgf, g_wgb, g_npre, g_bgf, g_bgb, g_gnorm, g_npost, loss_row) = _sum_small(
        _exchange("gather_small_grads", partials, ALL_FLIPS, (4, 2, 1), "gather"), "sum_small_grads")
    loss = loss_row[0, 0]
    small_out = _adamw_small(
        [(meta_tokens, g_meta, m_meta_tokens, v_meta_tokens), (norm_pre, g_npre, m_norm_pre, v_norm_pre),
         (conv_w, g_convw, m_conv_w, v_conv_w), (w_gate_fwd, g_wgf, m_w_gate_fwd, v_w_gate_fwd),
         (b_gate_fwd, g_bgf, m_b_gate_fwd, v_b_gate_fwd), (w_gate_bwd, g_wgb, m_w_gate_bwd, v_w_gate_bwd),
         (b_gate_bwd, g_bgb, m_b_gate_bwd, v_b_gate_bwd), (gla_norm, g_gnorm, m_gla_norm, v_gla_norm),
         (norm_post, g_npost, m_norm_post, v_norm_post)], shard_arr, "adamw_small")

    other_in, other_out = _plane_wait("swap_plane_sums_wait", swap_state, "swap", small_out[0][0])
    big_in = _adamw(jnp.transpose(w_in[0]), [plane_in, other_in], jnp.transpose(m_w_in[0]), jnp.transpose(v_w_in[0]),
                    "adamw_w_in")
    out_params = ((w_out_conv, m_w_out_conv, v_w_out_conv), (w_out_gla, m_w_out_gla, v_w_out_gla),
                  (w_merge_out, m_w_merge_out, v_w_merge_out))
    big_out = [_adamw(w[0], [plane_out, other_out], m[0], v[0], f"adamw_w_out_{i}", grad_row=i * (D // 4))
               for i, (w, m, v) in enumerate(out_params)]

    results = []
    for kind in range(4):
        small_kind = [p[kind] for p in small_out]
        w_in_part = jnp.transpose(big_in[kind])[None]
        outs3 = [big_out[i][kind][None] for i in range(3)]
        results.extend(small_kind[0:2] + [w_in_part] + small_kind[2:8] + outs3 + small_kind[8:9])
    return (loss, grad_x, *results)
```

```python
import functools

import jax
import jax.numpy as jnp
from jax import lax
from jax.experimental import pallas as pl
from jax.experimental.pallas import tpu as pltpu

F32 = jnp.float32
BF16 = jnp.bfloat16
MESH = pl.DeviceIdType.MESH

D = 1024
N_META = 16
CHUNK = 64
PAD_FRONT = CHUNK - N_META
N_HEADS = 4
HEAD_K = 128
HEAD_V = 256
RANK = 16
EPS = 1e-6
GATE_NORM = 16.0
N_IN = 9248
SHARD_IN = N_IN // 4
LANES = 128
N_CONV_TILES = 8
W_CONV = 4096
W_GLA = 2048
W_TAIL = 3072
N_MAIN = W_CONV + W_GLA + W_TAIL
OFF_Q, OFF_K, OFF_V, OFF_R = 4096, 4608, 5120, 6144
OFF_LR, OFF_MA = 7168, 7200
MIB = 1024 * 1024

ADAM_LR = 0.001
ADAM_B1 = 0.9
ADAM_B2 = 0.999
ADAM_EPS = 1e-08
ADAM_WD = 0.01
ADAM_STEP = 10


def _params(sem=None, vmem_mib=None):
    return pltpu.CompilerParams(
        dimension_semantics=sem,
        vmem_limit_bytes=None if vmem_mib is None else vmem_mib * MIB)


def _pick_tile(n, target, mult):
    best = None
    for t in range(mult, min(n, target) + 1, mult):
        if n % t == 0:
            best = t
    return n if best is None else best


def _sigmoid(v):
    return 1.0 / (1.0 + jnp.exp(-v))


def _log_sigmoid(v):
    return jnp.minimum(v, 0.0) - jnp.log(1.0 + jnp.exp(-jnp.abs(v)))


def _dot(a, b):
    return jnp.dot(a, b, preferred_element_type=F32)


def _dot_nt(a, b):
    return lax.dot_general(a, b, (((1,), (1,)), ((), ())), preferred_element_type=F32)


def _dot_tn(a, b):
    return lax.dot_general(a, b, (((0,), (0,)), ((), ())), preferred_element_type=F32)


def _gather_all(name, arrs):
    n = len(arrs)
    flips = tuple((m >> 2 & 1, m >> 1 & 1, m & 1) for m in range(1, 8))

    def body(*refs):
        ins, outs = refs[:n], refs[n:2 * n]
        send_sems, recv_sems, local_sems = refs[2 * n:]
        pos = (lax.axis_index("x"), lax.axis_index("y"), lax.axis_index("c"))

        def slot_of(p):
            return 4 * p[0] + 2 * p[1] + p[2]

        peers = [tuple(1 - pos[a] if f[a] else pos[a] for a in range(3)) for f in flips]
        me = slot_of(pos)
        copies = []
        for i in range(n):
            cp = pltpu.make_async_copy(ins[i], outs[i].at[me], local_sems.at[i])
            cp.start()
            copies.append(cp)
        sends = []
        for i in range(n):
            for k, peer in enumerate(peers):
                cp = pltpu.make_async_remote_copy(
                    src_ref=ins[i], dst_ref=outs[i].at[me], send_sem=send_sems.at[i, k], recv_sem=recv_sems.at[i, k],
                    device_id=peer, device_id_type=MESH)
                cp.start()
                sends.append(cp)
        for i in range(n):
            for k, peer in enumerate(peers):
                pltpu.make_async_remote_copy(
                    src_ref=ins[i], dst_ref=outs[i].at[slot_of(peer)], send_sem=send_sems.at[i, k],
                    recv_sem=recv_sems.at[i, k], device_id=peer, device_id_type=MESH).wait_recv()
        for cp in sends:
            cp.wait_send()
        for cp in copies:
            cp.wait()

    hbm = pl.BlockSpec(memory_space=pl.ANY)
    outs = pl.pallas_call(
        body, name=name, out_shape=[jax.ShapeDtypeStruct((8,) + a.shape, a.dtype) for a in arrs],
        in_specs=[hbm] * n, out_specs=[hbm] * n,
        scratch_shapes=[pltpu.SemaphoreType.DMA((n, 7)), pltpu.SemaphoreType.DMA((n, 7)),
                        pltpu.SemaphoreType.DMA((n,))],
        compiler_params=pltpu.CompilerParams(has_side_effects=True),
    )(*arrs)
    return list(outs)


def _gather_via_sibling(name, arrs, slotted):
    n = len(arrs)
    out_shape = [jax.ShapeDtypeStruct(a.shape if slotted[i] else (4,) + a.shape, a.dtype)
                 for i, a in enumerate(arrs)]

    def body(*refs):
        ins, outs = refs[:n], refs[n:2 * n]
        send_sems, recv_sems, local_sems = refs[2 * n:]
        x, y, c = lax.axis_index("x"), lax.axis_index("y"), lax.axis_index("c")
        me = 2 * x + y
        chips = [(1 - x, y), (x, 1 - y), (1 - x, 1 - y)]

        def half(ref, which):
            rows = ref.shape[0]
            cut = rows // 2 // 16 * 16
            return ref.at[pl.ds(0, cut)] if which == 0 else ref.at[pl.ds(cut, rows - cut)]

        def copy(src, dst, i, k, to):
            return pltpu.make_async_remote_copy(
                src_ref=src, dst_ref=dst, send_sem=send_sems.at[i, k], recv_sem=recv_sems.at[i, k],
                device_id=to, device_id_type=MESH)

        def run(mine):
            other = 1 - mine
            local, sends = [], []
            whole = [(not slotted[i]) and arrs[i].shape[0] < 32 for i in range(n)]
            for i in range(n):
                own = outs[i].at[me] if slotted[i] else ins[i]
                if not slotted[i]:
                    cp = pltpu.make_async_copy(ins[i], outs[i].at[me], local_sems.at[i])
                    cp.start()
                    local.append(cp)
                for k, (px, py) in enumerate(chips):
                    if whole[i]:
                        cp = copy(own, outs[i].at[me], i, k, (px, py, mine))
                    elif k < 2:
                        cp = copy(half(own, mine), half(outs[i].at[me], mine), i, k, (px, py, mine))
                    else:
                        continue
                    cp.start()
                    sends.append(cp)
            via = mine
            for k in (via, 1 - via, 2):
                px, py = chips[k]
                slot = 2 * px + py
                source = (px, py, mine) if k < 2 else chips[1 - via] + (mine,)
                for i in range(n):
                    if whole[i]:
                        copy(outs[i].at[slot], outs[i].at[slot], i, k, (px, py, mine)).wait_recv()
                        continue
                    landed = half(outs[i].at[slot], mine)
                    copy(landed, landed, i, k, source).wait_recv()
                    if k == via:
                        cp = copy(landed, landed, i, 2, chips[1 - via] + (mine,))
                        cp.start()
                        sends.append(cp)
                    cp = copy(landed, landed, i, 3 + k, (x, y, other))
                    cp.start()
                    sends.append(cp)
            for k, (px, py) in enumerate(chips):
                slot = 2 * px + py
                for i in range(n):
                    if whole[i]:
                        continue
                    passed = half(outs[i].at[slot], other)
                    copy(passed, passed, i, 3 + k, (x, y, other)).wait_recv()
            for cp in sends:
                cp.wait_send()
            for cp in local:
                cp.wait()

        for mine in (0, 1):
            pl.when(c == mine)(functools.partial(run, mine))

    hbm = pl.BlockSpec(memory_space=pl.ANY)
    outs = pl.pallas_call(
        body, name=name, out_shape=out_shape,
        in_specs=[hbm] * n, out_specs=[hbm] * n,
        scratch_shapes=[pltpu.SemaphoreType.DMA((n, 6)), pltpu.SemaphoreType.DMA((n, 6)),
                        pltpu.SemaphoreType.DMA((n,))],
        input_output_aliases={i: i for i in range(n) if slotted[i]},
        compiler_params=pltpu.CompilerParams(has_side_effects=True),
    )(*arrs)
    return list(outs)


HBM_SPEC = pl.BlockSpec(memory_space=pltpu.HBM)
SEM_SPEC = pl.BlockSpec(memory_space=pltpu.SEMAPHORE)
DATAFLOW = pltpu.SideEffectType.DATAFLOW_SIDE_EFFECTING


def _split_peers(mode):
    x, y, c = lax.axis_index("x"), lax.axis_index("y"), lax.axis_index("c")
    if mode == "swap":
        return 0, [((x, y, 1 - c), 0)]
    return 2 * x + y, [((1 - x, y, c), 2 * (1 - x) + y), ((x, 1 - y, c), 2 * x + 1 - y),
                       ((1 - x, 1 - y, c), 2 * (1 - x) + 1 - y)]


def _split_refs(mode, src, landing, me, peer_slot):
    if mode == "gather":
        return src.at[me], landing.at[me]
    if mode == "scatter":
        return src.at[peer_slot], landing.at[me]
    return src, landing


def _plane_start(name, arrs, mode, after):
    n = len(arrs)
    n_peers = 1 if mode == "swap" else 3
    if mode == "gather":
        srcs, lands = [], list(arrs)
    else:
        srcs, lands = list(arrs), [lax.empty(a.shape, a.dtype) for a in arrs]
    n_src = len(srcs)

    def body(*refs):
        landing = refs[n_src:n_src + n]
        sources = refs[:n_src] if n_src else landing
        send_sems, recv_sems = refs[n_src + n + 1], refs[n_src + n + 2]
        token = refs[-1]
        me, peers = _split_peers(mode)
        for i in range(n):
            for k, (peer, peer_slot) in enumerate(peers):
                src, dst = _split_refs(mode, sources[i], landing[i], me, peer_slot)
                pltpu.make_async_remote_copy(
                    src_ref=src, dst_ref=dst, send_sem=send_sems.at[n_peers * i + k],
                    recv_sem=recv_sems.at[n_peers * i + k], device_id=peer, device_id_type=MESH).start()
        token[...] = jnp.zeros_like(token)

    hbm_in = [pltpu.with_memory_space_constraint(a, pltpu.HBM) for a in srcs + lands]
    out = pl.pallas_call(
        body, name=name,
        out_shape=[pltpu.SemaphoreType.DMA((n_peers * n,)), pltpu.SemaphoreType.DMA((n_peers * n,))]
                  + [pltpu.HBM(a.shape, a.dtype) for a in lands]
                  + [jax.ShapeDtypeStruct((8, LANES), F32)],
        in_specs=[HBM_SPEC] * (n_src + n) + [pl.BlockSpec(memory_space=pl.ANY)],
        out_specs=[SEM_SPEC, SEM_SPEC] + [HBM_SPEC] * n + [pl.BlockSpec(memory_space=pltpu.VMEM)],
        input_output_aliases={n_src + i: 2 + i for i in range(n)},
        compiler_params=pltpu.CompilerParams(has_side_effects=DATAFLOW),
    )(*hbm_in, after)
    return out[:-1], out[-1]


def _plane_wait(name, state, mode, after):
    send_sems, recv_sems = state[0], state[1]
    bufs = list(state[2:])
    n = len(bufs)
    n_peers = 1 if mode == "swap" else 3

    def body(*refs):
        landing = refs[:n]
        send_sems, recv_sems = refs[n], refs[n + 1]
        _, peers = _split_peers(mode)
        for i in range(n):
            for k, (peer, peer_slot) in enumerate(peers):
                arrived = landing[i] if mode == "swap" else landing[i].at[peer_slot]
                cp = pltpu.make_async_remote_copy(
                    src_ref=arrived, dst_ref=arrived, send_sem=send_sems.at[n_peers * i + k],
                    recv_sem=recv_sems.at[n_peers * i + k], device_id=peer, device_id_type=MESH)
                cp.wait_send()
                cp.wait_recv()

    out = pl.pallas_call(
        body, name=name,
        out_shape=[pltpu.HBM(a.shape, a.dtype) for a in bufs],
        in_specs=[HBM_SPEC] * n + [SEM_SPEC, SEM_SPEC, pl.BlockSpec(memory_space=pl.ANY)],
        out_specs=[HBM_SPEC] * n,
        input_output_aliases={i: i for i in range(n)},
        compiler_params=pltpu.CompilerParams(has_side_effects=DATAFLOW),
    )(*bufs, send_sems, recv_sems, after)
    return list(out)


def _tile_2d(rows, cols, row_mult, max_elems=512 * 1024):
    if rows % row_mult == 0:
        rt = _pick_tile(rows, max(row_mult, max_elems // cols), row_mult)
        return (rt, cols), rows // rt, lambda i: (i, 0)
    ct = _pick_tile(cols, max(LANES, max_elems // rows), LANES)
    return (rows, ct), cols // ct, lambda i: (0, i)


def _cast_into_slot(a, slot, name):
    block, steps, index = _tile_2d(a.shape[0], a.shape[1], 16)

    def body(slot_ref, a_ref, o_ref):
        o_ref[...] = a_ref[...].astype(BF16)

    return pl.pallas_call(
        body, name=name,
        grid_spec=pltpu.PrefetchScalarGridSpec(
            num_scalar_prefetch=1, grid=(steps,),
            in_specs=[pl.BlockSpec(block, lambda i, s: index(i))],
            out_specs=pl.BlockSpec((None,) + block, lambda i, s: (s[0],) + index(i))),
        out_shape=jax.ShapeDtypeStruct((4,) + a.shape, BF16),
        compiler_params=_params(("arbitrary",)),
    )(slot, a)


def _sum_slots(buf, name, own=None, slot=None):
    n_slots, rows, cols = buf.shape
    (br, bc), steps, index = _tile_2d(rows, cols, 16, 320 * 1024)

    def body(*refs):
        if own is None:
            b_ref, o_ref = refs
        else:
            slot_ref, b_ref, own_ref, o_ref = refs
        acc = None
        for s in range(n_slots):
            term = b_ref[s] if own is None else jnp.where(slot_ref[0] == s, own_ref[...], b_ref[s])
            acc = term.astype(F32) if acc is None else acc + term.astype(F32)
        o_ref[...] = acc

    out_shape = jax.ShapeDtypeStruct((rows, cols), F32)
    if own is None:
        return pl.pallas_call(
            body, name=name, grid=(steps,),
            in_specs=[pl.BlockSpec((n_slots, br, bc), lambda i: (0,) + index(i))],
            out_specs=pl.BlockSpec((br, bc), index), out_shape=out_shape,
            compiler_params=_params(("parallel",), 48),
        )(buf)
    return pl.pallas_call(
        body, name=name,
        grid_spec=pltpu.PrefetchScalarGridSpec(
            num_scalar_prefetch=1, grid=(steps,),
            in_specs=[pl.BlockSpec((n_slots, br, bc), lambda i, s: (0,) + index(i)),
                      pl.BlockSpec((None, br, bc), lambda i, s: (s[0],) + index(i))],
            out_specs=pl.BlockSpec((br, bc), lambda i, s: index(i))),
        out_shape=out_shape,
        compiler_params=_params(("arbitrary",), 48),
    )(slot, buf, own)


def _sum_small(bufs, name):
    n = len(bufs)

    def body(*refs):
        for b_ref, o_ref in zip(refs[:n], refs[n:]):
            acc = b_ref[0]
            for s in range(1, b_ref.shape[0]):
                acc = acc + b_ref[s]
            o_ref[...] = acc

    vmem = pl.BlockSpec(memory_space=pltpu.VMEM)
    return pl.pallas_call(
        body, name=name, in_specs=[vmem] * n, out_specs=[vmem] * n,
        out_shape=[jax.ShapeDtypeStruct(b.shape[1:], b.dtype) for b in bufs],
    )(*bufs)


def _adam_update(w, g, m, v):
    c1 = 1.0 - ADAM_B1 ** ADAM_STEP
    c2 = 1.0 - ADAM_B2 ** ADAM_STEP
    m_new = ADAM_B1 * m + (1.0 - ADAM_B1) * g
    v_new = ADAM_B2 * v + (1.0 - ADAM_B2) * (g * g)
    m_hat = m_new / c1
    v_hat = v_new / c2
    return -ADAM_LR * (m_hat / (jnp.sqrt(v_hat) + ADAM_EPS) + ADAM_WD * w), m_new, v_new


def _adamw_small(params, slot, name):
    n = len(params)

    def spec_of(shape):
        lead = (None,) * (len(shape) - 2)
        return pl.BlockSpec(lead + tuple(shape[-2:]), lambda i, s, k=len(shape): (0,) * k)

    in_specs, operands, out_specs, out_shape = [], [], [], []
    for w, g, m, v in params:
        shard = g.shape[-1] != w.shape[-1]
        g_spec = pl.BlockSpec(tuple(w.shape[-2:]), (lambda i, s: (0, s[0])) if shard else (lambda i, s: (0, 0)))
        in_specs += [spec_of(w.shape), g_spec, spec_of(m.shape), spec_of(v.shape)]
        operands += [w, g, m, v]
        out_specs += [spec_of(w.shape)] * 4
        out_shape += [jax.ShapeDtypeStruct(w.shape, F32)] * 4

    def body(slot_ref, *refs):
        ins, outs = refs[:4 * n], refs[4 * n:]
        for p in range(n):
            w_ref, g_ref, m_ref, v_ref = ins[4 * p:4 * p + 4]
            g = g_ref[...]
            delta, m_new, v_new = _adam_update(w_ref[...], g, m_ref[...], v_ref[...])
            for o_ref, val in zip(outs[4 * p:4 * p + 4], (g, delta, m_new, v_new)):
                o_ref[...] = val

    out = pl.pallas_call(
        body, name=name,
        grid_spec=pltpu.PrefetchScalarGridSpec(num_scalar_prefetch=1, grid=(1,), in_specs=in_specs, out_specs=out_specs),
        out_shape=out_shape,
    )(slot, *operands)
    return [tuple(out[4 * p:4 * p + 4]) for p in range(n)]


def _adamw(w, grads, m, v, name, grad_row=0):
    rows, cols = w.shape
    (rt, _), _, _ = _tile_2d(rows, cols, 8, 160 * 1024)
    assert grad_row % rt == 0
    n_g = len(grads)

    def body(*refs):
        w_ref = refs[0]
        g_refs = refs[1:1 + n_g]
        m_ref, v_ref, g_out, d_out, m_out, v_out = refs[1 + n_g:]
        g = g_refs[0][...]
        for r in g_refs[1:]:
            g = g + r[...]
        g_out[...] = g
        d_out[...], m_out[...], v_out[...] = _adam_update(w_ref[...], g, m_ref[...], v_ref[...])

    spec = pl.BlockSpec((rt, cols), lambda i: (i, 0))
    grad_spec = pl.BlockSpec((rt, cols), lambda i: (i + grad_row // rt, 0))
    shape = jax.ShapeDtypeStruct((rows, cols), F32)
    return pl.pallas_call(
        body, name=name, grid=(rows // rt,),
        in_specs=[spec] + [grad_spec] * n_g + [spec] * 2, out_specs=[spec] * 4, out_shape=[shape] * 4,
        compiler_params=_params(("parallel",), 48),
    )(w, *grads, m, v)


def _weight_pieces():
    pieces = []
    for j in range(N_CONV_TILES):
        for g in range(4):
            pieces.append((512 * j + 128 * g, D * g + 128 * j, 128))
    for hd in range(N_HEADS):
        base = W_CONV + 512 * hd
        pieces.append((base, OFF_Q + HEAD_K * hd, HEAD_K))
        pieces.append((base + HEAD_K, OFF_K + HEAD_K * hd, HEAD_K))
        pieces.append((base + 2 * HEAD_K, OFF_V + HEAD_V * hd, HEAD_V))
    pieces.append((W_CONV + W_GLA, OFF_R, D))
    pieces.append((W_CONV + W_GLA + D, OFF_MA, 2 * D))
    return pieces


N_WEIGHT_COPIES = len(_weight_pieces()) + 1


def _load_weights(w_hbm, w_s, wlr_s, sems):
    copies = [pltpu.make_async_copy(w_hbm.at[pl.ds(src, n)], w_s.at[pl.ds(dst, n)], sems.at[i])
              for i, (dst, src, n) in enumerate(_weight_pieces())]
    copies.append(pltpu.make_async_copy(w_hbm.at[pl.ds(OFF_LR, LANES)], wlr_s, sems.at[N_WEIGHT_COPIES - 1]))
    for cp in copies:
        cp.start()
    for cp in copies:
        cp.wait()


def _in_proj(h, g_pre, w_full_t):
    t_rows = h.shape[0]
    tm = _pick_tile(t_rows, 384, LANES)
    n_main = N_MAIN

    def body(h_ref, g_ref, w_hbm, proj_ref, ut_ref, lr_ref, w_s, wlr_s, w_sems):
        @pl.when(pl.program_id(0) == 0)
        def _():
            _load_weights(w_hbm, w_s, wlr_s, w_sems)

        hh = h_ref[...]
        rstd = lax.rsqrt(jnp.mean(hh * hh, axis=-1, keepdims=True) + EPS)
        uf = hh * rstd * g_ref[...]
        u = uf.astype(BF16)
        ut_ref[...] = jnp.transpose(uf).astype(BF16)
        lr_ref[...] = _dot_nt(u, wlr_s[...])
        for j in range(n_main // D):
            cols = slice(j * D, (j + 1) * D)
            proj_ref[:, cols] = _dot_nt(u, w_s[cols, :]).astype(BF16)

    return pl.pallas_call(
        body, name="in_proj", grid=(t_rows // tm,),
        in_specs=[pl.BlockSpec((tm, D), lambda i: (i, 0)),
                  pl.BlockSpec((1, D), lambda i: (0, 0)),
                  pl.BlockSpec(memory_space=pl.ANY)],
        out_specs=[pl.BlockSpec((tm, n_main), lambda i: (i, 0)),
                   pl.BlockSpec((D, tm), lambda i: (0, i)),
                   pl.BlockSpec((tm, LANES), lambda i: (i, 0))],
        out_shape=[jax.ShapeDtypeStruct((t_rows, n_main), BF16),
                   jax.ShapeDtypeStruct((D, t_rows), BF16),
                   jax.ShapeDtypeStruct((t_rows, LANES), F32)],
        scratch_shapes=[pltpu.VMEM((n_main, D), BF16), pltpu.VMEM((LANES, D), BF16),
                        pltpu.SemaphoreType.DMA((N_WEIGHT_COPIES,))],
        compiler_params=_params(("arbitrary",), 56),
    )(h, g_pre, w_full_t)


def _conv_parts(p_ref, w_ref):
    cb = p_ref[:, 0:128].astype(F32)
    cc = p_ref[:, 128:256].astype(F32)
    cx = p_ref[:, 256:384].astype(F32)
    cz = p_ref[:, 384:512].astype(F32)
    rows = cb.shape[0]
    w = w_ref[...]
    p = cc * cx
    conv = pltpu.roll(p, 1, 0) * w[0:1] + p * w[1:2] + pltpu.roll(p, rows - 1, 0) * w[2:3]
    sz = _sigmoid(cz)
    return cb, cc, cx, cz, p, conv, sz, w


def _conv_fwd(proj, conv_w, n_seq, lf):
    def body(p_ref, w_ref, y_ref):
        cb, _, _, cz, _, conv, sz, _ = _conv_parts(p_ref, w_ref)
        y_ref[...] = (cb * conv * (cz * sz)).astype(BF16)

    return pl.pallas_call(
        body, name="conv_fwd", grid=(n_seq, N_CONV_TILES),
        in_specs=[pl.BlockSpec((lf, 512), lambda b, j: (b, j)),
                  pl.BlockSpec((3, 128), lambda b, j: (0, j))],
        out_specs=pl.BlockSpec((lf, 128), lambda b, j: (b, j)),
        out_shape=jax.ShapeDtypeStruct((n_seq * lf, D), BF16),
        compiler_params=_params(("parallel", "parallel"), 48),
    )(proj, conv_w)


def _conv_bwd(proj, conv_w, dyc, n_seq, lf):
    def body(p_ref, w_ref, dy_ref, dp_ref, dw_ref):
        cb, cc, cx, cz, p, conv, sz, w = _conv_parts(p_ref, w_ref)
        rows = cb.shape[0]
        dy = dy_ref[...].astype(F32)
        silu = cz * sz
        dcb = dy * conv * silu
        dconv = dy * cb * silu
        dcz = dy * cb * conv * (sz * (1.0 + cz * (1.0 - sz)))
        d_next = pltpu.roll(dconv, rows - 1, 0)
        d_prev = pltpu.roll(dconv, 1, 0)
        dp = d_next * w[0:1] + dconv * w[1:2] + d_prev * w[2:3]
        dp_ref[:, 0:128] = dcb.astype(BF16)
        dp_ref[:, 128:256] = (dp * cx).astype(BF16)
        dp_ref[:, 256:384] = (dp * cc).astype(BF16)
        dp_ref[:, 384:512] = dcz.astype(BF16)
        dw_ref[0:1, :] = jnp.sum(dconv * pltpu.roll(p, 1, 0), axis=0, keepdims=True)
        dw_ref[1:2, :] = jnp.sum(dconv * p, axis=0, keepdims=True)
        dw_ref[2:3, :] = jnp.sum(dconv * pltpu.roll(p, rows - 1, 0), axis=0, keepdims=True)

    return pl.pallas_call(
        body, name="conv_bwd", grid=(n_seq, N_CONV_TILES),
        in_specs=[pl.BlockSpec((lf, 512), lambda b, j: (b, j)),
                  pl.BlockSpec((3, 128), lambda b, j: (0, j)),
                  pl.BlockSpec((lf, 128), lambda b, j: (b, j))],
        out_specs=[pl.BlockSpec((lf, 512), lambda b, j: (b, j)),
                   pl.BlockSpec((None, 3, 128), lambda b, j: (b, 0, j))],
        out_shape=[jax.ShapeDtypeStruct((n_seq * lf, W_CONV), BF16),
                   jax.ShapeDtypeStruct((n_seq, 3, D), F32)],
        compiler_params=_params(("parallel", "parallel"), 48),
    )(proj, conv_w, dyc)


GROUP = 3
GROUP_ROWS = GROUP * CHUNK


def _row_group(shape):
    row = lax.broadcasted_iota(jnp.int32, shape, 0)
    grp = jnp.zeros(shape, jnp.int32)
    for r in range(1, GROUP):
        grp = grp + (row >= r * CHUNK).astype(jnp.int32)
    return grp


def _lane_group(shape, width):
    lane = lax.broadcasted_iota(jnp.int32, shape, 1)
    grp = jnp.zeros(shape, jnp.int32)
    for r in range(1, GROUP):
        grp = grp + (lane >= r * width).astype(jnp.int32)
    return grp


def _score_mask(direction):
    shape = (GROUP_ROWS, GROUP_ROWS)
    row = lax.broadcasted_iota(jnp.int32, shape, 0)
    col = lax.broadcasted_iota(jnp.int32, shape, 1)
    same = _row_group(shape) == _lane_group(shape, CHUNK)
    return same & ((col <= row) if direction == 0 else (col > row))


def _diag_blocks(v):
    w = v.shape[1]
    wide = jnp.concatenate([v] * GROUP, axis=1)
    return jnp.where(_row_group(wide.shape) == _lane_group(wide.shape, w), wide, jnp.zeros_like(wide))


def _per_chunk_dot(lhs, state, transposed):
    outs = []
    for r in range(GROUP):
        rows = lhs[r * CHUNK:(r + 1) * CHUNK, :]
        blk = state[:, r * HEAD_K:(r + 1) * HEAD_K]
        outs.append(_dot_nt(rows, blk) if transposed else _dot(rows, blk))
    return jnp.concatenate(outs, axis=0)


def _chunk_cumsum(v, suffix):
    pos = lax.broadcasted_iota(jnp.int32, v.shape, 0) & (CHUNK - 1)
    shift = 1
    while shift < CHUNK:
        if suffix:
            moved = pltpu.roll(v, GROUP_ROWS - shift, 0)
            v = v + jnp.where(pos < CHUNK - shift, moved, 0.0)
        else:
            moved = pltpu.roll(v, shift, 0)
            v = v + jnp.where(pos >= shift, moved, 0.0)
        shift *= 2
    return v


def _per_chunk_rows(rows_of_chunk):
    w = rows_of_chunk[0].shape[1]
    return jnp.concatenate([jnp.broadcast_to(v, (CHUNK, w)) for v in rows_of_chunk], axis=0)


def _chunk_end_rows(direction, b):
    at = CHUNK - 1 if direction == 0 else 0
    return [b[r * CHUNK + at:r * CHUNK + at + 1, :] for r in range(GROUP)]


def _gla_gates(lr_bf, wg_ref, bg_ref, lf):
    z = _dot(lr_bf, wg_ref[...]) + bg_ref[...]
    valid = lax.broadcasted_iota(jnp.int32, (lf, HEAD_K), 0) >= PAD_FRONT
    return z, valid


def _group_unroll(n_groups):
    return n_groups if n_groups <= 11 else 1


def _group_rows(g):
    return pl.ds(pl.multiple_of(g * GROUP_ROWS, GROUP_ROWS), GROUP_ROWS)


def _chunk_decay(direction, g, r, b_s):
    base = g * GROUP_ROWS + r * CHUNK
    if direction == 0:
        grp = b_s[pl.ds(pl.multiple_of(base + CHUNK - 8, 8), 8), :]
        return jnp.exp(grp[7:8, :])
    grp = b_s[pl.ds(pl.multiple_of(base, 8), 8), :]
    return jnp.exp(grp[0:1, :])


def _state_scan(direction, n_groups, b_s, st_s, reverse):
    ascending = (direction == 0) != reverse

    def step(i, carry):
        g = i if ascending else n_groups - 1 - i
        for rr in range(GROUP):
            r = rr if ascending else GROUP - 1 - rr
            lanes = slice(r * HEAD_K, (r + 1) * HEAD_K)
            decay = _chunk_decay(direction, g, r, b_s)
            local = st_s[g, :, lanes]
            st_s[g, :, lanes] = carry
            carry = (local + carry * decay) if reverse else (carry * decay + local)
        return carry

    lax.fori_loop(0, n_groups, step, jnp.zeros((HEAD_V, HEAD_K), F32), unroll=_group_unroll(n_groups))


def _gla_states(direction, n_groups, qkv_ref, g_s, b_s, st_s):
    def local(g, carry):
        rows = _group_rows(g)
        b = _chunk_cumsum(g_s[rows, :], direction == 1)
        b_s[rows, :] = b
        b_end = _per_chunk_rows(_chunk_end_rows(direction, b))
        k = qkv_ref[rows, 128:256].astype(F32)
        v = qkv_ref[rows, 256:512]
        k_dec = (k * jnp.exp(b_end - b)).astype(BF16)
        st_s[g] = _dot_tn(v, _diag_blocks(k_dec))
        return carry

    lax.fori_loop(0, n_groups, local, 0, unroll=_group_unroll(n_groups))
    _state_scan(direction, n_groups, b_s, st_s, False)


def _gla_fwd(proj, lr, wgf, wgb, bgf, bgb, n_seq, lf):
    assert lf % GROUP_ROWS == 0
    n_groups = lf // GROUP_ROWS
    scale = HEAD_K ** -0.5

    def body(qkv_ref, lr_ref, wgf_ref, wgb_ref, bgf_ref, bgb_ref, o_ref, g_s, b_s2, st_s2):
        lr_bf = lr_ref[...].astype(BF16)
        for direction in (0, 1):
            wg_ref, bg_ref = ((wgf_ref, bgf_ref), (wgb_ref, bgb_ref))[direction]
            z, valid = _gla_gates(lr_bf, wg_ref, bg_ref, lf)
            g_s[...] = jnp.where(valid, _log_sigmoid(z) / GATE_NORM, 0.0)
            _gla_states(direction, n_groups, qkv_ref, g_s, b_s2.at[direction], st_s2.at[direction])
        masks = [_score_mask(0), _score_mask(1)]

        def out(g, carry):
            rows = _group_rows(g)
            q = qkv_ref[rows, 0:128].astype(F32) * scale
            k = qkv_ref[rows, 128:256].astype(F32)
            v = qkv_ref[rows, 256:512]
            o = None
            for direction in (0, 1):
                b = b_s2[direction, rows, :]
                q_in = (q * jnp.exp(b)).astype(BF16)
                k_in = (k * jnp.exp(-b)).astype(BF16)
                s = jnp.where(masks[direction], _dot_nt(q_in, k_in), 0.0).astype(BF16)
                part = _dot(s, v) + _per_chunk_dot(q_in, st_s2[direction, g].astype(BF16), True)
                o = part if o is None else o + part
            o_ref[rows, :] = o
            return carry

        lax.fori_loop(0, n_groups, out, 0, unroll=_group_unroll(n_groups))

    return pl.pallas_call(
        body, name="gla_fwd", grid=(n_seq, N_HEADS),
        in_specs=[pl.BlockSpec((lf, 512), lambda b, h: (b, N_CONV_TILES + h)),
                  pl.BlockSpec((lf, LANES), lambda b, h: (b, 0)),
                  pl.BlockSpec((None, LANES, HEAD_K), lambda b, h: (h, 0, 0)),
                  pl.BlockSpec((None, LANES, HEAD_K), lambda b, h: (h, 0, 0)),
                  pl.BlockSpec((None, 1, HEAD_K), lambda b, h: (h, 0, 0)),
                  pl.BlockSpec((None, 1, HEAD_K), lambda b, h: (h, 0, 0))],
        out_specs=pl.BlockSpec((lf, HEAD_V), lambda b, h: (b, h)),
        out_shape=jax.ShapeDtypeStruct((n_seq * lf, D), F32),
        scratch_shapes=[pltpu.VMEM((lf, HEAD_K), F32), pltpu.VMEM((2, lf, HEAD_K), F32),
                        pltpu.VMEM((2, n_groups, HEAD_V, GROUP * HEAD_K), F32)],
        compiler_params=_params(("parallel", "parallel"), 48),
    )(proj, lr, wgf, wgb, bgf, bgb)


def _gla_bwd(proj, lr, d_o, wgf, wgb, bgf, bgb, n_seq, lf, token):
    assert lf % GROUP_ROWS == 0
    n_groups = lf // GROUP_ROWS
    scale = HEAD_K ** -0.5

    def body(qkv_ref, lr_ref, do_ref, wgf_ref, wgb_ref, bgf_ref, bgb_ref, token_ref,
             dqkv_ref, dlr_ref, dwgf_ref, dwgb_ref, dbg_ref,
             g_s, b_s2, fac_s2, dg_s2, st_s2, dst_s2):
        lr_bf = lr_ref[...].astype(BF16)
        gates = ((wgf_ref, bgf_ref), (wgb_ref, bgb_ref))
        for direction in (0, 1):
            wg_ref, bg_ref = gates[direction]
            b_s, st_s, dst_s = b_s2.at[direction], st_s2.at[direction], dst_s2.at[direction]
            z, valid = _gla_gates(lr_bf, wg_ref, bg_ref, lf)
            g_s[...] = jnp.where(valid, _log_sigmoid(z) / GATE_NORM, 0.0)
            fac_s2[direction] = jnp.where(valid, _sigmoid(-z) / GATE_NORM, 0.0)
            _gla_states(direction, n_groups, qkv_ref, g_s, b_s, st_s)

            def state_grad_local(g, carry):
                rows = _group_rows(g)
                q = qkv_ref[rows, 0:128].astype(F32) * scale
                q_in = (q * jnp.exp(b_s[rows, :])).astype(BF16)
                dst_s[g] = _dot_tn(do_ref[rows, :], _diag_blocks(q_in))
                return carry

            lax.fori_loop(0, n_groups, state_grad_local, 0, unroll=_group_unroll(n_groups))
            _state_scan(direction, n_groups, b_s, dst_s, True)

        masks = [_score_mask(0), _score_mask(1)]

        def group_grads(g, carry):
            rows = _group_rows(g)
            q = qkv_ref[rows, 0:128].astype(F32) * scale
            k = qkv_ref[rows, 128:256].astype(F32)
            v = qkv_ref[rows, 256:512]
            d_out = do_ref[rows, :]
            dq_sum = dk_sum = dv_sum = None
            for direction in (0, 1):
                end_row = CHUNK - 1 if direction == 0 else 0
                b = b_s2[direction, rows, :]
                ends = _chunk_end_rows(direction, b)
                b_end = _per_chunk_rows(ends)
                e_pos = jnp.exp(b)
                e_neg = jnp.exp(-b)
                e_end = jnp.exp(b_end - b)
                q_in = q * e_pos
                k_in = k * e_neg
                k_dec = k * e_end
                q_in_bf = q_in.astype(BF16)
                k_in_bf = k_in.astype(BF16)
                state = st_s2[direction, g]
                d_state = dst_s2[direction, g]
                state_bf = state.astype(BF16)
                d_state_bf = d_state.astype(BF16)
                s = jnp.where(masks[direction], _dot_nt(q_in_bf, k_in_bf), 0.0).astype(BF16)
                ds = jnp.where(masks[direction], _dot_nt(d_out, v), 0.0).astype(BF16)
                dv = _dot_tn(s, d_out) + _per_chunk_dot(k_dec.astype(BF16), d_state_bf, True)
                dq_in = _dot(ds, k_in_bf) + _per_chunk_dot(d_out, state_bf, False)
                dk_in = _dot_tn(ds, q_in_bf)
                dk_dec = _per_chunk_dot(v, d_state_bf, False)
                dq = dq_in * e_pos * scale
                dk = dk_in * e_neg + dk_dec * e_end
                dq_sum = dq if dq_sum is None else dq_sum + dq
                dk_sum = dk if dk_sum is None else dk_sum + dk
                dv_sum = dv if dv_sum is None else dv_sum + dv
                dkk = dk_dec * k_dec
                db = dq_in * q_in - dk_in * k_in - dkk
                d_decay = jnp.sum(d_state * state, axis=0, keepdims=True)
                db_end = [jnp.sum(dkk[r * CHUNK:(r + 1) * CHUNK, :], axis=0, keepdims=True)
                          + d_decay[:, r * HEAD_K:(r + 1) * HEAD_K] * jnp.exp(ends[r]) for r in range(GROUP)]
                row = lax.broadcasted_iota(jnp.int32, (GROUP_ROWS, HEAD_K), 0)
                at_end = row == end_row
                for r in range(1, GROUP):
                    at_end = at_end | (row == r * CHUNK + end_row)
                db = db + jnp.where(at_end, _per_chunk_rows(db_end), 0.0)
                dg_s2[direction, rows, :] = _chunk_cumsum(db, direction == 0)
            dqkv_ref[rows, 0:128] = dq_sum.astype(BF16)
            dqkv_ref[rows, 128:256] = dk_sum.astype(BF16)
            dqkv_ref[rows, 256:512] = dv_sum.astype(BF16)
            return carry

        lax.fori_loop(0, n_groups, group_grads, 0, unroll=_group_unroll(n_groups))

        dlr = jnp.zeros((lf, LANES), F32)
        for direction in (0, 1):
            dz = dg_s2[direction] * fac_s2[direction]
            dz_bf = dz.astype(BF16)
            dbg_ref[direction:direction + 1, :] = jnp.sum(dz, axis=0, keepdims=True)
            (dwgf_ref, dwgb_ref)[direction][...] = _dot_tn(lr_bf, dz_bf)
            dlr = dlr + _dot_nt(dz_bf, gates[direction][0][...])

        @pl.when(pl.program_id(1) == 0)
        def _():
            dlr_ref[...] = dlr

        @pl.when(pl.program_id(1) != 0)
        def _():
            dlr_ref[...] = dlr_ref[...] + dlr

    gate_w = pl.BlockSpec((None, LANES, HEAD_K), lambda b, h: (h, 0, 0))
    gate_b = pl.BlockSpec((None, 1, HEAD_K), lambda b, h: (h, 0, 0))
    return pl.pallas_call(
        body, name="gla_bwd", grid=(n_seq, N_HEADS),
        in_specs=[pl.BlockSpec((lf, 512), lambda b, h: (b, N_CONV_TILES + h)),
                  pl.BlockSpec((lf, LANES), lambda b, h: (b, 0)),
                  pl.BlockSpec((lf, HEAD_V), lambda b, h: (b, h)),
                  gate_w, gate_w, gate_b, gate_b,
                  pl.BlockSpec((8, LANES), lambda b, h: (0, 0))],
        out_specs=[pl.BlockSpec((lf, 512), lambda b, h: (b, h)),
                   pl.BlockSpec((lf, LANES), lambda b, h: (b, 0)),
                   pl.BlockSpec((None, None, LANES, HEAD_K), lambda b, h: (b, h, 0, 0)),
                   pl.BlockSpec((None, None, LANES, HEAD_K), lambda b, h: (b, h, 0, 0)),
                   pl.BlockSpec((None, None, 2, HEAD_K), lambda b, h: (b, h, 0, 0))],
        out_shape=[jax.ShapeDtypeStruct((n_seq * lf, W_GLA), BF16),
                   jax.ShapeDtypeStruct((n_seq * lf, LANES), F32),
                   jax.ShapeDtypeStruct((n_seq, N_HEADS, LANES, HEAD_K), F32),
                   jax.ShapeDtypeStruct((n_seq, N_HEADS, LANES, HEAD_K), F32),
                   jax.ShapeDtypeStruct((n_seq, N_HEADS, 2, HEAD_K), F32)],
        scratch_shapes=[pltpu.VMEM((lf, HEAD_K), F32), pltpu.VMEM((2, lf, HEAD_K), F32),
                        pltpu.VMEM((2, lf, HEAD_K), F32), pltpu.VMEM((2, lf, HEAD_K), F32),
                        pltpu.VMEM((2, n_groups, HEAD_V, GROUP * HEAD_K), F32),
                        pltpu.VMEM((2, n_groups, HEAD_V, GROUP * HEAD_K), F32)],
        compiler_params=_params(("parallel", "arbitrary"), 56),
    )(proj, lr, d_o, wgf, wgb, bgf, bgb, token)


def _tail(h, tgt, yc, o, proj, w3, gamma, g_post, lf):
    t_rows = h.shape[0]
    tm = _pick_tile(t_rows, 256, CHUNK)
    n_chunks = lf // CHUNK
    per_tile = tm // CHUNK

    def body(h_ref, tgt_ref, yc_ref, o_ref, r_ref, ma_ref, mb_ref, w_hbm, gamma_ref, gpost_ref,
             dres_ref, yg_ref, merged_ref, dout_ref, dpc_ref, dpg_ref, dyc_ref, do_ref, dtail_ref,
             loss_ref, dgpost_ref, dgamma_ref, w_s, w_sem):
        i = pl.program_id(0)

        @pl.when(i == 0)
        def _():
            cp = pltpu.make_async_copy(w_hbm, w_s, w_sem)
            cp.start()
            cp.wait()
            loss_ref[...] = jnp.zeros_like(loss_ref)
            dgpost_ref[...] = jnp.zeros_like(dgpost_ref)
            dgamma_ref[...] = jnp.zeros_like(dgamma_ref)

        gamma = gamma_ref[...]
        o = o_ref[...]
        r = r_ref[...].astype(F32)
        sr = _sigmoid(r)
        silu_r = r * sr
        n_parts, rstd_parts = [], []
        for hd in range(N_HEADS):
            oh = o[:, hd * HEAD_V:(hd + 1) * HEAD_V]
            rstd = lax.rsqrt(jnp.mean(oh * oh, axis=-1, keepdims=True) + EPS)
            n_parts.append(oh * rstd)
            rstd_parts.append(rstd)
        n = jnp.concatenate(n_parts, axis=-1)
        gamma_t = jnp.concatenate([gamma] * N_HEADS, axis=-1)
        yg = n * gamma_t * silu_r
        yg_bf = yg.astype(BF16)
        yg_ref[...] = yg_bf
        yc = yc_ref[...]
        pc = _dot(yc, w_s[0])
        pg = _dot(yg_bf, w_s[1])
        sa = _sigmoid(ma_ref[...].astype(F32))
        sb = _sigmoid(mb_ref[...].astype(F32))
        merged = (sa * pc + sb * pg).astype(BF16)
        merged_ref[...] = merged
        out = _dot(merged, w_s[2])
        rstd2 = lax.rsqrt(jnp.mean(out * out, axis=-1, keepdims=True) + EPS)
        nn = out * rstd2
        gpost = gpost_ref[...]
        y = h_ref[...] + nn * gpost

        rowi = lax.broadcasted_iota(jnp.int32, (tm, 1), 0)
        keep = jnp.zeros((tm, 1), F32)
        for kk in range(per_tile):
            is_tok = ((i * per_tile + kk) % n_chunks) != 0
            f = jnp.where(is_tok, 1.0, 0.0)
            keep = jnp.where((rowi >= kk * CHUNK) & (rowi < (kk + 1) * CHUNK), f, keep)
        diff = (y - tgt_ref[...]) * keep
        loss_ref[...] += jnp.sum(diff * diff) * (0.5 / D)
        dy = diff * (1.0 / D)
        dres_ref[...] = dy
        dgpost_ref[...] += jnp.sum(dy * nn, axis=0, keepdims=True)
        dn = dy * gpost
        dout_f = rstd2 * (dn - nn * jnp.mean(dn * nn, axis=-1, keepdims=True))
        dout = dout_f.astype(BF16)
        dout_ref[...] = jnp.transpose(dout_f).astype(BF16)
        dmerged = _dot_nt(dout, w_s[2])
        dpc_f = dmerged * sa
        dpg_f = dmerged * sb
        dpc = dpc_f.astype(BF16)
        dpg = dpg_f.astype(BF16)
        dpc_ref[...] = jnp.transpose(dpc_f).astype(BF16)
        dpg_ref[...] = jnp.transpose(dpg_f).astype(BF16)
        dtail_ref[:, D:2 * D] = (dmerged * pc * (sa * (1.0 - sa))).astype(BF16)
        dtail_ref[:, 2 * D:3 * D] = (dmerged * pg * (sb * (1.0 - sb))).astype(BF16)
        dyc_ref[...] = _dot_nt(dpc, w_s[0]).astype(BF16)
        dyg = _dot_nt(dpg, w_s[1])
        dtail_ref[:, 0:D] = (dyg * n * gamma_t * (sr * (1.0 + r * (1.0 - sr)))).astype(BF16)
        dgam_full = jnp.sum(dyg * n * silu_r, axis=0, keepdims=True)
        dgam = dgam_full[:, 0:HEAD_V]
        for hd in range(1, N_HEADS):
            dgam = dgam + dgam_full[:, hd * HEAD_V:(hd + 1) * HEAD_V]
        dgamma_ref[...] += dgam
        dng = dyg * gamma_t * silu_r
        do_parts = []
        for hd in range(N_HEADS):
            sl = slice(hd * HEAD_V, (hd + 1) * HEAD_V)
            dnh = dng[:, sl]
            nh = n_parts[hd]
            do_parts.append(rstd_parts[hd] * (dnh - nh * jnp.mean(dnh * nh, axis=-1, keepdims=True)))
        do_ref[...] = jnp.concatenate(do_parts, axis=-1).astype(BF16)

    row = lambda c: pl.BlockSpec((tm, D), lambda i: (i, c))
    col = pl.BlockSpec((D, tm), lambda i: (0, i))
    const = lambda shape: pl.BlockSpec(shape, lambda i: (0, 0))
    act = jax.ShapeDtypeStruct((t_rows, D), BF16)
    act_t = jax.ShapeDtypeStruct((D, t_rows), BF16)
    return pl.pallas_call(
        body, name="tail", grid=(t_rows // tm,),
        in_specs=[row(0), row(0), row(0), row(0), row(6), row(7), row(8),
                  pl.BlockSpec(memory_space=pl.ANY), const((1, HEAD_V)), const((1, D))],
        out_specs=[row(0)] * 3 + [col] * 3 + [row(0)] * 2
                  + [pl.BlockSpec((tm, W_TAIL), lambda i: (i, 0)),
                     const((8, LANES)), const((1, D)), const((1, HEAD_V))],
        out_shape=[jax.ShapeDtypeStruct((t_rows, D), F32)] + [act] * 2 + [act_t] * 3 + [act] * 2
                  + [jax.ShapeDtypeStruct((t_rows, W_TAIL), BF16),
                     jax.ShapeDtypeStruct((8, LANES), F32),
                     jax.ShapeDtypeStruct((1, D), F32),
                     jax.ShapeDtypeStruct((1, HEAD_V), F32)],
        scratch_shapes=[pltpu.VMEM((3, D, D), BF16), pltpu.SemaphoreType.DMA],
        compiler_params=_params(("arbitrary",), 56),
    )(h, tgt, yc, o, proj, proj, proj, w3, gamma, g_post)


def _wgrad_t(a_t, b, name, out_dtype=BF16):
    m, t_rows = a_t.shape
    n = b.shape[1]
    tn = D if n % D == 0 else n
    tk = _pick_tile(t_rows, 768, LANES)
    n_k = t_rows // tk

    def body(a_ref, b_ref, o_ref, acc):
        k = pl.program_id(1)

        @pl.when(k == 0)
        def _():
            acc[...] = jnp.zeros_like(acc)

        acc[...] += _dot(a_ref[...], b_ref[...].astype(BF16))

        @pl.when(k == n_k - 1)
        def _():
            o_ref[...] = jnp.transpose(acc[...]).astype(out_dtype)

    return pl.pallas_call(
        body, name=name, grid=(n // tn, n_k),
        in_specs=[pl.BlockSpec((m, tk), lambda j, k: (0, k)),
                  pl.BlockSpec((tk, tn), lambda j, k: (k, j))],
        out_specs=pl.BlockSpec((tn, m), lambda j, k: (j, 0)),
        out_shape=jax.ShapeDtypeStruct((n, m), out_dtype),
        scratch_shapes=[pltpu.VMEM((m, tn), F32)],
        compiler_params=_params(("parallel", "arbitrary"), 48),
    )(a_t, b)


def _dgrad_in(dpc, dpg, dpt, dlr, w_full_t, h, g_pre, dres, token):
    t_rows = h.shape[0]
    tm = _pick_tile(t_rows, 384, 16)
    n_main = N_MAIN

    def body(dpc_ref, dpg_ref, dpt_ref, dlr_ref, w_hbm, h_ref, g_ref, dres_ref, token_ref,
             dh_ref, dg_ref, w_s, wlr_s, w_sems):
        @pl.when(pl.program_id(0) == 0)
        def _():
            _load_weights(w_hbm, w_s, wlr_s, w_sems)
            dg_ref[...] = jnp.zeros_like(dg_ref)

        du = _dot(dlr_ref[...].astype(BF16), wlr_s[...])
        du += _dot(dpc_ref[...], w_s[0:W_CONV, :])
        du += _dot(dpg_ref[...], w_s[W_CONV:W_CONV + W_GLA, :])
        du += _dot(dpt_ref[...], w_s[W_CONV + W_GLA:n_main, :])
        hh = h_ref[...]
        rstd = lax.rsqrt(jnp.mean(hh * hh, axis=-1, keepdims=True) + EPS)
        xhat = hh * rstd
        dg_ref[...] += jnp.sum(du * xhat, axis=0, keepdims=True)
        dx = du * g_ref[...]
        dh_ref[...] = rstd * (dx - xhat * jnp.mean(dx * xhat, axis=-1, keepdims=True)) + dres_ref[...]

    row = lambda width: pl.BlockSpec((tm, width), lambda i: (i, 0))
    return pl.pallas_call(
        body, name="dgrad_in", grid=(t_rows // tm,),
        in_specs=[row(W_CONV), row(W_GLA), row(W_TAIL), row(LANES),
                  pl.BlockSpec(memory_space=pl.ANY),
                  row(D), pl.BlockSpec((1, D), lambda i: (0, 0)), row(D),
                  pl.BlockSpec((8, LANES), lambda i: (0, 0))],
        out_specs=[row(D), pl.BlockSpec((1, D), lambda i: (0, 0))],
        out_shape=[jax.ShapeDtypeStruct((t_rows, D), F32), jax.ShapeDtypeStruct((1, D), F32)],
        scratch_shapes=[pltpu.VMEM((n_main, D), BF16), pltpu.VMEM((LANES, D), BF16),
                        pltpu.SemaphoreType.DMA((N_WEIGHT_COPIES,))],
        compiler_params=_params(("arbitrary",), 56),
    )(dpc, dpg, dpt, dlr, w_full_t, h, g_pre, dres, token)


def _reference_rows(g_conv, g_gla, g_tail, g_lr):
    conv = g_conv.reshape(N_CONV_TILES, 4, 128, D).transpose(1, 0, 2, 3).reshape(W_CONV, D)
    gla = g_gla.reshape(N_HEADS, 512, D)
    q = gla[:, 0:128].reshape(N_HEADS * HEAD_K, D)
    k = gla[:, 128:256].reshape(N_HEADS * HEAD_K, D)
    v = gla[:, 256:512].reshape(N_HEADS * HEAD_V, D)
    return jnp.concatenate([conv, q, k, v, g_tail[0:D], g_lr[0:2 * RANK], g_tail[D:3 * D]], axis=0)


def kernel(x, meta_tokens, norm_pre, w_in, conv_w, w_gate_fwd, b_gate_fwd, w_gate_bwd, b_gate_bwd, gla_norm, w_out_conv, w_out_gla, w_merge_out, norm_post, loss_target, m_meta_tokens, m_norm_pre, m_w_in, m_conv_w, m_w_gate_fwd, m_b_gate_fwd, m_w_gate_bwd, m_b_gate_bwd, m_gla_norm, m_w_out_conv, m_w_out_gla, m_w_merge_out, m_norm_post, v_meta_tokens, v_norm_pre, v_w_in, v_conv_w, v_w_gate_fwd, v_b_gate_fwd, v_w_gate_bwd, v_b_gate_bwd, v_gla_norm, v_w_out_conv, v_w_out_gla, v_w_merge_out, v_norm_post):
    n_seq, seq, _ = x.shape
    lf = CHUNK + seq
    t_rows = n_seq * lf
    shard = 2 * lax.axis_index("x") + lax.axis_index("y")
    shard_arr = jnp.reshape(shard, (1,)).astype(jnp.int32)

    w_in_slots = _cast_into_slot(jnp.transpose(w_in[0]), shard_arr, "cast_w_in")
    w_out_slots = _cast_into_slot(jnp.concatenate([w_out_conv[0], w_out_gla[0], w_merge_out[0]], axis=0), shard_arr,
                                  "cast_w_out")
    w_in_all, meta_all, conv_all, wgf_all, wgb_all = _gather_via_sibling(
        "gather_w_in", [w_in_slots, meta_tokens, conv_w[0], w_gate_fwd[0], w_gate_bwd[0]],
        (True, False, False, False, False))
    w_out_state, _ = _plane_start("gather_w_out_start", [w_out_slots], "gather", wgb_all)

    w_full_t = w_in_all.reshape(N_IN, D)
    meta_full = jnp.transpose(meta_all, (1, 0, 2)).reshape(N_META, D)
    conv_full = jnp.transpose(conv_all, (1, 0, 2)).reshape(3, D)
    wgf = jnp.pad(wgf_all, ((0, 0), (0, LANES - RANK), (0, 0))).astype(BF16)
    wgb = jnp.pad(wgb_all, ((0, 0), (RANK, LANES - 2 * RANK), (0, 0))).astype(BF16)
    bgf = b_gate_fwd.reshape(N_HEADS, 1, HEAD_K)
    bgb = b_gate_bwd.reshape(N_HEADS, 1, HEAD_K)

    head = jnp.concatenate([jnp.zeros((PAD_FRONT, D), F32), meta_full], axis=0)
    h = jnp.concatenate([jnp.broadcast_to(head[None], (n_seq, CHUNK, D)), x], axis=1).reshape(t_rows, D)
    tgt = jnp.pad(loss_target, ((0, 0), (CHUNK, 0), (0, 0))).reshape(t_rows, D)

    proj, u_t, lr = _in_proj(h, norm_pre, w_full_t)
    yc = _conv_fwd(proj, conv_full, n_seq, lf)
    o = _gla_fwd(proj, lr, wgf, wgb, bgf, bgb, n_seq, lf)
    (w_out_all,) = _plane_wait("gather_w_out_wait", w_out_state, "gather", o)
    w3 = jnp.transpose(w_out_all.reshape(4, 3, D // 4, D), (1, 0, 2, 3)).reshape(3, D, D)
    (dres, yg, merged, dout_t, dpc_t, dpg_t, dyc, d_o, dtail, loss_acc, d_gpost, d_gamma) = _tail(
        h, tgt, yc, o, proj, w3, gla_norm, norm_post, lf)
    g_w_oc = _wgrad_t(dpc_t, yc, "wgrad_out_conv")
    g_w_og = _wgrad_t(dpg_t, yg, "wgrad_out_gla")
    g_w_mo = _wgrad_t(dout_t, merged, "wgrad_merge_out")
    g_out_slots = jnp.concatenate([g.reshape(4, D // 4, D) for g in (g_w_oc, g_w_og, g_w_mo)], axis=1)
    out_state, out_token = _plane_start("scatter_out_grads_start", [g_out_slots], "scatter", g_w_mo)
    dgla, dlr, dwgf_p, dwgb_p, dbg_p = _gla_bwd(proj, lr, d_o, wgf, wgb, bgf, bgb, n_seq, lf, out_token)
    (got_out,) = _plane_wait("scatter_out_grads_wait", out_state, "scatter", dlr)
    dconv, dconvw_p = _conv_bwd(proj, conv_full, dyc, n_seq, lf)
    g_conv = _wgrad_t(u_t, dconv, "wgrad_in_conv")
    g_gla = _wgrad_t(u_t, dgla, "wgrad_in_gla")
    g_tail = _wgrad_t(u_t, dtail, "wgrad_in_tail")
    g_lr = _wgrad_t(u_t, dlr, "wgrad_in_lr")

    g_in_slots = _reference_rows(g_conv, g_gla, g_tail, g_lr).reshape(4, SHARD_IN, D)
    in_state, in_token = _plane_start("scatter_in_grads_start", [g_in_slots], "scatter", g_lr)
    dh, d_gpre = _dgrad_in(dconv, dgla, dtail, dlr, w_full_t, h, norm_pre, dres, in_token)
    (got_in,) = _plane_wait("scatter_in_grads_wait", in_state, "scatter", d_gpre)

    plane_in = _sum_slots(got_in, "sum_w_in_grads", own=g_in_slots, slot=shard_arr)
    plane_out = _sum_slots(got_out, "sum_w_out_grads", own=g_out_slots, slot=shard_arr)
    swap_state, swap_token = _plane_start("swap_plane_sums_start", [plane_in, plane_out], "swap", plane_out)

    dh3 = dh.reshape(n_seq, lf, D)
    grad_x = dh3[:, CHUNK:, :]

    d_meta = jnp.sum(dh3[:, PAD_FRONT:CHUNK, :], axis=0)
    d_convw = jnp.sum(dconvw_p, axis=0)
    d_wgf = jnp.transpose(jnp.sum(dwgf_p, axis=0)[:, 0:RANK, :], (1, 0, 2)).reshape(RANK, N_HEADS * HEAD_K)
    d_wgb = jnp.transpose(jnp.sum(dwgb_p, axis=0)[:, RANK:2 * RANK, :], (1, 0, 2)).reshape(RANK, N_HEADS * HEAD_K)
    d_bg = jnp.sum(dbg_p, axis=0)
    d_bgf = d_bg[:, 0, :].reshape(1, N_HEADS * HEAD_K)
    d_bgb = d_bg[:, 1, :].reshape(1, N_HEADS * HEAD_K)
    loss_part = loss_acc[0:1, :] + swap_token[0:1, :]
    partials = [d_meta, d_convw, d_wgf, d_wgb, d_gpre, d_bgf, d_bgb, d_gamma, d_gpost, loss_part]
    (g_meta, g_convw, g_wgf, g_wgb, g_npre, g_bgf, g_bgb, g_gnorm, g_npost, loss_row) = _sum_small(
        _gather_all("gather_small_grads", partials), "sum_small_grads")
    loss = loss_row[0, 0]
    small_out = _adamw_small(
        [(meta_tokens, g_meta, m_meta_tokens, v_meta_tokens), (norm_pre, g_npre, m_norm_pre, v_norm_pre),
         (conv_w, g_convw, m_conv_w, v_conv_w), (w_gate_fwd, g_wgf, m_w_gate_fwd, v_w_gate_fwd),
         (b_gate_fwd, g_bgf, m_b_gate_fwd, v_b_gate_fwd), (w_gate_bwd, g_wgb, m_w_gate_bwd, v_w_gate_bwd),
         (b_gate_bwd, g_bgb, m_b_gate_bwd, v_b_gate_bwd), (gla_norm, g_gnorm, m_gla_norm, v_gla_norm),
         (norm_post, g_npost, m_norm_post, v_norm_post)], shard_arr, "adamw_small")

    other_in, other_out = _plane_wait("swap_plane_sums_wait", swap_state, "swap", small_out[0][0])
    big_in = _adamw(jnp.transpose(w_in[0]), [plane_in, other_in], jnp.transpose(m_w_in[0]), jnp.transpose(v_w_in[0]),
                    "adamw_w_in")
    out_params = ((w_out_conv, m_w_out_conv, v_w_out_conv), (w_out_gla, m_w_out_gla, v_w_out_gla),
                  (w_merge_out, m_w_merge_out, v_w_merge_out))
    big_out = [_adamw(w[0], [plane_out, other_out], m[0], v[0], f"adamw_w_out_{i}", grad_row=i * (D // 4))
               for i, (w, m, v) in enumerate(out_params)]

    results = []
    for kind in range(4):
        small_kind = [p[kind] for p in small_out]
        w_in_part = jnp.transpose(big_in[kind])[None]
        outs3 = [big_out[i][kind][None] for i in range(3)]
        results.extend(small_kind[0:2] + [w_in_part] + small_kind[2:8] + outs3 + small_kind[8:9])
    return (loss, grad_x, *results)
```

```python
import functools

import jax
import jax.numpy as jnp
from jax import lax
from jax.experimental import pallas as pl
from jax.experimental.pallas import tpu as pltpu

F32 = jnp.float32
BF16 = jnp.bfloat16
MESH = pl.DeviceIdType.MESH

D = 1024
N_META = 16
CHUNK = 64
PAD_FRONT = CHUNK - N_META
N_HEADS = 4
HEAD_K = 128
HEAD_V = 256
RANK = 16
EPS = 1e-6
GATE_NORM = 16.0
N_IN = 9248
SHARD_IN = N_IN // 4
LANES = 128
N_CONV_TILES = 8
W_CONV = 4096
W_GLA = 2048
W_TAIL = 3072
N_MAIN = W_CONV + W_GLA + W_TAIL
OFF_Q, OFF_K, OFF_V, OFF_R = 4096, 4608, 5120, 6144
OFF_LR, OFF_MA = 7168, 7200
MIB = 1024 * 1024

ADAM_LR = 0.001
ADAM_B1 = 0.9
ADAM_B2 = 0.999
ADAM_EPS = 1e-08
ADAM_WD = 0.01
ADAM_STEP = 10


def _params(sem=None, vmem_mib=None):
    return pltpu.CompilerParams(
        dimension_semantics=sem,
        vmem_limit_bytes=None if vmem_mib is None else vmem_mib * MIB)


def _pick_tile(n, target, mult):
    best = None
    for t in range(mult, min(n, target) + 1, mult):
        if n % t == 0:
            best = t
    return n if best is None else best


def _sigmoid(v):
    return 1.0 / (1.0 + jnp.exp(-v))


def _log_sigmoid(v):
    return jnp.minimum(v, 0.0) - jnp.log(1.0 + jnp.exp(-jnp.abs(v)))


def _dot(a, b):
    return jnp.dot(a, b, preferred_element_type=F32)


def _dot_nt(a, b):
    return lax.dot_general(a, b, (((1,), (1,)), ((), ())), preferred_element_type=F32)


def _dot_tn(a, b):
    return lax.dot_general(a, b, (((0,), (0,)), ((), ())), preferred_element_type=F32)


def _gather_all(name, arrs):
    n = len(arrs)
    flips = tuple((m >> 2 & 1, m >> 1 & 1, m & 1) for m in range(1, 8))

    def body(*refs):
        ins, outs = refs[:n], refs[n:2 * n]
        send_sems, recv_sems, local_sems = refs[2 * n:]
        pos = (lax.axis_index("x"), lax.axis_index("y"), lax.axis_index("c"))

        def slot_of(p):
            return 4 * p[0] + 2 * p[1] + p[2]

        peers = [tuple(1 - pos[a] if f[a] else pos[a] for a in range(3)) for f in flips]
        me = slot_of(pos)
        copies = []
        for i in range(n):
            cp = pltpu.make_async_copy(ins[i], outs[i].at[me], local_sems.at[i])
            cp.start()
            copies.append(cp)
        sends = []
        for i in range(n):
            for k, peer in enumerate(peers):
                cp = pltpu.make_async_remote_copy(
                    src_ref=ins[i], dst_ref=outs[i].at[me], send_sem=send_sems.at[i, k], recv_sem=recv_sems.at[i, k],
                    device_id=peer, device_id_type=MESH)
                cp.start()
                sends.append(cp)
        for i in range(n):
            for k, peer in enumerate(peers):
                pltpu.make_async_remote_copy(
                    src_ref=ins[i], dst_ref=outs[i].at[slot_of(peer)], send_sem=send_sems.at[i, k],
                    recv_sem=recv_sems.at[i, k], device_id=peer, device_id_type=MESH).wait_recv()
        for cp in sends:
            cp.wait_send()
        for cp in copies:
            cp.wait()

    hbm = pl.BlockSpec(memory_space=pl.ANY)
    outs = pl.pallas_call(
        body, name=name, out_shape=[jax.ShapeDtypeStruct((8,) + a.shape, a.dtype) for a in arrs],
        in_specs=[hbm] * n, out_specs=[hbm] * n,
        scratch_shapes=[pltpu.SemaphoreType.DMA((n, 7)), pltpu.SemaphoreType.DMA((n, 7)),
                        pltpu.SemaphoreType.DMA((n,))],
        compiler_params=pltpu.CompilerParams(has_side_effects=True),
    )(*arrs)
    return list(outs)


def _gather_via_sibling(name, arrs, slotted):
    n = len(arrs)
    out_shape = [jax.ShapeDtypeStruct(a.shape if slotted[i] else (4,) + a.shape, a.dtype)
                 for i, a in enumerate(arrs)]

    def body(*refs):
        ins, outs = refs[:n], refs[n:2 * n]
        send_sems, recv_sems, local_sems = refs[2 * n:]
        x, y, c = lax.axis_index("x"), lax.axis_index("y"), lax.axis_index("c")
        me = 2 * x + y
        chips = [(1 - x, y), (x, 1 - y), (1 - x, 1 - y)]

        def half(ref, which):
            rows = ref.shape[0]
            cut = rows // 2 // 16 * 16
            return ref.at[pl.ds(0, cut)] if which == 0 else ref.at[pl.ds(cut, rows - cut)]

        def copy(src, dst, i, k, to):
            return pltpu.make_async_remote_copy(
                src_ref=src, dst_ref=dst, send_sem=send_sems.at[i, k], recv_sem=recv_sems.at[i, k],
                device_id=to, device_id_type=MESH)

        def run(mine):
            other = 1 - mine
            local, sends = [], []
            whole = [(not slotted[i]) and arrs[i].shape[0] < 32 for i in range(n)]
            for i in range(n):
                own = outs[i].at[me] if slotted[i] else ins[i]
                if not slotted[i]:
                    cp = pltpu.make_async_copy(ins[i], outs[i].at[me], local_sems.at[i])
                    cp.start()
                    local.append(cp)
                for k, (px, py) in enumerate(chips):
                    if whole[i]:
                        cp = copy(own, outs[i].at[me], i, k, (px, py, mine))
                    elif k < 2:
                        cp = copy(half(own, mine), half(outs[i].at[me], mine), i, k, (px, py, mine))
                    else:
                        continue
                    cp.start()
                    sends.append(cp)
            via = mine
            for k in (via, 1 - via, 2):
                px, py = chips[k]
                slot = 2 * px + py
                source = (px, py, mine) if k < 2 else chips[1 - via] + (mine,)
                for i in range(n):
                    if whole[i]:
                        copy(outs[i].at[slot], outs[i].at[slot], i, k, (px, py, mine)).wait_recv()
                        continue
                    landed = half(outs[i].at[slot], mine)
                    copy(landed, landed, i, k, source).wait_recv()
                    if k == via:
                        cp = copy(landed, landed, i, 2, chips[1 - via] + (mine,))
                        cp.start()
                        sends.append(cp)
                    cp = copy(landed, landed, i, 3 + k, (x, y, other))
                    cp.start()
                    sends.append(cp)
            for k, (px, py) in enumerate(chips):
                slot = 2 * px + py
                for i in range(n):
                    if whole[i]:
                        continue
                    passed = half(outs[i].at[slot], other)
                    copy(passed, passed, i, 3 + k, (x, y, other)).wait_recv()
            for cp in sends:
                cp.wait_send()
            for cp in local:
                cp.wait()

        for mine in (0, 1):
            pl.when(c == mine)(functools.partial(run, mine))

    hbm = pl.BlockSpec(memory_space=pl.ANY)
    outs = pl.pallas_call(
        body, name=name, out_shape=out_shape,
        in_specs=[hbm] * n, out_specs=[hbm] * n,
        scratch_shapes=[pltpu.SemaphoreType.DMA((n, 6)), pltpu.SemaphoreType.DMA((n, 6)),
                        pltpu.SemaphoreType.DMA((n,))],
        input_output_aliases={i: i for i in range(n) if slotted[i]},
        compiler_params=pltpu.CompilerParams(has_side_effects=True),
    )(*arrs)
    return list(outs)


HBM_SPEC = pl.BlockSpec(memory_space=pltpu.HBM)
SEM_SPEC = pl.BlockSpec(memory_space=pltpu.SEMAPHORE)
DATAFLOW = pltpu.SideEffectType.DATAFLOW_SIDE_EFFECTING


def _split_peers(mode):
    x, y, c = lax.axis_index("x"), lax.axis_index("y"), lax.axis_index("c")
    if mode == "swap":
        return 0, [((x, y, 1 - c), 0)]
    return 2 * x + y, [((1 - x, y, c), 2 * (1 - x) + y), ((x, 1 - y, c), 2 * x + 1 - y),
                       ((1 - x, 1 - y, c), 2 * (1 - x) + 1 - y)]


def _split_refs(mode, src, landing, me, peer_slot):
    if mode == "gather":
        return src.at[me], landing.at[me]
    if mode == "scatter":
        return src.at[peer_slot], landing.at[me]
    return src, landing


def _plane_start(name, arrs, mode, after):
    n = len(arrs)
    n_peers = 1 if mode == "swap" else 3
    if mode == "gather":
        srcs, lands = [], list(arrs)
    else:
        srcs, lands = list(arrs), [lax.empty(a.shape, a.dtype) for a in arrs]
    n_src = len(srcs)

    def body(*refs):
        landing = refs[n_src:n_src + n]
        sources = refs[:n_src] if n_src else landing
        send_sems, recv_sems = refs[n_src + n + 1], refs[n_src + n + 2]
        token = refs[-1]
        me, peers = _split_peers(mode)
        for i in range(n):
            for k, (peer, peer_slot) in enumerate(peers):
                src, dst = _split_refs(mode, sources[i], landing[i], me, peer_slot)
                pltpu.make_async_remote_copy(
                    src_ref=src, dst_ref=dst, send_sem=send_sems.at[n_peers * i + k],
                    recv_sem=recv_sems.at[n_peers * i + k], device_id=peer, device_id_type=MESH).start()
        token[...] = jnp.zeros_like(token)

    hbm_in = [pltpu.with_memory_space_constraint(a, pltpu.HBM) for a in srcs + lands]
    out = pl.pallas_call(
        body, name=name,
        out_shape=[pltpu.SemaphoreType.DMA((n_peers * n,)), pltpu.SemaphoreType.DMA((n_peers * n,))]
                  + [pltpu.HBM(a.shape, a.dtype) for a in lands]
                  + [jax.ShapeDtypeStruct((8, LANES), F32)],
        in_specs=[HBM_SPEC] * (n_src + n) + [pl.BlockSpec(memory_space=pl.ANY)],
        out_specs=[SEM_SPEC, SEM_SPEC] + [HBM_SPEC] * n + [pl.BlockSpec(memory_space=pltpu.VMEM)],
        input_output_aliases={n_src + i: 2 + i for i in range(n)},
        compiler_params=pltpu.CompilerParams(has_side_effects=DATAFLOW),
    )(*hbm_in, after)
    return out[:-1], out[-1]


def _plane_wait(name, state, mode, after):
    send_sems, recv_sems = state[0], state[1]
    bufs = list(state[2:])
    n = len(bufs)
    n_peers = 1 if mode == "swap" else 3

    def body(*refs):
        landing = refs[:n]
        send_sems, recv_sems = refs[n], refs[n + 1]
        _, peers = _split_peers(mode)
        for i in range(n):
            for k, (peer, peer_slot) in enumerate(peers):
                arrived = landing[i] if mode == "swap" else landing[i].at[peer_slot]
                cp = pltpu.make_async_remote_copy(
                    src_ref=arrived, dst_ref=arrived, send_sem=send_sems.at[n_peers * i + k],
                    recv_sem=recv_sems.at[n_peers * i + k], device_id=peer, device_id_type=MESH)
                cp.wait_send()
                cp.wait_recv()

    out = pl.pallas_call(
        body, name=name,
        out_shape=[pltpu.HBM(a.shape, a.dtype) for a in bufs],
        in_specs=[HBM_SPEC] * n + [SEM_SPEC, SEM_SPEC, pl.BlockSpec(memory_space=pl.ANY)],
        out_specs=[HBM_SPEC] * n,
        input_output_aliases={i: i for i in range(n)},
        compiler_params=pltpu.CompilerParams(has_side_effects=DATAFLOW),
    )(*bufs, send_sems, recv_sems, after)
    return list(out)


def _tile_2d(rows, cols, row_mult, max_elems=512 * 1024):
    if rows % row_mult == 0:
        rt = _pick_tile(rows, max(row_mult, max_elems // cols), row_mult)
        return (rt, cols), rows // rt, lambda i: (i, 0)
    ct = _pick_tile(cols, max(LANES, max_elems // rows), LANES)
    return (rows, ct), cols // ct, lambda i: (0, i)


def _cast_into_slot(a, slot, name):
    block, steps, index = _tile_2d(a.shape[0], a.shape[1], 16)

    def body(slot_ref, a_ref, o_ref):
        o_ref[...] = a_ref[...].astype(BF16)

    return pl.pallas_call(
        body, name=name,
        grid_spec=pltpu.PrefetchScalarGridSpec(
            num_scalar_prefetch=1, grid=(steps,),
            in_specs=[pl.BlockSpec(block, lambda i, s: index(i))],
            out_specs=pl.BlockSpec((None,) + block, lambda i, s: (s[0],) + index(i))),
        out_shape=jax.ShapeDtypeStruct((4,) + a.shape, BF16),
        compiler_params=_params(("arbitrary",)),
    )(slot, a)


def _sum_slots(buf, name, own=None, slot=None):
    n_slots, rows, cols = buf.shape
    (br, bc), steps, index = _tile_2d(rows, cols, 16, 320 * 1024)

    def body(*refs):
        if own is None:
            b_ref, o_ref = refs
        else:
            slot_ref, b_ref, own_ref, o_ref = refs
        acc = None
        for s in range(n_slots):
            term = b_ref[s] if own is None else jnp.where(slot_ref[0] == s, own_ref[...], b_ref[s])
            acc = term.astype(F32) if acc is None else acc + term.astype(F32)
        o_ref[...] = acc

    out_shape = jax.ShapeDtypeStruct((rows, cols), F32)
    if own is None:
        return pl.pallas_call(
            body, name=name, grid=(steps,),
            in_specs=[pl.BlockSpec((n_slots, br, bc), lambda i: (0,) + index(i))],
            out_specs=pl.BlockSpec((br, bc), index), out_shape=out_shape,
            compiler_params=_params(("parallel",), 48),
        )(buf)
    return pl.pallas_call(
        body, name=name,
        grid_spec=pltpu.PrefetchScalarGridSpec(
            num_scalar_prefetch=1, grid=(steps,),
            in_specs=[pl.BlockSpec((n_slots, br, bc), lambda i, s: (0,) + index(i)),
                      pl.BlockSpec((None, br, bc), lambda i, s: (s[0],) + index(i))],
            out_specs=pl.BlockSpec((br, bc), lambda i, s: index(i))),
        out_shape=out_shape,
        compiler_params=_params(("arbitrary",), 48),
    )(slot, buf, own)


def _sum_small(bufs, name):
    n = len(bufs)

    def body(*refs):
        for b_ref, o_ref in zip(refs[:n], refs[n:]):
            acc = b_ref[0]
            for s in range(1, b_ref.shape[0]):
                acc = acc + b_ref[s]
            o_ref[...] = acc

    vmem = pl.BlockSpec(memory_space=pltpu.VMEM)
    return pl.pallas_call(
        body, name=name, in_specs=[vmem] * n, out_specs=[vmem] * n,
        out_shape=[jax.ShapeDtypeStruct(b.shape[1:], b.dtype) for b in bufs],
    )(*bufs)


def _adam_update(w, g, m, v):
    c1 = 1.0 - ADAM_B1 ** ADAM_STEP
    c2 = 1.0 - ADAM_B2 ** ADAM_STEP
    m_new = ADAM_B1 * m + (1.0 - ADAM_B1) * g
    v_new = ADAM_B2 * v + (1.0 - ADAM_B2) * (g * g)
    m_hat = m_new / c1
    v_hat = v_new / c2
    return -ADAM_LR * (m_hat / (jnp.sqrt(v_hat) + ADAM_EPS) + ADAM_WD * w), m_new, v_new


def _adamw_small(params, slot, name):
    n = len(params)

    def spec_of(shape):
        lead = (None,) * (len(shape) - 2)
        return pl.BlockSpec(lead + tuple(shape[-2:]), lambda i, s, k=len(shape): (0,) * k)

    in_specs, operands, out_specs, out_shape = [], [], [], []
    for w, g, m, v in params:
        shard = g.shape[-1] != w.shape[-1]
        g_spec = pl.BlockSpec(tuple(w.shape[-2:]), (lambda i, s: (0, s[0])) if shard else (lambda i, s: (0, 0)))
        in_specs += [spec_of(w.shape), g_spec, spec_of(m.shape), spec_of(v.shape)]
        operands += [w, g, m, v]
        out_specs += [spec_of(w.shape)] * 4
        out_shape += [jax.ShapeDtypeStruct(w.shape, F32)] * 4

    def body(slot_ref, *refs):
        ins, outs = refs[:4 * n], refs[4 * n:]
        for p in range(n):
            w_ref, g_ref, m_ref, v_ref = ins[4 * p:4 * p + 4]
            g = g_ref[...]
            delta, m_new, v_new = _adam_update(w_ref[...], g, m_ref[...], v_ref[...])
            for o_ref, val in zip(outs[4 * p:4 * p + 4], (g, delta, m_new, v_new)):
                o_ref[...] = val

    out = pl.pallas_call(
        body, name=name,
        grid_spec=pltpu.PrefetchScalarGridSpec(num_scalar_prefetch=1, grid=(1,), in_specs=in_specs, out_specs=out_specs),
        out_shape=out_shape,
    )(slot, *operands)
    return [tuple(out[4 * p:4 * p + 4]) for p in range(n)]


def _adamw(w, grads, m, v, name, grad_row=0):
    rows, cols = w.shape
    (rt, _), _, _ = _tile_2d(rows, cols, 8, 160 * 1024)
    assert grad_row % rt == 0
    n_g = len(grads)

    def body(*refs):
        w_ref = refs[0]
        g_refs = refs[1:1 + n_g]
        m_ref, v_ref, g_out, d_out, m_out, v_out = refs[1 + n_g:]
        g = g_refs[0][...]
        for r in g_refs[1:]:
            g = g + r[...]
        g_out[...] = g
        d_out[...], m_out[...], v_out[...] = _adam_update(w_ref[...], g, m_ref[...], v_ref[...])

    spec = pl.BlockSpec((rt, cols), lambda i: (i, 0))
    grad_spec = pl.BlockSpec((rt, cols), lambda i: (i + grad_row // rt, 0))
    shape = jax.ShapeDtypeStruct((rows, cols), F32)
    return pl.pallas_call(
        body, name=name, grid=(rows // rt,),
        in_specs=[spec] + [grad_spec] * n_g + [spec] * 2, out_specs=[spec] * 4, out_shape=[shape] * 4,
        compiler_params=_params(("parallel",), 48),
    )(w, *grads, m, v)


def _weight_pieces():
    pieces = []
    for j in range(N_CONV_TILES):
        for g in range(4):
            pieces.append((512 * j + 128 * g, D * g + 128 * j, 128))
    for hd in range(N_HEADS):
        base = W_CONV + 512 * hd
        pieces.append((base, OFF_Q + HEAD_K * hd, HEAD_K))
        pieces.append((base + HEAD_K, OFF_K + HEAD_K * hd, HEAD_K))
        pieces.append((base + 2 * HEAD_K, OFF_V + HEAD_V * hd, HEAD_V))
    pieces.append((W_CONV + W_GLA, OFF_R, D))
    pieces.append((W_CONV + W_GLA + D, OFF_MA, 2 * D))
    return pieces


N_WEIGHT_COPIES = len(_weight_pieces()) + 1


def _load_weights(w_hbm, w_s, wlr_s, sems):
    copies = [pltpu.make_async_copy(w_hbm.at[pl.ds(src, n)], w_s.at[pl.ds(dst, n)], sems.at[i])
              for i, (dst, src, n) in enumerate(_weight_pieces())]
    copies.append(pltpu.make_async_copy(w_hbm.at[pl.ds(OFF_LR, LANES)], wlr_s, sems.at[N_WEIGHT_COPIES - 1]))
    for cp in copies:
        cp.start()
    for cp in copies:
        cp.wait()


def _in_proj(h, g_pre, w_full_t):
    t_rows = h.shape[0]
    tm = _pick_tile(t_rows, 384, LANES)
    n_main = N_MAIN

    def body(h_ref, g_ref, w_hbm, proj_ref, ut_ref, lr_ref, w_s, wlr_s, w_sems):
        @pl.when(pl.program_id(0) == 0)
        def _():
            _load_weights(w_hbm, w_s, wlr_s, w_sems)

        hh = h_ref[...]
        rstd = lax.rsqrt(jnp.mean(hh * hh, axis=-1, keepdims=True) + EPS)
        uf = hh * rstd * g_ref[...]
        u = uf.astype(BF16)
        ut_ref[...] = jnp.transpose(uf).astype(BF16)
        lr_ref[...] = _dot_nt(u, wlr_s[...])
        for j in range(n_main // D):
            cols = slice(j * D, (j + 1) * D)
            proj_ref[:, cols] = _dot_nt(u, w_s[cols, :]).astype(BF16)

    return pl.pallas_call(
        body, name="in_proj", grid=(t_rows // tm,),
        in_specs=[pl.BlockSpec((tm, D), lambda i: (i, 0)),
                  pl.BlockSpec((1, D), lambda i: (0, 0)),
                  pl.BlockSpec(memory_space=pl.ANY)],
        out_specs=[pl.BlockSpec((tm, n_main), lambda i: (i, 0)),
                   pl.BlockSpec((D, tm), lambda i: (0, i)),
                   pl.BlockSpec((tm, LANES), lambda i: (i, 0))],
        out_shape=[jax.ShapeDtypeStruct((t_rows, n_main), BF16),
                   jax.ShapeDtypeStruct((D, t_rows), BF16),
                   jax.ShapeDtypeStruct((t_rows, LANES), F32)],
        scratch_shapes=[pltpu.VMEM((n_main, D), BF16), pltpu.VMEM((LANES, D), BF16),
                        pltpu.SemaphoreType.DMA((N_WEIGHT_COPIES,))],
        compiler_params=_params(("arbitrary",), 56),
    )(h, g_pre, w_full_t)


CONV_TILES_PER_STEP = 2


def _conv_parts(p_ref, w_ref, t):
    cb = p_ref[:, 512 * t:512 * t + 128].astype(F32)
    cc = p_ref[:, 512 * t + 128:512 * t + 256].astype(F32)
    cx = p_ref[:, 512 * t + 256:512 * t + 384].astype(F32)
    cz = p_ref[:, 512 * t + 384:512 * t + 512].astype(F32)
    rows = cb.shape[0]
    w = w_ref[:, 128 * t:128 * (t + 1)]
    p = cc * cx
    conv = pltpu.roll(p, 1, 0) * w[0:1] + p * w[1:2] + pltpu.roll(p, rows - 1, 0) * w[2:3]
    sz = _sigmoid(cz)
    return cb, cc, cx, cz, p, conv, sz, w


def _conv_fwd(proj, conv_w, n_seq, lf):
    per = CONV_TILES_PER_STEP

    def body(p_ref, w_ref, y_ref):
        for t in range(per):
            cb, _, _, cz, _, conv, sz, _ = _conv_parts(p_ref, w_ref, t)
            y_ref[:, 128 * t:128 * (t + 1)] = (cb * conv * (cz * sz)).astype(BF16)

    return pl.pallas_call(
        body, name="conv_fwd", grid=(n_seq, N_CONV_TILES // per),
        in_specs=[pl.BlockSpec((lf, 512 * per), lambda b, j: (b, j)),
                  pl.BlockSpec((3, 128 * per), lambda b, j: (0, j))],
        out_specs=pl.BlockSpec((lf, 128 * per), lambda b, j: (b, j)),
        out_shape=jax.ShapeDtypeStruct((n_seq * lf, D), BF16),
        compiler_params=_params(("parallel", "parallel"), 48),
    )(proj, conv_w)


def _conv_bwd(proj, conv_w, dyc, n_seq, lf):
    per = CONV_TILES_PER_STEP

    def body(p_ref, w_ref, dy_ref, dp_ref, dw_ref):
        for t in range(per):
            cb, cc, cx, cz, p, conv, sz, w = _conv_parts(p_ref, w_ref, t)
            rows = cb.shape[0]
            dy = dy_ref[:, 128 * t:128 * (t + 1)].astype(F32)
            silu = cz * sz
            dcb = dy * conv * silu
            dconv = dy * cb * silu
            dcz = dy * cb * conv * (sz * (1.0 + cz * (1.0 - sz)))
            d_next = pltpu.roll(dconv, rows - 1, 0)
            d_prev = pltpu.roll(dconv, 1, 0)
            dp = d_next * w[0:1] + dconv * w[1:2] + d_prev * w[2:3]
            base = 512 * t
            dp_ref[:, base:base + 128] = dcb.astype(BF16)
            dp_ref[:, base + 128:base + 256] = (dp * cx).astype(BF16)
            dp_ref[:, base + 256:base + 384] = (dp * cc).astype(BF16)
            dp_ref[:, base + 384:base + 512] = dcz.astype(BF16)
            lanes = slice(128 * t, 128 * (t + 1))
            dw_ref[0:1, lanes] = jnp.sum(dconv * pltpu.roll(p, 1, 0), axis=0, keepdims=True)
            dw_ref[1:2, lanes] = jnp.sum(dconv * p, axis=0, keepdims=True)
            dw_ref[2:3, lanes] = jnp.sum(dconv * pltpu.roll(p, rows - 1, 0), axis=0, keepdims=True)

    return pl.pallas_call(
        body, name="conv_bwd", grid=(n_seq, N_CONV_TILES // per),
        in_specs=[pl.BlockSpec((lf, 512 * per), lambda b, j: (b, j)),
                  pl.BlockSpec((3, 128 * per), lambda b, j: (0, j)),
                  pl.BlockSpec((lf, 128 * per), lambda b, j: (b, j))],
        out_specs=[pl.BlockSpec((lf, 512 * per), lambda b, j: (b, j)),
                   pl.BlockSpec((None, 3, 128 * per), lambda b, j: (b, 0, j))],
        out_shape=[jax.ShapeDtypeStruct((n_seq * lf, W_CONV), BF16),
                   jax.ShapeDtypeStruct((n_seq, 3, D), F32)],
        compiler_params=_params(("parallel", "parallel"), 48),
    )(proj, conv_w, dyc)


GROUP = 3
GROUP_ROWS = GROUP * CHUNK


def _row_group(shape):
    row = lax.broadcasted_iota(jnp.int32, shape, 0)
    grp = jnp.zeros(shape, jnp.int32)
    for r in range(1, GROUP):
        grp = grp + (row >= r * CHUNK).astype(jnp.int32)
    return grp


def _lane_group(shape, width):
    lane = lax.broadcasted_iota(jnp.int32, shape, 1)
    grp = jnp.zeros(shape, jnp.int32)
    for r in range(1, GROUP):
        grp = grp + (lane >= r * width).astype(jnp.int32)
    return grp


def _score_mask(direction):
    shape = (GROUP_ROWS, GROUP_ROWS)
    row = lax.broadcasted_iota(jnp.int32, shape, 0)
    col = lax.broadcasted_iota(jnp.int32, shape, 1)
    same = _row_group(shape) == _lane_group(shape, CHUNK)
    return same & ((col <= row) if direction == 0 else (col > row))


def _diag_blocks(v):
    w = v.shape[1]
    wide = jnp.concatenate([v] * GROUP, axis=1)
    return jnp.where(_row_group(wide.shape) == _lane_group(wide.shape, w), wide, jnp.zeros_like(wide))


def _per_chunk_dot(lhs, state, transposed):
    outs = []
    for r in range(GROUP):
        rows = lhs[r * CHUNK:(r + 1) * CHUNK, :]
        blk = state[:, r * HEAD_K:(r + 1) * HEAD_K]
        outs.append(_dot_nt(rows, blk) if transposed else _dot(rows, blk))
    return jnp.concatenate(outs, axis=0)


def _chunk_cumsum(v, suffix):
    pos = lax.broadcasted_iota(jnp.int32, v.shape, 0) & (CHUNK - 1)
    shift = 1
    while shift < CHUNK:
        if suffix:
            moved = pltpu.roll(v, GROUP_ROWS - shift, 0)
            v = v + jnp.where(pos < CHUNK - shift, moved, 0.0)
        else:
            moved = pltpu.roll(v, shift, 0)
            v = v + jnp.where(pos >= shift, moved, 0.0)
        shift *= 2
    return v


def _per_chunk_rows(rows_of_chunk):
    w = rows_of_chunk[0].shape[1]
    return jnp.concatenate([jnp.broadcast_to(v, (CHUNK, w)) for v in rows_of_chunk], axis=0)


def _chunk_end_rows(direction, b):
    at = CHUNK - 1 if direction == 0 else 0
    return [b[r * CHUNK + at:r * CHUNK + at + 1, :] for r in range(GROUP)]


def _gla_gates(lr_bf, wg_ref, bg_ref, lf):
    z = _dot(lr_bf, wg_ref[...]) + bg_ref[...]
    valid = lax.broadcasted_iota(jnp.int32, (lf, HEAD_K), 0) >= PAD_FRONT
    return z, valid


def _group_unroll(n_groups):
    return n_groups if n_groups <= 11 else 1


def _group_rows(g):
    return pl.ds(pl.multiple_of(g * GROUP_ROWS, GROUP_ROWS), GROUP_ROWS)


def _chunk_decay(direction, g, r, b_s):
    base = g * GROUP_ROWS + r * CHUNK
    if direction == 0:
        grp = b_s[pl.ds(pl.multiple_of(base + CHUNK - 8, 8), 8), :]
        return jnp.exp(grp[7:8, :])
    grp = b_s[pl.ds(pl.multiple_of(base, 8), 8), :]
    return jnp.exp(grp[0:1, :])


def _state_scan(direction, n_groups, b_s, st_s, reverse):
    ascending = (direction == 0) != reverse

    def step(i, carry):
        g = i if ascending else n_groups - 1 - i
        for rr in range(GROUP):
            r = rr if ascending else GROUP - 1 - rr
            lanes = slice(r * HEAD_K, (r + 1) * HEAD_K)
            decay = _chunk_decay(direction, g, r, b_s)
            local = st_s[g, :, lanes]
            st_s[g, :, lanes] = carry
            carry = (local + carry * decay) if reverse else (carry * decay + local)
        return carry

    lax.fori_loop(0, n_groups, step, jnp.zeros((HEAD_V, HEAD_K), F32), unroll=_group_unroll(n_groups))


def _gla_states(direction, n_groups, qkv_ref, g_s, b_s, st_s):
    def local(g, carry):
        rows = _group_rows(g)
        b = _chunk_cumsum(g_s[rows, :], direction == 1)
        b_s[rows, :] = b
        b_end = _per_chunk_rows(_chunk_end_rows(direction, b))
        k = qkv_ref[rows, 128:256].astype(F32)
        v = qkv_ref[rows, 256:512]
        k_dec = (k * jnp.exp(b_end - b)).astype(BF16)
        st_s[g] = _dot_tn(v, _diag_blocks(k_dec))
        return carry

    lax.fori_loop(0, n_groups, local, 0, unroll=_group_unroll(n_groups))
    _state_scan(direction, n_groups, b_s, st_s, False)


def _gla_fwd(proj, lr, wgf, wgb, bgf, bgb, n_seq, lf):
    assert lf % GROUP_ROWS == 0
    n_groups = lf // GROUP_ROWS
    scale = HEAD_K ** -0.5

    def body(qkv_ref, lr_ref, wgf_ref, wgb_ref, bgf_ref, bgb_ref, o_ref, g_s, b_s2, st_s2):
        lr_bf = lr_ref[...].astype(BF16)
        for direction in (0, 1):
            wg_ref, bg_ref = ((wgf_ref, bgf_ref), (wgb_ref, bgb_ref))[direction]
            z, valid = _gla_gates(lr_bf, wg_ref, bg_ref, lf)
            g_s[...] = jnp.where(valid, _log_sigmoid(z) / GATE_NORM, 0.0)
            _gla_states(direction, n_groups, qkv_ref, g_s, b_s2.at[direction], st_s2.at[direction])
        masks = [_score_mask(0), _score_mask(1)]

        def out(g, carry):
            rows = _group_rows(g)
            q = qkv_ref[rows, 0:128].astype(F32) * scale
            k = qkv_ref[rows, 128:256].astype(F32)
            v = qkv_ref[rows, 256:512]
            o = None
            for direction in (0, 1):
                b = b_s2[direction, rows, :]
                q_in = (q * jnp.exp(b)).astype(BF16)
                k_in = (k * jnp.exp(-b)).astype(BF16)
                s = jnp.where(masks[direction], _dot_nt(q_in, k_in), 0.0).astype(BF16)
                part = _dot(s, v) + _per_chunk_dot(q_in, st_s2[direction, g].astype(BF16), True)
                o = part if o is None else o + part
            o_ref[rows, :] = o
            return carry

        lax.fori_loop(0, n_groups, out, 0, unroll=_group_unroll(n_groups))

    return pl.pallas_call(
        body, name="gla_fwd", grid=(n_seq, N_HEADS),
        in_specs=[pl.BlockSpec((lf, 512), lambda b, h: (b, N_CONV_TILES + h)),
                  pl.BlockSpec((lf, LANES), lambda b, h: (b, 0)),
                  pl.BlockSpec((None, LANES, HEAD_K), lambda b, h: (h, 0, 0)),
                  pl.BlockSpec((None, LANES, HEAD_K), lambda b, h: (h, 0, 0)),
                  pl.BlockSpec((None, 1, HEAD_K), lambda b, h: (h, 0, 0)),
                  pl.BlockSpec((None, 1, HEAD_K), lambda b, h: (h, 0, 0))],
        out_specs=pl.BlockSpec((lf, HEAD_V), lambda b, h: (b, h)),
        out_shape=jax.ShapeDtypeStruct((n_seq * lf, D), F32),
        scratch_shapes=[pltpu.VMEM((lf, HEAD_K), F32), pltpu.VMEM((2, lf, HEAD_K), F32),
                        pltpu.VMEM((2, n_groups, HEAD_V, GROUP * HEAD_K), F32)],
        compiler_params=_params(("parallel", "parallel"), 48),
    )(proj, lr, wgf, wgb, bgf, bgb)


def _gla_bwd(proj, lr, d_o, wgf, wgb, bgf, bgb, n_seq, lf, token):
    assert lf % GROUP_ROWS == 0
    n_groups = lf // GROUP_ROWS
    scale = HEAD_K ** -0.5

    def body(qkv_ref, lr_ref, do_ref, wgf_ref, wgb_ref, bgf_ref, bgb_ref, token_ref,
             dqkv_ref, dlr_ref, dwgf_ref, dwgb_ref, dbg_ref,
             g_s, b_s2, fac_s2, dg_s2, st_s2, dst_s2):
        lr_bf = lr_ref[...].astype(BF16)
        gates = ((wgf_ref, bgf_ref), (wgb_ref, bgb_ref))
        for direction in (0, 1):
            wg_ref, bg_ref = gates[direction]
            b_s, st_s, dst_s = b_s2.at[direction], st_s2.at[direction], dst_s2.at[direction]
            z, valid = _gla_gates(lr_bf, wg_ref, bg_ref, lf)
            g_s[...] = jnp.where(valid, _log_sigmoid(z) / GATE_NORM, 0.0)
            fac_s2[direction] = jnp.where(valid, _sigmoid(-z) / GATE_NORM, 0.0)
            _gla_states(direction, n_groups, qkv_ref, g_s, b_s, st_s)

            def state_grad_local(g, carry):
                rows = _group_rows(g)
                q = qkv_ref[rows, 0:128].astype(F32) * scale
                q_in = (q * jnp.exp(b_s[rows, :])).astype(BF16)
                dst_s[g] = _dot_tn(do_ref[rows, :], _diag_blocks(q_in))
                return carry

            lax.fori_loop(0, n_groups, state_grad_local, 0, unroll=_group_unroll(n_groups))
            _state_scan(direction, n_groups, b_s, dst_s, True)

        masks = [_score_mask(0), _score_mask(1)]

        def group_grads(g, carry):
            rows = _group_rows(g)
            q = qkv_ref[rows, 0:128].astype(F32) * scale
            k = qkv_ref[rows, 128:256].astype(F32)
            v = qkv_ref[rows, 256:512]
            d_out = do_ref[rows, :]
            dq_sum = dk_sum = dv_sum = None
            for direction in (0, 1):
                end_row = CHUNK - 1 if direction == 0 else 0
                b = b_s2[direction, rows, :]
                ends = _chunk_end_rows(direction, b)
                b_end = _per_chunk_rows(ends)
                e_pos = jnp.exp(b)
                e_neg = jnp.exp(-b)
                e_end = jnp.exp(b_end - b)
                q_in = q * e_pos
                k_in = k * e_neg
                k_dec = k * e_end
                q_in_bf = q_in.astype(BF16)
                k_in_bf = k_in.astype(BF16)
                state = st_s2[direction, g]
                d_state = dst_s2[direction, g]
                state_bf = state.astype(BF16)
                d_state_bf = d_state.astype(BF16)
                s = jnp.where(masks[direction], _dot_nt(q_in_bf, k_in_bf), 0.0).astype(BF16)
                ds = jnp.where(masks[direction], _dot_nt(d_out, v), 0.0).astype(BF16)
                dv = _dot_tn(s, d_out) + _per_chunk_dot(k_dec.astype(BF16), d_state_bf, True)
                dq_in = _dot(ds, k_in_bf) + _per_chunk_dot(d_out, state_bf, False)
                dk_in = _dot_tn(ds, q_in_bf)
                dk_dec = _per_chunk_dot(v, d_state_bf, False)
                dq = dq_in * e_pos * scale
                dk = dk_in * e_neg + dk_dec * e_end
                dq_sum = dq if dq_sum is None else dq_sum + dq
                dk_sum = dk if dk_sum is None else dk_sum + dk
                dv_sum = dv if dv_sum is None else dv_sum + dv
                dkk = dk_dec * k_dec
                db = dq_in * q_in - dk_in * k_in - dkk
                d_decay = jnp.sum(d_state * state, axis=0, keepdims=True)
                db_end = [jnp.sum(dkk[r * CHUNK:(r + 1) * CHUNK, :], axis=0, keepdims=True)
                          + d_decay[:, r * HEAD_K:(r + 1) * HEAD_K] * jnp.exp(ends[r]) for r in range(GROUP)]
                row = lax.broadcasted_iota(jnp.int32, (GROUP_ROWS, HEAD_K), 0)
                at_end = row == end_row
                for r in range(1, GROUP):
                    at_end = at_end | (row == r * CHUNK + end_row)
                db = db + jnp.where(at_end, _per_chunk_rows(db_end), 0.0)
                dg_s2[direction, rows, :] = _chunk_cumsum(db, direction == 0)
            dqkv_ref[rows, 0:128] = dq_sum.astype(BF16)
            dqkv_ref[rows, 128:256] = dk_sum.astype(BF16)
            dqkv_ref[rows, 256:512] = dv_sum.astype(BF16)
            return carry

        lax.fori_loop(0, n_groups, group_grads, 0, unroll=_group_unroll(n_groups))

        dlr = jnp.zeros((lf, LANES), F32)
        for direction in (0, 1):
            dz = dg_s2[direction] * fac_s2[direction]
            dz_bf = dz.astype(BF16)
            dbg_ref[direction:direction + 1, :] = jnp.sum(dz, axis=0, keepdims=True)
            (dwgf_ref, dwgb_ref)[direction][...] = _dot_tn(lr_bf, dz_bf)
            dlr = dlr + _dot_nt(dz_bf, gates[direction][0][...])

        @pl.when(pl.program_id(1) == 0)
        def _():
            dlr_ref[...] = dlr

        @pl.when(pl.program_id(1) != 0)
        def _():
            dlr_ref[...] = dlr_ref[...] + dlr

    gate_w = pl.BlockSpec((None, LANES, HEAD_K), lambda b, h: (h, 0, 0))
    gate_b = pl.BlockSpec((None, 1, HEAD_K), lambda b, h: (h, 0, 0))
    return pl.pallas_call(
        body, name="gla_bwd", grid=(n_seq, N_HEADS),
        in_specs=[pl.BlockSpec((lf, 512), lambda b, h: (b, N_CONV_TILES + h)),
                  pl.BlockSpec((lf, LANES), lambda b, h: (b, 0)),
                  pl.BlockSpec((lf, HEAD_V), lambda b, h: (b, h)),
                  gate_w, gate_w, gate_b, gate_b,
                  pl.BlockSpec((8, LANES), lambda b, h: (0, 0))],
        out_specs=[pl.BlockSpec((lf, 512), lambda b, h: (b, h)),
                   pl.BlockSpec((lf, LANES), lambda b, h: (b, 0)),
                   pl.BlockSpec((None, None, LANES, HEAD_K), lambda b, h: (b, h, 0, 0)),
                   pl.BlockSpec((None, None, LANES, HEAD_K), lambda b, h: (b, h, 0, 0)),
                   pl.BlockSpec((None, None, 2, HEAD_K), lambda b, h: (b, h, 0, 0))],
        out_shape=[jax.ShapeDtypeStruct((n_seq * lf, W_GLA), BF16),
                   jax.ShapeDtypeStruct((n_seq * lf, LANES), F32),
                   jax.ShapeDtypeStruct((n_seq, N_HEADS, LANES, HEAD_K), F32),
                   jax.ShapeDtypeStruct((n_seq, N_HEADS, LANES, HEAD_K), F32),
                   jax.ShapeDtypeStruct((n_seq, N_HEADS, 2, HEAD_K), F32)],
        scratch_shapes=[pltpu.VMEM((lf, HEAD_K), F32), pltpu.VMEM((2, lf, HEAD_K), F32),
                        pltpu.VMEM((2, lf, HEAD_K), F32), pltpu.VMEM((2, lf, HEAD_K), F32),
                        pltpu.VMEM((2, n_groups, HEAD_V, GROUP * HEAD_K), F32),
                        pltpu.VMEM((2, n_groups, HEAD_V, GROUP * HEAD_K), F32)],
        compiler_params=_params(("parallel", "arbitrary"), 56),
    )(proj, lr, d_o, wgf, wgb, bgf, bgb, token)


def _tail(h, tgt, yc, o, proj, w3, gamma, g_post, lf):
    t_rows = h.shape[0]
    tm = _pick_tile(t_rows, 256, CHUNK)
    n_chunks = lf // CHUNK
    per_tile = tm // CHUNK

    def body(h_ref, tgt_ref, yc_ref, o_ref, r_ref, ma_ref, mb_ref, w_hbm, gamma_ref, gpost_ref,
             dres_ref, yg_ref, merged_ref, dout_ref, dpc_ref, dpg_ref, dyc_ref, do_ref, dtail_ref,
             loss_ref, dgpost_ref, dgamma_ref, w_s, w_sem):
        i = pl.program_id(0)

        @pl.when(i == 0)
        def _():
            cp = pltpu.make_async_copy(w_hbm, w_s, w_sem)
            cp.start()
            cp.wait()
            loss_ref[...] = jnp.zeros_like(loss_ref)
            dgpost_ref[...] = jnp.zeros_like(dgpost_ref)
            dgamma_ref[...] = jnp.zeros_like(dgamma_ref)

        gamma = gamma_ref[...]
        o = o_ref[...]
        r = r_ref[...].astype(F32)
        sr = _sigmoid(r)
        silu_r = r * sr
        n_parts, rstd_parts = [], []
        for hd in range(N_HEADS):
            oh = o[:, hd * HEAD_V:(hd + 1) * HEAD_V]
            rstd = lax.rsqrt(jnp.mean(oh * oh, axis=-1, keepdims=True) + EPS)
            n_parts.append(oh * rstd)
            rstd_parts.append(rstd)
        n = jnp.concatenate(n_parts, axis=-1)
        gamma_t = jnp.concatenate([gamma] * N_HEADS, axis=-1)
        yg = n * gamma_t * silu_r
        yg_bf = yg.astype(BF16)
        yg_ref[...] = yg_bf
        yc = yc_ref[...]
        pc = _dot(yc, w_s[0])
        pg = _dot(yg_bf, w_s[1])
        sa = _sigmoid(ma_ref[...].astype(F32))
        sb = _sigmoid(mb_ref[...].astype(F32))
        merged = (sa * pc + sb * pg).astype(BF16)
        merged_ref[...] = merged
        out = _dot(merged, w_s[2])
        rstd2 = lax.rsqrt(jnp.mean(out * out, axis=-1, keepdims=True) + EPS)
        nn = out * rstd2
        gpost = gpost_ref[...]
        y = h_ref[...] + nn * gpost

        rowi = lax.broadcasted_iota(jnp.int32, (tm, 1), 0)
        keep = jnp.zeros((tm, 1), F32)
        for kk in range(per_tile):
            is_tok = ((i * per_tile + kk) % n_chunks) != 0
            f = jnp.where(is_tok, 1.0, 0.0)
            keep = jnp.where((rowi >= kk * CHUNK) & (rowi < (kk + 1) * CHUNK), f, keep)
        diff = (y - tgt_ref[...]) * keep
        loss_ref[...] += jnp.sum(diff * diff) * (0.5 / D)
        dy = diff * (1.0 / D)
        dres_ref[...] = dy
        dgpost_ref[...] += jnp.sum(dy * nn, axis=0, keepdims=True)
        dn = dy * gpost
        dout_f = rstd2 * (dn - nn * jnp.mean(dn * nn, axis=-1, keepdims=True))
        dout = dout_f.astype(BF16)
        dout_ref[...] = jnp.transpose(dout_f).astype(BF16)
        dmerged = _dot_nt(dout, w_s[2])
        dpc_f = dmerged * sa
        dpg_f = dmerged * sb
        dpc = dpc_f.astype(BF16)
        dpg = dpg_f.astype(BF16)
        dpc_ref[...] = jnp.transpose(dpc_f).astype(BF16)
        dpg_ref[...] = jnp.transpose(dpg_f).astype(BF16)
        dtail_ref[:, D:2 * D] = (dmerged * pc * (sa * (1.0 - sa))).astype(BF16)
        dtail_ref[:, 2 * D:3 * D] = (dmerged * pg * (sb * (1.0 - sb))).astype(BF16)
        dyc_ref[...] = _dot_nt(dpc, w_s[0]).astype(BF16)
        dyg = _dot_nt(dpg, w_s[1])
        dtail_ref[:, 0:D] = (dyg * n * gamma_t * (sr * (1.0 + r * (1.0 - sr)))).astype(BF16)
        dgam_full = jnp.sum(dyg * n * silu_r, axis=0, keepdims=True)
        dgam = dgam_full[:, 0:HEAD_V]
        for hd in range(1, N_HEADS):
            dgam = dgam + dgam_full[:, hd * HEAD_V:(hd + 1) * HEAD_V]
        dgamma_ref[...] += dgam
        dng = dyg * gamma_t * silu_r
        do_parts = []
        for hd in range(N_HEADS):
            sl = slice(hd * HEAD_V, (hd + 1) * HEAD_V)
            dnh = dng[:, sl]
            nh = n_parts[hd]
            do_parts.append(rstd_parts[hd] * (dnh - nh * jnp.mean(dnh * nh, axis=-1, keepdims=True)))
        do_ref[...] = jnp.concatenate(do_parts, axis=-1).astype(BF16)

    row = lambda c: pl.BlockSpec((tm, D), lambda i: (i, c))
    col = pl.BlockSpec((D, tm), lambda i: (0, i))
    const = lambda shape: pl.BlockSpec(shape, lambda i: (0, 0))
    act = jax.ShapeDtypeStruct((t_rows, D), BF16)
    act_t = jax.ShapeDtypeStruct((D, t_rows), BF16)
    return pl.pallas_call(
        body, name="tail", grid=(t_rows // tm,),
        in_specs=[row(0), row(0), row(0), row(0), row(6), row(7), row(8),
                  pl.BlockSpec(memory_space=pl.ANY), const((1, HEAD_V)), const((1, D))],
        out_specs=[row(0)] * 3 + [col] * 3 + [row(0)] * 2
                  + [pl.BlockSpec((tm, W_TAIL), lambda i: (i, 0)),
                     const((8, LANES)), const((1, D)), const((1, HEAD_V))],
        out_shape=[jax.ShapeDtypeStruct((t_rows, D), F32)] + [act] * 2 + [act_t] * 3 + [act] * 2
                  + [jax.ShapeDtypeStruct((t_rows, W_TAIL), BF16),
                     jax.ShapeDtypeStruct((8, LANES), F32),
                     jax.ShapeDtypeStruct((1, D), F32),
                     jax.ShapeDtypeStruct((1, HEAD_V), F32)],
        scratch_shapes=[pltpu.VMEM((3, D, D), BF16), pltpu.SemaphoreType.DMA],
        compiler_params=_params(("arbitrary",), 56),
    )(h, tgt, yc, o, proj, proj, proj, w3, gamma, g_post)


def _wgrad_t(a_t, b, name, out_dtype=BF16):
    m, t_rows = a_t.shape
    n = b.shape[1]
    tn = D if n % D == 0 else n
    tk = _pick_tile(t_rows, 768, LANES)
    n_k = t_rows // tk

    def body(a_ref, b_ref, o_ref, acc):
        k = pl.program_id(1)

        @pl.when(k == 0)
        def _():
            acc[...] = jnp.zeros_like(acc)

        acc[...] += _dot(a_ref[...], b_ref[...].astype(BF16))

        @pl.when(k == n_k - 1)
        def _():
            o_ref[...] = jnp.transpose(acc[...]).astype(out_dtype)

    return pl.pallas_call(
        body, name=name, grid=(n // tn, n_k),
        in_specs=[pl.BlockSpec((m, tk), lambda j, k: (0, k)),
                  pl.BlockSpec((tk, tn), lambda j, k: (k, j))],
        out_specs=pl.BlockSpec((tn, m), lambda j, k: (j, 0)),
        out_shape=jax.ShapeDtypeStruct((n, m), out_dtype),
        scratch_shapes=[pltpu.VMEM((m, tn), F32)],
        compiler_params=_params(("parallel", "arbitrary"), 48),
    )(a_t, b)


def _dgrad_in(dpc, dpg, dpt, dlr, w_full_t, h, g_pre, dres, token):
    t_rows = h.shape[0]
    tm = _pick_tile(t_rows, 384, 16)
    n_main = N_MAIN

    def body(dpc_ref, dpg_ref, dpt_ref, dlr_ref, w_hbm, h_ref, g_ref, dres_ref, token_ref,
             dh_ref, dg_ref, w_s, wlr_s, w_sems):
        @pl.when(pl.program_id(0) == 0)
        def _():
            _load_weights(w_hbm, w_s, wlr_s, w_sems)
            dg_ref[...] = jnp.zeros_like(dg_ref)

        du = _dot(dlr_ref[...].astype(BF16), wlr_s[...])
        du += _dot(dpc_ref[...], w_s[0:W_CONV, :])
        du += _dot(dpg_ref[...], w_s[W_CONV:W_CONV + W_GLA, :])
        du += _dot(dpt_ref[...], w_s[W_CONV + W_GLA:n_main, :])
        hh = h_ref[...]
        rstd = lax.rsqrt(jnp.mean(hh * hh, axis=-1, keepdims=True) + EPS)
        xhat = hh * rstd
        dg_ref[...] += jnp.sum(du * xhat, axis=0, keepdims=True)
        dx = du * g_ref[...]
        dh_ref[...] = rstd * (dx - xhat * jnp.mean(dx * xhat, axis=-1, keepdims=True)) + dres_ref[...]

    row = lambda width: pl.BlockSpec((tm, width), lambda i: (i, 0))
    return pl.pallas_call(
        body, name="dgrad_in", grid=(t_rows // tm,),
        in_specs=[row(W_CONV), row(W_GLA), row(W_TAIL), row(LANES),
                  pl.BlockSpec(memory_space=pl.ANY),
                  row(D), pl.BlockSpec((1, D), lambda i: (0, 0)), row(D),
                  pl.BlockSpec((8, LANES), lambda i: (0, 0))],
        out_specs=[row(D), pl.BlockSpec((1, D), lambda i: (0, 0))],
        out_shape=[jax.ShapeDtypeStruct((t_rows, D), F32), jax.ShapeDtypeStruct((1, D), F32)],
        scratch_shapes=[pltpu.VMEM((n_main, D), BF16), pltpu.VMEM((LANES, D), BF16),
                        pltpu.SemaphoreType.DMA((N_WEIGHT_COPIES,))],
        compiler_params=_params(("arbitrary",), 56),
    )(dpc, dpg, dpt, dlr, w_full_t, h, g_pre, dres, token)


def _reference_rows(g_conv, g_gla, g_tail, g_lr):
    conv = g_conv.reshape(N_CONV_TILES, 4, 128, D).transpose(1, 0, 2, 3).reshape(W_CONV, D)
    gla = g_gla.reshape(N_HEADS, 512, D)
    q = gla[:, 0:128].reshape(N_HEADS * HEAD_K, D)
    k = gla[:, 128:256].reshape(N_HEADS * HEAD_K, D)
    v = gla[:, 256:512].reshape(N_HEADS * HEAD_V, D)
    return jnp.concatenate([conv, q, k, v, g_tail[0:D], g_lr[0:2 * RANK], g_tail[D:3 * D]], axis=0)


def kernel(x, meta_tokens, norm_pre, w_in, conv_w, w_gate_fwd, b_gate_fwd, w_gate_bwd, b_gate_bwd, gla_norm, w_out_conv, w_out_gla, w_merge_out, norm_post, loss_target, m_meta_tokens, m_norm_pre, m_w_in, m_conv_w, m_w_gate_fwd, m_b_gate_fwd, m_w_gate_bwd, m_b_gate_bwd, m_gla_norm, m_w_out_conv, m_w_out_gla, m_w_merge_out, m_norm_post, v_meta_tokens, v_norm_pre, v_w_in, v_conv_w, v_w_gate_fwd, v_b_gate_fwd, v_w_gate_bwd, v_b_gate_bwd, v_gla_norm, v_w_out_conv, v_w_out_gla, v_w_merge_out, v_norm_post):
    n_seq, seq, _ = x.shape
    lf = CHUNK + seq
    t_rows = n_seq * lf
    shard = 2 * lax.axis_index("x") + lax.axis_index("y")
    shard_arr = jnp.reshape(shard, (1,)).astype(jnp.int32)

    w_in_slots = _cast_into_slot(jnp.transpose(w_in[0]), shard_arr, "cast_w_in")
    w_out_slots = _cast_into_slot(jnp.concatenate([w_out_conv[0], w_out_gla[0], w_merge_out[0]], axis=0), shard_arr,
                                  "cast_w_out")
    w_in_all, meta_all, conv_all, wgf_all, wgb_all = _gather_via_sibling(
        "gather_w_in", [w_in_slots, meta_tokens, conv_w[0], w_gate_fwd[0], w_gate_bwd[0]],
        (True, False, False, False, False))
    w_out_state, _ = _plane_start("gather_w_out_start", [w_out_slots], "gather", wgb_all)

    w_full_t = w_in_all.reshape(N_IN, D)
    meta_full = jnp.transpose(meta_all, (1, 0, 2)).reshape(N_META, D)
    conv_full = jnp.transpose(conv_all, (1, 0, 2)).reshape(3, D)
    wgf = jnp.pad(wgf_all, ((0, 0), (0, LANES - RANK), (0, 0))).astype(BF16)
    wgb = jnp.pad(wgb_all, ((0, 0), (RANK, LANES - 2 * RANK), (0, 0))).astype(BF16)
    bgf = b_gate_fwd.reshape(N_HEADS, 1, HEAD_K)
    bgb = b_gate_bwd.reshape(N_HEADS, 1, HEAD_K)

    head = jnp.concatenate([jnp.zeros((PAD_FRONT, D), F32), meta_full], axis=0)
    h = jnp.concatenate([jnp.broadcast_to(head[None], (n_seq, CHUNK, D)), x], axis=1).reshape(t_rows, D)
    tgt = jnp.pad(loss_target, ((0, 0), (CHUNK, 0), (0, 0))).reshape(t_rows, D)

    proj, u_t, lr = _in_proj(h, norm_pre, w_full_t)
    yc = _conv_fwd(proj, conv_full, n_seq, lf)
    o = _gla_fwd(proj, lr, wgf, wgb, bgf, bgb, n_seq, lf)
    (w_out_all,) = _plane_wait("gather_w_out_wait", w_out_state, "gather", o)
    w3 = jnp.transpose(w_out_all.reshape(4, 3, D // 4, D), (1, 0, 2, 3)).reshape(3, D, D)
    (dres, yg, merged, dout_t, dpc_t, dpg_t, dyc, d_o, dtail, loss_acc, d_gpost, d_gamma) = _tail(
        h, tgt, yc, o, proj, w3, gla_norm, norm_post, lf)
    g_w_oc = _wgrad_t(dpc_t, yc, "wgrad_out_conv")
    g_w_og = _wgrad_t(dpg_t, yg, "wgrad_out_gla")
    g_w_mo = _wgrad_t(dout_t, merged, "wgrad_merge_out")
    g_out_slots = jnp.concatenate([g.reshape(4, D // 4, D) for g in (g_w_oc, g_w_og, g_w_mo)], axis=1)
    out_state, out_token = _plane_start("scatter_out_grads_start", [g_out_slots], "scatter", g_w_mo)
    dgla, dlr, dwgf_p, dwgb_p, dbg_p = _gla_bwd(proj, lr, d_o, wgf, wgb, bgf, bgb, n_seq, lf, out_token)
    (got_out,) = _plane_wait("scatter_out_grads_wait", out_state, "scatter", dlr)
    dconv, dconvw_p = _conv_bwd(proj, conv_full, dyc, n_seq, lf)
    g_conv = _wgrad_t(u_t, dconv, "wgrad_in_conv")
    g_gla = _wgrad_t(u_t, dgla, "wgrad_in_gla")
    g_tail = _wgrad_t(u_t, dtail, "wgrad_in_tail")
    g_lr = _wgrad_t(u_t, dlr, "wgrad_in_lr")

    g_in_slots = _reference_rows(g_conv, g_gla, g_tail, g_lr).reshape(4, SHARD_IN, D)
    in_state, in_token = _plane_start("scatter_in_grads_start", [g_in_slots], "scatter", g_lr)
    dh, d_gpre = _dgrad_in(dconv, dgla, dtail, dlr, w_full_t, h, norm_pre, dres, in_token)
    (got_in,) = _plane_wait("scatter_in_grads_wait", in_state, "scatter", d_gpre)

    plane_in = _sum_slots(got_in, "sum_w_in_grads", own=g_in_slots, slot=shard_arr)
    plane_out = _sum_slots(got_out, "sum_w_out_grads", own=g_out_slots, slot=shard_arr)
    swap_state, swap_token = _plane_start("swap_plane_sums_start", [plane_in, plane_out], "swap", plane_out)

    dh3 = dh.reshape(n_seq, lf, D)
    grad_x = dh3[:, CHUNK:, :]

    d_meta = jnp.sum(dh3[:, PAD_FRONT:CHUNK, :], axis=0)
    d_convw = jnp.sum(dconvw_p, axis=0)
    d_wgf = jnp.transpose(jnp.sum(dwgf_p, axis=0)[:, 0:RANK, :], (1, 0, 2)).reshape(RANK, N_HEADS * HEAD_K)
    d_wgb = jnp.transpose(jnp.sum(dwgb_p, axis=0)[:, RANK:2 * RANK, :], (1, 0, 2)).reshape(RANK, N_HEADS * HEAD_K)
    d_bg = jnp.sum(dbg_p, axis=0)
    d_bgf = d_bg[:, 0, :].reshape(1, N_HEADS * HEAD_K)
    d_bgb = d_bg[:, 1, :].reshape(1, N_HEADS * HEAD_K)
    loss_part = loss_acc[0:1, :] + swap_token[0:1, :]
    partials = [d_meta, d_convw, d_wgf, d_wgb, d_gpre, d_bgf, d_bgb, d_gamma, d_gpost, loss_part]
    (g_meta, g_convw, g_wgf, g_wgb, g_npre, g_bgf, g_bgb, g_gnorm, g_npost, loss_row) = _sum_small(
        _gather_all("gather_small_grads", partials), "sum_small_grads")
    loss = loss_row[0, 0]
    small_out = _adamw_small(
        [(meta_tokens, g_meta, m_meta_tokens, v_meta_tokens), (norm_pre, g_npre, m_norm_pre, v_norm_pre),
         (conv_w, g_convw, m_conv_w, v_conv_w), (w_gate_fwd, g_wgf, m_w_gate_fwd, v_w_gate_fwd),
         (b_gate_fwd, g_bgf, m_b_gate_fwd, v_b_gate_fwd), (w_gate_bwd, g_wgb, m_w_gate_bwd, v_w_gate_bwd),
         (b_gate_bwd, g_bgb, m_b_gate_bwd, v_b_gate_bwd), (gla_norm, g_gnorm, m_gla_norm, v_gla_norm),
         (norm_post, g_npost, m_norm_post, v_norm_post)], shard_arr, "adamw_small")

    other_in, other_out = _plane_wait("swap_plane_sums_wait", swap_state, "swap", small_out[0][0])
    big_in = _adamw(jnp.transpose(w_in[0]), [plane_in, other_in], jnp.transpose(m_w_in[0]), jnp.transpose(v_w_in[0]),
                    "adamw_w_in")
    out_params = ((w_out_conv, m_w_out_conv, v_w_out_conv), (w_out_gla, m_w_out_gla, v_w_out_gla),
                  (w_merge_out, m_w_merge_out, v_w_merge_out))
    big_out = [_adamw(w[0], [plane_out, other_out], m[0], v[0], f"adamw_w_out_{i}", grad_row=i * (D // 4))
               for i, (w, m, v) in enumerate(out_params)]

    results = []
    for kind in range(4):
        small_kind = [p[kind] for p in small_out]
        w_in_part = jnp.transpose(big_in[kind])[None]
        outs3 = [big_out[i][kind][None] for i in range(3)]
        results.extend(small_kind[0:2] + [w_in_part] + small_kind[2:8] + outs3 + small_kind[8:9])
    return (loss, grad_x, *results)
```

```python
import functools

import jax
import jax.numpy as jnp
from jax import lax
from jax.experimental import pallas as pl
from jax.experimental.pallas import tpu as pltpu

F32 = jnp.float32
BF16 = jnp.bfloat16
MESH = pl.DeviceIdType.MESH

D = 1024
N_META = 16
CHUNK = 64
PAD_FRONT = CHUNK - N_META
N_HEADS = 4
HEAD_K = 128
HEAD_V = 256
RANK = 16
EPS = 1e-6
GATE_NORM = 16.0
N_IN = 9248
SHARD_IN = N_IN // 4
LANES = 128
N_CONV_TILES = 8
W_CONV = 4096
W_GLA = 2048
W_TAIL = 3072
N_MAIN = W_CONV + W_GLA + W_TAIL
OFF_Q, OFF_K, OFF_V, OFF_R = 4096, 4608, 5120, 6144
OFF_LR, OFF_MA = 7168, 7200
MIB = 1024 * 1024

ADAM_LR = 0.001
ADAM_B1 = 0.9
ADAM_B2 = 0.999
ADAM_EPS = 1e-08
ADAM_WD = 0.01
ADAM_STEP = 10


def _params(sem=None, vmem_mib=None):
    return pltpu.CompilerParams(
        dimension_semantics=sem,
        vmem_limit_bytes=None if vmem_mib is None else vmem_mib * MIB)


def _pick_tile(n, target, mult):
    best = None
    for t in range(mult, min(n, target) + 1, mult):
        if n % t == 0:
            best = t
    return n if best is None else best


def _sigmoid(v):
    return 1.0 / (1.0 + jnp.exp(-v))


def _log_sigmoid(v):
    return jnp.minimum(v, 0.0) - jnp.log(1.0 + jnp.exp(-jnp.abs(v)))


def _dot(a, b):
    return jnp.dot(a, b, preferred_element_type=F32)


def _dot_nt(a, b):
    return lax.dot_general(a, b, (((1,), (1,)), ((), ())), preferred_element_type=F32)


def _dot_tn(a, b):
    return lax.dot_general(a, b, (((0,), (0,)), ((), ())), preferred_element_type=F32)


def _gather_all(name, arrs):
    n = len(arrs)
    flips = tuple((m >> 2 & 1, m >> 1 & 1, m & 1) for m in range(1, 8))

    def body(*refs):
        ins, outs = refs[:n], refs[n:2 * n]
        send_sems, recv_sems, local_sems = refs[2 * n:]
        pos = (lax.axis_index("x"), lax.axis_index("y"), lax.axis_index("c"))

        def slot_of(p):
            return 4 * p[0] + 2 * p[1] + p[2]

        peers = [tuple(1 - pos[a] if f[a] else pos[a] for a in range(3)) for f in flips]
        me = slot_of(pos)
        copies = []
        for i in range(n):
            cp = pltpu.make_async_copy(ins[i], outs[i].at[me], local_sems.at[i])
            cp.start()
            copies.append(cp)
        sends = []
        for i in range(n):
            for k, peer in enumerate(peers):
                cp = pltpu.make_async_remote_copy(
                    src_ref=ins[i], dst_ref=outs[i].at[me], send_sem=send_sems.at[i, k], recv_sem=recv_sems.at[i, k],
                    device_id=peer, device_id_type=MESH)
                cp.start()
                sends.append(cp)
        for i in range(n):
            for k, peer in enumerate(peers):
                pltpu.make_async_remote_copy(
                    src_ref=ins[i], dst_ref=outs[i].at[slot_of(peer)], send_sem=send_sems.at[i, k],
                    recv_sem=recv_sems.at[i, k], device_id=peer, device_id_type=MESH).wait_recv()
        for cp in sends:
            cp.wait_send()
        for cp in copies:
            cp.wait()

    hbm = pl.BlockSpec(memory_space=pl.ANY)
    outs = pl.pallas_call(
        body, name=name, out_shape=[jax.ShapeDtypeStruct((8,) + a.shape, a.dtype) for a in arrs],
        in_specs=[hbm] * n, out_specs=[hbm] * n,
        scratch_shapes=[pltpu.SemaphoreType.DMA((n, 7)), pltpu.SemaphoreType.DMA((n, 7)),
                        pltpu.SemaphoreType.DMA((n,))],
        compiler_params=pltpu.CompilerParams(has_side_effects=True),
    )(*arrs)
    return list(outs)


def _gather_via_sibling(name, arrs, slotted):
    n = len(arrs)
    out_shape = [jax.ShapeDtypeStruct(a.shape if slotted[i] else (4,) + a.shape, a.dtype)
                 for i, a in enumerate(arrs)]

    def body(*refs):
        ins, outs = refs[:n], refs[n:2 * n]
        send_sems, recv_sems, local_sems = refs[2 * n:]
        x, y, c = lax.axis_index("x"), lax.axis_index("y"), lax.axis_index("c")
        me = 2 * x + y
        chips = [(1 - x, y), (x, 1 - y), (1 - x, 1 - y)]

        def half(ref, which):
            rows = ref.shape[0]
            cut = rows // 2 // 16 * 16
            return ref.at[pl.ds(0, cut)] if which == 0 else ref.at[pl.ds(cut, rows - cut)]

        def copy(src, dst, i, k, to):
            return pltpu.make_async_remote_copy(
                src_ref=src, dst_ref=dst, send_sem=send_sems.at[i, k], recv_sem=recv_sems.at[i, k],
                device_id=to, device_id_type=MESH)

        def run(mine):
            other = 1 - mine
            local, sends = [], []
            whole = [(not slotted[i]) and arrs[i].shape[0] < 32 for i in range(n)]
            for i in range(n):
                own = outs[i].at[me] if slotted[i] else ins[i]
                if not slotted[i]:
                    cp = pltpu.make_async_copy(ins[i], outs[i].at[me], local_sems.at[i])
                    cp.start()
                    local.append(cp)
                for k, (px, py) in enumerate(chips):
                    if whole[i]:
                        cp = copy(own, outs[i].at[me], i, k, (px, py, mine))
                    elif k < 2:
                        cp = copy(half(own, mine), half(outs[i].at[me], mine), i, k, (px, py, mine))
                    else:
                        continue
                    cp.start()
                    sends.append(cp)
            via = mine
            for k in (via, 1 - via, 2):
                px, py = chips[k]
                slot = 2 * px + py
                source = (px, py, mine) if k < 2 else chips[1 - via] + (mine,)
                for i in range(n):
                    if whole[i]:
                        copy(outs[i].at[slot], outs[i].at[slot], i, k, (px, py, mine)).wait_recv()
                        continue
                    landed = half(outs[i].at[slot], mine)
                    copy(landed, landed, i, k, source).wait_recv()
                    if k == via:
                        cp = copy(landed, landed, i, 2, chips[1 - via] + (mine,))
                        cp.start()
                        sends.append(cp)
                    cp = copy(landed, landed, i, 3 + k, (x, y, other))
                    cp.start()
                    sends.append(cp)
            for k, (px, py) in enumerate(chips):
                slot = 2 * px + py
                for i in range(n):
                    if whole[i]:
                        continue
                    passed = half(outs[i].at[slot], other)
                    copy(passed, passed, i, 3 + k, (x, y, other)).wait_recv()
            for cp in sends:
                cp.wait_send()
            for cp in local:
                cp.wait()

        for mine in (0, 1):
            pl.when(c == mine)(functools.partial(run, mine))

    hbm = pl.BlockSpec(memory_space=pl.ANY)
    outs = pl.pallas_call(
        body, name=name, out_shape=out_shape,
        in_specs=[hbm] * n, out_specs=[hbm] * n,
        scratch_shapes=[pltpu.SemaphoreType.DMA((n, 6)), pltpu.SemaphoreType.DMA((n, 6)),
                        pltpu.SemaphoreType.DMA((n,))],
        input_output_aliases={i: i for i in range(n) if slotted[i]},
        compiler_params=pltpu.CompilerParams(has_side_effects=True),
    )(*arrs)
    return list(outs)


HBM_SPEC = pl.BlockSpec(memory_space=pltpu.HBM)
SEM_SPEC = pl.BlockSpec(memory_space=pltpu.SEMAPHORE)
DATAFLOW = pltpu.SideEffectType.DATAFLOW_SIDE_EFFECTING


def _split_peers(mode):
    x, y, c = lax.axis_index("x"), lax.axis_index("y"), lax.axis_index("c")
    if mode == "swap":
        return 0, [((x, y, 1 - c), 0)]
    return 2 * x + y, [((1 - x, y, c), 2 * (1 - x) + y), ((x, 1 - y, c), 2 * x + 1 - y),
                       ((1 - x, 1 - y, c), 2 * (1 - x) + 1 - y)]


def _split_refs(mode, src, landing, me, peer_slot):
    if mode == "gather":
        return src.at[me], landing.at[me]
    if mode == "scatter":
        return src.at[peer_slot], landing.at[me]
    return src, landing


def _plane_start(name, arrs, mode, after):
    n = len(arrs)
    n_peers = 1 if mode == "swap" else 3
    if mode == "gather":
        srcs, lands = [], list(arrs)
    else:
        srcs, lands = list(arrs), [lax.empty(a.shape, a.dtype) for a in arrs]
    n_src = len(srcs)

    def body(*refs):
        landing = refs[n_src:n_src + n]
        sources = refs[:n_src] if n_src else landing
        send_sems, recv_sems = refs[n_src + n + 1], refs[n_src + n + 2]
        token = refs[-1]
        me, peers = _split_peers(mode)
        for i in range(n):
            for k, (peer, peer_slot) in enumerate(peers):
                src, dst = _split_refs(mode, sources[i], landing[i], me, peer_slot)
                pltpu.make_async_remote_copy(
                    src_ref=src, dst_ref=dst, send_sem=send_sems.at[n_peers * i + k],
                    recv_sem=recv_sems.at[n_peers * i + k], device_id=peer, device_id_type=MESH).start()
        token[...] = jnp.zeros_like(token)

    hbm_in = [pltpu.with_memory_space_constraint(a, pltpu.HBM) for a in srcs + lands]
    out = pl.pallas_call(
        body, name=name,
        out_shape=[pltpu.SemaphoreType.DMA((n_peers * n,)), pltpu.SemaphoreType.DMA((n_peers * n,))]
                  + [pltpu.HBM(a.shape, a.dtype) for a in lands]
                  + [jax.ShapeDtypeStruct((8, LANES), F32)],
        in_specs=[HBM_SPEC] * (n_src + n) + [pl.BlockSpec(memory_space=pl.ANY)],
        out_specs=[SEM_SPEC, SEM_SPEC] + [HBM_SPEC] * n + [pl.BlockSpec(memory_space=pltpu.VMEM)],
        input_output_aliases={n_src + i: 2 + i for i in range(n)},
        compiler_params=pltpu.CompilerParams(has_side_effects=DATAFLOW),
    )(*hbm_in, after)
    return out[:-1], out[-1]


def _plane_wait(name, state, mode, after):
    send_sems, recv_sems = state[0], state[1]
    bufs = list(state[2:])
    n = len(bufs)
    n_peers = 1 if mode == "swap" else 3

    def body(*refs):
        landing = refs[:n]
        send_sems, recv_sems = refs[n], refs[n + 1]
        _, peers = _split_peers(mode)
        for i in range(n):
            for k, (peer, peer_slot) in enumerate(peers):
                arrived = landing[i] if mode == "swap" else landing[i].at[peer_slot]
                cp = pltpu.make_async_remote_copy(
                    src_ref=arrived, dst_ref=arrived, send_sem=send_sems.at[n_peers * i + k],
                    recv_sem=recv_sems.at[n_peers * i + k], device_id=peer, device_id_type=MESH)
                cp.wait_send()
                cp.wait_recv()

    out = pl.pallas_call(
        body, name=name,
        out_shape=[pltpu.HBM(a.shape, a.dtype) for a in bufs],
        in_specs=[HBM_SPEC] * n + [SEM_SPEC, SEM_SPEC, pl.BlockSpec(memory_space=pl.ANY)],
        out_specs=[HBM_SPEC] * n,
        input_output_aliases={i: i for i in range(n)},
        compiler_params=pltpu.CompilerParams(has_side_effects=DATAFLOW),
    )(*bufs, send_sems, recv_sems, after)
    return list(out)


def _tile_2d(rows, cols, row_mult, max_elems=512 * 1024):
    if rows % row_mult == 0:
        rt = _pick_tile(rows, max(row_mult, max_elems // cols), row_mult)
        return (rt, cols), rows // rt, lambda i: (i, 0)
    ct = _pick_tile(cols, max(LANES, max_elems // rows), LANES)
    return (rows, ct), cols // ct, lambda i: (0, i)


def _cast_into_slot(a, slot, name):
    block, steps, index = _tile_2d(a.shape[0], a.shape[1], 16)

    def body(slot_ref, a_ref, o_ref):
        o_ref[...] = a_ref[...].astype(BF16)

    return pl.pallas_call(
        body, name=name,
        grid_spec=pltpu.PrefetchScalarGridSpec(
            num_scalar_prefetch=1, grid=(steps,),
            in_specs=[pl.BlockSpec(block, lambda i, s: index(i))],
            out_specs=pl.BlockSpec((None,) + block, lambda i, s: (s[0],) + index(i))),
        out_shape=jax.ShapeDtypeStruct((4,) + a.shape, BF16),
        compiler_params=_params(("arbitrary",)),
    )(slot, a)


def _sum_slots(buf, name, own=None, slot=None):
    n_slots, rows, cols = buf.shape
    (br, bc), steps, index = _tile_2d(rows, cols, 16, 320 * 1024)

    def body(*refs):
        if own is None:
            b_ref, o_ref = refs
        else:
            slot_ref, b_ref, own_ref, o_ref = refs
        acc = None
        for s in range(n_slots):
            term = b_ref[s] if own is None else jnp.where(slot_ref[0] == s, own_ref[...], b_ref[s])
            acc = term.astype(F32) if acc is None else acc + term.astype(F32)
        o_ref[...] = acc

    out_shape = jax.ShapeDtypeStruct((rows, cols), F32)
    if own is None:
        return pl.pallas_call(
            body, name=name, grid=(steps,),
            in_specs=[pl.BlockSpec((n_slots, br, bc), lambda i: (0,) + index(i))],
            out_specs=pl.BlockSpec((br, bc), index), out_shape=out_shape,
            compiler_params=_params(("parallel",), 48),
        )(buf)
    return pl.pallas_call(
        body, name=name,
        grid_spec=pltpu.PrefetchScalarGridSpec(
            num_scalar_prefetch=1, grid=(steps,),
            in_specs=[pl.BlockSpec((n_slots, br, bc), lambda i, s: (0,) + index(i)),
                      pl.BlockSpec((None, br, bc), lambda i, s: (s[0],) + index(i))],
            out_specs=pl.BlockSpec((br, bc), lambda i, s: index(i))),
        out_shape=out_shape,
        compiler_params=_params(("arbitrary",), 48),
    )(slot, buf, own)


def _sum_small(bufs, name):
    n = len(bufs)

    def body(*refs):
        for b_ref, o_ref in zip(refs[:n], refs[n:]):
            acc = b_ref[0]
            for s in range(1, b_ref.shape[0]):
                acc = acc + b_ref[s]
            o_ref[...] = acc

    vmem = pl.BlockSpec(memory_space=pltpu.VMEM)
    return pl.pallas_call(
        body, name=name, in_specs=[vmem] * n, out_specs=[vmem] * n,
        out_shape=[jax.ShapeDtypeStruct(b.shape[1:], b.dtype) for b in bufs],
    )(*bufs)


def _adam_update(w, g, m, v):
    c1 = 1.0 - ADAM_B1 ** ADAM_STEP
    c2 = 1.0 - ADAM_B2 ** ADAM_STEP
    m_new = ADAM_B1 * m + (1.0 - ADAM_B1) * g
    v_new = ADAM_B2 * v + (1.0 - ADAM_B2) * (g * g)
    m_hat = m_new / c1
    v_hat = v_new / c2
    return -ADAM_LR * (m_hat / (jnp.sqrt(v_hat) + ADAM_EPS) + ADAM_WD * w), m_new, v_new


def _adamw_small(params, slot, name):
    n = len(params)

    def spec_of(shape):
        lead = (None,) * (len(shape) - 2)
        return pl.BlockSpec(lead + tuple(shape[-2:]), lambda i, s, k=len(shape): (0,) * k)

    in_specs, operands, out_specs, out_shape = [], [], [], []
    for w, g, m, v in params:
        shard = g.shape[-1] != w.shape[-1]
        g_spec = pl.BlockSpec(tuple(w.shape[-2:]), (lambda i, s: (0, s[0])) if shard else (lambda i, s: (0, 0)))
        in_specs += [spec_of(w.shape), g_spec, spec_of(m.shape), spec_of(v.shape)]
        operands += [w, g, m, v]
        out_specs += [spec_of(w.shape)] * 4
        out_shape += [jax.ShapeDtypeStruct(w.shape, F32)] * 4

    def body(slot_ref, *refs):
        ins, outs = refs[:4 * n], refs[4 * n:]
        for p in range(n):
            w_ref, g_ref, m_ref, v_ref = ins[4 * p:4 * p + 4]
            g = g_ref[...]
            delta, m_new, v_new = _adam_update(w_ref[...], g, m_ref[...], v_ref[...])
            for o_ref, val in zip(outs[4 * p:4 * p + 4], (g, delta, m_new, v_new)):
                o_ref[...] = val

    out = pl.pallas_call(
        body, name=name,
        grid_spec=pltpu.PrefetchScalarGridSpec(num_scalar_prefetch=1, grid=(1,), in_specs=in_specs, out_specs=out_specs),
        out_shape=out_shape,
    )(slot, *operands)
    return [tuple(out[4 * p:4 * p + 4]) for p in range(n)]


def _adamw(w, grads, m, v, name, grad_row=0):
    rows, cols = w.shape
    (rt, _), _, _ = _tile_2d(rows, cols, 8, 160 * 1024)
    assert grad_row % rt == 0
    n_g = len(grads)

    def body(*refs):
        w_ref = refs[0]
        g_refs = refs[1:1 + n_g]
        m_ref, v_ref, g_out, d_out, m_out, v_out = refs[1 + n_g:]
        g = g_refs[0][...]
        for r in g_refs[1:]:
            g = g + r[...]
        g_out[...] = g
        d_out[...], m_out[...], v_out[...] = _adam_update(w_ref[...], g, m_ref[...], v_ref[...])

    spec = pl.BlockSpec((rt, cols), lambda i: (i, 0))
    grad_spec = pl.BlockSpec((rt, cols), lambda i: (i + grad_row // rt, 0))
    shape = jax.ShapeDtypeStruct((rows, cols), F32)
    return pl.pallas_call(
        body, name=name, grid=(rows // rt,),
        in_specs=[spec] + [grad_spec] * n_g + [spec] * 2, out_specs=[spec] * 4, out_shape=[shape] * 4,
        compiler_params=_params(("parallel",), 48),
    )(w, *grads, m, v)


def _weight_pieces():
    pieces = []
    for j in range(N_CONV_TILES):
        for g in range(4):
            pieces.append((512 * j + 128 * g, D * g + 128 * j, 128))
    for hd in range(N_HEADS):
        base = W_CONV + 512 * hd
        pieces.append((base, OFF_Q + HEAD_K * hd, HEAD_K))
        pieces.append((base + HEAD_K, OFF_K + HEAD_K * hd, HEAD_K))
        pieces.append((base + 2 * HEAD_K, OFF_V + HEAD_V * hd, HEAD_V))
    pieces.append((W_CONV + W_GLA, OFF_R, D))
    pieces.append((W_CONV + W_GLA + D, OFF_MA, 2 * D))
    return pieces


N_WEIGHT_COPIES = len(_weight_pieces()) + 1


def _load_weights(w_hbm, w_s, wlr_s, sems):
    copies = [pltpu.make_async_copy(w_hbm.at[pl.ds(src, n)], w_s.at[pl.ds(dst, n)], sems.at[i])
              for i, (dst, src, n) in enumerate(_weight_pieces())]
    copies.append(pltpu.make_async_copy(w_hbm.at[pl.ds(OFF_LR, LANES)], wlr_s, sems.at[N_WEIGHT_COPIES - 1]))
    for cp in copies:
        cp.start()
    for cp in copies:
        cp.wait()


def _in_proj(h, g_pre, w_full_t):
    t_rows = h.shape[0]
    tm = _pick_tile(t_rows, 384, LANES)
    n_main = N_MAIN

    def body(h_ref, g_ref, w_hbm, proj_ref, ut_ref, lr_ref, w_s, wlr_s, w_sems):
        @pl.when(pl.program_id(0) == 0)
        def _():
            _load_weights(w_hbm, w_s, wlr_s, w_sems)

        hh = h_ref[...]
        rstd = lax.rsqrt(jnp.mean(hh * hh, axis=-1, keepdims=True) + EPS)
        uf = hh * rstd * g_ref[...]
        u = uf.astype(BF16)
        ut_ref[...] = jnp.transpose(uf).astype(BF16)
        lr_ref[...] = _dot_nt(u, wlr_s[...])
        for j in range(n_main // D):
            cols = slice(j * D, (j + 1) * D)
            proj_ref[:, cols] = _dot_nt(u, w_s[cols, :]).astype(BF16)

    return pl.pallas_call(
        body, name="in_proj", grid=(t_rows // tm,),
        in_specs=[pl.BlockSpec((tm, D), lambda i: (i, 0)),
                  pl.BlockSpec((1, D), lambda i: (0, 0)),
                  pl.BlockSpec(memory_space=pl.ANY)],
        out_specs=[pl.BlockSpec((tm, n_main), lambda i: (i, 0)),
                   pl.BlockSpec((D, tm), lambda i: (0, i)),
                   pl.BlockSpec((tm, LANES), lambda i: (i, 0))],
        out_shape=[jax.ShapeDtypeStruct((t_rows, n_main), BF16),
                   jax.ShapeDtypeStruct((D, t_rows), BF16),
                   jax.ShapeDtypeStruct((t_rows, LANES), F32)],
        scratch_shapes=[pltpu.VMEM((n_main, D), BF16), pltpu.VMEM((LANES, D), BF16),
                        pltpu.SemaphoreType.DMA((N_WEIGHT_COPIES,))],
        compiler_params=_params(("arbitrary",), 56),
    )(h, g_pre, w_full_t)


CONV_TILES_PER_STEP = 4


def _conv_parts(p_ref, w_ref, t):
    cb = p_ref[:, 512 * t:512 * t + 128].astype(F32)
    cc = p_ref[:, 512 * t + 128:512 * t + 256].astype(F32)
    cx = p_ref[:, 512 * t + 256:512 * t + 384].astype(F32)
    cz = p_ref[:, 512 * t + 384:512 * t + 512].astype(F32)
    rows = cb.shape[0]
    w = w_ref[:, 128 * t:128 * (t + 1)]
    p = cc * cx
    conv = pltpu.roll(p, 1, 0) * w[0:1] + p * w[1:2] + pltpu.roll(p, rows - 1, 0) * w[2:3]
    sz = _sigmoid(cz)
    return cb, cc, cx, cz, p, conv, sz, w


def _conv_fwd(proj, conv_w, n_seq, lf):
    per = CONV_TILES_PER_STEP

    def body(p_ref, w_ref, y_ref):
        for t in range(per):
            cb, _, _, cz, _, conv, sz, _ = _conv_parts(p_ref, w_ref, t)
            y_ref[:, 128 * t:128 * (t + 1)] = (cb * conv * (cz * sz)).astype(BF16)

    return pl.pallas_call(
        body, name="conv_fwd", grid=(n_seq, N_CONV_TILES // per),
        in_specs=[pl.BlockSpec((lf, 512 * per), lambda b, j: (b, j)),
                  pl.BlockSpec((3, 128 * per), lambda b, j: (0, j))],
        out_specs=pl.BlockSpec((lf, 128 * per), lambda b, j: (b, j)),
        out_shape=jax.ShapeDtypeStruct((n_seq * lf, D), BF16),
        compiler_params=_params(("parallel", "parallel"), 48),
    )(proj, conv_w)


def _conv_bwd(proj, conv_w, dyc, n_seq, lf):
    per = CONV_TILES_PER_STEP

    def body(p_ref, w_ref, dy_ref, dp_ref, dw_ref):
        for t in range(per):
            cb, cc, cx, cz, p, conv, sz, w = _conv_parts(p_ref, w_ref, t)
            rows = cb.shape[0]
            dy = dy_ref[:, 128 * t:128 * (t + 1)].astype(F32)
            silu = cz * sz
            dcb = dy * conv * silu
            dconv = dy * cb * silu
            dcz = dy * cb * conv * (sz * (1.0 + cz * (1.0 - sz)))
            d_next = pltpu.roll(dconv, rows - 1, 0)
            d_prev = pltpu.roll(dconv, 1, 0)
            dp = d_next * w[0:1] + dconv * w[1:2] + d_prev * w[2:3]
            base = 512 * t
            dp_ref[:, base:base + 128] = dcb.astype(BF16)
            dp_ref[:, base + 128:base + 256] = (dp * cx).astype(BF16)
            dp_ref[:, base + 256:base + 384] = (dp * cc).astype(BF16)
            dp_ref[:, base + 384:base + 512] = dcz.astype(BF16)
            lanes = slice(128 * t, 128 * (t + 1))
            dw_ref[0:1, lanes] = jnp.sum(dconv * pltpu.roll(p, 1, 0), axis=0, keepdims=True)
            dw_ref[1:2, lanes] = jnp.sum(dconv * p, axis=0, keepdims=True)
            dw_ref[2:3, lanes] = jnp.sum(dconv * pltpu.roll(p, rows - 1, 0), axis=0, keepdims=True)

    return pl.pallas_call(
        body, name="conv_bwd", grid=(n_seq, N_CONV_TILES // per),
        in_specs=[pl.BlockSpec((lf, 512 * per), lambda b, j: (b, j)),
                  pl.BlockSpec((3, 128 * per), lambda b, j: (0, j)),
                  pl.BlockSpec((lf, 128 * per), lambda b, j: (b, j))],
        out_specs=[pl.BlockSpec((lf, 512 * per), lambda b, j: (b, j)),
                   pl.BlockSpec((None, 3, 128 * per), lambda b, j: (b, 0, j))],
        out_shape=[jax.ShapeDtypeStruct((n_seq * lf, W_CONV), BF16),
                   jax.ShapeDtypeStruct((n_seq, 3, D), F32)],
        compiler_params=_params(("parallel", "parallel"), 48),
    )(proj, conv_w, dyc)


GROUP = 3
GROUP_ROWS = GROUP * CHUNK


def _row_group(shape):
    row = lax.broadcasted_iota(jnp.int32, shape, 0)
    grp = jnp.zeros(shape, jnp.int32)
    for r in range(1, GROUP):
        grp = grp + (row >= r * CHUNK).astype(jnp.int32)
    return grp


def _lane_group(shape, width):
    lane = lax.broadcasted_iota(jnp.int32, shape, 1)
    grp = jnp.zeros(shape, jnp.int32)
    for r in range(1, GROUP):
        grp = grp + (lane >= r * width).astype(jnp.int32)
    return grp


def _score_mask(direction):
    shape = (GROUP_ROWS, GROUP_ROWS)
    row = lax.broadcasted_iota(jnp.int32, shape, 0)
    col = lax.broadcasted_iota(jnp.int32, shape, 1)
    same = _row_group(shape) == _lane_group(shape, CHUNK)
    return same & ((col <= row) if direction == 0 else (col > row))


def _diag_blocks(v):
    w = v.shape[1]
    wide = jnp.concatenate([v] * GROUP, axis=1)
    return jnp.where(_row_group(wide.shape) == _lane_group(wide.shape, w), wide, jnp.zeros_like(wide))


def _per_chunk_dot(lhs, state, transposed):
    outs = []
    for r in range(GROUP):
        rows = lhs[r * CHUNK:(r + 1) * CHUNK, :]
        blk = state[:, r * HEAD_K:(r + 1) * HEAD_K]
        outs.append(_dot_nt(rows, blk) if transposed else _dot(rows, blk))
    return jnp.concatenate(outs, axis=0)


def _chunk_cumsum(v, suffix):
    pos = lax.broadcasted_iota(jnp.int32, v.shape, 0) & (CHUNK - 1)
    shift = 1
    while shift < CHUNK:
        if suffix:
            moved = pltpu.roll(v, GROUP_ROWS - shift, 0)
            v = v + jnp.where(pos < CHUNK - shift, moved, 0.0)
        else:
            moved = pltpu.roll(v, shift, 0)
            v = v + jnp.where(pos >= shift, moved, 0.0)
        shift *= 2
    return v


def _per_chunk_rows(rows_of_chunk):
    w = rows_of_chunk[0].shape[1]
    return jnp.concatenate([jnp.broadcast_to(v, (CHUNK, w)) for v in rows_of_chunk], axis=0)


def _chunk_end_rows(direction, b):
    at = CHUNK - 1 if direction == 0 else 0
    return [b[r * CHUNK + at:r * CHUNK + at + 1, :] for r in range(GROUP)]


def _gla_gates(lr_bf, wg_ref, bg_ref, lf):
    z = _dot(lr_bf, wg_ref[...]) + bg_ref[...]
    valid = lax.broadcasted_iota(jnp.int32, (lf, HEAD_K), 0) >= PAD_FRONT
    return z, valid


def _group_unroll(n_groups):
    return n_groups if n_groups <= 11 else 1


def _group_rows(g):
    return pl.ds(pl.multiple_of(g * GROUP_ROWS, GROUP_ROWS), GROUP_ROWS)


def _chunk_decay(direction, g, r, b_s):
    base = g * GROUP_ROWS + r * CHUNK
    if direction == 0:
        grp = b_s[pl.ds(pl.multiple_of(base + CHUNK - 8, 8), 8), :]
        return jnp.exp(grp[7:8, :])
    grp = b_s[pl.ds(pl.multiple_of(base, 8), 8), :]
    return jnp.exp(grp[0:1, :])


def _state_scan(direction, n_groups, b_s, st_s, reverse):
    ascending = (direction == 0) != reverse

    def step(i, carry):
        g = i if ascending else n_groups - 1 - i
        for rr in range(GROUP):
            r = rr if ascending else GROUP - 1 - rr
            lanes = slice(r * HEAD_K, (r + 1) * HEAD_K)
            decay = _chunk_decay(direction, g, r, b_s)
            local = st_s[g, :, lanes]
            st_s[g, :, lanes] = carry
            carry = (local + carry * decay) if reverse else (carry * decay + local)
        return carry

    lax.fori_loop(0, n_groups, step, jnp.zeros((HEAD_V, HEAD_K), F32), unroll=_group_unroll(n_groups))


def _gla_states(direction, n_groups, qkv_ref, g_s, b_s, st_s):
    def local(g, carry):
        rows = _group_rows(g)
        b = _chunk_cumsum(g_s[rows, :], direction == 1)
        b_s[rows, :] = b
        b_end = _per_chunk_rows(_chunk_end_rows(direction, b))
        k = qkv_ref[rows, 128:256].astype(F32)
        v = qkv_ref[rows, 256:512]
        k_dec = (k * jnp.exp(b_end - b)).astype(BF16)
        st_s[g] = _dot_tn(v, _diag_blocks(k_dec))
        return carry

    lax.fori_loop(0, n_groups, local, 0, unroll=_group_unroll(n_groups))
    _state_scan(direction, n_groups, b_s, st_s, False)


def _gla_fwd(proj, lr, wgf, wgb, bgf, bgb, n_seq, lf):
    assert lf % GROUP_ROWS == 0
    n_groups = lf // GROUP_ROWS
    scale = HEAD_K ** -0.5

    def body(qkv_ref, lr_ref, wgf_ref, wgb_ref, bgf_ref, bgb_ref, o_ref, g_s, b_s2, st_s2):
        lr_bf = lr_ref[...].astype(BF16)
        for direction in (0, 1):
            wg_ref, bg_ref = ((wgf_ref, bgf_ref), (wgb_ref, bgb_ref))[direction]
            z, valid = _gla_gates(lr_bf, wg_ref, bg_ref, lf)
            g_s[...] = jnp.where(valid, _log_sigmoid(z) / GATE_NORM, 0.0)
            _gla_states(direction, n_groups, qkv_ref, g_s, b_s2.at[direction], st_s2.at[direction])
        masks = [_score_mask(0), _score_mask(1)]

        def out(g, carry):
            rows = _group_rows(g)
            q = qkv_ref[rows, 0:128].astype(F32) * scale
            k = qkv_ref[rows, 128:256].astype(F32)
            v = qkv_ref[rows, 256:512]
            o = None
            for direction in (0, 1):
                b = b_s2[direction, rows, :]
                q_in = (q * jnp.exp(b)).astype(BF16)
                k_in = (k * jnp.exp(-b)).astype(BF16)
                s = jnp.where(masks[direction], _dot_nt(q_in, k_in), 0.0).astype(BF16)
                part = _dot(s, v) + _per_chunk_dot(q_in, st_s2[direction, g].astype(BF16), True)
                o = part if o is None else o + part
            o_ref[rows, :] = o
            return carry

        lax.fori_loop(0, n_groups, out, 0, unroll=_group_unroll(n_groups))

    return pl.pallas_call(
        body, name="gla_fwd", grid=(n_seq, N_HEADS),
        in_specs=[pl.BlockSpec((lf, 512), lambda b, h: (b, N_CONV_TILES + h)),
                  pl.BlockSpec((lf, LANES), lambda b, h: (b, 0)),
                  pl.BlockSpec((None, LANES, HEAD_K), lambda b, h: (h, 0, 0)),
                  pl.BlockSpec((None, LANES, HEAD_K), lambda b, h: (h, 0, 0)),
                  pl.BlockSpec((None, 1, HEAD_K), lambda b, h: (h, 0, 0)),
                  pl.BlockSpec((None, 1, HEAD_K), lambda b, h: (h, 0, 0))],
        out_specs=pl.BlockSpec((lf, HEAD_V), lambda b, h: (b, h)),
        out_shape=jax.ShapeDtypeStruct((n_seq * lf, D), F32),
        scratch_shapes=[pltpu.VMEM((lf, HEAD_K), F32), pltpu.VMEM((2, lf, HEAD_K), F32),
                        pltpu.VMEM((2, n_groups, HEAD_V, GROUP * HEAD_K), F32)],
        compiler_params=_params(("parallel", "parallel"), 48),
    )(proj, lr, wgf, wgb, bgf, bgb)


def _gla_bwd(proj, lr, d_o, wgf, wgb, bgf, bgb, n_seq, lf, token):
    assert lf % GROUP_ROWS == 0
    n_groups = lf // GROUP_ROWS
    scale = HEAD_K ** -0.5

    def body(qkv_ref, lr_ref, do_ref, wgf_ref, wgb_ref, bgf_ref, bgb_ref, token_ref,
             dqkv_ref, dlr_ref, dwgf_ref, dwgb_ref, dbg_ref,
             g_s, b_s2, fac_s2, dg_s2, st_s2, dst_s2):
        lr_bf = lr_ref[...].astype(BF16)
        gates = ((wgf_ref, bgf_ref), (wgb_ref, bgb_ref))
        for direction in (0, 1):
            wg_ref, bg_ref = gates[direction]
            b_s, st_s, dst_s = b_s2.at[direction], st_s2.at[direction], dst_s2.at[direction]
            z, valid = _gla_gates(lr_bf, wg_ref, bg_ref, lf)
            g_s[...] = jnp.where(valid, _log_sigmoid(z) / GATE_NORM, 0.0)
            fac_s2[direction] = jnp.where(valid, _sigmoid(-z) / GATE_NORM, 0.0)
            _gla_states(direction, n_groups, qkv_ref, g_s, b_s, st_s)

            def state_grad_local(g, carry):
                rows = _group_rows(g)
                q = qkv_ref[rows, 0:128].astype(F32) * scale
                q_in = (q * jnp.exp(b_s[rows, :])).astype(BF16)
                dst_s[g] = _dot_tn(do_ref[rows, :], _diag_blocks(q_in))
                return carry

            lax.fori_loop(0, n_groups, state_grad_local, 0, unroll=_group_unroll(n_groups))
            _state_scan(direction, n_groups, b_s, dst_s, True)

        masks = [_score_mask(0), _score_mask(1)]

        def group_grads(g, carry):
            rows = _group_rows(g)
            q = qkv_ref[rows, 0:128].astype(F32) * scale
            k = qkv_ref[rows, 128:256].astype(F32)
            v = qkv_ref[rows, 256:512]
            d_out = do_ref[rows, :]
            dq_sum = dk_sum = dv_sum = None
            for direction in (0, 1):
                end_row = CHUNK - 1 if direction == 0 else 0
                b = b_s2[direction, rows, :]
                ends = _chunk_end_rows(direction, b)
                b_end = _per_chunk_rows(ends)
                e_pos = jnp.exp(b)
                e_neg = jnp.exp(-b)
                e_end = jnp.exp(b_end - b)
                q_in = q * e_pos
                k_in = k * e_neg
                k_dec = k * e_end
                q_in_bf = q_in.astype(BF16)
                k_in_bf = k_in.astype(BF16)
                state = st_s2[direction, g]
                d_state = dst_s2[direction, g]
                state_bf = state.astype(BF16)
                d_state_bf = d_state.astype(BF16)
                s = jnp.where(masks[direction], _dot_nt(q_in_bf, k_in_bf), 0.0).astype(BF16)
                ds = jnp.where(masks[direction], _dot_nt(d_out, v), 0.0).astype(BF16)
                dv = _dot_tn(s, d_out) + _per_chunk_dot(k_dec.astype(BF16), d_state_bf, True)
                dq_in = _dot(ds, k_in_bf) + _per_chunk_dot(d_out, state_bf, False)
                dk_in = _dot_tn(ds, q_in_bf)
                dk_dec = _per_chunk_dot(v, d_state_bf, False)
                dq = dq_in * e_pos * scale
                dk = dk_in * e_neg + dk_dec * e_end
                dq_sum = dq if dq_sum is None else dq_sum + dq
                dk_sum = dk if dk_sum is None else dk_sum + dk
                dv_sum = dv if dv_sum is None else dv_sum + dv
                dkk = dk_dec * k_dec
                db = dq_in * q_in - dk_in * k_in - dkk
                d_decay = jnp.sum(d_state * state, axis=0, keepdims=True)
                db_end = [jnp.sum(dkk[r * CHUNK:(r + 1) * CHUNK, :], axis=0, keepdims=True)
                          + d_decay[:, r * HEAD_K:(r + 1) * HEAD_K] * jnp.exp(ends[r]) for r in range(GROUP)]
                row = lax.broadcasted_iota(jnp.int32, (GROUP_ROWS, HEAD_K), 0)
                at_end = row == end_row
                for r in range(1, GROUP):
                    at_end = at_end | (row == r * CHUNK + end_row)
                db = db + jnp.where(at_end, _per_chunk_rows(db_end), 0.0)
                dg_s2[direction, rows, :] = _chunk_cumsum(db, direction == 0)
            dqkv_ref[rows, 0:128] = dq_sum.astype(BF16)
            dqkv_ref[rows, 128:256] = dk_sum.astype(BF16)
            dqkv_ref[rows, 256:512] = dv_sum.astype(BF16)
            return carry

        lax.fori_loop(0, n_groups, group_grads, 0, unroll=_group_unroll(n_groups))

        dlr = jnp.zeros((lf, LANES), F32)
        for direction in (0, 1):
            dz = dg_s2[direction] * fac_s2[direction]
            dz_bf = dz.astype(BF16)
            dbg_ref[direction:direction + 1, :] = jnp.sum(dz, axis=0, keepdims=True)
            (dwgf_ref, dwgb_ref)[direction][...] = _dot_tn(lr_bf, dz_bf)
            dlr = dlr + _dot_nt(dz_bf, gates[direction][0][...])

        @pl.when(pl.program_id(1) == 0)
        def _():
            dlr_ref[...] = dlr

        @pl.when(pl.program_id(1) != 0)
        def _():
            dlr_ref[...] = dlr_ref[...] + dlr

    gate_w = pl.BlockSpec((None, LANES, HEAD_K), lambda b, h: (h, 0, 0))
    gate_b = pl.BlockSpec((None, 1, HEAD_K), lambda b, h: (h, 0, 0))
    return pl.pallas_call(
        body, name="gla_bwd", grid=(n_seq, N_HEADS),
        in_specs=[pl.BlockSpec((lf, 512), lambda b, h: (b, N_CONV_TILES + h)),
                  pl.BlockSpec((lf, LANES), lambda b, h: (b, 0)),
                  pl.BlockSpec((lf, HEAD_V), lambda b, h: (b, h)),
                  gate_w, gate_w, gate_b, gate_b,
                  pl.BlockSpec((8, LANES), lambda b, h: (0, 0))],
        out_specs=[pl.BlockSpec((lf, 512), lambda b, h: (b, h)),
                   pl.BlockSpec((lf, LANES), lambda b, h: (b, 0)),
                   pl.BlockSpec((None, None, LANES, HEAD_K), lambda b, h: (b, h, 0, 0)),
                   pl.BlockSpec((None, None, LANES, HEAD_K), lambda b, h: (b, h, 0, 0)),
                   pl.BlockSpec((None, None, 2, HEAD_K), lambda b, h: (b, h, 0, 0))],
        out_shape=[jax.ShapeDtypeStruct((n_seq * lf, W_GLA), BF16),
                   jax.ShapeDtypeStruct((n_seq * lf, LANES), F32),
                   jax.ShapeDtypeStruct((n_seq, N_HEADS, LANES, HEAD_K), F32),
                   jax.ShapeDtypeStruct((n_seq, N_HEADS, LANES, HEAD_K), F32),
                   jax.ShapeDtypeStruct((n_seq, N_HEADS, 2, HEAD_K), F32)],
        scratch_shapes=[pltpu.VMEM((lf, HEAD_K), F32), pltpu.VMEM((2, lf, HEAD_K), F32),
                        pltpu.VMEM((2, lf, HEAD_K), F32), pltpu.VMEM((2, lf, HEAD_K), F32),
                        pltpu.VMEM((2, n_groups, HEAD_V, GROUP * HEAD_K), F32),
                        pltpu.VMEM((2, n_groups, HEAD_V, GROUP * HEAD_K), F32)],
        compiler_params=_params(("parallel", "arbitrary"), 56),
    )(proj, lr, d_o, wgf, wgb, bgf, bgb, token)


def _tail(h, tgt, yc, o, proj, w3, gamma, g_post, lf):
    t_rows = h.shape[0]
    tm = _pick_tile(t_rows, 256, CHUNK)
    n_chunks = lf // CHUNK
    per_tile = tm // CHUNK

    def body(h_ref, tgt_ref, yc_ref, o_ref, r_ref, ma_ref, mb_ref, w_hbm, gamma_ref, gpost_ref,
             dres_ref, yg_ref, merged_ref, dout_ref, dpc_ref, dpg_ref, dyc_ref, do_ref, dtail_ref,
             loss_ref, dgpost_ref, dgamma_ref, w_s, w_sem):
        i = pl.program_id(0)

        @pl.when(i == 0)
        def _():
            cp = pltpu.make_async_copy(w_hbm, w_s, w_sem)
            cp.start()
            cp.wait()
            loss_ref[...] = jnp.zeros_like(loss_ref)
            dgpost_ref[...] = jnp.zeros_like(dgpost_ref)
            dgamma_ref[...] = jnp.zeros_like(dgamma_ref)

        gamma = gamma_ref[...]
        o = o_ref[...]
        r = r_ref[...].astype(F32)
        sr = _sigmoid(r)
        silu_r = r * sr
        n_parts, rstd_parts = [], []
        for hd in range(N_HEADS):
            oh = o[:, hd * HEAD_V:(hd + 1) * HEAD_V]
            rstd = lax.rsqrt(jnp.mean(oh * oh, axis=-1, keepdims=True) + EPS)
            n_parts.append(oh * rstd)
            rstd_parts.append(rstd)
        n = jnp.concatenate(n_parts, axis=-1)
        gamma_t = jnp.concatenate([gamma] * N_HEADS, axis=-1)
        yg = n * gamma_t * silu_r
        yg_bf = yg.astype(BF16)
        yg_ref[...] = yg_bf
        yc = yc_ref[...]
        pc = _dot(yc, w_s[0])
        pg = _dot(yg_bf, w_s[1])
        sa = _sigmoid(ma_ref[...].astype(F32))
        sb = _sigmoid(mb_ref[...].astype(F32))
        merged = (sa * pc + sb * pg).astype(BF16)
        merged_ref[...] = merged
        out = _dot(merged, w_s[2])
        rstd2 = lax.rsqrt(jnp.mean(out * out, axis=-1, keepdims=True) + EPS)
        nn = out * rstd2
        gpost = gpost_ref[...]
        y = h_ref[...] + nn * gpost

        rowi = lax.broadcasted_iota(jnp.int32, (tm, 1), 0)
        keep = jnp.zeros((tm, 1), F32)
        for kk in range(per_tile):
            is_tok = ((i * per_tile + kk) % n_chunks) != 0
            f = jnp.where(is_tok, 1.0, 0.0)
            keep = jnp.where((rowi >= kk * CHUNK) & (rowi < (kk + 1) * CHUNK), f, keep)
        diff = (y - tgt_ref[...]) * keep
        loss_ref[...] += jnp.sum(diff * diff) * (0.5 / D)
        dy = diff * (1.0 / D)
        dres_ref[...] = dy
        dgpost_ref[...] += jnp.sum(dy * nn, axis=0, keepdims=True)
        dn = dy * gpost
        dout_f = rstd2 * (dn - nn * jnp.mean(dn * nn, axis=-1, keepdims=True))
        dout = dout_f.astype(BF16)
        dout_ref[...] = jnp.transpose(dout_f).astype(BF16)
        dmerged = _dot_nt(dout, w_s[2])
        dpc_f = dmerged * sa
        dpg_f = dmerged * sb
        dpc = dpc_f.astype(BF16)
        dpg = dpg_f.astype(BF16)
        dpc_ref[...] = jnp.transpose(dpc_f).astype(BF16)
        dpg_ref[...] = jnp.transpose(dpg_f).astype(BF16)
        dtail_ref[:, D:2 * D] = (dmerged * pc * (sa * (1.0 - sa))).astype(BF16)
        dtail_ref[:, 2 * D:3 * D] = (dmerged * pg * (sb * (1.0 - sb))).astype(BF16)
        dyc_ref[...] = _dot_nt(dpc, w_s[0]).astype(BF16)
        dyg = _dot_nt(dpg, w_s[1])
        dtail_ref[:, 0:D] = (dyg * n * gamma_t * (sr * (1.0 + r * (1.0 - sr)))).astype(BF16)
        dgam_full = jnp.sum(dyg * n * silu_r, axis=0, keepdims=True)
        dgam = dgam_full[:, 0:HEAD_V]
        for hd in range(1, N_HEADS):
            dgam = dgam + dgam_full[:, hd * HEAD_V:(hd + 1) * HEAD_V]
        dgamma_ref[...] += dgam
        dng = dyg * gamma_t * silu_r
        do_parts = []
        for hd in range(N_HEADS):
            sl = slice(hd * HEAD_V, (hd + 1) * HEAD_V)
            dnh = dng[:, sl]
            nh = n_parts[hd]
            do_parts.append(rstd_parts[hd] * (dnh - nh * jnp.mean(dnh * nh, axis=-1, keepdims=True)))
        do_ref[...] = jnp.concatenate(do_parts, axis=-1).astype(BF16)

    row = lambda c: pl.BlockSpec((tm, D), lambda i: (i, c))
    col = pl.BlockSpec((D, tm), lambda i: (0, i))
    const = lambda shape: pl.BlockSpec(shape, lambda i: (0, 0))
    act = jax.ShapeDtypeStruct((t_rows, D), BF16)
    act_t = jax.ShapeDtypeStruct((D, t_rows), BF16)
    return pl.pallas_call(
        body, name="tail", grid=(t_rows // tm,),
        in_specs=[row(0), row(0), row(0), row(0), row(6), row(7), row(8),
                  pl.BlockSpec(memory_space=pl.ANY), const((1, HEAD_V)), const((1, D))],
        out_specs=[row(0)] * 3 + [col] * 3 + [row(0)] * 2
                  + [pl.BlockSpec((tm, W_TAIL), lambda i: (i, 0)),
                     const((8, LANES)), const((1, D)), const((1, HEAD_V))],
        out_shape=[jax.ShapeDtypeStruct((t_rows, D), F32)] + [act] * 2 + [act_t] * 3 + [act] * 2
                  + [jax.ShapeDtypeStruct((t_rows, W_TAIL), BF16),
                     jax.ShapeDtypeStruct((8, LANES), F32),
                     jax.ShapeDtypeStruct((1, D), F32),
                     jax.ShapeDtypeStruct((1, HEAD_V), F32)],
        scratch_shapes=[pltpu.VMEM((3, D, D), BF16), pltpu.SemaphoreType.DMA],
        compiler_params=_params(("arbitrary",), 56),
    )(h, tgt, yc, o, proj, proj, proj, w3, gamma, g_post)


def _wgrad_t(a_t, b, name, out_dtype=BF16):
    m, t_rows = a_t.shape
    n = b.shape[1]
    tn = D if n % D == 0 else n
    tk = _pick_tile(t_rows, 768, LANES)
    n_k = t_rows // tk

    def body(a_ref, b_ref, o_ref, acc):
        k = pl.program_id(1)

        @pl.when(k == 0)
        def _():
            acc[...] = jnp.zeros_like(acc)

        acc[...] += _dot(a_ref[...], b_ref[...].astype(BF16))

        @pl.when(k == n_k - 1)
        def _():
            o_ref[...] = jnp.transpose(acc[...]).astype(out_dtype)

    return pl.pallas_call(
        body, name=name, grid=(n // tn, n_k),
        in_specs=[pl.BlockSpec((m, tk), lambda j, k: (0, k)),
                  pl.BlockSpec((tk, tn), lambda j, k: (k, j))],
        out_specs=pl.BlockSpec((tn, m), lambda j, k: (j, 0)),
        out_shape=jax.ShapeDtypeStruct((n, m), out_dtype),
        scratch_shapes=[pltpu.VMEM((m, tn), F32)],
        compiler_params=_params(("parallel", "arbitrary"), 48),
    )(a_t, b)


def _dgrad_in(dpc, dpg, dpt, dlr, w_full_t, h, g_pre, dres, token):
    t_rows = h.shape[0]
    tm = _pick_tile(t_rows, 384, 16)
    n_main = N_MAIN

    def body(dpc_ref, dpg_ref, dpt_ref, dlr_ref, w_hbm, h_ref, g_ref, dres_ref, token_ref,
             dh_ref, dg_ref, w_s, wlr_s, w_sems):
        @pl.when(pl.program_id(0) == 0)
        def _():
            _load_weights(w_hbm, w_s, wlr_s, w_sems)
            dg_ref[...] = jnp.zeros_like(dg_ref)

        du = _dot(dlr_ref[...].astype(BF16), wlr_s[...])
        du += _dot(dpc_ref[...], w_s[0:W_CONV, :])
        du += _dot(dpg_ref[...], w_s[W_CONV:W_CONV + W_GLA, :])
        du += _dot(dpt_ref[...], w_s[W_CONV + W_GLA:n_main, :])
        hh = h_ref[...]
        rstd = lax.rsqrt(jnp.mean(hh * hh, axis=-1, keepdims=True) + EPS)
        xhat = hh * rstd
        dg_ref[...] += jnp.sum(du * xhat, axis=0, keepdims=True)
        dx = du * g_ref[...]
        dh_ref[...] = rstd * (dx - xhat * jnp.mean(dx * xhat, axis=-1, keepdims=True)) + dres_ref[...]

    row = lambda width: pl.BlockSpec((tm, width), lambda i: (i, 0))
    return pl.pallas_call(
        body, name="dgrad_in", grid=(t_rows // tm,),
        in_specs=[row(W_CONV), row(W_GLA), row(W_TAIL), row(LANES),
                  pl.BlockSpec(memory_space=pl.ANY),
                  row(D), pl.BlockSpec((1, D), lambda i: (0, 0)), row(D),
                  pl.BlockSpec((8, LANES), lambda i: (0, 0))],
        out_specs=[row(D), pl.BlockSpec((1, D), lambda i: (0, 0))],
        out_shape=[jax.ShapeDtypeStruct((t_rows, D), F32), jax.ShapeDtypeStruct((1, D), F32)],
        scratch_shapes=[pltpu.VMEM((n_main, D), BF16), pltpu.VMEM((LANES, D), BF16),
                        pltpu.SemaphoreType.DMA((N_WEIGHT_COPIES,))],
        compiler_params=_params(("arbitrary",), 56),
    )(dpc, dpg, dpt, dlr, w_full_t, h, g_pre, dres, token)


def _reference_rows(g_conv, g_gla, g_tail, g_lr):
    conv = g_conv.reshape(N_CONV_TILES, 4, 128, D).transpose(1, 0, 2, 3).reshape(W_CONV, D)
    gla = g_gla.reshape(N_HEADS, 512, D)
    q = gla[:, 0:128].reshape(N_HEADS * HEAD_K, D)
    k = gla[:, 128:256].reshape(N_HEADS * HEAD_K, D)
    v = gla[:, 256:512].reshape(N_HEADS * HEAD_V, D)
    return jnp.concatenate([conv, q, k, v, g_tail[0:D], g_lr[0:2 * RANK], g_tail[D:3 * D]], axis=0)


def kernel(x, meta_tokens, norm_pre, w_in, conv_w, w_gate_fwd, b_gate_fwd, w_gate_bwd, b_gate_bwd, gla_norm, w_out_conv, w_out_gla, w_merge_out, norm_post, loss_target, m_meta_tokens, m_norm_pre, m_w_in, m_conv_w, m_w_gate_fwd, m_b_gate_fwd, m_w_gate_bwd, m_b_gate_bwd, m_gla_norm, m_w_out_conv, m_w_out_gla, m_w_merge_out, m_norm_post, v_meta_tokens, v_norm_pre, v_w_in, v_conv_w, v_w_gate_fwd, v_b_gate_fwd, v_w_gate_bwd, v_b_gate_bwd, v_gla_norm, v_w_out_conv, v_w_out_gla, v_w_merge_out, v_norm_post):
    n_seq, seq, _ = x.shape
    lf = CHUNK + seq
    t_rows = n_seq * lf
    shard = 2 * lax.axis_index("x") + lax.axis_index("y")
    shard_arr = jnp.reshape(shard, (1,)).astype(jnp.int32)

    w_in_slots = _cast_into_slot(jnp.transpose(w_in[0]), shard_arr, "cast_w_in")
    w_out_slots = _cast_into_slot(jnp.concatenate([w_out_conv[0], w_out_gla[0], w_merge_out[0]], axis=0), shard_arr,
                                  "cast_w_out")
    w_in_all, meta_all, conv_all, wgf_all, wgb_all = _gather_via_sibling(
        "gather_w_in", [w_in_slots, meta_tokens, conv_w[0], w_gate_fwd[0], w_gate_bwd[0]],
        (True, False, False, False, False))
    w_out_state, _ = _plane_start("gather_w_out_start", [w_out_slots], "gather", wgb_all)

    w_full_t = w_in_all.reshape(N_IN, D)
    meta_full = jnp.transpose(meta_all, (1, 0, 2)).reshape(N_META, D)
    conv_full = jnp.transpose(conv_all, (1, 0, 2)).reshape(3, D)
    wgf = jnp.pad(wgf_all, ((0, 0), (0, LANES - RANK), (0, 0))).astype(BF16)
    wgb = jnp.pad(wgb_all, ((0, 0), (RANK, LANES - 2 * RANK), (0, 0))).astype(BF16)
    bgf = b_gate_fwd.reshape(N_HEADS, 1, HEAD_K)
    bgb = b_gate_bwd.reshape(N_HEADS, 1, HEAD_K)

    head = jnp.concatenate([jnp.zeros((PAD_FRONT, D), F32), meta_full], axis=0)
    h = jnp.concatenate([jnp.broadcast_to(head[None], (n_seq, CHUNK, D)), x], axis=1).reshape(t_rows, D)
    tgt = jnp.pad(loss_target, ((0, 0), (CHUNK, 0), (0, 0))).reshape(t_rows, D)

    proj, u_t, lr = _in_proj(h, norm_pre, w_full_t)
    yc = _conv_fwd(proj, conv_full, n_seq, lf)
    o = _gla_fwd(proj, lr, wgf, wgb, bgf, bgb, n_seq, lf)
    (w_out_all,) = _plane_wait("gather_w_out_wait", w_out_state, "gather", o)
    w3 = jnp.transpose(w_out_all.reshape(4, 3, D // 4, D), (1, 0, 2, 3)).reshape(3, D, D)
    (dres, yg, merged, dout_t, dpc_t, dpg_t, dyc, d_o, dtail, loss_acc, d_gpost, d_gamma) = _tail(
        h, tgt, yc, o, proj, w3, gla_norm, norm_post, lf)
    g_w_oc = _wgrad_t(dpc_t, yc, "wgrad_out_conv")
    g_w_og = _wgrad_t(dpg_t, yg, "wgrad_out_gla")
    g_w_mo = _wgrad_t(dout_t, merged, "wgrad_merge_out")
    g_out_slots = jnp.concatenate([g.reshape(4, D // 4, D) for g in (g_w_oc, g_w_og, g_w_mo)], axis=1)
    out_state, out_token = _plane_start("scatter_out_grads_start", [g_out_slots], "scatter", g_w_mo)
    dgla, dlr, dwgf_p, dwgb_p, dbg_p = _gla_bwd(proj, lr, d_o, wgf, wgb, bgf, bgb, n_seq, lf, out_token)
    (got_out,) = _plane_wait("scatter_out_grads_wait", out_state, "scatter", dlr)
    dconv, dconvw_p = _conv_bwd(proj, conv_full, dyc, n_seq, lf)
    g_conv = _wgrad_t(u_t, dconv, "wgrad_in_conv")
    g_gla = _wgrad_t(u_t, dgla, "wgrad_in_gla")
    g_tail = _wgrad_t(u_t, dtail, "wgrad_in_tail")
    g_lr = _wgrad_t(u_t, dlr, "wgrad_in_lr")

    g_in_slots = _reference_rows(g_conv, g_gla, g_tail, g_lr).reshape(4, SHARD_IN, D)
    in_state, in_token = _plane_start("scatter_in_grads_start", [g_in_slots], "scatter", g_lr)
    dh, d_gpre = _dgrad_in(dconv, dgla, dtail, dlr, w_full_t, h, norm_pre, dres, in_token)
    (got_in,) = _plane_wait("scatter_in_grads_wait", in_state, "scatter", d_gpre)

    plane_in = _sum_slots(got_in, "sum_w_in_grads", own=g_in_slots, slot=shard_arr)
    plane_out = _sum_slots(got_out, "sum_w_out_grads", own=g_out_slots, slot=shard_arr)
    swap_state, swap_token = _plane_start("swap_plane_sums_start", [plane_in, plane_out], "swap", plane_out)

    dh3 = dh.reshape(n_seq, lf, D)
    grad_x = dh3[:, CHUNK:, :]

    d_meta = jnp.sum(dh3[:, PAD_FRONT:CHUNK, :], axis=0)
    d_convw = jnp.sum(dconvw_p, axis=0)
    d_wgf = jnp.transpose(jnp.sum(dwgf_p, axis=0)[:, 0:RANK, :], (1, 0, 2)).reshape(RANK, N_HEADS * HEAD_K)
    d_wgb = jnp.transpose(jnp.sum(dwgb_p, axis=0)[:, RANK:2 * RANK, :], (1, 0, 2)).reshape(RANK, N_HEADS * HEAD_K)
    d_bg = jnp.sum(dbg_p, axis=0)
    d_bgf = d_bg[:, 0, :].reshape(1, N_HEADS * HEAD_K)
    d_bgb = d_bg[:, 1, :].reshape(1, N_HEADS * HEAD_K)
    loss_part = loss_acc[0:1, :] + swap_token[0:1, :]
    partials = [d_meta, d_convw, d_wgf, d_wgb, d_gpre, d_bgf, d_bgb, d_gamma, d_gpost, loss_part]
    (g_meta, g_convw, g_wgf, g_wgb, g_npre, g_bgf, g_bgb, g_gnorm, g_npost, loss_row) = _sum_small(
        _gather_all("gather_small_grads", partials), "sum_small_grads")
    loss = loss_row[0, 0]
    small_out = _adamw_small(
        [(meta_tokens, g_meta, m_meta_tokens, v_meta_tokens), (norm_pre, g_npre, m_norm_pre, v_norm_pre),
         (conv_w, g_convw, m_conv_w, v_conv_w), (w_gate_fwd, g_wgf, m_w_gate_fwd, v_w_gate_fwd),
         (b_gate_fwd, g_bgf, m_b_gate_fwd, v_b_gate_fwd), (w_gate_bwd, g_wgb, m_w_gate_bwd, v_w_gate_bwd),
         (b_gate_bwd, g_bgb, m_b_gate_bwd, v_b_gate_bwd), (gla_norm, g_gnorm, m_gla_norm, v_gla_norm),
         (norm_post, g_npost, m_norm_post, v_norm_post)], shard_arr, "adamw_small")

    other_in, other_out = _plane_wait("swap_plane_sums_wait", swap_state, "swap", small_out[0][0])
    big_in = _adamw(jnp.transpose(w_in[0]), [plane_in, other_in], jnp.transpose(m_w_in[0]), jnp.transpose(v_w_in[0]),
                    "adamw_w_in")
    out_params = ((w_out_conv, m_w_out_conv, v_w_out_conv), (w_out_gla, m_w_out_gla, v_w_out_gla),
                  (w_merge_out, m_w_merge_out, v_w_merge_out))
    big_out = [_adamw(w[0], [plane_out, other_out], m[0], v[0], f"adamw_w_out_{i}", grad_row=i * (D // 4))
               for i, (w, m, v) in enumerate(out_params)]

    results = []
    for kind in range(4):
        small_kind = [p[kind] for p in small_out]
        w_in_part = jnp.transpose(big_in[kind])[None]
        outs3 = [big_out[i][kind][None] for i in range(3)]
        results.extend(small_kind[0:2] + [w_in_part] + small_kind[2:8] + outs3 + small_kind[8:9])
    return (loss, grad_x, *results)
```

```python
import functools

import jax
import jax.numpy as jnp
from jax import lax
from jax.experimental import pallas as pl
from jax.experimental.pallas import tpu as pltpu

F32 = jnp.float32
BF16 = jnp.bfloat16
MESH = pl.DeviceIdType.MESH

D = 1024
N_META = 16
CHUNK = 64
PAD_FRONT = CHUNK - N_META
N_HEADS = 4
HEAD_K = 128
HEAD_V = 256
RANK = 16
EPS = 1e-6
GATE_NORM = 16.0
N_IN = 9248
SHARD_IN = N_IN // 4
LANES = 128
N_CONV_TILES = 8
W_CONV = 4096
W_GLA = 2048
W_TAIL = 3072
N_MAIN = W_CONV + W_GLA + W_TAIL
OFF_Q, OFF_K, OFF_V, OFF_R = 4096, 4608, 5120, 6144
OFF_LR, OFF_MA = 7168, 7200
MIB = 1024 * 1024

ADAM_LR = 0.001
ADAM_B1 = 0.9
ADAM_B2 = 0.999
ADAM_EPS = 1e-08
ADAM_WD = 0.01
ADAM_STEP = 10


def _params(sem=None, vmem_mib=None):
    return pltpu.CompilerParams(
        dimension_semantics=sem,
        vmem_limit_bytes=None if vmem_mib is None else vmem_mib * MIB)


def _pick_tile(n, target, mult):
    best = None
    for t in range(mult, min(n, target) + 1, mult):
        if n % t == 0:
            best = t
    return n if best is None else best


def _sigmoid(v):
    return 1.0 / (1.0 + jnp.exp(-v))


def _log_sigmoid(v):
    return jnp.minimum(v, 0.0) - jnp.log(1.0 + jnp.exp(-jnp.abs(v)))


def _dot(a, b):
    return jnp.dot(a, b, preferred_element_type=F32)


def _dot_nt(a, b):
    return lax.dot_general(a, b, (((1,), (1,)), ((), ())), preferred_element_type=F32)


def _dot_tn(a, b):
    return lax.dot_general(a, b, (((0,), (0,)), ((), ())), preferred_element_type=F32)


def _gather_all(name, arrs):
    n = len(arrs)
    flips = tuple((m >> 2 & 1, m >> 1 & 1, m & 1) for m in range(1, 8))

    def body(*refs):
        ins, outs = refs[:n], refs[n:2 * n]
        send_sems, recv_sems, local_sems = refs[2 * n:]
        pos = (lax.axis_index("x"), lax.axis_index("y"), lax.axis_index("c"))

        def slot_of(p):
            return 4 * p[0] + 2 * p[1] + p[2]

        peers = [tuple(1 - pos[a] if f[a] else pos[a] for a in range(3)) for f in flips]
        me = slot_of(pos)
        copies = []
        for i in range(n):
            cp = pltpu.make_async_copy(ins[i], outs[i].at[me], local_sems.at[i])
            cp.start()
            copies.append(cp)
        sends = []
        for i in range(n):
            for k, peer in enumerate(peers):
                cp = pltpu.make_async_remote_copy(
                    src_ref=ins[i], dst_ref=outs[i].at[me], send_sem=send_sems.at[i, k], recv_sem=recv_sems.at[i, k],
                    device_id=peer, device_id_type=MESH)
                cp.start()
                sends.append(cp)
        for i in range(n):
            for k, peer in enumerate(peers):
                pltpu.make_async_remote_copy(
                    src_ref=ins[i], dst_ref=outs[i].at[slot_of(peer)], send_sem=send_sems.at[i, k],
                    recv_sem=recv_sems.at[i, k], device_id=peer, device_id_type=MESH).wait_recv()
        for cp in sends:
            cp.wait_send()
        for cp in copies:
            cp.wait()

    hbm = pl.BlockSpec(memory_space=pl.ANY)
    outs = pl.pallas_call(
        body, name=name, out_shape=[jax.ShapeDtypeStruct((8,) + a.shape, a.dtype) for a in arrs],
        in_specs=[hbm] * n, out_specs=[hbm] * n,
        scratch_shapes=[pltpu.SemaphoreType.DMA((n, 7)), pltpu.SemaphoreType.DMA((n, 7)),
                        pltpu.SemaphoreType.DMA((n,))],
        compiler_params=pltpu.CompilerParams(has_side_effects=True),
    )(*arrs)
    return list(outs)


def _gather_via_sibling(name, arrs, slotted):
    n = len(arrs)
    out_shape = [jax.ShapeDtypeStruct(a.shape if slotted[i] else (4,) + a.shape, a.dtype)
                 for i, a in enumerate(arrs)]

    def body(*refs):
        ins, outs = refs[:n], refs[n:2 * n]
        send_sems, recv_sems, local_sems = refs[2 * n:]
        x, y, c = lax.axis_index("x"), lax.axis_index("y"), lax.axis_index("c")
        me = 2 * x + y
        chips = [(1 - x, y), (x, 1 - y), (1 - x, 1 - y)]

        def half(ref, which):
            rows = ref.shape[0]
            cut = rows // 2 // 16 * 16
            return ref.at[pl.ds(0, cut)] if which == 0 else ref.at[pl.ds(cut, rows - cut)]

        def copy(src, dst, i, k, to):
            return pltpu.make_async_remote_copy(
                src_ref=src, dst_ref=dst, send_sem=send_sems.at[i, k], recv_sem=recv_sems.at[i, k],
                device_id=to, device_id_type=MESH)

        def run(mine):
            other = 1 - mine
            local, sends = [], []
            whole = [(not slotted[i]) and arrs[i].shape[0] < 32 for i in range(n)]
            for i in range(n):
                own = outs[i].at[me] if slotted[i] else ins[i]
                if not slotted[i]:
                    cp = pltpu.make_async_copy(ins[i], outs[i].at[me], local_sems.at[i])
                    cp.start()
                    local.append(cp)
                for k, (px, py) in enumerate(chips):
                    if whole[i]:
                        cp = copy(own, outs[i].at[me], i, k, (px, py, mine))
                    elif k < 2:
                        cp = copy(half(own, mine), half(outs[i].at[me], mine), i, k, (px, py, mine))
                    else:
                        continue
                    cp.start()
                    sends.append(cp)
            via = mine
            for k in (via, 1 - via, 2):
                px, py = chips[k]
                slot = 2 * px + py
                source = (px, py, mine) if k < 2 else chips[1 - via] + (mine,)
                for i in range(n):
                    if whole[i]:
                        copy(outs[i].at[slot], outs[i].at[slot], i, k, (px, py, mine)).wait_recv()
                        continue
                    landed = half(outs[i].at[slot], mine)
                    copy(landed, landed, i, k, source).wait_recv()
                    if k == via:
                        cp = copy(landed, landed, i, 2, chips[1 - via] + (mine,))
                        cp.start()
                        sends.append(cp)
                    cp = copy(landed, landed, i, 3 + k, (x, y, other))
                    cp.start()
                    sends.append(cp)
            for k, (px, py) in enumerate(chips):
                slot = 2 * px + py
                for i in range(n):
                    if whole[i]:
                        continue
                    passed = half(outs[i].at[slot], other)
                    copy(passed, passed, i, 3 + k, (x, y, other)).wait_recv()
            for cp in sends:
                cp.wait_send()
            for cp in local:
                cp.wait()

        for mine in (0, 1):
            pl.when(c == mine)(functools.partial(run, mine))

    hbm = pl.BlockSpec(memory_space=pl.ANY)
    outs = pl.pallas_call(
        body, name=name, out_shape=out_shape,
        in_specs=[hbm] * n, out_specs=[hbm] * n,
        scratch_shapes=[pltpu.SemaphoreType.DMA((n, 6)), pltpu.SemaphoreType.DMA((n, 6)),
                        pltpu.SemaphoreType.DMA((n,))],
        input_output_aliases={i: i for i in range(n) if slotted[i]},
        compiler_params=pltpu.CompilerParams(has_side_effects=True),
    )(*arrs)
    return list(outs)


HBM_SPEC = pl.BlockSpec(memory_space=pltpu.HBM)
SEM_SPEC = pl.BlockSpec(memory_space=pltpu.SEMAPHORE)
DATAFLOW = pltpu.SideEffectType.DATAFLOW_SIDE_EFFECTING


def _split_peers(mode):
    x, y, c = lax.axis_index("x"), lax.axis_index("y"), lax.axis_index("c")
    if mode == "swap":
        return 0, [((x, y, 1 - c), 0)]
    return 2 * x + y, [((1 - x, y, c), 2 * (1 - x) + y), ((x, 1 - y, c), 2 * x + 1 - y),
                       ((1 - x, 1 - y, c), 2 * (1 - x) + 1 - y)]


def _split_refs(mode, src, landing, me, peer_slot):
    if mode == "gather":
        return src.at[me], landing.at[me]
    if mode == "scatter":
        return src.at[peer_slot], landing.at[me]
    return src, landing


def _plane_start(name, arrs, mode, after):
    n = len(arrs)
    n_peers = 1 if mode == "swap" else 3
    if mode == "gather":
        srcs, lands = [], list(arrs)
    else:
        srcs, lands = list(arrs), [lax.empty(a.shape, a.dtype) for a in arrs]
    n_src = len(srcs)

    def body(*refs):
        landing = refs[n_src:n_src + n]
        sources = refs[:n_src] if n_src else landing
        send_sems, recv_sems = refs[n_src + n + 1], refs[n_src + n + 2]
        token = refs[-1]
        me, peers = _split_peers(mode)
        for i in range(n):
            for k, (peer, peer_slot) in enumerate(peers):
                src, dst = _split_refs(mode, sources[i], landing[i], me, peer_slot)
                pltpu.make_async_remote_copy(
                    src_ref=src, dst_ref=dst, send_sem=send_sems.at[n_peers * i + k],
                    recv_sem=recv_sems.at[n_peers * i + k], device_id=peer, device_id_type=MESH).start()
        token[...] = jnp.zeros_like(token)

    hbm_in = [pltpu.with_memory_space_constraint(a, pltpu.HBM) for a in srcs + lands]
    out = pl.pallas_call(
        body, name=name,
        out_shape=[pltpu.SemaphoreType.DMA((n_peers * n,)), pltpu.SemaphoreType.DMA((n_peers * n,))]
                  + [pltpu.HBM(a.shape, a.dtype) for a in lands]
                  + [jax.ShapeDtypeStruct((8, LANES), F32)],
        in_specs=[HBM_SPEC] * (n_src + n) + [pl.BlockSpec(memory_space=pl.ANY)],
        out_specs=[SEM_SPEC, SEM_SPEC] + [HBM_SPEC] * n + [pl.BlockSpec(memory_space=pltpu.VMEM)],
        input_output_aliases={n_src + i: 2 + i for i in range(n)},
        compiler_params=pltpu.CompilerParams(has_side_effects=DATAFLOW),
    )(*hbm_in, after)
    return out[:-1], out[-1]


def _plane_wait(name, state, mode, after):
    send_sems, recv_sems = state[0], state[1]
    bufs = list(state[2:])
    n = len(bufs)
    n_peers = 1 if mode == "swap" else 3

    def body(*refs):
        landing = refs[:n]
        send_sems, recv_sems = refs[n], refs[n + 1]
        _, peers = _split_peers(mode)
        for i in range(n):
            for k, (peer, peer_slot) in enumerate(peers):
                arrived = landing[i] if mode == "swap" else landing[i].at[peer_slot]
                cp = pltpu.make_async_remote_copy(
                    src_ref=arrived, dst_ref=arrived, send_sem=send_sems.at[n_peers * i + k],
                    recv_sem=recv_sems.at[n_peers * i + k], device_id=peer, device_id_type=MESH)
                cp.wait_send()
                cp.wait_recv()

    out = pl.pallas_call(
        body, name=name,
        out_shape=[pltpu.HBM(a.shape, a.dtype) for a in bufs],
        in_specs=[HBM_SPEC] * n + [SEM_SPEC, SEM_SPEC, pl.BlockSpec(memory_space=pl.ANY)],
        out_specs=[HBM_SPEC] * n,
        input_output_aliases={i: i for i in range(n)},
        compiler_params=pltpu.CompilerParams(has_side_effects=DATAFLOW),
    )(*bufs, send_sems, recv_sems, after)
    return list(out)


def _tile_2d(rows, cols, row_mult, max_elems=512 * 1024):
    if rows % row_mult == 0:
        rt = _pick_tile(rows, max(row_mult, max_elems // cols), row_mult)
        return (rt, cols), rows // rt, lambda i: (i, 0)
    ct = _pick_tile(cols, max(LANES, max_elems // rows), LANES)
    return (rows, ct), cols // ct, lambda i: (0, i)


def _cast_into_slot(a, slot, name):
    block, steps, index = _tile_2d(a.shape[0], a.shape[1], 16)

    def body(slot_ref, a_ref, o_ref):
        o_ref[...] = a_ref[...].astype(BF16)

    return pl.pallas_call(
        body, name=name,
        grid_spec=pltpu.PrefetchScalarGridSpec(
            num_scalar_prefetch=1, grid=(steps,),
            in_specs=[pl.BlockSpec(block, lambda i, s: index(i))],
            out_specs=pl.BlockSpec((None,) + block, lambda i, s: (s[0],) + index(i))),
        out_shape=jax.ShapeDtypeStruct((4,) + a.shape, BF16),
        compiler_params=_params(("arbitrary",)),
    )(slot, a)


def _sum_slots(buf, name, own=None, slot=None):
    n_slots, rows, cols = buf.shape
    (br, bc), steps, index = _tile_2d(rows, cols, 16, 320 * 1024)

    def body(*refs):
        if own is None:
            b_ref, o_ref = refs
        else:
            slot_ref, b_ref, own_ref, o_ref = refs
        acc = None
        for s in range(n_slots):
            term = b_ref[s] if own is None else jnp.where(slot_ref[0] == s, own_ref[...], b_ref[s])
            acc = term.astype(F32) if acc is None else acc + term.astype(F32)
        o_ref[...] = acc

    out_shape = jax.ShapeDtypeStruct((rows, cols), F32)
    if own is None:
        return pl.pallas_call(
            body, name=name, grid=(steps,),
            in_specs=[pl.BlockSpec((n_slots, br, bc), lambda i: (0,) + index(i))],
            out_specs=pl.BlockSpec((br, bc), index), out_shape=out_shape,
            compiler_params=_params(("parallel",), 48),
        )(buf)
    return pl.pallas_call(
        body, name=name,
        grid_spec=pltpu.PrefetchScalarGridSpec(
            num_scalar_prefetch=1, grid=(steps,),
            in_specs=[pl.BlockSpec((n_slots, br, bc), lambda i, s: (0,) + index(i)),
                      pl.BlockSpec((None, br, bc), lambda i, s: (s[0],) + index(i))],
            out_specs=pl.BlockSpec((br, bc), lambda i, s: index(i))),
        out_shape=out_shape,
        compiler_params=_params(("arbitrary",), 48),
    )(slot, buf, own)


def _sum_small(bufs, name):
    n = len(bufs)

    def body(*refs):
        for b_ref, o_ref in zip(refs[:n], refs[n:]):
            acc = b_ref[0]
            for s in range(1, b_ref.shape[0]):
                acc = acc + b_ref[s]
            o_ref[...] = acc

    vmem = pl.BlockSpec(memory_space=pltpu.VMEM)
    return pl.pallas_call(
        body, name=name, in_specs=[vmem] * n, out_specs=[vmem] * n,
        out_shape=[jax.ShapeDtypeStruct(b.shape[1:], b.dtype) for b in bufs],
    )(*bufs)


def _adam_update(w, g, m, v):
    c1 = 1.0 - ADAM_B1 ** ADAM_STEP
    c2 = 1.0 - ADAM_B2 ** ADAM_STEP
    m_new = ADAM_B1 * m + (1.0 - ADAM_B1) * g
    v_new = ADAM_B2 * v + (1.0 - ADAM_B2) * (g * g)
    m_hat = m_new / c1
    v_hat = v_new / c2
    return -ADAM_LR * (m_hat / (jnp.sqrt(v_hat) + ADAM_EPS) + ADAM_WD * w), m_new, v_new


def _adamw_small(params, slot, name):
    n = len(params)

    def spec_of(shape):
        lead = (None,) * (len(shape) - 2)
        return pl.BlockSpec(lead + tuple(shape[-2:]), lambda i, s, k=len(shape): (0,) * k)

    in_specs, operands, out_specs, out_shape = [], [], [], []
    for w, g, m, v in params:
        shard = g.shape[-1] != w.shape[-1]
        g_spec = pl.BlockSpec(tuple(w.shape[-2:]), (lambda i, s: (0, s[0])) if shard else (lambda i, s: (0, 0)))
        in_specs += [spec_of(w.shape), g_spec, spec_of(m.shape), spec_of(v.shape)]
        operands += [w, g, m, v]
        out_specs += [spec_of(w.shape)] * 4
        out_shape += [jax.ShapeDtypeStruct(w.shape, F32)] * 4

    def body(slot_ref, *refs):
        ins, outs = refs[:4 * n], refs[4 * n:]
        for p in range(n):
            w_ref, g_ref, m_ref, v_ref = ins[4 * p:4 * p + 4]
            g = g_ref[...]
            delta, m_new, v_new = _adam_update(w_ref[...], g, m_ref[...], v_ref[...])
            for o_ref, val in zip(outs[4 * p:4 * p + 4], (g, delta, m_new, v_new)):
                o_ref[...] = val

    out = pl.pallas_call(
        body, name=name,
        grid_spec=pltpu.PrefetchScalarGridSpec(num_scalar_prefetch=1, grid=(1,), in_specs=in_specs, out_specs=out_specs),
        out_shape=out_shape,
    )(slot, *operands)
    return [tuple(out[4 * p:4 * p + 4]) for p in range(n)]


def _adamw(w, grads, m, v, name, grad_row=0):
    rows, cols = w.shape
    (rt, _), _, _ = _tile_2d(rows, cols, 8, 160 * 1024)
    assert grad_row % rt == 0
    n_g = len(grads)

    def body(*refs):
        w_ref = refs[0]
        g_refs = refs[1:1 + n_g]
        m_ref, v_ref, g_out, d_out, m_out, v_out = refs[1 + n_g:]
        g = g_refs[0][...]
        for r in g_refs[1:]:
            g = g + r[...]
        g_out[...] = g
        d_out[...], m_out[...], v_out[...] = _adam_update(w_ref[...], g, m_ref[...], v_ref[...])

    spec = pl.BlockSpec((rt, cols), lambda i: (i, 0))
    grad_spec = pl.BlockSpec((rt, cols), lambda i: (i + grad_row // rt, 0))
    shape = jax.ShapeDtypeStruct((rows, cols), F32)
    return pl.pallas_call(
        body, name=name, grid=(rows // rt,),
        in_specs=[spec] + [grad_spec] * n_g + [spec] * 2, out_specs=[spec] * 4, out_shape=[shape] * 4,
        compiler_params=_params(("parallel",), 48),
    )(w, *grads, m, v)


def _weight_pieces():
    pieces = []
    for j in range(N_CONV_TILES):
        for g in range(4):
            pieces.append((512 * j + 128 * g, D * g + 128 * j, 128))
    for hd in range(N_HEADS):
        base = W_CONV + 512 * hd
        pieces.append((base, OFF_Q + HEAD_K * hd, HEAD_K))
        pieces.append((base + HEAD_K, OFF_K + HEAD_K * hd, HEAD_K))
        pieces.append((base + 2 * HEAD_K, OFF_V + HEAD_V * hd, HEAD_V))
    pieces.append((W_CONV + W_GLA, OFF_R, D))
    pieces.append((W_CONV + W_GLA + D, OFF_MA, 2 * D))
    return pieces


N_WEIGHT_COPIES = len(_weight_pieces()) + 1


def _load_weights(w_hbm, w_s, wlr_s, sems):
    copies = [pltpu.make_async_copy(w_hbm.at[pl.ds(src, n)], w_s.at[pl.ds(dst, n)], sems.at[i])
              for i, (dst, src, n) in enumerate(_weight_pieces())]
    copies.append(pltpu.make_async_copy(w_hbm.at[pl.ds(OFF_LR, LANES)], wlr_s, sems.at[N_WEIGHT_COPIES - 1]))
    for cp in copies:
        cp.start()
    for cp in copies:
        cp.wait()


def _in_proj(h, g_pre, w_full_t):
    t_rows = h.shape[0]
    tm = _pick_tile(t_rows, 384, LANES)
    n_main = N_MAIN

    def body(h_ref, g_ref, w_hbm, proj_ref, ut_ref, lr_ref, w_s, wlr_s, w_sems):
        @pl.when(pl.program_id(0) == 0)
        def _():
            _load_weights(w_hbm, w_s, wlr_s, w_sems)

        hh = h_ref[...]
        rstd = lax.rsqrt(jnp.mean(hh * hh, axis=-1, keepdims=True) + EPS)
        uf = hh * rstd * g_ref[...]
        u = uf.astype(BF16)
        ut_ref[...] = jnp.transpose(uf).astype(BF16)
        lr_ref[...] = _dot_nt(u, wlr_s[...])
        for j in range(n_main // D):
            cols = slice(j * D, (j + 1) * D)
            proj_ref[:, cols] = _dot_nt(u, w_s[cols, :]).astype(BF16)

    return pl.pallas_call(
        body, name="in_proj", grid=(t_rows // tm,),
        in_specs=[pl.BlockSpec((tm, D), lambda i: (i, 0)),
                  pl.BlockSpec((1, D), lambda i: (0, 0)),
                  pl.BlockSpec(memory_space=pl.ANY)],
        out_specs=[pl.BlockSpec((tm, n_main), lambda i: (i, 0)),
                   pl.BlockSpec((D, tm), lambda i: (0, i)),
                   pl.BlockSpec((tm, LANES), lambda i: (i, 0))],
        out_shape=[jax.ShapeDtypeStruct((t_rows, n_main), BF16),
                   jax.ShapeDtypeStruct((D, t_rows), BF16),
                   jax.ShapeDtypeStruct((t_rows, LANES), F32)],
        scratch_shapes=[pltpu.VMEM((n_main, D), BF16), pltpu.VMEM((LANES, D), BF16),
                        pltpu.SemaphoreType.DMA((N_WEIGHT_COPIES,))],
        compiler_params=_params(("arbitrary",), 56),
    )(h, g_pre, w_full_t)


CONV_TILES_PER_STEP = 4


def _conv_parts(p_ref, w_ref, t):
    cb = p_ref[:, 512 * t:512 * t + 128].astype(F32)
    cc = p_ref[:, 512 * t + 128:512 * t + 256].astype(F32)
    cx = p_ref[:, 512 * t + 256:512 * t + 384].astype(F32)
    cz = p_ref[:, 512 * t + 384:512 * t + 512].astype(F32)
    rows = cb.shape[0]
    w = w_ref[:, 128 * t:128 * (t + 1)]
    p = cc * cx
    conv = pltpu.roll(p, 1, 0) * w[0:1] + p * w[1:2] + pltpu.roll(p, rows - 1, 0) * w[2:3]
    sz = _sigmoid(cz)
    return cb, cc, cx, cz, p, conv, sz, w


def _conv_fwd(proj, conv_w, n_seq, lf):
    per = CONV_TILES_PER_STEP

    def body(p_ref, w_ref, y_ref):
        for t in range(per):
            cb, _, _, cz, _, conv, sz, _ = _conv_parts(p_ref, w_ref, t)
            y_ref[:, 128 * t:128 * (t + 1)] = (cb * conv * (cz * sz)).astype(BF16)

    return pl.pallas_call(
        body, name="conv_fwd", grid=(n_seq, N_CONV_TILES // per),
        in_specs=[pl.BlockSpec((lf, 512 * per), lambda b, j: (b, j)),
                  pl.BlockSpec((3, 128 * per), lambda b, j: (0, j))],
        out_specs=pl.BlockSpec((lf, 128 * per), lambda b, j: (b, j)),
        out_shape=jax.ShapeDtypeStruct((n_seq * lf, D), BF16),
        compiler_params=_params(("parallel", "parallel"), 48),
    )(proj, conv_w)


def _conv_bwd(proj, conv_w, dyc, n_seq, lf):
    per = CONV_TILES_PER_STEP

    def body(p_ref, w_ref, dy_ref, dp_ref, dw_ref):
        for t in range(per):
            cb, cc, cx, cz, p, conv, sz, w = _conv_parts(p_ref, w_ref, t)
            rows = cb.shape[0]
            dy = dy_ref[:, 128 * t:128 * (t + 1)].astype(F32)
            silu = cz * sz
            dcb = dy * conv * silu
            dconv = dy * cb * silu
            dcz = dy * cb * conv * (sz * (1.0 + cz * (1.0 - sz)))
            d_next = pltpu.roll(dconv, rows - 1, 0)
            d_prev = pltpu.roll(dconv, 1, 0)
            dp = d_next * w[0:1] + dconv * w[1:2] + d_prev * w[2:3]
            base = 512 * t
            dp_ref[:, base:base + 128] = dcb.astype(BF16)
            dp_ref[:, base + 128:base + 256] = (dp * cx).astype(BF16)
            dp_ref[:, base + 256:base + 384] = (dp * cc).astype(BF16)
            dp_ref[:, base + 384:base + 512] = dcz.astype(BF16)
            lanes = slice(128 * t, 128 * (t + 1))
            dw_ref[0:1, lanes] = jnp.sum(dconv * pltpu.roll(p, 1, 0), axis=0, keepdims=True)
            dw_ref[1:2, lanes] = jnp.sum(dconv * p, axis=0, keepdims=True)
            dw_ref[2:3, lanes] = jnp.sum(dconv * pltpu.roll(p, rows - 1, 0), axis=0, keepdims=True)

    return pl.pallas_call(
        body, name="conv_bwd", grid=(n_seq, N_CONV_TILES // per),
        in_specs=[pl.BlockSpec((lf, 512 * per), lambda b, j: (b, j)),
                  pl.BlockSpec((3, 128 * per), lambda b, j: (0, j)),
                  pl.BlockSpec((lf, 128 * per), lambda b, j: (b, j))],
        out_specs=[pl.BlockSpec((lf, 512 * per), lambda b, j: (b, j)),
                   pl.BlockSpec((None, 3, 128 * per), lambda b, j: (b, 0, j))],
        out_shape=[jax.ShapeDtypeStruct((n_seq * lf, W_CONV), BF16),
                   jax.ShapeDtypeStruct((n_seq, 3, D), F32)],
        compiler_params=_params(("parallel", "parallel"), 48),
    )(proj, conv_w, dyc)


GROUP = 3
GROUP_ROWS = GROUP * CHUNK


def _row_group(shape):
    row = lax.broadcasted_iota(jnp.int32, shape, 0)
    grp = jnp.zeros(shape, jnp.int32)
    for r in range(1, GROUP):
        grp = grp + (row >= r * CHUNK).astype(jnp.int32)
    return grp


def _lane_group(shape, width):
    lane = lax.broadcasted_iota(jnp.int32, shape, 1)
    grp = jnp.zeros(shape, jnp.int32)
    for r in range(1, GROUP):
        grp = grp + (lane >= r * width).astype(jnp.int32)
    return grp


def _score_mask(direction):
    shape = (GROUP_ROWS, GROUP_ROWS)
    row = lax.broadcasted_iota(jnp.int32, shape, 0)
    col = lax.broadcasted_iota(jnp.int32, shape, 1)
    same = _row_group(shape) == _lane_group(shape, CHUNK)
    return same & ((col <= row) if direction == 0 else (col > row))


def _diag_blocks(v):
    w = v.shape[1]
    wide = jnp.concatenate([v] * GROUP, axis=1)
    return jnp.where(_row_group(wide.shape) == _lane_group(wide.shape, w), wide, jnp.zeros_like(wide))


def _per_chunk_dot(lhs, state, transposed):
    outs = []
    for r in range(GROUP):
        rows = lhs[r * CHUNK:(r + 1) * CHUNK, :]
        blk = state[:, r * HEAD_K:(r + 1) * HEAD_K]
        outs.append(_dot_nt(rows, blk) if transposed else _dot(rows, blk))
    return jnp.concatenate(outs, axis=0)


def _chunk_cumsum(v, suffix):
    pos = lax.broadcasted_iota(jnp.int32, v.shape, 0) & (CHUNK - 1)
    shift = 1
    while shift < CHUNK:
        if suffix:
            moved = pltpu.roll(v, GROUP_ROWS - shift, 0)
            v = v + jnp.where(pos < CHUNK - shift, moved, 0.0)
        else:
            moved = pltpu.roll(v, shift, 0)
            v = v + jnp.where(pos >= shift, moved, 0.0)
        shift *= 2
    return v


def _per_chunk_rows(rows_of_chunk):
    w = rows_of_chunk[0].shape[1]
    return jnp.concatenate([jnp.broadcast_to(v, (CHUNK, w)) for v in rows_of_chunk], axis=0)


def _chunk_end_rows(direction, b):
    at = CHUNK - 1 if direction == 0 else 0
    return [b[r * CHUNK + at:r * CHUNK + at + 1, :] for r in range(GROUP)]


def _gla_gates(lr_bf, wg_ref, bg_ref, lf):
    z = _dot(lr_bf, wg_ref[...]) + bg_ref[...]
    valid = lax.broadcasted_iota(jnp.int32, (lf, HEAD_K), 0) >= PAD_FRONT
    return z, valid


def _group_unroll(n_groups):
    return n_groups if n_groups <= 11 else 1


def _group_rows(g):
    return pl.ds(pl.multiple_of(g * GROUP_ROWS, GROUP_ROWS), GROUP_ROWS)


def _chunk_decay(direction, g, r, b_s):
    base = g * GROUP_ROWS + r * CHUNK
    if direction == 0:
        grp = b_s[pl.ds(pl.multiple_of(base + CHUNK - 8, 8), 8), :]
        return jnp.exp(grp[7:8, :])
    grp = b_s[pl.ds(pl.multiple_of(base, 8), 8), :]
    return jnp.exp(grp[0:1, :])


def _state_scan(direction, n_groups, b_s, st_s, reverse):
    ascending = (direction == 0) != reverse

    def step(i, carry):
        g = i if ascending else n_groups - 1 - i
        for rr in range(GROUP):
            r = rr if ascending else GROUP - 1 - rr
            lanes = slice(r * HEAD_K, (r + 1) * HEAD_K)
            decay = _chunk_decay(direction, g, r, b_s)
            local = st_s[g, :, lanes]
            st_s[g, :, lanes] = carry
            carry = (local + carry * decay) if reverse else (carry * decay + local)
        return carry

    lax.fori_loop(0, n_groups, step, jnp.zeros((HEAD_V, HEAD_K), F32), unroll=_group_unroll(n_groups))


def _gla_states(direction, n_groups, qkv_ref, g_s, b_s, st_s):
    def local(g, carry):
        rows = _group_rows(g)
        b = _chunk_cumsum(g_s[rows, :], direction == 1)
        b_s[rows, :] = b
        b_end = _per_chunk_rows(_chunk_end_rows(direction, b))
        k = qkv_ref[rows, 128:256].astype(F32)
        v = qkv_ref[rows, 256:512]
        k_dec = (k * jnp.exp(b_end - b)).astype(BF16)
        st_s[g] = _dot_tn(v, _diag_blocks(k_dec))
        return carry

    lax.fori_loop(0, n_groups, local, 0, unroll=_group_unroll(n_groups))
    _state_scan(direction, n_groups, b_s, st_s, False)


def _gla_fwd(proj, lr, wgf, wgb, bgf, bgb, n_seq, lf):
    assert lf % GROUP_ROWS == 0
    n_groups = lf // GROUP_ROWS
    scale = HEAD_K ** -0.5

    def body(qkv_ref, lr_ref, wgf_ref, wgb_ref, bgf_ref, bgb_ref, o_ref, g_s, b_s2, st_s2):
        lr_bf = lr_ref[...].astype(BF16)
        for direction in (0, 1):
            wg_ref, bg_ref = ((wgf_ref, bgf_ref), (wgb_ref, bgb_ref))[direction]
            z, valid = _gla_gates(lr_bf, wg_ref, bg_ref, lf)
            g_s[...] = jnp.where(valid, _log_sigmoid(z) / GATE_NORM, 0.0)
            _gla_states(direction, n_groups, qkv_ref, g_s, b_s2.at[direction], st_s2.at[direction])
        masks = [_score_mask(0), _score_mask(1)]

        def out(g, carry):
            rows = _group_rows(g)
            q = qkv_ref[rows, 0:128].astype(F32) * scale
            k = qkv_ref[rows, 128:256].astype(F32)
            v = qkv_ref[rows, 256:512]
            o = None
            for direction in (0, 1):
                b = b_s2[direction, rows, :]
                q_in = (q * jnp.exp(b)).astype(BF16)
                k_in = (k * jnp.exp(-b)).astype(BF16)
                s = jnp.where(masks[direction], _dot_nt(q_in, k_in), 0.0).astype(BF16)
                part = _dot(s, v) + _per_chunk_dot(q_in, st_s2[direction, g].astype(BF16), True)
                o = part if o is None else o + part
            o_ref[rows, :] = o
            return carry

        lax.fori_loop(0, n_groups, out, 0, unroll=_group_unroll(n_groups))

    return pl.pallas_call(
        body, name="gla_fwd", grid=(n_seq, N_HEADS),
        in_specs=[pl.BlockSpec((lf, 512), lambda b, h: (b, N_CONV_TILES + h)),
                  pl.BlockSpec((lf, LANES), lambda b, h: (b, 0)),
                  pl.BlockSpec((None, LANES, HEAD_K), lambda b, h: (h, 0, 0)),
                  pl.BlockSpec((None, LANES, HEAD_K), lambda b, h: (h, 0, 0)),
                  pl.BlockSpec((None, 1, HEAD_K), lambda b, h: (h, 0, 0)),
                  pl.BlockSpec((None, 1, HEAD_K), lambda b, h: (h, 0, 0))],
        out_specs=pl.BlockSpec((lf, HEAD_V), lambda b, h: (b, h)),
        out_shape=jax.ShapeDtypeStruct((n_seq * lf, D), F32),
        scratch_shapes=[pltpu.VMEM((lf, HEAD_K), F32), pltpu.VMEM((2, lf, HEAD_K), F32),
                        pltpu.VMEM((2, n_groups, HEAD_V, GROUP * HEAD_K), F32)],
        compiler_params=_params(("parallel", "parallel"), 48),
    )(proj, lr, wgf, wgb, bgf, bgb)


def _gla_bwd(proj, lr, d_o, wgf, wgb, bgf, bgb, n_seq, lf, token):
    assert lf % GROUP_ROWS == 0
    n_groups = lf // GROUP_ROWS
    scale = HEAD_K ** -0.5

    def body(qkv_ref, lr_ref, do_ref, wgf_ref, wgb_ref, bgf_ref, bgb_ref, token_ref,
             dqkv_ref, dlr_ref, dwgf_ref, dwgb_ref, dbg_ref,
             g_s, b_s2, fac_s2, dg_s2, st_s2, dst_s2):
        lr_bf = lr_ref[...].astype(BF16)
        gates = ((wgf_ref, bgf_ref), (wgb_ref, bgb_ref))
        for direction in (0, 1):
            wg_ref, bg_ref = gates[direction]
            b_s, st_s, dst_s = b_s2.at[direction], st_s2.at[direction], dst_s2.at[direction]
            z, valid = _gla_gates(lr_bf, wg_ref, bg_ref, lf)
            g_s[...] = jnp.where(valid, _log_sigmoid(z) / GATE_NORM, 0.0)
            fac_s2[direction] = jnp.where(valid, _sigmoid(-z) / GATE_NORM, 0.0)
            _gla_states(direction, n_groups, qkv_ref, g_s, b_s, st_s)

            def state_grad_local(g, carry):
                rows = _group_rows(g)
                q = qkv_ref[rows, 0:128].astype(F32) * scale
                q_in = (q * jnp.exp(b_s[rows, :])).astype(BF16)
                dst_s[g] = _dot_tn(do_ref[rows, :], _diag_blocks(q_in))
                return carry

            lax.fori_loop(0, n_groups, state_grad_local, 0, unroll=_group_unroll(n_groups))
            _state_scan(direction, n_groups, b_s, dst_s, True)

        masks = [_score_mask(0), _score_mask(1)]

        def group_grads(g, carry):
            rows = _group_rows(g)
            q = qkv_ref[rows, 0:128].astype(F32) * scale
            k = qkv_ref[rows, 128:256].astype(F32)
            v = qkv_ref[rows, 256:512]
            d_out = do_ref[rows, :]
            dq_sum = dk_sum = dv_sum = None
            for direction in (0, 1):
                end_row = CHUNK - 1 if direction == 0 else 0
                b = b_s2[direction, rows, :]
                ends = _chunk_end_rows(direction, b)
                b_end = _per_chunk_rows(ends)
                e_pos = jnp.exp(b)
                e_neg = jnp.exp(-b)
                e_end = jnp.exp(b_end - b)
                q_in = q * e_pos
                k_in = k * e_neg
                k_dec = k * e_end
                q_in_bf = q_in.astype(BF16)
                k_in_bf = k_in.astype(BF16)
                state = st_s2[direction, g]
                d_state = dst_s2[direction, g]
                state_bf = state.astype(BF16)
                d_state_bf = d_state.astype(BF16)
                s = jnp.where(masks[direction], _dot_nt(q_in_bf, k_in_bf), 0.0).astype(BF16)
                ds = jnp.where(masks[direction], _dot_nt(d_out, v), 0.0).astype(BF16)
                dv = _dot_tn(s, d_out) + _per_chunk_dot(k_dec.astype(BF16), d_state_bf, True)
                dq_in = _dot(ds, k_in_bf) + _per_chunk_dot(d_out, state_bf, False)
                dk_in = _dot_tn(ds, q_in_bf)
                dk_dec = _per_chunk_dot(v, d_state_bf, False)
                dq = dq_in * e_pos * scale
                dk = dk_in * e_neg + dk_dec * e_end
                dq_sum = dq if dq_sum is None else dq_sum + dq
                dk_sum = dk if dk_sum is None else dk_sum + dk
                dv_sum = dv if dv_sum is None else dv_sum + dv
                dkk = dk_dec * k_dec
                db = dq_in * q_in - dk_in * k_in - dkk
                d_decay = jnp.sum(d_state * state, axis=0, keepdims=True)
                db_end = [jnp.sum(dkk[r * CHUNK:(r + 1) * CHUNK, :], axis=0, keepdims=True)
                          + d_decay[:, r * HEAD_K:(r + 1) * HEAD_K] * jnp.exp(ends[r]) for r in range(GROUP)]
                row = lax.broadcasted_iota(jnp.int32, (GROUP_ROWS, HEAD_K), 0)
                at_end = row == end_row
                for r in range(1, GROUP):
                    at_end = at_end | (row == r * CHUNK + end_row)
                db = db + jnp.where(at_end, _per_chunk_rows(db_end), 0.0)
                dg_s2[direction, rows, :] = _chunk_cumsum(db, direction == 0)
            dqkv_ref[rows, 0:128] = dq_sum.astype(BF16)
            dqkv_ref[rows, 128:256] = dk_sum.astype(BF16)
            dqkv_ref[rows, 256:512] = dv_sum.astype(BF16)
            return carry

        lax.fori_loop(0, n_groups, group_grads, 0, unroll=_group_unroll(n_groups))

        dlr = jnp.zeros((lf, LANES), F32)
        for direction in (0, 1):
            dz = dg_s2[direction] * fac_s2[direction]
            dz_bf = dz.astype(BF16)
            dbg_ref[direction:direction + 1, :] = jnp.sum(dz, axis=0, keepdims=True)
            (dwgf_ref, dwgb_ref)[direction][...] = _dot_tn(lr_bf, dz_bf)
            dlr = dlr + _dot_nt(dz_bf, gates[direction][0][...])

        @pl.when(pl.program_id(1) == 0)
        def _():
            dlr_ref[...] = dlr

        @pl.when(pl.program_id(1) != 0)
        def _():
            dlr_ref[...] = dlr_ref[...] + dlr

    gate_w = pl.BlockSpec((None, LANES, HEAD_K), lambda b, h: (h, 0, 0))
    gate_b = pl.BlockSpec((None, 1, HEAD_K), lambda b, h: (h, 0, 0))
    return pl.pallas_call(
        body, name="gla_bwd", grid=(n_seq, N_HEADS),
        in_specs=[pl.BlockSpec((lf, 512), lambda b, h: (b, N_CONV_TILES + h)),
                  pl.BlockSpec((lf, LANES), lambda b, h: (b, 0)),
                  pl.BlockSpec((lf, HEAD_V), lambda b, h: (b, h)),
                  gate_w, gate_w, gate_b, gate_b,
                  pl.BlockSpec((8, LANES), lambda b, h: (0, 0))],
        out_specs=[pl.BlockSpec((lf, 512), lambda b, h: (b, h)),
                   pl.BlockSpec((lf, LANES), lambda b, h: (b, 0)),
                   pl.BlockSpec((None, None, LANES, HEAD_K), lambda b, h: (b, h, 0, 0)),
                   pl.BlockSpec((None, None, LANES, HEAD_K), lambda b, h: (b, h, 0, 0)),
                   pl.BlockSpec((None, None, 2, HEAD_K), lambda b, h: (b, h, 0, 0))],
        out_shape=[jax.ShapeDtypeStruct((n_seq * lf, W_GLA), BF16),
                   jax.ShapeDtypeStruct((n_seq * lf, LANES), F32),
                   jax.ShapeDtypeStruct((n_seq, N_HEADS, LANES, HEAD_K), F32),
                   jax.ShapeDtypeStruct((n_seq, N_HEADS, LANES, HEAD_K), F32),
                   jax.ShapeDtypeStruct((n_seq, N_HEADS, 2, HEAD_K), F32)],
        scratch_shapes=[pltpu.VMEM((lf, HEAD_K), F32), pltpu.VMEM((2, lf, HEAD_K), F32),
                        pltpu.VMEM((2, lf, HEAD_K), F32), pltpu.VMEM((2, lf, HEAD_K), F32),
                        pltpu.VMEM((2, n_groups, HEAD_V, GROUP * HEAD_K), F32),
                        pltpu.VMEM((2, n_groups, HEAD_V, GROUP * HEAD_K), F32)],
        compiler_params=_params(("parallel", "arbitrary"), 56),
    )(proj, lr, d_o, wgf, wgb, bgf, bgb, token)


def _tail(h, tgt, yc, o, proj, w3, gamma, g_post, lf):
    t_rows = h.shape[0]
    tm = _pick_tile(t_rows, 256, CHUNK)
    n_chunks = lf // CHUNK
    per_tile = tm // CHUNK

    def body(h_ref, tgt_ref, yc_ref, o_ref, r_ref, ma_ref, mb_ref, w_hbm, gamma_ref, gpost_ref,
             dres_ref, yg_ref, merged_ref, dout_ref, dpc_ref, dpg_ref, dyc_ref, do_ref, dtail_ref,
             loss_ref, dgpost_ref, dgamma_ref, w_s, w_sem):
        i = pl.program_id(0)

        @pl.when(i == 0)
        def _():
            cp = pltpu.make_async_copy(w_hbm, w_s, w_sem)
            cp.start()
            cp.wait()
            loss_ref[...] = jnp.zeros_like(loss_ref)
            dgpost_ref[...] = jnp.zeros_like(dgpost_ref)
            dgamma_ref[...] = jnp.zeros_like(dgamma_ref)

        gamma = gamma_ref[...]
        o = o_ref[...]
        r = r_ref[...].astype(F32)
        sr = _sigmoid(r)
        silu_r = r * sr
        n_parts, rstd_parts = [], []
        for hd in range(N_HEADS):
            oh = o[:, hd * HEAD_V:(hd + 1) * HEAD_V]
            rstd = lax.rsqrt(jnp.mean(oh * oh, axis=-1, keepdims=True) + EPS)
            n_parts.append(oh * rstd)
            rstd_parts.append(rstd)
        n = jnp.concatenate(n_parts, axis=-1)
        gamma_t = jnp.concatenate([gamma] * N_HEADS, axis=-1)
        yg = n * gamma_t * silu_r
        yg_bf = yg.astype(BF16)
        yg_ref[...] = yg_bf
        yc = yc_ref[...]
        pc = _dot(yc, w_s[0])
        pg = _dot(yg_bf, w_s[1])
        sa = _sigmoid(ma_ref[...].astype(F32))
        sb = _sigmoid(mb_ref[...].astype(F32))
        merged = (sa * pc + sb * pg).astype(BF16)
        merged_ref[...] = merged
        out = _dot(merged, w_s[2])
        rstd2 = lax.rsqrt(jnp.mean(out * out, axis=-1, keepdims=True) + EPS)
        nn = out * rstd2
        gpost = gpost_ref[...]
        y = h_ref[...] + nn * gpost

        rowi = lax.broadcasted_iota(jnp.int32, (tm, 1), 0)
        keep = jnp.zeros((tm, 1), F32)
        for kk in range(per_tile):
            is_tok = ((i * per_tile + kk) % n_chunks) != 0
            f = jnp.where(is_tok, 1.0, 0.0)
            keep = jnp.where((rowi >= kk * CHUNK) & (rowi < (kk + 1) * CHUNK), f, keep)
        diff = (y - tgt_ref[...]) * keep
        loss_ref[...] += jnp.sum(diff * diff) * (0.5 / D)
        dy = diff * (1.0 / D)
        dres_ref[...] = dy
        dgpost_ref[...] += jnp.sum(dy * nn, axis=0, keepdims=True)
        dn = dy * gpost
        dout_f = rstd2 * (dn - nn * jnp.mean(dn * nn, axis=-1, keepdims=True))
        dout = dout_f.astype(BF16)
        dout_ref[...] = jnp.transpose(dout_f).astype(BF16)
        dmerged = _dot_nt(dout, w_s[2])
        dpc_f = dmerged * sa
        dpg_f = dmerged * sb
        dpc = dpc_f.astype(BF16)
        dpg = dpg_f.astype(BF16)
        dpc_ref[...] = jnp.transpose(dpc_f).astype(BF16)
        dpg_ref[...] = jnp.transpose(dpg_f).astype(BF16)
        dtail_ref[:, D:2 * D] = (dmerged * pc * (sa * (1.0 - sa))).astype(BF16)
        dtail_ref[:, 2 * D:3 * D] = (dmerged * pg * (sb * (1.0 - sb))).astype(BF16)
        dyc_ref[...] = _dot_nt(dpc, w_s[0]).astype(BF16)
        dyg = _dot_nt(dpg, w_s[1])
        dtail_ref[:, 0:D] = (dyg * n * gamma_t * (sr * (1.0 + r * (1.0 - sr)))).astype(BF16)
        dgam_full = jnp.sum(dyg * n * silu_r, axis=0, keepdims=True)
        dgam = dgam_full[:, 0:HEAD_V]
        for hd in range(1, N_HEADS):
            dgam = dgam + dgam_full[:, hd * HEAD_V:(hd + 1) * HEAD_V]
        dgamma_ref[...] += dgam
        dng = dyg * gamma_t * silu_r
        do_parts = []
        for hd in range(N_HEADS):
            sl = slice(hd * HEAD_V, (hd + 1) * HEAD_V)
            dnh = dng[:, sl]
            nh = n_parts[hd]
            do_parts.append(rstd_parts[hd] * (dnh - nh * jnp.mean(dnh * nh, axis=-1, keepdims=True)))
        do_ref[...] = jnp.concatenate(do_parts, axis=-1).astype(BF16)

    row = lambda c: pl.BlockSpec((tm, D), lambda i: (i, c))
    col = pl.BlockSpec((D, tm), lambda i: (0, i))
    const = lambda shape: pl.BlockSpec(shape, lambda i: (0, 0))
    act = jax.ShapeDtypeStruct((t_rows, D), BF16)
    act_t = jax.ShapeDtypeStruct((D, t_rows), BF16)
    return pl.pallas_call(
        body, name="tail", grid=(t_rows // tm,),
        in_specs=[row(0), row(0), row(0), row(0), row(6), row(7), row(8),
                  pl.BlockSpec(memory_space=pl.ANY), const((1, HEAD_V)), const((1, D))],
        out_specs=[row(0)] * 3 + [col] * 3 + [row(0)] * 2
                  + [pl.BlockSpec((tm, W_TAIL), lambda i: (i, 0)),
                     const((8, LANES)), const((1, D)), const((1, HEAD_V))],
        out_shape=[jax.ShapeDtypeStruct((t_rows, D), F32)] + [act] * 2 + [act_t] * 3 + [act] * 2
                  + [jax.ShapeDtypeStruct((t_rows, W_TAIL), BF16),
                     jax.ShapeDtypeStruct((8, LANES), F32),
                     jax.ShapeDtypeStruct((1, D), F32),
                     jax.ShapeDtypeStruct((1, HEAD_V), F32)],
        scratch_shapes=[pltpu.VMEM((3, D, D), BF16), pltpu.SemaphoreType.DMA],
        compiler_params=_params(("arbitrary",), 56),
    )(h, tgt, yc, o, proj, proj, proj, w3, gamma, g_post)


def _wgrad_t(a_t, b, name, out_dtype=BF16):
    m, t_rows = a_t.shape
    n = b.shape[1]
    tn = D if n % D == 0 else n
    tk = _pick_tile(t_rows, 768, LANES)
    n_k = t_rows // tk

    def body(a_ref, b_ref, o_ref, acc):
        k = pl.program_id(1)

        @pl.when(k == 0)
        def _():
            acc[...] = jnp.zeros_like(acc)

        acc[...] += _dot(a_ref[...], b_ref[...].astype(BF16))

        @pl.when(k == n_k - 1)
        def _():
            o_ref[...] = jnp.transpose(acc[...]).astype(out_dtype)

    return pl.pallas_call(
        body, name=name, grid=(n // tn, n_k),
        in_specs=[pl.BlockSpec((m, tk), lambda j, k: (0, k)),
                  pl.BlockSpec((tk, tn), lambda j, k: (k, j))],
        out_specs=pl.BlockSpec((tn, m), lambda j, k: (j, 0)),
        out_shape=jax.ShapeDtypeStruct((n, m), out_dtype),
        scratch_shapes=[pltpu.VMEM((m, tn), F32)],
        compiler_params=_params(("parallel", "arbitrary"), 48),
    )(a_t, b)


def _wgrad_t_side(a_t, b, side, name, out_dtype=BF16):
    m, t_rows = a_t.shape
    n = b.shape[1]
    ns = side.shape[1]
    tn = D if n % D == 0 else n
    tk = _pick_tile(t_rows, 768, LANES)
    n_k = t_rows // tk

    def body(a_ref, b_ref, s_ref, o_ref, os_ref, acc, acc_side):
        j, k = pl.program_id(0), pl.program_id(1)

        @pl.when(k == 0)
        def _():
            acc[...] = jnp.zeros_like(acc)

        @pl.when((k == 0) & (j == 0))
        def _():
            acc_side[...] = jnp.zeros_like(acc_side)

        a = a_ref[...]
        acc[...] += _dot(a, b_ref[...].astype(BF16))

        @pl.when(j == 0)
        def _():
            acc_side[...] += _dot(a, s_ref[...].astype(BF16))

        @pl.when(k == n_k - 1)
        def _():
            o_ref[...] = jnp.transpose(acc[...]).astype(out_dtype)

        @pl.when((k == n_k - 1) & (j == 0))
        def _():
            os_ref[...] = jnp.transpose(acc_side[...]).astype(out_dtype)

    return pl.pallas_call(
        body, name=name, grid=(n // tn, n_k),
        in_specs=[pl.BlockSpec((m, tk), lambda j, k: (0, k)),
                  pl.BlockSpec((tk, tn), lambda j, k: (k, j)),
                  pl.BlockSpec((tk, ns), lambda j, k: (k, 0))],
        out_specs=[pl.BlockSpec((tn, m), lambda j, k: (j, 0)),
                   pl.BlockSpec((ns, m), lambda j, k: (0, 0))],
        out_shape=[jax.ShapeDtypeStruct((n, m), out_dtype), jax.ShapeDtypeStruct((ns, m), out_dtype)],
        scratch_shapes=[pltpu.VMEM((m, tn), F32), pltpu.VMEM((m, ns), F32)],
        compiler_params=_params(("arbitrary", "arbitrary"), 48),
    )(a_t, b, side)


def _dgrad_in(dpc, dpg, dpt, dlr, w_full_t, h, g_pre, dres, token):
    t_rows = h.shape[0]
    tm = _pick_tile(t_rows, 384, 16)
    n_main = N_MAIN

    def body(dpc_ref, dpg_ref, dpt_ref, dlr_ref, w_hbm, h_ref, g_ref, dres_ref, token_ref,
             dh_ref, dg_ref, w_s, wlr_s, w_sems):
        @pl.when(pl.program_id(0) == 0)
        def _():
            _load_weights(w_hbm, w_s, wlr_s, w_sems)
            dg_ref[...] = jnp.zeros_like(dg_ref)

        du = _dot(dlr_ref[...].astype(BF16), wlr_s[...])
        du += _dot(dpc_ref[...], w_s[0:W_CONV, :])
        du += _dot(dpg_ref[...], w_s[W_CONV:W_CONV + W_GLA, :])
        du += _dot(dpt_ref[...], w_s[W_CONV + W_GLA:n_main, :])
        hh = h_ref[...]
        rstd = lax.rsqrt(jnp.mean(hh * hh, axis=-1, keepdims=True) + EPS)
        xhat = hh * rstd
        dg_ref[...] += jnp.sum(du * xhat, axis=0, keepdims=True)
        dx = du * g_ref[...]
        dh_ref[...] = rstd * (dx - xhat * jnp.mean(dx * xhat, axis=-1, keepdims=True)) + dres_ref[...]

    row = lambda width: pl.BlockSpec((tm, width), lambda i: (i, 0))
    return pl.pallas_call(
        body, name="dgrad_in", grid=(t_rows // tm,),
        in_specs=[row(W_CONV), row(W_GLA), row(W_TAIL), row(LANES),
                  pl.BlockSpec(memory_space=pl.ANY),
                  row(D), pl.BlockSpec((1, D), lambda i: (0, 0)), row(D),
                  pl.BlockSpec((8, LANES), lambda i: (0, 0))],
        out_specs=[row(D), pl.BlockSpec((1, D), lambda i: (0, 0))],
        out_shape=[jax.ShapeDtypeStruct((t_rows, D), F32), jax.ShapeDtypeStruct((1, D), F32)],
        scratch_shapes=[pltpu.VMEM((n_main, D), BF16), pltpu.VMEM((LANES, D), BF16),
                        pltpu.SemaphoreType.DMA((N_WEIGHT_COPIES,))],
        compiler_params=_params(("arbitrary",), 56),
    )(dpc, dpg, dpt, dlr, w_full_t, h, g_pre, dres, token)


def _reference_rows(g_conv, g_gla, g_tail, g_lr):
    conv = g_conv.reshape(N_CONV_TILES, 4, 128, D).transpose(1, 0, 2, 3).reshape(W_CONV, D)
    gla = g_gla.reshape(N_HEADS, 512, D)
    q = gla[:, 0:128].reshape(N_HEADS * HEAD_K, D)
    k = gla[:, 128:256].reshape(N_HEADS * HEAD_K, D)
    v = gla[:, 256:512].reshape(N_HEADS * HEAD_V, D)
    return jnp.concatenate([conv, q, k, v, g_tail[0:D], g_lr[0:2 * RANK], g_tail[D:3 * D]], axis=0)


def kernel(x, meta_tokens, norm_pre, w_in, conv_w, w_gate_fwd, b_gate_fwd, w_gate_bwd, b_gate_bwd, gla_norm, w_out_conv, w_out_gla, w_merge_out, norm_post, loss_target, m_meta_tokens, m_norm_pre, m_w_in, m_conv_w, m_w_gate_fwd, m_b_gate_fwd, m_w_gate_bwd, m_b_gate_bwd, m_gla_norm, m_w_out_conv, m_w_out_gla, m_w_merge_out, m_norm_post, v_meta_tokens, v_norm_pre, v_w_in, v_conv_w, v_w_gate_fwd, v_b_gate_fwd, v_w_gate_bwd, v_b_gate_bwd, v_gla_norm, v_w_out_conv, v_w_out_gla, v_w_merge_out, v_norm_post):
    n_seq, seq, _ = x.shape
    lf = CHUNK + seq
    t_rows = n_seq * lf
    shard = 2 * lax.axis_index("x") + lax.axis_index("y")
    shard_arr = jnp.reshape(shard, (1,)).astype(jnp.int32)

    w_in_slots = _cast_into_slot(jnp.transpose(w_in[0]), shard_arr, "cast_w_in")
    w_out_slots = _cast_into_slot(jnp.concatenate([w_out_conv[0], w_out_gla[0], w_merge_out[0]], axis=0), shard_arr,
                                  "cast_w_out")
    w_in_all, meta_all, conv_all, wgf_all, wgb_all = _gather_via_sibling(
        "gather_w_in", [w_in_slots, meta_tokens, conv_w[0], w_gate_fwd[0], w_gate_bwd[0]],
        (True, False, False, False, False))
    w_out_state, _ = _plane_start("gather_w_out_start", [w_out_slots], "gather", wgb_all)

    w_full_t = w_in_all.reshape(N_IN, D)
    meta_full = jnp.transpose(meta_all, (1, 0, 2)).reshape(N_META, D)
    conv_full = jnp.transpose(conv_all, (1, 0, 2)).reshape(3, D)
    wgf = jnp.pad(wgf_all, ((0, 0), (0, LANES - RANK), (0, 0))).astype(BF16)
    wgb = jnp.pad(wgb_all, ((0, 0), (RANK, LANES - 2 * RANK), (0, 0))).astype(BF16)
    bgf = b_gate_fwd.reshape(N_HEADS, 1, HEAD_K)
    bgb = b_gate_bwd.reshape(N_HEADS, 1, HEAD_K)

    head = jnp.concatenate([jnp.zeros((PAD_FRONT, D), F32), meta_full], axis=0)
    h = jnp.concatenate([jnp.broadcast_to(head[None], (n_seq, CHUNK, D)), x], axis=1).reshape(t_rows, D)
    tgt = jnp.pad(loss_target, ((0, 0), (CHUNK, 0), (0, 0))).reshape(t_rows, D)

    proj, u_t, lr = _in_proj(h, norm_pre, w_full_t)
    yc = _conv_fwd(proj, conv_full, n_seq, lf)
    o = _gla_fwd(proj, lr, wgf, wgb, bgf, bgb, n_seq, lf)
    (w_out_all,) = _plane_wait("gather_w_out_wait", w_out_state, "gather", o)
    w3 = jnp.transpose(w_out_all.reshape(4, 3, D // 4, D), (1, 0, 2, 3)).reshape(3, D, D)
    (dres, yg, merged, dout_t, dpc_t, dpg_t, dyc, d_o, dtail, loss_acc, d_gpost, d_gamma) = _tail(
        h, tgt, yc, o, proj, w3, gla_norm, norm_post, lf)
    g_w_oc = _wgrad_t(dpc_t, yc, "wgrad_out_conv")
    g_w_og = _wgrad_t(dpg_t, yg, "wgrad_out_gla")
    g_w_mo = _wgrad_t(dout_t, merged, "wgrad_merge_out")
    g_out_slots = jnp.concatenate([g.reshape(4, D // 4, D) for g in (g_w_oc, g_w_og, g_w_mo)], axis=1)
    out_state, out_token = _plane_start("scatter_out_grads_start", [g_out_slots], "scatter", g_w_mo)
    dgla, dlr, dwgf_p, dwgb_p, dbg_p = _gla_bwd(proj, lr, d_o, wgf, wgb, bgf, bgb, n_seq, lf, out_token)
    (got_out,) = _plane_wait("scatter_out_grads_wait", out_state, "scatter", dlr)
    dconv, dconvw_p = _conv_bwd(proj, conv_full, dyc, n_seq, lf)
    g_conv = _wgrad_t(u_t, dconv, "wgrad_in_conv")
    g_gla, g_lr = _wgrad_t_side(u_t, dgla, dlr, "wgrad_in_gla")
    g_tail = _wgrad_t(u_t, dtail, "wgrad_in_tail")

    g_in_slots = _reference_rows(g_conv, g_gla, g_tail, g_lr).reshape(4, SHARD_IN, D)
    in_state, in_token = _plane_start("scatter_in_grads_start", [g_in_slots], "scatter", g_lr)
    dh, d_gpre = _dgrad_in(dconv, dgla, dtail, dlr, w_full_t, h, norm_pre, dres, in_token)
    (got_in,) = _plane_wait("scatter_in_grads_wait", in_state, "scatter", d_gpre)

    plane_in = _sum_slots(got_in, "sum_w_in_grads", own=g_in_slots, slot=shard_arr)
    plane_out = _sum_slots(got_out, "sum_w_out_grads", own=g_out_slots, slot=shard_arr)
    swap_state, swap_token = _plane_start("swap_plane_sums_start", [plane_in, plane_out], "swap", plane_out)

    dh3 = dh.reshape(n_seq, lf, D)
    grad_x = dh3[:, CHUNK:, :]

    d_meta = jnp.sum(dh3[:, PAD_FRONT:CHUNK, :], axis=0)
    d_convw = jnp.sum(dconvw_p, axis=0)
    d_wgf = jnp.transpose(jnp.sum(dwgf_p, axis=0)[:, 0:RANK, :], (1, 0, 2)).reshape(RANK, N_HEADS * HEAD_K)
    d_wgb = jnp.transpose(jnp.sum(dwgb_p, axis=0)[:, RANK:2 * RANK, :], (1, 0, 2)).reshape(RANK, N_HEADS * HEAD_K)
    d_bg = jnp.sum(dbg_p, axis=0)
    d_bgf = d_bg[:, 0, :].reshape(1, N_HEADS * HEAD_K)
    d_bgb = d_bg[:, 1, :].reshape(1, N_HEADS * HEAD_K)
    loss_part = loss_acc[0:1, :] + swap_token[0:1, :]
    partials = [d_meta, d_convw, d_wgf, d_wgb, d_gpre, d_bgf, d_bgb, d_gamma, d_gpost, loss_part]
    (g_meta, g_convw, g_wgf, g_wgb, g_npre, g_bgf, g_bgb, g_gnorm, g_npost, loss_row) = _sum_small(
        _gather_all("gather_small_grads", partials), "sum_small_grads")
    loss = loss_row[0, 0]
    small_out = _adamw_small(
        [(meta_tokens, g_meta, m_meta_tokens, v_meta_tokens), (norm_pre, g_npre, m_norm_pre, v_norm_pre),
         (conv_w, g_convw, m_conv_w, v_conv_w), (w_gate_fwd, g_wgf, m_w_gate_fwd, v_w_gate_fwd),
         (b_gate_fwd, g_bgf, m_b_gate_fwd, v_b_gate_fwd), (w_gate_bwd, g_wgb, m_w_gate_bwd, v_w_gate_bwd),
         (b_gate_bwd, g_bgb, m_b_gate_bwd, v_b_gate_bwd), (gla_norm, g_gnorm, m_gla_norm, v_gla_norm),
         (norm_post, g_npost, m_norm_post, v_norm_post)], shard_arr, "adamw_small")

    other_in, other_out = _plane_wait("swap_plane_sums_wait", swap_state, "swap", small_out[0][0])
    big_in = _adamw(jnp.transpose(w_in[0]), [plane_in, other_in], jnp.transpose(m_w_in[0]), jnp.transpose(v_w_in[0]),
                    "adamw_w_in")
    out_params = ((w_out_conv, m_w_out_conv, v_w_out_conv), (w_out_gla, m_w_out_gla, v_w_out_gla),
                  (w_merge_out, m_w_merge_out, v_w_merge_out))
    big_out = [_adamw(w[0], [plane_out, other_out], m[0], v[0], f"adamw_w_out_{i}", grad_row=i * (D // 4))
               for i, (w, m, v) in enumerate(out_params)]

    results = []
    for kind in range(4):
        small_kind = [p[kind] for p in small_out]
        w_in_part = jnp.transpose(big_in[kind])[None]
        outs3 = [big_out[i][kind][None] for i in range(3)]
        results.extend(small_kind[0:2] + [w_in_part] + small_kind[2:8] + outs3 + small_kind[8:9])
    return (loss, grad_x, *results)
```

```python
import functools

import jax
import jax.numpy as jnp
from jax import lax
from jax.experimental import pallas as pl
from jax.experimental.pallas import tpu as pltpu

F32 = jnp.float32
BF16 = jnp.bfloat16
MESH = pl.DeviceIdType.MESH

D = 1024
N_META = 16
CHUNK = 64
PAD_FRONT = CHUNK - N_META
N_HEADS = 4
HEAD_K = 128
HEAD_V = 256
RANK = 16
EPS = 1e-6
GATE_NORM = 16.0
N_IN = 9248
SHARD_IN = N_IN // 4
LANES = 128
N_CONV_TILES = 8
W_CONV = 4096
W_GLA = 2048
W_TAIL = 3072
N_MAIN = W_CONV + W_GLA + W_TAIL
OFF_Q, OFF_K, OFF_V, OFF_R = 4096, 4608, 5120, 6144
OFF_LR, OFF_MA = 7168, 7200
MIB = 1024 * 1024

ADAM_LR = 0.001
ADAM_B1 = 0.9
ADAM_B2 = 0.999
ADAM_EPS = 1e-08
ADAM_WD = 0.01
ADAM_STEP = 10


def _params(sem=None, vmem_mib=None):
    return pltpu.CompilerParams(
        dimension_semantics=sem,
        vmem_limit_bytes=None if vmem_mib is None else vmem_mib * MIB)


def _pick_tile(n, target, mult):
    best = None
    for t in range(mult, min(n, target) + 1, mult):
        if n % t == 0:
            best = t
    return n if best is None else best


def _sigmoid(v):
    return 1.0 / (1.0 + jnp.exp(-v))


def _log_sigmoid(v):
    return jnp.minimum(v, 0.0) - jnp.log(1.0 + jnp.exp(-jnp.abs(v)))


def _dot(a, b):
    return jnp.dot(a, b, preferred_element_type=F32)


def _dot_nt(a, b):
    return lax.dot_general(a, b, (((1,), (1,)), ((), ())), preferred_element_type=F32)


def _dot_tn(a, b):
    return lax.dot_general(a, b, (((0,), (0,)), ((), ())), preferred_element_type=F32)


def _gather_all(name, arrs):
    n = len(arrs)
    flips = tuple((m >> 2 & 1, m >> 1 & 1, m & 1) for m in range(1, 8))

    def body(*refs):
        ins, outs = refs[:n], refs[n:2 * n]
        send_sems, recv_sems, local_sems = refs[2 * n:]
        pos = (lax.axis_index("x"), lax.axis_index("y"), lax.axis_index("c"))

        def slot_of(p):
            return 4 * p[0] + 2 * p[1] + p[2]

        peers = [tuple(1 - pos[a] if f[a] else pos[a] for a in range(3)) for f in flips]
        me = slot_of(pos)
        copies = []
        for i in range(n):
            cp = pltpu.make_async_copy(ins[i], outs[i].at[me], local_sems.at[i])
            cp.start()
            copies.append(cp)
        sends = []
        for i in range(n):
            for k, peer in enumerate(peers):
                cp = pltpu.make_async_remote_copy(
                    src_ref=ins[i], dst_ref=outs[i].at[me], send_sem=send_sems.at[i, k], recv_sem=recv_sems.at[i, k],
                    device_id=peer, device_id_type=MESH)
                cp.start()
                sends.append(cp)
        for i in range(n):
            for k, peer in enumerate(peers):
                pltpu.make_async_remote_copy(
                    src_ref=ins[i], dst_ref=outs[i].at[slot_of(peer)], send_sem=send_sems.at[i, k],
                    recv_sem=recv_sems.at[i, k], device_id=peer, device_id_type=MESH).wait_recv()
        for cp in sends:
            cp.wait_send()
        for cp in copies:
            cp.wait()

    hbm = pl.BlockSpec(memory_space=pl.ANY)
    outs = pl.pallas_call(
        body, name=name, out_shape=[jax.ShapeDtypeStruct((8,) + a.shape, a.dtype) for a in arrs],
        in_specs=[hbm] * n, out_specs=[hbm] * n,
        scratch_shapes=[pltpu.SemaphoreType.DMA((n, 7)), pltpu.SemaphoreType.DMA((n, 7)),
                        pltpu.SemaphoreType.DMA((n,))],
        compiler_params=pltpu.CompilerParams(has_side_effects=True),
    )(*arrs)
    return list(outs)


def _gather_via_sibling(name, arrs, slotted):
    n = len(arrs)
    out_shape = [jax.ShapeDtypeStruct(a.shape if slotted[i] else (4,) + a.shape, a.dtype)
                 for i, a in enumerate(arrs)]

    def body(*refs):
        ins, outs = refs[:n], refs[n:2 * n]
        send_sems, recv_sems, local_sems = refs[2 * n:]
        x, y, c = lax.axis_index("x"), lax.axis_index("y"), lax.axis_index("c")
        me = 2 * x + y
        chips = [(1 - x, y), (x, 1 - y), (1 - x, 1 - y)]

        def half(ref, which):
            rows = ref.shape[0]
            cut = rows // 2 // 16 * 16
            return ref.at[pl.ds(0, cut)] if which == 0 else ref.at[pl.ds(cut, rows - cut)]

        def copy(src, dst, i, k, to):
            return pltpu.make_async_remote_copy(
                src_ref=src, dst_ref=dst, send_sem=send_sems.at[i, k], recv_sem=recv_sems.at[i, k],
                device_id=to, device_id_type=MESH)

        def run(mine):
            other = 1 - mine
            local, sends = [], []
            whole = [(not slotted[i]) and arrs[i].shape[0] < 32 for i in range(n)]
            for i in range(n):
                own = outs[i].at[me] if slotted[i] else ins[i]
                if not slotted[i]:
                    cp = pltpu.make_async_copy(ins[i], outs[i].at[me], local_sems.at[i])
                    cp.start()
                    local.append(cp)
                for k, (px, py) in enumerate(chips):
                    if whole[i]:
                        cp = copy(own, outs[i].at[me], i, k, (px, py, mine))
                    elif k < 2:
                        cp = copy(half(own, mine), half(outs[i].at[me], mine), i, k, (px, py, mine))
                    else:
                        continue
                    cp.start()
                    sends.append(cp)
            via = mine
            for k in (via, 1 - via, 2):
                px, py = chips[k]
                slot = 2 * px + py
                source = (px, py, mine) if k < 2 else chips[1 - via] + (mine,)
                for i in range(n):
                    if whole[i]:
                        copy(outs[i].at[slot], outs[i].at[slot], i, k, (px, py, mine)).wait_recv()
                        continue
                    landed = half(outs[i].at[slot], mine)
                    copy(landed, landed, i, k, source).wait_recv()
                    if k == via:
                        cp = copy(landed, landed, i, 2, chips[1 - via] + (mine,))
                        cp.start()
                        sends.append(cp)
                    cp = copy(landed, landed, i, 3 + k, (x, y, other))
                    cp.start()
                    sends.append(cp)
            for k, (px, py) in enumerate(chips):
                slot = 2 * px + py
                for i in range(n):
                    if whole[i]:
                        continue
                    passed = half(outs[i].at[slot], other)
                    copy(passed, passed, i, 3 + k, (x, y, other)).wait_recv()
            for cp in sends:
                cp.wait_send()
            for cp in local:
                cp.wait()

        for mine in (0, 1):
            pl.when(c == mine)(functools.partial(run, mine))

    hbm = pl.BlockSpec(memory_space=pl.ANY)
    outs = pl.pallas_call(
        body, name=name, out_shape=out_shape,
        in_specs=[hbm] * n, out_specs=[hbm] * n,
        scratch_shapes=[pltpu.SemaphoreType.DMA((n, 6)), pltpu.SemaphoreType.DMA((n, 6)),
                        pltpu.SemaphoreType.DMA((n,))],
        input_output_aliases={i: i for i in range(n) if slotted[i]},
        compiler_params=pltpu.CompilerParams(has_side_effects=True),
    )(*arrs)
    return list(outs)


HBM_SPEC = pl.BlockSpec(memory_space=pltpu.HBM)
SEM_SPEC = pl.BlockSpec(memory_space=pltpu.SEMAPHORE)
DATAFLOW = pltpu.SideEffectType.DATAFLOW_SIDE_EFFECTING


def _split_peers(mode):
    x, y, c = lax.axis_index("x"), lax.axis_index("y"), lax.axis_index("c")
    if mode == "swap":
        return 0, [((x, y, 1 - c), 0)]
    return 2 * x + y, [((1 - x, y, c), 2 * (1 - x) + y), ((x, 1 - y, c), 2 * x + 1 - y),
                       ((1 - x, 1 - y, c), 2 * (1 - x) + 1 - y)]


def _split_refs(mode, src, landing, me, peer_slot):
    if mode == "gather":
        return src.at[me], landing.at[me]
    if mode == "scatter":
        return src.at[peer_slot], landing.at[me]
    return src, landing


def _plane_start(name, arrs, mode, after):
    n = len(arrs)
    n_peers = 1 if mode == "swap" else 3
    if mode == "gather":
        srcs, lands = [], list(arrs)
    else:
        srcs, lands = list(arrs), [lax.empty(a.shape, a.dtype) for a in arrs]
    n_src = len(srcs)

    def body(*refs):
        landing = refs[n_src:n_src + n]
        sources = refs[:n_src] if n_src else landing
        send_sems, recv_sems = refs[n_src + n + 1], refs[n_src + n + 2]
        token = refs[-1]
        me, peers = _split_peers(mode)
        for i in range(n):
            for k, (peer, peer_slot) in enumerate(peers):
                src, dst = _split_refs(mode, sources[i], landing[i], me, peer_slot)
                pltpu.make_async_remote_copy(
                    src_ref=src, dst_ref=dst, send_sem=send_sems.at[n_peers * i + k],
                    recv_sem=recv_sems.at[n_peers * i + k], device_id=peer, device_id_type=MESH).start()
        token[...] = jnp.zeros_like(token)

    hbm_in = [pltpu.with_memory_space_constraint(a, pltpu.HBM) for a in srcs + lands]
    out = pl.pallas_call(
        body, name=name,
        out_shape=[pltpu.SemaphoreType.DMA((n_peers * n,)), pltpu.SemaphoreType.DMA((n_peers * n,))]
                  + [pltpu.HBM(a.shape, a.dtype) for a in lands]
                  + [jax.ShapeDtypeStruct((8, LANES), F32)],
        in_specs=[HBM_SPEC] * (n_src + n) + [pl.BlockSpec(memory_space=pl.ANY)],
        out_specs=[SEM_SPEC, SEM_SPEC] + [HBM_SPEC] * n + [pl.BlockSpec(memory_space=pltpu.VMEM)],
        input_output_aliases={n_src + i: 2 + i for i in range(n)},
        compiler_params=pltpu.CompilerParams(has_side_effects=DATAFLOW),
    )(*hbm_in, after)
    return out[:-1], out[-1]


def _plane_wait(name, state, mode, after):
    send_sems, recv_sems = state[0], state[1]
    bufs = list(state[2:])
    n = len(bufs)
    n_peers = 1 if mode == "swap" else 3

    def body(*refs):
        landing = refs[:n]
        send_sems, recv_sems = refs[n], refs[n + 1]
        _, peers = _split_peers(mode)
        for i in range(n):
            for k, (peer, peer_slot) in enumerate(peers):
                arrived = landing[i] if mode == "swap" else landing[i].at[peer_slot]
                cp = pltpu.make_async_remote_copy(
                    src_ref=arrived, dst_ref=arrived, send_sem=send_sems.at[n_peers * i + k],
                    recv_sem=recv_sems.at[n_peers * i + k], device_id=peer, device_id_type=MESH)
                cp.wait_send()
                cp.wait_recv()

    out = pl.pallas_call(
        body, name=name,
        out_shape=[pltpu.HBM(a.shape, a.dtype) for a in bufs],
        in_specs=[HBM_SPEC] * n + [SEM_SPEC, SEM_SPEC, pl.BlockSpec(memory_space=pl.ANY)],
        out_specs=[HBM_SPEC] * n,
        input_output_aliases={i: i for i in range(n)},
        compiler_params=pltpu.CompilerParams(has_side_effects=DATAFLOW),
    )(*bufs, send_sems, recv_sems, after)
    return list(out)


def _tile_2d(rows, cols, row_mult, max_elems=512 * 1024):
    if rows % row_mult == 0:
        rt = _pick_tile(rows, max(row_mult, max_elems // cols), row_mult)
        return (rt, cols), rows // rt, lambda i: (i, 0)
    ct = _pick_tile(cols, max(LANES, max_elems // rows), LANES)
    return (rows, ct), cols // ct, lambda i: (0, i)


def _cast_into_slot(a, slot, name):
    block, steps, index = _tile_2d(a.shape[0], a.shape[1], 16)

    def body(slot_ref, a_ref, o_ref):
        o_ref[...] = a_ref[...].astype(BF16)

    return pl.pallas_call(
        body, name=name,
        grid_spec=pltpu.PrefetchScalarGridSpec(
            num_scalar_prefetch=1, grid=(steps,),
            in_specs=[pl.BlockSpec(block, lambda i, s: index(i))],
            out_specs=pl.BlockSpec((None,) + block, lambda i, s: (s[0],) + index(i))),
        out_shape=jax.ShapeDtypeStruct((4,) + a.shape, BF16),
        compiler_params=_params(("arbitrary",)),
    )(slot, a)


def _sum_slots(buf, name, own=None, slot=None):
    n_slots, rows, cols = buf.shape
    (br, bc), steps, index = _tile_2d(rows, cols, 16, 320 * 1024)

    def body(*refs):
        if own is None:
            b_ref, o_ref = refs
        else:
            slot_ref, b_ref, own_ref, o_ref = refs
        acc = None
        for s in range(n_slots):
            term = b_ref[s] if own is None else jnp.where(slot_ref[0] == s, own_ref[...], b_ref[s])
            acc = term.astype(F32) if acc is None else acc + term.astype(F32)
        o_ref[...] = acc

    out_shape = jax.ShapeDtypeStruct((rows, cols), F32)
    if own is None:
        return pl.pallas_call(
            body, name=name, grid=(steps,),
            in_specs=[pl.BlockSpec((n_slots, br, bc), lambda i: (0,) + index(i))],
            out_specs=pl.BlockSpec((br, bc), index), out_shape=out_shape,
            compiler_params=_params(("parallel",), 48),
        )(buf)
    return pl.pallas_call(
        body, name=name,
        grid_spec=pltpu.PrefetchScalarGridSpec(
            num_scalar_prefetch=1, grid=(steps,),
            in_specs=[pl.BlockSpec((n_slots, br, bc), lambda i, s: (0,) + index(i)),
                      pl.BlockSpec((None, br, bc), lambda i, s: (s[0],) + index(i))],
            out_specs=pl.BlockSpec((br, bc), lambda i, s: index(i))),
        out_shape=out_shape,
        compiler_params=_params(("arbitrary",), 48),
    )(slot, buf, own)


def _sum_small(bufs, name):
    n = len(bufs)

    def body(*refs):
        for b_ref, o_ref in zip(refs[:n], refs[n:]):
            acc = b_ref[0]
            for s in range(1, b_ref.shape[0]):
                acc = acc + b_ref[s]
            o_ref[...] = acc

    vmem = pl.BlockSpec(memory_space=pltpu.VMEM)
    return pl.pallas_call(
        body, name=name, in_specs=[vmem] * n, out_specs=[vmem] * n,
        out_shape=[jax.ShapeDtypeStruct(b.shape[1:], b.dtype) for b in bufs],
    )(*bufs)


def _adam_update(w, g, m, v):
    c1 = 1.0 - ADAM_B1 ** ADAM_STEP
    c2 = 1.0 - ADAM_B2 ** ADAM_STEP
    m_new = ADAM_B1 * m + (1.0 - ADAM_B1) * g
    v_new = ADAM_B2 * v + (1.0 - ADAM_B2) * (g * g)
    m_hat = m_new / c1
    v_hat = v_new / c2
    return -ADAM_LR * (m_hat / (jnp.sqrt(v_hat) + ADAM_EPS) + ADAM_WD * w), m_new, v_new


def _adamw_small(params, slot, name):
    n = len(params)

    def spec_of(shape):
        lead = (None,) * (len(shape) - 2)
        return pl.BlockSpec(lead + tuple(shape[-2:]), lambda i, s, k=len(shape): (0,) * k)

    in_specs, operands, out_specs, out_shape = [], [], [], []
    for w, g, m, v in params:
        shard = g.shape[-1] != w.shape[-1]
        g_spec = pl.BlockSpec(tuple(w.shape[-2:]), (lambda i, s: (0, s[0])) if shard else (lambda i, s: (0, 0)))
        in_specs += [spec_of(w.shape), g_spec, spec_of(m.shape), spec_of(v.shape)]
        operands += [w, g, m, v]
        out_specs += [spec_of(w.shape)] * 4
        out_shape += [jax.ShapeDtypeStruct(w.shape, F32)] * 4

    def body(slot_ref, *refs):
        ins, outs = refs[:4 * n], refs[4 * n:]
        for p in range(n):
            w_ref, g_ref, m_ref, v_ref = ins[4 * p:4 * p + 4]
            g = g_ref[...]
            delta, m_new, v_new = _adam_update(w_ref[...], g, m_ref[...], v_ref[...])
            for o_ref, val in zip(outs[4 * p:4 * p + 4], (g, delta, m_new, v_new)):
                o_ref[...] = val

    out = pl.pallas_call(
        body, name=name,
        grid_spec=pltpu.PrefetchScalarGridSpec(num_scalar_prefetch=1, grid=(1,), in_specs=in_specs, out_specs=out_specs),
        out_shape=out_shape,
    )(slot, *operands)
    return [tuple(out[4 * p:4 * p + 4]) for p in range(n)]


def _adamw(w, grads, m, v, name, grad_row=0):
    rows, cols = w.shape
    (rt, _), _, _ = _tile_2d(rows, cols, 8, 160 * 1024)
    assert grad_row % rt == 0
    n_g = len(grads)

    def body(*refs):
        w_ref = refs[0]
        g_refs = refs[1:1 + n_g]
        m_ref, v_ref, g_out, d_out, m_out, v_out = refs[1 + n_g:]
        g = g_refs[0][...]
        for r in g_refs[1:]:
            g = g + r[...]
        g_out[...] = g
        d_out[...], m_out[...], v_out[...] = _adam_update(w_ref[...], g, m_ref[...], v_ref[...])

    spec = pl.BlockSpec((rt, cols), lambda i: (i, 0))
    grad_spec = pl.BlockSpec((rt, cols), lambda i: (i + grad_row // rt, 0))
    shape = jax.ShapeDtypeStruct((rows, cols), F32)
    return pl.pallas_call(
        body, name=name, grid=(rows // rt,),
        in_specs=[spec] + [grad_spec] * n_g + [spec] * 2, out_specs=[spec] * 4, out_shape=[shape] * 4,
        compiler_params=_params(("parallel",), 48),
    )(w, *grads, m, v)


def _weight_pieces():
    pieces = []
    for j in range(N_CONV_TILES):
        for g in range(4):
            pieces.append((512 * j + 128 * g, D * g + 128 * j, 128))
    for hd in range(N_HEADS):
        base = W_CONV + 512 * hd
        pieces.append((base, OFF_Q + HEAD_K * hd, HEAD_K))
        pieces.append((base + HEAD_K, OFF_K + HEAD_K * hd, HEAD_K))
        pieces.append((base + 2 * HEAD_K, OFF_V + HEAD_V * hd, HEAD_V))
    pieces.append((W_CONV + W_GLA, OFF_R, D))
    pieces.append((W_CONV + W_GLA + D, OFF_MA, 2 * D))
    return pieces


N_WEIGHT_COPIES = len(_weight_pieces()) + 1


def _load_weights(w_hbm, w_s, wlr_s, sems):
    copies = [pltpu.make_async_copy(w_hbm.at[pl.ds(src, n)], w_s.at[pl.ds(dst, n)], sems.at[i])
              for i, (dst, src, n) in enumerate(_weight_pieces())]
    copies.append(pltpu.make_async_copy(w_hbm.at[pl.ds(OFF_LR, LANES)], wlr_s, sems.at[N_WEIGHT_COPIES - 1]))
    for cp in copies:
        cp.start()
    for cp in copies:
        cp.wait()


def _in_proj(h, g_pre, w_full_t):
    t_rows = h.shape[0]
    tm = _pick_tile(t_rows, 384, LANES)
    n_main = N_MAIN

    def body(h_ref, g_ref, w_hbm, proj_ref, ut_ref, lr_ref, w_s, wlr_s, w_sems):
        @pl.when(pl.program_id(0) == 0)
        def _():
            _load_weights(w_hbm, w_s, wlr_s, w_sems)

        hh = h_ref[...]
        rstd = lax.rsqrt(jnp.mean(hh * hh, axis=-1, keepdims=True) + EPS)
        uf = hh * rstd * g_ref[...]
        u = uf.astype(BF16)
        ut_ref[...] = jnp.transpose(uf).astype(BF16)
        lr_ref[...] = _dot_nt(u, wlr_s[...])
        for j in range(n_main // D):
            cols = slice(j * D, (j + 1) * D)
            proj_ref[:, cols] = _dot_nt(u, w_s[cols, :]).astype(BF16)

    return pl.pallas_call(
        body, name="in_proj", grid=(t_rows // tm,),
        in_specs=[pl.BlockSpec((tm, D), lambda i: (i, 0)),
                  pl.BlockSpec((1, D), lambda i: (0, 0)),
                  pl.BlockSpec(memory_space=pl.ANY)],
        out_specs=[pl.BlockSpec((tm, n_main), lambda i: (i, 0)),
                   pl.BlockSpec((D, tm), lambda i: (0, i)),
                   pl.BlockSpec((tm, LANES), lambda i: (i, 0))],
        out_shape=[jax.ShapeDtypeStruct((t_rows, n_main), BF16),
                   jax.ShapeDtypeStruct((D, t_rows), BF16),
                   jax.ShapeDtypeStruct((t_rows, LANES), F32)],
        scratch_shapes=[pltpu.VMEM((n_main, D), BF16), pltpu.VMEM((LANES, D), BF16),
                        pltpu.SemaphoreType.DMA((N_WEIGHT_COPIES,))],
        compiler_params=_params(("arbitrary",), 56),
    )(h, g_pre, w_full_t)


CONV_TILES_PER_STEP = 4


def _conv_parts(p_ref, w_ref, t):
    cb = p_ref[:, 512 * t:512 * t + 128].astype(F32)
    cc = p_ref[:, 512 * t + 128:512 * t + 256].astype(F32)
    cx = p_ref[:, 512 * t + 256:512 * t + 384].astype(F32)
    cz = p_ref[:, 512 * t + 384:512 * t + 512].astype(F32)
    rows = cb.shape[0]
    w = w_ref[:, 128 * t:128 * (t + 1)]
    p = cc * cx
    conv = pltpu.roll(p, 1, 0) * w[0:1] + p * w[1:2] + pltpu.roll(p, rows - 1, 0) * w[2:3]
    sz = _sigmoid(cz)
    return cb, cc, cx, cz, p, conv, sz, w


def _conv_fwd(proj, conv_w, n_seq, lf):
    per = CONV_TILES_PER_STEP

    def body(p_ref, w_ref, y_ref):
        for t in range(per):
            cb, _, _, cz, _, conv, sz, _ = _conv_parts(p_ref, w_ref, t)
            y_ref[:, 128 * t:128 * (t + 1)] = (cb * conv * (cz * sz)).astype(BF16)

    return pl.pallas_call(
        body, name="conv_fwd", grid=(n_seq, N_CONV_TILES // per),
        in_specs=[pl.BlockSpec((lf, 512 * per), lambda b, j: (b, j)),
                  pl.BlockSpec((3, 128 * per), lambda b, j: (0, j))],
        out_specs=pl.BlockSpec((lf, 128 * per), lambda b, j: (b, j)),
        out_shape=jax.ShapeDtypeStruct((n_seq * lf, D), BF16),
        compiler_params=_params(("parallel", "parallel"), 48),
    )(proj, conv_w)


def _conv_bwd(proj, conv_w, dyc, n_seq, lf):
    per = CONV_TILES_PER_STEP

    def body(p_ref, w_ref, dy_ref, dp_ref, dw_ref):
        for t in range(per):
            cb, cc, cx, cz, p, conv, sz, w = _conv_parts(p_ref, w_ref, t)
            rows = cb.shape[0]
            dy = dy_ref[:, 128 * t:128 * (t + 1)].astype(F32)
            silu = cz * sz
            dcb = dy * conv * silu
            dconv = dy * cb * silu
            dcz = dy * cb * conv * (sz * (1.0 + cz * (1.0 - sz)))
            d_next = pltpu.roll(dconv, rows - 1, 0)
            d_prev = pltpu.roll(dconv, 1, 0)
            dp = d_next * w[0:1] + dconv * w[1:2] + d_prev * w[2:3]
            base = 512 * t
            dp_ref[:, base:base + 128] = dcb.astype(BF16)
            dp_ref[:, base + 128:base + 256] = (dp * cx).astype(BF16)
            dp_ref[:, base + 256:base + 384] = (dp * cc).astype(BF16)
            dp_ref[:, base + 384:base + 512] = dcz.astype(BF16)
            lanes = slice(128 * t, 128 * (t + 1))
            dw_ref[0:1, lanes] = jnp.sum(dconv * pltpu.roll(p, 1, 0), axis=0, keepdims=True)
            dw_ref[1:2, lanes] = jnp.sum(dconv * p, axis=0, keepdims=True)
            dw_ref[2:3, lanes] = jnp.sum(dconv * pltpu.roll(p, rows - 1, 0), axis=0, keepdims=True)

    return pl.pallas_call(
        body, name="conv_bwd", grid=(n_seq, N_CONV_TILES // per),
        in_specs=[pl.BlockSpec((lf, 512 * per), lambda b, j: (b, j)),
                  pl.BlockSpec((3, 128 * per), lambda b, j: (0, j)),
                  pl.BlockSpec((lf, 128 * per), lambda b, j: (b, j))],
        out_specs=[pl.BlockSpec((lf, 512 * per), lambda b, j: (b, j)),
                   pl.BlockSpec((None, 3, 128 * per), lambda b, j: (b, 0, j))],
        out_shape=[jax.ShapeDtypeStruct((n_seq * lf, W_CONV), BF16),
                   jax.ShapeDtypeStruct((n_seq, 3, D), F32)],
        compiler_params=_params(("parallel", "parallel"), 48),
    )(proj, conv_w, dyc)


GROUP = 3
GROUP_ROWS = GROUP * CHUNK


def _row_group(shape):
    row = lax.broadcasted_iota(jnp.int32, shape, 0)
    grp = jnp.zeros(shape, jnp.int32)
    for r in range(1, GROUP):
        grp = grp + (row >= r * CHUNK).astype(jnp.int32)
    return grp


def _lane_group(shape, width):
    lane = lax.broadcasted_iota(jnp.int32, shape, 1)
    grp = jnp.zeros(shape, jnp.int32)
    for r in range(1, GROUP):
        grp = grp + (lane >= r * width).astype(jnp.int32)
    return grp


def _score_mask(direction):
    shape = (GROUP_ROWS, GROUP_ROWS)
    row = lax.broadcasted_iota(jnp.int32, shape, 0)
    col = lax.broadcasted_iota(jnp.int32, shape, 1)
    same = _row_group(shape) == _lane_group(shape, CHUNK)
    return same & ((col <= row) if direction == 0 else (col > row))


def _diag_blocks(v):
    w = v.shape[1]
    wide = jnp.concatenate([v] * GROUP, axis=1)
    return jnp.where(_row_group(wide.shape) == _lane_group(wide.shape, w), wide, jnp.zeros_like(wide))


def _per_chunk_dot(lhs, state, transposed):
    outs = []
    for r in range(GROUP):
        rows = lhs[r * CHUNK:(r + 1) * CHUNK, :]
        blk = state[:, r * HEAD_K:(r + 1) * HEAD_K]
        outs.append(_dot_nt(rows, blk) if transposed else _dot(rows, blk))
    return jnp.concatenate(outs, axis=0)


def _chunk_cumsum(v, suffix):
    pos = lax.broadcasted_iota(jnp.int32, v.shape, 0) & (CHUNK - 1)
    shift = 1
    while shift < CHUNK:
        if suffix:
            moved = pltpu.roll(v, GROUP_ROWS - shift, 0)
            v = v + jnp.where(pos < CHUNK - shift, moved, 0.0)
        else:
            moved = pltpu.roll(v, shift, 0)
            v = v + jnp.where(pos >= shift, moved, 0.0)
        shift *= 2
    return v


def _per_chunk_rows(rows_of_chunk):
    w = rows_of_chunk[0].shape[1]
    return jnp.concatenate([jnp.broadcast_to(v, (CHUNK, w)) for v in rows_of_chunk], axis=0)


def _chunk_end_rows(direction, b):
    at = CHUNK - 1 if direction == 0 else 0
    return [b[r * CHUNK + at:r * CHUNK + at + 1, :] for r in range(GROUP)]


def _gla_gates(lr_bf, wg_ref, bg_ref, lf):
    z = _dot(lr_bf, wg_ref[...]) + bg_ref[...]
    valid = lax.broadcasted_iota(jnp.int32, (lf, HEAD_K), 0) >= PAD_FRONT
    return z, valid


def _group_unroll(n_groups):
    return n_groups if n_groups <= 11 else 1


def _group_rows(g):
    return pl.ds(pl.multiple_of(g * GROUP_ROWS, GROUP_ROWS), GROUP_ROWS)


def _chunk_decay(direction, g, r, b_s):
    base = g * GROUP_ROWS + r * CHUNK
    if direction == 0:
        grp = b_s[pl.ds(pl.multiple_of(base + CHUNK - 8, 8), 8), :]
        return jnp.exp(grp[7:8, :])
    grp = b_s[pl.ds(pl.multiple_of(base, 8), 8), :]
    return jnp.exp(grp[0:1, :])


def _state_scan(direction, n_groups, b_s, st_s, reverse):
    ascending = (direction == 0) != reverse

    def step(i, carry):
        g = i if ascending else n_groups - 1 - i
        for rr in range(GROUP):
            r = rr if ascending else GROUP - 1 - rr
            lanes = slice(r * HEAD_K, (r + 1) * HEAD_K)
            decay = _chunk_decay(direction, g, r, b_s)
            local = st_s[g, :, lanes]
            st_s[g, :, lanes] = carry
            carry = (local + carry * decay) if reverse else (carry * decay + local)
        return carry

    lax.fori_loop(0, n_groups, step, jnp.zeros((HEAD_V, HEAD_K), F32), unroll=_group_unroll(n_groups))


def _gla_states(direction, n_groups, qkv_ref, g_s, b_s, st_s):
    def local(g, carry):
        rows = _group_rows(g)
        b = _chunk_cumsum(g_s[rows, :], direction == 1)
        b_s[rows, :] = b
        b_end = _per_chunk_rows(_chunk_end_rows(direction, b))
        k = qkv_ref[rows, 128:256].astype(F32)
        v = qkv_ref[rows, 256:512]
        k_dec = (k * jnp.exp(b_end - b)).astype(BF16)
        st_s[g] = _dot_tn(v, _diag_blocks(k_dec))
        return carry

    lax.fori_loop(0, n_groups, local, 0, unroll=_group_unroll(n_groups))
    _state_scan(direction, n_groups, b_s, st_s, False)


def _gla_fwd(proj, lr, wgf, wgb, bgf, bgb, n_seq, lf):
    assert lf % GROUP_ROWS == 0
    n_groups = lf // GROUP_ROWS
    scale = HEAD_K ** -0.5

    def body(qkv_ref, lr_ref, wgf_ref, wgb_ref, bgf_ref, bgb_ref, o_ref, g_s, b_s2, st_s2):
        lr_bf = lr_ref[...].astype(BF16)
        for direction in (0, 1):
            wg_ref, bg_ref = ((wgf_ref, bgf_ref), (wgb_ref, bgb_ref))[direction]
            z, valid = _gla_gates(lr_bf, wg_ref, bg_ref, lf)
            g_s[...] = jnp.where(valid, _log_sigmoid(z) / GATE_NORM, 0.0)
            _gla_states(direction, n_groups, qkv_ref, g_s, b_s2.at[direction], st_s2.at[direction])
        masks = [_score_mask(0), _score_mask(1)]

        def out(g, carry):
            rows = _group_rows(g)
            q = qkv_ref[rows, 0:128].astype(F32) * scale
            k = qkv_ref[rows, 128:256].astype(F32)
            v = qkv_ref[rows, 256:512]
            o = None
            for direction in (0, 1):
                b = b_s2[direction, rows, :]
                q_in = (q * jnp.exp(b)).astype(BF16)
                k_in = (k * jnp.exp(-b)).astype(BF16)
                s = jnp.where(masks[direction], _dot_nt(q_in, k_in), 0.0).astype(BF16)
                part = _dot(s, v) + _per_chunk_dot(q_in, st_s2[direction, g].astype(BF16), True)
                o = part if o is None else o + part
            o_ref[rows, :] = o
            return carry

        lax.fori_loop(0, n_groups, out, 0, unroll=_group_unroll(n_groups))

    return pl.pallas_call(
        body, name="gla_fwd", grid=(n_seq, N_HEADS),
        in_specs=[pl.BlockSpec((lf, 512), lambda b, h: (b, N_CONV_TILES + h)),
                  pl.BlockSpec((lf, LANES), lambda b, h: (b, 0)),
                  pl.BlockSpec((None, LANES, HEAD_K), lambda b, h: (h, 0, 0)),
                  pl.BlockSpec((None, LANES, HEAD_K), lambda b, h: (h, 0, 0)),
                  pl.BlockSpec((None, 1, HEAD_K), lambda b, h: (h, 0, 0)),
                  pl.BlockSpec((None, 1, HEAD_K), lambda b, h: (h, 0, 0))],
        out_specs=pl.BlockSpec((lf, HEAD_V), lambda b, h: (b, h)),
        out_shape=jax.ShapeDtypeStruct((n_seq * lf, D), F32),
        scratch_shapes=[pltpu.VMEM((lf, HEAD_K), F32), pltpu.VMEM((2, lf, HEAD_K), F32),
                        pltpu.VMEM((2, n_groups, HEAD_V, GROUP * HEAD_K), F32)],
        compiler_params=_params(("parallel", "parallel"), 48),
    )(proj, lr, wgf, wgb, bgf, bgb)


def _gla_bwd(proj, lr, d_o, wgf, wgb, bgf, bgb, n_seq, lf, token):
    assert lf % GROUP_ROWS == 0
    n_groups = lf // GROUP_ROWS
    scale = HEAD_K ** -0.5

    def body(qkv_ref, lr_ref, do_ref, wgf_ref, wgb_ref, bgf_ref, bgb_ref, token_ref,
             dqkv_ref, dlr_ref, dwgf_ref, dwgb_ref, dbg_ref,
             g_s, b_s2, fac_s2, dg_s2, st_s2, dst_s2):
        lr_bf = lr_ref[...].astype(BF16)
        gates = ((wgf_ref, bgf_ref), (wgb_ref, bgb_ref))
        for direction in (0, 1):
            wg_ref, bg_ref = gates[direction]
            b_s, st_s, dst_s = b_s2.at[direction], st_s2.at[direction], dst_s2.at[direction]
            z, valid = _gla_gates(lr_bf, wg_ref, bg_ref, lf)
            g_s[...] = jnp.where(valid, _log_sigmoid(z) / GATE_NORM, 0.0)
            fac_s2[direction] = jnp.where(valid, _sigmoid(-z) / GATE_NORM, 0.0)
            _gla_states(direction, n_groups, qkv_ref, g_s, b_s, st_s)

            def state_grad_local(g, carry):
                rows = _group_rows(g)
                q = qkv_ref[rows, 0:128].astype(F32) * scale
                q_in = (q * jnp.exp(b_s[rows, :])).astype(BF16)
                dst_s[g] = _dot_tn(do_ref[rows, :], _diag_blocks(q_in))
                return carry

            lax.fori_loop(0, n_groups, state_grad_local, 0, unroll=_group_unroll(n_groups))
            _state_scan(direction, n_groups, b_s, dst_s, True)

        masks = [_score_mask(0), _score_mask(1)]

        def group_grads(g, carry):
            rows = _group_rows(g)
            q = qkv_ref[rows, 0:128].astype(F32) * scale
            k = qkv_ref[rows, 128:256].astype(F32)
            v = qkv_ref[rows, 256:512]
            d_out = do_ref[rows, :]
            dq_sum = dk_sum = dv_sum = None
            for direction in (0, 1):
                end_row = CHUNK - 1 if direction == 0 else 0
                b = b_s2[direction, rows, :]
                ends = _chunk_end_rows(direction, b)
                b_end = _per_chunk_rows(ends)
                e_pos = jnp.exp(b)
                e_neg = jnp.exp(-b)
                e_end = jnp.exp(b_end - b)
                q_in = q * e_pos
                k_in = k * e_neg
                k_dec = k * e_end
                q_in_bf = q_in.astype(BF16)
                k_in_bf = k_in.astype(BF16)
                state = st_s2[direction, g]
                d_state = dst_s2[direction, g]
                state_bf = state.astype(BF16)
                d_state_bf = d_state.astype(BF16)
                s = jnp.where(masks[direction], _dot_nt(q_in_bf, k_in_bf), 0.0).astype(BF16)
                ds = jnp.where(masks[direction], _dot_nt(d_out, v), 0.0).astype(BF16)
                dv = _dot_tn(s, d_out) + _per_chunk_dot(k_dec.astype(BF16), d_state_bf, True)
                dq_in = _dot(ds, k_in_bf) + _per_chunk_dot(d_out, state_bf, False)
                dk_in = _dot_tn(ds, q_in_bf)
                dk_dec = _per_chunk_dot(v, d_state_bf, False)
                dq = dq_in * e_pos * scale
                dk = dk_in * e_neg + dk_dec * e_end
                dq_sum = dq if dq_sum is None else dq_sum + dq
                dk_sum = dk if dk_sum is None else dk_sum + dk
                dv_sum = dv if dv_sum is None else dv_sum + dv
                dkk = dk_dec * k_dec
                db = dq_in * q_in - dk_in * k_in - dkk
                d_decay = jnp.sum(d_state * state, axis=0, keepdims=True)
                db_end = [jnp.sum(dkk[r * CHUNK:(r + 1) * CHUNK, :], axis=0, keepdims=True)
                          + d_decay[:, r * HEAD_K:(r + 1) * HEAD_K] * jnp.exp(ends[r]) for r in range(GROUP)]
                row = lax.broadcasted_iota(jnp.int32, (GROUP_ROWS, HEAD_K), 0)
                at_end = row == end_row
                for r in range(1, GROUP):
                    at_end = at_end | (row == r * CHUNK + end_row)
                db = db + jnp.where(at_end, _per_chunk_rows(db_end), 0.0)
                dg_s2[direction, rows, :] = _chunk_cumsum(db, direction == 0)
            dqkv_ref[rows, 0:128] = dq_sum.astype(BF16)
            dqkv_ref[rows, 128:256] = dk_sum.astype(BF16)
            dqkv_ref[rows, 256:512] = dv_sum.astype(BF16)
            return carry

        lax.fori_loop(0, n_groups, group_grads, 0, unroll=_group_unroll(n_groups))

        dlr = jnp.zeros((lf, LANES), F32)
        for direction in (0, 1):
            dz = dg_s2[direction] * fac_s2[direction]
            dz_bf = dz.astype(BF16)
            dbg_ref[direction:direction + 1, :] = jnp.sum(dz, axis=0, keepdims=True)
            (dwgf_ref, dwgb_ref)[direction][...] = _dot_tn(lr_bf, dz_bf)
            dlr = dlr + _dot_nt(dz_bf, gates[direction][0][...])

        @pl.when(pl.program_id(1) == 0)
        def _():
            dlr_ref[...] = dlr

        @pl.when(pl.program_id(1) != 0)
        def _():
            dlr_ref[...] = dlr_ref[...] + dlr

    gate_w = pl.BlockSpec((None, LANES, HEAD_K), lambda b, h: (h, 0, 0))
    gate_b = pl.BlockSpec((None, 1, HEAD_K), lambda b, h: (h, 0, 0))
    return pl.pallas_call(
        body, name="gla_bwd", grid=(n_seq, N_HEADS),
        in_specs=[pl.BlockSpec((lf, 512), lambda b, h: (b, N_CONV_TILES + h)),
                  pl.BlockSpec((lf, LANES), lambda b, h: (b, 0)),
                  pl.BlockSpec((lf, HEAD_V), lambda b, h: (b, h)),
                  gate_w, gate_w, gate_b, gate_b,
                  pl.BlockSpec((8, LANES), lambda b, h: (0, 0))],
        out_specs=[pl.BlockSpec((lf, 512), lambda b, h: (b, h)),
                   pl.BlockSpec((lf, LANES), lambda b, h: (b, 0)),
                   pl.BlockSpec((None, None, LANES, HEAD_K), lambda b, h: (b, h, 0, 0)),
                   pl.BlockSpec((None, None, LANES, HEAD_K), lambda b, h: (b, h, 0, 0)),
                   pl.BlockSpec((None, None, 2, HEAD_K), lambda b, h: (b, h, 0, 0))],
        out_shape=[jax.ShapeDtypeStruct((n_seq * lf, W_GLA), BF16),
                   jax.ShapeDtypeStruct((n_seq * lf, LANES), F32),
                   jax.ShapeDtypeStruct((n_seq, N_HEADS, LANES, HEAD_K), F32),
                   jax.ShapeDtypeStruct((n_seq, N_HEADS, LANES, HEAD_K), F32),
                   jax.ShapeDtypeStruct((n_seq, N_HEADS, 2, HEAD_K), F32)],
        scratch_shapes=[pltpu.VMEM((lf, HEAD_K), F32), pltpu.VMEM((2, lf, HEAD_K), F32),
                        pltpu.VMEM((2, lf, HEAD_K), F32), pltpu.VMEM((2, lf, HEAD_K), F32),
                        pltpu.VMEM((2, n_groups, HEAD_V, GROUP * HEAD_K), F32),
                        pltpu.VMEM((2, n_groups, HEAD_V, GROUP * HEAD_K), F32)],
        compiler_params=_params(("parallel", "arbitrary"), 56),
    )(proj, lr, d_o, wgf, wgb, bgf, bgb, token)


def _tail(h, tgt, yc, o, proj, w3, gamma, g_post, lf):
    t_rows = h.shape[0]
    tm = _pick_tile(t_rows, 256, CHUNK)
    n_chunks = lf // CHUNK
    per_tile = tm // CHUNK

    def body(h_ref, *refs):
        tgt_refs = refs[:per_tile]
        (yc_ref, o_ref, r_ref, ma_ref, mb_ref, w_hbm, gamma_ref, gpost_ref,
         dres_ref, yg_ref, merged_ref, dout_ref, dpc_ref, dpg_ref, dyc_ref, do_ref, dtail_ref,
         loss_ref, dgpost_ref, dgamma_ref, w_s, w_sem) = refs[per_tile:]
        i = pl.program_id(0)

        @pl.when(i == 0)
        def _():
            cp = pltpu.make_async_copy(w_hbm, w_s, w_sem)
            cp.start()
            cp.wait()
            loss_ref[...] = jnp.zeros_like(loss_ref)
            dgpost_ref[...] = jnp.zeros_like(dgpost_ref)
            dgamma_ref[...] = jnp.zeros_like(dgamma_ref)

        gamma = gamma_ref[...]
        o = o_ref[...]
        r = r_ref[...].astype(F32)
        sr = _sigmoid(r)
        silu_r = r * sr
        n_parts, rstd_parts = [], []
        for hd in range(N_HEADS):
            oh = o[:, hd * HEAD_V:(hd + 1) * HEAD_V]
            rstd = lax.rsqrt(jnp.mean(oh * oh, axis=-1, keepdims=True) + EPS)
            n_parts.append(oh * rstd)
            rstd_parts.append(rstd)
        n = jnp.concatenate(n_parts, axis=-1)
        gamma_t = jnp.concatenate([gamma] * N_HEADS, axis=-1)
        yg = n * gamma_t * silu_r
        yg_bf = yg.astype(BF16)
        yg_ref[...] = yg_bf
        yc = yc_ref[...]
        pc = _dot(yc, w_s[0])
        pg = _dot(yg_bf, w_s[1])
        sa = _sigmoid(ma_ref[...].astype(F32))
        sb = _sigmoid(mb_ref[...].astype(F32))
        merged = (sa * pc + sb * pg).astype(BF16)
        merged_ref[...] = merged
        out = _dot(merged, w_s[2])
        rstd2 = lax.rsqrt(jnp.mean(out * out, axis=-1, keepdims=True) + EPS)
        nn = out * rstd2
        gpost = gpost_ref[...]
        y = h_ref[...] + nn * gpost

        rowi = lax.broadcasted_iota(jnp.int32, (tm, 1), 0)
        keep = jnp.zeros((tm, 1), F32)
        for kk in range(per_tile):
            is_tok = ((i * per_tile + kk) % n_chunks) != 0
            f = jnp.where(is_tok, 1.0, 0.0)
            keep = jnp.where((rowi >= kk * CHUNK) & (rowi < (kk + 1) * CHUNK), f, keep)
        tgt = jnp.concatenate([t_ref[...] for t_ref in tgt_refs], axis=0)
        diff = jnp.where(keep > 0.0, y - tgt, 0.0)
        loss_ref[...] += jnp.sum(diff * diff) * (0.5 / D)
        dy = diff * (1.0 / D)
        dres_ref[...] = dy
        dgpost_ref[...] += jnp.sum(dy * nn, axis=0, keepdims=True)
        dn = dy * gpost
        dout_f = rstd2 * (dn - nn * jnp.mean(dn * nn, axis=-1, keepdims=True))
        dout = dout_f.astype(BF16)
        dout_ref[...] = jnp.transpose(dout_f).astype(BF16)
        dmerged = _dot_nt(dout, w_s[2])
        dpc_f = dmerged * sa
        dpg_f = dmerged * sb
        dpc = dpc_f.astype(BF16)
        dpg = dpg_f.astype(BF16)
        dpc_ref[...] = jnp.transpose(dpc_f).astype(BF16)
        dpg_ref[...] = jnp.transpose(dpg_f).astype(BF16)
        dtail_ref[:, D:2 * D] = (dmerged * pc * (sa * (1.0 - sa))).astype(BF16)
        dtail_ref[:, 2 * D:3 * D] = (dmerged * pg * (sb * (1.0 - sb))).astype(BF16)
        dyc_ref[...] = _dot_nt(dpc, w_s[0]).astype(BF16)
        dyg = _dot_nt(dpg, w_s[1])
        dtail_ref[:, 0:D] = (dyg * n * gamma_t * (sr * (1.0 + r * (1.0 - sr)))).astype(BF16)
        dgam_full = jnp.sum(dyg * n * silu_r, axis=0, keepdims=True)
        dgam = dgam_full[:, 0:HEAD_V]
        for hd in range(1, N_HEADS):
            dgam = dgam + dgam_full[:, hd * HEAD_V:(hd + 1) * HEAD_V]
        dgamma_ref[...] += dgam
        dng = dyg * gamma_t * silu_r
        do_parts = []
        for hd in range(N_HEADS):
            sl = slice(hd * HEAD_V, (hd + 1) * HEAD_V)
            dnh = dng[:, sl]
            nh = n_parts[hd]
            do_parts.append(rstd_parts[hd] * (dnh - nh * jnp.mean(dnh * nh, axis=-1, keepdims=True)))
        do_ref[...] = jnp.concatenate(do_parts, axis=-1).astype(BF16)

    row = lambda c: pl.BlockSpec((tm, D), lambda i: (i, c))
    col = pl.BlockSpec((D, tm), lambda i: (0, i))

    def tgt_chunk(kk):
        def index(i):
            q = i * per_tile + kk
            return (q // n_chunks) * (n_chunks - 1) + jnp.maximum(q % n_chunks - 1, 0), 0
        return pl.BlockSpec((CHUNK, D), index)

    const = lambda shape: pl.BlockSpec(shape, lambda i: (0, 0))
    act = jax.ShapeDtypeStruct((t_rows, D), BF16)
    act_t = jax.ShapeDtypeStruct((D, t_rows), BF16)
    return pl.pallas_call(
        body, name="tail", grid=(t_rows // tm,),
        in_specs=[row(0)] + [tgt_chunk(kk) for kk in range(per_tile)] + [row(0), row(0), row(6), row(7), row(8),
                  pl.BlockSpec(memory_space=pl.ANY), const((1, HEAD_V)), const((1, D))],
        out_specs=[row(0)] * 3 + [col] * 3 + [row(0)] * 2
                  + [pl.BlockSpec((tm, W_TAIL), lambda i: (i, 0)),
                     const((8, LANES)), const((1, D)), const((1, HEAD_V))],
        out_shape=[jax.ShapeDtypeStruct((t_rows, D), F32)] + [act] * 2 + [act_t] * 3 + [act] * 2
                  + [jax.ShapeDtypeStruct((t_rows, W_TAIL), BF16),
                     jax.ShapeDtypeStruct((8, LANES), F32),
                     jax.ShapeDtypeStruct((1, D), F32),
                     jax.ShapeDtypeStruct((1, HEAD_V), F32)],
        scratch_shapes=[pltpu.VMEM((3, D, D), BF16), pltpu.SemaphoreType.DMA],
        compiler_params=_params(("arbitrary",), 56),
    )(h, *[tgt] * per_tile, yc, o, proj, proj, proj, w3, gamma, g_post)


def _wgrad_t(a_t, b, name, out_dtype=BF16):
    m, t_rows = a_t.shape
    n = b.shape[1]
    tn = D if n % D == 0 else n
    tk = _pick_tile(t_rows, 768, LANES)
    n_k = t_rows // tk

    def body(a_ref, b_ref, o_ref, acc):
        k = pl.program_id(1)

        @pl.when(k == 0)
        def _():
            acc[...] = jnp.zeros_like(acc)

        acc[...] += _dot(a_ref[...], b_ref[...].astype(BF16))

        @pl.when(k == n_k - 1)
        def _():
            o_ref[...] = jnp.transpose(acc[...]).astype(out_dtype)

    return pl.pallas_call(
        body, name=name, grid=(n // tn, n_k),
        in_specs=[pl.BlockSpec((m, tk), lambda j, k: (0, k)),
                  pl.BlockSpec((tk, tn), lambda j, k: (k, j))],
        out_specs=pl.BlockSpec((tn, m), lambda j, k: (j, 0)),
        out_shape=jax.ShapeDtypeStruct((n, m), out_dtype),
        scratch_shapes=[pltpu.VMEM((m, tn), F32)],
        compiler_params=_params(("parallel", "arbitrary"), 48),
    )(a_t, b)


def _dgrad_in(dpc, dpg, dpt, dlr, w_full_t, h, g_pre, dres, token):
    t_rows = h.shape[0]
    tm = _pick_tile(t_rows, 384, 16)
    n_main = N_MAIN

    def body(dpc_ref, dpg_ref, dpt_ref, dlr_ref, w_hbm, h_ref, g_ref, dres_ref, token_ref,
             dh_ref, dg_ref, w_s, wlr_s, w_sems):
        @pl.when(pl.program_id(0) == 0)
        def _():
            _load_weights(w_hbm, w_s, wlr_s, w_sems)
            dg_ref[...] = jnp.zeros_like(dg_ref)

        du = _dot(dlr_ref[...].astype(BF16), wlr_s[...])
        du += _dot(dpc_ref[...], w_s[0:W_CONV, :])
        du += _dot(dpg_ref[...], w_s[W_CONV:W_CONV + W_GLA, :])
        du += _dot(dpt_ref[...], w_s[W_CONV + W_GLA:n_main, :])
        hh = h_ref[...]
        rstd = lax.rsqrt(jnp.mean(hh * hh, axis=-1, keepdims=True) + EPS)
        xhat = hh * rstd
        dg_ref[...] += jnp.sum(du * xhat, axis=0, keepdims=True)
        dx = du * g_ref[...]
        dh_ref[...] = rstd * (dx - xhat * jnp.mean(dx * xhat, axis=-1, keepdims=True)) + dres_ref[...]

    row = lambda width: pl.BlockSpec((tm, width), lambda i: (i, 0))
    return pl.pallas_call(
        body, name="dgrad_in", grid=(t_rows // tm,),
        in_specs=[row(W_CONV), row(W_GLA), row(W_TAIL), row(LANES),
                  pl.BlockSpec(memory_space=pl.ANY),
                  row(D), pl.BlockSpec((1, D), lambda i: (0, 0)), row(D),
                  pl.BlockSpec((8, LANES), lambda i: (0, 0))],
        out_specs=[row(D), pl.BlockSpec((1, D), lambda i: (0, 0))],
        out_shape=[jax.ShapeDtypeStruct((t_rows, D), F32), jax.ShapeDtypeStruct((1, D), F32)],
        scratch_shapes=[pltpu.VMEM((n_main, D), BF16), pltpu.VMEM((LANES, D), BF16),
                        pltpu.SemaphoreType.DMA((N_WEIGHT_COPIES,))],
        compiler_params=_params(("arbitrary",), 56),
    )(dpc, dpg, dpt, dlr, w_full_t, h, g_pre, dres, token)


def _reference_rows(g_conv, g_gla, g_tail, g_lr):
    conv = g_conv.reshape(N_CONV_TILES, 4, 128, D).transpose(1, 0, 2, 3).reshape(W_CONV, D)
    gla = g_gla.reshape(N_HEADS, 512, D)
    q = gla[:, 0:128].reshape(N_HEADS * HEAD_K, D)
    k = gla[:, 128:256].reshape(N_HEADS * HEAD_K, D)
    v = gla[:, 256:512].reshape(N_HEADS * HEAD_V, D)
    return jnp.concatenate([conv, q, k, v, g_tail[0:D], g_lr[0:2 * RANK], g_tail[D:3 * D]], axis=0)


def kernel(x, meta_tokens, norm_pre, w_in, conv_w, w_gate_fwd, b_gate_fwd, w_gate_bwd, b_gate_bwd, gla_norm, w_out_conv, w_out_gla, w_merge_out, norm_post, loss_target, m_meta_tokens, m_norm_pre, m_w_in, m_conv_w, m_w_gate_fwd, m_b_gate_fwd, m_w_gate_bwd, m_b_gate_bwd, m_gla_norm, m_w_out_conv, m_w_out_gla, m_w_merge_out, m_norm_post, v_meta_tokens, v_norm_pre, v_w_in, v_conv_w, v_w_gate_fwd, v_b_gate_fwd, v_w_gate_bwd, v_b_gate_bwd, v_gla_norm, v_w_out_conv, v_w_out_gla, v_w_merge_out, v_norm_post):
    n_seq, seq, _ = x.shape
    lf = CHUNK + seq
    t_rows = n_seq * lf
    shard = 2 * lax.axis_index("x") + lax.axis_index("y")
    shard_arr = jnp.reshape(shard, (1,)).astype(jnp.int32)

    w_in_slots = _cast_into_slot(jnp.transpose(w_in[0]), shard_arr, "cast_w_in")
    w_out_slots = _cast_into_slot(jnp.concatenate([w_out_conv[0], w_out_gla[0], w_merge_out[0]], axis=0), shard_arr,
                                  "cast_w_out")
    w_in_all, meta_all, conv_all, wgf_all, wgb_all = _gather_via_sibling(
        "gather_w_in", [w_in_slots, meta_tokens, conv_w[0], w_gate_fwd[0], w_gate_bwd[0]],
        (True, False, False, False, False))
    w_out_state, _ = _plane_start("gather_w_out_start", [w_out_slots], "gather", wgb_all)

    w_full_t = w_in_all.reshape(N_IN, D)
    meta_full = jnp.transpose(meta_all, (1, 0, 2)).reshape(N_META, D)
    conv_full = jnp.transpose(conv_all, (1, 0, 2)).reshape(3, D)
    wgf = jnp.pad(wgf_all, ((0, 0), (0, LANES - RANK), (0, 0))).astype(BF16)
    wgb = jnp.pad(wgb_all, ((0, 0), (RANK, LANES - 2 * RANK), (0, 0))).astype(BF16)
    bgf = b_gate_fwd.reshape(N_HEADS, 1, HEAD_K)
    bgb = b_gate_bwd.reshape(N_HEADS, 1, HEAD_K)

    head = jnp.concatenate([jnp.zeros((PAD_FRONT, D), F32), meta_full], axis=0)
    h = jnp.concatenate([jnp.broadcast_to(head[None], (n_seq, CHUNK, D)), x], axis=1).reshape(t_rows, D)
    tgt = loss_target.reshape(n_seq * seq, D)

    proj, u_t, lr = _in_proj(h, norm_pre, w_full_t)
    yc = _conv_fwd(proj, conv_full, n_seq, lf)
    o = _gla_fwd(proj, lr, wgf, wgb, bgf, bgb, n_seq, lf)
    (w_out_all,) = _plane_wait("gather_w_out_wait", w_out_state, "gather", o)
    w3 = jnp.transpose(w_out_all.reshape(4, 3, D // 4, D), (1, 0, 2, 3)).reshape(3, D, D)
    (dres, yg, merged, dout_t, dpc_t, dpg_t, dyc, d_o, dtail, loss_acc, d_gpost, d_gamma) = _tail(
        h, tgt, yc, o, proj, w3, gla_norm, norm_post, lf)
    g_w_oc = _wgrad_t(dpc_t, yc, "wgrad_out_conv")
    g_w_og = _wgrad_t(dpg_t, yg, "wgrad_out_gla")
    g_w_mo = _wgrad_t(dout_t, merged, "wgrad_merge_out")
    g_out_slots = jnp.concatenate([g.reshape(4, D // 4, D) for g in (g_w_oc, g_w_og, g_w_mo)], axis=1)
    out_state, out_token = _plane_start("scatter_out_grads_start", [g_out_slots], "scatter", g_w_mo)
    dgla, dlr, dwgf_p, dwgb_p, dbg_p = _gla_bwd(proj, lr, d_o, wgf, wgb, bgf, bgb, n_seq, lf, out_token)
    (got_out,) = _plane_wait("scatter_out_grads_wait", out_state, "scatter", dlr)
    dconv, dconvw_p = _conv_bwd(proj, conv_full, dyc, n_seq, lf)
    g_conv = _wgrad_t(u_t, dconv, "wgrad_in_conv")
    g_gla = _wgrad_t(u_t, dgla, "wgrad_in_gla")
    g_tail = _wgrad_t(u_t, dtail, "wgrad_in_tail")
    g_lr = _wgrad_t(u_t, dlr, "wgrad_in_lr")

    g_in_slots = _reference_rows(g_conv, g_gla, g_tail, g_lr).reshape(4, SHARD_IN, D)
    in_state, in_token = _plane_start("scatter_in_grads_start", [g_in_slots], "scatter", g_lr)
    dh, d_gpre = _dgrad_in(dconv, dgla, dtail, dlr, w_full_t, h, norm_pre, dres, in_token)
    (got_in,) = _plane_wait("scatter_in_grads_wait", in_state, "scatter", d_gpre)

    plane_in = _sum_slots(got_in, "sum_w_in_grads", own=g_in_slots, slot=shard_arr)
    plane_out = _sum_slots(got_out, "sum_w_out_grads", own=g_out_slots, slot=shard_arr)
    swap_state, swap_token = _plane_start("swap_plane_sums_start", [plane_in, plane_out], "swap", plane_out)

    dh3 = dh.reshape(n_seq, lf, D)
    grad_x = dh3[:, CHUNK:, :]

    d_meta = jnp.sum(dh3[:, PAD_FRONT:CHUNK, :], axis=0)
    d_convw = jnp.sum(dconvw_p, axis=0)
    d_wgf = jnp.transpose(jnp.sum(dwgf_p, axis=0)[:, 0:RANK, :], (1, 0, 2)).reshape(RANK, N_HEADS * HEAD_K)
    d_wgb = jnp.transpose(jnp.sum(dwgb_p, axis=0)[:, RANK:2 * RANK, :], (1, 0, 2)).reshape(RANK, N_HEADS * HEAD_K)
    d_bg = jnp.sum(dbg_p, axis=0)
    d_bgf = d_bg[:, 0, :].reshape(1, N_HEADS * HEAD_K)
    d_bgb = d_bg[:, 1, :].reshape(1, N_HEADS * HEAD_K)
    loss_part = loss_acc[0:1, :] + swap_token[0:1, :]
    partials = [d_meta, d_convw, d_wgf, d_wgb, d_gpre, d_bgf, d_bgb, d_gamma, d_gpost, loss_part]
    (g_meta, g_convw, g_wgf, g_wgb, g_npre, g_bgf, g_bgb, g_gnorm, g_npost, loss_row) = _sum_small(
        _gather_all("gather_small_grads", partials), "sum_small_grads")
    loss = loss_row[0, 0]
    small_out = _adamw_small(
        [(meta_tokens, g_meta, m_meta_tokens, v_meta_tokens), (norm_pre, g_npre, m_norm_pre, v_norm_pre),
         (conv_w, g_convw, m_conv_w, v_conv_w), (w_gate_fwd, g_wgf, m_w_gate_fwd, v_w_gate_fwd),
         (b_gate_fwd, g_bgf, m_b_gate_fwd, v_b_gate_fwd), (w_gate_bwd, g_wgb, m_w_gate_bwd, v_w_gate_bwd),
         (b_gate_bwd, g_bgb, m_b_gate_bwd, v_b_gate_bwd), (gla_norm, g_gnorm, m_gla_norm, v_gla_norm),
         (norm_post, g_npost, m_norm_post, v_norm_post)], shard_arr, "adamw_small")

    other_in, other_out = _plane_wait("swap_plane_sums_wait", swap_state, "swap", small_out[0][0])
    big_in = _adamw(jnp.transpose(w_in[0]), [plane_in, other_in], jnp.transpose(m_w_in[0]), jnp.transpose(v_w_in[0]),
                    "adamw_w_in")
    out_params = ((w_out_conv, m_w_out_conv, v_w_out_conv), (w_out_gla, m_w_out_gla, v_w_out_gla),
                  (w_merge_out, m_w_merge_out, v_w_merge_out))
    big_out = [_adamw(w[0], [plane_out, other_out], m[0], v[0], f"adamw_w_out_{i}", grad_row=i * (D // 4))
               for i, (w, m, v) in enumerate(out_params)]

    results = []
    for kind in range(4):
        small_kind = [p[kind] for p in small_out]
        w_in_part = jnp.transpose(big_in[kind])[None]
        outs3 = [big_out[i][kind][None] for i in range(3)]
        results.extend(small_kind[0:2] + [w_in_part] + small_kind[2:8] + outs3 + small_kind[8:9])
    return (loss, grad_x, *results)
```

```python
import functools

import jax
import jax.numpy as jnp
from jax import lax
from jax.experimental import pallas as pl
from jax.experimental.pallas import tpu as pltpu

F32 = jnp.float32
BF16 = jnp.bfloat16
MESH = pl.DeviceIdType.MESH

D = 1024
N_META = 16
CHUNK = 64
PAD_FRONT = CHUNK - N_META
N_HEADS = 4
HEAD_K = 128
HEAD_V = 256
RANK = 16
EPS = 1e-6
GATE_NORM = 16.0
N_IN = 9248
SHARD_IN = N_IN // 4
LANES = 128
N_CONV_TILES = 8
W_CONV = 4096
W_GLA = 2048
W_TAIL = 3072
N_MAIN = W_CONV + W_GLA + W_TAIL
OFF_Q, OFF_K, OFF_V, OFF_R = 4096, 4608, 5120, 6144
OFF_LR, OFF_MA = 7168, 7200
MIB = 1024 * 1024

ADAM_LR = 0.001
ADAM_B1 = 0.9
ADAM_B2 = 0.999
ADAM_EPS = 1e-08
ADAM_WD = 0.01
ADAM_STEP = 10


def _params(sem=None, vmem_mib=None):
    return pltpu.CompilerParams(
        dimension_semantics=sem,
        vmem_limit_bytes=None if vmem_mib is None else vmem_mib * MIB)


def _pick_tile(n, target, mult):
    best = None
    for t in range(mult, min(n, target) + 1, mult):
        if n % t == 0:
            best = t
    return n if best is None else best


def _sigmoid(v):
    return 1.0 / (1.0 + jnp.exp(-v))


def _log_sigmoid(v):
    return jnp.minimum(v, 0.0) - jnp.log(1.0 + jnp.exp(-jnp.abs(v)))


def _dot(a, b):
    return jnp.dot(a, b, preferred_element_type=F32)


def _dot_nt(a, b):
    return lax.dot_general(a, b, (((1,), (1,)), ((), ())), preferred_element_type=F32)


def _dot_tn(a, b):
    return lax.dot_general(a, b, (((0,), (0,)), ((), ())), preferred_element_type=F32)


def _gather_all(name, arrs):
    n = len(arrs)
    flips = tuple((m >> 2 & 1, m >> 1 & 1, m & 1) for m in range(1, 8))

    def body(*refs):
        ins, outs = refs[:n], refs[n:2 * n]
        send_sems, recv_sems, local_sems = refs[2 * n:]
        pos = (lax.axis_index("x"), lax.axis_index("y"), lax.axis_index("c"))

        def slot_of(p):
            return 4 * p[0] + 2 * p[1] + p[2]

        peers = [tuple(1 - pos[a] if f[a] else pos[a] for a in range(3)) for f in flips]
        me = slot_of(pos)
        copies = []
        for i in range(n):
            cp = pltpu.make_async_copy(ins[i], outs[i].at[me], local_sems.at[i])
            cp.start()
            copies.append(cp)
        sends = []
        for i in range(n):
            for k, peer in enumerate(peers):
                cp = pltpu.make_async_remote_copy(
                    src_ref=ins[i], dst_ref=outs[i].at[me], send_sem=send_sems.at[i, k], recv_sem=recv_sems.at[i, k],
                    device_id=peer, device_id_type=MESH)
                cp.start()
                sends.append(cp)
        for i in range(n):
            for k, peer in enumerate(peers):
                pltpu.make_async_remote_copy(
                    src_ref=ins[i], dst_ref=outs[i].at[slot_of(peer)], send_sem=send_sems.at[i, k],
                    recv_sem=recv_sems.at[i, k], device_id=peer, device_id_type=MESH).wait_recv()
        for cp in sends:
            cp.wait_send()
        for cp in copies:
            cp.wait()

    hbm = pl.BlockSpec(memory_space=pl.ANY)
    outs = pl.pallas_call(
        body, name=name, out_shape=[jax.ShapeDtypeStruct((8,) + a.shape, a.dtype) for a in arrs],
        in_specs=[hbm] * n, out_specs=[hbm] * n,
        scratch_shapes=[pltpu.SemaphoreType.DMA((n, 7)), pltpu.SemaphoreType.DMA((n, 7)),
                        pltpu.SemaphoreType.DMA((n,))],
        compiler_params=pltpu.CompilerParams(has_side_effects=True),
    )(*arrs)
    return list(outs)


def _gather_via_sibling(name, arrs, slotted):
    n = len(arrs)
    out_shape = [jax.ShapeDtypeStruct(a.shape if slotted[i] else (4,) + a.shape, a.dtype)
                 for i, a in enumerate(arrs)]

    def body(*refs):
        ins, outs = refs[:n], refs[n:2 * n]
        send_sems, recv_sems, local_sems = refs[2 * n:]
        x, y, c = lax.axis_index("x"), lax.axis_index("y"), lax.axis_index("c")
        me = 2 * x + y
        chips = [(1 - x, y), (x, 1 - y), (1 - x, 1 - y)]

        def half(ref, which):
            rows = ref.shape[0]
            cut = rows // 2 // 16 * 16
            return ref.at[pl.ds(0, cut)] if which == 0 else ref.at[pl.ds(cut, rows - cut)]

        def copy(src, dst, i, k, to):
            return pltpu.make_async_remote_copy(
                src_ref=src, dst_ref=dst, send_sem=send_sems.at[i, k], recv_sem=recv_sems.at[i, k],
                device_id=to, device_id_type=MESH)

        def run(mine):
            other = 1 - mine
            local, sends = [], []
            whole = [(not slotted[i]) and arrs[i].shape[0] < 32 for i in range(n)]
            for i in range(n):
                own = outs[i].at[me] if slotted[i] else ins[i]
                if not slotted[i]:
                    cp = pltpu.make_async_copy(ins[i], outs[i].at[me], local_sems.at[i])
                    cp.start()
                    local.append(cp)
                for k, (px, py) in enumerate(chips):
                    if whole[i]:
                        cp = copy(own, outs[i].at[me], i, k, (px, py, mine))
                    elif k < 2:
                        cp = copy(half(own, mine), half(outs[i].at[me], mine), i, k, (px, py, mine))
                    else:
                        continue
                    cp.start()
                    sends.append(cp)
            via = mine
            for k in (via, 1 - via, 2):
                px, py = chips[k]
                slot = 2 * px + py
                source = (px, py, mine) if k < 2 else chips[1 - via] + (mine,)
                for i in range(n):
                    if whole[i]:
                        copy(outs[i].at[slot], outs[i].at[slot], i, k, (px, py, mine)).wait_recv()
                        continue
                    landed = half(outs[i].at[slot], mine)
                    copy(landed, landed, i, k, source).wait_recv()
                    if k == via:
                        cp = copy(landed, landed, i, 2, chips[1 - via] + (mine,))
                        cp.start()
                        sends.append(cp)
                    cp = copy(landed, landed, i, 3 + k, (x, y, other))
                    cp.start()
                    sends.append(cp)
            for k, (px, py) in enumerate(chips):
                slot = 2 * px + py
                for i in range(n):
                    if whole[i]:
                        continue
                    passed = half(outs[i].at[slot], other)
                    copy(passed, passed, i, 3 + k, (x, y, other)).wait_recv()
            for cp in sends:
                cp.wait_send()
            for cp in local:
                cp.wait()

        for mine in (0, 1):
            pl.when(c == mine)(functools.partial(run, mine))

    hbm = pl.BlockSpec(memory_space=pl.ANY)
    outs = pl.pallas_call(
        body, name=name, out_shape=out_shape,
        in_specs=[hbm] * n, out_specs=[hbm] * n,
        scratch_shapes=[pltpu.SemaphoreType.DMA((n, 6)), pltpu.SemaphoreType.DMA((n, 6)),
                        pltpu.SemaphoreType.DMA((n,))],
        input_output_aliases={i: i for i in range(n) if slotted[i]},
        compiler_params=pltpu.CompilerParams(has_side_effects=True),
    )(*arrs)
    return list(outs)


HBM_SPEC = pl.BlockSpec(memory_space=pltpu.HBM)
SEM_SPEC = pl.BlockSpec(memory_space=pltpu.SEMAPHORE)
DATAFLOW = pltpu.SideEffectType.DATAFLOW_SIDE_EFFECTING


def _split_peers(mode):
    x, y, c = lax.axis_index("x"), lax.axis_index("y"), lax.axis_index("c")
    if mode == "swap":
        return 0, [((x, y, 1 - c), 0)]
    return 2 * x + y, [((1 - x, y, c), 2 * (1 - x) + y), ((x, 1 - y, c), 2 * x + 1 - y),
                       ((1 - x, 1 - y, c), 2 * (1 - x) + 1 - y)]


def _split_refs(mode, src, landing, me, peer_slot):
    if mode == "gather":
        return src.at[me], landing.at[me]
    if mode == "scatter":
        return src.at[peer_slot], landing.at[me]
    return src, landing


def _plane_start(name, arrs, mode, after):
    n = len(arrs)
    n_peers = 1 if mode == "swap" else 3
    if mode == "gather":
        srcs, lands = [], list(arrs)
    else:
        srcs, lands = list(arrs), [lax.empty(a.shape, a.dtype) for a in arrs]
    n_src = len(srcs)

    def body(*refs):
        landing = refs[n_src:n_src + n]
        sources = refs[:n_src] if n_src else landing
        send_sems, recv_sems = refs[n_src + n + 1], refs[n_src + n + 2]
        token = refs[-1]
        me, peers = _split_peers(mode)
        for i in range(n):
            for k, (peer, peer_slot) in enumerate(peers):
                src, dst = _split_refs(mode, sources[i], landing[i], me, peer_slot)
                pltpu.make_async_remote_copy(
                    src_ref=src, dst_ref=dst, send_sem=send_sems.at[n_peers * i + k],
                    recv_sem=recv_sems.at[n_peers * i + k], device_id=peer, device_id_type=MESH).start()
        token[...] = jnp.zeros_like(token)

    hbm_in = [pltpu.with_memory_space_constraint(a, pltpu.HBM) for a in srcs + lands]
    out = pl.pallas_call(
        body, name=name,
        out_shape=[pltpu.SemaphoreType.DMA((n_peers * n,)), pltpu.SemaphoreType.DMA((n_peers * n,))]
                  + [pltpu.HBM(a.shape, a.dtype) for a in lands]
                  + [jax.ShapeDtypeStruct((8, LANES), F32)],
        in_specs=[HBM_SPEC] * (n_src + n) + [pl.BlockSpec(memory_space=pl.ANY)],
        out_specs=[SEM_SPEC, SEM_SPEC] + [HBM_SPEC] * n + [pl.BlockSpec(memory_space=pltpu.VMEM)],
        input_output_aliases={n_src + i: 2 + i for i in range(n)},
        compiler_params=pltpu.CompilerParams(has_side_effects=DATAFLOW),
    )(*hbm_in, after)
    return out[:-1], out[-1]


def _plane_wait(name, state, mode, after):
    send_sems, recv_sems = state[0], state[1]
    bufs = list(state[2:])
    n = len(bufs)
    n_peers = 1 if mode == "swap" else 3

    def body(*refs):
        landing = refs[:n]
        send_sems, recv_sems = refs[n], refs[n + 1]
        _, peers = _split_peers(mode)
        for i in range(n):
            for k, (peer, peer_slot) in enumerate(peers):
                arrived = landing[i] if mode == "swap" else landing[i].at[peer_slot]
                cp = pltpu.make_async_remote_copy(
                    src_ref=arrived, dst_ref=arrived, send_sem=send_sems.at[n_peers * i + k],
                    recv_sem=recv_sems.at[n_peers * i + k], device_id=peer, device_id_type=MESH)
                cp.wait_send()
                cp.wait_recv()

    out = pl.pallas_call(
        body, name=name,
        out_shape=[pltpu.HBM(a.shape, a.dtype) for a in bufs],
        in_specs=[HBM_SPEC] * n + [SEM_SPEC, SEM_SPEC, pl.BlockSpec(memory_space=pl.ANY)],
        out_specs=[HBM_SPEC] * n,
        input_output_aliases={i: i for i in range(n)},
        compiler_params=pltpu.CompilerParams(has_side_effects=DATAFLOW),
    )(*bufs, send_sems, recv_sems, after)
    return list(out)


def _tile_2d(rows, cols, row_mult, max_elems=512 * 1024):
    if rows % row_mult == 0:
        rt = _pick_tile(rows, max(row_mult, max_elems // cols), row_mult)
        return (rt, cols), rows // rt, lambda i: (i, 0)
    ct = _pick_tile(cols, max(LANES, max_elems // rows), LANES)
    return (rows, ct), cols // ct, lambda i: (0, i)


def _cast_into_slot(a, slot, name):
    block, steps, index = _tile_2d(a.shape[0], a.shape[1], 16)

    def body(slot_ref, a_ref, o_ref):
        o_ref[...] = a_ref[...].astype(BF16)

    return pl.pallas_call(
        body, name=name,
        grid_spec=pltpu.PrefetchScalarGridSpec(
            num_scalar_prefetch=1, grid=(steps,),
            in_specs=[pl.BlockSpec(block, lambda i, s: index(i))],
            out_specs=pl.BlockSpec((None,) + block, lambda i, s: (s[0],) + index(i))),
        out_shape=jax.ShapeDtypeStruct((4,) + a.shape, BF16),
        compiler_params=_params(("arbitrary",)),
    )(slot, a)


def _sum_slots(buf, name, own=None, slot=None):
    n_slots, rows, cols = buf.shape
    (br, bc), steps, index = _tile_2d(rows, cols, 16, 320 * 1024)

    def body(*refs):
        if own is None:
            b_ref, o_ref = refs
        else:
            slot_ref, b_ref, own_ref, o_ref = refs
        acc = None
        for s in range(n_slots):
            term = b_ref[s] if own is None else jnp.where(slot_ref[0] == s, own_ref[...], b_ref[s])
            acc = term.astype(F32) if acc is None else acc + term.astype(F32)
        o_ref[...] = acc

    out_shape = jax.ShapeDtypeStruct((rows, cols), F32)
    if own is None:
        return pl.pallas_call(
            body, name=name, grid=(steps,),
            in_specs=[pl.BlockSpec((n_slots, br, bc), lambda i: (0,) + index(i))],
            out_specs=pl.BlockSpec((br, bc), index), out_shape=out_shape,
            compiler_params=_params(("parallel",), 48),
        )(buf)
    return pl.pallas_call(
        body, name=name,
        grid_spec=pltpu.PrefetchScalarGridSpec(
            num_scalar_prefetch=1, grid=(steps,),
            in_specs=[pl.BlockSpec((n_slots, br, bc), lambda i, s: (0,) + index(i)),
                      pl.BlockSpec((None, br, bc), lambda i, s: (s[0],) + index(i))],
            out_specs=pl.BlockSpec((br, bc), lambda i, s: index(i))),
        out_shape=out_shape,
        compiler_params=_params(("arbitrary",), 48),
    )(slot, buf, own)


def _sum_small(bufs, name):
    n = len(bufs)

    def body(*refs):
        for b_ref, o_ref in zip(refs[:n], refs[n:]):
            acc = b_ref[0]
            for s in range(1, b_ref.shape[0]):
                acc = acc + b_ref[s]
            o_ref[...] = acc

    vmem = pl.BlockSpec(memory_space=pltpu.VMEM)
    return pl.pallas_call(
        body, name=name, in_specs=[vmem] * n, out_specs=[vmem] * n,
        out_shape=[jax.ShapeDtypeStruct(b.shape[1:], b.dtype) for b in bufs],
    )(*bufs)


def _adam_update(w, g, m, v):
    c1 = 1.0 - ADAM_B1 ** ADAM_STEP
    c2 = 1.0 - ADAM_B2 ** ADAM_STEP
    m_new = ADAM_B1 * m + (1.0 - ADAM_B1) * g
    v_new = ADAM_B2 * v + (1.0 - ADAM_B2) * (g * g)
    m_hat = m_new / c1
    v_hat = v_new / c2
    return -ADAM_LR * (m_hat / (jnp.sqrt(v_hat) + ADAM_EPS) + ADAM_WD * w), m_new, v_new


def _adamw_small(params, slot, name):
    n = len(params)

    def spec_of(shape):
        lead = (None,) * (len(shape) - 2)
        return pl.BlockSpec(lead + tuple(shape[-2:]), lambda i, s, k=len(shape): (0,) * k)

    in_specs, operands, out_specs, out_shape = [], [], [], []
    for w, g, m, v in params:
        shard = g.shape[-1] != w.shape[-1]
        g_spec = pl.BlockSpec(tuple(w.shape[-2:]), (lambda i, s: (0, s[0])) if shard else (lambda i, s: (0, 0)))
        in_specs += [spec_of(w.shape), g_spec, spec_of(m.shape), spec_of(v.shape)]
        operands += [w, g, m, v]
        out_specs += [spec_of(w.shape)] * 4
        out_shape += [jax.ShapeDtypeStruct(w.shape, F32)] * 4

    def body(slot_ref, *refs):
        ins, outs = refs[:4 * n], refs[4 * n:]
        for p in range(n):
            w_ref, g_ref, m_ref, v_ref = ins[4 * p:4 * p + 4]
            g = g_ref[...]
            delta, m_new, v_new = _adam_update(w_ref[...], g, m_ref[...], v_ref[...])
            for o_ref, val in zip(outs[4 * p:4 * p + 4], (g, delta, m_new, v_new)):
                o_ref[...] = val

    out = pl.pallas_call(
        body, name=name,
        grid_spec=pltpu.PrefetchScalarGridSpec(num_scalar_prefetch=1, grid=(1,), in_specs=in_specs, out_specs=out_specs),
        out_shape=out_shape,
    )(slot, *operands)
    return [tuple(out[4 * p:4 * p + 4]) for p in range(n)]


def _adamw(w, grads, m, v, name, grad_row=0):
    rows, cols = w.shape
    (rt, _), _, _ = _tile_2d(rows, cols, 8, 160 * 1024)
    assert grad_row % rt == 0
    n_g = len(grads)

    def body(*refs):
        w_ref = refs[0]
        g_refs = refs[1:1 + n_g]
        m_ref, v_ref, g_out, d_out, m_out, v_out = refs[1 + n_g:]
        g = g_refs[0][...]
        for r in g_refs[1:]:
            g = g + r[...]
        g_out[...] = g
        d_out[...], m_out[...], v_out[...] = _adam_update(w_ref[...], g, m_ref[...], v_ref[...])

    spec = pl.BlockSpec((rt, cols), lambda i: (i, 0))
    grad_spec = pl.BlockSpec((rt, cols), lambda i: (i + grad_row // rt, 0))
    shape = jax.ShapeDtypeStruct((rows, cols), F32)
    return pl.pallas_call(
        body, name=name, grid=(rows // rt,),
        in_specs=[spec] + [grad_spec] * n_g + [spec] * 2, out_specs=[spec] * 4, out_shape=[shape] * 4,
        compiler_params=_params(("parallel",), 48),
    )(w, *grads, m, v)


def _weight_pieces():
    pieces = []
    for j in range(N_CONV_TILES):
        for g in range(4):
            pieces.append((512 * j + 128 * g, D * g + 128 * j, 128))
    for hd in range(N_HEADS):
        base = W_CONV + 512 * hd
        pieces.append((base, OFF_Q + HEAD_K * hd, HEAD_K))
        pieces.append((base + HEAD_K, OFF_K + HEAD_K * hd, HEAD_K))
        pieces.append((base + 2 * HEAD_K, OFF_V + HEAD_V * hd, HEAD_V))
    pieces.append((W_CONV + W_GLA, OFF_R, D))
    pieces.append((W_CONV + W_GLA + D, OFF_MA, 2 * D))
    return pieces


N_WEIGHT_COPIES = len(_weight_pieces()) + 1


def _load_weights(w_hbm, w_s, wlr_s, sems):
    copies = [pltpu.make_async_copy(w_hbm.at[pl.ds(src, n)], w_s.at[pl.ds(dst, n)], sems.at[i])
              for i, (dst, src, n) in enumerate(_weight_pieces())]
    copies.append(pltpu.make_async_copy(w_hbm.at[pl.ds(OFF_LR, LANES)], wlr_s, sems.at[N_WEIGHT_COPIES - 1]))
    for cp in copies:
        cp.start()
    for cp in copies:
        cp.wait()


def _in_proj(x_rows, head, g_pre, w_full_t, n_seq, lf):
    t_rows = n_seq * lf
    tm = _pick_tile(t_rows, 384, LANES)
    n_main = N_MAIN
    n_chunks = lf // CHUNK
    per_tile = tm // CHUNK

    def body(*refs):
        x_refs = refs[:per_tile]
        head_ref, g_ref, w_hbm, h_ref, proj_ref, ut_ref, lr_ref, w_s, wlr_s, w_sems = refs[per_tile:]
        i = pl.program_id(0)

        @pl.when(i == 0)
        def _():
            _load_weights(w_hbm, w_s, wlr_s, w_sems)

        rowi = lax.broadcasted_iota(jnp.int32, (tm, 1), 0)
        is_tok_row = jnp.zeros((tm, 1), F32)
        for kk in range(per_tile):
            f = jnp.where(((i * per_tile + kk) % n_chunks) != 0, 1.0, 0.0)
            is_tok_row = jnp.where((rowi >= kk * CHUNK) & (rowi < (kk + 1) * CHUNK), f, is_tok_row)
        tokens = jnp.concatenate([x_ref[...] for x_ref in x_refs], axis=0)
        heads = jnp.concatenate([head_ref[...]] * per_tile, axis=0)
        hh = jnp.where(is_tok_row > 0.0, tokens, heads)
        h_ref[...] = hh
        rstd = lax.rsqrt(jnp.mean(hh * hh, axis=-1, keepdims=True) + EPS)
        uf = hh * rstd * g_ref[...]
        u = uf.astype(BF16)
        ut_ref[...] = jnp.transpose(uf).astype(BF16)
        lr_ref[...] = _dot_nt(u, wlr_s[...])
        for j in range(n_main // D):
            cols = slice(j * D, (j + 1) * D)
            proj_ref[:, cols] = _dot_nt(u, w_s[cols, :]).astype(BF16)

    def token_chunk(kk):
        def index(i):
            q = i * per_tile + kk
            return (q // n_chunks) * (n_chunks - 1) + jnp.maximum(q % n_chunks - 1, 0), 0
        return pl.BlockSpec((CHUNK, D), index)

    return pl.pallas_call(
        body, name="in_proj", grid=(t_rows // tm,),
        in_specs=[token_chunk(kk) for kk in range(per_tile)]
                 + [pl.BlockSpec((CHUNK, D), lambda i: (0, 0)),
                    pl.BlockSpec((1, D), lambda i: (0, 0)),
                    pl.BlockSpec(memory_space=pl.ANY)],
        out_specs=[pl.BlockSpec((tm, D), lambda i: (i, 0)),
                   pl.BlockSpec((tm, n_main), lambda i: (i, 0)),
                   pl.BlockSpec((D, tm), lambda i: (0, i)),
                   pl.BlockSpec((tm, LANES), lambda i: (i, 0))],
        out_shape=[jax.ShapeDtypeStruct((t_rows, D), F32),
                   jax.ShapeDtypeStruct((t_rows, n_main), BF16),
                   jax.ShapeDtypeStruct((D, t_rows), BF16),
                   jax.ShapeDtypeStruct((t_rows, LANES), F32)],
        scratch_shapes=[pltpu.VMEM((n_main, D), BF16), pltpu.VMEM((LANES, D), BF16),
                        pltpu.SemaphoreType.DMA((N_WEIGHT_COPIES,))],
        compiler_params=_params(("arbitrary",), 56),
    )(*[x_rows] * per_tile, head, g_pre, w_full_t)


CONV_TILES_PER_STEP = 4


def _conv_parts(p_ref, w_ref, t):
    cb = p_ref[:, 512 * t:512 * t + 128].astype(F32)
    cc = p_ref[:, 512 * t + 128:512 * t + 256].astype(F32)
    cx = p_ref[:, 512 * t + 256:512 * t + 384].astype(F32)
    cz = p_ref[:, 512 * t + 384:512 * t + 512].astype(F32)
    rows = cb.shape[0]
    w = w_ref[:, 128 * t:128 * (t + 1)]
    p = cc * cx
    conv = pltpu.roll(p, 1, 0) * w[0:1] + p * w[1:2] + pltpu.roll(p, rows - 1, 0) * w[2:3]
    sz = _sigmoid(cz)
    return cb, cc, cx, cz, p, conv, sz, w


def _conv_fwd(proj, conv_w, n_seq, lf):
    per = CONV_TILES_PER_STEP

    def body(p_ref, w_ref, y_ref):
        for t in range(per):
            cb, _, _, cz, _, conv, sz, _ = _conv_parts(p_ref, w_ref, t)
            y_ref[:, 128 * t:128 * (t + 1)] = (cb * conv * (cz * sz)).astype(BF16)

    return pl.pallas_call(
        body, name="conv_fwd", grid=(n_seq, N_CONV_TILES // per),
        in_specs=[pl.BlockSpec((lf, 512 * per), lambda b, j: (b, j)),
                  pl.BlockSpec((3, 128 * per), lambda b, j: (0, j))],
        out_specs=pl.BlockSpec((lf, 128 * per), lambda b, j: (b, j)),
        out_shape=jax.ShapeDtypeStruct((n_seq * lf, D), BF16),
        compiler_params=_params(("parallel", "parallel"), 48),
    )(proj, conv_w)


def _conv_bwd(proj, conv_w, dyc, n_seq, lf):
    per = CONV_TILES_PER_STEP

    def body(p_ref, w_ref, dy_ref, dp_ref, dw_ref):
        for t in range(per):
            cb, cc, cx, cz, p, conv, sz, w = _conv_parts(p_ref, w_ref, t)
            rows = cb.shape[0]
            dy = dy_ref[:, 128 * t:128 * (t + 1)].astype(F32)
            silu = cz * sz
            dcb = dy * conv * silu
            dconv = dy * cb * silu
            dcz = dy * cb * conv * (sz * (1.0 + cz * (1.0 - sz)))
            d_next = pltpu.roll(dconv, rows - 1, 0)
            d_prev = pltpu.roll(dconv, 1, 0)
            dp = d_next * w[0:1] + dconv * w[1:2] + d_prev * w[2:3]
            base = 512 * t
            dp_ref[:, base:base + 128] = dcb.astype(BF16)
            dp_ref[:, base + 128:base + 256] = (dp * cx).astype(BF16)
            dp_ref[:, base + 256:base + 384] = (dp * cc).astype(BF16)
            dp_ref[:, base + 384:base + 512] = dcz.astype(BF16)
            lanes = slice(128 * t, 128 * (t + 1))
            dw_ref[0:1, lanes] = jnp.sum(dconv * pltpu.roll(p, 1, 0), axis=0, keepdims=True)
            dw_ref[1:2, lanes] = jnp.sum(dconv * p, axis=0, keepdims=True)
            dw_ref[2:3, lanes] = jnp.sum(dconv * pltpu.roll(p, rows - 1, 0), axis=0, keepdims=True)

    return pl.pallas_call(
        body, name="conv_bwd", grid=(n_seq, N_CONV_TILES // per),
        in_specs=[pl.BlockSpec((lf, 512 * per), lambda b, j: (b, j)),
                  pl.BlockSpec((3, 128 * per), lambda b, j: (0, j)),
                  pl.BlockSpec((lf, 128 * per), lambda b, j: (b, j))],
        out_specs=[pl.BlockSpec((lf, 512 * per), lambda b, j: (b, j)),
                   pl.BlockSpec((None, 3, 128 * per), lambda b, j: (b, 0, j))],
        out_shape=[jax.ShapeDtypeStruct((n_seq * lf, W_CONV), BF16),
                   jax.ShapeDtypeStruct((n_seq, 3, D), F32)],
        compiler_params=_params(("parallel", "parallel"), 48),
    )(proj, conv_w, dyc)


GROUP = 3
GROUP_ROWS = GROUP * CHUNK


def _row_group(shape):
    row = lax.broadcasted_iota(jnp.int32, shape, 0)
    grp = jnp.zeros(shape, jnp.int32)
    for r in range(1, GROUP):
        grp = grp + (row >= r * CHUNK).astype(jnp.int32)
    return grp


def _lane_group(shape, width):
    lane = lax.broadcasted_iota(jnp.int32, shape, 1)
    grp = jnp.zeros(shape, jnp.int32)
    for r in range(1, GROUP):
        grp = grp + (lane >= r * width).astype(jnp.int32)
    return grp


def _score_mask(direction):
    shape = (GROUP_ROWS, GROUP_ROWS)
    row = lax.broadcasted_iota(jnp.int32, shape, 0)
    col = lax.broadcasted_iota(jnp.int32, shape, 1)
    same = _row_group(shape) == _lane_group(shape, CHUNK)
    return same & ((col <= row) if direction == 0 else (col > row))


def _diag_blocks(v):
    w = v.shape[1]
    wide = jnp.concatenate([v] * GROUP, axis=1)
    return jnp.where(_row_group(wide.shape) == _lane_group(wide.shape, w), wide, jnp.zeros_like(wide))


def _per_chunk_dot(lhs, state, transposed):
    outs = []
    for r in range(GROUP):
        rows = lhs[r * CHUNK:(r + 1) * CHUNK, :]
        blk = state[:, r * HEAD_K:(r + 1) * HEAD_K]
        outs.append(_dot_nt(rows, blk) if transposed else _dot(rows, blk))
    return jnp.concatenate(outs, axis=0)


def _chunk_cumsum(v, suffix):
    pos = lax.broadcasted_iota(jnp.int32, v.shape, 0) & (CHUNK - 1)
    shift = 1
    while shift < CHUNK:
        if suffix:
            moved = pltpu.roll(v, GROUP_ROWS - shift, 0)
            v = v + jnp.where(pos < CHUNK - shift, moved, 0.0)
        else:
            moved = pltpu.roll(v, shift, 0)
            v = v + jnp.where(pos >= shift, moved, 0.0)
        shift *= 2
    return v


def _per_chunk_rows(rows_of_chunk):
    w = rows_of_chunk[0].shape[1]
    return jnp.concatenate([jnp.broadcast_to(v, (CHUNK, w)) for v in rows_of_chunk], axis=0)


def _chunk_end_rows(direction, b):
    at = CHUNK - 1 if direction == 0 else 0
    return [b[r * CHUNK + at:r * CHUNK + at + 1, :] for r in range(GROUP)]


def _gla_gates(lr_bf, wg_ref, bg_ref, lf):
    z = _dot(lr_bf, wg_ref[...]) + bg_ref[...]
    valid = lax.broadcasted_iota(jnp.int32, (lf, HEAD_K), 0) >= PAD_FRONT
    return z, valid


def _group_unroll(n_groups):
    return n_groups if n_groups <= 11 else 1


def _group_rows(g):
    return pl.ds(pl.multiple_of(g * GROUP_ROWS, GROUP_ROWS), GROUP_ROWS)


def _chunk_decay(direction, g, r, b_s):
    base = g * GROUP_ROWS + r * CHUNK
    if direction == 0:
        grp = b_s[pl.ds(pl.multiple_of(base + CHUNK - 8, 8), 8), :]
        return jnp.exp(grp[7:8, :])
    grp = b_s[pl.ds(pl.multiple_of(base, 8), 8), :]
    return jnp.exp(grp[0:1, :])


def _state_scan(direction, n_groups, b_s, st_s, reverse):
    ascending = (direction == 0) != reverse

    def step(i, carry):
        g = i if ascending else n_groups - 1 - i
        for rr in range(GROUP):
            r = rr if ascending else GROUP - 1 - rr
            lanes = slice(r * HEAD_K, (r + 1) * HEAD_K)
            decay = _chunk_decay(direction, g, r, b_s)
            local = st_s[g, :, lanes]
            st_s[g, :, lanes] = carry
            carry = (local + carry * decay) if reverse else (carry * decay + local)
        return carry

    lax.fori_loop(0, n_groups, step, jnp.zeros((HEAD_V, HEAD_K), F32), unroll=_group_unroll(n_groups))


def _gla_states(direction, n_groups, qkv_ref, g_s, b_s, st_s):
    def local(g, carry):
        rows = _group_rows(g)
        b = _chunk_cumsum(g_s[rows, :], direction == 1)
        b_s[rows, :] = b
        b_end = _per_chunk_rows(_chunk_end_rows(direction, b))
        k = qkv_ref[rows, 128:256].astype(F32)
        v = qkv_ref[rows, 256:512]
        k_dec = (k * jnp.exp(b_end - b)).astype(BF16)
        st_s[g] = _dot_tn(v, _diag_blocks(k_dec))
        return carry

    lax.fori_loop(0, n_groups, local, 0, unroll=_group_unroll(n_groups))
    _state_scan(direction, n_groups, b_s, st_s, False)


def _gla_fwd(proj, lr, wgf, wgb, bgf, bgb, n_seq, lf):
    assert lf % GROUP_ROWS == 0
    n_groups = lf // GROUP_ROWS
    scale = HEAD_K ** -0.5

    def body(qkv_ref, lr_ref, wgf_ref, wgb_ref, bgf_ref, bgb_ref, o_ref, g_s, b_s2, st_s2):
        lr_bf = lr_ref[...].astype(BF16)
        for direction in (0, 1):
            wg_ref, bg_ref = ((wgf_ref, bgf_ref), (wgb_ref, bgb_ref))[direction]
            z, valid = _gla_gates(lr_bf, wg_ref, bg_ref, lf)
            g_s[...] = jnp.where(valid, _log_sigmoid(z) / GATE_NORM, 0.0)
            _gla_states(direction, n_groups, qkv_ref, g_s, b_s2.at[direction], st_s2.at[direction])
        masks = [_score_mask(0), _score_mask(1)]

        def out(g, carry):
            rows = _group_rows(g)
            q = qkv_ref[rows, 0:128].astype(F32) * scale
            k = qkv_ref[rows, 128:256].astype(F32)
            v = qkv_ref[rows, 256:512]
            o = None
            for direction in (0, 1):
                b = b_s2[direction, rows, :]
                q_in = (q * jnp.exp(b)).astype(BF16)
                k_in = (k * jnp.exp(-b)).astype(BF16)
                s = jnp.where(masks[direction], _dot_nt(q_in, k_in), 0.0).astype(BF16)
                part = _dot(s, v) + _per_chunk_dot(q_in, st_s2[direction, g].astype(BF16), True)
                o = part if o is None else o + part
            o_ref[rows, :] = o
            return carry

        lax.fori_loop(0, n_groups, out, 0, unroll=_group_unroll(n_groups))

    return pl.pallas_call(
        body, name="gla_fwd", grid=(n_seq, N_HEADS),
        in_specs=[pl.BlockSpec((lf, 512), lambda b, h: (b, N_CONV_TILES + h)),
                  pl.BlockSpec((lf, LANES), lambda b, h: (b, 0)),
                  pl.BlockSpec((None, LANES, HEAD_K), lambda b, h: (h, 0, 0)),
                  pl.BlockSpec((None, LANES, HEAD_K), lambda b, h: (h, 0, 0)),
                  pl.BlockSpec((None, 1, HEAD_K), lambda b, h: (h, 0, 0)),
                  pl.BlockSpec((None, 1, HEAD_K), lambda b, h: (h, 0, 0))],
        out_specs=pl.BlockSpec((lf, HEAD_V), lambda b, h: (b, h)),
        out_shape=jax.ShapeDtypeStruct((n_seq * lf, D), F32),
        scratch_shapes=[pltpu.VMEM((lf, HEAD_K), F32), pltpu.VMEM((2, lf, HEAD_K), F32),
                        pltpu.VMEM((2, n_groups, HEAD_V, GROUP * HEAD_K), F32)],
        compiler_params=_params(("parallel", "parallel"), 48),
    )(proj, lr, wgf, wgb, bgf, bgb)


def _gla_bwd(proj, lr, d_o, wgf, wgb, bgf, bgb, n_seq, lf, token):
    assert lf % GROUP_ROWS == 0
    n_groups = lf // GROUP_ROWS
    scale = HEAD_K ** -0.5

    def body(qkv_ref, lr_ref, do_ref, wgf_ref, wgb_ref, bgf_ref, bgb_ref, token_ref,
             dqkv_ref, dlr_ref, dwgf_ref, dwgb_ref, dbg_ref,
             g_s, b_s2, fac_s2, dg_s2, st_s2, dst_s2):
        lr_bf = lr_ref[...].astype(BF16)
        gates = ((wgf_ref, bgf_ref), (wgb_ref, bgb_ref))
        for direction in (0, 1):
            wg_ref, bg_ref = gates[direction]
            b_s, st_s, dst_s = b_s2.at[direction], st_s2.at[direction], dst_s2.at[direction]
            z, valid = _gla_gates(lr_bf, wg_ref, bg_ref, lf)
            g_s[...] = jnp.where(valid, _log_sigmoid(z) / GATE_NORM, 0.0)
            fac_s2[direction] = jnp.where(valid, _sigmoid(-z) / GATE_NORM, 0.0)
            _gla_states(direction, n_groups, qkv_ref, g_s, b_s, st_s)

            def state_grad_local(g, carry):
                rows = _group_rows(g)
                q = qkv_ref[rows, 0:128].astype(F32) * scale
                q_in = (q * jnp.exp(b_s[rows, :])).astype(BF16)
                dst_s[g] = _dot_tn(do_ref[rows, :], _diag_blocks(q_in))
                return carry

            lax.fori_loop(0, n_groups, state_grad_local, 0, unroll=_group_unroll(n_groups))
            _state_scan(direction, n_groups, b_s, dst_s, True)

        masks = [_score_mask(0), _score_mask(1)]

        def group_grads(g, carry):
            rows = _group_rows(g)
            q = qkv_ref[rows, 0:128].astype(F32) * scale
            k = qkv_ref[rows, 128:256].astype(F32)
            v = qkv_ref[rows, 256:512]
            d_out = do_ref[rows, :]
            dq_sum = dk_sum = dv_sum = None
            for direction in (0, 1):
                end_row = CHUNK - 1 if direction == 0 else 0
                b = b_s2[direction, rows, :]
                ends = _chunk_end_rows(direction, b)
                b_end = _per_chunk_rows(ends)
                e_pos = jnp.exp(b)
                e_neg = jnp.exp(-b)
                e_end = jnp.exp(b_end - b)
                q_in = q * e_pos
                k_in = k * e_neg
                k_dec = k * e_end
                q_in_bf = q_in.astype(BF16)
                k_in_bf = k_in.astype(BF16)
                state = st_s2[direction, g]
                d_state = dst_s2[direction, g]
                state_bf = state.astype(BF16)
                d_state_bf = d_state.astype(BF16)
                s = jnp.where(masks[direction], _dot_nt(q_in_bf, k_in_bf), 0.0).astype(BF16)
                ds = jnp.where(masks[direction], _dot_nt(d_out, v), 0.0).astype(BF16)
                dv = _dot_tn(s, d_out) + _per_chunk_dot(k_dec.astype(BF16), d_state_bf, True)
                dq_in = _dot(ds, k_in_bf) + _per_chunk_dot(d_out, state_bf, False)
                dk_in = _dot_tn(ds, q_in_bf)
                dk_dec = _per_chunk_dot(v, d_state_bf, False)
                dq = dq_in * e_pos * scale
                dk = dk_in * e_neg + dk_dec * e_end
                dq_sum = dq if dq_sum is None else dq_sum + dq
                dk_sum = dk if dk_sum is None else dk_sum + dk
                dv_sum = dv if dv_sum is None else dv_sum + dv
                dkk = dk_dec * k_dec
                db = dq_in * q_in - dk_in * k_in - dkk
                d_decay = jnp.sum(d_state * state, axis=0, keepdims=True)
                db_end = [jnp.sum(dkk[r * CHUNK:(r + 1) * CHUNK, :], axis=0, keepdims=True)
                          + d_decay[:, r * HEAD_K:(r + 1) * HEAD_K] * jnp.exp(ends[r]) for r in range(GROUP)]
                row = lax.broadcasted_iota(jnp.int32, (GROUP_ROWS, HEAD_K), 0)
                at_end = row == end_row
                for r in range(1, GROUP):
                    at_end = at_end | (row == r * CHUNK + end_row)
                db = db + jnp.where(at_end, _per_chunk_rows(db_end), 0.0)
                dg_s2[direction, rows, :] = _chunk_cumsum(db, direction == 0)
            dqkv_ref[rows, 0:128] = dq_sum.astype(BF16)
            dqkv_ref[rows, 128:256] = dk_sum.astype(BF16)
            dqkv_ref[rows, 256:512] = dv_sum.astype(BF16)
            return carry

        lax.fori_loop(0, n_groups, group_grads, 0, unroll=_group_unroll(n_groups))

        dlr = jnp.zeros((lf, LANES), F32)
        for direction in (0, 1):
            dz = dg_s2[direction] * fac_s2[direction]
            dz_bf = dz.astype(BF16)
            dbg_ref[direction:direction + 1, :] = jnp.sum(dz, axis=0, keepdims=True)
            (dwgf_ref, dwgb_ref)[direction][...] = _dot_tn(lr_bf, dz_bf)
            dlr = dlr + _dot_nt(dz_bf, gates[direction][0][...])

        @pl.when(pl.program_id(1) == 0)
        def _():
            dlr_ref[...] = dlr

        @pl.when(pl.program_id(1) != 0)
        def _():
            dlr_ref[...] = dlr_ref[...] + dlr

    gate_w = pl.BlockSpec((None, LANES, HEAD_K), lambda b, h: (h, 0, 0))
    gate_b = pl.BlockSpec((None, 1, HEAD_K), lambda b, h: (h, 0, 0))
    return pl.pallas_call(
        body, name="gla_bwd", grid=(n_seq, N_HEADS),
        in_specs=[pl.BlockSpec((lf, 512), lambda b, h: (b, N_CONV_TILES + h)),
                  pl.BlockSpec((lf, LANES), lambda b, h: (b, 0)),
                  pl.BlockSpec((lf, HEAD_V), lambda b, h: (b, h)),
                  gate_w, gate_w, gate_b, gate_b,
                  pl.BlockSpec((8, LANES), lambda b, h: (0, 0))],
        out_specs=[pl.BlockSpec((lf, 512), lambda b, h: (b, h)),
                   pl.BlockSpec((lf, LANES), lambda b, h: (b, 0)),
                   pl.BlockSpec((None, None, LANES, HEAD_K), lambda b, h: (b, h, 0, 0)),
                   pl.BlockSpec((None, None, LANES, HEAD_K), lambda b, h: (b, h, 0, 0)),
                   pl.BlockSpec((None, None, 2, HEAD_K), lambda b, h: (b, h, 0, 0))],
        out_shape=[jax.ShapeDtypeStruct((n_seq * lf, W_GLA), BF16),
                   jax.ShapeDtypeStruct((n_seq * lf, LANES), F32),
                   jax.ShapeDtypeStruct((n_seq, N_HEADS, LANES, HEAD_K), F32),
                   jax.ShapeDtypeStruct((n_seq, N_HEADS, LANES, HEAD_K), F32),
                   jax.ShapeDtypeStruct((n_seq, N_HEADS, 2, HEAD_K), F32)],
        scratch_shapes=[pltpu.VMEM((lf, HEAD_K), F32), pltpu.VMEM((2, lf, HEAD_K), F32),
                        pltpu.VMEM((2, lf, HEAD_K), F32), pltpu.VMEM((2, lf, HEAD_K), F32),
                        pltpu.VMEM((2, n_groups, HEAD_V, GROUP * HEAD_K), F32),
                        pltpu.VMEM((2, n_groups, HEAD_V, GROUP * HEAD_K), F32)],
        compiler_params=_params(("parallel", "arbitrary"), 56),
    )(proj, lr, d_o, wgf, wgb, bgf, bgb, token)


def _tail(h, tgt, yc, o, proj, w3, gamma, g_post, lf):
    t_rows = h.shape[0]
    tm = _pick_tile(t_rows, 256, CHUNK)
    n_chunks = lf // CHUNK
    per_tile = tm // CHUNK

    def body(h_ref, *refs):
        tgt_refs = refs[:per_tile]
        (yc_ref, o_ref, r_ref, ma_ref, mb_ref, w_hbm, gamma_ref, gpost_ref,
         dres_ref, yg_ref, merged_ref, dout_ref, dpc_ref, dpg_ref, dyc_ref, do_ref, dtail_ref,
         loss_ref, dgpost_ref, dgamma_ref, w_s, w_sem) = refs[per_tile:]
        i = pl.program_id(0)

        @pl.when(i == 0)
        def _():
            cp = pltpu.make_async_copy(w_hbm, w_s, w_sem)
            cp.start()
            cp.wait()
            loss_ref[...] = jnp.zeros_like(loss_ref)
            dgpost_ref[...] = jnp.zeros_like(dgpost_ref)
            dgamma_ref[...] = jnp.zeros_like(dgamma_ref)

        gamma = gamma_ref[...]
        o = o_ref[...]
        r = r_ref[...].astype(F32)
        sr = _sigmoid(r)
        silu_r = r * sr
        n_parts, rstd_parts = [], []
        for hd in range(N_HEADS):
            oh = o[:, hd * HEAD_V:(hd + 1) * HEAD_V]
            rstd = lax.rsqrt(jnp.mean(oh * oh, axis=-1, keepdims=True) + EPS)
            n_parts.append(oh * rstd)
            rstd_parts.append(rstd)
        n = jnp.concatenate(n_parts, axis=-1)
        gamma_t = jnp.concatenate([gamma] * N_HEADS, axis=-1)
        yg = n * gamma_t * silu_r
        yg_bf = yg.astype(BF16)
        yg_ref[...] = yg_bf
        yc = yc_ref[...]
        pc = _dot(yc, w_s[0])
        pg = _dot(yg_bf, w_s[1])
        sa = _sigmoid(ma_ref[...].astype(F32))
        sb = _sigmoid(mb_ref[...].astype(F32))
        merged = (sa * pc + sb * pg).astype(BF16)
        merged_ref[...] = merged
        out = _dot(merged, w_s[2])
        rstd2 = lax.rsqrt(jnp.mean(out * out, axis=-1, keepdims=True) + EPS)
        nn = out * rstd2
        gpost = gpost_ref[...]
        y = h_ref[...] + nn * gpost

        rowi = lax.broadcasted_iota(jnp.int32, (tm, 1), 0)
        keep = jnp.zeros((tm, 1), F32)
        for kk in range(per_tile):
            is_tok = ((i * per_tile + kk) % n_chunks) != 0
            f = jnp.where(is_tok, 1.0, 0.0)
            keep = jnp.where((rowi >= kk * CHUNK) & (rowi < (kk + 1) * CHUNK), f, keep)
        tgt = jnp.concatenate([t_ref[...] for t_ref in tgt_refs], axis=0)
        diff = jnp.where(keep > 0.0, y - tgt, 0.0)
        loss_ref[...] += jnp.sum(diff * diff) * (0.5 / D)
        dy = diff * (1.0 / D)
        dres_ref[...] = dy
        dgpost_ref[...] += jnp.sum(dy * nn, axis=0, keepdims=True)
        dn = dy * gpost
        dout_f = rstd2 * (dn - nn * jnp.mean(dn * nn, axis=-1, keepdims=True))
        dout = dout_f.astype(BF16)
        dout_ref[...] = jnp.transpose(dout_f).astype(BF16)
        dmerged = _dot_nt(dout, w_s[2])
        dpc_f = dmerged * sa
        dpg_f = dmerged * sb
        dpc = dpc_f.astype(BF16)
        dpg = dpg_f.astype(BF16)
        dpc_ref[...] = jnp.transpose(dpc_f).astype(BF16)
        dpg_ref[...] = jnp.transpose(dpg_f).astype(BF16)
        dtail_ref[:, D:2 * D] = (dmerged * pc * (sa * (1.0 - sa))).astype(BF16)
        dtail_ref[:, 2 * D:3 * D] = (dmerged * pg * (sb * (1.0 - sb))).astype(BF16)
        dyc_ref[...] = _dot_nt(dpc, w_s[0]).astype(BF16)
        dyg = _dot_nt(dpg, w_s[1])
        dtail_ref[:, 0:D] = (dyg * n * gamma_t * (sr * (1.0 + r * (1.0 - sr)))).astype(BF16)
        dgam_full = jnp.sum(dyg * n * silu_r, axis=0, keepdims=True)
        dgam = dgam_full[:, 0:HEAD_V]
        for hd in range(1, N_HEADS):
            dgam = dgam + dgam_full[:, hd * HEAD_V:(hd + 1) * HEAD_V]
        dgamma_ref[...] += dgam
        dng = dyg * gamma_t * silu_r
        do_parts = []
        for hd in range(N_HEADS):
            sl = slice(hd * HEAD_V, (hd + 1) * HEAD_V)
            dnh = dng[:, sl]
            nh = n_parts[hd]
            do_parts.append(rstd_parts[hd] * (dnh - nh * jnp.mean(dnh * nh, axis=-1, keepdims=True)))
        do_ref[...] = jnp.concatenate(do_parts, axis=-1).astype(BF16)

    row = lambda c: pl.BlockSpec((tm, D), lambda i: (i, c))
    col = pl.BlockSpec((D, tm), lambda i: (0, i))

    def tgt_chunk(kk):
        def index(i):
            q = i * per_tile + kk
            return (q // n_chunks) * (n_chunks - 1) + jnp.maximum(q % n_chunks - 1, 0), 0
        return pl.BlockSpec((CHUNK, D), index)

    const = lambda shape: pl.BlockSpec(shape, lambda i: (0, 0))
    act = jax.ShapeDtypeStruct((t_rows, D), BF16)
    act_t = jax.ShapeDtypeStruct((D, t_rows), BF16)
    return pl.pallas_call(
        body, name="tail", grid=(t_rows // tm,),
        in_specs=[row(0)] + [tgt_chunk(kk) for kk in range(per_tile)] + [row(0), row(0), row(6), row(7), row(8),
                  pl.BlockSpec(memory_space=pl.ANY), const((1, HEAD_V)), const((1, D))],
        out_specs=[row(0)] * 3 + [col] * 3 + [row(0)] * 2
                  + [pl.BlockSpec((tm, W_TAIL), lambda i: (i, 0)),
                     const((8, LANES)), const((1, D)), const((1, HEAD_V))],
        out_shape=[jax.ShapeDtypeStruct((t_rows, D), F32)] + [act] * 2 + [act_t] * 3 + [act] * 2
                  + [jax.ShapeDtypeStruct((t_rows, W_TAIL), BF16),
                     jax.ShapeDtypeStruct((8, LANES), F32),
                     jax.ShapeDtypeStruct((1, D), F32),
                     jax.ShapeDtypeStruct((1, HEAD_V), F32)],
        scratch_shapes=[pltpu.VMEM((3, D, D), BF16), pltpu.SemaphoreType.DMA],
        compiler_params=_params(("arbitrary",), 56),
    )(h, *[tgt] * per_tile, yc, o, proj, proj, proj, w3, gamma, g_post)


def _wgrad_t(a_t, b, name, out_dtype=BF16):
    m, t_rows = a_t.shape
    n = b.shape[1]
    tn = D if n % D == 0 else n
    tk = _pick_tile(t_rows, 768, LANES)
    n_k = t_rows // tk

    def body(a_ref, b_ref, o_ref, acc):
        k = pl.program_id(1)

        @pl.when(k == 0)
        def _():
            acc[...] = jnp.zeros_like(acc)

        acc[...] += _dot(a_ref[...], b_ref[...].astype(BF16))

        @pl.when(k == n_k - 1)
        def _():
            o_ref[...] = jnp.transpose(acc[...]).astype(out_dtype)

    return pl.pallas_call(
        body, name=name, grid=(n // tn, n_k),
        in_specs=[pl.BlockSpec((m, tk), lambda j, k: (0, k)),
                  pl.BlockSpec((tk, tn), lambda j, k: (k, j))],
        out_specs=pl.BlockSpec((tn, m), lambda j, k: (j, 0)),
        out_shape=jax.ShapeDtypeStruct((n, m), out_dtype),
        scratch_shapes=[pltpu.VMEM((m, tn), F32)],
        compiler_params=_params(("parallel", "arbitrary"), 48),
    )(a_t, b)


def _dgrad_in(dpc, dpg, dpt, dlr, w_full_t, h, g_pre, dres, token):
    t_rows = h.shape[0]
    tm = _pick_tile(t_rows, 384, 16)
    n_main = N_MAIN

    def body(dpc_ref, dpg_ref, dpt_ref, dlr_ref, w_hbm, h_ref, g_ref, dres_ref, token_ref,
             dh_ref, dg_ref, w_s, wlr_s, w_sems):
        @pl.when(pl.program_id(0) == 0)
        def _():
            _load_weights(w_hbm, w_s, wlr_s, w_sems)
            dg_ref[...] = jnp.zeros_like(dg_ref)

        du = _dot(dlr_ref[...].astype(BF16), wlr_s[...])
        du += _dot(dpc_ref[...], w_s[0:W_CONV, :])
        du += _dot(dpg_ref[...], w_s[W_CONV:W_CONV + W_GLA, :])
        du += _dot(dpt_ref[...], w_s[W_CONV + W_GLA:n_main, :])
        hh = h_ref[...]
        rstd = lax.rsqrt(jnp.mean(hh * hh, axis=-1, keepdims=True) + EPS)
        xhat = hh * rstd
        dg_ref[...] += jnp.sum(du * xhat, axis=0, keepdims=True)
        dx = du * g_ref[...]
        dh_ref[...] = rstd * (dx - xhat * jnp.mean(dx * xhat, axis=-1, keepdims=True)) + dres_ref[...]

    row = lambda width: pl.BlockSpec((tm, width), lambda i: (i, 0))
    return pl.pallas_call(
        body, name="dgrad_in", grid=(t_rows // tm,),
        in_specs=[row(W_CONV), row(W_GLA), row(W_TAIL), row(LANES),
                  pl.BlockSpec(memory_space=pl.ANY),
                  row(D), pl.BlockSpec((1, D), lambda i: (0, 0)), row(D),
                  pl.BlockSpec((8, LANES), lambda i: (0, 0))],
        out_specs=[row(D), pl.BlockSpec((1, D), lambda i: (0, 0))],
        out_shape=[jax.ShapeDtypeStruct((t_rows, D), F32), jax.ShapeDtypeStruct((1, D), F32)],
        scratch_shapes=[pltpu.VMEM((n_main, D), BF16), pltpu.VMEM((LANES, D), BF16),
                        pltpu.SemaphoreType.DMA((N_WEIGHT_COPIES,))],
        compiler_params=_params(("arbitrary",), 56),
    )(dpc, dpg, dpt, dlr, w_full_t, h, g_pre, dres, token)


def _reference_rows(g_conv, g_gla, g_tail, g_lr):
    conv = g_conv.reshape(N_CONV_TILES, 4, 128, D).transpose(1, 0, 2, 3).reshape(W_CONV, D)
    gla = g_gla.reshape(N_HEADS, 512, D)
    q = gla[:, 0:128].reshape(N_HEADS * HEAD_K, D)
    k = gla[:, 128:256].reshape(N_HEADS * HEAD_K, D)
    v = gla[:, 256:512].reshape(N_HEADS * HEAD_V, D)
    return jnp.concatenate([conv, q, k, v, g_tail[0:D], g_lr[0:2 * RANK], g_tail[D:3 * D]], axis=0)


def kernel(x, meta_tokens, norm_pre, w_in, conv_w, w_gate_fwd, b_gate_fwd, w_gate_bwd, b_gate_bwd, gla_norm, w_out_conv, w_out_gla, w_merge_out, norm_post, loss_target, m_meta_tokens, m_norm_pre, m_w_in, m_conv_w, m_w_gate_fwd, m_b_gate_fwd, m_w_gate_bwd, m_b_gate_bwd, m_gla_norm, m_w_out_conv, m_w_out_gla, m_w_merge_out, m_norm_post, v_meta_tokens, v_norm_pre, v_w_in, v_conv_w, v_w_gate_fwd, v_b_gate_fwd, v_w_gate_bwd, v_b_gate_bwd, v_gla_norm, v_w_out_conv, v_w_out_gla, v_w_merge_out, v_norm_post):
    n_seq, seq, _ = x.shape
    lf = CHUNK + seq
    t_rows = n_seq * lf
    shard = 2 * lax.axis_index("x") + lax.axis_index("y")
    shard_arr = jnp.reshape(shard, (1,)).astype(jnp.int32)

    w_in_slots = _cast_into_slot(jnp.transpose(w_in[0]), shard_arr, "cast_w_in")
    w_out_slots = _cast_into_slot(jnp.concatenate([w_out_conv[0], w_out_gla[0], w_merge_out[0]], axis=0), shard_arr,
                                  "cast_w_out")
    w_in_all, meta_all, conv_all, wgf_all, wgb_all = _gather_via_sibling(
        "gather_w_in", [w_in_slots, meta_tokens, conv_w[0], w_gate_fwd[0], w_gate_bwd[0]],
        (True, False, False, False, False))
    w_out_state, _ = _plane_start("gather_w_out_start", [w_out_slots], "gather", wgb_all)

    w_full_t = w_in_all.reshape(N_IN, D)
    meta_full = jnp.transpose(meta_all, (1, 0, 2)).reshape(N_META, D)
    conv_full = jnp.transpose(conv_all, (1, 0, 2)).reshape(3, D)
    wgf = jnp.pad(wgf_all, ((0, 0), (0, LANES - RANK), (0, 0))).astype(BF16)
    wgb = jnp.pad(wgb_all, ((0, 0), (RANK, LANES - 2 * RANK), (0, 0))).astype(BF16)
    bgf = b_gate_fwd.reshape(N_HEADS, 1, HEAD_K)
    bgb = b_gate_bwd.reshape(N_HEADS, 1, HEAD_K)

    head = jnp.concatenate([jnp.zeros((PAD_FRONT, D), F32), meta_full], axis=0)
    tgt = loss_target.reshape(n_seq * seq, D)

    h, proj, u_t, lr = _in_proj(x.reshape(n_seq * seq, D), head, norm_pre, w_full_t, n_seq, lf)
    yc = _conv_fwd(proj, conv_full, n_seq, lf)
    o = _gla_fwd(proj, lr, wgf, wgb, bgf, bgb, n_seq, lf)
    (w_out_all,) = _plane_wait("gather_w_out_wait", w_out_state, "gather", o)
    w3 = jnp.transpose(w_out_all.reshape(4, 3, D // 4, D), (1, 0, 2, 3)).reshape(3, D, D)
    (dres, yg, merged, dout_t, dpc_t, dpg_t, dyc, d_o, dtail, loss_acc, d_gpost, d_gamma) = _tail(
        h, tgt, yc, o, proj, w3, gla_norm, norm_post, lf)
    g_w_oc = _wgrad_t(dpc_t, yc, "wgrad_out_conv")
    g_w_og = _wgrad_t(dpg_t, yg, "wgrad_out_gla")
    g_w_mo = _wgrad_t(dout_t, merged, "wgrad_merge_out")
    g_out_slots = jnp.concatenate([g.reshape(4, D // 4, D) for g in (g_w_oc, g_w_og, g_w_mo)], axis=1)
    out_state, out_token = _plane_start("scatter_out_grads_start", [g_out_slots], "scatter", g_w_mo)
    dgla, dlr, dwgf_p, dwgb_p, dbg_p = _gla_bwd(proj, lr, d_o, wgf, wgb, bgf, bgb, n_seq, lf, out_token)
    (got_out,) = _plane_wait("scatter_out_grads_wait", out_state, "scatter", dlr)
    dconv, dconvw_p = _conv_bwd(proj, conv_full, dyc, n_seq, lf)
    g_conv = _wgrad_t(u_t, dconv, "wgrad_in_conv")
    g_gla = _wgrad_t(u_t, dgla, "wgrad_in_gla")
    g_tail = _wgrad_t(u_t, dtail, "wgrad_in_tail")
    g_lr = _wgrad_t(u_t, dlr, "wgrad_in_lr")

    g_in_slots = _reference_rows(g_conv, g_gla, g_tail, g_lr).reshape(4, SHARD_IN, D)
    in_state, in_token = _plane_start("scatter_in_grads_start", [g_in_slots], "scatter", g_lr)
    dh, d_gpre = _dgrad_in(dconv, dgla, dtail, dlr, w_full_t, h, norm_pre, dres, in_token)
    (got_in,) = _plane_wait("scatter_in_grads_wait", in_state, "scatter", d_gpre)

    plane_in = _sum_slots(got_in, "sum_w_in_grads", own=g_in_slots, slot=shard_arr)
    plane_out = _sum_slots(got_out, "sum_w_out_grads", own=g_out_slots, slot=shard_arr)
    swap_state, swap_token = _plane_start("swap_plane_sums_start", [plane_in, plane_out], "swap", plane_out)

    dh3 = dh.reshape(n_seq, lf, D)
    grad_x = dh3[:, CHUNK:, :]

    d_meta = jnp.sum(dh3[:, PAD_FRONT:CHUNK, :], axis=0)
    d_convw = jnp.sum(dconvw_p, axis=0)
    d_wgf = jnp.transpose(jnp.sum(dwgf_p, axis=0)[:, 0:RANK, :], (1, 0, 2)).reshape(RANK, N_HEADS * HEAD_K)
    d_wgb = jnp.transpose(jnp.sum(dwgb_p, axis=0)[:, RANK:2 * RANK, :], (1, 0, 2)).reshape(RANK, N_HEADS * HEAD_K)
    d_bg = jnp.sum(dbg_p, axis=0)
    d_bgf = d_bg[:, 0, :].reshape(1, N_HEADS * HEAD_K)
    d_bgb = d_bg[:, 1, :].reshape(1, N_HEADS * HEAD_K)
    loss_part = loss_acc[0:1, :] + swap_token[0:1, :]
    partials = [d_meta, d_convw, d_wgf, d_wgb, d_gpre, d_bgf, d_bgb, d_gamma, d_gpost, loss_part]
    (g_meta, g_convw, g_wgf, g_wgb, g_npre, g_bgf, g_bgb, g_gnorm, g_npost, loss_row) = _sum_small(
        _gather_all("gather_small_grads", partials), "sum_small_grads")
    loss = loss_row[0, 0]
    small_out = _adamw_small(
        [(meta_tokens, g_meta, m_meta_tokens, v_meta_tokens), (norm_pre, g_npre, m_norm_pre, v_norm_pre),
         (conv_w, g_convw, m_conv_w, v_conv_w), (w_gate_fwd, g_wgf, m_w_gate_fwd, v_w_gate_fwd),
         (b_gate_fwd, g_bgf, m_b_gate_fwd, v_b_gate_fwd), (w_gate_bwd, g_wgb, m_w_gate_bwd, v_w_gate_bwd),
         (b_gate_bwd, g_bgb, m_b_gate_bwd, v_b_gate_bwd), (gla_norm, g_gnorm, m_gla_norm, v_gla_norm),
         (norm_post, g_npost, m_norm_post, v_norm_post)], shard_arr, "adamw_small")

    other_in, other_out = _plane_wait("swap_plane_sums_wait", swap_state, "swap", small_out[0][0])
    big_in = _adamw(jnp.transpose(w_in[0]), [plane_in, other_in], jnp.transpose(m_w_in[0]), jnp.transpose(v_w_in[0]),
                    "adamw_w_in")
    out_params = ((w_out_conv, m_w_out_conv, v_w_out_conv), (w_out_gla, m_w_out_gla, v_w_out_gla),
                  (w_merge_out, m_w_merge_out, v_w_merge_out))
    big_out = [_adamw(w[0], [plane_out, other_out], m[0], v[0], f"adamw_w_out_{i}", grad_row=i * (D // 4))
               for i, (w, m, v) in enumerate(out_params)]

    results = []
    for kind in range(4):
        small_kind = [p[kind] for p in small_out]
        w_in_part = jnp.transpose(big_in[kind])[None]
        outs3 = [big_out[i][kind][None] for i in range(3)]
        results.extend(small_kind[0:2] + [w_in_part] + small_kind[2:8] + outs3 + small_kind[8:9])
    return (loss, grad_x, *results)
```

```python
import functools

import jax
import jax.numpy as jnp
from jax import lax
from jax.experimental import pallas as pl
from jax.experimental.pallas import tpu as pltpu

F32 = jnp.float32
BF16 = jnp.bfloat16
MESH = pl.DeviceIdType.MESH

D = 1024
N_META = 16
CHUNK = 64
PAD_FRONT = CHUNK - N_META
N_HEADS = 4
HEAD_K = 128
HEAD_V = 256
RANK = 16
EPS = 1e-6
GATE_NORM = 16.0
N_IN = 9248
SHARD_IN = N_IN // 4
LANES = 128
N_CONV_TILES = 8
W_CONV = 4096
W_GLA = 2048
W_TAIL = 3072
N_MAIN = W_CONV + W_GLA + W_TAIL
OFF_Q, OFF_K, OFF_V, OFF_R = 4096, 4608, 5120, 6144
OFF_LR, OFF_MA = 7168, 7200
MIB = 1024 * 1024

ADAM_LR = 0.001
ADAM_B1 = 0.9
ADAM_B2 = 0.999
ADAM_EPS = 1e-08
ADAM_WD = 0.01
ADAM_STEP = 10


def _params(sem=None, vmem_mib=None):
    return pltpu.CompilerParams(
        dimension_semantics=sem,
        vmem_limit_bytes=None if vmem_mib is None else vmem_mib * MIB)


def _pick_tile(n, target, mult):
    best = None
    for t in range(mult, min(n, target) + 1, mult):
        if n % t == 0:
            best = t
    return n if best is None else best


def _sigmoid(v):
    return 1.0 / (1.0 + jnp.exp(-v))


def _log_sigmoid(v):
    return jnp.minimum(v, 0.0) - jnp.log(1.0 + jnp.exp(-jnp.abs(v)))


def _dot(a, b):
    return jnp.dot(a, b, preferred_element_type=F32)


def _dot_nt(a, b):
    return lax.dot_general(a, b, (((1,), (1,)), ((), ())), preferred_element_type=F32)


def _dot_tn(a, b):
    return lax.dot_general(a, b, (((0,), (0,)), ((), ())), preferred_element_type=F32)


def _gather_all(name, arrs):
    n = len(arrs)
    flips = tuple((m >> 2 & 1, m >> 1 & 1, m & 1) for m in range(1, 8))

    def body(*refs):
        ins, outs = refs[:n], refs[n:2 * n]
        send_sems, recv_sems, local_sems = refs[2 * n:]
        pos = (lax.axis_index("x"), lax.axis_index("y"), lax.axis_index("c"))

        def slot_of(p):
            return 4 * p[0] + 2 * p[1] + p[2]

        peers = [tuple(1 - pos[a] if f[a] else pos[a] for a in range(3)) for f in flips]
        me = slot_of(pos)
        copies = []
        for i in range(n):
            cp = pltpu.make_async_copy(ins[i], outs[i].at[me], local_sems.at[i])
            cp.start()
            copies.append(cp)
        sends = []
        for i in range(n):
            for k, peer in enumerate(peers):
                cp = pltpu.make_async_remote_copy(
                    src_ref=ins[i], dst_ref=outs[i].at[me], send_sem=send_sems.at[i, k], recv_sem=recv_sems.at[i, k],
                    device_id=peer, device_id_type=MESH)
                cp.start()
                sends.append(cp)
        for i in range(n):
            for k, peer in enumerate(peers):
                pltpu.make_async_remote_copy(
                    src_ref=ins[i], dst_ref=outs[i].at[slot_of(peer)], send_sem=send_sems.at[i, k],
                    recv_sem=recv_sems.at[i, k], device_id=peer, device_id_type=MESH).wait_recv()
        for cp in sends:
            cp.wait_send()
        for cp in copies:
            cp.wait()

    hbm = pl.BlockSpec(memory_space=pl.ANY)
    outs = pl.pallas_call(
        body, name=name, out_shape=[jax.ShapeDtypeStruct((8,) + a.shape, a.dtype) for a in arrs],
        in_specs=[hbm] * n, out_specs=[hbm] * n,
        scratch_shapes=[pltpu.SemaphoreType.DMA((n, 7)), pltpu.SemaphoreType.DMA((n, 7)),
                        pltpu.SemaphoreType.DMA((n,))],
        compiler_params=pltpu.CompilerParams(has_side_effects=True),
    )(*arrs)
    return list(outs)


def _gather_via_sibling(name, arrs, slotted):
    n = len(arrs)
    out_shape = [jax.ShapeDtypeStruct(a.shape if slotted[i] else (4,) + a.shape, a.dtype)
                 for i, a in enumerate(arrs)]

    def body(*refs):
        ins, outs = refs[:n], refs[n:2 * n]
        send_sems, recv_sems, local_sems = refs[2 * n:]
        x, y, c = lax.axis_index("x"), lax.axis_index("y"), lax.axis_index("c")
        me = 2 * x + y
        chips = [(1 - x, y), (x, 1 - y), (1 - x, 1 - y)]

        def half(ref, which):
            rows = ref.shape[0]
            cut = rows // 2 // 16 * 16
            return ref.at[pl.ds(0, cut)] if which == 0 else ref.at[pl.ds(cut, rows - cut)]

        def copy(src, dst, i, k, to):
            return pltpu.make_async_remote_copy(
                src_ref=src, dst_ref=dst, send_sem=send_sems.at[i, k], recv_sem=recv_sems.at[i, k],
                device_id=to, device_id_type=MESH)

        def run(mine):
            other = 1 - mine
            local, sends = [], []
            whole = [(not slotted[i]) and arrs[i].shape[0] < 32 for i in range(n)]
            for i in range(n):
                own = outs[i].at[me] if slotted[i] else ins[i]
                if not slotted[i]:
                    cp = pltpu.make_async_copy(ins[i], outs[i].at[me], local_sems.at[i])
                    cp.start()
                    local.append(cp)
                for k, (px, py) in enumerate(chips):
                    if whole[i]:
                        cp = copy(own, outs[i].at[me], i, k, (px, py, mine))
                    elif k < 2:
                        cp = copy(half(own, mine), half(outs[i].at[me], mine), i, k, (px, py, mine))
                    else:
                        continue
                    cp.start()
                    sends.append(cp)
            via = mine
            for k in (via, 1 - via, 2):
                px, py = chips[k]
                slot = 2 * px + py
                source = (px, py, mine) if k < 2 else chips[1 - via] + (mine,)
                for i in range(n):
                    if whole[i]:
                        copy(outs[i].at[slot], outs[i].at[slot], i, k, (px, py, mine)).wait_recv()
                        continue
                    landed = half(outs[i].at[slot], mine)
                    copy(landed, landed, i, k, source).wait_recv()
                    if k == via:
                        cp = copy(landed, landed, i, 2, chips[1 - via] + (mine,))
                        cp.start()
                        sends.append(cp)
                    cp = copy(landed, landed, i, 3 + k, (x, y, other))
                    cp.start()
                    sends.append(cp)
            for k, (px, py) in enumerate(chips):
                slot = 2 * px + py
                for i in range(n):
                    if whole[i]:
                        continue
                    passed = half(outs[i].at[slot], other)
                    copy(passed, passed, i, 3 + k, (x, y, other)).wait_recv()
            for cp in sends:
                cp.wait_send()
            for cp in local:
                cp.wait()

        for mine in (0, 1):
            pl.when(c == mine)(functools.partial(run, mine))

    hbm = pl.BlockSpec(memory_space=pl.ANY)
    outs = pl.pallas_call(
        body, name=name, out_shape=out_shape,
        in_specs=[hbm] * n, out_specs=[hbm] * n,
        scratch_shapes=[pltpu.SemaphoreType.DMA((n, 6)), pltpu.SemaphoreType.DMA((n, 6)),
                        pltpu.SemaphoreType.DMA((n,))],
        input_output_aliases={i: i for i in range(n) if slotted[i]},
        compiler_params=pltpu.CompilerParams(has_side_effects=True),
    )(*arrs)
    return list(outs)


HBM_SPEC = pl.BlockSpec(memory_space=pltpu.HBM)
SEM_SPEC = pl.BlockSpec(memory_space=pltpu.SEMAPHORE)
DATAFLOW = pltpu.SideEffectType.DATAFLOW_SIDE_EFFECTING


def _split_peers(mode):
    x, y, c = lax.axis_index("x"), lax.axis_index("y"), lax.axis_index("c")
    if mode == "swap":
        return 0, [((x, y, 1 - c), 0)]
    return 2 * x + y, [((1 - x, y, c), 2 * (1 - x) + y), ((x, 1 - y, c), 2 * x + 1 - y),
                       ((1 - x, 1 - y, c), 2 * (1 - x) + 1 - y)]


def _split_refs(mode, src, landing, me, peer_slot):
    if mode == "gather":
        return src.at[me], landing.at[me]
    if mode == "scatter":
        return src.at[peer_slot], landing.at[me]
    return src, landing


def _plane_start(name, arrs, mode, after):
    n = len(arrs)
    n_peers = 1 if mode == "swap" else 3
    if mode == "gather":
        srcs, lands = [], list(arrs)
    else:
        srcs, lands = list(arrs), [lax.empty(a.shape, a.dtype) for a in arrs]
    n_src = len(srcs)

    def body(*refs):
        landing = refs[n_src:n_src + n]
        sources = refs[:n_src] if n_src else landing
        send_sems, recv_sems = refs[n_src + n + 1], refs[n_src + n + 2]
        token = refs[-1]
        me, peers = _split_peers(mode)
        for i in range(n):
            for k, (peer, peer_slot) in enumerate(peers):
                src, dst = _split_refs(mode, sources[i], landing[i], me, peer_slot)
                pltpu.make_async_remote_copy(
                    src_ref=src, dst_ref=dst, send_sem=send_sems.at[n_peers * i + k],
                    recv_sem=recv_sems.at[n_peers * i + k], device_id=peer, device_id_type=MESH).start()
        token[...] = jnp.zeros_like(token)

    hbm_in = [pltpu.with_memory_space_constraint(a, pltpu.HBM) for a in srcs + lands]
    out = pl.pallas_call(
        body, name=name,
        out_shape=[pltpu.SemaphoreType.DMA((n_peers * n,)), pltpu.SemaphoreType.DMA((n_peers * n,))]
                  + [pltpu.HBM(a.shape, a.dtype) for a in lands]
                  + [jax.ShapeDtypeStruct((8, LANES), F32)],
        in_specs=[HBM_SPEC] * (n_src + n) + [pl.BlockSpec(memory_space=pl.ANY)],
        out_specs=[SEM_SPEC, SEM_SPEC] + [HBM_SPEC] * n + [pl.BlockSpec(memory_space=pltpu.VMEM)],
        input_output_aliases={n_src + i: 2 + i for i in range(n)},
        compiler_params=pltpu.CompilerParams(has_side_effects=DATAFLOW),
    )(*hbm_in, after)
    return out[:-1], out[-1]


def _plane_wait(name, state, mode, after):
    send_sems, recv_sems = state[0], state[1]
    bufs = list(state[2:])
    n = len(bufs)
    n_peers = 1 if mode == "swap" else 3

    def body(*refs):
        landing = refs[:n]
        send_sems, recv_sems = refs[n], refs[n + 1]
        _, peers = _split_peers(mode)
        for i in range(n):
            for k, (peer, peer_slot) in enumerate(peers):
                arrived = landing[i] if mode == "swap" else landing[i].at[peer_slot]
                cp = pltpu.make_async_remote_copy(
                    src_ref=arrived, dst_ref=arrived, send_sem=send_sems.at[n_peers * i + k],
                    recv_sem=recv_sems.at[n_peers * i + k], device_id=peer, device_id_type=MESH)
                cp.wait_send()
                cp.wait_recv()

    out = pl.pallas_call(
        body, name=name,
        out_shape=[pltpu.HBM(a.shape, a.dtype) for a in bufs],
        in_specs=[HBM_SPEC] * n + [SEM_SPEC, SEM_SPEC, pl.BlockSpec(memory_space=pl.ANY)],
        out_specs=[HBM_SPEC] * n,
        input_output_aliases={i: i for i in range(n)},
        compiler_params=pltpu.CompilerParams(has_side_effects=DATAFLOW),
    )(*bufs, send_sems, recv_sems, after)
    return list(out)


def _tile_2d(rows, cols, row_mult, max_elems=512 * 1024):
    if rows % row_mult == 0:
        rt = _pick_tile(rows, max(row_mult, max_elems // cols), row_mult)
        return (rt, cols), rows // rt, lambda i: (i, 0)
    ct = _pick_tile(cols, max(LANES, max_elems // rows), LANES)
    return (rows, ct), cols // ct, lambda i: (0, i)


def _cast_into_slot(a, slot, name):
    block, steps, index = _tile_2d(a.shape[0], a.shape[1], 16)

    def body(slot_ref, a_ref, o_ref):
        o_ref[...] = a_ref[...].astype(BF16)

    return pl.pallas_call(
        body, name=name,
        grid_spec=pltpu.PrefetchScalarGridSpec(
            num_scalar_prefetch=1, grid=(steps,),
            in_specs=[pl.BlockSpec(block, lambda i, s: index(i))],
            out_specs=pl.BlockSpec((None,) + block, lambda i, s: (s[0],) + index(i))),
        out_shape=jax.ShapeDtypeStruct((4,) + a.shape, BF16),
        compiler_params=_params(("arbitrary",)),
    )(slot, a)


def _sum_slots(buf, name, own=None, slot=None):
    n_slots, rows, cols = buf.shape
    (br, bc), steps, index = _tile_2d(rows, cols, 16, 320 * 1024)

    def body(*refs):
        if own is None:
            b_ref, o_ref = refs
        else:
            slot_ref, b_ref, own_ref, o_ref = refs
        acc = None
        for s in range(n_slots):
            term = b_ref[s] if own is None else jnp.where(slot_ref[0] == s, own_ref[...], b_ref[s])
            acc = term.astype(F32) if acc is None else acc + term.astype(F32)
        o_ref[...] = acc

    out_shape = jax.ShapeDtypeStruct((rows, cols), F32)
    if own is None:
        return pl.pallas_call(
            body, name=name, grid=(steps,),
            in_specs=[pl.BlockSpec((n_slots, br, bc), lambda i: (0,) + index(i))],
            out_specs=pl.BlockSpec((br, bc), index), out_shape=out_shape,
            compiler_params=_params(("parallel",), 48),
        )(buf)
    return pl.pallas_call(
        body, name=name,
        grid_spec=pltpu.PrefetchScalarGridSpec(
            num_scalar_prefetch=1, grid=(steps,),
            in_specs=[pl.BlockSpec((n_slots, br, bc), lambda i, s: (0,) + index(i)),
                      pl.BlockSpec((None, br, bc), lambda i, s: (s[0],) + index(i))],
            out_specs=pl.BlockSpec((br, bc), lambda i, s: index(i))),
        out_shape=out_shape,
        compiler_params=_params(("arbitrary",), 48),
    )(slot, buf, own)


def _sum_small(bufs, name):
    n = len(bufs)

    def body(*refs):
        for b_ref, o_ref in zip(refs[:n], refs[n:]):
            acc = b_ref[0]
            for s in range(1, b_ref.shape[0]):
                acc = acc + b_ref[s]
            o_ref[...] = acc

    vmem = pl.BlockSpec(memory_space=pltpu.VMEM)
    return pl.pallas_call(
        body, name=name, in_specs=[vmem] * n, out_specs=[vmem] * n,
        out_shape=[jax.ShapeDtypeStruct(b.shape[1:], b.dtype) for b in bufs],
    )(*bufs)


def _adam_update(w, g, m, v):
    c1 = 1.0 - ADAM_B1 ** ADAM_STEP
    c2 = 1.0 - ADAM_B2 ** ADAM_STEP
    m_new = ADAM_B1 * m + (1.0 - ADAM_B1) * g
    v_new = ADAM_B2 * v + (1.0 - ADAM_B2) * (g * g)
    m_hat = m_new / c1
    v_hat = v_new / c2
    return -ADAM_LR * (m_hat / (jnp.sqrt(v_hat) + ADAM_EPS) + ADAM_WD * w), m_new, v_new


def _adamw_small(params, slot, name):
    n = len(params)

    def spec_of(shape):
        lead = (None,) * (len(shape) - 2)
        return pl.BlockSpec(lead + tuple(shape[-2:]), lambda i, s, k=len(shape): (0,) * k)

    in_specs, operands, out_specs, out_shape = [], [], [], []
    for w, g, m, v in params:
        shard = g.shape[-1] != w.shape[-1]
        g_spec = pl.BlockSpec(tuple(w.shape[-2:]), (lambda i, s: (0, s[0])) if shard else (lambda i, s: (0, 0)))
        in_specs += [spec_of(w.shape), g_spec, spec_of(m.shape), spec_of(v.shape)]
        operands += [w, g, m, v]
        out_specs += [spec_of(w.shape)] * 4
        out_shape += [jax.ShapeDtypeStruct(w.shape, F32)] * 4

    def body(slot_ref, *refs):
        ins, outs = refs[:4 * n], refs[4 * n:]
        for p in range(n):
            w_ref, g_ref, m_ref, v_ref = ins[4 * p:4 * p + 4]
            g = g_ref[...]
            delta, m_new, v_new = _adam_update(w_ref[...], g, m_ref[...], v_ref[...])
            for o_ref, val in zip(outs[4 * p:4 * p + 4], (g, delta, m_new, v_new)):
                o_ref[...] = val

    out = pl.pallas_call(
        body, name=name,
        grid_spec=pltpu.PrefetchScalarGridSpec(num_scalar_prefetch=1, grid=(1,), in_specs=in_specs, out_specs=out_specs),
        out_shape=out_shape,
    )(slot, *operands)
    return [tuple(out[4 * p:4 * p + 4]) for p in range(n)]


def _adamw(w, grads, m, v, name, grad_row=0):
    rows, cols = w.shape
    (rt, _), _, _ = _tile_2d(rows, cols, 8, 160 * 1024)
    assert grad_row % rt == 0
    n_g = len(grads)

    def body(*refs):
        w_ref = refs[0]
        g_refs = refs[1:1 + n_g]
        m_ref, v_ref, g_out, d_out, m_out, v_out = refs[1 + n_g:]
        g = g_refs[0][...]
        for r in g_refs[1:]:
            g = g + r[...]
        g_out[...] = g
        d_out[...], m_out[...], v_out[...] = _adam_update(w_ref[...], g, m_ref[...], v_ref[...])

    spec = pl.BlockSpec((rt, cols), lambda i: (i, 0))
    grad_spec = pl.BlockSpec((rt, cols), lambda i: (i + grad_row // rt, 0))
    shape = jax.ShapeDtypeStruct((rows, cols), F32)
    return pl.pallas_call(
        body, name=name, grid=(rows // rt,),
        in_specs=[spec] + [grad_spec] * n_g + [spec] * 2, out_specs=[spec] * 4, out_shape=[shape] * 4,
        compiler_params=_params(("parallel",), 48),
    )(w, *grads, m, v)


def _weight_pieces():
    pieces = []
    for j in range(N_CONV_TILES):
        for g in range(4):
            pieces.append((512 * j + 128 * g, D * g + 128 * j, 128))
    for hd in range(N_HEADS):
        base = W_CONV + 512 * hd
        pieces.append((base, OFF_Q + HEAD_K * hd, HEAD_K))
        pieces.append((base + HEAD_K, OFF_K + HEAD_K * hd, HEAD_K))
        pieces.append((base + 2 * HEAD_K, OFF_V + HEAD_V * hd, HEAD_V))
    pieces.append((W_CONV + W_GLA, OFF_R, D))
    pieces.append((W_CONV + W_GLA + D, OFF_MA, 2 * D))
    return pieces


N_WEIGHT_COPIES = len(_weight_pieces()) + 1


def _load_weights(w_hbm, w_s, wlr_s, sems):
    copies = [pltpu.make_async_copy(w_hbm.at[pl.ds(src, n)], w_s.at[pl.ds(dst, n)], sems.at[i])
              for i, (dst, src, n) in enumerate(_weight_pieces())]
    copies.append(pltpu.make_async_copy(w_hbm.at[pl.ds(OFF_LR, LANES)], wlr_s, sems.at[N_WEIGHT_COPIES - 1]))
    for cp in copies:
        cp.start()
    for cp in copies:
        cp.wait()


def _in_proj(x_rows, head, g_pre, w_full_t, n_seq, lf):
    t_rows = n_seq * lf
    tm = _pick_tile(t_rows, 384, LANES)
    n_main = N_MAIN
    n_chunks = lf // CHUNK
    per_tile = tm // CHUNK

    def body(*refs):
        x_refs = refs[:per_tile]
        head_ref, g_ref, w_hbm, h_ref, proj_ref, ut_ref, lr_ref, w_s, wlr_s, w_sems = refs[per_tile:]
        i = pl.program_id(0)

        @pl.when(i == 0)
        def _():
            _load_weights(w_hbm, w_s, wlr_s, w_sems)

        rowi = lax.broadcasted_iota(jnp.int32, (tm, 1), 0)
        is_tok_row = jnp.zeros((tm, 1), F32)
        for kk in range(per_tile):
            f = jnp.where(((i * per_tile + kk) % n_chunks) != 0, 1.0, 0.0)
            is_tok_row = jnp.where((rowi >= kk * CHUNK) & (rowi < (kk + 1) * CHUNK), f, is_tok_row)
        tokens = jnp.concatenate([x_ref[...] for x_ref in x_refs], axis=0)
        heads = jnp.concatenate([head_ref[...]] * per_tile, axis=0)
        hh = jnp.where(is_tok_row > 0.0, tokens, heads)
        h_ref[...] = hh
        rstd = lax.rsqrt(jnp.mean(hh * hh, axis=-1, keepdims=True) + EPS)
        uf = hh * rstd * g_ref[...]
        u = uf.astype(BF16)
        ut_ref[...] = jnp.transpose(uf).astype(BF16)
        lr_ref[...] = _dot_nt(u, wlr_s[...])
        for j in range(n_main // D):
            cols = slice(j * D, (j + 1) * D)
            proj_ref[:, cols] = _dot_nt(u, w_s[cols, :]).astype(BF16)

    def token_chunk(kk):
        def index(i):
            q = i * per_tile + kk
            return (q // n_chunks) * (n_chunks - 1) + jnp.maximum(q % n_chunks - 1, 0), 0
        return pl.BlockSpec((CHUNK, D), index)

    return pl.pallas_call(
        body, name="in_proj", grid=(t_rows // tm,),
        in_specs=[token_chunk(kk) for kk in range(per_tile)]
                 + [pl.BlockSpec((CHUNK, D), lambda i: (0, 0)),
                    pl.BlockSpec((1, D), lambda i: (0, 0)),
                    pl.BlockSpec(memory_space=pl.ANY)],
        out_specs=[pl.BlockSpec((tm, D), lambda i: (i, 0)),
                   pl.BlockSpec((tm, n_main), lambda i: (i, 0)),
                   pl.BlockSpec((D, tm), lambda i: (0, i)),
                   pl.BlockSpec((tm, LANES), lambda i: (i, 0))],
        out_shape=[jax.ShapeDtypeStruct((t_rows, D), F32),
                   jax.ShapeDtypeStruct((t_rows, n_main), BF16),
                   jax.ShapeDtypeStruct((D, t_rows), BF16),
                   jax.ShapeDtypeStruct((t_rows, LANES), F32)],
        scratch_shapes=[pltpu.VMEM((n_main, D), BF16), pltpu.VMEM((LANES, D), BF16),
                        pltpu.SemaphoreType.DMA((N_WEIGHT_COPIES,))],
        compiler_params=_params(("arbitrary",), 56),
    )(*[x_rows] * per_tile, head, g_pre, w_full_t)


CONV_TILES_PER_STEP = 4


def _conv_parts(p_ref, w_ref, t):
    cb = p_ref[:, 512 * t:512 * t + 128].astype(F32)
    cc = p_ref[:, 512 * t + 128:512 * t + 256].astype(F32)
    cx = p_ref[:, 512 * t + 256:512 * t + 384].astype(F32)
    cz = p_ref[:, 512 * t + 384:512 * t + 512].astype(F32)
    rows = cb.shape[0]
    w = w_ref[:, 128 * t:128 * (t + 1)]
    p = cc * cx
    conv = pltpu.roll(p, 1, 0) * w[0:1] + p * w[1:2] + pltpu.roll(p, rows - 1, 0) * w[2:3]
    sz = _sigmoid(cz)
    return cb, cc, cx, cz, p, conv, sz, w


def _conv_fwd(proj, conv_w, n_seq, lf):
    per = CONV_TILES_PER_STEP

    def body(p_ref, w_ref, y_ref):
        for t in range(per):
            cb, _, _, cz, _, conv, sz, _ = _conv_parts(p_ref, w_ref, t)
            y_ref[:, 128 * t:128 * (t + 1)] = (cb * conv * (cz * sz)).astype(BF16)

    return pl.pallas_call(
        body, name="conv_fwd", grid=(n_seq, N_CONV_TILES // per),
        in_specs=[pl.BlockSpec((lf, 512 * per), lambda b, j: (b, j)),
                  pl.BlockSpec((3, 128 * per), lambda b, j: (0, j))],
        out_specs=pl.BlockSpec((lf, 128 * per), lambda b, j: (b, j)),
        out_shape=jax.ShapeDtypeStruct((n_seq * lf, D), BF16),
        compiler_params=_params(("parallel", "parallel"), 48),
    )(proj, conv_w)


def _conv_bwd(proj, conv_w, dyc, n_seq, lf):
    per = CONV_TILES_PER_STEP

    def body(p_ref, w_ref, dy_ref, dp_ref, dw_ref):
        for t in range(per):
            cb, cc, cx, cz, p, conv, sz, w = _conv_parts(p_ref, w_ref, t)
            rows = cb.shape[0]
            dy = dy_ref[:, 128 * t:128 * (t + 1)].astype(F32)
            silu = cz * sz
            dcb = dy * conv * silu
            dconv = dy * cb * silu
            dcz = dy * cb * conv * (sz * (1.0 + cz * (1.0 - sz)))
            d_next = pltpu.roll(dconv, rows - 1, 0)
            d_prev = pltpu.roll(dconv, 1, 0)
            dp = d_next * w[0:1] + dconv * w[1:2] + d_prev * w[2:3]
            base = 512 * t
            dp_ref[:, base:base + 128] = dcb.astype(BF16)
            dp_ref[:, base + 128:base + 256] = (dp * cx).astype(BF16)
            dp_ref[:, base + 256:base + 384] = (dp * cc).astype(BF16)
            dp_ref[:, base + 384:base + 512] = dcz.astype(BF16)
            lanes = slice(128 * t, 128 * (t + 1))
            dw_ref[0:1, lanes] = jnp.sum(dconv * pltpu.roll(p, 1, 0), axis=0, keepdims=True)
            dw_ref[1:2, lanes] = jnp.sum(dconv * p, axis=0, keepdims=True)
            dw_ref[2:3, lanes] = jnp.sum(dconv * pltpu.roll(p, rows - 1, 0), axis=0, keepdims=True)

    return pl.pallas_call(
        body, name="conv_bwd", grid=(n_seq, N_CONV_TILES // per),
        in_specs=[pl.BlockSpec((lf, 512 * per), lambda b, j: (b, j)),
                  pl.BlockSpec((3, 128 * per), lambda b, j: (0, j)),
                  pl.BlockSpec((lf, 128 * per), lambda b, j: (b, j))],
        out_specs=[pl.BlockSpec((lf, 512 * per), lambda b, j: (b, j)),
                   pl.BlockSpec((None, 3, 128 * per), lambda b, j: (b, 0, j))],
        out_shape=[jax.ShapeDtypeStruct((n_seq * lf, W_CONV), BF16),
                   jax.ShapeDtypeStruct((n_seq, 3, D), F32)],
        compiler_params=_params(("parallel", "parallel"), 48),
    )(proj, conv_w, dyc)


GROUP = 3
GROUP_ROWS = GROUP * CHUNK


def _row_group(shape):
    row = lax.broadcasted_iota(jnp.int32, shape, 0)
    grp = jnp.zeros(shape, jnp.int32)
    for r in range(1, GROUP):
        grp = grp + (row >= r * CHUNK).astype(jnp.int32)
    return grp


def _lane_group(shape, width):
    lane = lax.broadcasted_iota(jnp.int32, shape, 1)
    grp = jnp.zeros(shape, jnp.int32)
    for r in range(1, GROUP):
        grp = grp + (lane >= r * width).astype(jnp.int32)
    return grp


def _score_mask(direction):
    shape = (GROUP_ROWS, GROUP_ROWS)
    row = lax.broadcasted_iota(jnp.int32, shape, 0)
    col = lax.broadcasted_iota(jnp.int32, shape, 1)
    same = _row_group(shape) == _lane_group(shape, CHUNK)
    return same & ((col <= row) if direction == 0 else (col > row))


def _diag_blocks(v):
    w = v.shape[1]
    wide = jnp.concatenate([v] * GROUP, axis=1)
    return jnp.where(_row_group(wide.shape) == _lane_group(wide.shape, w), wide, jnp.zeros_like(wide))


def _per_chunk_dot(lhs, state, transposed):
    outs = []
    for r in range(GROUP):
        rows = lhs[r * CHUNK:(r + 1) * CHUNK, :]
        blk = state[:, r * HEAD_K:(r + 1) * HEAD_K]
        outs.append(_dot_nt(rows, blk) if transposed else _dot(rows, blk))
    return jnp.concatenate(outs, axis=0)


def _chunk_cumsum(v, suffix):
    pos = lax.broadcasted_iota(jnp.int32, v.shape, 0) & (CHUNK - 1)
    shift = 1
    while shift < CHUNK:
        if suffix:
            moved = pltpu.roll(v, GROUP_ROWS - shift, 0)
            v = v + jnp.where(pos < CHUNK - shift, moved, 0.0)
        else:
            moved = pltpu.roll(v, shift, 0)
            v = v + jnp.where(pos >= shift, moved, 0.0)
        shift *= 2
    return v


def _per_chunk_rows(rows_of_chunk):
    w = rows_of_chunk[0].shape[1]
    return jnp.concatenate([jnp.broadcast_to(v, (CHUNK, w)) for v in rows_of_chunk], axis=0)


def _chunk_end_rows(direction, b):
    at = CHUNK - 1 if direction == 0 else 0
    return [b[r * CHUNK + at:r * CHUNK + at + 1, :] for r in range(GROUP)]


def _gla_gates(lr_bf, wg_ref, bg_ref, lf):
    z = _dot(lr_bf, wg_ref[...]) + bg_ref[...]
    valid = lax.broadcasted_iota(jnp.int32, (lf, HEAD_K), 0) >= PAD_FRONT
    return z, valid


def _group_unroll(n_groups):
    return n_groups if n_groups <= 11 else 1


def _group_rows(g):
    return pl.ds(pl.multiple_of(g * GROUP_ROWS, GROUP_ROWS), GROUP_ROWS)


def _chunk_decay(direction, g, r, b_s):
    base = g * GROUP_ROWS + r * CHUNK
    if direction == 0:
        grp = b_s[pl.ds(pl.multiple_of(base + CHUNK - 8, 8), 8), :]
        return jnp.exp(grp[7:8, :])
    grp = b_s[pl.ds(pl.multiple_of(base, 8), 8), :]
    return jnp.exp(grp[0:1, :])


def _state_scan(direction, n_groups, b_s, st_s, reverse):
    ascending = (direction == 0) != reverse

    def step(i, carry):
        g = i if ascending else n_groups - 1 - i
        for rr in range(GROUP):
            r = rr if ascending else GROUP - 1 - rr
            lanes = slice(r * HEAD_K, (r + 1) * HEAD_K)
            decay = _chunk_decay(direction, g, r, b_s)
            local = st_s[g, :, lanes]
            st_s[g, :, lanes] = carry
            carry = (local + carry * decay) if reverse else (carry * decay + local)
        return carry

    lax.fori_loop(0, n_groups, step, jnp.zeros((HEAD_V, HEAD_K), F32), unroll=_group_unroll(n_groups))


def _gla_states(direction, n_groups, qkv_ref, g_s, b_s, st_s):
    def local(g, carry):
        rows = _group_rows(g)
        b = _chunk_cumsum(g_s[rows, :], direction == 1)
        b_s[rows, :] = b
        b_end = _per_chunk_rows(_chunk_end_rows(direction, b))
        k = qkv_ref[rows, 128:256].astype(F32)
        v = qkv_ref[rows, 256:512]
        k_dec = (k * jnp.exp(b_end - b)).astype(BF16)
        st_s[g] = _dot_tn(v, _diag_blocks(k_dec))
        return carry

    lax.fori_loop(0, n_groups, local, 0, unroll=_group_unroll(n_groups))
    _state_scan(direction, n_groups, b_s, st_s, False)


def _gla_fwd(proj, lr, wgf, wgb, bgf, bgb, n_seq, lf):
    assert lf % GROUP_ROWS == 0
    n_groups = lf // GROUP_ROWS
    scale = HEAD_K ** -0.5

    def body(qkv_ref, lr_ref, wgf_ref, wgb_ref, bgf_ref, bgb_ref, o_ref, g_s, b_s2, st_s2):
        lr_bf = lr_ref[...].astype(BF16)
        for direction in (0, 1):
            wg_ref, bg_ref = ((wgf_ref, bgf_ref), (wgb_ref, bgb_ref))[direction]
            z, valid = _gla_gates(lr_bf, wg_ref, bg_ref, lf)
            g_s[...] = jnp.where(valid, _log_sigmoid(z) / GATE_NORM, 0.0)
            _gla_states(direction, n_groups, qkv_ref, g_s, b_s2.at[direction], st_s2.at[direction])
        masks = [_score_mask(0), _score_mask(1)]

        def out(g, carry):
            rows = _group_rows(g)
            q = qkv_ref[rows, 0:128].astype(F32) * scale
            k = qkv_ref[rows, 128:256].astype(F32)
            v = qkv_ref[rows, 256:512]
            o = None
            for direction in (0, 1):
                b = b_s2[direction, rows, :]
                q_in = (q * jnp.exp(b)).astype(BF16)
                k_in = (k * jnp.exp(-b)).astype(BF16)
                s = jnp.where(masks[direction], _dot_nt(q_in, k_in), 0.0).astype(BF16)
                part = _dot(s, v) + _per_chunk_dot(q_in, st_s2[direction, g].astype(BF16), True)
                o = part if o is None else o + part
            o_ref[rows, :] = o
            return carry

        lax.fori_loop(0, n_groups, out, 0, unroll=_group_unroll(n_groups))

    return pl.pallas_call(
        body, name="gla_fwd", grid=(n_seq, N_HEADS),
        in_specs=[pl.BlockSpec((lf, 512), lambda b, h: (b, N_CONV_TILES + h)),
                  pl.BlockSpec((lf, LANES), lambda b, h: (b, 0)),
                  pl.BlockSpec((None, LANES, HEAD_K), lambda b, h: (h, 0, 0)),
                  pl.BlockSpec((None, LANES, HEAD_K), lambda b, h: (h, 0, 0)),
                  pl.BlockSpec((None, 1, HEAD_K), lambda b, h: (h, 0, 0)),
                  pl.BlockSpec((None, 1, HEAD_K), lambda b, h: (h, 0, 0))],
        out_specs=pl.BlockSpec((lf, HEAD_V), lambda b, h: (b, h)),
        out_shape=jax.ShapeDtypeStruct((n_seq * lf, D), F32),
        scratch_shapes=[pltpu.VMEM((lf, HEAD_K), F32), pltpu.VMEM((2, lf, HEAD_K), F32),
                        pltpu.VMEM((2, n_groups, HEAD_V, GROUP * HEAD_K), F32)],
        compiler_params=_params(("parallel", "parallel"), 48),
    )(proj, lr, wgf, wgb, bgf, bgb)


def _gla_bwd(proj, lr, d_o, wgf, wgb, bgf, bgb, n_seq, lf, token):
    assert lf % GROUP_ROWS == 0
    n_groups = lf // GROUP_ROWS
    scale = HEAD_K ** -0.5

    def body(qkv_ref, lr_ref, do_ref, wgf_ref, wgb_ref, bgf_ref, bgb_ref, token_ref,
             dqkv_ref, dlr_ref, dwgf_ref, dwgb_ref, dbg_ref,
             g_s, b_s2, fac_s2, dg_s2, st_s2, dst_s2):
        lr_bf = lr_ref[...].astype(BF16)
        gates = ((wgf_ref, bgf_ref), (wgb_ref, bgb_ref))
        for direction in (0, 1):
            wg_ref, bg_ref = gates[direction]
            b_s, st_s, dst_s = b_s2.at[direction], st_s2.at[direction], dst_s2.at[direction]
            z, valid = _gla_gates(lr_bf, wg_ref, bg_ref, lf)
            g_s[...] = jnp.where(valid, _log_sigmoid(z) / GATE_NORM, 0.0)
            fac_s2[direction] = jnp.where(valid, _sigmoid(-z) / GATE_NORM, 0.0)
            _gla_states(direction, n_groups, qkv_ref, g_s, b_s, st_s)

            def state_grad_local(g, carry):
                rows = _group_rows(g)
                q = qkv_ref[rows, 0:128].astype(F32) * scale
                q_in = (q * jnp.exp(b_s[rows, :])).astype(BF16)
                dst_s[g] = _dot_tn(do_ref[rows, :], _diag_blocks(q_in))
                return carry

            lax.fori_loop(0, n_groups, state_grad_local, 0, unroll=_group_unroll(n_groups))
            _state_scan(direction, n_groups, b_s, dst_s, True)

        masks = [_score_mask(0), _score_mask(1)]

        def group_grads(g, carry):
            rows = _group_rows(g)
            q = qkv_ref[rows, 0:128].astype(F32) * scale
            k = qkv_ref[rows, 128:256].astype(F32)
            v = qkv_ref[rows, 256:512]
            d_out = do_ref[rows, :]
            dq_sum = dk_sum = dv_sum = None
            for direction in (0, 1):
                end_row = CHUNK - 1 if direction == 0 else 0
                b = b_s2[direction, rows, :]
                ends = _chunk_end_rows(direction, b)
                b_end = _per_chunk_rows(ends)
                e_pos = jnp.exp(b)
                e_neg = jnp.exp(-b)
                e_end = jnp.exp(b_end - b)
                q_in = q * e_pos
                k_in = k * e_neg
                k_dec = k * e_end
                q_in_bf = q_in.astype(BF16)
                k_in_bf = k_in.astype(BF16)
                state = st_s2[direction, g]
                d_state = dst_s2[direction, g]
                state_bf = state.astype(BF16)
                d_state_bf = d_state.astype(BF16)
                s = jnp.where(masks[direction], _dot_nt(q_in_bf, k_in_bf), 0.0).astype(BF16)
                ds = jnp.where(masks[direction], _dot_nt(d_out, v), 0.0).astype(BF16)
                dv = _dot_tn(s, d_out) + _per_chunk_dot(k_dec.astype(BF16), d_state_bf, True)
                dq_in = _dot(ds, k_in_bf) + _per_chunk_dot(d_out, state_bf, False)
                dk_in = _dot_tn(ds, q_in_bf)
                dk_dec = _per_chunk_dot(v, d_state_bf, False)
                dq = dq_in * e_pos * scale
                dk = dk_in * e_neg + dk_dec * e_end
                dq_sum = dq if dq_sum is None else dq_sum + dq
                dk_sum = dk if dk_sum is None else dk_sum + dk
                dv_sum = dv if dv_sum is None else dv_sum + dv
                dkk = dk_dec * k_dec
                db = dq_in * q_in - dk_in * k_in - dkk
                d_decay = jnp.sum(d_state * state, axis=0, keepdims=True)
                db_end = [jnp.sum(dkk[r * CHUNK:(r + 1) * CHUNK, :], axis=0, keepdims=True)
                          + d_decay[:, r * HEAD_K:(r + 1) * HEAD_K] * jnp.exp(ends[r]) for r in range(GROUP)]
                row = lax.broadcasted_iota(jnp.int32, (GROUP_ROWS, HEAD_K), 0)
                at_end = row == end_row
                for r in range(1, GROUP):
                    at_end = at_end | (row == r * CHUNK + end_row)
                db = db + jnp.where(at_end, _per_chunk_rows(db_end), 0.0)
                dg_s2[direction, rows, :] = _chunk_cumsum(db, direction == 0)
            dqkv_ref[rows, 0:128] = dq_sum.astype(BF16)
            dqkv_ref[rows, 128:256] = dk_sum.astype(BF16)
            dqkv_ref[rows, 256:512] = dv_sum.astype(BF16)
            return carry

        lax.fori_loop(0, n_groups, group_grads, 0, unroll=_group_unroll(n_groups))

        dlr = jnp.zeros((lf, LANES), F32)
        for direction in (0, 1):
            dz = dg_s2[direction] * fac_s2[direction]
            dz_bf = dz.astype(BF16)
            dbg_ref[direction:direction + 1, :] = jnp.sum(dz, axis=0, keepdims=True)
            (dwgf_ref, dwgb_ref)[direction][...] = _dot_tn(lr_bf, dz_bf)
            dlr = dlr + _dot_nt(dz_bf, gates[direction][0][...])

        @pl.when(pl.program_id(1) == 0)
        def _():
            dlr_ref[...] = dlr

        @pl.when(pl.program_id(1) != 0)
        def _():
            dlr_ref[...] = dlr_ref[...] + dlr

    gate_w = pl.BlockSpec((None, LANES, HEAD_K), lambda b, h: (h, 0, 0))
    gate_b = pl.BlockSpec((None, 1, HEAD_K), lambda b, h: (h, 0, 0))
    return pl.pallas_call(
        body, name="gla_bwd", grid=(n_seq, N_HEADS),
        in_specs=[pl.BlockSpec((lf, 512), lambda b, h: (b, N_CONV_TILES + h)),
                  pl.BlockSpec((lf, LANES), lambda b, h: (b, 0)),
                  pl.BlockSpec((lf, HEAD_V), lambda b, h: (b, h)),
                  gate_w, gate_w, gate_b, gate_b,
                  pl.BlockSpec((8, LANES), lambda b, h: (0, 0))],
        out_specs=[pl.BlockSpec((lf, 512), lambda b, h: (b, h)),
                   pl.BlockSpec((lf, LANES), lambda b, h: (b, 0)),
                   pl.BlockSpec((None, None, LANES, HEAD_K), lambda b, h: (b, h, 0, 0)),
                   pl.BlockSpec((None, None, LANES, HEAD_K), lambda b, h: (b, h, 0, 0)),
                   pl.BlockSpec((None, None, 2, HEAD_K), lambda b, h: (b, h, 0, 0))],
        out_shape=[jax.ShapeDtypeStruct((n_seq * lf, W_GLA), BF16),
                   jax.ShapeDtypeStruct((n_seq * lf, LANES), F32),
                   jax.ShapeDtypeStruct((n_seq, N_HEADS, LANES, HEAD_K), F32),
                   jax.ShapeDtypeStruct((n_seq, N_HEADS, LANES, HEAD_K), F32),
                   jax.ShapeDtypeStruct((n_seq, N_HEADS, 2, HEAD_K), F32)],
        scratch_shapes=[pltpu.VMEM((lf, HEAD_K), F32), pltpu.VMEM((2, lf, HEAD_K), F32),
                        pltpu.VMEM((2, lf, HEAD_K), F32), pltpu.VMEM((2, lf, HEAD_K), F32),
                        pltpu.VMEM((2, n_groups, HEAD_V, GROUP * HEAD_K), F32),
                        pltpu.VMEM((2, n_groups, HEAD_V, GROUP * HEAD_K), F32)],
        compiler_params=_params(("parallel", "arbitrary"), 56),
    )(proj, lr, d_o, wgf, wgb, bgf, bgb, token)


def _tail(h, tgt, yc, o, proj, w3, gamma, g_post, lf):
    t_rows = h.shape[0]
    tm = _pick_tile(t_rows, 256, CHUNK)
    n_chunks = lf // CHUNK
    per_tile = tm // CHUNK

    def body(h_ref, *refs):
        tgt_refs = refs[:per_tile]
        (yc_ref, o_ref, r_ref, ma_ref, mb_ref, w_hbm, gamma_ref, gpost_ref,
         dres_ref, yg_ref, merged_ref, dout_ref, dpc_ref, dpg_ref, dyc_ref, do_ref, dtail_ref,
         loss_ref, dgpost_ref, dgamma_ref, w_s, w_sem) = refs[per_tile:]
        i = pl.program_id(0)

        @pl.when(i == 0)
        def _():
            cp = pltpu.make_async_copy(w_hbm, w_s, w_sem)
            cp.start()
            cp.wait()
            loss_ref[...] = jnp.zeros_like(loss_ref)
            dgpost_ref[...] = jnp.zeros_like(dgpost_ref)
            dgamma_ref[...] = jnp.zeros_like(dgamma_ref)

        gamma = gamma_ref[...]
        o = o_ref[...]
        r = r_ref[...].astype(F32)
        sr = _sigmoid(r)
        silu_r = r * sr
        n_parts, rstd_parts = [], []
        for hd in range(N_HEADS):
            oh = o[:, hd * HEAD_V:(hd + 1) * HEAD_V]
            rstd = lax.rsqrt(jnp.mean(oh * oh, axis=-1, keepdims=True) + EPS)
            n_parts.append(oh * rstd)
            rstd_parts.append(rstd)
        n = jnp.concatenate(n_parts, axis=-1)
        gamma_t = jnp.concatenate([gamma] * N_HEADS, axis=-1)
        yg = n * gamma_t * silu_r
        yg_bf = yg.astype(BF16)
        yg_ref[...] = yg_bf
        yc = yc_ref[...]
        pc = _dot(yc, w_s[0])
        pg = _dot(yg_bf, w_s[1])
        sa = _sigmoid(ma_ref[...].astype(F32))
        sb = _sigmoid(mb_ref[...].astype(F32))
        merged = (sa * pc + sb * pg).astype(BF16)
        merged_ref[...] = merged
        out = _dot(merged, w_s[2])
        rstd2 = lax.rsqrt(jnp.mean(out * out, axis=-1, keepdims=True) + EPS)
        nn = out * rstd2
        gpost = gpost_ref[...]
        y = h_ref[...] + nn * gpost

        rowi = lax.broadcasted_iota(jnp.int32, (tm, 1), 0)
        keep = jnp.zeros((tm, 1), F32)
        for kk in range(per_tile):
            is_tok = ((i * per_tile + kk) % n_chunks) != 0
            f = jnp.where(is_tok, 1.0, 0.0)
            keep = jnp.where((rowi >= kk * CHUNK) & (rowi < (kk + 1) * CHUNK), f, keep)
        tgt = jnp.concatenate([t_ref[...] for t_ref in tgt_refs], axis=0)
        diff = jnp.where(keep > 0.0, y - tgt, 0.0)
        loss_ref[...] += jnp.sum(diff * diff) * (0.5 / D)
        dy = diff * (1.0 / D)
        dres_ref[...] = dy
        dgpost_ref[...] += jnp.sum(dy * nn, axis=0, keepdims=True)
        dn = dy * gpost
        dout_f = rstd2 * (dn - nn * jnp.mean(dn * nn, axis=-1, keepdims=True))
        dout = dout_f.astype(BF16)
        dout_ref[...] = jnp.transpose(dout_f).astype(BF16)
        dmerged = _dot_nt(dout, w_s[2])
        dpc_f = dmerged * sa
        dpg_f = dmerged * sb
        dpc = dpc_f.astype(BF16)
        dpg = dpg_f.astype(BF16)
        dpc_ref[...] = jnp.transpose(dpc_f).astype(BF16)
        dpg_ref[...] = jnp.transpose(dpg_f).astype(BF16)
        dtail_ref[:, D:2 * D] = (dmerged * pc * (sa * (1.0 - sa))).astype(BF16)
        dtail_ref[:, 2 * D:3 * D] = (dmerged * pg * (sb * (1.0 - sb))).astype(BF16)
        dyc_ref[...] = _dot_nt(dpc, w_s[0]).astype(BF16)
        dyg = _dot_nt(dpg, w_s[1])
        dtail_ref[:, 0:D] = (dyg * n * gamma_t * (sr * (1.0 + r * (1.0 - sr)))).astype(BF16)
        dgam_full = jnp.sum(dyg * n * silu_r, axis=0, keepdims=True)
        dgam = dgam_full[:, 0:HEAD_V]
        for hd in range(1, N_HEADS):
            dgam = dgam + dgam_full[:, hd * HEAD_V:(hd + 1) * HEAD_V]
        dgamma_ref[...] += dgam
        dng = dyg * gamma_t * silu_r
        do_parts = []
        for hd in range(N_HEADS):
            sl = slice(hd * HEAD_V, (hd + 1) * HEAD_V)
            dnh = dng[:, sl]
            nh = n_parts[hd]
            do_parts.append(rstd_parts[hd] * (dnh - nh * jnp.mean(dnh * nh, axis=-1, keepdims=True)))
        do_ref[...] = jnp.concatenate(do_parts, axis=-1).astype(BF16)

    row = lambda c: pl.BlockSpec((tm, D), lambda i: (i, c))
    col = pl.BlockSpec((D, tm), lambda i: (0, i))

    def tgt_chunk(kk):
        def index(i):
            q = i * per_tile + kk
            return (q // n_chunks) * (n_chunks - 1) + jnp.maximum(q % n_chunks - 1, 0), 0
        return pl.BlockSpec((CHUNK, D), index)

    const = lambda shape: pl.BlockSpec(shape, lambda i: (0, 0))
    act = jax.ShapeDtypeStruct((t_rows, D), BF16)
    act_t = jax.ShapeDtypeStruct((D, t_rows), BF16)
    return pl.pallas_call(
        body, name="tail", grid=(t_rows // tm,),
        in_specs=[row(0)] + [tgt_chunk(kk) for kk in range(per_tile)] + [row(0), row(0), row(6), row(7), row(8),
                  pl.BlockSpec(memory_space=pl.ANY), const((1, HEAD_V)), const((1, D))],
        out_specs=[row(0)] * 3 + [col] * 3 + [row(0)] * 2
                  + [pl.BlockSpec((tm, W_TAIL), lambda i: (i, 0)),
                     const((8, LANES)), const((1, D)), const((1, HEAD_V))],
        out_shape=[jax.ShapeDtypeStruct((t_rows, D), F32)] + [act] * 2 + [act_t] * 3 + [act] * 2
                  + [jax.ShapeDtypeStruct((t_rows, W_TAIL), BF16),
                     jax.ShapeDtypeStruct((8, LANES), F32),
                     jax.ShapeDtypeStruct((1, D), F32),
                     jax.ShapeDtypeStruct((1, HEAD_V), F32)],
        scratch_shapes=[pltpu.VMEM((3, D, D), BF16), pltpu.SemaphoreType.DMA],
        compiler_params=_params(("arbitrary",), 56),
    )(h, *[tgt] * per_tile, yc, o, proj, proj, proj, w3, gamma, g_post)


def _wgrad_t(a_t, b, name, out_dtype=BF16):
    m, t_rows = a_t.shape
    n = b.shape[1]
    tn = D if n % D == 0 else n
    tk = _pick_tile(t_rows, 768, LANES)
    n_k = t_rows // tk

    def body(a_ref, b_ref, o_ref, acc):
        k = pl.program_id(1)

        @pl.when(k == 0)
        def _():
            acc[...] = jnp.zeros_like(acc)

        acc[...] += _dot(a_ref[...], b_ref[...].astype(BF16))

        @pl.when(k == n_k - 1)
        def _():
            o_ref[...] = jnp.transpose(acc[...]).astype(out_dtype)

    return pl.pallas_call(
        body, name=name, grid=(n // tn, n_k),
        in_specs=[pl.BlockSpec((m, tk), lambda j, k: (0, k)),
                  pl.BlockSpec((tk, tn), lambda j, k: (k, j))],
        out_specs=pl.BlockSpec((tn, m), lambda j, k: (j, 0)),
        out_shape=jax.ShapeDtypeStruct((n, m), out_dtype),
        scratch_shapes=[pltpu.VMEM((m, tn), F32)],
        compiler_params=_params(("parallel", "arbitrary"), 48),
    )(a_t, b)


def _dgrad_in(dpc, dpg, dpt, dlr, w_full_t, h, g_pre, dres, token, n_seq, lf):
    t_rows = h.shape[0]
    tm = _pick_tile(t_rows, 384, CHUNK)
    n_main = N_MAIN
    n_chunks = lf // CHUNK
    per_tile = tm // CHUNK
    n_steps = t_rows // tm

    def body(dpc_ref, dpg_ref, dpt_ref, dlr_ref, w_hbm, h_ref, g_ref, dres_ref, token_ref,
             tok_hbm, head_hbm, dg_ref, w_s, wlr_s, w_sems, dh_s, out_sems):
        i = pl.program_id(0)

        def chunk_copies(step, act):
            for kk in range(per_tile):
                q = step * per_tile + kk
                sq, c = q // n_chunks, q % n_chunks
                src = dh_s.at[pl.ds(kk * CHUNK, CHUNK)]
                tok_row = pl.multiple_of((sq * (n_chunks - 1) + jnp.maximum(c - 1, 0)) * CHUNK, CHUNK)
                head_row = pl.multiple_of(sq * CHUNK, CHUNK)
                to_tok = pltpu.make_async_copy(src, tok_hbm.at[pl.ds(tok_row, CHUNK)], out_sems.at[kk])
                to_head = pltpu.make_async_copy(src, head_hbm.at[pl.ds(head_row, CHUNK)], out_sems.at[kk])
                pl.when(c != 0)(lambda cp=to_tok: act(cp))
                pl.when(c == 0)(lambda cp=to_head: act(cp))

        @pl.when(i == 0)
        def _():
            _load_weights(w_hbm, w_s, wlr_s, w_sems)
            dg_ref[...] = jnp.zeros_like(dg_ref)

        du = _dot(dlr_ref[...].astype(BF16), wlr_s[...])
        du += _dot(dpc_ref[...], w_s[0:W_CONV, :])
        du += _dot(dpg_ref[...], w_s[W_CONV:W_CONV + W_GLA, :])
        du += _dot(dpt_ref[...], w_s[W_CONV + W_GLA:n_main, :])
        hh = h_ref[...]
        rstd = lax.rsqrt(jnp.mean(hh * hh, axis=-1, keepdims=True) + EPS)
        xhat = hh * rstd
        dg_ref[...] += jnp.sum(du * xhat, axis=0, keepdims=True)
        dx = du * g_ref[...]
        dh = rstd * (dx - xhat * jnp.mean(dx * xhat, axis=-1, keepdims=True)) + dres_ref[...]

        @pl.when(i > 0)
        def _():
            chunk_copies(i - 1, lambda cp: cp.wait())

        dh_s[...] = dh
        chunk_copies(i, lambda cp: cp.start())

        @pl.when(i == n_steps - 1)
        def _():
            chunk_copies(i, lambda cp: cp.wait())

    row = lambda width: pl.BlockSpec((tm, width), lambda i: (i, 0))
    return pl.pallas_call(
        body, name="dgrad_in", grid=(n_steps,),
        in_specs=[row(W_CONV), row(W_GLA), row(W_TAIL), row(LANES),
                  pl.BlockSpec(memory_space=pl.ANY),
                  row(D), pl.BlockSpec((1, D), lambda i: (0, 0)), row(D),
                  pl.BlockSpec((8, LANES), lambda i: (0, 0))],
        out_specs=[pl.BlockSpec(memory_space=pl.ANY), pl.BlockSpec(memory_space=pl.ANY),
                   pl.BlockSpec((1, D), lambda i: (0, 0))],
        out_shape=[jax.ShapeDtypeStruct((t_rows - n_seq * CHUNK, D), F32),
                   jax.ShapeDtypeStruct((n_seq * CHUNK, D), F32),
                   jax.ShapeDtypeStruct((1, D), F32)],
        scratch_shapes=[pltpu.VMEM((n_main, D), BF16), pltpu.VMEM((LANES, D), BF16),
                        pltpu.SemaphoreType.DMA((N_WEIGHT_COPIES,)),
                        pltpu.VMEM((tm, D), F32), pltpu.SemaphoreType.DMA((per_tile,))],
        compiler_params=_params(("arbitrary",), 56),
    )(dpc, dpg, dpt, dlr, w_full_t, h, g_pre, dres, token)


def _reference_rows(g_conv, g_gla, g_tail, g_lr):
    conv = g_conv.reshape(N_CONV_TILES, 4, 128, D).transpose(1, 0, 2, 3).reshape(W_CONV, D)
    gla = g_gla.reshape(N_HEADS, 512, D)
    q = gla[:, 0:128].reshape(N_HEADS * HEAD_K, D)
    k = gla[:, 128:256].reshape(N_HEADS * HEAD_K, D)
    v = gla[:, 256:512].reshape(N_HEADS * HEAD_V, D)
    return jnp.concatenate([conv, q, k, v, g_tail[0:D], g_lr[0:2 * RANK], g_tail[D:3 * D]], axis=0)


def kernel(x, meta_tokens, norm_pre, w_in, conv_w, w_gate_fwd, b_gate_fwd, w_gate_bwd, b_gate_bwd, gla_norm, w_out_conv, w_out_gla, w_merge_out, norm_post, loss_target, m_meta_tokens, m_norm_pre, m_w_in, m_conv_w, m_w_gate_fwd, m_b_gate_fwd, m_w_gate_bwd, m_b_gate_bwd, m_gla_norm, m_w_out_conv, m_w_out_gla, m_w_merge_out, m_norm_post, v_meta_tokens, v_norm_pre, v_w_in, v_conv_w, v_w_gate_fwd, v_b_gate_fwd, v_w_gate_bwd, v_b_gate_bwd, v_gla_norm, v_w_out_conv, v_w_out_gla, v_w_merge_out, v_norm_post):
    n_seq, seq, _ = x.shape
    lf = CHUNK + seq
    t_rows = n_seq * lf
    shard = 2 * lax.axis_index("x") + lax.axis_index("y")
    shard_arr = jnp.reshape(shard, (1,)).astype(jnp.int32)

    w_in_slots = _cast_into_slot(jnp.transpose(w_in[0]), shard_arr, "cast_w_in")
    w_out_slots = _cast_into_slot(jnp.concatenate([w_out_conv[0], w_out_gla[0], w_merge_out[0]], axis=0), shard_arr,
                                  "cast_w_out")
    w_in_all, meta_all, conv_all, wgf_all, wgb_all = _gather_via_sibling(
        "gather_w_in", [w_in_slots, meta_tokens, conv_w[0], w_gate_fwd[0], w_gate_bwd[0]],
        (True, False, False, False, False))
    w_out_state, _ = _plane_start("gather_w_out_start", [w_out_slots], "gather", wgb_all)

    w_full_t = w_in_all.reshape(N_IN, D)
    meta_full = jnp.transpose(meta_all, (1, 0, 2)).reshape(N_META, D)
    conv_full = jnp.transpose(conv_all, (1, 0, 2)).reshape(3, D)
    wgf = jnp.pad(wgf_all, ((0, 0), (0, LANES - RANK), (0, 0))).astype(BF16)
    wgb = jnp.pad(wgb_all, ((0, 0), (RANK, LANES - 2 * RANK), (0, 0))).astype(BF16)
    bgf = b_gate_fwd.reshape(N_HEADS, 1, HEAD_K)
    bgb = b_gate_bwd.reshape(N_HEADS, 1, HEAD_K)

    head = jnp.concatenate([jnp.zeros((PAD_FRONT, D), F32), meta_full], axis=0)
    tgt = loss_target.reshape(n_seq * seq, D)

    h, proj, u_t, lr = _in_proj(x.reshape(n_seq * seq, D), head, norm_pre, w_full_t, n_seq, lf)
    yc = _conv_fwd(proj, conv_full, n_seq, lf)
    o = _gla_fwd(proj, lr, wgf, wgb, bgf, bgb, n_seq, lf)
    (w_out_all,) = _plane_wait("gather_w_out_wait", w_out_state, "gather", o)
    w3 = jnp.transpose(w_out_all.reshape(4, 3, D // 4, D), (1, 0, 2, 3)).reshape(3, D, D)
    (dres, yg, merged, dout_t, dpc_t, dpg_t, dyc, d_o, dtail, loss_acc, d_gpost, d_gamma) = _tail(
        h, tgt, yc, o, proj, w3, gla_norm, norm_post, lf)
    g_w_oc = _wgrad_t(dpc_t, yc, "wgrad_out_conv")
    g_w_og = _wgrad_t(dpg_t, yg, "wgrad_out_gla")
    g_w_mo = _wgrad_t(dout_t, merged, "wgrad_merge_out")
    g_out_slots = jnp.concatenate([g.reshape(4, D // 4, D) for g in (g_w_oc, g_w_og, g_w_mo)], axis=1)
    out_state, out_token = _plane_start("scatter_out_grads_start", [g_out_slots], "scatter", g_w_mo)
    dgla, dlr, dwgf_p, dwgb_p, dbg_p = _gla_bwd(proj, lr, d_o, wgf, wgb, bgf, bgb, n_seq, lf, out_token)
    (got_out,) = _plane_wait("scatter_out_grads_wait", out_state, "scatter", dlr)
    dconv, dconvw_p = _conv_bwd(proj, conv_full, dyc, n_seq, lf)
    g_conv = _wgrad_t(u_t, dconv, "wgrad_in_conv")
    g_gla = _wgrad_t(u_t, dgla, "wgrad_in_gla")
    g_tail = _wgrad_t(u_t, dtail, "wgrad_in_tail")
    g_lr = _wgrad_t(u_t, dlr, "wgrad_in_lr")

    g_in_slots = _reference_rows(g_conv, g_gla, g_tail, g_lr).reshape(4, SHARD_IN, D)
    in_state, in_token = _plane_start("scatter_in_grads_start", [g_in_slots], "scatter", g_lr)
    dh_tok, dh_head, d_gpre = _dgrad_in(dconv, dgla, dtail, dlr, w_full_t, h, norm_pre, dres, in_token, n_seq, lf)
    (got_in,) = _plane_wait("scatter_in_grads_wait", in_state, "scatter", d_gpre)

    plane_in = _sum_slots(got_in, "sum_w_in_grads", own=g_in_slots, slot=shard_arr)
    plane_out = _sum_slots(got_out, "sum_w_out_grads", own=g_out_slots, slot=shard_arr)
    swap_state, swap_token = _plane_start("swap_plane_sums_start", [plane_in, plane_out], "swap", plane_out)

    grad_x = dh_tok.reshape(n_seq, seq, D)

    d_meta = jnp.sum(dh_head.reshape(n_seq, CHUNK, D)[:, PAD_FRONT:, :], axis=0)
    d_convw = jnp.sum(dconvw_p, axis=0)
    d_wgf = jnp.transpose(jnp.sum(dwgf_p, axis=0)[:, 0:RANK, :], (1, 0, 2)).reshape(RANK, N_HEADS * HEAD_K)
    d_wgb = jnp.transpose(jnp.sum(dwgb_p, axis=0)[:, RANK:2 * RANK, :], (1, 0, 2)).reshape(RANK, N_HEADS * HEAD_K)
    d_bg = jnp.sum(dbg_p, axis=0)
    d_bgf = d_bg[:, 0, :].reshape(1, N_HEADS * HEAD_K)
    d_bgb = d_bg[:, 1, :].reshape(1, N_HEADS * HEAD_K)
    loss_part = loss_acc[0:1, :] + swap_token[0:1, :]
    partials = [d_meta, d_convw, d_wgf, d_wgb, d_gpre, d_bgf, d_bgb, d_gamma, d_gpost, loss_part]
    (g_meta, g_convw, g_wgf, g_wgb, g_npre, g_bgf, g_bgb, g_gnorm, g_npost, loss_row) = _sum_small(
        _gather_all("gather_small_grads", partials), "sum_small_grads")
    loss = loss_row[0, 0]
    small_out = _adamw_small(
        [(meta_tokens, g_meta, m_meta_tokens, v_meta_tokens), (norm_pre, g_npre, m_norm_pre, v_norm_pre),
         (conv_w, g_convw, m_conv_w, v_conv_w), (w_gate_fwd, g_wgf, m_w_gate_fwd, v_w_gate_fwd),
         (b_gate_fwd, g_bgf, m_b_gate_fwd, v_b_gate_fwd), (w_gate_bwd, g_wgb, m_w_gate_bwd, v_w_gate_bwd),
         (b_gate_bwd, g_bgb, m_b_gate_bwd, v_b_gate_bwd), (gla_norm, g_gnorm, m_gla_norm, v_gla_norm),
         (norm_post, g_npost, m_norm_post, v_norm_post)], shard_arr, "adamw_small")

    other_in, other_out = _plane_wait("swap_plane_sums_wait", swap_state, "swap", small_out[0][0])
    big_in = _adamw(jnp.transpose(w_in[0]), [plane_in, other_in], jnp.transpose(m_w_in[0]), jnp.transpose(v_w_in[0]),
                    "adamw_w_in")
    out_params = ((w_out_conv, m_w_out_conv, v_w_out_conv), (w_out_gla, m_w_out_gla, v_w_out_gla),
                  (w_merge_out, m_w_merge_out, v_w_merge_out))
    big_out = [_adamw(w[0], [plane_out, other_out], m[0], v[0], f"adamw_w_out_{i}", grad_row=i * (D // 4))
               for i, (w, m, v) in enumerate(out_params)]

    results = []
    for kind in range(4):
        small_kind = [p[kind] for p in small_out]
        w_in_part = jnp.transpose(big_in[kind])[None]
        outs3 = [big_out[i][kind][None] for i in range(3)]
        results.extend(small_kind[0:2] + [w_in_part] + small_kind[2:8] + outs3 + small_kind[8:9])
    return (loss, grad_x, *results)
```

```python
import functools

import jax
import jax.numpy as jnp
from jax import lax
from jax.experimental import pallas as pl
from jax.experimental.pallas import tpu as pltpu

F32 = jnp.float32
BF16 = jnp.bfloat16
MESH = pl.DeviceIdType.MESH

D = 1024
N_META = 16
CHUNK = 64
PAD_FRONT = CHUNK - N_META
N_HEADS = 4
HEAD_K = 128
HEAD_V = 256
RANK = 16
EPS = 1e-6
GATE_NORM = 16.0
N_IN = 9248
SHARD_IN = N_IN // 4
LANES = 128
N_CONV_TILES = 8
W_CONV = 4096
W_GLA = 2048
W_TAIL = 3072
N_MAIN = W_CONV + W_GLA + W_TAIL
OFF_Q, OFF_K, OFF_V, OFF_R = 4096, 4608, 5120, 6144
OFF_LR, OFF_MA = 7168, 7200
MIB = 1024 * 1024

ADAM_LR = 0.001
ADAM_B1 = 0.9
ADAM_B2 = 0.999
ADAM_EPS = 1e-08
ADAM_WD = 0.01
ADAM_STEP = 10


def _params(sem=None, vmem_mib=None):
    return pltpu.CompilerParams(
        dimension_semantics=sem,
        vmem_limit_bytes=None if vmem_mib is None else vmem_mib * MIB)


def _pick_tile(n, target, mult):
    best = None
    for t in range(mult, min(n, target) + 1, mult):
        if n % t == 0:
            best = t
    return n if best is None else best


def _sigmoid(v):
    return 1.0 / (1.0 + jnp.exp(-v))


def _log_sigmoid(v):
    return jnp.minimum(v, 0.0) - jnp.log(1.0 + jnp.exp(-jnp.abs(v)))


def _dot(a, b):
    return jnp.dot(a, b, preferred_element_type=F32)


def _dot_nt(a, b):
    return lax.dot_general(a, b, (((1,), (1,)), ((), ())), preferred_element_type=F32)


def _dot_tn(a, b):
    return lax.dot_general(a, b, (((0,), (0,)), ((), ())), preferred_element_type=F32)


def _gather_all(name, arrs):
    n = len(arrs)
    flips = tuple((m >> 2 & 1, m >> 1 & 1, m & 1) for m in range(1, 8))

    def body(*refs):
        ins, outs = refs[:n], refs[n:2 * n]
        send_sems, recv_sems, local_sems = refs[2 * n:]
        pos = (lax.axis_index("x"), lax.axis_index("y"), lax.axis_index("c"))

        def slot_of(p):
            return 4 * p[0] + 2 * p[1] + p[2]

        peers = [tuple(1 - pos[a] if f[a] else pos[a] for a in range(3)) for f in flips]
        me = slot_of(pos)
        copies = []
        for i in range(n):
            cp = pltpu.make_async_copy(ins[i], outs[i].at[me], local_sems.at[i])
            cp.start()
            copies.append(cp)
        sends = []
        for i in range(n):
            for k, peer in enumerate(peers):
                cp = pltpu.make_async_remote_copy(
                    src_ref=ins[i], dst_ref=outs[i].at[me], send_sem=send_sems.at[i, k], recv_sem=recv_sems.at[i, k],
                    device_id=peer, device_id_type=MESH)
                cp.start()
                sends.append(cp)
        for i in range(n):
            for k, peer in enumerate(peers):
                pltpu.make_async_remote_copy(
                    src_ref=ins[i], dst_ref=outs[i].at[slot_of(peer)], send_sem=send_sems.at[i, k],
                    recv_sem=recv_sems.at[i, k], device_id=peer, device_id_type=MESH).wait_recv()
        for cp in sends:
            cp.wait_send()
        for cp in copies:
            cp.wait()

    hbm = pl.BlockSpec(memory_space=pl.ANY)
    outs = pl.pallas_call(
        body, name=name, out_shape=[jax.ShapeDtypeStruct((8,) + a.shape, a.dtype) for a in arrs],
        in_specs=[hbm] * n, out_specs=[hbm] * n,
        scratch_shapes=[pltpu.SemaphoreType.DMA((n, 7)), pltpu.SemaphoreType.DMA((n, 7)),
                        pltpu.SemaphoreType.DMA((n,))],
        compiler_params=pltpu.CompilerParams(has_side_effects=True),
    )(*arrs)
    return list(outs)


def _gather_via_sibling(name, arrs, slotted):
    n = len(arrs)
    out_shape = [jax.ShapeDtypeStruct(a.shape if slotted[i] else (4,) + a.shape, a.dtype)
                 for i, a in enumerate(arrs)]

    def body(*refs):
        ins, outs = refs[:n], refs[n:2 * n]
        send_sems, recv_sems, local_sems = refs[2 * n:]
        x, y, c = lax.axis_index("x"), lax.axis_index("y"), lax.axis_index("c")
        me = 2 * x + y
        chips = [(1 - x, y), (x, 1 - y), (1 - x, 1 - y)]

        def half(ref, which):
            rows = ref.shape[0]
            cut = rows // 2 // 16 * 16
            return ref.at[pl.ds(0, cut)] if which == 0 else ref.at[pl.ds(cut, rows - cut)]

        def copy(src, dst, i, k, to):
            return pltpu.make_async_remote_copy(
                src_ref=src, dst_ref=dst, send_sem=send_sems.at[i, k], recv_sem=recv_sems.at[i, k],
                device_id=to, device_id_type=MESH)

        def run(mine):
            other = 1 - mine
            local, sends = [], []
            whole = [(not slotted[i]) and arrs[i].shape[0] < 32 for i in range(n)]
            for i in range(n):
                own = outs[i].at[me] if slotted[i] else ins[i]
                if not slotted[i]:
                    cp = pltpu.make_async_copy(ins[i], outs[i].at[me], local_sems.at[i])
                    cp.start()
                    local.append(cp)
                for k, (px, py) in enumerate(chips):
                    if whole[i]:
                        cp = copy(own, outs[i].at[me], i, k, (px, py, mine))
                    elif k < 2:
                        cp = copy(half(own, mine), half(outs[i].at[me], mine), i, k, (px, py, mine))
                    else:
                        continue
                    cp.start()
                    sends.append(cp)
            via = mine
            for k in (via, 1 - via, 2):
                px, py = chips[k]
                slot = 2 * px + py
                source = (px, py, mine) if k < 2 else chips[1 - via] + (mine,)
                for i in range(n):
                    if whole[i]:
                        copy(outs[i].at[slot], outs[i].at[slot], i, k, (px, py, mine)).wait_recv()
                        continue
                    landed = half(outs[i].at[slot], mine)
                    copy(landed, landed, i, k, source).wait_recv()
                    if k == via:
                        cp = copy(landed, landed, i, 2, chips[1 - via] + (mine,))
                        cp.start()
                        sends.append(cp)
                    cp = copy(landed, landed, i, 3 + k, (x, y, other))
                    cp.start()
                    sends.append(cp)
            for k, (px, py) in enumerate(chips):
                slot = 2 * px + py
                for i in range(n):
                    if whole[i]:
                        continue
                    passed = half(outs[i].at[slot], other)
                    copy(passed, passed, i, 3 + k, (x, y, other)).wait_recv()
            for cp in sends:
                cp.wait_send()
            for cp in local:
                cp.wait()

        for mine in (0, 1):
            pl.when(c == mine)(functools.partial(run, mine))

    hbm = pl.BlockSpec(memory_space=pl.ANY)
    outs = pl.pallas_call(
        body, name=name, out_shape=out_shape,
        in_specs=[hbm] * n, out_specs=[hbm] * n,
        scratch_shapes=[pltpu.SemaphoreType.DMA((n, 6)), pltpu.SemaphoreType.DMA((n, 6)),
                        pltpu.SemaphoreType.DMA((n,))],
        input_output_aliases={i: i for i in range(n) if slotted[i]},
        compiler_params=pltpu.CompilerParams(has_side_effects=True),
    )(*arrs)
    return list(outs)


HBM_SPEC = pl.BlockSpec(memory_space=pltpu.HBM)
SEM_SPEC = pl.BlockSpec(memory_space=pltpu.SEMAPHORE)
DATAFLOW = pltpu.SideEffectType.DATAFLOW_SIDE_EFFECTING


def _split_peers(mode):
    x, y, c = lax.axis_index("x"), lax.axis_index("y"), lax.axis_index("c")
    if mode == "swap":
        return 0, [((x, y, 1 - c), 0)]
    return 2 * x + y, [((1 - x, y, c), 2 * (1 - x) + y), ((x, 1 - y, c), 2 * x + 1 - y),
                       ((1 - x, 1 - y, c), 2 * (1 - x) + 1 - y)]


def _split_refs(mode, src, landing, me, peer_slot):
    if mode == "gather":
        return src.at[me], landing.at[me]
    if mode == "scatter":
        return src.at[peer_slot], landing.at[me]
    return src, landing


def _plane_start(name, arrs, mode, after):
    n = len(arrs)
    n_peers = 1 if mode == "swap" else 3
    if mode == "gather":
        srcs, lands = [], list(arrs)
    else:
        srcs, lands = list(arrs), [lax.empty(a.shape, a.dtype) for a in arrs]
    n_src = len(srcs)

    def body(*refs):
        landing = refs[n_src:n_src + n]
        sources = refs[:n_src] if n_src else landing
        send_sems, recv_sems = refs[n_src + n + 1], refs[n_src + n + 2]
        token = refs[-1]
        me, peers = _split_peers(mode)
        for i in range(n):
            for k, (peer, peer_slot) in enumerate(peers):
                src, dst = _split_refs(mode, sources[i], landing[i], me, peer_slot)
                pltpu.make_async_remote_copy(
                    src_ref=src, dst_ref=dst, send_sem=send_sems.at[n_peers * i + k],
                    recv_sem=recv_sems.at[n_peers * i + k], device_id=peer, device_id_type=MESH).start()
        token[...] = jnp.zeros_like(token)

    hbm_in = [pltpu.with_memory_space_constraint(a, pltpu.HBM) for a in srcs + lands]
    out = pl.pallas_call(
        body, name=name,
        out_shape=[pltpu.SemaphoreType.DMA((n_peers * n,)), pltpu.SemaphoreType.DMA((n_peers * n,))]
                  + [pltpu.HBM(a.shape, a.dtype) for a in lands]
                  + [jax.ShapeDtypeStruct((8, LANES), F32)],
        in_specs=[HBM_SPEC] * (n_src + n) + [pl.BlockSpec(memory_space=pl.ANY)],
        out_specs=[SEM_SPEC, SEM_SPEC] + [HBM_SPEC] * n + [pl.BlockSpec(memory_space=pltpu.VMEM)],
        input_output_aliases={n_src + i: 2 + i for i in range(n)},
        compiler_params=pltpu.CompilerParams(has_side_effects=DATAFLOW),
    )(*hbm_in, after)
    return out[:-1], out[-1]


def _plane_wait(name, state, mode, after):
    send_sems, recv_sems = state[0], state[1]
    bufs = list(state[2:])
    n = len(bufs)
    n_peers = 1 if mode == "swap" else 3

    def body(*refs):
        landing = refs[:n]
        send_sems, recv_sems = refs[n], refs[n + 1]
        _, peers = _split_peers(mode)
        for i in range(n):
            for k, (peer, peer_slot) in enumerate(peers):
                arrived = landing[i] if mode == "swap" else landing[i].at[peer_slot]
                cp = pltpu.make_async_remote_copy(
                    src_ref=arrived, dst_ref=arrived, send_sem=send_sems.at[n_peers * i + k],
                    recv_sem=recv_sems.at[n_peers * i + k], device_id=peer, device_id_type=MESH)
                cp.wait_send()
                cp.wait_recv()

    out = pl.pallas_call(
        body, name=name,
        out_shape=[pltpu.HBM(a.shape, a.dtype) for a in bufs],
        in_specs=[HBM_SPEC] * n + [SEM_SPEC, SEM_SPEC, pl.BlockSpec(memory_space=pl.ANY)],
        out_specs=[HBM_SPEC] * n,
        input_output_aliases={i: i for i in range(n)},
        compiler_params=pltpu.CompilerParams(has_side_effects=DATAFLOW),
    )(*bufs, send_sems, recv_sems, after)
    return list(out)


def _tile_2d(rows, cols, row_mult, max_elems=512 * 1024):
    if rows % row_mult == 0:
        rt = _pick_tile(rows, max(row_mult, max_elems // cols), row_mult)
        return (rt, cols), rows // rt, lambda i: (i, 0)
    ct = _pick_tile(cols, max(LANES, max_elems // rows), LANES)
    return (rows, ct), cols // ct, lambda i: (0, i)


def _cast_into_slot(a, slot, name):
    block, steps, index = _tile_2d(a.shape[0], a.shape[1], 16)

    def body(slot_ref, a_ref, o_ref):
        o_ref[...] = a_ref[...].astype(BF16)

    return pl.pallas_call(
        body, name=name,
        grid_spec=pltpu.PrefetchScalarGridSpec(
            num_scalar_prefetch=1, grid=(steps,),
            in_specs=[pl.BlockSpec(block, lambda i, s: index(i))],
            out_specs=pl.BlockSpec((None,) + block, lambda i, s: (s[0],) + index(i))),
        out_shape=jax.ShapeDtypeStruct((4,) + a.shape, BF16),
        compiler_params=_params(("arbitrary",)),
    )(slot, a)


def _sum_slots(buf, name, own=None, slot=None):
    n_slots, rows, cols = buf.shape
    (br, bc), steps, index = _tile_2d(rows, cols, 16, 320 * 1024)

    def body(*refs):
        if own is None:
            b_ref, o_ref = refs
        else:
            slot_ref, b_ref, own_ref, o_ref = refs
        acc = None
        for s in range(n_slots):
            term = b_ref[s] if own is None else jnp.where(slot_ref[0] == s, own_ref[...], b_ref[s])
            acc = term.astype(F32) if acc is None else acc + term.astype(F32)
        o_ref[...] = acc

    out_shape = jax.ShapeDtypeStruct((rows, cols), F32)
    if own is None:
        return pl.pallas_call(
            body, name=name, grid=(steps,),
            in_specs=[pl.BlockSpec((n_slots, br, bc), lambda i: (0,) + index(i))],
            out_specs=pl.BlockSpec((br, bc), index), out_shape=out_shape,
            compiler_params=_params(("parallel",), 48),
        )(buf)
    return pl.pallas_call(
        body, name=name,
        grid_spec=pltpu.PrefetchScalarGridSpec(
            num_scalar_prefetch=1, grid=(steps,),
            in_specs=[pl.BlockSpec((n_slots, br, bc), lambda i, s: (0,) + index(i)),
                      pl.BlockSpec((None, br, bc), lambda i, s: (s[0],) + index(i))],
            out_specs=pl.BlockSpec((br, bc), lambda i, s: index(i))),
        out_shape=out_shape,
        compiler_params=_params(("arbitrary",), 48),
    )(slot, buf, own)


def _sum_small(bufs, name):
    n = len(bufs)

    def body(*refs):
        for b_ref, o_ref in zip(refs[:n], refs[n:]):
            acc = b_ref[0]
            for s in range(1, b_ref.shape[0]):
                acc = acc + b_ref[s]
            o_ref[...] = acc

    vmem = pl.BlockSpec(memory_space=pltpu.VMEM)
    return pl.pallas_call(
        body, name=name, in_specs=[vmem] * n, out_specs=[vmem] * n,
        out_shape=[jax.ShapeDtypeStruct(b.shape[1:], b.dtype) for b in bufs],
    )(*bufs)


def _adam_update(w, g, m, v):
    c1 = 1.0 - ADAM_B1 ** ADAM_STEP
    c2 = 1.0 - ADAM_B2 ** ADAM_STEP
    m_new = ADAM_B1 * m + (1.0 - ADAM_B1) * g
    v_new = ADAM_B2 * v + (1.0 - ADAM_B2) * (g * g)
    m_hat = m_new / c1
    v_hat = v_new / c2
    return -ADAM_LR * (m_hat / (jnp.sqrt(v_hat) + ADAM_EPS) + ADAM_WD * w), m_new, v_new


def _adamw_small(params, slot, name):
    n = len(params)

    def spec_of(shape):
        lead = (None,) * (len(shape) - 2)
        return pl.BlockSpec(lead + tuple(shape[-2:]), lambda i, s, k=len(shape): (0,) * k)

    in_specs, operands, out_specs, out_shape = [], [], [], []
    for w, g, m, v in params:
        shard = g.shape[-1] != w.shape[-1]
        g_spec = pl.BlockSpec(tuple(w.shape[-2:]), (lambda i, s: (0, s[0])) if shard else (lambda i, s: (0, 0)))
        in_specs += [spec_of(w.shape), g_spec, spec_of(m.shape), spec_of(v.shape)]
        operands += [w, g, m, v]
        out_specs += [spec_of(w.shape)] * 4
        out_shape += [jax.ShapeDtypeStruct(w.shape, F32)] * 4

    def body(slot_ref, *refs):
        ins, outs = refs[:4 * n], refs[4 * n:]
        for p in range(n):
            w_ref, g_ref, m_ref, v_ref = ins[4 * p:4 * p + 4]
            g = g_ref[...]
            delta, m_new, v_new = _adam_update(w_ref[...], g, m_ref[...], v_ref[...])
            for o_ref, val in zip(outs[4 * p:4 * p + 4], (g, delta, m_new, v_new)):
                o_ref[...] = val

    out = pl.pallas_call(
        body, name=name,
        grid_spec=pltpu.PrefetchScalarGridSpec(num_scalar_prefetch=1, grid=(1,), in_specs=in_specs, out_specs=out_specs),
        out_shape=out_shape,
    )(slot, *operands)
    return [tuple(out[4 * p:4 * p + 4]) for p in range(n)]


def _adamw(w, grads, m, v, name, grad_row=0):
    rows, cols = w.shape
    (rt, _), _, _ = _tile_2d(rows, cols, 8, 160 * 1024)
    assert grad_row % rt == 0
    n_g = len(grads)

    def body(*refs):
        w_ref = refs[0]
        g_refs = refs[1:1 + n_g]
        m_ref, v_ref, g_out, d_out, m_out, v_out = refs[1 + n_g:]
        g = g_refs[0][...]
        for r in g_refs[1:]:
            g = g + r[...]
        g_out[...] = g
        d_out[...], m_out[...], v_out[...] = _adam_update(w_ref[...], g, m_ref[...], v_ref[...])

    spec = pl.BlockSpec((rt, cols), lambda i: (i, 0))
    grad_spec = pl.BlockSpec((rt, cols), lambda i: (i + grad_row // rt, 0))
    shape = jax.ShapeDtypeStruct((rows, cols), F32)
    return pl.pallas_call(
        body, name=name, grid=(rows // rt,),
        in_specs=[spec] + [grad_spec] * n_g + [spec] * 2, out_specs=[spec] * 4, out_shape=[shape] * 4,
        compiler_params=_params(("parallel",), 48),
    )(w, *grads, m, v)


def _weight_pieces():
    pieces = []
    for j in range(N_CONV_TILES):
        for g in range(4):
            pieces.append((512 * j + 128 * g, D * g + 128 * j, 128))
    for hd in range(N_HEADS):
        base = W_CONV + 512 * hd
        pieces.append((base, OFF_Q + HEAD_K * hd, HEAD_K))
        pieces.append((base + HEAD_K, OFF_K + HEAD_K * hd, HEAD_K))
        pieces.append((base + 2 * HEAD_K, OFF_V + HEAD_V * hd, HEAD_V))
    pieces.append((W_CONV + W_GLA, OFF_R, D))
    pieces.append((W_CONV + W_GLA + D, OFF_MA, 2 * D))
    return pieces


N_WEIGHT_COPIES = len(_weight_pieces()) + 1


def _weight_copies(w_hbm, w_s, wlr_s, sems, row_lo, row_hi):
    return [pltpu.make_async_copy(w_hbm.at[pl.ds(src, n)], w_s.at[pl.ds(dst, n)], sems.at[i])
            for i, (dst, src, n) in enumerate(_weight_pieces()) if row_lo <= dst < row_hi]


def _low_rank_copy(w_hbm, wlr_s, sems):
    return pltpu.make_async_copy(w_hbm.at[pl.ds(OFF_LR, LANES)], wlr_s, sems.at[N_WEIGHT_COPIES - 1])


def _start_weights(w_hbm, w_s, wlr_s, sems):
    _low_rank_copy(w_hbm, wlr_s, sems).start()
    for cp in _weight_copies(w_hbm, w_s, wlr_s, sems, 0, N_MAIN):
        cp.start()


def _in_proj(x_rows, head, g_pre, w_full_t, n_seq, lf):
    t_rows = n_seq * lf
    tm = _pick_tile(t_rows, 384, LANES)
    n_main = N_MAIN
    n_chunks = lf // CHUNK
    per_tile = tm // CHUNK

    def body(*refs):
        x_refs = refs[:per_tile]
        head_ref, g_ref, w_hbm, h_ref, proj_ref, ut_ref, lr_ref, w_s, wlr_s, w_sems = refs[per_tile:]
        i = pl.program_id(0)

        first = i == 0
        pl.when(first)(lambda: _start_weights(w_hbm, w_s, wlr_s, w_sems))

        rowi = lax.broadcasted_iota(jnp.int32, (tm, 1), 0)
        is_tok_row = jnp.zeros((tm, 1), F32)
        for kk in range(per_tile):
            f = jnp.where(((i * per_tile + kk) % n_chunks) != 0, 1.0, 0.0)
            is_tok_row = jnp.where((rowi >= kk * CHUNK) & (rowi < (kk + 1) * CHUNK), f, is_tok_row)
        tokens = jnp.concatenate([x_ref[...] for x_ref in x_refs], axis=0)
        heads = jnp.concatenate([head_ref[...]] * per_tile, axis=0)
        hh = jnp.where(is_tok_row > 0.0, tokens, heads)
        h_ref[...] = hh
        rstd = lax.rsqrt(jnp.mean(hh * hh, axis=-1, keepdims=True) + EPS)
        uf = hh * rstd * g_ref[...]
        u = uf.astype(BF16)
        ut_ref[...] = jnp.transpose(uf).astype(BF16)
        pl.when(first)(lambda: _low_rank_copy(w_hbm, wlr_s, w_sems).wait())
        lr_ref[...] = _dot_nt(u, wlr_s[...])
        for j in range(n_main // D):
            cols = slice(j * D, (j + 1) * D)

            @pl.when(first)
            def _(j=j):
                for cp in _weight_copies(w_hbm, w_s, wlr_s, w_sems, j * D, (j + 1) * D):
                    cp.wait()

            proj_ref[:, cols] = _dot_nt(u, w_s[cols, :]).astype(BF16)

    def token_chunk(kk):
        def index(i):
            q = i * per_tile + kk
            return (q // n_chunks) * (n_chunks - 1) + jnp.maximum(q % n_chunks - 1, 0), 0
        return pl.BlockSpec((CHUNK, D), index)

    return pl.pallas_call(
        body, name="in_proj", grid=(t_rows // tm,),
        in_specs=[token_chunk(kk) for kk in range(per_tile)]
                 + [pl.BlockSpec((CHUNK, D), lambda i: (0, 0)),
                    pl.BlockSpec((1, D), lambda i: (0, 0)),
                    pl.BlockSpec(memory_space=pl.ANY)],
        out_specs=[pl.BlockSpec((tm, D), lambda i: (i, 0)),
                   pl.BlockSpec((tm, n_main), lambda i: (i, 0)),
                   pl.BlockSpec((D, tm), lambda i: (0, i)),
                   pl.BlockSpec((tm, LANES), lambda i: (i, 0))],
        out_shape=[jax.ShapeDtypeStruct((t_rows, D), F32),
                   jax.ShapeDtypeStruct((t_rows, n_main), BF16),
                   jax.ShapeDtypeStruct((D, t_rows), BF16),
                   jax.ShapeDtypeStruct((t_rows, LANES), F32)],
        scratch_shapes=[pltpu.VMEM((n_main, D), BF16), pltpu.VMEM((LANES, D), BF16),
                        pltpu.SemaphoreType.DMA((N_WEIGHT_COPIES,))],
        compiler_params=_params(("arbitrary",), 56),
    )(*[x_rows] * per_tile, head, g_pre, w_full_t)


CONV_TILES_PER_STEP = 4


def _conv_parts(p_ref, w_ref, t):
    cb = p_ref[:, 512 * t:512 * t + 128].astype(F32)
    cc = p_ref[:, 512 * t + 128:512 * t + 256].astype(F32)
    cx = p_ref[:, 512 * t + 256:512 * t + 384].astype(F32)
    cz = p_ref[:, 512 * t + 384:512 * t + 512].astype(F32)
    rows = cb.shape[0]
    w = w_ref[:, 128 * t:128 * (t + 1)]
    p = cc * cx
    conv = pltpu.roll(p, 1, 0) * w[0:1] + p * w[1:2] + pltpu.roll(p, rows - 1, 0) * w[2:3]
    sz = _sigmoid(cz)
    return cb, cc, cx, cz, p, conv, sz, w


def _conv_fwd(proj, conv_w, n_seq, lf):
    per = CONV_TILES_PER_STEP

    def body(p_ref, w_ref, y_ref):
        for t in range(per):
            cb, _, _, cz, _, conv, sz, _ = _conv_parts(p_ref, w_ref, t)
            y_ref[:, 128 * t:128 * (t + 1)] = (cb * conv * (cz * sz)).astype(BF16)

    return pl.pallas_call(
        body, name="conv_fwd", grid=(n_seq, N_CONV_TILES // per),
        in_specs=[pl.BlockSpec((lf, 512 * per), lambda b, j: (b, j)),
                  pl.BlockSpec((3, 128 * per), lambda b, j: (0, j))],
        out_specs=pl.BlockSpec((lf, 128 * per), lambda b, j: (b, j)),
        out_shape=jax.ShapeDtypeStruct((n_seq * lf, D), BF16),
        compiler_params=_params(("parallel", "parallel"), 48),
    )(proj, conv_w)


def _conv_bwd(proj, conv_w, dyc, n_seq, lf):
    per = CONV_TILES_PER_STEP

    def body(p_ref, w_ref, dy_ref, dp_ref, dw_ref):
        for t in range(per):
            cb, cc, cx, cz, p, conv, sz, w = _conv_parts(p_ref, w_ref, t)
            rows = cb.shape[0]
            dy = dy_ref[:, 128 * t:128 * (t + 1)].astype(F32)
            silu = cz * sz
            dcb = dy * conv * silu
            dconv = dy * cb * silu
            dcz = dy * cb * conv * (sz * (1.0 + cz * (1.0 - sz)))
            d_next = pltpu.roll(dconv, rows - 1, 0)
            d_prev = pltpu.roll(dconv, 1, 0)
            dp = d_next * w[0:1] + dconv * w[1:2] + d_prev * w[2:3]
            base = 512 * t
            dp_ref[:, base:base + 128] = dcb.astype(BF16)
            dp_ref[:, base + 128:base + 256] = (dp * cx).astype(BF16)
            dp_ref[:, base + 256:base + 384] = (dp * cc).astype(BF16)
            dp_ref[:, base + 384:base + 512] = dcz.astype(BF16)
            lanes = slice(128 * t, 128 * (t + 1))
            dw_ref[0:1, lanes] = jnp.sum(dconv * pltpu.roll(p, 1, 0), axis=0, keepdims=True)
            dw_ref[1:2, lanes] = jnp.sum(dconv * p, axis=0, keepdims=True)
            dw_ref[2:3, lanes] = jnp.sum(dconv * pltpu.roll(p, rows - 1, 0), axis=0, keepdims=True)

    return pl.pallas_call(
        body, name="conv_bwd", grid=(n_seq, N_CONV_TILES // per),
        in_specs=[pl.BlockSpec((lf, 512 * per), lambda b, j: (b, j)),
                  pl.BlockSpec((3, 128 * per), lambda b, j: (0, j)),
                  pl.BlockSpec((lf, 128 * per), lambda b, j: (b, j))],
        out_specs=[pl.BlockSpec((lf, 512 * per), lambda b, j: (b, j)),
                   pl.BlockSpec((None, 3, 128 * per), lambda b, j: (b, 0, j))],
        out_shape=[jax.ShapeDtypeStruct((n_seq * lf, W_CONV), BF16),
                   jax.ShapeDtypeStruct((n_seq, 3, D), F32)],
        compiler_params=_params(("parallel", "parallel"), 48),
    )(proj, conv_w, dyc)


GROUP = 3
GROUP_ROWS = GROUP * CHUNK


def _row_group(shape):
    row = lax.broadcasted_iota(jnp.int32, shape, 0)
    grp = jnp.zeros(shape, jnp.int32)
    for r in range(1, GROUP):
        grp = grp + (row >= r * CHUNK).astype(jnp.int32)
    return grp


def _lane_group(shape, width):
    lane = lax.broadcasted_iota(jnp.int32, shape, 1)
    grp = jnp.zeros(shape, jnp.int32)
    for r in range(1, GROUP):
        grp = grp + (lane >= r * width).astype(jnp.int32)
    return grp


def _score_mask(direction):
    shape = (GROUP_ROWS, GROUP_ROWS)
    row = lax.broadcasted_iota(jnp.int32, shape, 0)
    col = lax.broadcasted_iota(jnp.int32, shape, 1)
    same = _row_group(shape) == _lane_group(shape, CHUNK)
    return same & ((col <= row) if direction == 0 else (col > row))


def _diag_blocks(v):
    w = v.shape[1]
    wide = jnp.concatenate([v] * GROUP, axis=1)
    return jnp.where(_row_group(wide.shape) == _lane_group(wide.shape, w), wide, jnp.zeros_like(wide))


def _per_chunk_dot(lhs, state, transposed):
    outs = []
    for r in range(GROUP):
        rows = lhs[r * CHUNK:(r + 1) * CHUNK, :]
        blk = state[:, r * HEAD_K:(r + 1) * HEAD_K]
        outs.append(_dot_nt(rows, blk) if transposed else _dot(rows, blk))
    return jnp.concatenate(outs, axis=0)


def _chunk_cumsum(v, suffix):
    pos = lax.broadcasted_iota(jnp.int32, v.shape, 0) & (CHUNK - 1)
    shift = 1
    while shift < CHUNK:
        if suffix:
            moved = pltpu.roll(v, GROUP_ROWS - shift, 0)
            v = v + jnp.where(pos < CHUNK - shift, moved, 0.0)
        else:
            moved = pltpu.roll(v, shift, 0)
            v = v + jnp.where(pos >= shift, moved, 0.0)
        shift *= 2
    return v


def _per_chunk_rows(rows_of_chunk):
    w = rows_of_chunk[0].shape[1]
    return jnp.concatenate([jnp.broadcast_to(v, (CHUNK, w)) for v in rows_of_chunk], axis=0)


def _chunk_end_rows(direction, b):
    at = CHUNK - 1 if direction == 0 else 0
    return [b[r * CHUNK + at:r * CHUNK + at + 1, :] for r in range(GROUP)]


def _gla_gates(lr_bf, wg_ref, bg_ref, lf):
    z = _dot(lr_bf, wg_ref[...]) + bg_ref[...]
    valid = lax.broadcasted_iota(jnp.int32, (lf, HEAD_K), 0) >= PAD_FRONT
    return z, valid


def _group_unroll(n_groups):
    return n_groups if n_groups <= 11 else 1


def _group_rows(g):
    return pl.ds(pl.multiple_of(g * GROUP_ROWS, GROUP_ROWS), GROUP_ROWS)


def _chunk_decay(direction, g, r, b_s):
    base = g * GROUP_ROWS + r * CHUNK
    if direction == 0:
        grp = b_s[pl.ds(pl.multiple_of(base + CHUNK - 8, 8), 8), :]
        return jnp.exp(grp[7:8, :])
    grp = b_s[pl.ds(pl.multiple_of(base, 8), 8), :]
    return jnp.exp(grp[0:1, :])


def _state_scan(direction, n_groups, b_s, st_s, reverse):
    ascending = (direction == 0) != reverse

    def step(i, carry):
        g = i if ascending else n_groups - 1 - i
        for rr in range(GROUP):
            r = rr if ascending else GROUP - 1 - rr
            lanes = slice(r * HEAD_K, (r + 1) * HEAD_K)
            decay = _chunk_decay(direction, g, r, b_s)
            local = st_s[g, :, lanes]
            st_s[g, :, lanes] = carry
            carry = (local + carry * decay) if reverse else (carry * decay + local)
        return carry

    lax.fori_loop(0, n_groups, step, jnp.zeros((HEAD_V, HEAD_K), F32), unroll=_group_unroll(n_groups))


def _gla_states(direction, n_groups, qkv_ref, g_s, b_s, st_s):
    def local(g, carry):
        rows = _group_rows(g)
        b = _chunk_cumsum(g_s[rows, :], direction == 1)
        b_s[rows, :] = b
        b_end = _per_chunk_rows(_chunk_end_rows(direction, b))
        k = qkv_ref[rows, 128:256].astype(F32)
        v = qkv_ref[rows, 256:512]
        k_dec = (k * jnp.exp(b_end - b)).astype(BF16)
        st_s[g] = _dot_tn(v, _diag_blocks(k_dec))
        return carry

    lax.fori_loop(0, n_groups, local, 0, unroll=_group_unroll(n_groups))
    _state_scan(direction, n_groups, b_s, st_s, False)


def _gla_fwd(proj, lr, wgf, wgb, bgf, bgb, n_seq, lf):
    assert lf % GROUP_ROWS == 0
    n_groups = lf // GROUP_ROWS
    scale = HEAD_K ** -0.5

    def body(qkv_ref, lr_ref, wgf_ref, wgb_ref, bgf_ref, bgb_ref, o_ref, g_s, b_s2, st_s2):
        lr_bf = lr_ref[...].astype(BF16)
        for direction in (0, 1):
            wg_ref, bg_ref = ((wgf_ref, bgf_ref), (wgb_ref, bgb_ref))[direction]
            z, valid = _gla_gates(lr_bf, wg_ref, bg_ref, lf)
            g_s[...] = jnp.where(valid, _log_sigmoid(z) / GATE_NORM, 0.0)
            _gla_states(direction, n_groups, qkv_ref, g_s, b_s2.at[direction], st_s2.at[direction])
        masks = [_score_mask(0), _score_mask(1)]

        def out(g, carry):
            rows = _group_rows(g)
            q = qkv_ref[rows, 0:128].astype(F32) * scale
            k = qkv_ref[rows, 128:256].astype(F32)
            v = qkv_ref[rows, 256:512]
            o = None
            for direction in (0, 1):
                b = b_s2[direction, rows, :]
                q_in = (q * jnp.exp(b)).astype(BF16)
                k_in = (k * jnp.exp(-b)).astype(BF16)
                s = jnp.where(masks[direction], _dot_nt(q_in, k_in), 0.0).astype(BF16)
                part = _dot(s, v) + _per_chunk_dot(q_in, st_s2[direction, g].astype(BF16), True)
                o = part if o is None else o + part
            o_ref[rows, :] = o
            return carry

        lax.fori_loop(0, n_groups, out, 0, unroll=_group_unroll(n_groups))

    return pl.pallas_call(
        body, name="gla_fwd", grid=(n_seq, N_HEADS),
        in_specs=[pl.BlockSpec((lf, 512), lambda b, h: (b, N_CONV_TILES + h)),
                  pl.BlockSpec((lf, LANES), lambda b, h: (b, 0)),
                  pl.BlockSpec((None, LANES, HEAD_K), lambda b, h: (h, 0, 0)),
                  pl.BlockSpec((None, LANES, HEAD_K), lambda b, h: (h, 0, 0)),
                  pl.BlockSpec((None, 1, HEAD_K), lambda b, h: (h, 0, 0)),
                  pl.BlockSpec((None, 1, HEAD_K), lambda b, h: (h, 0, 0))],
        out_specs=pl.BlockSpec((lf, HEAD_V), lambda b, h: (b, h)),
        out_shape=jax.ShapeDtypeStruct((n_seq * lf, D), F32),
        scratch_shapes=[pltpu.VMEM((lf, HEAD_K), F32), pltpu.VMEM((2, lf, HEAD_K), F32),
                        pltpu.VMEM((2, n_groups, HEAD_V, GROUP * HEAD_K), F32)],
        compiler_params=_params(("parallel", "parallel"), 48),
    )(proj, lr, wgf, wgb, bgf, bgb)


def _gla_bwd(proj, lr, d_o, wgf, wgb, bgf, bgb, n_seq, lf, token):
    assert lf % GROUP_ROWS == 0
    n_groups = lf // GROUP_ROWS
    scale = HEAD_K ** -0.5

    def body(qkv_ref, lr_ref, do_ref, wgf_ref, wgb_ref, bgf_ref, bgb_ref, token_ref,
             dqkv_ref, dlr_ref, dwgf_ref, dwgb_ref, dbg_ref,
             g_s, b_s2, fac_s2, dg_s2, st_s2, dst_s2):
        lr_bf = lr_ref[...].astype(BF16)
        gates = ((wgf_ref, bgf_ref), (wgb_ref, bgb_ref))
        for direction in (0, 1):
            wg_ref, bg_ref = gates[direction]
            b_s, st_s, dst_s = b_s2.at[direction], st_s2.at[direction], dst_s2.at[direction]
            z, valid = _gla_gates(lr_bf, wg_ref, bg_ref, lf)
            g_s[...] = jnp.where(valid, _log_sigmoid(z) / GATE_NORM, 0.0)
            fac_s2[direction] = jnp.where(valid, _sigmoid(-z) / GATE_NORM, 0.0)
            _gla_states(direction, n_groups, qkv_ref, g_s, b_s, st_s)

            def state_grad_local(g, carry):
                rows = _group_rows(g)
                q = qkv_ref[rows, 0:128].astype(F32) * scale
                q_in = (q * jnp.exp(b_s[rows, :])).astype(BF16)
                dst_s[g] = _dot_tn(do_ref[rows, :], _diag_blocks(q_in))
                return carry

            lax.fori_loop(0, n_groups, state_grad_local, 0, unroll=_group_unroll(n_groups))
            _state_scan(direction, n_groups, b_s, dst_s, True)

        masks = [_score_mask(0), _score_mask(1)]

        def group_grads(g, carry):
            rows = _group_rows(g)
            q = qkv_ref[rows, 0:128].astype(F32) * scale
            k = qkv_ref[rows, 128:256].astype(F32)
            v = qkv_ref[rows, 256:512]
            d_out = do_ref[rows, :]
            dq_sum = dk_sum = dv_sum = None
            for direction in (0, 1):
                end_row = CHUNK - 1 if direction == 0 else 0
                b = b_s2[direction, rows, :]
                ends = _chunk_end_rows(direction, b)
                b_end = _per_chunk_rows(ends)
                e_pos = jnp.exp(b)
                e_neg = jnp.exp(-b)
                e_end = jnp.exp(b_end - b)
                q_in = q * e_pos
                k_in = k * e_neg
                k_dec = k * e_end
                q_in_bf = q_in.astype(BF16)
                k_in_bf = k_in.astype(BF16)
                state = st_s2[direction, g]
                d_state = dst_s2[direction, g]
                state_bf = state.astype(BF16)
                d_state_bf = d_state.astype(BF16)
                s = jnp.where(masks[direction], _dot_nt(q_in_bf, k_in_bf), 0.0).astype(BF16)
                ds = jnp.where(masks[direction], _dot_nt(d_out, v), 0.0).astype(BF16)
                dv = _dot_tn(s, d_out) + _per_chunk_dot(k_dec.astype(BF16), d_state_bf, True)
                dq_in = _dot(ds, k_in_bf) + _per_chunk_dot(d_out, state_bf, False)
                dk_in = _dot_tn(ds, q_in_bf)
                dk_dec = _per_chunk_dot(v, d_state_bf, False)
                dq = dq_in * e_pos * scale
                dk = dk_in * e_neg + dk_dec * e_end
                dq_sum = dq if dq_sum is None else dq_sum + dq
                dk_sum = dk if dk_sum is None else dk_sum + dk
                dv_sum = dv if dv_sum is None else dv_sum + dv
                dkk = dk_dec * k_dec
                db = dq_in * q_in - dk_in * k_in - dkk
                d_decay = jnp.sum(d_state * state, axis=0, keepdims=True)
                db_end = [jnp.sum(dkk[r * CHUNK:(r + 1) * CHUNK, :], axis=0, keepdims=True)
                          + d_decay[:, r * HEAD_K:(r + 1) * HEAD_K] * jnp.exp(ends[r]) for r in range(GROUP)]
                row = lax.broadcasted_iota(jnp.int32, (GROUP_ROWS, HEAD_K), 0)
                at_end = row == end_row
                for r in range(1, GROUP):
                    at_end = at_end | (row == r * CHUNK + end_row)
                db = db + jnp.where(at_end, _per_chunk_rows(db_end), 0.0)
                dg_s2[direction, rows, :] = _chunk_cumsum(db, direction == 0)
            dqkv_ref[rows, 0:128] = dq_sum.astype(BF16)
            dqkv_ref[rows, 128:256] = dk_sum.astype(BF16)
            dqkv_ref[rows, 256:512] = dv_sum.astype(BF16)
            return carry

        lax.fori_loop(0, n_groups, group_grads, 0, unroll=_group_unroll(n_groups))

        dlr = jnp.zeros((lf, LANES), F32)
        for direction in (0, 1):
            dz = dg_s2[direction] * fac_s2[direction]
            dz_bf = dz.astype(BF16)
            dbg_ref[direction:direction + 1, :] = jnp.sum(dz, axis=0, keepdims=True)
            (dwgf_ref, dwgb_ref)[direction][...] = _dot_tn(lr_bf, dz_bf)
            dlr = dlr + _dot_nt(dz_bf, gates[direction][0][...])

        @pl.when(pl.program_id(1) == 0)
        def _():
            dlr_ref[...] = dlr

        @pl.when(pl.program_id(1) != 0)
        def _():
            dlr_ref[...] = dlr_ref[...] + dlr

    gate_w = pl.BlockSpec((None, LANES, HEAD_K), lambda b, h: (h, 0, 0))
    gate_b = pl.BlockSpec((None, 1, HEAD_K), lambda b, h: (h, 0, 0))
    return pl.pallas_call(
        body, name="gla_bwd", grid=(n_seq, N_HEADS),
        in_specs=[pl.BlockSpec((lf, 512), lambda b, h: (b, N_CONV_TILES + h)),
                  pl.BlockSpec((lf, LANES), lambda b, h: (b, 0)),
                  pl.BlockSpec((lf, HEAD_V), lambda b, h: (b, h)),
                  gate_w, gate_w, gate_b, gate_b,
                  pl.BlockSpec((8, LANES), lambda b, h: (0, 0))],
        out_specs=[pl.BlockSpec((lf, 512), lambda b, h: (b, h)),
                   pl.BlockSpec((lf, LANES), lambda b, h: (b, 0)),
                   pl.BlockSpec((None, None, LANES, HEAD_K), lambda b, h: (b, h, 0, 0)),
                   pl.BlockSpec((None, None, LANES, HEAD_K), lambda b, h: (b, h, 0, 0)),
                   pl.BlockSpec((None, None, 2, HEAD_K), lambda b, h: (b, h, 0, 0))],
        out_shape=[jax.ShapeDtypeStruct((n_seq * lf, W_GLA), BF16),
                   jax.ShapeDtypeStruct((n_seq * lf, LANES), F32),
                   jax.ShapeDtypeStruct((n_seq, N_HEADS, LANES, HEAD_K), F32),
                   jax.ShapeDtypeStruct((n_seq, N_HEADS, LANES, HEAD_K), F32),
                   jax.ShapeDtypeStruct((n_seq, N_HEADS, 2, HEAD_K), F32)],
        scratch_shapes=[pltpu.VMEM((lf, HEAD_K), F32), pltpu.VMEM((2, lf, HEAD_K), F32),
                        pltpu.VMEM((2, lf, HEAD_K), F32), pltpu.VMEM((2, lf, HEAD_K), F32),
                        pltpu.VMEM((2, n_groups, HEAD_V, GROUP * HEAD_K), F32),
                        pltpu.VMEM((2, n_groups, HEAD_V, GROUP * HEAD_K), F32)],
        compiler_params=_params(("parallel", "arbitrary"), 56),
    )(proj, lr, d_o, wgf, wgb, bgf, bgb, token)


def _tail(h, tgt, yc, o, proj, w3, gamma, g_post, lf):
    t_rows = h.shape[0]
    tm = _pick_tile(t_rows, 256, CHUNK)
    n_chunks = lf // CHUNK
    per_tile = tm // CHUNK

    def body(h_ref, *refs):
        tgt_refs = refs[:per_tile]
        (yc_ref, o_ref, r_ref, ma_ref, mb_ref, w_hbm, gamma_ref, gpost_ref,
         dres_ref, yg_ref, merged_ref, dout_ref, dpc_ref, dpg_ref, dyc_ref, do_ref, dtail_ref,
         loss_ref, dgpost_ref, dgamma_ref, w_s, w_sem) = refs[per_tile:]
        i = pl.program_id(0)

        @pl.when(i == 0)
        def _():
            cp = pltpu.make_async_copy(w_hbm, w_s, w_sem)
            cp.start()
            cp.wait()
            loss_ref[...] = jnp.zeros_like(loss_ref)
            dgpost_ref[...] = jnp.zeros_like(dgpost_ref)
            dgamma_ref[...] = jnp.zeros_like(dgamma_ref)

        gamma = gamma_ref[...]
        o = o_ref[...]
        r = r_ref[...].astype(F32)
        sr = _sigmoid(r)
        silu_r = r * sr
        n_parts, rstd_parts = [], []
        for hd in range(N_HEADS):
            oh = o[:, hd * HEAD_V:(hd + 1) * HEAD_V]
            rstd = lax.rsqrt(jnp.mean(oh * oh, axis=-1, keepdims=True) + EPS)
            n_parts.append(oh * rstd)
            rstd_parts.append(rstd)
        n = jnp.concatenate(n_parts, axis=-1)
        gamma_t = jnp.concatenate([gamma] * N_HEADS, axis=-1)
        yg = n * gamma_t * silu_r
        yg_bf = yg.astype(BF16)
        yg_ref[...] = yg_bf
        yc = yc_ref[...]
        pc = _dot(yc, w_s[0])
        pg = _dot(yg_bf, w_s[1])
        sa = _sigmoid(ma_ref[...].astype(F32))
        sb = _sigmoid(mb_ref[...].astype(F32))
        merged = (sa * pc + sb * pg).astype(BF16)
        merged_ref[...] = merged
        out = _dot(merged, w_s[2])
        rstd2 = lax.rsqrt(jnp.mean(out * out, axis=-1, keepdims=True) + EPS)
        nn = out * rstd2
        gpost = gpost_ref[...]
        y = h_ref[...] + nn * gpost

        rowi = lax.broadcasted_iota(jnp.int32, (tm, 1), 0)
        keep = jnp.zeros((tm, 1), F32)
        for kk in range(per_tile):
            is_tok = ((i * per_tile + kk) % n_chunks) != 0
            f = jnp.where(is_tok, 1.0, 0.0)
            keep = jnp.where((rowi >= kk * CHUNK) & (rowi < (kk + 1) * CHUNK), f, keep)
        tgt = jnp.concatenate([t_ref[...] for t_ref in tgt_refs], axis=0)
        diff = jnp.where(keep > 0.0, y - tgt, 0.0)
        loss_ref[...] += jnp.sum(diff * diff) * (0.5 / D)
        dy = diff * (1.0 / D)
        dres_ref[...] = dy
        dgpost_ref[...] += jnp.sum(dy * nn, axis=0, keepdims=True)
        dn = dy * gpost
        dout_f = rstd2 * (dn - nn * jnp.mean(dn * nn, axis=-1, keepdims=True))
        dout = dout_f.astype(BF16)
        dout_ref[...] = jnp.transpose(dout_f).astype(BF16)
        dmerged = _dot_nt(dout, w_s[2])
        dpc_f = dmerged * sa
        dpg_f = dmerged * sb
        dpc = dpc_f.astype(BF16)
        dpg = dpg_f.astype(BF16)
        dpc_ref[...] = jnp.transpose(dpc_f).astype(BF16)
        dpg_ref[...] = jnp.transpose(dpg_f).astype(BF16)
        dtail_ref[:, D:2 * D] = (dmerged * pc * (sa * (1.0 - sa))).astype(BF16)
        dtail_ref[:, 2 * D:3 * D] = (dmerged * pg * (sb * (1.0 - sb))).astype(BF16)
        dyc_ref[...] = _dot_nt(dpc, w_s[0]).astype(BF16)
        dyg = _dot_nt(dpg, w_s[1])
        dtail_ref[:, 0:D] = (dyg * n * gamma_t * (sr * (1.0 + r * (1.0 - sr)))).astype(BF16)
        dgam_full = jnp.sum(dyg * n * silu_r, axis=0, keepdims=True)
        dgam = dgam_full[:, 0:HEAD_V]
        for hd in range(1, N_HEADS):
            dgam = dgam + dgam_full[:, hd * HEAD_V:(hd + 1) * HEAD_V]
        dgamma_ref[...] += dgam
        dng = dyg * gamma_t * silu_r
        do_parts = []
        for hd in range(N_HEADS):
            sl = slice(hd * HEAD_V, (hd + 1) * HEAD_V)
            dnh = dng[:, sl]
            nh = n_parts[hd]
            do_parts.append(rstd_parts[hd] * (dnh - nh * jnp.mean(dnh * nh, axis=-1, keepdims=True)))
        do_ref[...] = jnp.concatenate(do_parts, axis=-1).astype(BF16)

    row = lambda c: pl.BlockSpec((tm, D), lambda i: (i, c))
    col = pl.BlockSpec((D, tm), lambda i: (0, i))

    def tgt_chunk(kk):
        def index(i):
            q = i * per_tile + kk
            return (q // n_chunks) * (n_chunks - 1) + jnp.maximum(q % n_chunks - 1, 0), 0
        return pl.BlockSpec((CHUNK, D), index)

    const = lambda shape: pl.BlockSpec(shape, lambda i: (0, 0))
    act = jax.ShapeDtypeStruct((t_rows, D), BF16)
    act_t = jax.ShapeDtypeStruct((D, t_rows), BF16)
    return pl.pallas_call(
        body, name="tail", grid=(t_rows // tm,),
        in_specs=[row(0)] + [tgt_chunk(kk) for kk in range(per_tile)] + [row(0), row(0), row(6), row(7), row(8),
                  pl.BlockSpec(memory_space=pl.ANY), const((1, HEAD_V)), const((1, D))],
        out_specs=[row(0)] * 3 + [col] * 3 + [row(0)] * 2
                  + [pl.BlockSpec((tm, W_TAIL), lambda i: (i, 0)),
                     const((8, LANES)), const((1, D)), const((1, HEAD_V))],
        out_shape=[jax.ShapeDtypeStruct((t_rows, D), F32)] + [act] * 2 + [act_t] * 3 + [act] * 2
                  + [jax.ShapeDtypeStruct((t_rows, W_TAIL), BF16),
                     jax.ShapeDtypeStruct((8, LANES), F32),
                     jax.ShapeDtypeStruct((1, D), F32),
                     jax.ShapeDtypeStruct((1, HEAD_V), F32)],
        scratch_shapes=[pltpu.VMEM((3, D, D), BF16), pltpu.SemaphoreType.DMA],
        compiler_params=_params(("arbitrary",), 56),
    )(h, *[tgt] * per_tile, yc, o, proj, proj, proj, w3, gamma, g_post)


def _wgrad_t(a_t, b, name, out_dtype=BF16):
    m, t_rows = a_t.shape
    n = b.shape[1]
    tn = D if n % D == 0 else n
    tk = _pick_tile(t_rows, 768, LANES)
    n_k = t_rows // tk

    def body(a_ref, b_ref, o_ref, acc):
        k = pl.program_id(1)

        @pl.when(k == 0)
        def _():
            acc[...] = jnp.zeros_like(acc)

        acc[...] += _dot(a_ref[...], b_ref[...].astype(BF16))

        @pl.when(k == n_k - 1)
        def _():
            o_ref[...] = jnp.transpose(acc[...]).astype(out_dtype)

    return pl.pallas_call(
        body, name=name, grid=(n // tn, n_k),
        in_specs=[pl.BlockSpec((m, tk), lambda j, k: (0, k)),
                  pl.BlockSpec((tk, tn), lambda j, k: (k, j))],
        out_specs=pl.BlockSpec((tn, m), lambda j, k: (j, 0)),
        out_shape=jax.ShapeDtypeStruct((n, m), out_dtype),
        scratch_shapes=[pltpu.VMEM((m, tn), F32)],
        compiler_params=_params(("parallel", "arbitrary"), 48),
    )(a_t, b)


def _dgrad_in(dpc, dpg, dpt, dlr, w_full_t, h, g_pre, dres, token, n_seq, lf):
    t_rows = h.shape[0]
    tm = _pick_tile(t_rows, 384, CHUNK)
    n_main = N_MAIN
    n_chunks = lf // CHUNK
    per_tile = tm // CHUNK
    n_steps = t_rows // tm

    def body(dpc_ref, dpg_ref, dpt_ref, dlr_ref, w_hbm, h_ref, g_ref, dres_ref, token_ref,
             tok_hbm, head_hbm, dg_ref, w_s, wlr_s, w_sems, dh_s, out_sems):
        i = pl.program_id(0)

        def chunk_copies(step, act):
            for kk in range(per_tile):
                q = step * per_tile + kk
                sq, c = q // n_chunks, q % n_chunks
                src = dh_s.at[pl.ds(kk * CHUNK, CHUNK)]
                tok_row = pl.multiple_of((sq * (n_chunks - 1) + jnp.maximum(c - 1, 0)) * CHUNK, CHUNK)
                head_row = pl.multiple_of(sq * CHUNK, CHUNK)
                to_tok = pltpu.make_async_copy(src, tok_hbm.at[pl.ds(tok_row, CHUNK)], out_sems.at[kk])
                to_head = pltpu.make_async_copy(src, head_hbm.at[pl.ds(head_row, CHUNK)], out_sems.at[kk])
                pl.when(c != 0)(lambda cp=to_tok: act(cp))
                pl.when(c == 0)(lambda cp=to_head: act(cp))

        first = i == 0

        @pl.when(first)
        def _():
            _start_weights(w_hbm, w_s, wlr_s, w_sems)
            dg_ref[...] = jnp.zeros_like(dg_ref)

        def landed(row_lo, row_hi):
            @pl.when(first)
            def _():
                for cp in _weight_copies(w_hbm, w_s, wlr_s, w_sems, row_lo, row_hi):
                    cp.wait()

        pl.when(first)(lambda: _low_rank_copy(w_hbm, wlr_s, w_sems).wait())
        du = _dot(dlr_ref[...].astype(BF16), wlr_s[...])
        landed(0, W_CONV)
        du += _dot(dpc_ref[...], w_s[0:W_CONV, :])
        landed(W_CONV, W_CONV + W_GLA)
        du += _dot(dpg_ref[...], w_s[W_CONV:W_CONV + W_GLA, :])
        landed(W_CONV + W_GLA, n_main)
        du += _dot(dpt_ref[...], w_s[W_CONV + W_GLA:n_main, :])
        hh = h_ref[...]
        rstd = lax.rsqrt(jnp.mean(hh * hh, axis=-1, keepdims=True) + EPS)
        xhat = hh * rstd
        dg_ref[...] += jnp.sum(du * xhat, axis=0, keepdims=True)
        dx = du * g_ref[...]
        dh = rstd * (dx - xhat * jnp.mean(dx * xhat, axis=-1, keepdims=True)) + dres_ref[...]

        @pl.when(i > 0)
        def _():
            chunk_copies(i - 1, lambda cp: cp.wait())

        dh_s[...] = dh
        chunk_copies(i, lambda cp: cp.start())

        @pl.when(i == n_steps - 1)
        def _():
            chunk_copies(i, lambda cp: cp.wait())

    row = lambda width: pl.BlockSpec((tm, width), lambda i: (i, 0))
    return pl.pallas_call(
        body, name="dgrad_in", grid=(n_steps,),
        in_specs=[row(W_CONV), row(W_GLA), row(W_TAIL), row(LANES),
                  pl.BlockSpec(memory_space=pl.ANY),
                  row(D), pl.BlockSpec((1, D), lambda i: (0, 0)), row(D),
                  pl.BlockSpec((8, LANES), lambda i: (0, 0))],
        out_specs=[pl.BlockSpec(memory_space=pl.ANY), pl.BlockSpec(memory_space=pl.ANY),
                   pl.BlockSpec((1, D), lambda i: (0, 0))],
        out_shape=[jax.ShapeDtypeStruct((t_rows - n_seq * CHUNK, D), F32),
                   jax.ShapeDtypeStruct((n_seq * CHUNK, D), F32),
                   jax.ShapeDtypeStruct((1, D), F32)],
        scratch_shapes=[pltpu.VMEM((n_main, D), BF16), pltpu.VMEM((LANES, D), BF16),
                        pltpu.SemaphoreType.DMA((N_WEIGHT_COPIES,)),
                        pltpu.VMEM((tm, D), F32), pltpu.SemaphoreType.DMA((per_tile,))],
        compiler_params=_params(("arbitrary",), 56),
    )(dpc, dpg, dpt, dlr, w_full_t, h, g_pre, dres, token)


def _reference_rows(g_conv, g_gla, g_tail, g_lr):
    conv = g_conv.reshape(N_CONV_TILES, 4, 128, D).transpose(1, 0, 2, 3).reshape(W_CONV, D)
    gla = g_gla.reshape(N_HEADS, 512, D)
    q = gla[:, 0:128].reshape(N_HEADS * HEAD_K, D)
    k = gla[:, 128:256].reshape(N_HEADS * HEAD_K, D)
    v = gla[:, 256:512].reshape(N_HEADS * HEAD_V, D)
    return jnp.concatenate([conv, q, k, v, g_tail[0:D], g_lr[0:2 * RANK], g_tail[D:3 * D]], axis=0)


def kernel(x, meta_tokens, norm_pre, w_in, conv_w, w_gate_fwd, b_gate_fwd, w_gate_bwd, b_gate_bwd, gla_norm, w_out_conv, w_out_gla, w_merge_out, norm_post, loss_target, m_meta_tokens, m_norm_pre, m_w_in, m_conv_w, m_w_gate_fwd, m_b_gate_fwd, m_w_gate_bwd, m_b_gate_bwd, m_gla_norm, m_w_out_conv, m_w_out_gla, m_w_merge_out, m_norm_post, v_meta_tokens, v_norm_pre, v_w_in, v_conv_w, v_w_gate_fwd, v_b_gate_fwd, v_w_gate_bwd, v_b_gate_bwd, v_gla_norm, v_w_out_conv, v_w_out_gla, v_w_merge_out, v_norm_post):
    n_seq, seq, _ = x.shape
    lf = CHUNK + seq
    t_rows = n_seq * lf
    shard = 2 * lax.axis_index("x") + lax.axis_index("y")
    shard_arr = jnp.reshape(shard, (1,)).astype(jnp.int32)

    w_in_slots = _cast_into_slot(jnp.transpose(w_in[0]), shard_arr, "cast_w_in")
    w_out_slots = _cast_into_slot(jnp.concatenate([w_out_conv[0], w_out_gla[0], w_merge_out[0]], axis=0), shard_arr,
                                  "cast_w_out")
    w_in_all, meta_all, conv_all, wgf_all, wgb_all = _gather_via_sibling(
        "gather_w_in", [w_in_slots, meta_tokens, conv_w[0], w_gate_fwd[0], w_gate_bwd[0]],
        (True, False, False, False, False))
    w_out_state, _ = _plane_start("gather_w_out_start", [w_out_slots], "gather", wgb_all)

    w_full_t = w_in_all.reshape(N_IN, D)
    meta_full = jnp.transpose(meta_all, (1, 0, 2)).reshape(N_META, D)
    conv_full = jnp.transpose(conv_all, (1, 0, 2)).reshape(3, D)
    wgf = jnp.pad(wgf_all, ((0, 0), (0, LANES - RANK), (0, 0))).astype(BF16)
    wgb = jnp.pad(wgb_all, ((0, 0), (RANK, LANES - 2 * RANK), (0, 0))).astype(BF16)
    bgf = b_gate_fwd.reshape(N_HEADS, 1, HEAD_K)
    bgb = b_gate_bwd.reshape(N_HEADS, 1, HEAD_K)

    head = jnp.concatenate([jnp.zeros((PAD_FRONT, D), F32), meta_full], axis=0)
    tgt = loss_target.reshape(n_seq * seq, D)

    h, proj, u_t, lr = _in_proj(x.reshape(n_seq * seq, D), head, norm_pre, w_full_t, n_seq, lf)
    yc = _conv_fwd(proj, conv_full, n_seq, lf)
    o = _gla_fwd(proj, lr, wgf, wgb, bgf, bgb, n_seq, lf)
    (w_out_all,) = _plane_wait("gather_w_out_wait", w_out_state, "gather", o)
    w3 = jnp.transpose(w_out_all.reshape(4, 3, D // 4, D), (1, 0, 2, 3)).reshape(3, D, D)
    (dres, yg, merged, dout_t, dpc_t, dpg_t, dyc, d_o, dtail, loss_acc, d_gpost, d_gamma) = _tail(
        h, tgt, yc, o, proj, w3, gla_norm, norm_post, lf)
    g_w_oc = _wgrad_t(dpc_t, yc, "wgrad_out_conv")
    g_w_og = _wgrad_t(dpg_t, yg, "wgrad_out_gla")
    g_w_mo = _wgrad_t(dout_t, merged, "wgrad_merge_out")
    g_out_slots = jnp.concatenate([g.reshape(4, D // 4, D) for g in (g_w_oc, g_w_og, g_w_mo)], axis=1)
    out_state, out_token = _plane_start("scatter_out_grads_start", [g_out_slots], "scatter", g_w_mo)
    dgla, dlr, dwgf_p, dwgb_p, dbg_p = _gla_bwd(proj, lr, d_o, wgf, wgb, bgf, bgb, n_seq, lf, out_token)
    (got_out,) = _plane_wait("scatter_out_grads_wait", out_state, "scatter", dlr)
    dconv, dconvw_p = _conv_bwd(proj, conv_full, dyc, n_seq, lf)
    g_conv = _wgrad_t(u_t, dconv, "wgrad_in_conv")
    g_gla = _wgrad_t(u_t, dgla, "wgrad_in_gla")
    g_tail = _wgrad_t(u_t, dtail, "wgrad_in_tail")
    g_lr = _wgrad_t(u_t, dlr, "wgrad_in_lr")

    g_in_slots = _reference_rows(g_conv, g_gla, g_tail, g_lr).reshape(4, SHARD_IN, D)
    in_state, in_token = _plane_start("scatter_in_grads_start", [g_in_slots], "scatter", g_lr)
    dh_tok, dh_head, d_gpre = _dgrad_in(dconv, dgla, dtail, dlr, w_full_t, h, norm_pre, dres, in_token, n_seq, lf)
    (got_in,) = _plane_wait("scatter_in_grads_wait", in_state, "scatter", d_gpre)

    plane_in = _sum_slots(got_in, "sum_w_in_grads", own=g_in_slots, slot=shard_arr)
    plane_out = _sum_slots(got_out, "sum_w_out_grads", own=g_out_slots, slot=shard_arr)
    swap_state, swap_token = _plane_start("swap_plane_sums_start", [plane_in, plane_out], "swap", plane_out)

    grad_x = dh_tok.reshape(n_seq, seq, D)

    d_meta = jnp.sum(dh_head.reshape(n_seq, CHUNK, D)[:, PAD_FRONT:, :], axis=0)
    d_convw = jnp.sum(dconvw_p, axis=0)
    d_wgf = jnp.transpose(jnp.sum(dwgf_p, axis=0)[:, 0:RANK, :], (1, 0, 2)).reshape(RANK, N_HEADS * HEAD_K)
    d_wgb = jnp.transpose(jnp.sum(dwgb_p, axis=0)[:, RANK:2 * RANK, :], (1, 0, 2)).reshape(RANK, N_HEADS * HEAD_K)
    d_bg = jnp.sum(dbg_p, axis=0)
    d_bgf = d_bg[:, 0, :].reshape(1, N_HEADS * HEAD_K)
    d_bgb = d_bg[:, 1, :].reshape(1, N_HEADS * HEAD_K)
    loss_part = loss_acc[0:1, :] + swap_token[0:1, :]
    partials = [d_meta, d_convw, d_wgf, d_wgb, d_gpre, d_bgf, d_bgb, d_gamma, d_gpost, loss_part]
    (g_meta, g_convw, g_wgf, g_wgb, g_npre, g_bgf, g_bgb, g_gnorm, g_npost, loss_row) = _sum_small(
        _gather_all("gather_small_grads", partials), "sum_small_grads")
    loss = loss_row[0, 0]
    small_out = _adamw_small(
        [(meta_tokens, g_meta, m_meta_tokens, v_meta_tokens), (norm_pre, g_npre, m_norm_pre, v_norm_pre),
         (conv_w, g_convw, m_conv_w, v_conv_w), (w_gate_fwd, g_wgf, m_w_gate_fwd, v_w_gate_fwd),
         (b_gate_fwd, g_bgf, m_b_gate_fwd, v_b_gate_fwd), (w_gate_bwd, g_wgb, m_w_gate_bwd, v_w_gate_bwd),
         (b_gate_bwd, g_bgb, m_b_gate_bwd, v_b_gate_bwd), (gla_norm, g_gnorm, m_gla_norm, v_gla_norm),
         (norm_post, g_npost, m_norm_post, v_norm_post)], shard_arr, "adamw_small")

    other_in, other_out = _plane_wait("swap_plane_sums_wait", swap_state, "swap", small_out[0][0])
    big_in = _adamw(jnp.transpose(w_in[0]), [plane_in, other_in], jnp.transpose(m_w_in[0]), jnp.transpose(v_w_in[0]),
                    "adamw_w_in")
    out_params = ((w_out_conv, m_w_out_conv, v_w_out_conv), (w_out_gla, m_w_out_gla, v_w_out_gla),
                  (w_merge_out, m_w_merge_out, v_w_merge_out))
    big_out = [_adamw(w[0], [plane_out, other_out], m[0], v[0], f"adamw_w_out_{i}", grad_row=i * (D // 4))
               for i, (w, m, v) in enumerate(out_params)]

    results = []
    for kind in range(4):
        small_kind = [p[kind] for p in small_out]
        w_in_part = jnp.transpose(big_in[kind])[None]
        outs3 = [big_out[i][kind][None] for i in range(3)]
        results.extend(small_kind[0:2] + [w_in_part] + small_kind[2:8] + outs3 + small_kind[8:9])
    return (loss, grad_x, *results)
```
